```python
import math
import jax
import jax.numpy as jnp
from jax import lax
import numpy as np

D_MODEL = 1024
BATCH = 8
SEQ = 8192
DEPTH = 2

EPS = 1e-6
N_BRANCHES = 3
D_SSM = 3 * D_MODEL // 4
SSM_GROUP = 16
SSM_GROUPS = D_SSM // SSM_GROUP
SSM_STATE = 64
DT_MIN = 0.001
DT_MAX = 0.1
ATTN_HEAD_DIM = 64
ATTN_HEADS_PER_GROUP = 4
ATTN_CONFIGS = ((128, 1), (512, 4), (2048, 16))
N_ATTN_HEADS = ATTN_HEADS_PER_GROUP * len(ATTN_CONFIGS)
D_ATTN = N_ATTN_HEADS * ATTN_HEAD_DIM
ATTN_BLOCK = 128
NUM_BUCKETS = 32
REL_MAX_DISTANCE = 2048
NEG_INF = -1e30
MEM_LEN = 256
MEM_HEADS = 4
MEM_HEAD_DIM = D_MODEL // 8
D_MEM = MEM_HEADS * MEM_HEAD_DIM
D_IN = 2 * D_SSM + 4 * D_ATTN + 2 * D_MEM + N_BRANCHES * D_MODEL

kernel_name = "hybrid_s5_dilated_attn_memxattn_gated"


def rms_norm(x, g):
    x32 = x.astype(jnp.float32)
    y = x32 * lax.rsqrt(jnp.mean(x32 * x32, axis=-1, keepdims=True) + EPS)
    return (y * g.astype(jnp.float32)).astype(x.dtype)


def rel_bucket(dist):
    n = jnp.maximum(dist, 0)
    max_exact = NUM_BUCKETS // 2
    n_f = jnp.maximum(n, 1).astype(jnp.float32)
    large = max_exact + (jnp.log(n_f / max_exact) / math.log(REL_MAX_DISTANCE / max_exact)
                         * (NUM_BUCKETS - max_exact)).astype(jnp.int32)
    large = jnp.minimum(large, NUM_BUCKETS - 1)
    return jnp.where(n < max_exact, n, large)


def s5_ssm(u, lam_re, lam_im, log_dt, b_re, b_im, c_re, c_im, d):
    B, L, _ = u.shape
    f32 = jnp.float32
    u32 = u.astype(f32).reshape(B, L, SSM_GROUPS, SSM_GROUP)
    lre, lim = lam_re.astype(f32), lam_im.astype(f32)
    dt = jnp.exp(log_dt.astype(f32))[:, None]
    mag = jnp.exp(lre * dt)
    abar_re, abar_im = mag * jnp.cos(lim * dt), mag * jnp.sin(lim * dt)
    den = lre * lre + lim * lim
    nr, ni = abar_re - 1.0, abar_im
    f_re = (nr * lre + ni * lim) / den
    f_im = (ni * lre - nr * lim) / den
    br, bi = b_re.astype(f32), b_im.astype(f32)
    bbar_re = f_re[..., None] * br - f_im[..., None] * bi
    bbar_im = f_re[..., None] * bi + f_im[..., None] * br
    bu_re = jnp.einsum('blgh,gph->blgp', u32, bbar_re)
    bu_im = jnp.einsum('blgh,gph->blgp', u32, bbar_im)
    a_re = jnp.broadcast_to(abar_re, (L, SSM_GROUPS, SSM_STATE))
    a_im = jnp.broadcast_to(abar_im, (L, SSM_GROUPS, SSM_STATE))

    def combine(e1, e2):
        a1r, a1i, b1r, b1i = e1
        a2r, a2i, b2r, b2i = e2
        return (a1r * a2r - a1i * a2i,
                a1r * a2i + a1i * a2r,
                a2r * b1r - a2i * b1i + b2r,
                a2r * b1i + a2i * b1r + b2i)

    def scan_one(br_seq, bi_seq):
        _, _, xr, xi = lax.associative_scan(combine, (a_re, a_im, br_seq, bi_seq), axis=0)
        return xr, xi

    xr, xi = jax.vmap(scan_one)(bu_re, bu_im)
    y = (jnp.einsum('blgp,ghp->blgh', xr, c_re.astype(f32))
         - jnp.einsum('blgp,ghp->blgh', xi, c_im.astype(f32))
         + d.astype(f32).reshape(SSM_GROUPS, SSM_GROUP) * u32)
    return y.reshape(B, L, D_SSM).astype(u.dtype)


def dilated_window_attention(q, k, v, bias_tab, window, dilation):
    B, L, H, hd = q.shape
    r = dilation
    span = window // dilation
    unit = r * ATTN_BLOCK
    Lp = -(-L // unit) * unit
    M = Lp // r
    nb = M // ATTN_BLOCK
    pad = ((0, 0), (0, Lp - L), (0, 0), (0, 0))

    def to_blocks(a):
        a = jnp.pad(a, pad).reshape(B, M, r, H, hd).transpose(0, 2, 1, 3, 4)
        return a.reshape(B, r, nb, ATTN_BLOCK, H, hd)

    def with_prev(a):
        prev = jnp.pad(a[:, :, :-1], ((0, 0), (0, 0), (1, 0), (0, 0), (0, 0), (0, 0)))
        return jnp.concatenate([prev, a], axis=3)

    qb = to_blocks(q)
    kw = with_prev(to_blocks(k))
    vw = with_prev(to_blocks(v))
    s = jnp.einsum('brnqhd,brnkhd->brnhqk', qb, kw).astype(jnp.float32) * (hd ** -0.5)
    qi = jnp.arange(ATTN_BLOCK)[:, None]
    kj = jnp.arange(2 * ATTN_BLOCK)[None, :]
    delta = ATTN_BLOCK + qi - kj
    band = (delta >= 0) & (delta <= span)
    has_prev = (jnp.arange(nb) > 0)[:, None, None] | (kj >= ATTN_BLOCK)[None]
    valid = band[None] & has_prev
    bias = bias_tab[rel_bucket(jnp.maximum(delta, 0) * r)]
    s = s + bias.transpose(2, 0, 1).astype(jnp.float32)
    s = jnp.where(valid[:, None], s, NEG_INF)
    m = jnp.max(s, axis=-1, keepdims=True)
    p = jnp.exp(s - m)
    l = jnp.sum(p, axis=-1, keepdims=True)
    o = jnp.einsum('brnhqk,brnkhd->brnqhd', (p / l).astype(v.dtype), vw)
    lse = (m + jnp.log(l))[..., 0]
    o = o.reshape(B, r, M, H, hd).transpose(0, 2, 1, 3, 4).reshape(B, Lp, H, hd)[:, :L]
    lse = lse.transpose(0, 1, 2, 4, 3).reshape(B, r, M, H).transpose(0, 2, 1, 3).reshape(B, Lp, H)[:, :L]
    return o, lse


def _fwd_setup_inputs(seed: int = 0) -> dict:
    key = jax.random.key(seed)
    ks = jax.random.split(key, 32)
    f32 = jnp.float32

    def nrm(k, shape, scale):
        return jax.random.normal(k, shape, f32) * scale

    G, P, Hg = SSM_GROUPS, SSM_STATE, SSM_GROUP
    return {
        "x": nrm(ks[0], (BATCH, SEQ, D_MODEL), 1.0),
        "mem": nrm(ks[1], (BATCH, MEM_LEN, D_MODEL), 1.0),
        "norm_g": 1.0 + nrm(ks[2], (DEPTH, D_MODEL), 0.1),
        "mem_norm_g": 1.0 + nrm(ks[3], (DEPTH, D_MODEL), 0.1),
        "w_in": nrm(ks[4], (DEPTH, D_MODEL, D_IN), D_MODEL ** -0.5),
        "b_gate": nrm(ks[5], (DEPTH, N_BRANCHES * D_MODEL), 0.1),
        "ssm_lambda_re": -0.5 + nrm(ks[6], (DEPTH, G, P), 0.01),
        "ssm_lambda_im": math.pi * jnp.arange(P, dtype=f32) + nrm(ks[7], (DEPTH, G, P), 0.01),
        "ssm_log_dt": jax.random.uniform(ks[8], (DEPTH, G), f32, math.log(DT_MIN), math.log(DT_MAX)),
        "ssm_b_re": nrm(ks[9], (DEPTH, G, P, Hg), (0.5 / Hg) ** 0.5),
        "ssm_b_im": nrm(ks[10], (DEPTH, G, P, Hg), (0.5 / Hg) ** 0.5),
        "ssm_c_re": nrm(ks[11], (DEPTH, G, Hg, P), (0.5 / P) ** 0.5),
        "ssm_c_im": nrm(ks[12], (DEPTH, G, Hg, P), (0.5 / P) ** 0.5),
        "ssm_d": nrm(ks[13], (DEPTH, D_SSM), 0.5),
        "w_glu": nrm(ks[14], (DEPTH, D_SSM, D_SSM), D_SSM ** -0.5),
        "b_glu": nrm(ks[15], (DEPTH, D_SSM), 0.1),
        "w_mem_kv": nrm(ks[16], (DEPTH, D_MODEL, 2 * D_MEM), D_MODEL ** -0.5),
        "w_br_ssm": nrm(ks[17], (DEPTH, D_SSM, D_MODEL), D_SSM ** -0.5),
        "w_br_attn": nrm(ks[18], (DEPTH, D_ATTN, D_MODEL), D_ATTN ** -0.5),
        "w_br_mem": nrm(ks[19], (DEPTH, D_MEM, D_MODEL), D_MEM ** -0.5),
        "w_out": nrm(ks[20], (DEPTH, D_MODEL, D_MODEL), D_MODEL ** -0.5),
        "rel_bias": nrm(ks[21], (NUM_BUCKETS, N_ATTN_HEADS), 0.5),
        "final_norm_g": 1.0 + nrm(ks[22], (D_MODEL,), 0.1),
    }


def _fwd_reference(x, mem, norm_g, mem_norm_g, w_in, b_gate, ssm_lambda_re, ssm_lambda_im, ssm_log_dt,
              ssm_b_re, ssm_b_im, ssm_c_re, ssm_c_im, ssm_d, w_glu, b_glu, w_mem_kv, w_br_ssm,
              w_br_attn, w_br_mem, w_out, rel_bias, final_norm_g):
    B, L, _ = x.shape
    sizes = (D_SSM, D_SSM, D_ATTN, D_ATTN, D_ATTN, D_ATTN, D_MEM, D_MEM, N_BRANCHES * D_MODEL)
    split_at = np.cumsum(sizes)[:-1].tolist()
    for layer in range(DEPTH):
        h = rms_norm(x, norm_g[layer])
        proj = h @ w_in[layer]
        u_ssm, z_ssm, q, k, v, z_attn, q_mem, z_mem, gate_logits = jnp.split(proj, split_at, axis=-1)

        y = s5_ssm(u_ssm, ssm_lambda_re[layer], ssm_lambda_im[layer], ssm_log_dt[layer],
                   ssm_b_re[layer], ssm_b_im[layer], ssm_c_re[layer], ssm_c_im[layer], ssm_d[layer])
        y = jax.nn.gelu(y)
        y = y * jax.nn.sigmoid(y @ w_glu[layer] + b_glu[layer])
        o_ssm = y * jax.nn.silu(z_ssm)

        qh = q.reshape(B, L, N_ATTN_HEADS, ATTN_HEAD_DIM)
        kh = k.reshape(B, L, N_ATTN_HEADS, ATTN_HEAD_DIM)
        vh = v.reshape(B, L, N_ATTN_HEADS, ATTN_HEAD_DIM)
        outs, lses = [], []
        for g, (window, dilation) in enumerate(ATTN_CONFIGS):
            sl = slice(g * ATTN_HEADS_PER_GROUP, (g + 1) * ATTN_HEADS_PER_GROUP)
            o_g, lse_g = dilated_window_attention(qh[:, :, sl], kh[:, :, sl], vh[:, :, sl],
                                                  rel_bias[:, sl], window, dilation)
            outs.append(o_g)
            lses.append(lse_g)
        alpha = jax.nn.softmax(jnp.stack(lses, axis=0), axis=0)
        o_attn = jnp.concatenate([o_g * alpha[g][..., None].astype(o_g.dtype) for g, o_g in enumerate(outs)],
                                 axis=2).reshape(B, L, D_ATTN)
        o_attn = o_attn * jax.nn.silu(z_attn)

        kv_mem = rms_norm(mem, mem_norm_g[layer]) @ w_mem_kv[layer]
        k_mem, v_mem = jnp.split(kv_mem, 2, axis=-1)
        k_mem = k_mem.reshape(B, MEM_LEN, MEM_HEADS, MEM_HEAD_DIM)
        v_mem = v_mem.reshape(B, MEM_LEN, MEM_HEADS, MEM_HEAD_DIM)
        qm = q_mem.reshape(B, L, MEM_HEADS, MEM_HEAD_DIM)
        s_mem = jnp.einsum('blhd,bmhd->bhlm', qm, k_mem).astype(jnp.float32) * (MEM_HEAD_DIM ** -0.5)
        p_mem = jax.nn.softmax(s_mem, axis=-1).astype(v_mem.dtype)
        o_mem = jnp.einsum('bhlm,bmhd->blhd', p_mem, v_mem).reshape(B, L, D_MEM)
        o_mem = o_mem * jax.nn.silu(z_mem)

        gates = jax.nn.sigmoid((gate_logits + b_gate[layer]).astype(jnp.float32))
        gates = gates.reshape(B, L, N_BRANCHES, D_MODEL).astype(x.dtype)
        merged = (gates[:, :, 0] * (o_ssm @ w_br_ssm[layer])
                  + gates[:, :, 1] * (o_attn @ w_br_attn[layer])
                  + gates[:, :, 2] * (o_mem @ w_br_mem[layer]))
        x = x + merged @ w_out[layer]
    return rms_norm(x, final_norm_g)


import jax as _jax
import jax.numpy as _jnp

TWIN_FORMAT = 'train_step'
FWD_PARAMS = ['x', 'mem', 'norm_g', 'mem_norm_g', 'w_in', 'b_gate', 'ssm_lambda_re', 'ssm_lambda_im', 'ssm_log_dt', 'ssm_b_re', 'ssm_b_im', 'ssm_c_re', 'ssm_c_im', 'ssm_d', 'w_glu', 'b_glu', 'w_mem_kv', 'w_br_ssm', 'w_br_attn', 'w_br_mem', 'w_out', 'rel_bias', 'final_norm_g']
TWIN_WEIGHTS = ['norm_g', 'mem_norm_g', 'w_in', 'b_gate', 'ssm_lambda_re', 'ssm_lambda_im', 'ssm_log_dt', 'ssm_b_re', 'ssm_b_im', 'ssm_c_re', 'ssm_c_im', 'ssm_d', 'w_glu', 'b_glu', 'w_mem_kv', 'w_br_ssm', 'w_br_attn', 'w_br_mem', 'w_out', 'rel_bias', 'final_norm_g']
TWIN_DIFF_INPUT = 'x'
TWIN_INPUTS = ['x', 'mem', 'norm_g', 'mem_norm_g', 'w_in', 'b_gate', 'ssm_lambda_re', 'ssm_lambda_im', 'ssm_log_dt', 'ssm_b_re', 'ssm_b_im', 'ssm_c_re', 'ssm_c_im', 'ssm_d', 'w_glu', 'b_glu', 'w_mem_kv', 'w_br_ssm', 'w_br_attn', 'w_br_mem', 'w_out', 'rel_bias', 'final_norm_g', 'loss_target', 'm_norm_g', 'm_mem_norm_g', 'm_w_in', 'm_b_gate', 'm_ssm_lambda_re', 'm_ssm_lambda_im', 'm_ssm_log_dt', 'm_ssm_b_re', 'm_ssm_b_im', 'm_ssm_c_re', 'm_ssm_c_im', 'm_ssm_d', 'm_w_glu', 'm_b_glu', 'm_w_mem_kv', 'm_w_br_ssm', 'm_w_br_attn', 'm_w_br_mem', 'm_w_out', 'm_rel_bias', 'm_final_norm_g', 'v_norm_g', 'v_mem_norm_g', 'v_w_in', 'v_b_gate', 'v_ssm_lambda_re', 'v_ssm_lambda_im', 'v_ssm_log_dt', 'v_ssm_b_re', 'v_ssm_b_im', 'v_ssm_c_re', 'v_ssm_c_im', 'v_ssm_d', 'v_w_glu', 'v_b_glu', 'v_w_mem_kv', 'v_w_br_ssm', 'v_w_br_attn', 'v_w_br_mem', 'v_w_out', 'v_rel_bias', 'v_final_norm_g']
TWIN_OUTPUTS = ['loss', 'grad_x', 'grad_norm_g', 'grad_mem_norm_g', 'grad_w_in', 'grad_b_gate', 'grad_ssm_lambda_re', 'grad_ssm_lambda_im', 'grad_ssm_log_dt', 'grad_ssm_b_re', 'grad_ssm_b_im', 'grad_ssm_c_re', 'grad_ssm_c_im', 'grad_ssm_d', 'grad_w_glu', 'grad_b_glu', 'grad_w_mem_kv', 'grad_w_br_ssm', 'grad_w_br_attn', 'grad_w_br_mem', 'grad_w_out', 'grad_rel_bias', 'grad_final_norm_g', 'delta_norm_g', 'delta_mem_norm_g', 'delta_w_in', 'delta_b_gate', 'delta_ssm_lambda_re', 'delta_ssm_lambda_im', 'delta_ssm_log_dt', 'delta_ssm_b_re', 'delta_ssm_b_im', 'delta_ssm_c_re', 'delta_ssm_c_im', 'delta_ssm_d', 'delta_w_glu', 'delta_b_glu', 'delta_w_mem_kv', 'delta_w_br_ssm', 'delta_w_br_attn', 'delta_w_br_mem', 'delta_w_out', 'delta_rel_bias', 'delta_final_norm_g', 'new_m_norm_g', 'new_m_mem_norm_g', 'new_m_w_in', 'new_m_b_gate', 'new_m_ssm_lambda_re', 'new_m_ssm_lambda_im', 'new_m_ssm_log_dt', 'new_m_ssm_b_re', 'new_m_ssm_b_im', 'new_m_ssm_c_re', 'new_m_ssm_c_im', 'new_m_ssm_d', 'new_m_w_glu', 'new_m_b_glu', 'new_m_w_mem_kv', 'new_m_w_br_ssm', 'new_m_w_br_attn', 'new_m_w_br_mem', 'new_m_w_out', 'new_m_rel_bias', 'new_m_final_norm_g', 'new_v_norm_g', 'new_v_mem_norm_g', 'new_v_w_in', 'new_v_b_gate', 'new_v_ssm_lambda_re', 'new_v_ssm_lambda_im', 'new_v_ssm_log_dt', 'new_v_ssm_b_re', 'new_v_ssm_b_im', 'new_v_ssm_c_re', 'new_v_ssm_c_im', 'new_v_ssm_d', 'new_v_w_glu', 'new_v_b_glu', 'new_v_w_mem_kv', 'new_v_w_br_ssm', 'new_v_w_br_attn', 'new_v_w_br_mem', 'new_v_w_out', 'new_v_rel_bias', 'new_v_final_norm_g']
TWIN_LEAF_KINDS = {'loss': 'loss', 'grad_x': 'grad_x', 'grad_norm_g': 'grad_w', 'grad_mem_norm_g': 'grad_w', 'grad_w_in': 'grad_w', 'grad_b_gate': 'grad_w', 'grad_ssm_lambda_re': 'grad_w', 'grad_ssm_lambda_im': 'grad_w', 'grad_ssm_log_dt': 'grad_w', 'grad_ssm_b_re': 'grad_w', 'grad_ssm_b_im': 'grad_w', 'grad_ssm_c_re': 'grad_w', 'grad_ssm_c_im': 'grad_w', 'grad_ssm_d': 'grad_w', 'grad_w_glu': 'grad_w', 'grad_b_glu': 'grad_w', 'grad_w_mem_kv': 'grad_w', 'grad_w_br_ssm': 'grad_w', 'grad_w_br_attn': 'grad_w', 'grad_w_br_mem': 'grad_w', 'grad_w_out': 'grad_w', 'grad_rel_bias': 'grad_w', 'grad_final_norm_g': 'grad_w', 'delta_norm_g': 'delta_w', 'delta_mem_norm_g': 'delta_w', 'delta_w_in': 'delta_w', 'delta_b_gate': 'delta_w', 'delta_ssm_lambda_re': 'delta_w', 'delta_ssm_lambda_im': 'delta_w', 'delta_ssm_log_dt': 'delta_w', 'delta_ssm_b_re': 'delta_w', 'delta_ssm_b_im': 'delta_w', 'delta_ssm_c_re': 'delta_w', 'delta_ssm_c_im': 'delta_w', 'delta_ssm_d': 'delta_w', 'delta_w_glu': 'delta_w', 'delta_b_glu': 'delta_w', 'delta_w_mem_kv': 'delta_w', 'delta_w_br_ssm': 'delta_w', 'delta_w_br_attn': 'delta_w', 'delta_w_br_mem': 'delta_w', 'delta_w_out': 'delta_w', 'delta_rel_bias': 'delta_w', 'delta_final_norm_g': 'delta_w', 'new_m_norm_g': 'new_m', 'new_m_mem_norm_g': 'new_m', 'new_m_w_in': 'new_m', 'new_m_b_gate': 'new_m', 'new_m_ssm_lambda_re': 'new_m', 'new_m_ssm_lambda_im': 'new_m', 'new_m_ssm_log_dt': 'new_m', 'new_m_ssm_b_re': 'new_m', 'new_m_ssm_b_im': 'new_m', 'new_m_ssm_c_re': 'new_m', 'new_m_ssm_c_im': 'new_m', 'new_m_ssm_d': 'new_m', 'new_m_w_glu': 'new_m', 'new_m_b_glu': 'new_m', 'new_m_w_mem_kv': 'new_m', 'new_m_w_br_ssm': 'new_m', 'new_m_w_br_attn': 'new_m', 'new_m_w_br_mem': 'new_m', 'new_m_w_out': 'new_m', 'new_m_rel_bias': 'new_m', 'new_m_final_norm_g': 'new_m', 'new_v_norm_g': 'new_v', 'new_v_mem_norm_g': 'new_v', 'new_v_w_in': 'new_v', 'new_v_b_gate': 'new_v', 'new_v_ssm_lambda_re': 'new_v', 'new_v_ssm_lambda_im': 'new_v', 'new_v_ssm_log_dt': 'new_v', 'new_v_ssm_b_re': 'new_v', 'new_v_ssm_b_im': 'new_v', 'new_v_ssm_c_re': 'new_v', 'new_v_ssm_c_im': 'new_v', 'new_v_ssm_d': 'new_v', 'new_v_w_glu': 'new_v', 'new_v_b_glu': 'new_v', 'new_v_w_mem_kv': 'new_v', 'new_v_w_br_ssm': 'new_v', 'new_v_w_br_attn': 'new_v', 'new_v_w_br_mem': 'new_v', 'new_v_w_out': 'new_v', 'new_v_rel_bias': 'new_v', 'new_v_final_norm_g': 'new_v'}


def _forward(args):
    return _fwd_reference(*[args[k] for k in FWD_PARAMS])


def _output_shape():
    def fwd():
        inp = _fwd_setup_inputs(0)
        return _fwd_reference(*[inp[k] for k in FWD_PARAMS])
    out = _jax.eval_shape(fwd)
    return out.shape, out.dtype

N_MICROBATCH = 1
ADAM_LR = 0.001
ADAM_B1 = 0.9
ADAM_B2 = 0.999
ADAM_EPS = 1e-08
ADAM_WD = 0.01
ADAM_STEP = 10
PER_EXAMPLE_BATCH_AXIS = {'x': 0, 'mem': 0, 'loss_target': 0}
SHARED_INPUTS = []
_WEIGHT_DTYPES = {'norm_g': _jnp.float32, 'mem_norm_g': _jnp.float32, 'w_in': _jnp.float32, 'b_gate': _jnp.float32, 'ssm_lambda_re': _jnp.float32, 'ssm_lambda_im': _jnp.float32, 'ssm_log_dt': _jnp.float32, 'ssm_b_re': _jnp.float32, 'ssm_b_im': _jnp.float32, 'ssm_c_re': _jnp.float32, 'ssm_c_im': _jnp.float32, 'ssm_d': _jnp.float32, 'w_glu': _jnp.float32, 'b_glu': _jnp.float32, 'w_mem_kv': _jnp.float32, 'w_br_ssm': _jnp.float32, 'w_br_attn': _jnp.float32, 'w_br_mem': _jnp.float32, 'w_out': _jnp.float32, 'rel_bias': _jnp.float32, 'final_norm_g': _jnp.float32}
MOMENT_SCALE = {'norm_g': 3.116804e-02, 'mem_norm_g': 1.404945e-02, 'w_in': 1.071316e-02, 'b_gate': 4.279159e-03, 'ssm_lambda_re': 1.799777e-03, 'ssm_lambda_im': 1.591974e-03, 'ssm_log_dt': 1.363342e+00, 'ssm_b_re': 1.067948e-03, 'ssm_b_im': 1.064160e-03, 'ssm_c_re': 2.148908e-03, 'ssm_c_im': 2.132537e-03, 'ssm_d': 3.218802e-02, 'w_glu': 2.469924e-03, 'b_glu': 7.740896e-03, 'w_mem_kv': 1.268752e-02, 'w_br_ssm': 1.419533e-02, 'w_br_attn': 9.993476e-03, 'w_br_mem': 9.091136e-03, 'w_out': 1.912052e-02, 'rel_bias': 1.652371e-02, 'final_norm_g': 6.434778e+01}


def _to_microbatches(a, axis):
    t = _jnp.moveaxis(a, axis, 0)
    t = t.reshape((N_MICROBATCH, t.shape[0] // N_MICROBATCH) + t.shape[1:])
    return _jnp.moveaxis(t, 1, axis + 1)


def setup_inputs(seed: int = 0) -> dict:
    inp = _fwd_setup_inputs(seed)
    key = _jax.random.fold_in(_jax.random.key(seed), 7919)
    shape, _ = _output_shape()
    out = dict(inp)
    out["loss_target"] = _jax.random.normal(_jax.random.fold_in(key, 0), shape, _jnp.float32)
    for i, name in enumerate(TWIN_WEIGHTS):
        w = inp[name].astype(_jnp.float32)
        if MOMENT_SCALE is None:
            s = _jnp.sqrt(_jnp.mean(_jnp.square(w)) + 1e-30)
        else:
            s = MOMENT_SCALE[name]
        km, kv = _jax.random.split(_jax.random.fold_in(key, i + 1))
        out[name] = w
        out["m_" + name] = s * _jax.random.normal(km, w.shape, _jnp.float32)
        out["v_" + name] = (s * s) * _jax.random.uniform(kv, w.shape, _jnp.float32, 0.5, 1.5)
    if N_MICROBATCH > 1:
        for name, axis in PER_EXAMPLE_BATCH_AXIS.items():
            out[name] = _to_microbatches(out[name], axis)
    return {'x': out['x'], 'mem': out['mem'], 'norm_g': out['norm_g'], 'mem_norm_g': out['mem_norm_g'], 'w_in': out['w_in'], 'b_gate': out['b_gate'], 'ssm_lambda_re': out['ssm_lambda_re'], 'ssm_lambda_im': out['ssm_lambda_im'], 'ssm_log_dt': out['ssm_log_dt'], 'ssm_b_re': out['ssm_b_re'], 'ssm_b_im': out['ssm_b_im'], 'ssm_c_re': out['ssm_c_re'], 'ssm_c_im': out['ssm_c_im'], 'ssm_d': out['ssm_d'], 'w_glu': out['w_glu'], 'b_glu': out['b_glu'], 'w_mem_kv': out['w_mem_kv'], 'w_br_ssm': out['w_br_ssm'], 'w_br_attn': out['w_br_attn'], 'w_br_mem': out['w_br_mem'], 'w_out': out['w_out'], 'rel_bias': out['rel_bias'], 'final_norm_g': out['final_norm_g'], 'loss_target': out['loss_target'], 'm_norm_g': out['m_norm_g'], 'm_mem_norm_g': out['m_mem_norm_g'], 'm_w_in': out['m_w_in'], 'm_b_gate': out['m_b_gate'], 'm_ssm_lambda_re': out['m_ssm_lambda_re'], 'm_ssm_lambda_im': out['m_ssm_lambda_im'], 'm_ssm_log_dt': out['m_ssm_log_dt'], 'm_ssm_b_re': out['m_ssm_b_re'], 'm_ssm_b_im': out['m_ssm_b_im'], 'm_ssm_c_re': out['m_ssm_c_re'], 'm_ssm_c_im': out['m_ssm_c_im'], 'm_ssm_d': out['m_ssm_d'], 'm_w_glu': out['m_w_glu'], 'm_b_glu': out['m_b_glu'], 'm_w_mem_kv': out['m_w_mem_kv'], 'm_w_br_ssm': out['m_w_br_ssm'], 'm_w_br_attn': out['m_w_br_attn'], 'm_w_br_mem': out['m_w_br_mem'], 'm_w_out': out['m_w_out'], 'm_rel_bias': out['m_rel_bias'], 'm_final_norm_g': out['m_final_norm_g'], 'v_norm_g': out['v_norm_g'], 'v_mem_norm_g': out['v_mem_norm_g'], 'v_w_in': out['v_w_in'], 'v_b_gate': out['v_b_gate'], 'v_ssm_lambda_re': out['v_ssm_lambda_re'], 'v_ssm_lambda_im': out['v_ssm_lambda_im'], 'v_ssm_log_dt': out['v_ssm_log_dt'], 'v_ssm_b_re': out['v_ssm_b_re'], 'v_ssm_b_im': out['v_ssm_b_im'], 'v_ssm_c_re': out['v_ssm_c_re'], 'v_ssm_c_im': out['v_ssm_c_im'], 'v_ssm_d': out['v_ssm_d'], 'v_w_glu': out['v_w_glu'], 'v_b_glu': out['v_b_glu'], 'v_w_mem_kv': out['v_w_mem_kv'], 'v_w_br_ssm': out['v_w_br_ssm'], 'v_w_br_attn': out['v_w_br_attn'], 'v_w_br_mem': out['v_w_br_mem'], 'v_w_out': out['v_w_out'], 'v_rel_bias': out['v_rel_bias'], 'v_final_norm_g': out['v_final_norm_g']}


def _loss(weights, diff, rest, loss_target):
    with _jax.named_scope("forward"):
        args = {**rest, TWIN_DIFF_INPUT: diff, **{k: w.astype(_WEIGHT_DTYPES[k]) for k, w in weights.items()}}
        y = _forward(args)
    with _jax.named_scope("loss_head"):
        err = _jnp.square(y.astype(_jnp.float32) - loss_target)
        return 0.5 * _jnp.sum(_jnp.mean(err, axis=-1)) if err.ndim else 0.5 * err


def _adamw(w, g, m, v):
    m = ADAM_B1 * m + (1.0 - ADAM_B1) * g
    v = ADAM_B2 * v + (1.0 - ADAM_B2) * _jnp.square(g)
    m_hat = m / (1.0 - ADAM_B1 ** ADAM_STEP)
    v_hat = v / (1.0 - ADAM_B2 ** ADAM_STEP)
    delta = -ADAM_LR * (m_hat / (_jnp.sqrt(v_hat) + ADAM_EPS) + ADAM_WD * w)
    return delta, m, v


def reference(x, mem, norm_g, mem_norm_g, w_in, b_gate, ssm_lambda_re, ssm_lambda_im, ssm_log_dt, ssm_b_re, ssm_b_im, ssm_c_re, ssm_c_im, ssm_d, w_glu, b_glu, w_mem_kv, w_br_ssm, w_br_attn, w_br_mem, w_out, rel_bias, final_norm_g, loss_target, m_norm_g, m_mem_norm_g, m_w_in, m_b_gate, m_ssm_lambda_re, m_ssm_lambda_im, m_ssm_log_dt, m_ssm_b_re, m_ssm_b_im, m_ssm_c_re, m_ssm_c_im, m_ssm_d, m_w_glu, m_b_glu, m_w_mem_kv, m_w_br_ssm, m_w_br_attn, m_w_br_mem, m_w_out, m_rel_bias, m_final_norm_g, v_norm_g, v_mem_norm_g, v_w_in, v_b_gate, v_ssm_lambda_re, v_ssm_lambda_im, v_ssm_log_dt, v_ssm_b_re, v_ssm_b_im, v_ssm_c_re, v_ssm_c_im, v_ssm_d, v_w_glu, v_b_glu, v_w_mem_kv, v_w_br_ssm, v_w_br_attn, v_w_br_mem, v_w_out, v_rel_bias, v_final_norm_g):
    given = dict(x=x, mem=mem, norm_g=norm_g, mem_norm_g=mem_norm_g, w_in=w_in, b_gate=b_gate, ssm_lambda_re=ssm_lambda_re, ssm_lambda_im=ssm_lambda_im, ssm_log_dt=ssm_log_dt, ssm_b_re=ssm_b_re, ssm_b_im=ssm_b_im, ssm_c_re=ssm_c_re, ssm_c_im=ssm_c_im, ssm_d=ssm_d, w_glu=w_glu, b_glu=b_glu, w_mem_kv=w_mem_kv, w_br_ssm=w_br_ssm, w_br_attn=w_br_attn, w_br_mem=w_br_mem, w_out=w_out, rel_bias=rel_bias, final_norm_g=final_norm_g, loss_target=loss_target, m_norm_g=m_norm_g, m_mem_norm_g=m_mem_norm_g, m_w_in=m_w_in, m_b_gate=m_b_gate, m_ssm_lambda_re=m_ssm_lambda_re, m_ssm_lambda_im=m_ssm_lambda_im, m_ssm_log_dt=m_ssm_log_dt, m_ssm_b_re=m_ssm_b_re, m_ssm_b_im=m_ssm_b_im, m_ssm_c_re=m_ssm_c_re, m_ssm_c_im=m_ssm_c_im, m_ssm_d=m_ssm_d, m_w_glu=m_w_glu, m_b_glu=m_b_glu, m_w_mem_kv=m_w_mem_kv, m_w_br_ssm=m_w_br_ssm, m_w_br_attn=m_w_br_attn, m_w_br_mem=m_w_br_mem, m_w_out=m_w_out, m_rel_bias=m_rel_bias, m_final_norm_g=m_final_norm_g, v_norm_g=v_norm_g, v_mem_norm_g=v_mem_norm_g, v_w_in=v_w_in, v_b_gate=v_b_gate, v_ssm_lambda_re=v_ssm_lambda_re, v_ssm_lambda_im=v_ssm_lambda_im, v_ssm_log_dt=v_ssm_log_dt, v_ssm_b_re=v_ssm_b_re, v_ssm_b_im=v_ssm_b_im, v_ssm_c_re=v_ssm_c_re, v_ssm_c_im=v_ssm_c_im, v_ssm_d=v_ssm_d, v_w_glu=v_w_glu, v_b_glu=v_b_glu, v_w_mem_kv=v_w_mem_kv, v_w_br_ssm=v_w_br_ssm, v_w_br_attn=v_w_br_attn, v_w_br_mem=v_w_br_mem, v_w_out=v_w_out, v_rel_bias=v_rel_bias, v_final_norm_g=v_final_norm_g)
    weights = {n: given[n] for n in TWIN_WEIGHTS}
    shared = {n: given[n] for n in SHARED_INPUTS}
    per_example = {n: given[n] for n in ['x', 'mem']}
    grad_fn = _jax.value_and_grad(_loss, argnums=(0, 1))

    def one_microbatch(ex, loss_target):
        ex = dict(ex)
        diff = ex.pop(TWIN_DIFF_INPUT)
        return grad_fn(weights, diff, {**shared, **ex}, loss_target)

    if N_MICROBATCH == 1:
        loss, (grad_w, grad_x) = one_microbatch(per_example, given["loss_target"])
    else:
        def body(carry, xs):
            loss_sum, grad_sum = carry
            l_k, (gw_k, gx_k) = one_microbatch(xs[0], xs[1])
            with _jax.named_scope("update"):
                return (loss_sum + l_k, _jax.tree.map(_jnp.add, grad_sum, gw_k)), gx_k

        init = (_jnp.zeros((), _jnp.float32), _jax.tree.map(_jnp.zeros_like, weights))
        (loss, grad_w), grad_x = _jax.lax.scan(body, init, (per_example, given["loss_target"]))
    with _jax.named_scope("update"):
        delta_w, new_m, new_v = {}, {}, {}
        for n in TWIN_WEIGHTS:
            delta_w[n], new_m[n], new_v[n] = _adamw(weights[n], grad_w[n], given["m_" + n], given["v_" + n])
    return (loss, grad_x, *[grad_w[n] for n in TWIN_WEIGHTS], *[delta_w[n] for n in TWIN_WEIGHTS],
            *[new_m[n] for n in TWIN_WEIGHTS], *[new_v[n] for n in TWIN_WEIGHTS])
```

```python
import functools
import math

import jax
import jax.numpy as jnp
import numpy as np
from jax import lax
from jax.experimental import pallas as pl
from jax.experimental.pallas import tpu as pltpu

F32 = jnp.float32
BF16 = jnp.bfloat16

D_MODEL = 1024
DEPTH = 2
EPS = 1e-6
D_SSM = 768
SSM_GROUP = 16
SSM_GROUPS = 48
SSM_STATE = 64
N_STATE = SSM_GROUPS * SSM_STATE
SSM_BLOCKS = 6
D_ATTN = 768
ATTN_HEAD_DIM = 64
ATTN_HEADS_PER_GROUP = 4
ATTN_DILATIONS = (1, 4, 16)
ATTN_SPAN = 128
ATTN_BLOCK = 128
NUM_BUCKETS = 32
REL_MAX_DISTANCE = 2048
NEG_INF = -1e30
MEM_HEADS = 4
MEM_HEAD_DIM = 128
D_MEM = 512
N_GATES = 3 * D_MODEL
D_IN = 8704
N_DEV = 8
ADAM_LR = 0.001
ADAM_B1 = 0.9
ADAM_B2 = 0.999
ADAM_EPS = 1e-08
ADAM_WD = 0.01
ADAM_STEP = 10

_SEGS = (("gates", 3072, 5632), ("u", 768, 0), ("z_ssm", 768, 768), ("q", 768, 1536), ("k", 768, 2304),
         ("v", 768, 3072), ("z_attn", 768, 3840), ("q_mem", 512, 4608), ("z_mem", 512, 5120))
_OFF = {}
_o = 0
for _n, _w, _s in _SEGS:
    _OFF[_n] = _o
    _o += _w

NN = (((1,), (0,)), ((), ()))
NT = (((1,), (1,)), ((), ()))
TN = (((0,), (0,)), ((), ()))

VMEM_LIMIT = 56 * 1024 * 1024


def _dot(a, b, dims=NN):
    return lax.dot_general(a, b, dims, preferred_element_type=F32)


def _sigmoid(x):
    return 1.0 / (1.0 + jnp.exp(-x))


def _gelu_parts(x):
    k = math.sqrt(2.0 / math.pi)
    t = jnp.tanh(k * (x + 0.044715 * (x * x * x)))
    cdf = 0.5 * (1.0 + t)
    dcdf = 0.5 * (1.0 - t * t) * k * (1.0 + 3.0 * 0.044715 * (x * x))
    return x * cdf, cdf + x * dcdf


def _params(sem, vmem=VMEM_LIMIT):
    return pltpu.CompilerParams(dimension_semantics=sem, vmem_limit_bytes=vmem)


def _full(shape):
    return pl.BlockSpec(shape, lambda *_: (0,) * len(shape))


def _norm_proj(x, g, w, tm, tn, name):
    T, D = x.shape
    N = w.shape[1]

    def body(x_ref, g_ref, w_ref, o_ref, h_ref, hs):
        @pl.when(pl.program_id(1) == 0)
        def _():
            xv = x_ref[...]
            r = lax.rsqrt(jnp.mean(xv * xv, axis=-1, keepdims=True) + EPS)
            hv = (xv * r * g_ref[...]).astype(BF16)
            hs[...] = hv
            h_ref[...] = hv

        o_ref[...] = _dot(hs[...], w_ref[...])

    return pl.pallas_call(
        body, name=name, grid=(T // tm, N // tn),
        in_specs=[pl.BlockSpec((tm, D), lambda i, j: (i, 0)), _full((1, D)),
                  pl.BlockSpec((D, tn), lambda i, j: (0, j))],
        out_specs=[pl.BlockSpec((tm, tn), lambda i, j: (i, j)), pl.BlockSpec((tm, D), lambda i, j: (i, 0))],
        out_shape=[jax.ShapeDtypeStruct((T, N), F32), jax.ShapeDtypeStruct((T, D), BF16)],
        scratch_shapes=[pltpu.VMEM((tm, D), BF16)],
        compiler_params=_params(("parallel", "arbitrary")),
    )(x, g, w)


def _mm_tn(a, b, tm, tn, tk, name):
    K, M = a.shape
    N = b.shape[1]
    nk = K // tk

    def body(a_ref, b_ref, o_ref, acc):
        k = pl.program_id(2)

        @pl.when(k == 0)
        def _():
            acc[...] = jnp.zeros_like(acc)

        acc[...] += _dot(a_ref[...].astype(BF16), b_ref[...].astype(BF16), TN)

        @pl.when(k == nk - 1)
        def _():
            o_ref[...] = acc[...]

    return pl.pallas_call(
        body, name=name, grid=(M // tm, N // tn, nk),
        in_specs=[pl.BlockSpec((tk, tm), lambda i, j, k: (k, i)), pl.BlockSpec((tk, tn), lambda i, j, k: (k, j))],
        out_specs=pl.BlockSpec((tm, tn), lambda i, j, k: (i, j)),
        out_shape=jax.ShapeDtypeStruct((M, N), F32),
        scratch_shapes=[pltpu.VMEM((tm, tn), F32)],
        compiler_params=_params(("parallel", "parallel", "arbitrary")),
    )(a, b)


def _proj_bwd(dp, w, x, g, dres, tm, tk, name):
    T, N = dp.shape
    D = w.shape[0]
    nk = N // tk

    def body(dp_ref, w_ref, x_ref, g_ref, dres_ref, dx_ref, dg_ref, acc):
        i, k = pl.program_id(0), pl.program_id(1)

        @pl.when(k == 0)
        def _():
            acc[...] = jnp.zeros_like(acc)

        @pl.when((i == 0) & (k == 0))
        def _():
            dg_ref[...] = jnp.zeros_like(dg_ref)

        acc[...] += _dot(dp_ref[...], w_ref[...], NT)

        @pl.when(k == nk - 1)
        def _():
            xv = x_ref[...]
            dh = acc[...]
            r = lax.rsqrt(jnp.mean(xv * xv, axis=-1, keepdims=True) + EPS)
            xr = xv * r
            dg_ref[...] += jnp.sum(dh * xr, axis=0, keepdims=True)
            wv = dh * g_ref[...]
            dx_ref[...] = dres_ref[...] + r * (wv - xr * jnp.mean(wv * xr, axis=-1, keepdims=True))

    return pl.pallas_call(
        body, name=name, grid=(T // tm, nk),
        in_specs=[pl.BlockSpec((tm, tk), lambda i, k: (i, k)), pl.BlockSpec((D, tk), lambda i, k: (0, k)),
                  pl.BlockSpec((tm, D), lambda i, k: (i, 0)), _full((1, D)),
                  pl.BlockSpec((tm, D), lambda i, k: (i, 0))],
        out_specs=[pl.BlockSpec((tm, D), lambda i, k: (i, 0)), _full((1, D))],
        out_shape=[jax.ShapeDtypeStruct((T, D), F32), jax.ShapeDtypeStruct((1, D), F32)],
        scratch_shapes=[pltpu.VMEM((tm, D), F32)],
        compiler_params=_params(("arbitrary", "arbitrary")),
    )(dp, w, x, g, dres)


def _ssm_fwd(proj, bre, bim, cre, cimn, are, aim, d, wglu, bglu, tc, name):
    T = proj.shape[0]
    ucol, zcol = _OFF["u"] // D_SSM, _OFF["z_ssm"] // D_SSM

    def body(u_ref, z_ref, bre_ref, bim_ref, cre_ref, cim_ref, are_ref, aim_ref, d_ref, wg_ref, bg_ref,
             xr_ref, xi_ref, y_ref, o_ref, car_r, car_i):
        @pl.when(pl.program_id(0) == 0)
        def _():
            car_r[...] = jnp.zeros_like(car_r)
            car_i[...] = jnp.zeros_like(car_i)

        u = u_ref[...]
        ub = u.astype(BF16)
        for k in range(SSM_BLOCKS):
            uk = ub[:, 128 * k:128 * (k + 1)]
            xr_ref[:, 512 * k:512 * (k + 1)] = _dot(uk, bre_ref[k])
            xi_ref[:, 512 * k:512 * (k + 1)] = _dot(uk, bim_ref[k])
        ar, ai = are_ref[...], aim_ref[...]

        def step(t, c):
            pr, pi = c
            nr = ar * pr - ai * pi + xr_ref[pl.ds(t, 1), :]
            ni = ar * pi + ai * pr + xi_ref[pl.ds(t, 1), :]
            xr_ref[pl.ds(t, 1), :] = nr
            xi_ref[pl.ds(t, 1), :] = ni
            return nr, ni

        pr, pi = lax.fori_loop(0, tc, step, (car_r[...], car_i[...]))
        car_r[...] = pr
        car_i[...] = pi

        ys = []
        for k in range(SSM_BLOCKS):
            xrk = xr_ref[:, 512 * k:512 * (k + 1)].astype(BF16)
            xik = xi_ref[:, 512 * k:512 * (k + 1)].astype(BF16)
            ys.append(_dot(xrk, cre_ref[k]) + _dot(xik, cim_ref[k]))
        y = jnp.concatenate(ys, axis=1) + d_ref[...] * u
        y_ref[...] = y
        gl, _ = _gelu_parts(y)
        t = _dot(gl.astype(BF16), wg_ref[...]) + bg_ref[...]
        z = z_ref[...]
        o_ref[...] = (gl * _sigmoid(t) * (z * _sigmoid(z))).astype(BF16)

    return pl.pallas_call(
        body, name=name, grid=(T // tc,),
        in_specs=[pl.BlockSpec((tc, D_SSM), lambda i: (i, ucol)), pl.BlockSpec((tc, D_SSM), lambda i: (i, zcol)),
                  _full((SSM_BLOCKS, 128, 512)), _full((SSM_BLOCKS, 128, 512)),
                  _full((SSM_BLOCKS, 512, 128)), _full((SSM_BLOCKS, 512, 128)),
                  _full((1, N_STATE)), _full((1, N_STATE)), _full((1, D_SSM)),
                  _full((D_SSM, D_SSM)), _full((1, D_SSM))],
        out_specs=[pl.BlockSpec((tc, N_STATE), lambda i: (i, 0)), pl.BlockSpec((tc, N_STATE), lambda i: (i, 0)),
                   pl.BlockSpec((tc, D_SSM), lambda i: (i, 0)), pl.BlockSpec((tc, D_SSM), lambda i: (i, 0))],
        out_shape=[jax.ShapeDtypeStruct((T, N_STATE), F32), jax.ShapeDtypeStruct((T, N_STATE), F32),
                   jax.ShapeDtypeStruct((T, D_SSM), F32), jax.ShapeDtypeStruct((T, D_SSM), BF16)],
        scratch_shapes=[pltpu.VMEM((1, N_STATE), F32), pltpu.VMEM((1, N_STATE), F32)],
        compiler_params=_params(("arbitrary",)),
    )(proj, proj, bre, bim, cre, cimn, are, aim, d, wglu, bglu)


def _glu_bwd(do, y, proj, wglu, bglu, tm, name):
    T = y.shape[0]
    zcol = _OFF["z_ssm"] // D_SSM

    def body(do_ref, y_ref, z_ref, wg_ref, bg_ref, dy_ref, dz_ref, g_ref, dt_ref, db_ref):
        @pl.when(pl.program_id(0) == 0)
        def _():
            db_ref[...] = jnp.zeros_like(db_ref)

        dov = do_ref[...]
        gl, dgl = _gelu_parts(y_ref[...])
        glb = gl.astype(BF16)
        sg = _sigmoid(_dot(glb, wg_ref[...]) + bg_ref[...])
        z = z_ref[...]
        sz = _sigmoid(z)
        dz_ref[...] = (dov * (gl * sg) * (sz * (1.0 + z * (1.0 - sz)))).astype(BF16)
        dy2 = dov * (z * sz)
        dt = dy2 * gl * (sg * (1.0 - sg))
        dtb = dt.astype(BF16)
        dg = dy2 * sg + _dot(dtb, wg_ref[...], NT)
        dy_ref[...] = dg * dgl
        g_ref[...] = glb
        dt_ref[...] = dtb
        db_ref[...] += jnp.sum(dt, axis=0, keepdims=True)

    row = lambda i: (i, 0)
    return pl.pallas_call(
        body, name=name, grid=(T // tm,),
        in_specs=[pl.BlockSpec((tm, D_SSM), row), pl.BlockSpec((tm, D_SSM), row),
                  pl.BlockSpec((tm, D_SSM), lambda i: (i, zcol)), _full((D_SSM, D_SSM)), _full((1, D_SSM))],
        out_specs=[pl.BlockSpec((tm, D_SSM), row)] * 4 + [_full((1, D_SSM))],
        out_shape=[jax.ShapeDtypeStruct((T, D_SSM), F32), jax.ShapeDtypeStruct((T, D_SSM), BF16),
                   jax.ShapeDtypeStruct((T, D_SSM), BF16), jax.ShapeDtypeStruct((T, D_SSM), BF16),
                   jax.ShapeDtypeStruct((1, D_SSM), F32)],
        compiler_params=_params(("arbitrary",)),
    )(do, y, proj, wglu, bglu)


def _ssm_bwd(dy, proj, xr, xi, ctre, ctimn, btre, btim, are, aim, d, tc, name):
    T = dy.shape[0]
    nc = T // tc
    ucol = _OFF["u"] // D_SSM
    rb = tc // 8

    def body(dy_ref, u_ref, xr_ref, xi_ref, xpr_ref, xpi_ref, ctre_ref, ctim_ref, btre_ref, btim_ref,
             are_ref, aim_ref, d_ref,
             du_ref, dbre_ref, dbim_ref, dcre_ref, dcim_ref, dare_ref, daim_ref, dd_ref, gr, gi, car_r, car_i):
        i = pl.program_id(0)

        @pl.when(i == 0)
        def _():
            for ref in (car_r, car_i, dbre_ref, dbim_ref, dcre_ref, dcim_ref, dare_ref, daim_ref, dd_ref):
                ref[...] = jnp.zeros_like(ref)

        dyv = dy_ref[...]
        dyb = dyv.astype(BF16)
        u = u_ref[...]
        ub = u.astype(BF16)
        for k in range(SSM_BLOCKS):
            dk = dyb[:, 128 * k:128 * (k + 1)]
            gr[:, 512 * k:512 * (k + 1)] = _dot(dk, ctre_ref[k])
            gi[:, 512 * k:512 * (k + 1)] = _dot(dk, ctim_ref[k])
        ar, ai = are_ref[...], aim_ref[...]

        def step(s, c):
            pr, pi = c
            t = tc - 1 - s
            nr = gr[pl.ds(t, 1), :] + ar * pr + ai * pi
            ni = gi[pl.ds(t, 1), :] + ar * pi - ai * pr
            gr[pl.ds(t, 1), :] = nr
            gi[pl.ds(t, 1), :] = ni
            return nr, ni

        pr, pi = lax.fori_loop(0, tc, step, (car_r[...], car_i[...]))
        car_r[...] = pr
        car_i[...] = pi

        xrv, xiv = xr_ref[...], xi_ref[...]
        keep = jnp.where(i == nc - 1, 0.0, 1.0)
        row0 = lax.broadcasted_iota(jnp.int32, (tc, 1), 0) == 0
        xsr = jnp.where(row0, xpr_ref[7:8, :] * keep, pltpu.roll(xrv, 1, axis=0))
        xsi = jnp.where(row0, xpi_ref[7:8, :] * keep, pltpu.roll(xiv, 1, axis=0))
        grv, giv = gr[...], gi[...]
        dare_ref[...] += jnp.sum(grv * xsr + giv * xsi, axis=0, keepdims=True)
        daim_ref[...] += jnp.sum(giv * xsr - grv * xsi, axis=0, keepdims=True)
        dd_ref[...] += jnp.sum(dyv * u, axis=0, keepdims=True)

        dus = []
        for k in range(SSM_BLOCKS):
            sl = slice(512 * k, 512 * (k + 1))
            ch = slice(128 * k, 128 * (k + 1))
            grb, gib = grv[:, sl].astype(BF16), giv[:, sl].astype(BF16)
            dus.append(_dot(grb, btre_ref[k]) + _dot(gib, btim_ref[k]))
            dbre_ref[k] += _dot(grb, ub[:, ch], TN)
            dbim_ref[k] += _dot(gib, ub[:, ch], TN)
            dcre_ref[k] += _dot(dyb[:, ch], xrv[:, sl].astype(BF16), TN)
            dcim_ref[k] -= _dot(dyb[:, ch], xiv[:, sl].astype(BF16), TN)
        du_ref[...] = (jnp.concatenate(dus, axis=1) + d_ref[...] * dyv).astype(BF16)

    rev = lambda i: (nc - 1 - i, 0)
    prev = lambda i: (jnp.maximum((nc - 1 - i) * rb - 1, 0), 0)
    return pl.pallas_call(
        body, name=name, grid=(nc,),
        in_specs=[pl.BlockSpec((tc, D_SSM), rev), pl.BlockSpec((tc, D_SSM), lambda i: (nc - 1 - i, ucol)),
                  pl.BlockSpec((tc, N_STATE), rev), pl.BlockSpec((tc, N_STATE), rev),
                  pl.BlockSpec((8, N_STATE), prev), pl.BlockSpec((8, N_STATE), prev),
                  _full((SSM_BLOCKS, 128, 512)), _full((SSM_BLOCKS, 128, 512)),
                  _full((SSM_BLOCKS, 512, 128)), _full((SSM_BLOCKS, 512, 128)),
                  _full((1, N_STATE)), _full((1, N_STATE)), _full((1, D_SSM))],
        out_specs=[pl.BlockSpec((tc, D_SSM), rev),
                   _full((SSM_BLOCKS, 512, 128)), _full((SSM_BLOCKS, 512, 128)),
                   _full((SSM_BLOCKS, 128, 512)), _full((SSM_BLOCKS, 128, 512)),
                   _full((1, N_STATE)), _full((1, N_STATE)), _full((1, D_SSM))],
        out_shape=[jax.ShapeDtypeStruct((T, D_SSM), BF16),
                   jax.ShapeDtypeStruct((SSM_BLOCKS, 512, 128), F32), jax.ShapeDtypeStruct((SSM_BLOCKS, 512, 128), F32),
                   jax.ShapeDtypeStruct((SSM_BLOCKS, 128, 512), F32), jax.ShapeDtypeStruct((SSM_BLOCKS, 128, 512), F32),
                   jax.ShapeDtypeStruct((1, N_STATE), F32), jax.ShapeDtypeStruct((1, N_STATE), F32),
                   jax.ShapeDtypeStruct((1, D_SSM), F32)],
        scratch_shapes=[pltpu.VMEM((tc, N_STATE), F32), pltpu.VMEM((tc, N_STATE), F32),
                        pltpu.VMEM((1, N_STATE), F32), pltpu.VMEM((1, N_STATE), F32)],
        compiler_params=_params(("arbitrary",)),
    )(dy, proj, xr, xi, xr, xi, ctre, ctimn, btre, btim, are, aim, d)


def _first_block(g, b, nb):
    return (b & ((nb >> (2 * g)) - 1)) == 0


def _attn_scores(q_ref, kp_ref, kc_ref, bias_ref, h, no_prev):
    kcat = jnp.concatenate([kp_ref[h], kc_ref[h]], axis=0)
    s = _dot(q_ref[h], kcat, NT) * (ATTN_HEAD_DIM ** -0.5) + bias_ref[h]
    kj = lax.broadcasted_iota(jnp.int32, s.shape, 1)
    return jnp.where(no_prev & (kj < ATTN_BLOCK), NEG_INF, s), kcat


def _attn_fwd(q, k, v, bias, name):
    T = q.shape[2]
    nb = T // ATTN_BLOCK

    def body(q_ref, kc_ref, kp_ref, vc_ref, vp_ref, bias_ref, o_ref, lse_ref):
        no_prev = _first_block(pl.program_id(0), pl.program_id(1), nb)
        for h in range(ATTN_HEADS_PER_GROUP):
            s, _ = _attn_scores(q_ref, kp_ref, kc_ref, bias_ref, h, no_prev)
            m = jnp.max(s, axis=-1, keepdims=True)
            p = jnp.exp(s - m)
            l = jnp.sum(p, axis=-1, keepdims=True)
            vcat = jnp.concatenate([vp_ref[h], vc_ref[h]], axis=0)
            o_ref[h] = _dot((p / l).astype(BF16), vcat)
            lse_ref[h] = jnp.broadcast_to(m + jnp.log(l), (ATTN_BLOCK, ATTN_HEAD_DIM))

    blk = (None, ATTN_HEADS_PER_GROUP, ATTN_BLOCK, ATTN_HEAD_DIM)
    cur = pl.BlockSpec(blk, lambda g, b: (g, 0, b, 0))
    prev = pl.BlockSpec(blk, lambda g, b: (g, 0, jnp.maximum(b - 1, 0), 0))
    return pl.pallas_call(
        body, name=name, grid=(3, nb),
        in_specs=[cur, cur, prev, cur, prev,
                  pl.BlockSpec((None, ATTN_HEADS_PER_GROUP, ATTN_BLOCK, 2 * ATTN_BLOCK), lambda g, b: (g, 0, 0, 0))],
        out_specs=[cur, cur],
        out_shape=[jax.ShapeDtypeStruct(q.shape, F32), jax.ShapeDtypeStruct(q.shape, F32)],
        compiler_params=_params(("parallel", "arbitrary")),
    )(q, k, k, v, v, bias)


def _attn_bwd(q, k, v, do, lse, corr, bias, name):
    T = q.shape[2]
    nb = T // ATTN_BLOCK
    H = ATTN_HEADS_PER_GROUP

    def body(q_ref, kc_ref, kp_ref, vc_ref, vp_ref, do_ref, lse_ref, corr_ref, bias_ref,
             dq_ref, dk_ref, dv_ref, db_ref, kacc, vacc):
        g, b = pl.program_id(0), pl.program_id(1)

        @pl.when(b == 0)
        def _():
            db_ref[...] = jnp.zeros_like(db_ref)
            kacc[...] = jnp.zeros_like(kacc)
            vacc[...] = jnp.zeros_like(vacc)

        @pl.when(b == nb)
        def _():
            dk_ref[...] = kacc[...]
            dv_ref[...] = vacc[...]

        @pl.when(b < nb)
        def _():
            no_prev = _first_block(g, b, nb)
            for h in range(H):
                s, kcat = _attn_scores(q_ref, kp_ref, kc_ref, bias_ref, h, no_prev)
                p = jnp.exp(s - lse_ref[h][:, 0:1])
                vcat = jnp.concatenate([vp_ref[h], vc_ref[h]], axis=0)
                doh = do_ref[h]
                dp = _dot(doh, vcat, NT)
                ds = p * (dp - corr_ref[h][:, 0:1])
                db_ref[h] += ds
                dsb = ds.astype(BF16)
                scale = ATTN_HEAD_DIM ** -0.5
                dq_ref[h] = _dot(dsb, kcat) * scale
                dkc = _dot(dsb, q_ref[h], TN) * scale
                dvc = _dot(p.astype(BF16), doh, TN)
                dk_ref[h] = kacc[h] + dkc[:ATTN_BLOCK]
                dv_ref[h] = vacc[h] + dvc[:ATTN_BLOCK]
                kacc[h] = dkc[ATTN_BLOCK:]
                vacc[h] = dvc[ATTN_BLOCK:]

    blk = (None, H, ATTN_BLOCK, ATTN_HEAD_DIM)
    last = nb - 1
    cur = pl.BlockSpec(blk, lambda g, b: (g, 0, jnp.minimum(b, last), 0))
    prev = pl.BlockSpec(blk, lambda g, b: (g, 0, jnp.clip(b - 1, 0, last), 0))
    bspec = pl.BlockSpec((None, H, ATTN_BLOCK, 2 * ATTN_BLOCK), lambda g, b: (g, 0, 0, 0))
    return pl.pallas_call(
        body, name=name, grid=(3, nb + 1),
        in_specs=[cur, cur, prev, cur, prev, cur, cur, cur, bspec],
        out_specs=[cur, prev, prev, bspec],
        out_shape=[jax.ShapeDtypeStruct(q.shape, F32)] * 3 + [jax.ShapeDtypeStruct(bias.shape, F32)],
        scratch_shapes=[pltpu.VMEM((H, ATTN_BLOCK, ATTN_HEAD_DIM), F32), pltpu.VMEM((H, ATTN_BLOCK, ATTN_HEAD_DIM), F32)],
        compiler_params=_params(("arbitrary", "arbitrary")),
    )(q, k, k, v, v, do, lse, corr, bias)


def _mix_weights(lse):
    l0, l1, l2 = lse[:, 0:256], lse[:, 256:512], lse[:, 512:768]
    m = jnp.maximum(jnp.maximum(l0, l1), l2)
    e0, e1, e2 = jnp.exp(l0 - m), jnp.exp(l1 - m), jnp.exp(l2 - m)
    inv = 1.0 / (e0 + e1 + e2)
    return jnp.concatenate([e0 * inv, e1 * inv, e2 * inv], axis=1)


def _attn_mix(o, lse, proj, tm, name):
    T = o.shape[0]
    zcol = _OFF["z_attn"] // D_ATTN

    def body(o_ref, lse_ref, z_ref, out_ref):
        z = z_ref[...]
        out_ref[...] = (o_ref[...] * _mix_weights(lse_ref[...]) * (z * _sigmoid(z))).astype(BF16)

    row = lambda i: (i, 0)
    return pl.pallas_call(
        body, name=name, grid=(T // tm,),
        in_specs=[pl.BlockSpec((tm, D_ATTN), row), pl.BlockSpec((tm, D_ATTN), row),
                  pl.BlockSpec((tm, D_ATTN), lambda i: (i, zcol))],
        out_specs=pl.BlockSpec((tm, D_ATTN), row),
        out_shape=jax.ShapeDtypeStruct((T, D_ATTN), BF16),
        compiler_params=_params(("parallel",)),
    )(o, lse, proj)


def _attn_mix_bwd(d, o, lse, proj, tm, name):
    T = o.shape[0]
    zcol = _OFF["z_attn"] // D_ATTN

    def body(d_ref, o_ref, lse_ref, z_ref, do_ref, corr_ref, dz_ref):
        dv, ov, z = d_ref[...], o_ref[...], z_ref[...]
        alpha = _mix_weights(lse_ref[...])
        sz = _sigmoid(z)
        oc = ov * alpha
        dz_ref[...] = (dv * oc * (sz * (1.0 + z * (1.0 - sz)))).astype(BF16)
        doc = dv * (z * sz)
        do_ref[...] = (doc * alpha).astype(BF16)
        pr = doc * oc
        p3 = pr[:, 0:256] + pr[:, 256:512] + pr[:, 512:768]
        li = lax.broadcasted_iota(jnp.int32, (256, 256), 0) // ATTN_HEAD_DIM
        lj = lax.broadcasted_iota(jnp.int32, (256, 256), 1) // ATTN_HEAD_DIM
        ones = jnp.where(li == lj, 1.0, 0.0).astype(F32)
        s = lax.dot_general(p3, ones, NN, precision=lax.Precision.HIGHEST, preferred_element_type=F32)
        corr_ref[...] = alpha * jnp.concatenate([s, s, s], axis=1)

    row = lambda i: (i, 0)
    return pl.pallas_call(
        body, name=name, grid=(T // tm,),
        in_specs=[pl.BlockSpec((tm, D_ATTN), row), pl.BlockSpec((tm, D_ATTN), row), pl.BlockSpec((tm, D_ATTN), row),
                  pl.BlockSpec((tm, D_ATTN), lambda i: (i, zcol))],
        out_specs=[pl.BlockSpec((tm, D_ATTN), row)] * 3,
        out_shape=[jax.ShapeDtypeStruct((T, D_ATTN), BF16), jax.ShapeDtypeStruct((T, D_ATTN), F32),
                   jax.ShapeDtypeStruct((T, D_ATTN), BF16)],
        compiler_params=_params(("parallel",)),
    )(d, o, lse, proj)


def _mem_probs(q_ref, kv_ref, h):
    hs = slice(MEM_HEAD_DIM * h, MEM_HEAD_DIM * (h + 1))
    qh = q_ref[:, hs].astype(BF16)
    kh = kv_ref[:, hs]
    vh = kv_ref[:, D_MEM + MEM_HEAD_DIM * h:D_MEM + MEM_HEAD_DIM * (h + 1)]
    s = _dot(qh, kh, NT) * (MEM_HEAD_DIM ** -0.5)
    p = jnp.exp(s - jnp.max(s, axis=-1, keepdims=True))
    pn = p / jnp.sum(p, axis=-1, keepdims=True)
    return qh, kh, vh, pn


def _mem_fwd(proj, kv, tm, name):
    T = proj.shape[0]
    M = kv.shape[0]
    qcol, zcol = _OFF["q_mem"] // D_MEM, _OFF["z_mem"] // D_MEM

    def body(q_ref, z_ref, kv_ref, o_ref):
        outs = []
        for h in range(MEM_HEADS):
            _, _, vh, pn = _mem_probs(q_ref, kv_ref, h)
            outs.append(_dot(pn.astype(BF16), vh))
        z = z_ref[...]
        o_ref[...] = (jnp.concatenate(outs, axis=1) * (z * _sigmoid(z))).astype(BF16)

    return pl.pallas_call(
        body, name=name, grid=(T // tm,),
        in_specs=[pl.BlockSpec((tm, D_MEM), lambda i: (i, qcol)), pl.BlockSpec((tm, D_MEM), lambda i: (i, zcol)),
                  _full((M, 2 * D_MEM))],
        out_specs=pl.BlockSpec((tm, D_MEM), lambda i: (i, 0)),
        out_shape=jax.ShapeDtypeStruct((T, D_MEM), BF16),
        compiler_params=_params(("parallel",)),
    )(proj, proj, kv)


def _mem_bwd(d, proj, kv, tm, name):
    T = proj.shape[0]
    M = kv.shape[0]
    qcol, zcol = _OFF["q_mem"] // D_MEM, _OFF["z_mem"] // D_MEM

    def body(d_ref, q_ref, z_ref, kv_ref, dq_ref, dz_ref, dkv_ref):
        @pl.when(pl.program_id(0) == 0)
        def _():
            dkv_ref[...] = jnp.zeros_like(dkv_ref)

        z = z_ref[...]
        sz = _sigmoid(z)
        dv = d_ref[...]
        dov = dv * (z * sz)
        scale = MEM_HEAD_DIM ** -0.5
        outs, dqs = [], []
        for h in range(MEM_HEADS):
            hs = slice(MEM_HEAD_DIM * h, MEM_HEAD_DIM * (h + 1))
            qh, kh, vh, pn = _mem_probs(q_ref, kv_ref, h)
            pnb = pn.astype(BF16)
            oh = _dot(pnb, vh)
            outs.append(oh)
            doh = dov[:, hs]
            dohb = doh.astype(BF16)
            dp = _dot(dohb, vh, NT)
            ds = pn * (dp - jnp.sum(doh * oh, axis=-1, keepdims=True))
            dsb = ds.astype(BF16)
            dqs.append(_dot(dsb, kh) * scale)
            dkv_ref[:, hs] += _dot(dsb, qh, TN) * scale
            vs = slice(D_MEM + MEM_HEAD_DIM * h, D_MEM + MEM_HEAD_DIM * (h + 1))
            dkv_ref[:, vs] += _dot(pnb, dohb, TN)
        dq_ref[...] = jnp.concatenate(dqs, axis=1).astype(BF16)
        dz_ref[...] = (dv * jnp.concatenate(outs, axis=1) * (sz * (1.0 + z * (1.0 - sz)))).astype(BF16)

    row = lambda i: (i, 0)
    return pl.pallas_call(
        body, name=name, grid=(T // tm,),
        in_specs=[pl.BlockSpec((tm, D_MEM), row), pl.BlockSpec((tm, D_MEM), lambda i: (i, qcol)),
                  pl.BlockSpec((tm, D_MEM), lambda i: (i, zcol)), _full((M, 2 * D_MEM))],
        out_specs=[pl.BlockSpec((tm, D_MEM), row), pl.BlockSpec((tm, D_MEM), row), _full((M, 2 * D_MEM))],
        out_shape=[jax.ShapeDtypeStruct((T, D_MEM), BF16), jax.ShapeDtypeStruct((T, D_MEM), BF16),
                   jax.ShapeDtypeStruct((M, 2 * D_MEM), F32)],
        compiler_params=_params(("arbitrary",)),
    )(d, proj, proj, kv)


def _branches_and_gates(os_ref, oa_ref, om_ref, gl_refs, bg_ref, ws_ref, wa_ref, wm_ref):
    outs = (_dot(os_ref[...], ws_ref[...]), _dot(oa_ref[...], wa_ref[...]), _dot(om_ref[...], wm_ref[...]))
    gates = tuple(_sigmoid(gl_refs[k][...] + bg_ref[:, D_MODEL * k:D_MODEL * (k + 1)]) for k in range(3))
    return outs, gates


def _merge_specs(tm):
    row = lambda i: (i, 0)
    gate = [pl.BlockSpec((tm, D_MODEL), (lambda i, k=k: (i, k))) for k in range(3)]
    return ([pl.BlockSpec((tm, D_SSM), row), pl.BlockSpec((tm, D_ATTN), row), pl.BlockSpec((tm, D_MEM), row)] + gate
            + [_full((1, N_GATES)), _full((D_SSM, D_MODEL)), _full((D_ATTN, D_MODEL)), _full((D_MEM, D_MODEL)),
               _full((D_MODEL, D_MODEL))])


def _merge_fwd(x, o_ssm, o_attn, o_mem, proj, bg, ws, wa, wm, wo, tm, name):
    T = x.shape[0]

    def body(os_ref, oa_ref, om_ref, g0, g1, g2, bg_ref, ws_ref, wa_ref, wm_ref, wo_ref, x_ref, xo_ref, mg_ref):
        outs, gates = _branches_and_gates(os_ref, oa_ref, om_ref, (g0, g1, g2), bg_ref, ws_ref, wa_ref, wm_ref)
        merged = (gates[0] * outs[0] + gates[1] * outs[1] + gates[2] * outs[2]).astype(BF16)
        mg_ref[...] = merged
        xo_ref[...] = x_ref[...] + _dot(merged, wo_ref[...])

    row = lambda i: (i, 0)
    return pl.pallas_call(
        body, name=name, grid=(T // tm,),
        in_specs=_merge_specs(tm) + [pl.BlockSpec((tm, D_MODEL), row)],
        out_specs=[pl.BlockSpec((tm, D_MODEL), row), pl.BlockSpec((tm, D_MODEL), row)],
        out_shape=[jax.ShapeDtypeStruct((T, D_MODEL), F32), jax.ShapeDtypeStruct((T, D_MODEL), BF16)],
        compiler_params=_params(("parallel",)),
    )(o_ssm, o_attn, o_mem, proj, proj, proj, bg, ws, wa, wm, wo, x)


def _merge_bwd(dx, o_ssm, o_attn, o_mem, proj, bg, ws, wa, wm, wo, tm, name):
    T = dx.shape[0]

    def body(os_ref, oa_ref, om_ref, g0, g1, g2, bg_ref, ws_ref, wa_ref, wm_ref, wo_ref, dx_ref,
             dgl_ref, db_ref, dos_ref, doa_ref, dom_ref, dbg_ref):
        @pl.when(pl.program_id(0) == 0)
        def _():
            dbg_ref[...] = jnp.zeros_like(dbg_ref)

        outs, gates = _branches_and_gates(os_ref, oa_ref, om_ref, (g0, g1, g2), bg_ref, ws_ref, wa_ref, wm_ref)
        dm = _dot(dx_ref[...].astype(BF16), wo_ref[...], NT)
        w_refs = (ws_ref, wa_ref, wm_ref)
        do_refs = (dos_ref, doa_ref, dom_ref)
        for k in range(3):
            cols = slice(D_MODEL * k, D_MODEL * (k + 1))
            dgl = dm * outs[k] * (gates[k] * (1.0 - gates[k]))
            dgl_ref[:, cols] = dgl.astype(BF16)
            dbg_ref[:, cols] += jnp.sum(dgl, axis=0, keepdims=True)
            dbk = (dm * gates[k]).astype(BF16)
            db_ref[:, cols] = dbk
            do_refs[k][...] = _dot(dbk, w_refs[k][...], NT)

    row = lambda i: (i, 0)
    return pl.pallas_call(
        body, name=name, grid=(T // tm,),
        in_specs=_merge_specs(tm) + [pl.BlockSpec((tm, D_MODEL), row)],
        out_specs=[pl.BlockSpec((tm, N_GATES), row), pl.BlockSpec((tm, N_GATES), row), pl.BlockSpec((tm, D_SSM), row),
                   pl.BlockSpec((tm, D_ATTN), row), pl.BlockSpec((tm, D_MEM), row), _full((1, N_GATES))],
        out_shape=[jax.ShapeDtypeStruct((T, N_GATES), BF16), jax.ShapeDtypeStruct((T, N_GATES), BF16),
                   jax.ShapeDtypeStruct((T, D_SSM), F32), jax.ShapeDtypeStruct((T, D_ATTN), F32),
                   jax.ShapeDtypeStruct((T, D_MEM), F32), jax.ShapeDtypeStruct((1, N_GATES), F32)],
        compiler_params=_params(("arbitrary",)),
    )(o_ssm, o_attn, o_mem, proj, proj, proj, bg, ws, wa, wm, wo, dx)


def _loss_head(x, g, target, tm, name):
    T, D = x.shape

    def body(x_ref, g_ref, t_ref, loss_ref, dx_ref, dg_ref):
        @pl.when(pl.program_id(0) == 0)
        def _():
            loss_ref[...] = jnp.zeros_like(loss_ref)
            dg_ref[...] = jnp.zeros_like(dg_ref)

        xv = x_ref[...]
        r = lax.rsqrt(jnp.mean(xv * xv, axis=-1, keepdims=True) + EPS)
        xr = xv * r
        err = xr * g_ref[...] - t_ref[...]
        loss_ref[...] += 0.5 * jnp.sum(jnp.mean(err * err, axis=-1, keepdims=True), axis=0, keepdims=True)
        dy = err * (1.0 / D)
        dg_ref[...] += jnp.sum(dy * xr, axis=0, keepdims=True)
        wv = dy * g_ref[...]
        dx_ref[...] = r * (wv - xr * jnp.mean(wv * xr, axis=-1, keepdims=True))

    row = lambda i: (i, 0)
    return pl.pallas_call(
        body, name=name, grid=(T // tm,),
        in_specs=[pl.BlockSpec((tm, D), row), _full((1, D)), pl.BlockSpec((tm, D), row)],
        out_specs=[_full((1, 128)), pl.BlockSpec((tm, D), row), _full((1, D))],
        out_shape=[jax.ShapeDtypeStruct((1, 128), F32), jax.ShapeDtypeStruct((T, D), F32),
                   jax.ShapeDtypeStruct((1, D), F32)],
        compiler_params=_params(("arbitrary",)),
    )(x, g, target)


def _adamw(parts, w, m, v, tr, name):
    R = w.shape[0]

    def body(p_ref, w_ref, m_ref, v_ref, g_ref, d_ref, mo_ref, vo_ref):
        g = p_ref[0]
        for s in range(1, N_DEV):
            g = g + p_ref[s]
        mn = ADAM_B1 * m_ref[...] + (1.0 - ADAM_B1) * g
        vn = ADAM_B2 * v_ref[...] + (1.0 - ADAM_B2) * (g * g)
        m_hat = mn / (1.0 - ADAM_B1 ** ADAM_STEP)
        v_hat = vn / (1.0 - ADAM_B2 ** ADAM_STEP)
        g_ref[...] = g
        d_ref[...] = -ADAM_LR * (m_hat / (jnp.sqrt(v_hat) + ADAM_EPS) + ADAM_WD * w_ref[...])
        mo_ref[...] = mn
        vo_ref[...] = vn

    row = pl.BlockSpec((tr, 128), lambda i: (i, 0))
    return pl.pallas_call(
        body, name=name, grid=(R // tr,),
        in_specs=[pl.BlockSpec((N_DEV, tr, 128), lambda i: (0, i, 0)), row, row, row],
        out_specs=[row] * 4,
        out_shape=[jax.ShapeDtypeStruct((R, 128), F32)] * 4,
        compiler_params=_params(("parallel",)),
    )(parts, w, m, v)


def _peers():
    x, y, c = lax.axis_index("x"), lax.axis_index("y"), lax.axis_index("c")
    me = 4 * x + 2 * y + c
    out = []
    for k in range(1, N_DEV):
        px = 1 - x if k & 4 else x
        py = 1 - y if k & 2 else y
        pc = 1 - c if k & 1 else c
        out.append(((px, py, pc), 4 * px + 2 * py + pc))
    return me, out


def _exchange(x, scatter, name):
    shard = x.shape[1:] if scatter else x.shape

    def body(x_ref, o_ref, send_sems, recv_sems, local_sem):
        me, peers = _peers()
        mine = pltpu.make_async_copy(x_ref.at[me] if scatter else x_ref, o_ref.at[me], local_sem)
        mine.start()
        sends = []
        for k, (peer, lin) in enumerate(peers):
            cp = pltpu.make_async_remote_copy(
                src_ref=x_ref.at[lin] if scatter else x_ref, dst_ref=o_ref.at[me],
                send_sem=send_sems.at[k], recv_sem=recv_sems.at[k],
                device_id=peer, device_id_type=pl.DeviceIdType.MESH)
            cp.start()
            sends.append(cp)
        for k, (peer, lin) in enumerate(peers):
            pltpu.make_async_remote_copy(
                src_ref=x_ref.at[lin] if scatter else x_ref, dst_ref=o_ref.at[lin],
                send_sem=send_sems.at[k], recv_sem=recv_sems.at[k],
                device_id=peer, device_id_type=pl.DeviceIdType.MESH).wait_recv()
        for cp in sends:
            cp.wait_send()
        mine.wait()

    return pl.pallas_call(
        body, name=name,
        in_specs=[pl.BlockSpec(memory_space=pl.ANY)], out_specs=pl.BlockSpec(memory_space=pl.ANY),
        out_shape=jax.ShapeDtypeStruct((N_DEV,) + tuple(shard), x.dtype),
        scratch_shapes=[pltpu.SemaphoreType.DMA((N_DEV - 1,)), pltpu.SemaphoreType.DMA((N_DEV - 1,)),
                        pltpu.SemaphoreType.DMA],
    )(x)


_SHARDED = (("w_in", (1024, 1088), 1), ("w_glu", (96, 768), 0), ("w_mem_kv", (128, 1024), 0),
            ("w_br_ssm", (768, 128), 1), ("w_br_attn", (768, 128), 1), ("w_br_mem", (512, 128), 1),
            ("w_out", (128, 1024), 0))
_ROWS = {n: s[0] * s[1] // 128 for n, s, _ in _SHARDED}
_LAYER_ROWS = sum(_ROWS.values())

_REPLICATED = (("norm_g", (2, 1024)), ("mem_norm_g", (2, 1024)), ("b_gate", (2, 3072)),
               ("ssm_lambda_re", (2, 48, 64)), ("ssm_lambda_im", (2, 48, 64)), ("ssm_log_dt", (2, 48)),
               ("ssm_b_re", (2, 48, 64, 16)), ("ssm_b_im", (2, 48, 64, 16)), ("ssm_c_re", (2, 48, 16, 64)),
               ("ssm_c_im", (2, 48, 16, 64)), ("ssm_d", (2, 768)), ("b_glu", (2, 768)), ("rel_bias", (32, 12)),
               ("final_norm_g", (1024,)))
_REP_TILE = 512 * 128


def _pack_sharded(tree):
    return jnp.concatenate([tree[n][l].reshape(-1, 128) for l in range(DEPTH) for n, _, _ in _SHARDED], axis=0)


def _unpack_sharded(flat):
    out = {n: [] for n, _, _ in _SHARDED}
    r = 0
    for l in range(DEPTH):
        for n, s, _ in _SHARDED:
            out[n].append(flat[r:r + _ROWS[n]].reshape(s))
            r += _ROWS[n]
    return {n: jnp.stack(v) for n, v in out.items()}


def _pack_replicated(tree):
    parts = []
    for n, s in _REPLICATED:
        a = tree[n].reshape(-1)
        parts.append(jnp.pad(a, (0, -a.shape[0] % 128)))
    flat = jnp.concatenate(parts)
    return jnp.pad(flat, (0, -flat.shape[0] % _REP_TILE)).reshape(-1, 128)


def _unpack_replicated(flat):
    flat = flat.reshape(-1)
    out, r = {}, 0
    for n, s in _REPLICATED:
        size = int(np.prod(s))
        out[n] = flat[r:r + size].reshape(s)
        r += size + (-size % 128)
    return out


def _full_weights(gathered, layer):
    out = {}
    r = layer * _LAYER_ROWS
    for n, s, axis in _SHARDED:
        part = gathered[:, r:r + _ROWS[n]].reshape((N_DEV,) + s)
        r += _ROWS[n]
        if axis == 0:
            out[n] = part.reshape(N_DEV * s[0], s[1])
        else:
            out[n] = part.transpose(1, 0, 2).reshape(s[0], N_DEV * s[1])
    w = out["w_in"]
    out["w_in"] = jnp.concatenate([w[:, st:st + wd] for _, wd, st in _SEGS], axis=1)
    return out


def _scatter_layout(grads):
    g = dict(grads)
    w = g["w_in"]
    order = sorted(_SEGS, key=lambda t: t[2])
    g["w_in"] = jnp.concatenate([w[:, _OFF[n]:_OFF[n] + wd] for n, wd, _ in order], axis=1)
    parts = []
    for n, s, axis in _SHARDED:
        a = g[n]
        if axis == 0:
            a = a.reshape((N_DEV,) + s)
        else:
            a = a.reshape(s[0], N_DEV, s[1]).transpose(1, 0, 2)
        parts.append(a.reshape(N_DEV, -1, 128))
    return jnp.concatenate(parts, axis=1)


def _rel_bucket(dist):
    n = jnp.maximum(dist, 0)
    max_exact = NUM_BUCKETS // 2
    n_f = jnp.maximum(n, 1).astype(F32)
    large = max_exact + (jnp.log(n_f / max_exact) / math.log(REL_MAX_DISTANCE / max_exact)
                         * (NUM_BUCKETS - max_exact)).astype(jnp.int32)
    large = jnp.minimum(large, NUM_BUCKETS - 1)
    return jnp.where(n < max_exact, n, large)


def _bias_table(rel_bias):
    qi = jnp.arange(ATTN_BLOCK)[:, None]
    kj = jnp.arange(2 * ATTN_BLOCK)[None, :]
    delta = ATTN_BLOCK + qi - kj
    band = (delta >= 0) & (delta <= ATTN_SPAN)
    tabs = []
    for g, r in enumerate(ATTN_DILATIONS):
        b = rel_bias[:, 4 * g:4 * g + 4][_rel_bucket(jnp.maximum(delta, 0) * r)]
        tabs.append(jnp.where(band[None], b.transpose(2, 0, 1), NEG_INF))
    return jnp.stack(tabs)


def _to_dilated(a):
    T = a.shape[0]
    outs = []
    for g, r in enumerate(ATTN_DILATIONS):
        ag = a[:, 256 * g:256 * (g + 1)].reshape(T // r, r, 4, 64)
        outs.append(ag.transpose(2, 1, 0, 3).reshape(4, T, 64))
    return jnp.stack(outs)


def _from_dilated(a):
    T = a.shape[2]
    outs = []
    for g, r in enumerate(ATTN_DILATIONS):
        outs.append(a[g].reshape(4, r, T // r, 64).transpose(2, 1, 0, 3).reshape(T, 256))
    return jnp.concatenate(outs, axis=1)


def _discretize(lam_re, lam_im, log_dt, b_re, b_im):
    dt = jnp.exp(log_dt)[:, None]
    mag = jnp.exp(lam_re * dt)
    abar_re, abar_im = mag * jnp.cos(lam_im * dt), mag * jnp.sin(lam_im * dt)
    den = lam_re * lam_re + lam_im * lam_im
    nr, ni = abar_re - 1.0, abar_im
    f_re = (nr * lam_re + ni * lam_im) / den
    f_im = (ni * lam_re - nr * lam_im) / den
    bbar_re = f_re[..., None] * b_re - f_im[..., None] * b_im
    bbar_im = f_re[..., None] * b_im + f_im[..., None] * b_re
    return abar_re, abar_im, bbar_re, bbar_im


def _block_diag(a):
    _, R, C = a.shape
    a = a.reshape(SSM_BLOCKS, 8, R, C)
    eye = jnp.eye(8, dtype=a.dtype)
    return jnp.einsum("kjrc,jJ->kjrJc", a, eye).reshape(SSM_BLOCKS, 8 * R, 8 * C)


def _diag_blocks(a, R, C):
    a = a.reshape(SSM_BLOCKS, 8, R, 8, C)
    idx = jnp.arange(8)
    return a[:, idx, :, idx, :].transpose(1, 0, 2, 3).reshape(SSM_GROUPS, R, C)


def _layer_fwd(x, mem, W, P, layer):
    tag = f"l{layer}"
    abar_re, abar_im, bbar_re, bbar_im = _discretize(P["ssm_lambda_re"][layer], P["ssm_lambda_im"][layer],
                                                     P["ssm_log_dt"][layer], P["ssm_b_re"][layer], P["ssm_b_im"][layer])
    c_re, c_im = P["ssm_c_re"][layer], P["ssm_c_im"][layer]
    ssm = dict(
        are=abar_re.reshape(1, N_STATE), aim=abar_im.reshape(1, N_STATE),
        bre=_block_diag(bbar_re.transpose(0, 2, 1)).astype(BF16), bim=_block_diag(bbar_im.transpose(0, 2, 1)).astype(BF16),
        cre=_block_diag(c_re.transpose(0, 2, 1)).astype(BF16), cimn=_block_diag(-c_im.transpose(0, 2, 1)).astype(BF16),
        ctre=_block_diag(c_re).astype(BF16), ctimn=_block_diag(-c_im).astype(BF16),
        btre=_block_diag(bbar_re).astype(BF16), btim=_block_diag(bbar_im).astype(BF16),
        d=P["ssm_d"][layer].reshape(1, D_SSM))
    bglu = P["b_glu"][layer].reshape(1, D_SSM)
    bgate = P["b_gate"][layer].reshape(1, N_GATES)
    g = P["norm_g"][layer].reshape(1, D_MODEL)
    gm = P["mem_norm_g"][layer].reshape(1, D_MODEL)
    bias = _bias_table(P["rel_bias"])

    T = x.shape[0]
    proj, h = _norm_proj(x, g, W["w_in"], min(T, 1024), 2176, f"{tag}_proj")
    xr, xi, y, o_ssm = _ssm_fwd(proj, ssm["bre"], ssm["bim"], ssm["cre"], ssm["cimn"], ssm["are"], ssm["aim"],
                                ssm["d"], W["w_glu"], bglu, 256, f"{tag}_ssm")
    qkv = [_to_dilated(proj[:, _OFF[n]:_OFF[n] + D_ATTN].astype(BF16)) for n in ("q", "k", "v")]
    o_dil, lse_dil = _attn_fwd(qkv[0], qkv[1], qkv[2], bias, f"{tag}_attn")
    o_att, lse = _from_dilated(o_dil), _from_dilated(lse_dil)
    o_attn = _attn_mix(o_att, lse, proj, 512, f"{tag}_attn_mix")
    kv, hm = _norm_proj(mem, gm, W["w_mem_kv"], mem.shape[0], 1024, f"{tag}_mem_kv")
    kvb = kv.astype(BF16)
    o_mem = _mem_fwd(proj, kvb, 512, f"{tag}_mem")
    x_out, merged = _merge_fwd(x, o_ssm, o_attn, o_mem, proj, bgate, W["w_br_ssm"], W["w_br_attn"], W["w_br_mem"],
                               W["w_out"], 256, f"{tag}_merge")
    res = dict(x=x, mem=mem, proj=proj, h=h, xr=xr, xi=xi, y=y, o_ssm=o_ssm, qkv=qkv, o_att=o_att, lse=lse,
               lse_dil=lse_dil, o_attn=o_attn, kvb=kvb, hm=hm, o_mem=o_mem, merged=merged, ssm=ssm, bglu=bglu,
               bgate=bgate, g=g, gm=gm, bias=bias)
    return x_out, res


def _layer_bwd(dx, res, W, P, layer):
    tag = f"l{layer}b"
    proj, ssm = res["proj"], res["ssm"]
    T = dx.shape[0]
    dgl, dbr, do_ssm, do_attn, do_mem, dbg = _merge_bwd(dx, res["o_ssm"], res["o_attn"], res["o_mem"], proj,
                                                        res["bgate"], W["w_br_ssm"], W["w_br_attn"], W["w_br_mem"],
                                                        W["w_out"], 256, f"{tag}_merge")
    gw = {}
    gw["w_out"] = _mm_tn(res["merged"], dx, 1024, 1024, 512, f"{tag}_dw_out")
    gw["w_br_ssm"] = _mm_tn(res["o_ssm"], dbr[:, 0:1024], 768, 1024, 512, f"{tag}_dw_br_ssm")
    gw["w_br_attn"] = _mm_tn(res["o_attn"], dbr[:, 1024:2048], 768, 1024, 512, f"{tag}_dw_br_attn")
    gw["w_br_mem"] = _mm_tn(res["o_mem"], dbr[:, 2048:3072], 512, 1024, 512, f"{tag}_dw_br_mem")

    dqm, dzm, dkv = _mem_bwd(do_mem, proj, res["kvb"], 512, f"{tag}_mem")
    M = dkv.shape[0]
    gw["w_mem_kv"] = _mm_tn(res["hm"], dkv, 1024, 1024, M, f"{tag}_dw_mem_kv")
    _, dgm = _proj_bwd(dkv.astype(BF16), W["w_mem_kv"], res["mem"], res["gm"], jnp.zeros_like(res["mem"]), M, 1024,
                       f"{tag}_mem_norm")

    do_g, corr, dza = _attn_mix_bwd(do_attn, res["o_att"], res["lse"], proj, 512, f"{tag}_attn_mix")
    dq_d, dk_d, dv_d, dbias = _attn_bwd(res["qkv"][0], res["qkv"][1], res["qkv"][2], _to_dilated(do_g),
                                        res["lse_dil"], _to_dilated(corr), res["bias"], f"{tag}_attn")
    dq, dk, dv = (_from_dilated(a).astype(BF16) for a in (dq_d, dk_d, dv_d))
    _, bias_vjp = jax.vjp(_bias_table, P["rel_bias"])
    d_rel_bias, = bias_vjp(dbias)

    dy, dzs, gelu_b, dt_b, dbglu = _glu_bwd(do_ssm, res["y"], proj, W["w_glu"], res["bglu"], 512, f"{tag}_glu")
    gw["w_glu"] = _mm_tn(gelu_b, dt_b, 768, 768, 512, f"{tag}_dw_glu")
    du, dbre, dbim, dcre, dcim, dare, daim, dd = _ssm_bwd(dy, proj, res["xr"], res["xi"], ssm["ctre"], ssm["ctimn"],
                                                          ssm["btre"], ssm["btim"], ssm["are"], ssm["aim"], ssm["d"],
                                                          256, f"{tag}_ssm")
    _, disc_vjp = jax.vjp(_discretize, P["ssm_lambda_re"][layer], P["ssm_lambda_im"][layer], P["ssm_log_dt"][layer],
                          P["ssm_b_re"][layer], P["ssm_b_im"][layer])
    d_lre, d_lim, d_ldt, d_bre, d_bim = disc_vjp((dare.reshape(SSM_GROUPS, SSM_STATE), daim.reshape(SSM_GROUPS, SSM_STATE),
                                                  _diag_blocks(dbre, SSM_STATE, SSM_GROUP),
                                                  _diag_blocks(dbim, SSM_STATE, SSM_GROUP)))

    dproj = jnp.concatenate([dgl, du, dzs, dq, dk, dv, dza, dqm, dzm], axis=1)
    gw["w_in"] = _mm_tn(res["h"], dproj, 1024, 2176, 512, f"{tag}_dw_in")
    dx_in, dg = _proj_bwd(dproj, W["w_in"], res["x"], res["g"], dx, min(T, 512), 2176, f"{tag}_proj")

    gp = dict(norm_g=dg[0], mem_norm_g=dgm[0], b_gate=dbg[0], ssm_lambda_re=d_lre, ssm_lambda_im=d_lim,
              ssm_log_dt=d_ldt, ssm_b_re=d_bre, ssm_b_im=d_bim,
              ssm_c_re=_diag_blocks(dcre, SSM_GROUP, SSM_STATE), ssm_c_im=_diag_blocks(dcim, SSM_GROUP, SSM_STATE),
              ssm_d=dd[0], b_glu=dbglu[0], rel_bias=d_rel_bias)
    return dx_in, gw, gp


def _local_step(x, mem, target, gathered, P):
    Ws = [_full_weights(gathered, l) for l in range(DEPTH)]
    res = []
    for l in range(DEPTH):
        x, r = _layer_fwd(x, mem, Ws[l], P, l)
        res.append(r)
    loss, dx, dgf = _loss_head(x, P["final_norm_g"].reshape(1, D_MODEL), target, 512, "loss_head")
    gws, gps = [None] * DEPTH, [None] * DEPTH
    for l in reversed(range(DEPTH)):
        dx, gws[l], gps[l] = _layer_bwd(dx, res[l], Ws[l], P, l)
    rep = {n: jnp.stack([gps[l][n] for l in range(DEPTH)]) for n, _ in _REPLICATED if n not in ("rel_bias", "final_norm_g")}
    rep["rel_bias"] = gps[0]["rel_bias"] + gps[1]["rel_bias"]
    rep["final_norm_g"] = dgf[0]
    scat = jnp.concatenate([_scatter_layout(gws[l]) for l in range(DEPTH)], axis=1)
    return loss[0, 0], dx, scat, rep


_WEIGHTS = ["norm_g", "mem_norm_g", "w_in", "b_gate", "ssm_lambda_re", "ssm_lambda_im", "ssm_log_dt", "ssm_b_re",
            "ssm_b_im", "ssm_c_re", "ssm_c_im", "ssm_d", "w_glu", "b_glu", "w_mem_kv", "w_br_ssm", "w_br_attn",
            "w_br_mem", "w_out", "rel_bias", "final_norm_g"]


def kernel(x, mem, norm_g, mem_norm_g, w_in, b_gate, ssm_lambda_re, ssm_lambda_im, ssm_log_dt, ssm_b_re, ssm_b_im, ssm_c_re, ssm_c_im, ssm_d, w_glu, b_glu, w_mem_kv, w_br_ssm, w_br_attn, w_br_mem, w_out, rel_bias, final_norm_g, loss_target, m_norm_g, m_mem_norm_g, m_w_in, m_b_gate, m_ssm_lambda_re, m_ssm_lambda_im, m_ssm_log_dt, m_ssm_b_re, m_ssm_b_im, m_ssm_c_re, m_ssm_c_im, m_ssm_d, m_w_glu, m_b_glu, m_w_mem_kv, m_w_br_ssm, m_w_br_attn, m_w_br_mem, m_w_out, m_rel_bias, m_final_norm_g, v_norm_g, v_mem_norm_g, v_w_in, v_b_gate, v_ssm_lambda_re, v_ssm_lambda_im, v_ssm_log_dt, v_ssm_b_re, v_ssm_b_im, v_ssm_c_re, v_ssm_c_im, v_ssm_d, v_w_glu, v_b_glu, v_w_mem_kv, v_w_br_ssm, v_w_br_attn, v_w_br_mem, v_w_out, v_rel_bias, v_final_norm_g):
    given = dict(locals())
    w = {n: given[n] for n in _WEIGHTS}
    m = {n: given["m_" + n] for n in _WEIGHTS}
    v = {n: given["v_" + n] for n in _WEIGHTS}

    gathered = _exchange(_pack_sharded(w).astype(BF16), False, "gather_weights")
    loss, dx, scat, rep = _local_step(x[0], mem[0], loss_target[0], gathered, w)
    loss = lax.psum(loss, ("x", "y", "c"))

    parts = _exchange(scat, True, "scatter_grads")
    sh = [_unpack_sharded(a) for a in _adamw(parts, _pack_sharded(w), _pack_sharded(m), _pack_sharded(v), 1672,
                                             "adamw_sharded")]
    rparts = _exchange(_pack_replicated(rep), False, "gather_small_grads")
    rp = [_unpack_replicated(a) for a in _adamw(rparts, _pack_replicated(w), _pack_replicated(m), _pack_replicated(v),
                                                512, "adamw_replicated")]
    outs = [loss, dx[None]]
    for kind in range(4):
        for n in _WEIGHTS:
            outs.append(sh[kind][n] if n in sh[kind] else rp[kind][n])
    return tuple(outs)
```

```python
import math

import jax
import jax.numpy as jnp
import numpy as np
from jax import lax
from jax.experimental import pallas as pl
from jax.experimental.pallas import tpu as pltpu

F32 = jnp.float32
BF16 = jnp.bfloat16

D_MODEL = 1024
DEPTH = 2
EPS = 1e-6
D_SSM = 768
SSM_GROUP = 16
SSM_GROUPS = 48
SSM_STATE = 64
N_STATE = SSM_GROUPS * SSM_STATE
SSM_BLOCKS = 6
D_ATTN = 768
ATTN_HEAD_DIM = 64
ATTN_GROUP_WIDTH = 256
ATTN_DILATIONS = (1, 4, 16)
ATTN_SPAN = 128
ATTN_BLOCK = 128
NUM_BUCKETS = 32
REL_MAX_DISTANCE = 2048
NEG_INF = -1e30
MEM_HEADS = 4
MEM_HEAD_DIM = 128
D_MEM = 512
N_GATES = 3 * D_MODEL
D_IN = 8704
N_DEV = 8
LANES = 128
ADAM_LR = 0.001
ADAM_B1 = 0.9
ADAM_B2 = 0.999
ADAM_EPS = 1e-08
ADAM_WD = 0.01
ADAM_STEP = 10

_SEGS = (("gates", 3072, 5632), ("u", 768, 0), ("z_ssm", 768, 768), ("q", 768, 1536), ("k", 768, 2304),
         ("v", 768, 3072), ("z_attn", 768, 3840), ("q_mem", 512, 4608), ("z_mem", 512, 5120))
_OFF = {}
_o = 0
for _n, _w, _s in _SEGS:
    _OFF[_n] = _o
    _o += _w

NN = (((1,), (0,)), ((), ()))
NT = (((1,), (1,)), ((), ()))
TN = (((0,), (0,)), ((), ()))

VMEM_LIMIT = 56 * 1024 * 1024


def _dot(a, b, dims=NN):
    return lax.dot_general(a, b, dims, preferred_element_type=F32)


def _sigmoid(x):
    return 1.0 / (1.0 + jnp.exp(-x))


def _gelu_parts(x):
    k = math.sqrt(2.0 / math.pi)
    t = jnp.tanh(k * (x + 0.044715 * (x * x * x)))
    cdf = 0.5 * (1.0 + t)
    dcdf = 0.5 * (1.0 - t * t) * k * (1.0 + 3.0 * 0.044715 * (x * x))
    return x * cdf, cdf + x * dcdf


def _params(sem, vmem=VMEM_LIMIT):
    return pltpu.CompilerParams(dimension_semantics=sem, vmem_limit_bytes=vmem)


def _full(shape):
    return pl.BlockSpec(shape, lambda *_: (0,) * len(shape))


def _norm_proj(x, g, w, tm, tn, name, out_dtype=F32):
    T, D = x.shape
    N = w.shape[1]

    def body(x_ref, g_ref, w_ref, o_ref, h_ref, hs):
        @pl.when(pl.program_id(1) == 0)
        def _():
            xv = x_ref[...]
            r = lax.rsqrt(jnp.mean(xv * xv, axis=-1, keepdims=True) + EPS)
            hv = (xv * r * g_ref[...]).astype(BF16)
            hs[...] = hv
            h_ref[...] = hv

        o_ref[...] = _dot(hs[...], w_ref[...]).astype(out_dtype)

    return pl.pallas_call(
        body, name=name, grid=(T // tm, N // tn),
        in_specs=[pl.BlockSpec((tm, D), lambda i, j: (i, 0)), _full((1, D)),
                  pl.BlockSpec((D, tn), lambda i, j: (0, j))],
        out_specs=[pl.BlockSpec((tm, tn), lambda i, j: (i, j)), pl.BlockSpec((tm, D), lambda i, j: (i, 0))],
        out_shape=[jax.ShapeDtypeStruct((T, N), out_dtype), jax.ShapeDtypeStruct((T, D), BF16)],
        scratch_shapes=[pltpu.VMEM((tm, D), BF16)],
        compiler_params=_params(("parallel", "arbitrary")),
    )(x, g, w)


def _mm_tn(a, b, tm, tn, tk, name, b_col=0, n=None):
    K, M = a.shape
    N = b.shape[1] if n is None else n
    nk = K // tk
    j0 = b_col // tn

    def body(a_ref, b_ref, o_ref, acc):
        k = pl.program_id(2)

        @pl.when(k == 0)
        def _():
            acc[...] = jnp.zeros_like(acc)

        acc[...] += _dot(a_ref[...].astype(BF16), b_ref[...].astype(BF16), TN)

        @pl.when(k == nk - 1)
        def _():
            o_ref[...] = acc[...].astype(BF16)

    return pl.pallas_call(
        body, name=name, grid=(M // tm, N // tn, nk),
        in_specs=[pl.BlockSpec((tk, tm), lambda i, j, k: (k, i)), pl.BlockSpec((tk, tn), lambda i, j, k: (k, j0 + j))],
        out_specs=pl.BlockSpec((tm, tn), lambda i, j, k: (i, j)),
        out_shape=jax.ShapeDtypeStruct((M, N), BF16),
        scratch_shapes=[pltpu.VMEM((tm, tn), F32)],
        compiler_params=_params(("parallel", "parallel", "arbitrary")),
    )(a, b)


def _proj_bwd(dp, w, x, g, dres, tm, tk, name):
    T, N = dp.shape
    D = w.shape[0]
    nk = N // tk

    def body(dp_ref, w_ref, x_ref, g_ref, dres_ref, dx_ref, dg_ref, acc):
        i, k = pl.program_id(0), pl.program_id(1)

        @pl.when(k == 0)
        def _():
            acc[...] = jnp.zeros_like(acc)

        @pl.when((i == 0) & (k == 0))
        def _():
            dg_ref[...] = jnp.zeros_like(dg_ref)

        acc[...] += _dot(dp_ref[...], w_ref[...], NT)

        @pl.when(k == nk - 1)
        def _():
            xv = x_ref[...]
            dh = acc[...]
            r = lax.rsqrt(jnp.mean(xv * xv, axis=-1, keepdims=True) + EPS)
            xr = xv * r
            dg_ref[...] += jnp.sum(dh * xr, axis=0, keepdims=True)
            wv = dh * g_ref[...]
            dx_ref[...] = dres_ref[...] + r * (wv - xr * jnp.mean(wv * xr, axis=-1, keepdims=True))

    return pl.pallas_call(
        body, name=name, grid=(T // tm, nk),
        in_specs=[pl.BlockSpec((tm, tk), lambda i, k: (i, k)), pl.BlockSpec((D, tk), lambda i, k: (0, k)),
                  pl.BlockSpec((tm, D), lambda i, k: (i, 0)), _full((1, D)),
                  pl.BlockSpec((tm, D), lambda i, k: (i, 0))],
        out_specs=[pl.BlockSpec((tm, D), lambda i, k: (i, 0)), _full((1, D))],
        out_shape=[jax.ShapeDtypeStruct((T, D), F32), jax.ShapeDtypeStruct((1, D), F32)],
        scratch_shapes=[pltpu.VMEM((tm, D), F32)],
        compiler_params=_params(("arbitrary", "arbitrary")),
    )(dp, w, x, g, dres)


def _ssm_fwd(proj, bre, bim, cre, cimn, are, aim, d, wglu, bglu, tc, name):
    T = proj.shape[0]
    ucol, zcol = _OFF["u"] // D_SSM, _OFF["z_ssm"] // D_SSM

    def body(u_ref, z_ref, bre_ref, bim_ref, cre_ref, cim_ref, are_ref, aim_ref, d_ref, wg_ref, bg_ref,
             xr_ref, xi_ref, y_ref, o_ref, car_r, car_i):
        @pl.when(pl.program_id(0) == 0)
        def _():
            car_r[...] = jnp.zeros_like(car_r)
            car_i[...] = jnp.zeros_like(car_i)

        u = u_ref[...]
        ub = u.astype(BF16)
        for k in range(SSM_BLOCKS):
            uk = ub[:, 128 * k:128 * (k + 1)]
            xr_ref[:, 512 * k:512 * (k + 1)] = _dot(uk, bre_ref[k])
            xi_ref[:, 512 * k:512 * (k + 1)] = _dot(uk, bim_ref[k])
        ar, ai = are_ref[...], aim_ref[...]

        def step(t, c):
            pr, pi = c
            nr = ar * pr - ai * pi + xr_ref[pl.ds(t, 1), :]
            ni = ar * pi + ai * pr + xi_ref[pl.ds(t, 1), :]
            xr_ref[pl.ds(t, 1), :] = nr
            xi_ref[pl.ds(t, 1), :] = ni
            return nr, ni

        pr, pi = lax.fori_loop(0, tc, step, (car_r[...], car_i[...]))
        car_r[...] = pr
        car_i[...] = pi

        ys = []
        for k in range(SSM_BLOCKS):
            xrk = xr_ref[:, 512 * k:512 * (k + 1)].astype(BF16)
            xik = xi_ref[:, 512 * k:512 * (k + 1)].astype(BF16)
            ys.append(_dot(xrk, cre_ref[k]) + _dot(xik, cim_ref[k]))
        y = jnp.concatenate(ys, axis=1) + d_ref[...] * u
        y_ref[...] = y
        gl, _ = _gelu_parts(y)
        t = _dot(gl.astype(BF16), wg_ref[...]) + bg_ref[...]
        z = z_ref[...]
        o_ref[...] = (gl * _sigmoid(t) * (z * _sigmoid(z))).astype(BF16)

    return pl.pallas_call(
        body, name=name, grid=(T // tc,),
        in_specs=[pl.BlockSpec((tc, D_SSM), lambda i: (i, ucol)), pl.BlockSpec((tc, D_SSM), lambda i: (i, zcol)),
                  _full((SSM_BLOCKS, 128, 512)), _full((SSM_BLOCKS, 128, 512)),
                  _full((SSM_BLOCKS, 512, 128)), _full((SSM_BLOCKS, 512, 128)),
                  _full((1, N_STATE)), _full((1, N_STATE)), _full((1, D_SSM)),
                  _full((D_SSM, D_SSM)), _full((1, D_SSM))],
        out_specs=[pl.BlockSpec((tc, N_STATE), lambda i: (i, 0)), pl.BlockSpec((tc, N_STATE), lambda i: (i, 0)),
                   pl.BlockSpec((tc, D_SSM), lambda i: (i, 0)), pl.BlockSpec((tc, D_SSM), lambda i: (i, 0))],
        out_shape=[jax.ShapeDtypeStruct((T, N_STATE), F32), jax.ShapeDtypeStruct((T, N_STATE), F32),
                   jax.ShapeDtypeStruct((T, D_SSM), F32), jax.ShapeDtypeStruct((T, D_SSM), BF16)],
        scratch_shapes=[pltpu.VMEM((1, N_STATE), F32), pltpu.VMEM((1, N_STATE), F32)],
        compiler_params=_params(("arbitrary",)),
    )(proj, proj, bre, bim, cre, cimn, are, aim, d, wglu, bglu)


def _glu_bwd(do, y, proj, wglu, bglu, tm, name):
    T = y.shape[0]
    zcol = _OFF["z_ssm"] // D_SSM

    def body(do_ref, y_ref, z_ref, wg_ref, bg_ref, dy_ref, dz_ref, g_ref, dt_ref, db_ref):
        @pl.when(pl.program_id(0) == 0)
        def _():
            db_ref[...] = jnp.zeros_like(db_ref)

        dov = do_ref[...]
        gl, dgl = _gelu_parts(y_ref[...])
        glb = gl.astype(BF16)
        sg = _sigmoid(_dot(glb, wg_ref[...]) + bg_ref[...])
        z = z_ref[...]
        sz = _sigmoid(z)
        dz_ref[...] = (dov * (gl * sg) * (sz * (1.0 + z * (1.0 - sz)))).astype(BF16)
        dy2 = dov * (z * sz)
        dt = dy2 * gl * (sg * (1.0 - sg))
        dtb = dt.astype(BF16)
        dg = dy2 * sg + _dot(dtb, wg_ref[...], NT)
        dy_ref[...] = dg * dgl
        g_ref[...] = glb
        dt_ref[...] = dtb
        db_ref[...] += jnp.sum(dt, axis=0, keepdims=True)

    row = lambda i: (i, 0)
    return pl.pallas_call(
        body, name=name, grid=(T // tm,),
        in_specs=[pl.BlockSpec((tm, D_SSM), row), pl.BlockSpec((tm, D_SSM), row),
                  pl.BlockSpec((tm, D_SSM), lambda i: (i, zcol)), _full((D_SSM, D_SSM)), _full((1, D_SSM))],
        out_specs=[pl.BlockSpec((tm, D_SSM), row)] * 4 + [_full((1, D_SSM))],
        out_shape=[jax.ShapeDtypeStruct((T, D_SSM), F32), jax.ShapeDtypeStruct((T, D_SSM), BF16),
                   jax.ShapeDtypeStruct((T, D_SSM), BF16), jax.ShapeDtypeStruct((T, D_SSM), BF16),
                   jax.ShapeDtypeStruct((1, D_SSM), F32)],
        compiler_params=_params(("arbitrary",)),
    )(do, y, proj, wglu, bglu)


def _ssm_bwd(dy, proj, xr, xi, ctre, ctimn, btre, btim, are, aim, d, tc, name):
    T = dy.shape[0]
    nc = T // tc
    ucol = _OFF["u"] // D_SSM
    rb = tc // 8

    def body(dy_ref, u_ref, xr_ref, xi_ref, xpr_ref, xpi_ref, ctre_ref, ctim_ref, btre_ref, btim_ref,
             are_ref, aim_ref, d_ref,
             du_ref, dbre_ref, dbim_ref, dcre_ref, dcim_ref, dare_ref, daim_ref, dd_ref, gr, gi, car_r, car_i):
        i = pl.program_id(0)

        @pl.when(i == 0)
        def _():
            for ref in (car_r, car_i, dbre_ref, dbim_ref, dcre_ref, dcim_ref, dare_ref, daim_ref, dd_ref):
                ref[...] = jnp.zeros_like(ref)

        dyv = dy_ref[...]
        dyb = dyv.astype(BF16)
        u = u_ref[...]
        ub = u.astype(BF16)
        for k in range(SSM_BLOCKS):
            dk = dyb[:, 128 * k:128 * (k + 1)]
            gr[:, 512 * k:512 * (k + 1)] = _dot(dk, ctre_ref[k])
            gi[:, 512 * k:512 * (k + 1)] = _dot(dk, ctim_ref[k])
        ar, ai = are_ref[...], aim_ref[...]

        def step(s, c):
            pr, pi = c
            t = tc - 1 - s
            nr = gr[pl.ds(t, 1), :] + ar * pr + ai * pi
            ni = gi[pl.ds(t, 1), :] + ar * pi - ai * pr
            gr[pl.ds(t, 1), :] = nr
            gi[pl.ds(t, 1), :] = ni
            return nr, ni

        pr, pi = lax.fori_loop(0, tc, step, (car_r[...], car_i[...]))
        car_r[...] = pr
        car_i[...] = pi

        xrv, xiv = xr_ref[...], xi_ref[...]
        keep = jnp.where(i == nc - 1, 0.0, 1.0)
        row0 = lax.broadcasted_iota(jnp.int32, (tc, 1), 0) == 0
        xsr = jnp.where(row0, xpr_ref[7:8, :] * keep, pltpu.roll(xrv, 1, axis=0))
        xsi = jnp.where(row0, xpi_ref[7:8, :] * keep, pltpu.roll(xiv, 1, axis=0))
        grv, giv = gr[...], gi[...]
        dare_ref[...] += jnp.sum(grv * xsr + giv * xsi, axis=0, keepdims=True)
        daim_ref[...] += jnp.sum(giv * xsr - grv * xsi, axis=0, keepdims=True)
        dd_ref[...] += jnp.sum(dyv * u, axis=0, keepdims=True)

        dus = []
        for k in range(SSM_BLOCKS):
            sl = slice(512 * k, 512 * (k + 1))
            ch = slice(128 * k, 128 * (k + 1))
            grb, gib = grv[:, sl].astype(BF16), giv[:, sl].astype(BF16)
            dus.append(_dot(grb, btre_ref[k]) + _dot(gib, btim_ref[k]))
            dbre_ref[k] += _dot(grb, ub[:, ch], TN)
            dbim_ref[k] += _dot(gib, ub[:, ch], TN)
            dcre_ref[k] += _dot(dyb[:, ch], xrv[:, sl].astype(BF16), TN)
            dcim_ref[k] -= _dot(dyb[:, ch], xiv[:, sl].astype(BF16), TN)
        du_ref[...] = (jnp.concatenate(dus, axis=1) + d_ref[...] * dyv).astype(BF16)

    rev = lambda i: (nc - 1 - i, 0)
    prev = lambda i: (jnp.maximum((nc - 1 - i) * rb - 1, 0), 0)
    return pl.pallas_call(
        body, name=name, grid=(nc,),
        in_specs=[pl.BlockSpec((tc, D_SSM), rev), pl.BlockSpec((tc, D_SSM), lambda i: (nc - 1 - i, ucol)),
                  pl.BlockSpec((tc, N_STATE), rev), pl.BlockSpec((tc, N_STATE), rev),
                  pl.BlockSpec((8, N_STATE), prev), pl.BlockSpec((8, N_STATE), prev),
                  _full((SSM_BLOCKS, 128, 512)), _full((SSM_BLOCKS, 128, 512)),
                  _full((SSM_BLOCKS, 512, 128)), _full((SSM_BLOCKS, 512, 128)),
                  _full((1, N_STATE)), _full((1, N_STATE)), _full((1, D_SSM))],
        out_specs=[pl.BlockSpec((tc, D_SSM), rev),
                   _full((SSM_BLOCKS, 512, 128)), _full((SSM_BLOCKS, 512, 128)),
                   _full((SSM_BLOCKS, 128, 512)), _full((SSM_BLOCKS, 128, 512)),
                   _full((1, N_STATE)), _full((1, N_STATE)), _full((1, D_SSM))],
        out_shape=[jax.ShapeDtypeStruct((T, D_SSM), BF16),
                   jax.ShapeDtypeStruct((SSM_BLOCKS, 512, 128), F32), jax.ShapeDtypeStruct((SSM_BLOCKS, 512, 128), F32),
                   jax.ShapeDtypeStruct((SSM_BLOCKS, 128, 512), F32), jax.ShapeDtypeStruct((SSM_BLOCKS, 128, 512), F32),
                   jax.ShapeDtypeStruct((1, N_STATE), F32), jax.ShapeDtypeStruct((1, N_STATE), F32),
                   jax.ShapeDtypeStruct((1, D_SSM), F32)],
        scratch_shapes=[pltpu.VMEM((tc, N_STATE), F32), pltpu.VMEM((tc, N_STATE), F32),
                        pltpu.VMEM((1, N_STATE), F32), pltpu.VMEM((1, N_STATE), F32)],
        compiler_params=_params(("arbitrary",)),
    )(dy, proj, xr, xi, xr, xi, ctre, ctimn, btre, btim, are, aim, d)


def _rel_bucket(dist):
    n = jnp.maximum(dist, 0)
    max_exact = NUM_BUCKETS // 2
    n_f = jnp.maximum(n, 1).astype(F32)
    large = max_exact + (jnp.log(n_f / max_exact) / math.log(REL_MAX_DISTANCE / max_exact)
                         * (NUM_BUCKETS - max_exact)).astype(jnp.int32)
    large = jnp.minimum(large, NUM_BUCKETS - 1)
    return jnp.where(n < max_exact, n, large)


def _bucket_tables():
    qi = jnp.arange(ATTN_BLOCK)[:, None]
    kj = jnp.arange(2 * ATTN_BLOCK)[None, :]
    delta = jnp.maximum(ATTN_BLOCK + qi - kj, 0)
    return jnp.stack([_rel_bucket(delta * r) for r in ATTN_DILATIONS]).astype(jnp.int32)


def _bias_tables(rel_bias, buckets, name):
    def body(tab_ref, bk_ref, o_ref):
        g = pl.program_id(0)
        bk = bk_ref[...]
        qi = lax.broadcasted_iota(jnp.int32, bk.shape, 0)
        kj = lax.broadcasted_iota(jnp.int32, bk.shape, 1)
        delta = ATTN_BLOCK + qi - kj
        band = (delta >= 0) & (delta <= ATTN_SPAN)
        accs = [jnp.zeros(bk.shape, F32) for _ in range(4)]
        for b in range(NUM_BUCKETS):
            hit = bk == b
            for h in range(4):
                accs[h] = jnp.where(hit, tab_ref[b, 4 * g + h], accs[h])
        for h in range(4):
            o_ref[h] = jnp.where(band, accs[h], NEG_INF)

    return pl.pallas_call(
        body, name=name, grid=(3,),
        in_specs=[pl.BlockSpec(memory_space=pltpu.SMEM),
                  pl.BlockSpec((None, ATTN_BLOCK, 2 * ATTN_BLOCK), lambda g: (g, 0, 0))],
        out_specs=pl.BlockSpec((None, 4, ATTN_BLOCK, 2 * ATTN_BLOCK), lambda g: (g, 0, 0, 0)),
        out_shape=jax.ShapeDtypeStruct((3, 4, ATTN_BLOCK, 2 * ATTN_BLOCK), F32),
        compiler_params=_params(("parallel",)),
    )(rel_bias, buckets)


def _bias_grad(db0, db1, buckets, name):
    def body(a_ref, b_ref, bk_ref, o_ref):
        bk = bk_ref[...]
        for h in range(4):
            dv = a_ref[h] + b_ref[h]
            for b in range(NUM_BUCKETS):
                o_ref[h, b:b + 1, :] = jnp.sum(jnp.where(bk == b, dv, 0.0), axis=0, keepdims=True)

    tab = pl.BlockSpec((None, 4, ATTN_BLOCK, 2 * ATTN_BLOCK), lambda g: (g, 0, 0, 0))
    return pl.pallas_call(
        body, name=name, grid=(3,),
        in_specs=[tab, tab, pl.BlockSpec((None, ATTN_BLOCK, 2 * ATTN_BLOCK), lambda g: (g, 0, 0))],
        out_specs=pl.BlockSpec((None, 4, NUM_BUCKETS, 2 * ATTN_BLOCK), lambda g: (g, 0, 0, 0)),
        out_shape=jax.ShapeDtypeStruct((3, 4, NUM_BUCKETS, 2 * ATTN_BLOCK), F32),
        compiler_params=_params(("parallel",)),
    )(db0, db1, buckets)


def _residue_rows(s, r):
    return pl.ds(s, ATTN_BLOCK, stride=r) if r > 1 else pl.ds(0, ATTN_BLOCK)


def _over_residues(r, fn):
    if r == 1:
        fn(0)
    else:
        lax.fori_loop(0, r, lambda s, c: (fn(s), c)[1], 0)


def _attn_cols(g):
    return tuple((_OFF[n] + ATTN_GROUP_WIDTH * g) // LANES for n in ("q", "k", "v"))


def _attn_fwd(proj, bias, g, name):
    r = ATTN_DILATIONS[g]
    T = proj.shape[0]
    tb = ATTN_BLOCK * r
    qc, kc, vc = _attn_cols(g)
    scale = ATTN_HEAD_DIM ** -0.5

    def body(q_ref, kc_ref, kp_ref, vc_ref, vp_ref, bias_ref, o_ref, lse_ref):
        lane = lax.broadcasted_iota(jnp.int32, (ATTN_BLOCK, LANES), 1)
        kj = lax.broadcasted_iota(jnp.int32, (ATTN_BLOCK, 2 * ATTN_BLOCK), 1)
        dead = (pl.program_id(0) == 0) & (kj < ATTN_BLOCK)

        def one(s):
            rows = _residue_rows(s, r)
            q = q_ref[rows, :]
            kcat = jnp.concatenate([kp_ref[rows, :], kc_ref[rows, :]], axis=0).astype(BF16)
            vcat = jnp.concatenate([vp_ref[rows, :], vc_ref[rows, :]], axis=0).astype(BF16)
            o_acc = jnp.zeros((ATTN_BLOCK, LANES), F32)
            l_acc = jnp.zeros((ATTN_BLOCK, LANES), F32)
            for hh in range(2):
                mine = (lane >= ATTN_HEAD_DIM) if hh else (lane < ATTN_HEAD_DIM)
                qm = jnp.where(mine, q, 0.0).astype(BF16)
                sc = jnp.where(dead, NEG_INF, _dot(qm, kcat, NT) * scale + bias_ref[hh])
                m = jnp.max(sc, axis=-1, keepdims=True)
                p = jnp.exp(sc - m)
                l = jnp.sum(p, axis=-1, keepdims=True)
                o_acc = jnp.where(mine, _dot((p / l).astype(BF16), vcat), o_acc)
                l_acc = jnp.where(mine, m + jnp.log(l), l_acc)
            o_ref[rows, :] = o_acc
            lse_ref[rows, :] = l_acc

        _over_residues(r, one)

    cur = lambda c: pl.BlockSpec((tb, LANES), lambda b, p: (b, c + p))
    prev = lambda c: pl.BlockSpec((tb, LANES), lambda b, p: (jnp.maximum(b - 1, 0), c + p))
    out = pl.BlockSpec((tb, LANES), lambda b, p: (b, p))
    return pl.pallas_call(
        body, name=name, grid=(T // tb, 2),
        in_specs=[cur(qc), cur(kc), prev(kc), cur(vc), prev(vc),
                  pl.BlockSpec((2, ATTN_BLOCK, 2 * ATTN_BLOCK), lambda b, p: (p, 0, 0))],
        out_specs=[out, out],
        out_shape=[jax.ShapeDtypeStruct((T, ATTN_GROUP_WIDTH), F32), jax.ShapeDtypeStruct((T, ATTN_GROUP_WIDTH), F32)],
        compiler_params=_params(("parallel", "parallel")),
    )(proj, proj, proj, proj, proj, bias)


def _attn_bwd(proj, do, corr, lse, bias, g, name):
    r = ATTN_DILATIONS[g]
    T = proj.shape[0]
    tb = ATTN_BLOCK * r
    nb = T // tb
    qc, kc, vc = _attn_cols(g)
    dc = ATTN_GROUP_WIDTH * g // LANES
    scale = ATTN_HEAD_DIM ** -0.5

    def body(q_ref, kc_ref, kp_ref, vc_ref, vp_ref, do_ref, corr_ref, lse_ref, bias_ref,
             dq_ref, dk_ref, dv_ref, db_ref, dq_s, dkc_s, dkp_s, dvc_s, dvp_s, kacc, vacc):
        b = pl.program_id(1)

        @pl.when(b == 0)
        def _():
            db_ref[...] = jnp.zeros_like(db_ref)
            kacc[...] = jnp.zeros_like(kacc)
            vacc[...] = jnp.zeros_like(vacc)

        @pl.when(b == nb)
        def _():
            dk_ref[...] = kacc[...].astype(BF16)
            dv_ref[...] = vacc[...].astype(BF16)

        @pl.when(b < nb)
        def _():
            lane = lax.broadcasted_iota(jnp.int32, (ATTN_BLOCK, LANES), 1)
            kj = lax.broadcasted_iota(jnp.int32, (ATTN_BLOCK, 2 * ATTN_BLOCK), 1)
            dead = (b == 0) & (kj < ATTN_BLOCK)

            def one(s):
                rows = _residue_rows(s, r)
                q = q_ref[rows, :]
                kcat = jnp.concatenate([kp_ref[rows, :], kc_ref[rows, :]], axis=0).astype(BF16)
                vcat = jnp.concatenate([vp_ref[rows, :], vc_ref[rows, :]], axis=0).astype(BF16)
                dov, corrv, lsev = do_ref[rows, :], corr_ref[rows, :], lse_ref[rows, :]
                dq_acc = jnp.zeros((ATTN_BLOCK, LANES), F32)
                dk_acc = jnp.zeros((2 * ATTN_BLOCK, LANES), F32)
                dv_acc = jnp.zeros((2 * ATTN_BLOCK, LANES), F32)
                for hh in range(2):
                    mine = (lane >= ATTN_HEAD_DIM) if hh else (lane < ATTN_HEAD_DIM)
                    col = slice(ATTN_HEAD_DIM * hh, ATTN_HEAD_DIM * hh + 1)
                    qm = jnp.where(mine, q, 0.0).astype(BF16)
                    dom = jnp.where(mine, dov, 0.0).astype(BF16)
                    sc = jnp.where(dead, NEG_INF, _dot(qm, kcat, NT) * scale + bias_ref[hh])
                    p = jnp.exp(sc - lsev[:, col])
                    ds = p * (_dot(dom, vcat, NT) - corrv[:, col])
                    db_ref[hh] += ds
                    dsb = ds.astype(BF16)
                    dq_acc = jnp.where(mine, _dot(dsb, kcat) * scale, dq_acc)
                    dk_acc += _dot(dsb, qm, TN) * scale
                    dv_acc += _dot(p.astype(BF16), dom, TN)
                dq_s[rows, :] = dq_acc
                dkp_s[rows, :] = dk_acc[:ATTN_BLOCK]
                dkc_s[rows, :] = dk_acc[ATTN_BLOCK:]
                dvp_s[rows, :] = dv_acc[:ATTN_BLOCK]
                dvc_s[rows, :] = dv_acc[ATTN_BLOCK:]

            _over_residues(r, one)
            dq_ref[...] = dq_s[...].astype(BF16)
            dk_ref[...] = (kacc[...] + dkp_s[...]).astype(BF16)
            dv_ref[...] = (vacc[...] + dvp_s[...]).astype(BF16)
            kacc[...] = dkc_s[...]
            vacc[...] = dvc_s[...]

    last = nb - 1
    blk = (tb, LANES)
    cur = lambda c: pl.BlockSpec(blk, lambda p, b: (jnp.minimum(b, last), c + p))
    prev = lambda c: pl.BlockSpec(blk, lambda p, b: (jnp.clip(b - 1, 0, last), c + p))
    tab = pl.BlockSpec((2, ATTN_BLOCK, 2 * ATTN_BLOCK), lambda p, b: (p, 0, 0))
    return pl.pallas_call(
        body, name=name, grid=(2, nb + 1),
        in_specs=[cur(qc), cur(kc), prev(kc), cur(vc), prev(vc), cur(dc), cur(dc), cur(0), tab],
        out_specs=[cur(0), prev(0), prev(0), tab],
        out_shape=[jax.ShapeDtypeStruct((T, ATTN_GROUP_WIDTH), BF16)] * 3
        + [jax.ShapeDtypeStruct((4, ATTN_BLOCK, 2 * ATTN_BLOCK), F32)],
        scratch_shapes=[pltpu.VMEM(blk, F32)] * 7,
        compiler_params=_params(("arbitrary", "arbitrary")),
    )(proj, proj, proj, proj, proj, do, corr, lse, bias)


def _mix_weights(lses):
    m = jnp.maximum(jnp.maximum(lses[0], lses[1]), lses[2])
    es = [jnp.exp(l - m) for l in lses]
    inv = 1.0 / (es[0] + es[1] + es[2])
    return jnp.concatenate([e * inv for e in es], axis=1)


def _attn_mix(os, lses, proj, tm, name):
    T = proj.shape[0]
    zcol = _OFF["z_attn"] // D_ATTN

    def body(o0, o1, o2, l0, l1, l2, z_ref, out_ref):
        z = z_ref[...]
        o = jnp.concatenate([o0[...], o1[...], o2[...]], axis=1)
        alpha = _mix_weights([l0[...], l1[...], l2[...]])
        out_ref[...] = (o * alpha * (z * _sigmoid(z))).astype(BF16)

    row = lambda i: (i, 0)
    grp = pl.BlockSpec((tm, ATTN_GROUP_WIDTH), row)
    return pl.pallas_call(
        body, name=name, grid=(T // tm,),
        in_specs=[grp] * 6 + [pl.BlockSpec((tm, D_ATTN), lambda i: (i, zcol))],
        out_specs=pl.BlockSpec((tm, D_ATTN), row),
        out_shape=jax.ShapeDtypeStruct((T, D_ATTN), BF16),
        compiler_params=_params(("parallel",)),
    )(*os, *lses, proj)


def _attn_mix_bwd(d, os, lses, proj, tm, name):
    T = proj.shape[0]
    zcol = _OFF["z_attn"] // D_ATTN

    def body(d_ref, o0, o1, o2, l0, l1, l2, z_ref, do_ref, corr_ref, dz_ref):
        dv, z = d_ref[...], z_ref[...]
        ov = jnp.concatenate([o0[...], o1[...], o2[...]], axis=1)
        alpha = _mix_weights([l0[...], l1[...], l2[...]])
        sz = _sigmoid(z)
        oc = ov * alpha
        dz_ref[...] = (dv * oc * (sz * (1.0 + z * (1.0 - sz)))).astype(BF16)
        doc = dv * (z * sz)
        do_ref[...] = doc * alpha
        pr = doc * oc
        p3 = pr[:, 0:256] + pr[:, 256:512] + pr[:, 512:768]
        li = lax.broadcasted_iota(jnp.int32, (256, 256), 0) // ATTN_HEAD_DIM
        lj = lax.broadcasted_iota(jnp.int32, (256, 256), 1) // ATTN_HEAD_DIM
        ones = jnp.where(li == lj, 1.0, 0.0).astype(F32)
        s = lax.dot_general(p3, ones, NN, precision=lax.Precision.HIGHEST, preferred_element_type=F32)
        corr_ref[...] = alpha * jnp.concatenate([s, s, s], axis=1)

    row = lambda i: (i, 0)
    grp = pl.BlockSpec((tm, ATTN_GROUP_WIDTH), row)
    return pl.pallas_call(
        body, name=name, grid=(T // tm,),
        in_specs=[pl.BlockSpec((tm, D_ATTN), row)] + [grp] * 6 + [pl.BlockSpec((tm, D_ATTN), lambda i: (i, zcol))],
        out_specs=[pl.BlockSpec((tm, D_ATTN), row)] * 3,
        out_shape=[jax.ShapeDtypeStruct((T, D_ATTN), F32), jax.ShapeDtypeStruct((T, D_ATTN), F32),
                   jax.ShapeDtypeStruct((T, D_ATTN), BF16)],
        compiler_params=_params(("parallel",)),
    )(d, *os, *lses, proj)


def _mem_probs(q_ref, kv_ref, h):
    hs = slice(MEM_HEAD_DIM * h, MEM_HEAD_DIM * (h + 1))
    qh = q_ref[:, hs].astype(BF16)
    kh = kv_ref[:, hs]
    vh = kv_ref[:, D_MEM + MEM_HEAD_DIM * h:D_MEM + MEM_HEAD_DIM * (h + 1)]
    s = _dot(qh, kh, NT) * (MEM_HEAD_DIM ** -0.5)
    p = jnp.exp(s - jnp.max(s, axis=-1, keepdims=True))
    pn = p / jnp.sum(p, axis=-1, keepdims=True)
    return qh, kh, vh, pn


def _mem_fwd(proj, kv, tm, name):
    T = proj.shape[0]
    M = kv.shape[0]
    qcol, zcol = _OFF["q_mem"] // D_MEM, _OFF["z_mem"] // D_MEM

    def body(q_ref, z_ref, kv_ref, o_ref):
        outs = []
        for h in range(MEM_HEADS):
            _, _, vh, pn = _mem_probs(q_ref, kv_ref, h)
            outs.append(_dot(pn.astype(BF16), vh))
        z = z_ref[...]
        o_ref[...] = (jnp.concatenate(outs, axis=1) * (z * _sigmoid(z))).astype(BF16)

    return pl.pallas_call(
        body, name=name, grid=(T // tm,),
        in_specs=[pl.BlockSpec((tm, D_MEM), lambda i: (i, qcol)), pl.BlockSpec((tm, D_MEM), lambda i: (i, zcol)),
                  _full((M, 2 * D_MEM))],
        out_specs=pl.BlockSpec((tm, D_MEM), lambda i: (i, 0)),
        out_shape=jax.ShapeDtypeStruct((T, D_MEM), BF16),
        compiler_params=_params(("parallel",)),
    )(proj, proj, kv)


def _mem_bwd(d, proj, kv, tm, name):
    T = proj.shape[0]
    M = kv.shape[0]
    qcol, zcol = _OFF["q_mem"] // D_MEM, _OFF["z_mem"] // D_MEM

    def body(d_ref, q_ref, z_ref, kv_ref, dq_ref, dz_ref, dkv_ref):
        @pl.when(pl.program_id(0) == 0)
        def _():
            dkv_ref[...] = jnp.zeros_like(dkv_ref)

        z = z_ref[...]
        sz = _sigmoid(z)
        dv = d_ref[...]
        dov = dv * (z * sz)
        scale = MEM_HEAD_DIM ** -0.5
        outs, dqs = [], []
        for h in range(MEM_HEADS):
            hs = slice(MEM_HEAD_DIM * h, MEM_HEAD_DIM * (h + 1))
            qh, kh, vh, pn = _mem_probs(q_ref, kv_ref, h)
            pnb = pn.astype(BF16)
            oh = _dot(pnb, vh)
            outs.append(oh)
            doh = dov[:, hs]
            dohb = doh.astype(BF16)
            dp = _dot(dohb, vh, NT)
            ds = pn * (dp - jnp.sum(doh * oh, axis=-1, keepdims=True))
            dsb = ds.astype(BF16)
            dqs.append(_dot(dsb, kh) * scale)
            dkv_ref[:, hs] += _dot(dsb, qh, TN) * scale
            vs = slice(D_MEM + MEM_HEAD_DIM * h, D_MEM + MEM_HEAD_DIM * (h + 1))
            dkv_ref[:, vs] += _dot(pnb, dohb, TN)
        dq_ref[...] = jnp.concatenate(dqs, axis=1).astype(BF16)
        dz_ref[...] = (dv * jnp.concatenate(outs, axis=1) * (sz * (1.0 + z * (1.0 - sz)))).astype(BF16)

    row = lambda i: (i, 0)
    return pl.pallas_call(
        body, name=name, grid=(T // tm,),
        in_specs=[pl.BlockSpec((tm, D_MEM), row), pl.BlockSpec((tm, D_MEM), lambda i: (i, qcol)),
                  pl.BlockSpec((tm, D_MEM), lambda i: (i, zcol)), _full((M, 2 * D_MEM))],
        out_specs=[pl.BlockSpec((tm, D_MEM), row), pl.BlockSpec((tm, D_MEM), row), _full((M, 2 * D_MEM))],
        out_shape=[jax.ShapeDtypeStruct((T, D_MEM), BF16), jax.ShapeDtypeStruct((T, D_MEM), BF16),
                   jax.ShapeDtypeStruct((M, 2 * D_MEM), F32)],
        compiler_params=_params(("arbitrary",)),
    )(d, proj, proj, kv)


def _branches_and_gates(os_ref, oa_ref, om_ref, gl_refs, bg_ref, ws_ref, wa_ref, wm_ref):
    outs = (_dot(os_ref[...], ws_ref[...]), _dot(oa_ref[...], wa_ref[...]), _dot(om_ref[...], wm_ref[...]))
    gates = tuple(_sigmoid(gl_refs[k][...] + bg_ref[:, D_MODEL * k:D_MODEL * (k + 1)]) for k in range(3))
    return outs, gates


def _merge_specs(tm):
    row = lambda i: (i, 0)
    gate = [pl.BlockSpec((tm, D_MODEL), (lambda i, k=k: (i, k))) for k in range(3)]
    return ([pl.BlockSpec((tm, D_SSM), row), pl.BlockSpec((tm, D_ATTN), row), pl.BlockSpec((tm, D_MEM), row)] + gate
            + [_full((1, N_GATES)), _full((D_SSM, D_MODEL)), _full((D_ATTN, D_MODEL)), _full((D_MEM, D_MODEL)),
               _full((D_MODEL, D_MODEL))])


def _merge_fwd(x, o_ssm, o_attn, o_mem, proj, bg, ws, wa, wm, wo, tm, name):
    T = x.shape[0]

    def body(os_ref, oa_ref, om_ref, g0, g1, g2, bg_ref, ws_ref, wa_ref, wm_ref, wo_ref, x_ref, xo_ref, mg_ref):
        outs, gates = _branches_and_gates(os_ref, oa_ref, om_ref, (g0, g1, g2), bg_ref, ws_ref, wa_ref, wm_ref)
        merged = (gates[0] * outs[0] + gates[1] * outs[1] + gates[2] * outs[2]).astype(BF16)
        mg_ref[...] = merged
        xo_ref[...] = x_ref[...] + _dot(merged, wo_ref[...])

    row = lambda i: (i, 0)
    return pl.pallas_call(
        body, name=name, grid=(T // tm,),
        in_specs=_merge_specs(tm) + [pl.BlockSpec((tm, D_MODEL), row)],
        out_specs=[pl.BlockSpec((tm, D_MODEL), row), pl.BlockSpec((tm, D_MODEL), row)],
        out_shape=[jax.ShapeDtypeStruct((T, D_MODEL), F32), jax.ShapeDtypeStruct((T, D_MODEL), BF16)],
        compiler_params=_params(("parallel",)),
    )(o_ssm, o_attn, o_mem, proj, proj, proj, bg, ws, wa, wm, wo, x)


def _merge_bwd(dx, o_ssm, o_attn, o_mem, proj, bg, ws, wa, wm, wo, tm, name):
    T = dx.shape[0]

    def body(os_ref, oa_ref, om_ref, g0, g1, g2, bg_ref, ws_ref, wa_ref, wm_ref, wo_ref, dx_ref,
             dgl_ref, db_ref, dos_ref, doa_ref, dom_ref, dbg_ref):
        @pl.when(pl.program_id(0) == 0)
        def _():
            dbg_ref[...] = jnp.zeros_like(dbg_ref)

        outs, gates = _branches_and_gates(os_ref, oa_ref, om_ref, (g0, g1, g2), bg_ref, ws_ref, wa_ref, wm_ref)
        dm = _dot(dx_ref[...].astype(BF16), wo_ref[...], NT)
        w_refs = (ws_ref, wa_ref, wm_ref)
        do_refs = (dos_ref, doa_ref, dom_ref)
        for k in range(3):
            cols = slice(D_MODEL * k, D_MODEL * (k + 1))
            dgl = dm * outs[k] * (gates[k] * (1.0 - gates[k]))
            dgl_ref[:, cols] = dgl.astype(BF16)
            dbg_ref[:, cols] += jnp.sum(dgl, axis=0, keepdims=True)
            dbk = (dm * gates[k]).astype(BF16)
            db_ref[:, cols] = dbk
            do_refs[k][...] = _dot(dbk, w_refs[k][...], NT)

    row = lambda i: (i, 0)
    return pl.pallas_call(
        body, name=name, grid=(T // tm,),
        in_specs=_merge_specs(tm) + [pl.BlockSpec((tm, D_MODEL), row)],
        out_specs=[pl.BlockSpec((tm, N_GATES), row), pl.BlockSpec((tm, N_GATES), row), pl.BlockSpec((tm, D_SSM), row),
                   pl.BlockSpec((tm, D_ATTN), row), pl.BlockSpec((tm, D_MEM), row), _full((1, N_GATES))],
        out_shape=[jax.ShapeDtypeStruct((T, N_GATES), BF16), jax.ShapeDtypeStruct((T, N_GATES), BF16),
                   jax.ShapeDtypeStruct((T, D_SSM), F32), jax.ShapeDtypeStruct((T, D_ATTN), F32),
                   jax.ShapeDtypeStruct((T, D_MEM), F32), jax.ShapeDtypeStruct((1, N_GATES), F32)],
        compiler_params=_params(("arbitrary",)),
    )(o_ssm, o_attn, o_mem, proj, proj, proj, bg, ws, wa, wm, wo, dx)


def _loss_head(x, g, target, tm, name):
    T, D = x.shape

    def body(x_ref, g_ref, t_ref, loss_ref, dx_ref, dg_ref):
        @pl.when(pl.program_id(0) == 0)
        def _():
            loss_ref[...] = jnp.zeros_like(loss_ref)
            dg_ref[...] = jnp.zeros_like(dg_ref)

        xv = x_ref[...]
        r = lax.rsqrt(jnp.mean(xv * xv, axis=-1, keepdims=True) + EPS)
        xr = xv * r
        err = xr * g_ref[...] - t_ref[...]
        loss_ref[...] += 0.5 * jnp.sum(jnp.mean(err * err, axis=-1, keepdims=True), axis=0, keepdims=True)
        dy = err * (1.0 / D)
        dg_ref[...] += jnp.sum(dy * xr, axis=0, keepdims=True)
        wv = dy * g_ref[...]
        dx_ref[...] = r * (wv - xr * jnp.mean(wv * xr, axis=-1, keepdims=True))

    row = lambda i: (i, 0)
    return pl.pallas_call(
        body, name=name, grid=(T // tm,),
        in_specs=[pl.BlockSpec((tm, D), row), _full((1, D)), pl.BlockSpec((tm, D), row)],
        out_specs=[_full((1, 128)), pl.BlockSpec((tm, D), row), _full((1, D))],
        out_shape=[jax.ShapeDtypeStruct((1, 128), F32), jax.ShapeDtypeStruct((T, D), F32),
                   jax.ShapeDtypeStruct((1, D), F32)],
        compiler_params=_params(("arbitrary",)),
    )(x, g, target)


def _adamw(parts, w, m, v, tr, name):
    L, R, C = w.shape

    def body(p_ref, w_ref, m_ref, v_ref, g_ref, d_ref, mo_ref, vo_ref):
        g = p_ref[0].astype(F32)
        for s in range(1, N_DEV):
            g = g + p_ref[s].astype(F32)
        mn = ADAM_B1 * m_ref[...] + (1.0 - ADAM_B1) * g
        vn = ADAM_B2 * v_ref[...] + (1.0 - ADAM_B2) * (g * g)
        m_hat = mn / (1.0 - ADAM_B1 ** ADAM_STEP)
        v_hat = vn / (1.0 - ADAM_B2 ** ADAM_STEP)
        g_ref[...] = g
        d_ref[...] = -ADAM_LR * (m_hat / (jnp.sqrt(v_hat) + ADAM_EPS) + ADAM_WD * w_ref[...])
        mo_ref[...] = mn
        vo_ref[...] = vn

    one = pl.BlockSpec((None, tr, C), lambda l, i: (l, i, 0))
    return pl.pallas_call(
        body, name=name, grid=(L, R // tr),
        in_specs=[pl.BlockSpec((N_DEV, None, tr, C), lambda l, i: (0, l, i, 0)), one, one, one],
        out_specs=[one] * 4,
        out_shape=[jax.ShapeDtypeStruct((L, R, C), F32)] * 4,
        compiler_params=_params(("parallel", "parallel")),
    )(parts, w, m, v)


_SHARDED = (("w_in", (1024, 1088), 1), ("w_glu", (96, 768), 0), ("w_mem_kv", (128, 1024), 0),
            ("w_br_ssm", (768, 128), 1), ("w_br_attn", (768, 128), 1), ("w_br_mem", (512, 128), 1),
            ("w_out", (128, 1024), 0))
_N_COPIES = DEPTH * len(_SHARDED)


def _peers():
    x, y, c = lax.axis_index("x"), lax.axis_index("y"), lax.axis_index("c")
    me = 4 * x + 2 * y + c
    out = []
    for k in range(1, N_DEV):
        px = 1 - x if k & 4 else x
        py = 1 - y if k & 2 else y
        pc = 1 - c if k & 1 else c
        out.append(((px, py, pc), 4 * px + 2 * py + pc))
    return me, out


def _run_exchange(pairs, send_sems, recv_sems, local_sems):
    me, peers = _peers()
    local = [pltpu.make_async_copy(src(me), dst(me), local_sems.at[j]) for j, (src, dst) in enumerate(pairs)]
    for cp in local:
        cp.start()
    sends = []
    for k, (peer, lin) in enumerate(peers):
        for j, (src, dst) in enumerate(pairs):
            cp = pltpu.make_async_remote_copy(src_ref=src(lin), dst_ref=dst(me), send_sem=send_sems.at[j, k],
                                              recv_sem=recv_sems.at[j, k], device_id=peer,
                                              device_id_type=pl.DeviceIdType.MESH)
            cp.start()
            sends.append(cp)
    for k, (peer, lin) in enumerate(peers):
        for j, (src, dst) in enumerate(pairs):
            pltpu.make_async_remote_copy(src_ref=src(lin), dst_ref=dst(lin), send_sem=send_sems.at[j, k],
                                         recv_sem=recv_sems.at[j, k], device_id=peer,
                                         device_id_type=pl.DeviceIdType.MESH).wait_recv()
    for cp in sends:
        cp.wait_send()
    for cp in local:
        cp.wait()


def _exchange_call(body, n_copies, ins, out_shape, name):
    any_spec = pl.BlockSpec(memory_space=pl.ANY)
    return pl.pallas_call(
        body, name=name,
        in_specs=[any_spec] * len(ins), out_specs=[any_spec] * len(out_shape), out_shape=out_shape,
        scratch_shapes=[pltpu.SemaphoreType.DMA((n_copies, N_DEV - 1)), pltpu.SemaphoreType.DMA((n_copies, N_DEV - 1)),
                        pltpu.SemaphoreType.DMA((n_copies,))],
    )(*ins)


def _gather_weights(shards):
    n_w = len(_SHARDED)

    def body(*refs):
        ins, outs = refs[:n_w], refs[n_w:2 * n_w]
        pairs = []
        for i, (n, s, axis) in enumerate(_SHARDED):
            for l in range(DEPTH):
                if n == "w_in":
                    dst = lambda who, i=i, l=l: outs[i].at[who, l]
                elif axis == 0:
                    dst = lambda who, i=i, l=l: outs[i].at[l, who]
                else:
                    dst = lambda who, i=i, l=l: outs[i].at[l, :, pl.ds(pl.multiple_of(who * LANES, LANES), LANES)]
                pairs.append((lambda who, i=i, l=l: ins[i].at[l], dst))
        _run_exchange(pairs, *refs[2 * n_w:])

    out_shape = []
    for n, s, axis in _SHARDED:
        if n == "w_in":
            out_shape.append(jax.ShapeDtypeStruct((N_DEV, DEPTH) + s, BF16))
        elif axis == 0:
            out_shape.append(jax.ShapeDtypeStruct((DEPTH, N_DEV) + s, BF16))
        else:
            out_shape.append(jax.ShapeDtypeStruct((DEPTH, s[0], N_DEV * s[1]), BF16))
    return _exchange_call(body, _N_COPIES, shards, out_shape, "gather_weights")


def _scatter_grads(grads):
    n_w = len(_SHARDED)

    def body(*refs):
        ins, outs = refs[:DEPTH * n_w], refs[DEPTH * n_w:(DEPTH + 1) * n_w]
        pairs = []
        for l in range(DEPTH):
            for i, (n, s, axis) in enumerate(_SHARDED):
                ref = ins[l * n_w + i]
                if n == "w_in":
                    src = lambda who, ref=ref: ref.at[who]
                elif axis == 0:
                    src = lambda who, ref=ref, s=s: ref.at[pl.ds(pl.multiple_of(who * s[0], 16), s[0])]
                else:
                    src = lambda who, ref=ref: ref.at[:, pl.ds(pl.multiple_of(who * LANES, LANES), LANES)]
                pairs.append((src, lambda who, i=i, l=l: outs[i].at[who, l]))
        _run_exchange(pairs, *refs[(DEPTH + 1) * n_w:])

    out_shape = [jax.ShapeDtypeStruct((N_DEV, DEPTH) + s, BF16) for _, s, _ in _SHARDED]
    return _exchange_call(body, _N_COPIES, [g for gl in grads for g in gl], out_shape, "scatter_grads")


def _gather_rows(x, name):
    def body(x_ref, o_ref, *sems):
        _run_exchange([(lambda who: x_ref, lambda who: o_ref.at[who])], *sems)

    return _exchange_call(body, 1, [x], [jax.ShapeDtypeStruct((N_DEV,) + x.shape, x.dtype)], name)[0]


_REPLICATED = (("norm_g", (2, 1024)), ("mem_norm_g", (2, 1024)), ("b_gate", (2, 3072)),
               ("ssm_lambda_re", (2, 48, 64)), ("ssm_lambda_im", (2, 48, 64)), ("ssm_log_dt", (2, 48)),
               ("ssm_b_re", (2, 48, 64, 16)), ("ssm_b_im", (2, 48, 64, 16)), ("ssm_c_re", (2, 48, 16, 64)),
               ("ssm_c_im", (2, 48, 16, 64)), ("ssm_d", (2, 768)), ("b_glu", (2, 768)), ("rel_bias", (32, 12)),
               ("final_norm_g", (1024,)))
_REP_SIZE = sum(int(np.prod(s)) for _, s in _REPLICATED)
_REP_TILE_ROWS = 512
_REP_ROWS = -(-_REP_SIZE // (LANES * _REP_TILE_ROWS)) * _REP_TILE_ROWS


def _pack_replicated(tree):
    flat = jnp.concatenate([tree[n].reshape(-1) for n, _ in _REPLICATED])
    return jnp.pad(flat, (0, _REP_ROWS * LANES - _REP_SIZE)).reshape(1, _REP_ROWS, LANES)


def _unpack_replicated(packed):
    flat = packed.reshape(-1)
    out, r = {}, 0
    for n, s in _REPLICATED:
        size = int(np.prod(s))
        out[n] = flat[r:r + size].reshape(s)
        r += size
    return out


def _full_w_in(gathered, layer):
    w = gathered[:, layer].transpose(1, 0, 2).reshape(D_MODEL, D_IN)
    return jnp.concatenate([w[:, st:st + wd] for _, wd, st in _SEGS], axis=1)


def _owner_rows_w_in(dw):
    order = sorted(_SEGS, key=lambda t: t[2])
    dw = jnp.concatenate([dw[:, _OFF[n]:_OFF[n] + wd] for n, wd, _ in order], axis=1)
    return dw.reshape(D_MODEL, N_DEV, D_IN // N_DEV).transpose(1, 0, 2)


def _discretize(lam_re, lam_im, log_dt, b_re, b_im):
    dt = jnp.exp(log_dt)[:, None]
    mag = jnp.exp(lam_re * dt)
    abar_re, abar_im = mag * jnp.cos(lam_im * dt), mag * jnp.sin(lam_im * dt)
    den = lam_re * lam_re + lam_im * lam_im
    nr, ni = abar_re - 1.0, abar_im
    f_re = (nr * lam_re + ni * lam_im) / den
    f_im = (ni * lam_re - nr * lam_im) / den
    bbar_re = f_re[..., None] * b_re - f_im[..., None] * b_im
    bbar_im = f_re[..., None] * b_im + f_im[..., None] * b_re
    return abar_re, abar_im, bbar_re, bbar_im


def _block_diag(a):
    _, R, C = a.shape
    a = a.reshape(SSM_BLOCKS, 8, R, C)
    eye = jnp.eye(8, dtype=a.dtype)
    return (a[:, :, :, None, :] * eye[None, :, None, :, None]).reshape(SSM_BLOCKS, 8 * R, 8 * C)


def _diag_blocks(a, R, C):
    a = a.reshape(SSM_BLOCKS, 8, R, 8, C)
    eye = jnp.eye(8, dtype=a.dtype)
    return jnp.sum(a * eye[None, :, None, :, None], axis=3).reshape(SSM_GROUPS, R, C)


def _layer_fwd(x, mem, W, P, bias, layer):
    tag = f"l{layer}"
    abar_re, abar_im, bbar_re, bbar_im = _discretize(P["ssm_lambda_re"][layer], P["ssm_lambda_im"][layer],
                                                     P["ssm_log_dt"][layer], P["ssm_b_re"][layer], P["ssm_b_im"][layer])
    c_re, c_im = P["ssm_c_re"][layer], P["ssm_c_im"][layer]
    ssm = dict(
        are=abar_re.reshape(1, N_STATE), aim=abar_im.reshape(1, N_STATE),
        bre=_block_diag(bbar_re.transpose(0, 2, 1)).astype(BF16), bim=_block_diag(bbar_im.transpose(0, 2, 1)).astype(BF16),
        cre=_block_diag(c_re.transpose(0, 2, 1)).astype(BF16), cimn=_block_diag(-c_im.transpose(0, 2, 1)).astype(BF16),
        ctre=_block_diag(c_re).astype(BF16), ctimn=_block_diag(-c_im).astype(BF16),
        btre=_block_diag(bbar_re).astype(BF16), btim=_block_diag(bbar_im).astype(BF16),
        d=P["ssm_d"][layer].reshape(1, D_SSM))
    bglu = P["b_glu"][layer].reshape(1, D_SSM)
    bgate = P["b_gate"][layer].reshape(1, N_GATES)
    g = P["norm_g"][layer].reshape(1, D_MODEL)
    gm = P["mem_norm_g"][layer].reshape(1, D_MODEL)

    T = x.shape[0]
    proj, h = _norm_proj(x, g, W["w_in"], min(T, 1024), 2176, f"{tag}_proj")
    xr, xi, y, o_ssm = _ssm_fwd(proj, ssm["bre"], ssm["bim"], ssm["cre"], ssm["cimn"], ssm["are"], ssm["aim"],
                                ssm["d"], W["w_glu"], bglu, 256, f"{tag}_ssm")
    os, lses = [], []
    for grp in range(3):
        o_g, lse_g = _attn_fwd(proj, bias[grp], grp, f"{tag}_attn{grp}")
        os.append(o_g)
        lses.append(lse_g)
    o_attn = _attn_mix(os, lses, proj, 512, f"{tag}_attn_mix")
    kvb, hm = _norm_proj(mem, gm, W["w_mem_kv"], mem.shape[0], 1024, f"{tag}_mem_kv", out_dtype=BF16)
    o_mem = _mem_fwd(proj, kvb, 512, f"{tag}_mem")
    x_out, merged = _merge_fwd(x, o_ssm, o_attn, o_mem, proj, bgate, W["w_br_ssm"], W["w_br_attn"], W["w_br_mem"],
                               W["w_out"], 256, f"{tag}_merge")
    res = dict(x=x, mem=mem, proj=proj, h=h, xr=xr, xi=xi, y=y, o_ssm=o_ssm, os=os, lses=lses,
               o_attn=o_attn, kvb=kvb, hm=hm, o_mem=o_mem, merged=merged, ssm=ssm, bglu=bglu,
               bgate=bgate, g=g, gm=gm)
    return x_out, res


def _layer_bwd(dx, res, W, P, bias, layer):
    tag = f"l{layer}b"
    proj, ssm = res["proj"], res["ssm"]
    T = dx.shape[0]
    dgl, dbr, do_ssm, do_attn, do_mem, dbg = _merge_bwd(dx, res["o_ssm"], res["o_attn"], res["o_mem"], proj,
                                                        res["bgate"], W["w_br_ssm"], W["w_br_attn"], W["w_br_mem"],
                                                        W["w_out"], 256, f"{tag}_merge")
    gw = {}
    gw["w_out"] = _mm_tn(res["merged"], dx, 1024, 1024, 512, f"{tag}_dw_out")
    gw["w_br_ssm"] = _mm_tn(res["o_ssm"], dbr, 768, 1024, 512, f"{tag}_dw_br_ssm", b_col=0, n=1024)
    gw["w_br_attn"] = _mm_tn(res["o_attn"], dbr, 768, 1024, 512, f"{tag}_dw_br_attn", b_col=1024, n=1024)
    gw["w_br_mem"] = _mm_tn(res["o_mem"], dbr, 512, 1024, 512, f"{tag}_dw_br_mem", b_col=2048, n=1024)

    dqm, dzm, dkv = _mem_bwd(do_mem, proj, res["kvb"], 512, f"{tag}_mem")
    M = dkv.shape[0]
    gw["w_mem_kv"] = _mm_tn(res["hm"], dkv, 1024, 1024, M, f"{tag}_dw_mem_kv")
    _, dgm = _proj_bwd(dkv.astype(BF16), W["w_mem_kv"], res["mem"], res["gm"], jnp.zeros_like(res["mem"]), M, 1024,
                       f"{tag}_mem_norm")

    do_g, corr, dza = _attn_mix_bwd(do_attn, res["os"], res["lses"], proj, 512, f"{tag}_attn_mix")
    dqs, dks, dvs, dbs = [], [], [], []
    for grp in range(3):
        dq_g, dk_g, dv_g, db_g = _attn_bwd(proj, do_g, corr, res["lses"][grp], bias[grp], grp, f"{tag}_attn{grp}")
        dqs.append(dq_g)
        dks.append(dk_g)
        dvs.append(dv_g)
        dbs.append(db_g)
    dbias = jnp.stack(dbs)

    dy, dzs, gelu_b, dt_b, dbglu = _glu_bwd(do_ssm, res["y"], proj, W["w_glu"], res["bglu"], 512, f"{tag}_glu")
    gw["w_glu"] = _mm_tn(gelu_b, dt_b, 768, 768, 512, f"{tag}_dw_glu")
    du, dbre, dbim, dcre, dcim, dare, daim, dd = _ssm_bwd(dy, proj, res["xr"], res["xi"], ssm["ctre"], ssm["ctimn"],
                                                          ssm["btre"], ssm["btim"], ssm["are"], ssm["aim"], ssm["d"],
                                                          256, f"{tag}_ssm")
    _, disc_vjp = jax.vjp(_discretize, P["ssm_lambda_re"][layer], P["ssm_lambda_im"][layer], P["ssm_log_dt"][layer],
                          P["ssm_b_re"][layer], P["ssm_b_im"][layer])
    d_lre, d_lim, d_ldt, d_bre, d_bim = disc_vjp((dare.reshape(SSM_GROUPS, SSM_STATE), daim.reshape(SSM_GROUPS, SSM_STATE),
                                                  _diag_blocks(dbre, SSM_STATE, SSM_GROUP),
                                                  _diag_blocks(dbim, SSM_STATE, SSM_GROUP)))

    dproj = jnp.concatenate([dgl, du, dzs] + dqs + dks + dvs + [dza, dqm, dzm], axis=1)
    gw["w_in"] = _owner_rows_w_in(_mm_tn(res["h"], dproj, 1024, 2176, 512, f"{tag}_dw_in"))
    dx_in, dg = _proj_bwd(dproj, W["w_in"], res["x"], res["g"], dx, min(T, 512), 2176, f"{tag}_proj")

    gp = dict(norm_g=dg[0], mem_norm_g=dgm[0], b_gate=dbg[0], ssm_lambda_re=d_lre, ssm_lambda_im=d_lim,
              ssm_log_dt=d_ldt, ssm_b_re=d_bre, ssm_b_im=d_bim,
              ssm_c_re=_diag_blocks(dcre, SSM_GROUP, SSM_STATE), ssm_c_im=_diag_blocks(dcim, SSM_GROUP, SSM_STATE),
              ssm_d=dd[0], b_glu=dbglu[0])
    return dx_in, [gw[n] for n, _, _ in _SHARDED], gp, dbias


def _local_step(x, mem, target, gathered, P):
    Ws = []
    for l in range(DEPTH):
        W = {}
        for (n, s, axis), a in zip(_SHARDED, gathered):
            if n == "w_in":
                W[n] = _full_w_in(a, l)
            elif axis == 0:
                W[n] = a[l].reshape(N_DEV * s[0], s[1])
            else:
                W[n] = a[l]
        Ws.append(W)
    buckets = _bucket_tables()
    bias = _bias_tables(P["rel_bias"], buckets, "bias_tables")
    res = []
    for l in range(DEPTH):
        x, r = _layer_fwd(x, mem, Ws[l], P, bias, l)
        res.append(r)
    loss, dx, dgf = _loss_head(x, P["final_norm_g"].reshape(1, D_MODEL), target, 512, "loss_head")
    gws, gps, dbiases = [None] * DEPTH, [None] * DEPTH, [None] * DEPTH
    for l in reversed(range(DEPTH)):
        dx, gws[l], gps[l], dbiases[l] = _layer_bwd(dx, res[l], Ws[l], P, bias, l)
    rep = {n: jnp.stack([gps[l][n] for l in range(DEPTH)]) for n in gps[0]}
    d_rel = _bias_grad(dbiases[0], dbiases[1], buckets, "bias_grad")
    rep["rel_bias"] = jnp.sum(d_rel, axis=-1).transpose(2, 0, 1).reshape(NUM_BUCKETS, 12)
    rep["final_norm_g"] = dgf[0]
    return loss[0, 0], dx, gws, rep


_WEIGHTS = ["norm_g", "mem_norm_g", "w_in", "b_gate", "ssm_lambda_re", "ssm_lambda_im", "ssm_log_dt", "ssm_b_re",
            "ssm_b_im", "ssm_c_re", "ssm_c_im", "ssm_d", "w_glu", "b_glu", "w_mem_kv", "w_br_ssm", "w_br_attn",
            "w_br_mem", "w_out", "rel_bias", "final_norm_g"]
_ADAM_ROWS = {"w_in": 128, "w_glu": 96, "w_mem_kv": 128, "w_br_ssm": 768, "w_br_attn": 768, "w_br_mem": 512,
              "w_out": 128}


def kernel(x, mem, norm_g, mem_norm_g, w_in, b_gate, ssm_lambda_re, ssm_lambda_im, ssm_log_dt, ssm_b_re, ssm_b_im, ssm_c_re, ssm_c_im, ssm_d, w_glu, b_glu, w_mem_kv, w_br_ssm, w_br_attn, w_br_mem, w_out, rel_bias, final_norm_g, loss_target, m_norm_g, m_mem_norm_g, m_w_in, m_b_gate, m_ssm_lambda_re, m_ssm_lambda_im, m_ssm_log_dt, m_ssm_b_re, m_ssm_b_im, m_ssm_c_re, m_ssm_c_im, m_ssm_d, m_w_glu, m_b_glu, m_w_mem_kv, m_w_br_ssm, m_w_br_attn, m_w_br_mem, m_w_out, m_rel_bias, m_final_norm_g, v_norm_g, v_mem_norm_g, v_w_in, v_b_gate, v_ssm_lambda_re, v_ssm_lambda_im, v_ssm_log_dt, v_ssm_b_re, v_ssm_b_im, v_ssm_c_re, v_ssm_c_im, v_ssm_d, v_w_glu, v_b_glu, v_w_mem_kv, v_w_br_ssm, v_w_br_attn, v_w_br_mem, v_w_out, v_rel_bias, v_final_norm_g):
    given = dict(locals())
    w = {n: given[n] for n in _WEIGHTS}
    m = {n: given["m_" + n] for n in _WEIGHTS}
    v = {n: given["v_" + n] for n in _WEIGHTS}

    gathered = _gather_weights([w[n].astype(BF16) for n, _, _ in _SHARDED])
    loss, dx, gws, rep = _local_step(x[0], mem[0], loss_target[0], gathered, w)
    loss = lax.psum(loss, ("x", "y", "c"))

    new = {}
    parts = _scatter_grads(gws)
    for (n, _, _), p in zip(_SHARDED, parts):
        new[n] = _adamw(p, w[n], m[n], v[n], _ADAM_ROWS[n], f"adamw_{n}")
    rparts = _gather_rows(_pack_replicated(rep), "gather_small_grads")
    rp = [_unpack_replicated(a) for a in _adamw(rparts, _pack_replicated(w), _pack_replicated(m), _pack_replicated(v),
                                                _REP_TILE_ROWS, "adamw_replicated")]
    for n, _ in _REPLICATED:
        new[n] = [rp[kind][n] for kind in range(4)]
    outs = [loss, dx[None]]
    for kind in range(4):
        outs.extend(new[n][kind] for n in _WEIGHTS)
    return tuple(outs)
```

```python
import functools
import math
from typing import Callable, NamedTuple

import jax
import jax.numpy as jnp
import numpy as np
from jax import lax
from jax.experimental import pallas as pl
from jax.experimental.pallas import tpu as pltpu

F32 = jnp.float32
BF16 = jnp.bfloat16

D_MODEL = 1024
DEPTH = 2
EPS = 1e-6
D_SSM = 768
SSM_GROUP = 16
SSM_GROUPS = 48
SSM_STATE = 64
N_STATE = SSM_GROUPS * SSM_STATE
SSM_BLOCKS = 6
D_ATTN = 768
ATTN_HEAD_DIM = 64
ATTN_GROUP_WIDTH = 256
ATTN_DILATIONS = (1, 4, 16)
ATTN_SPAN = 128
ATTN_BLOCK = 128
NUM_BUCKETS = 32
REL_MAX_DISTANCE = 2048
NEG_INF = -1e30
MEM_HEADS = 4
MEM_HEAD_DIM = 128
D_MEM = 512
N_GATES = 3 * D_MODEL
D_IN = 8704
N_DEV = 8
LANES = 128
ADAM_LR = 0.001
ADAM_B1 = 0.9
ADAM_B2 = 0.999
ADAM_EPS = 1e-08
ADAM_WD = 0.01
ADAM_STEP = 10

_SEGS = (("gates", 3072, 5632), ("u", 768, 0), ("z_ssm", 768, 768), ("q", 768, 1536), ("k", 768, 2304),
         ("v", 768, 3072), ("z_attn", 768, 3840), ("q_mem", 512, 4608), ("z_mem", 512, 5120))
_OFF = {}
_o = 0
for _n, _w, _s in _SEGS:
    _OFF[_n] = _o
    _o += _w

NN = (((1,), (0,)), ((), ()))
NT = (((1,), (1,)), ((), ()))
TN = (((0,), (0,)), ((), ()))

VMEM_LIMIT = 56 * 1024 * 1024


def _dot(a, b, dims=NN):
    return lax.dot_general(a, b, dims, preferred_element_type=F32)


def _sigmoid(x):
    return 1.0 / (1.0 + jnp.exp(-x))


def _gelu_parts(x):
    k = math.sqrt(2.0 / math.pi)
    t = jnp.tanh(k * (x + 0.044715 * (x * x * x)))
    cdf = 0.5 * (1.0 + t)
    dcdf = 0.5 * (1.0 - t * t) * k * (1.0 + 3.0 * 0.044715 * (x * x))
    return x * cdf, cdf + x * dcdf


def _params(sem, vmem=VMEM_LIMIT):
    return pltpu.CompilerParams(dimension_semantics=sem, vmem_limit_bytes=vmem)


def _full(shape):
    return pl.BlockSpec(shape, lambda *_: (0,) * len(shape))


def _norm_proj(x, g, w, tm, tn, name, out_dtype=F32, job=None):
    T, D = x.shape
    N = w.shape[1]

    def body(x_ref, g_ref, w_ref, o_ref, h_ref, hs):
        @pl.when(pl.program_id(1) == 0)
        def _():
            xv = x_ref[...]
            r = lax.rsqrt(jnp.mean(xv * xv, axis=-1, keepdims=True) + EPS)
            hv = (xv * r * g_ref[...]).astype(BF16)
            hs[...] = hv
            h_ref[...] = hv

        o_ref[...] = _dot(hs[...], w_ref[...]).astype(out_dtype)

    return _pc(
        body, job, name=name, grid=(T // tm, N // tn),
        in_specs=[pl.BlockSpec((tm, D), lambda i, j: (i, 0)), _full((1, D)),
                  pl.BlockSpec((D, tn), lambda i, j: (0, j))],
        out_specs=[pl.BlockSpec((tm, tn), lambda i, j: (i, j)), pl.BlockSpec((tm, D), lambda i, j: (i, 0))],
        out_shape=[jax.ShapeDtypeStruct((T, N), out_dtype), jax.ShapeDtypeStruct((T, D), BF16)],
        scratch_shapes=[pltpu.VMEM((tm, D), BF16)], sem=("parallel", "arbitrary"), operands=(x, g, w))


def _mm_tn(a, b, tm, tn, tk, name, b_col=0, n=None, job=None):
    K, M = a.shape
    N = b.shape[1] if n is None else n
    nk = K // tk
    j0 = b_col // tn

    def body(a_ref, b_ref, o_ref, acc):
        k = pl.program_id(2)

        @pl.when(k == 0)
        def _():
            acc[...] = jnp.zeros_like(acc)

        acc[...] += _dot(a_ref[...].astype(BF16), b_ref[...].astype(BF16), TN)

        @pl.when(k == nk - 1)
        def _():
            o_ref[...] = acc[...].astype(BF16)

    out = _pc(
        body, job, name=name, grid=(M // tm, N // tn, nk),
        in_specs=[pl.BlockSpec((tk, tm), lambda i, j, k: (k, i)), pl.BlockSpec((tk, tn), lambda i, j, k: (k, j0 + j))],
        out_specs=[pl.BlockSpec((tm, tn), lambda i, j, k: (i, j))],
        out_shape=[jax.ShapeDtypeStruct((M, N), BF16)],
        scratch_shapes=[pltpu.VMEM((tm, tn), F32)], sem=("parallel", "parallel", "arbitrary"), operands=(a, b))
    return out[0] if job is None else (out[0][0], out[1])


def _proj_bwd(dp, w, x, g, dres, tm, tk, name, job=None):
    T, N = dp.shape
    D = w.shape[0]
    nk = N // tk

    def body(dp_ref, w_ref, x_ref, g_ref, dres_ref, dx_ref, dg_ref, acc):
        i, k = pl.program_id(0), pl.program_id(1)

        @pl.when(k == 0)
        def _():
            acc[...] = jnp.zeros_like(acc)

        @pl.when((i == 0) & (k == 0))
        def _():
            dg_ref[...] = jnp.zeros_like(dg_ref)

        acc[...] += _dot(dp_ref[...], w_ref[...], NT)

        @pl.when(k == nk - 1)
        def _():
            xv = x_ref[...]
            dh = acc[...]
            r = lax.rsqrt(jnp.mean(xv * xv, axis=-1, keepdims=True) + EPS)
            xr = xv * r
            dg_ref[...] += jnp.sum(dh * xr, axis=0, keepdims=True)
            wv = dh * g_ref[...]
            dx_ref[...] = dres_ref[...] + r * (wv - xr * jnp.mean(wv * xr, axis=-1, keepdims=True))

    return _pc(
        body, job, name=name, grid=(T // tm, nk),
        in_specs=[pl.BlockSpec((tm, tk), lambda i, k: (i, k)), pl.BlockSpec((D, tk), lambda i, k: (0, k)),
                  pl.BlockSpec((tm, D), lambda i, k: (i, 0)), _full((1, D)),
                  pl.BlockSpec((tm, D), lambda i, k: (i, 0))],
        out_specs=[pl.BlockSpec((tm, D), lambda i, k: (i, 0)), _full((1, D))],
        out_shape=[jax.ShapeDtypeStruct((T, D), F32), jax.ShapeDtypeStruct((1, D), F32)],
        scratch_shapes=[pltpu.VMEM((tm, D), F32)], sem=("arbitrary", "arbitrary"), operands=(dp, w, x, g, dres))


def _ssm_fwd(proj, bre, bim, cre, cimn, are, aim, d, wglu, bglu, tc, name, job=None):
    T = proj.shape[0]
    ucol, zcol = _OFF["u"] // D_SSM, _OFF["z_ssm"] // D_SSM

    def body(u_ref, z_ref, bre_ref, bim_ref, cre_ref, cim_ref, are_ref, aim_ref, d_ref, wg_ref, bg_ref,
             xr_ref, xi_ref, y_ref, o_ref, car_r, car_i):
        @pl.when(pl.program_id(0) == 0)
        def _():
            car_r[...] = jnp.zeros_like(car_r)
            car_i[...] = jnp.zeros_like(car_i)

        u = u_ref[...]
        ub = u.astype(BF16)
        for k in range(SSM_BLOCKS):
            uk = ub[:, 128 * k:128 * (k + 1)]
            xr_ref[:, 512 * k:512 * (k + 1)] = _dot(uk, bre_ref[k])
            xi_ref[:, 512 * k:512 * (k + 1)] = _dot(uk, bim_ref[k])
        ar, ai = are_ref[...], aim_ref[...]

        def step(t, c):
            pr, pi = c
            nr = ar * pr - ai * pi + xr_ref[pl.ds(t, 1), :]
            ni = ar * pi + ai * pr + xi_ref[pl.ds(t, 1), :]
            xr_ref[pl.ds(t, 1), :] = nr
            xi_ref[pl.ds(t, 1), :] = ni
            return nr, ni

        pr, pi = lax.fori_loop(0, tc, step, (car_r[...], car_i[...]))
        car_r[...] = pr
        car_i[...] = pi

        ys = []
        for k in range(SSM_BLOCKS):
            xrk = xr_ref[:, 512 * k:512 * (k + 1)].astype(BF16)
            xik = xi_ref[:, 512 * k:512 * (k + 1)].astype(BF16)
            ys.append(_dot(xrk, cre_ref[k]) + _dot(xik, cim_ref[k]))
        y = jnp.concatenate(ys, axis=1) + d_ref[...] * u
        y_ref[...] = y
        gl, _ = _gelu_parts(y)
        t = _dot(gl.astype(BF16), wg_ref[...]) + bg_ref[...]
        z = z_ref[...]
        o_ref[...] = (gl * _sigmoid(t) * (z * _sigmoid(z))).astype(BF16)

    return _pc(
        body, job, name=name, grid=(T // tc,),
        in_specs=[pl.BlockSpec((tc, D_SSM), lambda i: (i, ucol)), pl.BlockSpec((tc, D_SSM), lambda i: (i, zcol)),
                  _full((SSM_BLOCKS, 128, 512)), _full((SSM_BLOCKS, 128, 512)),
                  _full((SSM_BLOCKS, 512, 128)), _full((SSM_BLOCKS, 512, 128)),
                  _full((1, N_STATE)), _full((1, N_STATE)), _full((1, D_SSM)),
                  _full((D_SSM, D_SSM)), _full((1, D_SSM))],
        out_specs=[pl.BlockSpec((tc, N_STATE), lambda i: (i, 0)), pl.BlockSpec((tc, N_STATE), lambda i: (i, 0)),
                   pl.BlockSpec((tc, D_SSM), lambda i: (i, 0)), pl.BlockSpec((tc, D_SSM), lambda i: (i, 0))],
        out_shape=[jax.ShapeDtypeStruct((T, N_STATE), F32), jax.ShapeDtypeStruct((T, N_STATE), F32),
                   jax.ShapeDtypeStruct((T, D_SSM), F32), jax.ShapeDtypeStruct((T, D_SSM), BF16)],
        scratch_shapes=[pltpu.VMEM((1, N_STATE), F32), pltpu.VMEM((1, N_STATE), F32)], sem=("arbitrary",),
        operands=(proj, proj, bre, bim, cre, cimn, are, aim, d, wglu, bglu))


def _glu_bwd(do, y, proj, wglu, bglu, tm, name):
    T = y.shape[0]
    zcol = _OFF["z_ssm"] // D_SSM

    def body(do_ref, y_ref, z_ref, wg_ref, bg_ref, dy_ref, dz_ref, g_ref, dt_ref, db_ref):
        @pl.when(pl.program_id(0) == 0)
        def _():
            db_ref[...] = jnp.zeros_like(db_ref)

        dov = do_ref[...]
        gl, dgl = _gelu_parts(y_ref[...])
        glb = gl.astype(BF16)
        sg = _sigmoid(_dot(glb, wg_ref[...]) + bg_ref[...])
        z = z_ref[...]
        sz = _sigmoid(z)
        dz_ref[...] = (dov * (gl * sg) * (sz * (1.0 + z * (1.0 - sz)))).astype(BF16)
        dy2 = dov * (z * sz)
        dt = dy2 * gl * (sg * (1.0 - sg))
        dtb = dt.astype(BF16)
        dg = dy2 * sg + _dot(dtb, wg_ref[...], NT)
        dy_ref[...] = dg * dgl
        g_ref[...] = glb
        dt_ref[...] = dtb
        db_ref[...] += jnp.sum(dt, axis=0, keepdims=True)

    row = lambda i: (i, 0)
    return pl.pallas_call(
        body, name=name, grid=(T // tm,),
        in_specs=[pl.BlockSpec((tm, D_SSM), row), pl.BlockSpec((tm, D_SSM), row),
                  pl.BlockSpec((tm, D_SSM), lambda i: (i, zcol)), _full((D_SSM, D_SSM)), _full((1, D_SSM))],
        out_specs=[pl.BlockSpec((tm, D_SSM), row)] * 4 + [_full((1, D_SSM))],
        out_shape=[jax.ShapeDtypeStruct((T, D_SSM), F32), jax.ShapeDtypeStruct((T, D_SSM), BF16),
                   jax.ShapeDtypeStruct((T, D_SSM), BF16), jax.ShapeDtypeStruct((T, D_SSM), BF16),
                   jax.ShapeDtypeStruct((1, D_SSM), F32)],
        compiler_params=_params(("arbitrary",)),
    )(do, y, proj, wglu, bglu)


def _ssm_bwd(dy, proj, xr, xi, ctre, ctimn, btre, btim, are, aim, d, tc, name, job=None):
    T = dy.shape[0]
    nc = T // tc
    ucol = _OFF["u"] // D_SSM
    rb = tc // 8

    def body(dy_ref, u_ref, xr_ref, xi_ref, xpr_ref, xpi_ref, ctre_ref, ctim_ref, btre_ref, btim_ref,
             are_ref, aim_ref, d_ref,
             du_ref, dbre_ref, dbim_ref, dcre_ref, dcim_ref, dare_ref, daim_ref, dd_ref, gr, gi, car_r, car_i):
        i = pl.program_id(0)

        @pl.when(i == 0)
        def _():
            for ref in (car_r, car_i, dbre_ref, dbim_ref, dcre_ref, dcim_ref, dare_ref, daim_ref, dd_ref):
                ref[...] = jnp.zeros_like(ref)

        dyv = dy_ref[...]
        dyb = dyv.astype(BF16)
        u = u_ref[...]
        ub = u.astype(BF16)
        for k in range(SSM_BLOCKS):
            dk = dyb[:, 128 * k:128 * (k + 1)]
            gr[:, 512 * k:512 * (k + 1)] = _dot(dk, ctre_ref[k])
            gi[:, 512 * k:512 * (k + 1)] = _dot(dk, ctim_ref[k])
        ar, ai = are_ref[...], aim_ref[...]

        def step(s, c):
            pr, pi = c
            t = tc - 1 - s
            nr = gr[pl.ds(t, 1), :] + ar * pr + ai * pi
            ni = gi[pl.ds(t, 1), :] + ar * pi - ai * pr
            gr[pl.ds(t, 1), :] = nr
            gi[pl.ds(t, 1), :] = ni
            return nr, ni

        pr, pi = lax.fori_loop(0, tc, step, (car_r[...], car_i[...]))
        car_r[...] = pr
        car_i[...] = pi

        xrv, xiv = xr_ref[...], xi_ref[...]
        keep = jnp.where(i == nc - 1, 0.0, 1.0)
        row0 = lax.broadcasted_iota(jnp.int32, (tc, 1), 0) == 0
        xsr = jnp.where(row0, xpr_ref[7:8, :] * keep, pltpu.roll(xrv, 1, axis=0))
        xsi = jnp.where(row0, xpi_ref[7:8, :] * keep, pltpu.roll(xiv, 1, axis=0))
        grv, giv = gr[...], gi[...]
        dare_ref[...] += jnp.sum(grv * xsr + giv * xsi, axis=0, keepdims=True)
        daim_ref[...] += jnp.sum(giv * xsr - grv * xsi, axis=0, keepdims=True)
        dd_ref[...] += jnp.sum(dyv * u, axis=0, keepdims=True)

        dus = []
        for k in range(SSM_BLOCKS):
            sl = slice(512 * k, 512 * (k + 1))
            ch = slice(128 * k, 128 * (k + 1))
            grb, gib = grv[:, sl].astype(BF16), giv[:, sl].astype(BF16)
            dus.append(_dot(grb, btre_ref[k]) + _dot(gib, btim_ref[k]))
            dbre_ref[k] += _dot(grb, ub[:, ch], TN)
            dbim_ref[k] += _dot(gib, ub[:, ch], TN)
            dcre_ref[k] += _dot(dyb[:, ch], xrv[:, sl].astype(BF16), TN)
            dcim_ref[k] -= _dot(dyb[:, ch], xiv[:, sl].astype(BF16), TN)
        du_ref[...] = (jnp.concatenate(dus, axis=1) + d_ref[...] * dyv).astype(BF16)

    rev = lambda i: (nc - 1 - i, 0)
    prev = lambda i: (jnp.maximum((nc - 1 - i) * rb - 1, 0), 0)
    return _pc(
        body, job, name=name, grid=(nc,),
        in_specs=[pl.BlockSpec((tc, D_SSM), rev), pl.BlockSpec((tc, D_SSM), lambda i: (nc - 1 - i, ucol)),
                  pl.BlockSpec((tc, N_STATE), rev), pl.BlockSpec((tc, N_STATE), rev),
                  pl.BlockSpec((8, N_STATE), prev), pl.BlockSpec((8, N_STATE), prev),
                  _full((SSM_BLOCKS, 128, 512)), _full((SSM_BLOCKS, 128, 512)),
                  _full((SSM_BLOCKS, 512, 128)), _full((SSM_BLOCKS, 512, 128)),
                  _full((1, N_STATE)), _full((1, N_STATE)), _full((1, D_SSM))],
        out_specs=[pl.BlockSpec((tc, D_SSM), rev),
                   _full((SSM_BLOCKS, 512, 128)), _full((SSM_BLOCKS, 512, 128)),
                   _full((SSM_BLOCKS, 128, 512)), _full((SSM_BLOCKS, 128, 512)),
                   _full((1, N_STATE)), _full((1, N_STATE)), _full((1, D_SSM))],
        out_shape=[jax.ShapeDtypeStruct((T, D_SSM), BF16),
                   jax.ShapeDtypeStruct((SSM_BLOCKS, 512, 128), F32), jax.ShapeDtypeStruct((SSM_BLOCKS, 512, 128), F32),
                   jax.ShapeDtypeStruct((SSM_BLOCKS, 128, 512), F32), jax.ShapeDtypeStruct((SSM_BLOCKS, 128, 512), F32),
                   jax.ShapeDtypeStruct((1, N_STATE), F32), jax.ShapeDtypeStruct((1, N_STATE), F32),
                   jax.ShapeDtypeStruct((1, D_SSM), F32)],
        scratch_shapes=[pltpu.VMEM((tc, N_STATE), F32), pltpu.VMEM((tc, N_STATE), F32),
                        pltpu.VMEM((1, N_STATE), F32), pltpu.VMEM((1, N_STATE), F32)], sem=("arbitrary",),
        operands=(dy, proj, xr, xi, xr, xi, ctre, ctimn, btre, btim, are, aim, d))


def _rel_bucket(dist):
    n = jnp.maximum(dist, 0)
    max_exact = NUM_BUCKETS // 2
    n_f = jnp.maximum(n, 1).astype(F32)
    large = max_exact + (jnp.log(n_f / max_exact) / math.log(REL_MAX_DISTANCE / max_exact)
                         * (NUM_BUCKETS - max_exact)).astype(jnp.int32)
    large = jnp.minimum(large, NUM_BUCKETS - 1)
    return jnp.where(n < max_exact, n, large)


def _bucket_tables():
    qi = jnp.arange(ATTN_BLOCK)[:, None]
    kj = jnp.arange(2 * ATTN_BLOCK)[None, :]
    delta = jnp.maximum(ATTN_BLOCK + qi - kj, 0)
    return jnp.stack([_rel_bucket(delta * r) for r in ATTN_DILATIONS]).astype(jnp.int32)


def _bias_tables(rel_bias, buckets, name):
    def body(tab_ref, bk_ref, o_ref):
        g = pl.program_id(0)
        bk = bk_ref[...]
        qi = lax.broadcasted_iota(jnp.int32, bk.shape, 0)
        kj = lax.broadcasted_iota(jnp.int32, bk.shape, 1)
        delta = ATTN_BLOCK + qi - kj
        band = (delta >= 0) & (delta <= ATTN_SPAN)
        accs = [jnp.zeros(bk.shape, F32) for _ in range(4)]
        for b in range(NUM_BUCKETS):
            hit = bk == b
            for h in range(4):
                accs[h] = jnp.where(hit, tab_ref[b, 4 * g + h], accs[h])
        for h in range(4):
            o_ref[h] = jnp.where(band, accs[h], NEG_INF)

    return pl.pallas_call(
        body, name=name, grid=(3,),
        in_specs=[pl.BlockSpec(memory_space=pltpu.SMEM),
                  pl.BlockSpec((None, ATTN_BLOCK, 2 * ATTN_BLOCK), lambda g: (g, 0, 0))],
        out_specs=pl.BlockSpec((None, 4, ATTN_BLOCK, 2 * ATTN_BLOCK), lambda g: (g, 0, 0, 0)),
        out_shape=jax.ShapeDtypeStruct((3, 4, ATTN_BLOCK, 2 * ATTN_BLOCK), F32),
        compiler_params=_params(("parallel",)),
    )(rel_bias, buckets)


def _bias_grad(db0, db1, buckets, name):
    def body(a_ref, b_ref, bk_ref, o_ref):
        bk = bk_ref[...]
        for h in range(4):
            dv = a_ref[h] + b_ref[h]
            for b in range(NUM_BUCKETS):
                o_ref[h, b:b + 1, :] = jnp.sum(jnp.where(bk == b, dv, 0.0), axis=0, keepdims=True)

    tab = pl.BlockSpec((None, 4, ATTN_BLOCK, 2 * ATTN_BLOCK), lambda g: (g, 0, 0, 0))
    return pl.pallas_call(
        body, name=name, grid=(3,),
        in_specs=[tab, tab, pl.BlockSpec((None, ATTN_BLOCK, 2 * ATTN_BLOCK), lambda g: (g, 0, 0))],
        out_specs=pl.BlockSpec((None, 4, NUM_BUCKETS, 2 * ATTN_BLOCK), lambda g: (g, 0, 0, 0)),
        out_shape=jax.ShapeDtypeStruct((3, 4, NUM_BUCKETS, 2 * ATTN_BLOCK), F32),
        compiler_params=_params(("parallel",)),
    )(db0, db1, buckets)


def _residue_rows(s, r):
    return pl.ds(s, ATTN_BLOCK, stride=r) if r > 1 else pl.ds(0, ATTN_BLOCK)


def _over_residues(r, fn):
    if r == 1:
        fn(0)
    else:
        lax.fori_loop(0, r, lambda s, c: (fn(s), c)[1], 0)


def _attn_cols(g):
    return tuple((_OFF[n] + ATTN_GROUP_WIDTH * g) // LANES for n in ("q", "k", "v"))


def _attn_fwd(proj, bias, g, name, job=None):
    r = ATTN_DILATIONS[g]
    T = proj.shape[0]
    tb = ATTN_BLOCK * r
    qc, kc, vc = _attn_cols(g)
    scale = ATTN_HEAD_DIM ** -0.5

    def body(q_ref, kc_ref, kp_ref, vc_ref, vp_ref, bias_ref, o_ref, lse_ref):
        lane = lax.broadcasted_iota(jnp.int32, (ATTN_BLOCK, LANES), 1)
        kj = lax.broadcasted_iota(jnp.int32, (ATTN_BLOCK, 2 * ATTN_BLOCK), 1)
        dead = (pl.program_id(0) == 0) & (kj < ATTN_BLOCK)

        def one(s):
            rows = _residue_rows(s, r)
            q = q_ref[rows, :]
            kcat = jnp.concatenate([kp_ref[rows, :], kc_ref[rows, :]], axis=0).astype(BF16)
            vcat = jnp.concatenate([vp_ref[rows, :], vc_ref[rows, :]], axis=0).astype(BF16)
            o_acc = jnp.zeros((ATTN_BLOCK, LANES), F32)
            l_acc = jnp.zeros((ATTN_BLOCK, LANES), F32)
            for hh in range(2):
                mine = (lane >= ATTN_HEAD_DIM) if hh else (lane < ATTN_HEAD_DIM)
                qm = jnp.where(mine, q, 0.0).astype(BF16)
                sc = jnp.where(dead, NEG_INF, _dot(qm, kcat, NT) * scale + bias_ref[hh])
                m = jnp.max(sc, axis=-1, keepdims=True)
                p = jnp.exp(sc - m)
                l = jnp.sum(p, axis=-1, keepdims=True)
                o_acc = jnp.where(mine, _dot((p / l).astype(BF16), vcat), o_acc)
                l_acc = jnp.where(mine, m + jnp.log(l), l_acc)
            o_ref[rows, :] = o_acc
            lse_ref[rows, :] = l_acc

        _over_residues(r, one)

    cur = lambda c: pl.BlockSpec((tb, LANES), lambda b, p: (b, c + p))
    prev = lambda c: pl.BlockSpec((tb, LANES), lambda b, p: (jnp.maximum(b - 1, 0), c + p))
    out = pl.BlockSpec((tb, LANES), lambda b, p: (b, p))
    return _pc(
        body, job, name=name, grid=(T // tb, 2),
        in_specs=[cur(qc), cur(kc), prev(kc), cur(vc), prev(vc),
                  pl.BlockSpec((2, ATTN_BLOCK, 2 * ATTN_BLOCK), lambda b, p: (p, 0, 0))],
        out_specs=[out, out],
        out_shape=[jax.ShapeDtypeStruct((T, ATTN_GROUP_WIDTH), F32), jax.ShapeDtypeStruct((T, ATTN_GROUP_WIDTH), F32)],
        scratch_shapes=[], sem=("parallel", "parallel"), operands=(proj, proj, proj, proj, proj, bias))


def _attn_bwd(proj, do, corr, lse, bias, g, name):
    r = ATTN_DILATIONS[g]
    T = proj.shape[0]
    tb = ATTN_BLOCK * r
    nb = T // tb
    qc, kc, vc = _attn_cols(g)
    dc = ATTN_GROUP_WIDTH * g // LANES
    scale = ATTN_HEAD_DIM ** -0.5

    def body(q_ref, kc_ref, kp_ref, vc_ref, vp_ref, do_ref, corr_ref, lse_ref, bias_ref,
             dq_ref, dk_ref, dv_ref, db_ref, dq_s, dkc_s, dkp_s, dvc_s, dvp_s, kacc, vacc):
        b = pl.program_id(1)

        @pl.when(b == 0)
        def _():
            db_ref[...] = jnp.zeros_like(db_ref)
            kacc[...] = jnp.zeros_like(kacc)
            vacc[...] = jnp.zeros_like(vacc)

        @pl.when(b == nb)
        def _():
            dk_ref[...] = kacc[...].astype(BF16)
            dv_ref[...] = vacc[...].astype(BF16)

        @pl.when(b < nb)
        def _():
            lane = lax.broadcasted_iota(jnp.int32, (ATTN_BLOCK, LANES), 1)
            kj = lax.broadcasted_iota(jnp.int32, (ATTN_BLOCK, 2 * ATTN_BLOCK), 1)
            dead = (b == 0) & (kj < ATTN_BLOCK)

            def one(s):
                rows = _residue_rows(s, r)
                q = q_ref[rows, :]
                kcat = jnp.concatenate([kp_ref[rows, :], kc_ref[rows, :]], axis=0).astype(BF16)
                vcat = jnp.concatenate([vp_ref[rows, :], vc_ref[rows, :]], axis=0).astype(BF16)
                dov, corrv, lsev = do_ref[rows, :], corr_ref[rows, :], lse_ref[rows, :]
                dq_acc = jnp.zeros((ATTN_BLOCK, LANES), F32)
                dk_acc = jnp.zeros((2 * ATTN_BLOCK, LANES), F32)
                dv_acc = jnp.zeros((2 * ATTN_BLOCK, LANES), F32)
                for hh in range(2):
                    mine = (lane >= ATTN_HEAD_DIM) if hh else (lane < ATTN_HEAD_DIM)
                    col = slice(ATTN_HEAD_DIM * hh, ATTN_HEAD_DIM * hh + 1)
                    qm = jnp.where(mine, q, 0.0).astype(BF16)
                    dom = jnp.where(mine, dov, 0.0).astype(BF16)
                    sc = jnp.where(dead, NEG_INF, _dot(qm, kcat, NT) * scale + bias_ref[hh])
                    p = jnp.exp(sc - lsev[:, col])
                    ds = p * (_dot(dom, vcat, NT) - corrv[:, col])
                    db_ref[hh] += ds
                    dsb = ds.astype(BF16)
                    dq_acc = jnp.where(mine, _dot(dsb, kcat) * scale, dq_acc)
                    dk_acc += _dot(dsb, qm, TN) * scale
                    dv_acc += _dot(p.astype(BF16), dom, TN)
                dq_s[rows, :] = dq_acc
                dkp_s[rows, :] = dk_acc[:ATTN_BLOCK]
                dkc_s[rows, :] = dk_acc[ATTN_BLOCK:]
                dvp_s[rows, :] = dv_acc[:ATTN_BLOCK]
                dvc_s[rows, :] = dv_acc[ATTN_BLOCK:]

            _over_residues(r, one)
            dq_ref[...] = dq_s[...].astype(BF16)
            dk_ref[...] = (kacc[...] + dkp_s[...]).astype(BF16)
            dv_ref[...] = (vacc[...] + dvp_s[...]).astype(BF16)
            kacc[...] = dkc_s[...]
            vacc[...] = dvc_s[...]

    last = nb - 1
    blk = (tb, LANES)
    cur = lambda c: pl.BlockSpec(blk, lambda p, b: (jnp.minimum(b, last), c + p))
    prev = lambda c: pl.BlockSpec(blk, lambda p, b: (jnp.clip(b - 1, 0, last), c + p))
    tab = pl.BlockSpec((2, ATTN_BLOCK, 2 * ATTN_BLOCK), lambda p, b: (p, 0, 0))
    return pl.pallas_call(
        body, name=name, grid=(2, nb + 1),
        in_specs=[cur(qc), cur(kc), prev(kc), cur(vc), prev(vc), cur(dc), cur(dc), cur(0), tab],
        out_specs=[cur(0), prev(0), prev(0), tab],
        out_shape=[jax.ShapeDtypeStruct((T, ATTN_GROUP_WIDTH), BF16)] * 3
        + [jax.ShapeDtypeStruct((4, ATTN_BLOCK, 2 * ATTN_BLOCK), F32)],
        scratch_shapes=[pltpu.VMEM(blk, F32)] * 7,
        compiler_params=_params(("arbitrary", "arbitrary")),
    )(proj, proj, proj, proj, proj, do, corr, lse, bias)


def _mix_weights(lses):
    m = jnp.maximum(jnp.maximum(lses[0], lses[1]), lses[2])
    es = [jnp.exp(l - m) for l in lses]
    inv = 1.0 / (es[0] + es[1] + es[2])
    return jnp.concatenate([e * inv for e in es], axis=1)


def _attn_mix(os, lses, proj, tm, name):
    T = proj.shape[0]
    zcol = _OFF["z_attn"] // D_ATTN

    def body(o0, o1, o2, l0, l1, l2, z_ref, out_ref):
        z = z_ref[...]
        o = jnp.concatenate([o0[...], o1[...], o2[...]], axis=1)
        alpha = _mix_weights([l0[...], l1[...], l2[...]])
        out_ref[...] = (o * alpha * (z * _sigmoid(z))).astype(BF16)

    row = lambda i: (i, 0)
    grp = pl.BlockSpec((tm, ATTN_GROUP_WIDTH), row)
    return pl.pallas_call(
        body, name=name, grid=(T // tm,),
        in_specs=[grp] * 6 + [pl.BlockSpec((tm, D_ATTN), lambda i: (i, zcol))],
        out_specs=pl.BlockSpec((tm, D_ATTN), row),
        out_shape=jax.ShapeDtypeStruct((T, D_ATTN), BF16),
        compiler_params=_params(("parallel",)),
    )(*os, *lses, proj)


def _attn_mix_bwd(d, os, lses, proj, tm, name):
    T = proj.shape[0]
    zcol = _OFF["z_attn"] // D_ATTN

    def body(d_ref, o0, o1, o2, l0, l1, l2, z_ref, do_ref, corr_ref, dz_ref):
        dv, z = d_ref[...], z_ref[...]
        ov = jnp.concatenate([o0[...], o1[...], o2[...]], axis=1)
        alpha = _mix_weights([l0[...], l1[...], l2[...]])
        sz = _sigmoid(z)
        oc = ov * alpha
        dz_ref[...] = (dv * oc * (sz * (1.0 + z * (1.0 - sz)))).astype(BF16)
        doc = dv * (z * sz)
        do_ref[...] = doc * alpha
        pr = doc * oc
        p3 = pr[:, 0:256] + pr[:, 256:512] + pr[:, 512:768]
        li = lax.broadcasted_iota(jnp.int32, (256, 256), 0) // ATTN_HEAD_DIM
        lj = lax.broadcasted_iota(jnp.int32, (256, 256), 1) // ATTN_HEAD_DIM
        ones = jnp.where(li == lj, 1.0, 0.0).astype(F32)
        s = lax.dot_general(p3, ones, NN, precision=lax.Precision.HIGHEST, preferred_element_type=F32)
        corr_ref[...] = alpha * jnp.concatenate([s, s, s], axis=1)

    row = lambda i: (i, 0)
    grp = pl.BlockSpec((tm, ATTN_GROUP_WIDTH), row)
    return pl.pallas_call(
        body, name=name, grid=(T // tm,),
        in_specs=[pl.BlockSpec((tm, D_ATTN), row)] + [grp] * 6 + [pl.BlockSpec((tm, D_ATTN), lambda i: (i, zcol))],
        out_specs=[pl.BlockSpec((tm, D_ATTN), row)] * 3,
        out_shape=[jax.ShapeDtypeStruct((T, D_ATTN), F32), jax.ShapeDtypeStruct((T, D_ATTN), F32),
                   jax.ShapeDtypeStruct((T, D_ATTN), BF16)],
        compiler_params=_params(("parallel",)),
    )(d, *os, *lses, proj)


def _mem_probs(q_ref, kv_ref, h):
    hs = slice(MEM_HEAD_DIM * h, MEM_HEAD_DIM * (h + 1))
    qh = q_ref[:, hs].astype(BF16)
    kh = kv_ref[:, hs]
    vh = kv_ref[:, D_MEM + MEM_HEAD_DIM * h:D_MEM + MEM_HEAD_DIM * (h + 1)]
    s = _dot(qh, kh, NT) * (MEM_HEAD_DIM ** -0.5)
    p = jnp.exp(s - jnp.max(s, axis=-1, keepdims=True))
    pn = p / jnp.sum(p, axis=-1, keepdims=True)
    return qh, kh, vh, pn


def _mem_fwd(proj, kv, tm, name):
    T = proj.shape[0]
    M = kv.shape[0]
    qcol, zcol = _OFF["q_mem"] // D_MEM, _OFF["z_mem"] // D_MEM

    def body(q_ref, z_ref, kv_ref, o_ref):
        outs = []
        for h in range(MEM_HEADS):
            _, _, vh, pn = _mem_probs(q_ref, kv_ref, h)
            outs.append(_dot(pn.astype(BF16), vh))
        z = z_ref[...]
        o_ref[...] = (jnp.concatenate(outs, axis=1) * (z * _sigmoid(z))).astype(BF16)

    return pl.pallas_call(
        body, name=name, grid=(T // tm,),
        in_specs=[pl.BlockSpec((tm, D_MEM), lambda i: (i, qcol)), pl.BlockSpec((tm, D_MEM), lambda i: (i, zcol)),
                  _full((M, 2 * D_MEM))],
        out_specs=pl.BlockSpec((tm, D_MEM), lambda i: (i, 0)),
        out_shape=jax.ShapeDtypeStruct((T, D_MEM), BF16),
        compiler_params=_params(("parallel",)),
    )(proj, proj, kv)


def _mem_bwd(d, proj, kv, tm, name):
    T = proj.shape[0]
    M = kv.shape[0]
    qcol, zcol = _OFF["q_mem"] // D_MEM, _OFF["z_mem"] // D_MEM

    def body(d_ref, q_ref, z_ref, kv_ref, dq_ref, dz_ref, dkv_ref):
        @pl.when(pl.program_id(0) == 0)
        def _():
            dkv_ref[...] = jnp.zeros_like(dkv_ref)

        z = z_ref[...]
        sz = _sigmoid(z)
        dv = d_ref[...]
        dov = dv * (z * sz)
        scale = MEM_HEAD_DIM ** -0.5
        outs, dqs = [], []
        for h in range(MEM_HEADS):
            hs = slice(MEM_HEAD_DIM * h, MEM_HEAD_DIM * (h + 1))
            qh, kh, vh, pn = _mem_probs(q_ref, kv_ref, h)
            pnb = pn.astype(BF16)
            oh = _dot(pnb, vh)
            outs.append(oh)
            doh = dov[:, hs]
            dohb = doh.astype(BF16)
            dp = _dot(dohb, vh, NT)
            ds = pn * (dp - jnp.sum(doh * oh, axis=-1, keepdims=True))
            dsb = ds.astype(BF16)
            dqs.append(_dot(dsb, kh) * scale)
            dkv_ref[:, hs] += _dot(dsb, qh, TN) * scale
            vs = slice(D_MEM + MEM_HEAD_DIM * h, D_MEM + MEM_HEAD_DIM * (h + 1))
            dkv_ref[:, vs] += _dot(pnb, dohb, TN)
        dq_ref[...] = jnp.concatenate(dqs, axis=1).astype(BF16)
        dz_ref[...] = (dv * jnp.concatenate(outs, axis=1) * (sz * (1.0 + z * (1.0 - sz)))).astype(BF16)

    row = lambda i: (i, 0)
    return pl.pallas_call(
        body, name=name, grid=(T // tm,),
        in_specs=[pl.BlockSpec((tm, D_MEM), row), pl.BlockSpec((tm, D_MEM), lambda i: (i, qcol)),
                  pl.BlockSpec((tm, D_MEM), lambda i: (i, zcol)), _full((M, 2 * D_MEM))],
        out_specs=[pl.BlockSpec((tm, D_MEM), row), pl.BlockSpec((tm, D_MEM), row), _full((M, 2 * D_MEM))],
        out_shape=[jax.ShapeDtypeStruct((T, D_MEM), BF16), jax.ShapeDtypeStruct((T, D_MEM), BF16),
                   jax.ShapeDtypeStruct((M, 2 * D_MEM), F32)],
        compiler_params=_params(("arbitrary",)),
    )(d, proj, proj, kv)


def _branches_and_gates(os_ref, oa_ref, om_ref, gl_refs, bg_ref, ws_ref, wa_ref, wm_ref):
    outs = (_dot(os_ref[...], ws_ref[...]), _dot(oa_ref[...], wa_ref[...]), _dot(om_ref[...], wm_ref[...]))
    gates = tuple(_sigmoid(gl_refs[k][...] + bg_ref[:, D_MODEL * k:D_MODEL * (k + 1)]) for k in range(3))
    return outs, gates


def _merge_specs(tm):
    row = lambda i: (i, 0)
    gate = [pl.BlockSpec((tm, D_MODEL), (lambda i, k=k: (i, k))) for k in range(3)]
    return ([pl.BlockSpec((tm, D_SSM), row), pl.BlockSpec((tm, D_ATTN), row), pl.BlockSpec((tm, D_MEM), row)] + gate
            + [_full((1, N_GATES)), _full((D_SSM, D_MODEL)), _full((D_ATTN, D_MODEL)), _full((D_MEM, D_MODEL)),
               _full((D_MODEL, D_MODEL))])


def _merge_fwd(x, o_ssm, o_attn, o_mem, proj, bg, ws, wa, wm, wo, tm, name):
    T = x.shape[0]

    def body(os_ref, oa_ref, om_ref, g0, g1, g2, bg_ref, ws_ref, wa_ref, wm_ref, wo_ref, x_ref, xo_ref, mg_ref):
        outs, gates = _branches_and_gates(os_ref, oa_ref, om_ref, (g0, g1, g2), bg_ref, ws_ref, wa_ref, wm_ref)
        merged = (gates[0] * outs[0] + gates[1] * outs[1] + gates[2] * outs[2]).astype(BF16)
        mg_ref[...] = merged
        xo_ref[...] = x_ref[...] + _dot(merged, wo_ref[...])

    row = lambda i: (i, 0)
    return pl.pallas_call(
        body, name=name, grid=(T // tm,),
        in_specs=_merge_specs(tm) + [pl.BlockSpec((tm, D_MODEL), row)],
        out_specs=[pl.BlockSpec((tm, D_MODEL), row), pl.BlockSpec((tm, D_MODEL), row)],
        out_shape=[jax.ShapeDtypeStruct((T, D_MODEL), F32), jax.ShapeDtypeStruct((T, D_MODEL), BF16)],
        compiler_params=_params(("parallel",)),
    )(o_ssm, o_attn, o_mem, proj, proj, proj, bg, ws, wa, wm, wo, x)


def _merge_bwd(dx, o_ssm, o_attn, o_mem, proj, bg, ws, wa, wm, wo, tm, name, job=None):
    T = dx.shape[0]

    def body(os_ref, oa_ref, om_ref, g0, g1, g2, bg_ref, ws_ref, wa_ref, wm_ref, wo_ref, dx_ref,
             dgl_ref, db_ref, dos_ref, doa_ref, dom_ref, dbg_ref):
        @pl.when(pl.program_id(0) == 0)
        def _():
            dbg_ref[...] = jnp.zeros_like(dbg_ref)

        outs, gates = _branches_and_gates(os_ref, oa_ref, om_ref, (g0, g1, g2), bg_ref, ws_ref, wa_ref, wm_ref)
        dm = _dot(dx_ref[...].astype(BF16), wo_ref[...], NT)
        w_refs = (ws_ref, wa_ref, wm_ref)
        do_refs = (dos_ref, doa_ref, dom_ref)
        for k in range(3):
            cols = slice(D_MODEL * k, D_MODEL * (k + 1))
            dgl = dm * outs[k] * (gates[k] * (1.0 - gates[k]))
            dgl_ref[:, cols] = dgl.astype(BF16)
            dbg_ref[:, cols] += jnp.sum(dgl, axis=0, keepdims=True)
            dbk = (dm * gates[k]).astype(BF16)
            db_ref[:, cols] = dbk
            do_refs[k][...] = _dot(dbk, w_refs[k][...], NT)

    row = lambda i: (i, 0)
    return _pc(
        body, job, name=name, grid=(T // tm,),
        in_specs=_merge_specs(tm) + [pl.BlockSpec((tm, D_MODEL), row)],
        out_specs=[pl.BlockSpec((tm, N_GATES), row), pl.BlockSpec((tm, N_GATES), row), pl.BlockSpec((tm, D_SSM), row),
                   pl.BlockSpec((tm, D_ATTN), row), pl.BlockSpec((tm, D_MEM), row), _full((1, N_GATES))],
        out_shape=[jax.ShapeDtypeStruct((T, N_GATES), BF16), jax.ShapeDtypeStruct((T, N_GATES), BF16),
                   jax.ShapeDtypeStruct((T, D_SSM), F32), jax.ShapeDtypeStruct((T, D_ATTN), F32),
                   jax.ShapeDtypeStruct((T, D_MEM), F32), jax.ShapeDtypeStruct((1, N_GATES), F32)],
        scratch_shapes=[], sem=("arbitrary",), operands=(o_ssm, o_attn, o_mem, proj, proj, proj, bg, ws, wa, wm, wo, dx))


def _loss_head(x, g, target, tm, name):
    T, D = x.shape

    def body(x_ref, g_ref, t_ref, loss_ref, dx_ref, dg_ref):
        @pl.when(pl.program_id(0) == 0)
        def _():
            loss_ref[...] = jnp.zeros_like(loss_ref)
            dg_ref[...] = jnp.zeros_like(dg_ref)

        xv = x_ref[...]
        r = lax.rsqrt(jnp.mean(xv * xv, axis=-1, keepdims=True) + EPS)
        xr = xv * r
        err = xr * g_ref[...] - t_ref[...]
        loss_ref[...] += 0.5 * jnp.sum(jnp.mean(err * err, axis=-1, keepdims=True), axis=0, keepdims=True)
        dy = err * (1.0 / D)
        dg_ref[...] += jnp.sum(dy * xr, axis=0, keepdims=True)
        wv = dy * g_ref[...]
        dx_ref[...] = r * (wv - xr * jnp.mean(wv * xr, axis=-1, keepdims=True))

    row = lambda i: (i, 0)
    return pl.pallas_call(
        body, name=name, grid=(T // tm,),
        in_specs=[pl.BlockSpec((tm, D), row), _full((1, D)), pl.BlockSpec((tm, D), row)],
        out_specs=[_full((1, 128)), pl.BlockSpec((tm, D), row), _full((1, D))],
        out_shape=[jax.ShapeDtypeStruct((1, 128), F32), jax.ShapeDtypeStruct((T, D), F32),
                   jax.ShapeDtypeStruct((1, D), F32)],
        compiler_params=_params(("arbitrary",)),
    )(x, g, target)


def _adamw(parts, w, m, v, tr, name):
    L, R, C = w.shape

    def body(p_ref, w_ref, m_ref, v_ref, g_ref, d_ref, mo_ref, vo_ref):
        g = p_ref[0].astype(F32)
        for s in range(1, N_DEV):
            g = g + p_ref[s].astype(F32)
        mn = ADAM_B1 * m_ref[...] + (1.0 - ADAM_B1) * g
        vn = ADAM_B2 * v_ref[...] + (1.0 - ADAM_B2) * (g * g)
        m_hat = mn / (1.0 - ADAM_B1 ** ADAM_STEP)
        v_hat = vn / (1.0 - ADAM_B2 ** ADAM_STEP)
        g_ref[...] = g
        d_ref[...] = -ADAM_LR * (m_hat / (jnp.sqrt(v_hat) + ADAM_EPS) + ADAM_WD * w_ref[...])
        mo_ref[...] = mn
        vo_ref[...] = vn

    one = pl.BlockSpec((None, tr, C), lambda l, i: (l, i, 0))
    return pl.pallas_call(
        body, name=name, grid=(L, R // tr),
        in_specs=[pl.BlockSpec((N_DEV, None, tr, C), lambda l, i: (0, l, i, 0)), one, one, one],
        out_specs=[one] * 4,
        out_shape=[jax.ShapeDtypeStruct((L, R, C), F32)] * 4,
        compiler_params=_params(("parallel", "parallel")),
    )(parts, w, m, v)


_SHARDED = (("w_in", (1024, 1088), 1), ("w_glu", (96, 768), 0), ("w_mem_kv", (128, 1024), 0),
            ("w_br_ssm", (768, 128), 1), ("w_br_attn", (768, 128), 1), ("w_br_mem", (512, 128), 1),
            ("w_out", (128, 1024), 0))
_W_IN = 0
_SMALL = tuple(range(1, len(_SHARDED)))


class _Job(NamedTuple):
    ins: list
    out_shape: list
    aliases: dict
    pairs: Callable
    n: int


def _peers():
    x, y, c = lax.axis_index("x"), lax.axis_index("y"), lax.axis_index("c")
    me = 4 * x + 2 * y + c
    out = []
    for k in range(1, N_DEV):
        px = 1 - x if k & 4 else x
        py = 1 - y if k & 2 else y
        pc = 1 - c if k & 1 else c
        out.append(((px, py, pc), 4 * px + 2 * py + pc))
    return me, out


def _copies(pairs, send_sems, recv_sems, local_sems, arrivals):
    me, peers = _peers()
    local = [pltpu.make_async_copy(src(me), dst(me), local_sems.at[j]) for j, (src, dst) in enumerate(pairs)]
    sends, recvs = [], []
    for k, (peer, lin) in enumerate(peers):
        for j, (src, dst) in enumerate(pairs):
            for to, out in ((dst(me), sends), (dst(lin), recvs)):
                if out is sends or arrivals:
                    out.append(pltpu.make_async_remote_copy(
                        src_ref=src(lin), dst_ref=to, send_sem=send_sems.at[j, k], recv_sem=recv_sems.at[j, k],
                        device_id=peer, device_id_type=pl.DeviceIdType.MESH))
    return local, sends, recvs


def _start_copies(pairs, *sems):
    local, sends, _ = _copies(pairs, *sems, arrivals=False)
    for cp in local + sends:
        cp.start()


def _wait_copies(pairs, *sems):
    local, sends, recvs = _copies(pairs, *sems, arrivals=True)
    for cp in recvs:
        cp.wait_recv()
    for cp in sends:
        cp.wait_send()
    for cp in local:
        cp.wait()


def _job_scratch(job):
    return [pltpu.SemaphoreType.DMA((job.n, N_DEV - 1)), pltpu.SemaphoreType.DMA((job.n, N_DEV - 1)),
            pltpu.SemaphoreType.DMA((job.n,))]


def _pc(body, job, *, name, grid, in_specs, out_specs, out_shape, scratch_shapes, sem, operands):
    if job is None:
        return pl.pallas_call(body, name=name, grid=grid, in_specs=in_specs, out_specs=out_specs, out_shape=out_shape,
                              scratch_shapes=scratch_shapes, compiler_params=_params(sem))(*operands)
    a = len(in_specs)
    b = a + len(job.ins)
    c = b + len(out_shape)
    d = c + len(job.out_shape)
    e = d + len(scratch_shapes)

    def carried(*refs):
        pairs = job.pairs(refs[a:b], refs[c:d])
        ids = [pl.program_id(k) for k in range(len(grid))]
        first = functools.reduce(jnp.logical_and, [i == 0 for i in ids])
        last = functools.reduce(jnp.logical_and, [i == n - 1 for i, n in zip(ids, grid)])

        @pl.when(first)
        def _():
            _start_copies(pairs, *refs[e:])

        body(*refs[:a], *refs[b:c], *refs[d:e])

        @pl.when(last)
        def _():
            _wait_copies(pairs, *refs[e:])

    hbm = pl.BlockSpec(memory_space=pl.ANY)
    outs = pl.pallas_call(
        carried, name=name, grid=grid,
        in_specs=list(in_specs) + [hbm] * len(job.ins), out_specs=list(out_specs) + [hbm] * len(job.out_shape),
        out_shape=list(out_shape) + list(job.out_shape),
        input_output_aliases={a + i: len(out_shape) + o for i, o in job.aliases.items()},
        scratch_shapes=list(scratch_shapes) + _job_scratch(job),
        compiler_params=_params(("arbitrary",) * len(grid)),
    )(*operands, *job.ins)
    return outs[:len(out_shape)], outs[len(out_shape):]


def _exchange_only(job, name):
    n_in = len(job.ins)

    def body(*refs):
        pairs = job.pairs(refs[:n_in], refs[n_in:n_in + len(job.out_shape)])
        sems = refs[n_in + len(job.out_shape):]
        _start_copies(pairs, *sems)
        _wait_copies(pairs, *sems)

    hbm = pl.BlockSpec(memory_space=pl.ANY)
    return pl.pallas_call(
        body, name=name, in_specs=[hbm] * n_in, out_specs=[hbm] * len(job.out_shape), out_shape=list(job.out_shape),
        input_output_aliases=dict(job.aliases), scratch_shapes=_job_scratch(job),
    )(*job.ins)


def _join_jobs(p, q):
    ni, no = len(p.ins), len(p.out_shape)
    return _Job(list(p.ins) + list(q.ins), list(p.out_shape) + list(q.out_shape),
                {**p.aliases, **{ni + i: no + o for i, o in q.aliases.items()}},
                lambda ins, outs: p.pairs(ins[:ni], outs[:no]) + q.pairs(ins[ni:], outs[no:]), p.n + q.n)


def _lane_window(ref, who):
    return ref.at[:, pl.ds(pl.multiple_of(who * LANES, LANES), LANES)]


def _gather_job(shards, items):
    out_shape = []
    for i, _ in items:
        _, s, axis = _SHARDED[i]
        whole = i != _W_IN and axis == 1
        out_shape.append(jax.ShapeDtypeStruct((s[0], N_DEV * s[1]) if whole else (N_DEV,) + s, BF16))

    def pairs(in_refs, out_refs):
        out = []
        for (i, l), src, dst in zip(items, in_refs, out_refs):
            if i != _W_IN and _SHARDED[i][2] == 1:
                out.append((lambda who, src=src, l=l: src.at[l], lambda who, dst=dst: _lane_window(dst, who)))
            else:
                out.append((lambda who, src=src, l=l: src.at[l], lambda who, dst=dst: dst.at[who]))
        return out

    return _Job([shards[i] for i, _ in items], out_shape, {}, pairs, len(items))


def _landed_weights(items, landed):
    out = {}
    for (i, _), a in zip(items, landed):
        n, s, axis = _SHARDED[i]
        if i == _W_IN:
            w = a.transpose(1, 0, 2).reshape(D_MODEL, D_IN)
            out[n] = jnp.concatenate([w[:, st:st + wd] for _, wd, st in _SEGS], axis=1)
        elif axis == 0:
            out[n] = a.reshape(N_DEV * s[0], s[1])
        else:
            out[n] = a
    return out


def _scatter_job(grads, items, layer, parts=None):
    ng = len(grads)
    out_shape = [jax.ShapeDtypeStruct((N_DEV, DEPTH) + _SHARDED[i][1], BF16) for i in items]

    def pairs(in_refs, out_refs):
        out = []
        for i, src, dst in zip(items, in_refs[:ng], out_refs):
            _, s, axis = _SHARDED[i]
            if i == _W_IN:
                take = lambda who, src=src: src.at[who]
            elif axis == 0:
                take = lambda who, src=src, s=s: src.at[pl.ds(pl.multiple_of(who * s[0], 16), s[0])]
            else:
                take = lambda who, src=src: _lane_window(src, who)
            out.append((take, lambda who, dst=dst: dst.at[who, layer]))
        return out

    aliases = {} if parts is None else {ng + j: j for j in range(len(items))}
    return _Job(list(grads) + ([] if parts is None else list(parts)), out_shape, aliases, pairs, len(items))


def _rows_job(src, row0, landing=None):
    n = src.shape[0]
    pairs = lambda in_refs, out_refs: [(lambda who: in_refs[0], lambda who: out_refs[0].at[who, pl.ds(row0, n)])]
    return _Job([src] + ([] if landing is None else [landing]), [jax.ShapeDtypeStruct((N_DEV, _REP_ROWS, LANES), F32)],
                {} if landing is None else {1: 0}, pairs, 1)


_REPLICATED = (("norm_g", (2, 1024)), ("mem_norm_g", (2, 1024)), ("b_gate", (2, 3072)),
               ("ssm_lambda_re", (2, 48, 64)), ("ssm_lambda_im", (2, 48, 64)), ("ssm_log_dt", (2, 48)),
               ("ssm_b_re", (2, 48, 64, 16)), ("ssm_b_im", (2, 48, 64, 16)), ("ssm_c_re", (2, 48, 16, 64)),
               ("ssm_c_im", (2, 48, 16, 64)), ("ssm_d", (2, 768)), ("b_glu", (2, 768)), ("rel_bias", (32, 12)),
               ("final_norm_g", (1024,)))
_PER_LAYER = tuple((n, s[1:]) for n, s in _REPLICATED if s[0] == DEPTH and len(s) > 1)
_SHARED = tuple((n, s) for n, s in _REPLICATED if (n, s[1:]) not in _PER_LAYER)
_REP_HALF_ROWS = 1664
_REP_ROWS = 2 * _REP_HALF_ROWS
assert sum(int(np.prod(s)) for _, s in _PER_LAYER + _SHARED) <= _REP_HALF_ROWS * LANES


def _pack_half(tree, layer, shared):
    flat = [tree[n][layer].reshape(-1) for n, _ in _PER_LAYER]
    if shared:
        flat += [tree[n].reshape(-1) for n, _ in _SHARED]
    flat = jnp.concatenate(flat)
    return jnp.pad(flat, (0, _REP_HALF_ROWS * LANES - flat.shape[0])).reshape(_REP_HALF_ROWS, LANES)


def _pack_replicated(tree):
    return jnp.concatenate([_pack_half(tree, 1, False), _pack_half(tree, 0, True)])[None]


def _unpack_replicated(packed):
    halves = packed.reshape(2, -1)
    out, r = {}, 0
    for n, s in _PER_LAYER:
        size = int(np.prod(s))
        out[n] = jnp.stack([halves[1, r:r + size].reshape(s), halves[0, r:r + size].reshape(s)])
        r += size
    for n, s in _SHARED:
        size = int(np.prod(s))
        out[n] = halves[1, r:r + size].reshape(s)
        r += size
    return out


def _owner_rows_w_in(dw):
    order = sorted(_SEGS, key=lambda t: t[2])
    dw = jnp.concatenate([dw[:, _OFF[n]:_OFF[n] + wd] for n, wd, _ in order], axis=1)
    return dw.reshape(D_MODEL, N_DEV, D_IN // N_DEV).transpose(1, 0, 2)


def _discretize(lam_re, lam_im, log_dt, b_re, b_im):
    dt = jnp.exp(log_dt)[:, None]
    mag = jnp.exp(lam_re * dt)
    abar_re, abar_im = mag * jnp.cos(lam_im * dt), mag * jnp.sin(lam_im * dt)
    den = lam_re * lam_re + lam_im * lam_im
    nr, ni = abar_re - 1.0, abar_im
    f_re = (nr * lam_re + ni * lam_im) / den
    f_im = (ni * lam_re - nr * lam_im) / den
    bbar_re = f_re[..., None] * b_re - f_im[..., None] * b_im
    bbar_im = f_re[..., None] * b_im + f_im[..., None] * b_re
    return abar_re, abar_im, bbar_re, bbar_im


def _block_diag(a):
    _, R, C = a.shape
    a = a.reshape(SSM_BLOCKS, 8, R, C)
    eye = jnp.eye(8, dtype=a.dtype)
    return (a[:, :, :, None, :] * eye[None, :, None, :, None]).reshape(SSM_BLOCKS, 8 * R, 8 * C)


def _diag_blocks(a, R, C):
    a = a.reshape(SSM_BLOCKS, 8, R, 8, C)
    eye = jnp.eye(8, dtype=a.dtype)
    return jnp.sum(a * eye[None, :, None, :, None], axis=3).reshape(SSM_GROUPS, R, C)


def _carried(result, job):
    return (result, None) if job is None else result


def _layer_fwd(x, mem, W, P, bias, layer, jobs):
    tag = f"l{layer}"
    abar_re, abar_im, bbar_re, bbar_im = _discretize(P["ssm_lambda_re"][layer], P["ssm_lambda_im"][layer],
                                                     P["ssm_log_dt"][layer], P["ssm_b_re"][layer], P["ssm_b_im"][layer])
    c_re, c_im = P["ssm_c_re"][layer], P["ssm_c_im"][layer]
    ssm = dict(
        are=abar_re.reshape(1, N_STATE), aim=abar_im.reshape(1, N_STATE),
        bre=_block_diag(bbar_re.transpose(0, 2, 1)).astype(BF16), bim=_block_diag(bbar_im.transpose(0, 2, 1)).astype(BF16),
        cre=_block_diag(c_re.transpose(0, 2, 1)).astype(BF16), cimn=_block_diag(-c_im.transpose(0, 2, 1)).astype(BF16),
        ctre=_block_diag(c_re).astype(BF16), ctimn=_block_diag(-c_im).astype(BF16),
        btre=_block_diag(bbar_re).astype(BF16), btim=_block_diag(bbar_im).astype(BF16),
        d=P["ssm_d"][layer].reshape(1, D_SSM))
    bglu = P["b_glu"][layer].reshape(1, D_SSM)
    bgate = P["b_gate"][layer].reshape(1, N_GATES)
    g = P["norm_g"][layer].reshape(1, D_MODEL)
    gm = P["mem_norm_g"][layer].reshape(1, D_MODEL)
    delivered = {}

    def carry(stage):
        return jobs[stage][0] if stage in jobs else None

    def deliver(stage, landed):
        if landed is not None:
            delivered[stage] = _landed_weights(jobs[stage][1], landed)

    T = x.shape[0]
    (proj, h), landed = _carried(_norm_proj(x, g, W["w_in"], min(T, 1024), 2176, f"{tag}_proj", job=carry("proj")),
                                 carry("proj"))
    deliver("proj", landed)
    W = {**W, **delivered.get("proj", {})}
    (xr, xi, y, o_ssm), landed = _carried(
        _ssm_fwd(proj, ssm["bre"], ssm["bim"], ssm["cre"], ssm["cimn"], ssm["are"], ssm["aim"], ssm["d"], W["w_glu"],
                 bglu, 256, f"{tag}_ssm", job=carry("ssm")), carry("ssm"))
    deliver("ssm", landed)
    os, lses = [], []
    for grp in range(3):
        stage = f"attn{grp}"
        (o_g, lse_g), landed = _carried(_attn_fwd(proj, bias[grp], grp, f"{tag}_{stage}", job=carry(stage)), carry(stage))
        deliver(stage, landed)
        os.append(o_g)
        lses.append(lse_g)
    o_attn = _attn_mix(os, lses, proj, 512, f"{tag}_attn_mix")
    kvb, hm = _norm_proj(mem, gm, W["w_mem_kv"], mem.shape[0], 1024, f"{tag}_mem_kv", out_dtype=BF16)
    o_mem = _mem_fwd(proj, kvb, 512, f"{tag}_mem")
    x_out, merged = _merge_fwd(x, o_ssm, o_attn, o_mem, proj, bgate, W["w_br_ssm"], W["w_br_attn"], W["w_br_mem"],
                               W["w_out"], 256, f"{tag}_merge")
    res = dict(x=x, mem=mem, proj=proj, h=h, xr=xr, xi=xi, y=y, o_ssm=o_ssm, os=os, lses=lses,
               o_attn=o_attn, kvb=kvb, hm=hm, o_mem=o_mem, merged=merged, ssm=ssm, bglu=bglu,
               bgate=bgate, g=g, gm=gm, W=W)
    return x_out, res, delivered


def _layer_bwd(dx, res, P, bias, layer, jobs):
    tag = f"l{layer}b"
    proj, ssm, W = res["proj"], res["ssm"], res["W"]
    T = dx.shape[0]
    landed = {}

    def run(stage, fn, job):
        out, landed[stage] = _carried(fn(job), job)
        if job is None:
            del landed[stage]
        return out

    dgl, dbr, do_ssm, do_attn, do_mem, dbg = run(
        "merge", lambda job: _merge_bwd(dx, res["o_ssm"], res["o_attn"], res["o_mem"], proj, res["bgate"], W["w_br_ssm"],
                                        W["w_br_attn"], W["w_br_mem"], W["w_out"], 256, f"{tag}_merge", job=job),
        jobs.get("merge"))
    gw = {}
    gw["w_out"] = _mm_tn(res["merged"], dx, 1024, 1024, 512, f"{tag}_dw_out")
    gw["w_br_ssm"] = _mm_tn(res["o_ssm"], dbr, 768, 1024, 512, f"{tag}_dw_br_ssm", b_col=0, n=1024)
    gw["w_br_attn"] = _mm_tn(res["o_attn"], dbr, 768, 1024, 512, f"{tag}_dw_br_attn", b_col=1024, n=1024)
    gw["w_br_mem"] = _mm_tn(res["o_mem"], dbr, 512, 1024, 512, f"{tag}_dw_br_mem", b_col=2048, n=1024)

    dqm, dzm, dkv = _mem_bwd(do_mem, proj, res["kvb"], 512, f"{tag}_mem")
    M = dkv.shape[0]
    gw["w_mem_kv"] = _mm_tn(res["hm"], dkv, 1024, 1024, M, f"{tag}_dw_mem_kv")
    _, dgm = _proj_bwd(dkv.astype(BF16), W["w_mem_kv"], res["mem"], res["gm"], jnp.zeros_like(res["mem"]), M, 1024,
                       f"{tag}_mem_norm")

    do_g, corr, dza = _attn_mix_bwd(do_attn, res["os"], res["lses"], proj, 512, f"{tag}_attn_mix")
    dqs, dks, dvs, dbs = [], [], [], []
    for grp in range(3):
        dq_g, dk_g, dv_g, db_g = _attn_bwd(proj, do_g, corr, res["lses"][grp], bias[grp], grp, f"{tag}_attn{grp}")
        dqs.append(dq_g)
        dks.append(dk_g)
        dvs.append(dv_g)
        dbs.append(db_g)
    dbias = jnp.stack(dbs)

    dy, dzs, gelu_b, dt_b, dbglu = _glu_bwd(do_ssm, res["y"], proj, W["w_glu"], res["bglu"], 512, f"{tag}_glu")
    gw["w_glu"] = _mm_tn(gelu_b, dt_b, 768, 768, 512, f"{tag}_dw_glu")
    du, dbre, dbim, dcre, dcim, dare, daim, dd = run(
        "ssm", lambda job: _ssm_bwd(dy, proj, res["xr"], res["xi"], ssm["ctre"], ssm["ctimn"], ssm["btre"], ssm["btim"],
                                    ssm["are"], ssm["aim"], ssm["d"], 256, f"{tag}_ssm", job=job), jobs.get("ssm"))
    _, disc_vjp = jax.vjp(_discretize, P["ssm_lambda_re"][layer], P["ssm_lambda_im"][layer], P["ssm_log_dt"][layer],
                          P["ssm_b_re"][layer], P["ssm_b_im"][layer])
    d_lre, d_lim, d_ldt, d_bre, d_bim = disc_vjp((dare.reshape(SSM_GROUPS, SSM_STATE), daim.reshape(SSM_GROUPS, SSM_STATE),
                                                  _diag_blocks(dbre, SSM_STATE, SSM_GROUP),
                                                  _diag_blocks(dbim, SSM_STATE, SSM_GROUP)))

    small = [gw[_SHARDED[i][0]] for i in _SMALL]
    dproj = jnp.concatenate([dgl, du, dzs] + dqs + dks + dvs + [dza, dqm, dzm], axis=1)
    dw_in = run("dw_in", lambda job: _mm_tn(res["h"], dproj, 1024, 2176, 512, f"{tag}_dw_in", job=job),
                jobs["dw_in"](small) if "dw_in" in jobs else None)
    dx_in, dg = run("proj", lambda job: _proj_bwd(dproj, W["w_in"], res["x"], res["g"], dx, min(T, 512), 2176,
                                                  f"{tag}_proj", job=job),
                    jobs["proj"](small) if "proj" in jobs else None)

    gp = dict(norm_g=dg[0], mem_norm_g=dgm[0], b_gate=dbg[0], ssm_lambda_re=d_lre, ssm_lambda_im=d_lim,
              ssm_log_dt=d_ldt, ssm_b_re=d_bre, ssm_b_im=d_bim,
              ssm_c_re=_diag_blocks(dcre, SSM_GROUP, SSM_STATE), ssm_c_im=_diag_blocks(dcim, SSM_GROUP, SSM_STATE),
              ssm_d=dd[0], b_glu=dbglu[0])
    return dx_in, _owner_rows_w_in(dw_in), gp, dbias, landed


def _train_step(x, mem, target, shards, P):
    rest0 = [(i, 0) for i in _SMALL]
    rest1 = [(i, 1) for i in _SMALL]
    first = [(_W_IN, 0)]
    W0 = _landed_weights(first, _exchange_only(_gather_job(shards, first), "gather_w_in0"))
    buckets = _bucket_tables()
    bias = _bias_tables(P["rel_bias"], buckets, "bias_tables")
    jobs0 = {"proj": (_gather_job(shards, rest0), rest0), "ssm": (_gather_job(shards, [(_W_IN, 1)]), [(_W_IN, 1)]),
             "attn0": (_gather_job(shards, rest1), rest1)}
    x, res0, delivered = _layer_fwd(x, mem, W0, P, bias, 0, jobs0)
    x, res1, _ = _layer_fwd(x, mem, {**delivered["ssm"], **delivered["attn0"]}, P, bias, 1, {})
    loss, dx, dgf = _loss_head(x, P["final_norm_g"].reshape(1, D_MODEL), target, 512, "loss_head")

    dx, dw_in1, gp1, dbias1, landed1 = _layer_bwd(
        dx, res1, P, bias, 1, {"proj": lambda small: _scatter_job(small, _SMALL, 1)})
    rep1 = _pack_half({n: a[None] for n, a in gp1.items()}, 0, False)
    dx, dw_in0, gp0, dbias0, landed0 = _layer_bwd(
        dx, res0, P, bias, 0,
        {"merge": _rows_job(rep1, 0), "ssm": _scatter_job([dw_in1], [_W_IN], 1),
         "dw_in": lambda small: _scatter_job(small, _SMALL, 0, parts=landed1["proj"])})
    d_rel = _bias_grad(dbias0, dbias1, buckets, "bias_grad")
    gp0 = {n: a[None] for n, a in gp0.items()}
    gp0["rel_bias"] = jnp.sum(d_rel, axis=-1).transpose(2, 0, 1).reshape(NUM_BUCKETS, 12)
    gp0["final_norm_g"] = dgf[0]
    rep0 = _pack_half(gp0, 0, True)
    parts_w_in, rparts = _exchange_only(
        _join_jobs(_scatter_job([dw_in0], [_W_IN], 0, parts=landed0["ssm"]),
                   _rows_job(rep0, _REP_HALF_ROWS, landing=landed0["merge"][0])), "exchange_tail")
    return loss[0, 0], dx, [parts_w_in] + list(landed0["dw_in"]), rparts


_WEIGHTS = ["norm_g", "mem_norm_g", "w_in", "b_gate", "ssm_lambda_re", "ssm_lambda_im", "ssm_log_dt", "ssm_b_re",
            "ssm_b_im", "ssm_c_re", "ssm_c_im", "ssm_d", "w_glu", "b_glu", "w_mem_kv", "w_br_ssm", "w_br_attn",
            "w_br_mem", "w_out", "rel_bias", "final_norm_g"]
_ADAM_ROWS = {"w_in": 128, "w_glu": 96, "w_mem_kv": 128, "w_br_ssm": 768, "w_br_attn": 768, "w_br_mem": 512,
              "w_out": 128}


def kernel(x, mem, norm_g, mem_norm_g, w_in, b_gate, ssm_lambda_re, ssm_lambda_im, ssm_log_dt, ssm_b_re, ssm_b_im, ssm_c_re, ssm_c_im, ssm_d, w_glu, b_glu, w_mem_kv, w_br_ssm, w_br_attn, w_br_mem, w_out, rel_bias, final_norm_g, loss_target, m_norm_g, m_mem_norm_g, m_w_in, m_b_gate, m_ssm_lambda_re, m_ssm_lambda_im, m_ssm_log_dt, m_ssm_b_re, m_ssm_b_im, m_ssm_c_re, m_ssm_c_im, m_ssm_d, m_w_glu, m_b_glu, m_w_mem_kv, m_w_br_ssm, m_w_br_attn, m_w_br_mem, m_w_out, m_rel_bias, m_final_norm_g, v_norm_g, v_mem_norm_g, v_w_in, v_b_gate, v_ssm_lambda_re, v_ssm_lambda_im, v_ssm_log_dt, v_ssm_b_re, v_ssm_b_im, v_ssm_c_re, v_ssm_c_im, v_ssm_d, v_w_glu, v_b_glu, v_w_mem_kv, v_w_br_ssm, v_w_br_attn, v_w_br_mem, v_w_out, v_rel_bias, v_final_norm_g):
    given = dict(locals())
    w = {n: given[n] for n in _WEIGHTS}
    m = {n: given["m_" + n] for n in _WEIGHTS}
    v = {n: given["v_" + n] for n in _WEIGHTS}

    shards = [w[n].astype(BF16) for n, _, _ in _SHARDED]
    loss, dx, parts, rparts = _train_step(x[0], mem[0], loss_target[0], shards, w)
    loss = lax.psum(loss, ("x", "y", "c"))

    new = {}
    for (n, _, _), p in zip(_SHARDED, parts):
        new[n] = _adamw(p, w[n], m[n], v[n], _ADAM_ROWS[n], f"adamw_{n}")
    rp = [_unpack_replicated(a) for a in _adamw(rparts[:, None], _pack_replicated(w), _pack_replicated(m),
                                                _pack_replicated(v), _REP_ROWS // 4, "adamw_replicated")]
    for n, _ in _REPLICATED:
        new[n] = [rp[kind][n] for kind in range(4)]
    outs = [loss, dx[None]]
    for kind in range(4):
        outs.extend(new[n][kind] for n in _WEIGHTS)
    return tuple(outs)
```

```python
import functools
import math
from typing import Callable, NamedTuple

import jax
import jax.numpy as jnp
import numpy as np
from jax import lax
from jax.experimental import pallas as pl
from jax.experimental.pallas import tpu as pltpu

F32 = jnp.float32
BF16 = jnp.bfloat16

D_MODEL = 1024
DEPTH = 2
EPS = 1e-6
D_SSM = 768
SSM_GROUP = 16
SSM_GROUPS = 48
SSM_STATE = 64
N_STATE = SSM_GROUPS * SSM_STATE
SSM_BLOCKS = 6
SCAN_UNROLL = 8
D_ATTN = 768
ATTN_HEAD_DIM = 64
ATTN_GROUP_WIDTH = 256
ATTN_DILATIONS = (1, 4, 16)
ATTN_SPAN = 128
ATTN_BLOCK = 128
NUM_BUCKETS = 32
REL_MAX_DISTANCE = 2048
NEG_INF = -1e30
MEM_HEADS = 4
MEM_HEAD_DIM = 128
D_MEM = 512
N_GATES = 3 * D_MODEL
D_IN = 8704
N_DEV = 8
LANES = 128
ADAM_LR = 0.001
ADAM_B1 = 0.9
ADAM_B2 = 0.999
ADAM_EPS = 1e-08
ADAM_WD = 0.01
ADAM_STEP = 10

_SEGS = (("gates", 3072, 5632), ("u", 768, 0), ("z_ssm", 768, 768), ("q", 768, 1536), ("k", 768, 2304),
         ("v", 768, 3072), ("z_attn", 768, 3840), ("q_mem", 512, 4608), ("z_mem", 512, 5120))
_OFF = {}
_o = 0
for _n, _w, _s in _SEGS:
    _OFF[_n] = _o
    _o += _w

NN = (((1,), (0,)), ((), ()))
NT = (((1,), (1,)), ((), ()))
TN = (((0,), (0,)), ((), ()))

VMEM_LIMIT = 56 * 1024 * 1024


def _dot(a, b, dims=NN):
    return lax.dot_general(a, b, dims, preferred_element_type=F32)


def _sigmoid(x):
    return 1.0 / (1.0 + jnp.exp(-x))


def _gelu_parts(x):
    k = math.sqrt(2.0 / math.pi)
    t = jnp.tanh(k * (x + 0.044715 * (x * x * x)))
    cdf = 0.5 * (1.0 + t)
    dcdf = 0.5 * (1.0 - t * t) * k * (1.0 + 3.0 * 0.044715 * (x * x))
    return x * cdf, cdf + x * dcdf


def _params(sem, vmem=VMEM_LIMIT):
    return pltpu.CompilerParams(dimension_semantics=sem, vmem_limit_bytes=vmem)


def _full(shape):
    return pl.BlockSpec(shape, lambda *_: (0,) * len(shape))


def _norm_proj(x, g, w, tm, tn, name, out_dtype=F32, job=None):
    T, D = x.shape
    N = w.shape[1]

    def body(x_ref, g_ref, w_ref, o_ref, h_ref, hs):
        @pl.when(pl.program_id(1) == 0)
        def _():
            xv = x_ref[...]
            r = lax.rsqrt(jnp.mean(xv * xv, axis=-1, keepdims=True) + EPS)
            hv = (xv * r * g_ref[...]).astype(BF16)
            hs[...] = hv
            h_ref[...] = hv

        o_ref[...] = _dot(hs[...], w_ref[...]).astype(out_dtype)

    return _pc(
        body, job, name=name, grid=(T // tm, N // tn),
        in_specs=[pl.BlockSpec((tm, D), lambda i, j: (i, 0)), _full((1, D)),
                  pl.BlockSpec((D, tn), lambda i, j: (0, j))],
        out_specs=[pl.BlockSpec((tm, tn), lambda i, j: (i, j)), pl.BlockSpec((tm, D), lambda i, j: (i, 0))],
        out_shape=[jax.ShapeDtypeStruct((T, N), out_dtype), jax.ShapeDtypeStruct((T, D), BF16)],
        scratch_shapes=[pltpu.VMEM((tm, D), BF16)], sem=("parallel", "arbitrary"), operands=(x, g, w))


def _mm_tn(a, b, tm, tn, tk, name, b_col=0, n=None, job=None):
    K, M = a.shape
    N = b.shape[1] if n is None else n
    nk = K // tk
    j0 = b_col // tn

    def body(a_ref, b_ref, o_ref, acc):
        k = pl.program_id(2)

        @pl.when(k == 0)
        def _():
            acc[...] = jnp.zeros_like(acc)

        acc[...] += _dot(a_ref[...].astype(BF16), b_ref[...].astype(BF16), TN)

        @pl.when(k == nk - 1)
        def _():
            o_ref[...] = acc[...].astype(BF16)

    out = _pc(
        body, job, name=name, grid=(M // tm, N // tn, nk),
        in_specs=[pl.BlockSpec((tk, tm), lambda i, j, k: (k, i)), pl.BlockSpec((tk, tn), lambda i, j, k: (k, j0 + j))],
        out_specs=[pl.BlockSpec((tm, tn), lambda i, j, k: (i, j))],
        out_shape=[jax.ShapeDtypeStruct((M, N), BF16)],
        scratch_shapes=[pltpu.VMEM((tm, tn), F32)], sem=("parallel", "parallel", "arbitrary"), operands=(a, b))
    return out[0] if job is None else (out[0][0], out[1])


def _proj_bwd(dp, w, x, g, dres, tm, tk, name, job=None):
    T, N = dp.shape
    D = w.shape[0]
    nk = N // tk

    def body(dp_ref, w_ref, x_ref, g_ref, dres_ref, dx_ref, dg_ref, acc):
        i, k = pl.program_id(0), pl.program_id(1)

        @pl.when(k == 0)
        def _():
            acc[...] = jnp.zeros_like(acc)

        @pl.when((i == 0) & (k == 0))
        def _():
            dg_ref[...] = jnp.zeros_like(dg_ref)

        acc[...] += _dot(dp_ref[...], w_ref[...], NT)

        @pl.when(k == nk - 1)
        def _():
            xv = x_ref[...]
            dh = acc[...]
            r = lax.rsqrt(jnp.mean(xv * xv, axis=-1, keepdims=True) + EPS)
            xr = xv * r
            dg_ref[...] += jnp.sum(dh * xr, axis=0, keepdims=True)
            wv = dh * g_ref[...]
            dx_ref[...] = dres_ref[...] + r * (wv - xr * jnp.mean(wv * xr, axis=-1, keepdims=True))

    return _pc(
        body, job, name=name, grid=(T // tm, nk),
        in_specs=[pl.BlockSpec((tm, tk), lambda i, k: (i, k)), pl.BlockSpec((D, tk), lambda i, k: (0, k)),
                  pl.BlockSpec((tm, D), lambda i, k: (i, 0)), _full((1, D)),
                  pl.BlockSpec((tm, D), lambda i, k: (i, 0))],
        out_specs=[pl.BlockSpec((tm, D), lambda i, k: (i, 0)), _full((1, D))],
        out_shape=[jax.ShapeDtypeStruct((T, D), F32), jax.ShapeDtypeStruct((1, D), F32)],
        scratch_shapes=[pltpu.VMEM((tm, D), F32)], sem=("arbitrary", "arbitrary"), operands=(dp, w, x, g, dres))


def _ssm_fwd(proj, bre, bim, cre, cimn, are, aim, d, wglu, bglu, tc, name, job=None):
    T = proj.shape[0]
    ucol, zcol = _OFF["u"] // D_SSM, _OFF["z_ssm"] // D_SSM

    def body(u_ref, z_ref, bre_ref, bim_ref, cre_ref, cim_ref, are_ref, aim_ref, d_ref, wg_ref, bg_ref,
             xr_ref, xi_ref, y_ref, o_ref, car_r, car_i):
        @pl.when(pl.program_id(0) == 0)
        def _():
            car_r[...] = jnp.zeros_like(car_r)
            car_i[...] = jnp.zeros_like(car_i)

        u = u_ref[...]
        ub = u.astype(BF16)
        for k in range(SSM_BLOCKS):
            uk = ub[:, 128 * k:128 * (k + 1)]
            xr_ref[:, 512 * k:512 * (k + 1)] = _dot(uk, bre_ref[k])
            xi_ref[:, 512 * k:512 * (k + 1)] = _dot(uk, bim_ref[k])
        ar, ai = are_ref[...], aim_ref[...]

        def step(t, c):
            pr, pi = c
            nr = ar * pr - ai * pi + xr_ref[pl.ds(t, 1), :]
            ni = ar * pi + ai * pr + xi_ref[pl.ds(t, 1), :]
            xr_ref[pl.ds(t, 1), :] = nr
            xi_ref[pl.ds(t, 1), :] = ni
            return nr, ni

        pr, pi = lax.fori_loop(0, tc, step, (car_r[...], car_i[...]), unroll=SCAN_UNROLL)
        car_r[...] = pr
        car_i[...] = pi

        ys = []
        for k in range(SSM_BLOCKS):
            xrk = xr_ref[:, 512 * k:512 * (k + 1)].astype(BF16)
            xik = xi_ref[:, 512 * k:512 * (k + 1)].astype(BF16)
            ys.append(_dot(xrk, cre_ref[k]) + _dot(xik, cim_ref[k]))
        y = jnp.concatenate(ys, axis=1) + d_ref[...] * u
        y_ref[...] = y
        gl, _ = _gelu_parts(y)
        t = _dot(gl.astype(BF16), wg_ref[...]) + bg_ref[...]
        z = z_ref[...]
        o_ref[...] = (gl * _sigmoid(t) * (z * _sigmoid(z))).astype(BF16)

    return _pc(
        body, job, name=name, grid=(T // tc,),
        in_specs=[pl.BlockSpec((tc, D_SSM), lambda i: (i, ucol)), pl.BlockSpec((tc, D_SSM), lambda i: (i, zcol)),
                  _full((SSM_BLOCKS, 128, 512)), _full((SSM_BLOCKS, 128, 512)),
                  _full((SSM_BLOCKS, 512, 128)), _full((SSM_BLOCKS, 512, 128)),
                  _full((1, N_STATE)), _full((1, N_STATE)), _full((1, D_SSM)),
                  _full((D_SSM, D_SSM)), _full((1, D_SSM))],
        out_specs=[pl.BlockSpec((tc, N_STATE), lambda i: (i, 0)), pl.BlockSpec((tc, N_STATE), lambda i: (i, 0)),
                   pl.BlockSpec((tc, D_SSM), lambda i: (i, 0)), pl.BlockSpec((tc, D_SSM), lambda i: (i, 0))],
        out_shape=[jax.ShapeDtypeStruct((T, N_STATE), F32), jax.ShapeDtypeStruct((T, N_STATE), F32),
                   jax.ShapeDtypeStruct((T, D_SSM), F32), jax.ShapeDtypeStruct((T, D_SSM), BF16)],
        scratch_shapes=[pltpu.VMEM((1, N_STATE), F32), pltpu.VMEM((1, N_STATE), F32)], sem=("arbitrary",),
        operands=(proj, proj, bre, bim, cre, cimn, are, aim, d, wglu, bglu))


def _glu_bwd(do, y, proj, wglu, bglu, tm, name):
    T = y.shape[0]
    zcol = _OFF["z_ssm"] // D_SSM

    def body(do_ref, y_ref, z_ref, wg_ref, bg_ref, dy_ref, dz_ref, g_ref, dt_ref, db_ref):
        @pl.when(pl.program_id(0) == 0)
        def _():
            db_ref[...] = jnp.zeros_like(db_ref)

        dov = do_ref[...]
        gl, dgl = _gelu_parts(y_ref[...])
        glb = gl.astype(BF16)
        sg = _sigmoid(_dot(glb, wg_ref[...]) + bg_ref[...])
        z = z_ref[...]
        sz = _sigmoid(z)
        dz_ref[...] = (dov * (gl * sg) * (sz * (1.0 + z * (1.0 - sz)))).astype(BF16)
        dy2 = dov * (z * sz)
        dt = dy2 * gl * (sg * (1.0 - sg))
        dtb = dt.astype(BF16)
        dg = dy2 * sg + _dot(dtb, wg_ref[...], NT)
        dy_ref[...] = dg * dgl
        g_ref[...] = glb
        dt_ref[...] = dtb
        db_ref[...] += jnp.sum(dt, axis=0, keepdims=True)

    row = lambda i: (i, 0)
    return pl.pallas_call(
        body, name=name, grid=(T // tm,),
        in_specs=[pl.BlockSpec((tm, D_SSM), row), pl.BlockSpec((tm, D_SSM), row),
                  pl.BlockSpec((tm, D_SSM), lambda i: (i, zcol)), _full((D_SSM, D_SSM)), _full((1, D_SSM))],
        out_specs=[pl.BlockSpec((tm, D_SSM), row)] * 4 + [_full((1, D_SSM))],
        out_shape=[jax.ShapeDtypeStruct((T, D_SSM), F32), jax.ShapeDtypeStruct((T, D_SSM), BF16),
                   jax.ShapeDtypeStruct((T, D_SSM), BF16), jax.ShapeDtypeStruct((T, D_SSM), BF16),
                   jax.ShapeDtypeStruct((1, D_SSM), F32)],
        compiler_params=_params(("arbitrary",)),
    )(do, y, proj, wglu, bglu)


def _ssm_bwd(dy, proj, xr, xi, ctre, ctimn, btre, btim, are, aim, d, tc, name, job=None):
    T = dy.shape[0]
    nc = T // tc
    ucol = _OFF["u"] // D_SSM
    rb = tc // 8

    def body(dy_ref, u_ref, xr_ref, xi_ref, xpr_ref, xpi_ref, ctre_ref, ctim_ref, btre_ref, btim_ref,
             are_ref, aim_ref, d_ref,
             du_ref, dbre_ref, dbim_ref, dcre_ref, dcim_ref, dare_ref, daim_ref, dd_ref, gr, gi, car_r, car_i):
        i = pl.program_id(0)

        @pl.when(i == 0)
        def _():
            for ref in (car_r, car_i, dbre_ref, dbim_ref, dcre_ref, dcim_ref, dare_ref, daim_ref, dd_ref):
                ref[...] = jnp.zeros_like(ref)

        dyv = dy_ref[...]
        dyb = dyv.astype(BF16)
        u = u_ref[...]
        ub = u.astype(BF16)
        for k in range(SSM_BLOCKS):
            dk = dyb[:, 128 * k:128 * (k + 1)]
            gr[:, 512 * k:512 * (k + 1)] = _dot(dk, ctre_ref[k])
            gi[:, 512 * k:512 * (k + 1)] = _dot(dk, ctim_ref[k])
        ar, ai = are_ref[...], aim_ref[...]

        def step(s, c):
            pr, pi = c
            t = tc - 1 - s
            nr = gr[pl.ds(t, 1), :] + ar * pr + ai * pi
            ni = gi[pl.ds(t, 1), :] + ar * pi - ai * pr
            gr[pl.ds(t, 1), :] = nr
            gi[pl.ds(t, 1), :] = ni
            return nr, ni

        pr, pi = lax.fori_loop(0, tc, step, (car_r[...], car_i[...]), unroll=SCAN_UNROLL)
        car_r[...] = pr
        car_i[...] = pi

        xrv, xiv = xr_ref[...], xi_ref[...]
        keep = jnp.where(i == nc - 1, 0.0, 1.0)
        row0 = lax.broadcasted_iota(jnp.int32, (tc, 1), 0) == 0
        xsr = jnp.where(row0, xpr_ref[7:8, :] * keep, pltpu.roll(xrv, 1, axis=0))
        xsi = jnp.where(row0, xpi_ref[7:8, :] * keep, pltpu.roll(xiv, 1, axis=0))
        grv, giv = gr[...], gi[...]
        dare_ref[...] += jnp.sum(grv * xsr + giv * xsi, axis=0, keepdims=True)
        daim_ref[...] += jnp.sum(giv * xsr - grv * xsi, axis=0, keepdims=True)
        dd_ref[...] += jnp.sum(dyv * u, axis=0, keepdims=True)

        dus = []
        for k in range(SSM_BLOCKS):
            sl = slice(512 * k, 512 * (k + 1))
            ch = slice(128 * k, 128 * (k + 1))
            grb, gib = grv[:, sl].astype(BF16), giv[:, sl].astype(BF16)
            dus.append(_dot(grb, btre_ref[k]) + _dot(gib, btim_ref[k]))
            dbre_ref[k] += _dot(grb, ub[:, ch], TN)
            dbim_ref[k] += _dot(gib, ub[:, ch], TN)
            dcre_ref[k] += _dot(dyb[:, ch], xrv[:, sl].astype(BF16), TN)
            dcim_ref[k] -= _dot(dyb[:, ch], xiv[:, sl].astype(BF16), TN)
        du_ref[...] = (jnp.concatenate(dus, axis=1) + d_ref[...] * dyv).astype(BF16)

    rev = lambda i: (nc - 1 - i, 0)
    prev = lambda i: (jnp.maximum((nc - 1 - i) * rb - 1, 0), 0)
    return _pc(
        body, job, name=name, grid=(nc,),
        in_specs=[pl.BlockSpec((tc, D_SSM), rev), pl.BlockSpec((tc, D_SSM), lambda i: (nc - 1 - i, ucol)),
                  pl.BlockSpec((tc, N_STATE), rev), pl.BlockSpec((tc, N_STATE), rev),
                  pl.BlockSpec((8, N_STATE), prev), pl.BlockSpec((8, N_STATE), prev),
                  _full((SSM_BLOCKS, 128, 512)), _full((SSM_BLOCKS, 128, 512)),
                  _full((SSM_BLOCKS, 512, 128)), _full((SSM_BLOCKS, 512, 128)),
                  _full((1, N_STATE)), _full((1, N_STATE)), _full((1, D_SSM))],
        out_specs=[pl.BlockSpec((tc, D_SSM), rev),
                   _full((SSM_BLOCKS, 512, 128)), _full((SSM_BLOCKS, 512, 128)),
                   _full((SSM_BLOCKS, 128, 512)), _full((SSM_BLOCKS, 128, 512)),
                   _full((1, N_STATE)), _full((1, N_STATE)), _full((1, D_SSM))],
        out_shape=[jax.ShapeDtypeStruct((T, D_SSM), BF16),
                   jax.ShapeDtypeStruct((SSM_BLOCKS, 512, 128), F32), jax.ShapeDtypeStruct((SSM_BLOCKS, 512, 128), F32),
                   jax.ShapeDtypeStruct((SSM_BLOCKS, 128, 512), F32), jax.ShapeDtypeStruct((SSM_BLOCKS, 128, 512), F32),
                   jax.ShapeDtypeStruct((1, N_STATE), F32), jax.ShapeDtypeStruct((1, N_STATE), F32),
                   jax.ShapeDtypeStruct((1, D_SSM), F32)],
        scratch_shapes=[pltpu.VMEM((tc, N_STATE), F32), pltpu.VMEM((tc, N_STATE), F32),
                        pltpu.VMEM((1, N_STATE), F32), pltpu.VMEM((1, N_STATE), F32)], sem=("arbitrary",),
        operands=(dy, proj, xr, xi, xr, xi, ctre, ctimn, btre, btim, are, aim, d))


def _rel_bucket(dist):
    n = jnp.maximum(dist, 0)
    max_exact = NUM_BUCKETS // 2
    n_f = jnp.maximum(n, 1).astype(F32)
    large = max_exact + (jnp.log(n_f / max_exact) / math.log(REL_MAX_DISTANCE / max_exact)
                         * (NUM_BUCKETS - max_exact)).astype(jnp.int32)
    large = jnp.minimum(large, NUM_BUCKETS - 1)
    return jnp.where(n < max_exact, n, large)


def _bucket_tables():
    qi = jnp.arange(ATTN_BLOCK)[:, None]
    kj = jnp.arange(2 * ATTN_BLOCK)[None, :]
    delta = jnp.maximum(ATTN_BLOCK + qi - kj, 0)
    return jnp.stack([_rel_bucket(delta * r) for r in ATTN_DILATIONS]).astype(jnp.int32)


def _bias_tables(rel_bias, buckets, name):
    def body(tab_ref, bk_ref, o_ref):
        g = pl.program_id(0)
        bk = bk_ref[...]
        qi = lax.broadcasted_iota(jnp.int32, bk.shape, 0)
        kj = lax.broadcasted_iota(jnp.int32, bk.shape, 1)
        delta = ATTN_BLOCK + qi - kj
        band = (delta >= 0) & (delta <= ATTN_SPAN)
        accs = [jnp.zeros(bk.shape, F32) for _ in range(4)]
        for b in range(NUM_BUCKETS):
            hit = bk == b
            for h in range(4):
                accs[h] = jnp.where(hit, tab_ref[b, 4 * g + h], accs[h])
        for h in range(4):
            o_ref[h] = jnp.where(band, accs[h], NEG_INF)

    return pl.pallas_call(
        body, name=name, grid=(3,),
        in_specs=[pl.BlockSpec(memory_space=pltpu.SMEM),
                  pl.BlockSpec((None, ATTN_BLOCK, 2 * ATTN_BLOCK), lambda g: (g, 0, 0))],
        out_specs=pl.BlockSpec((None, 4, ATTN_BLOCK, 2 * ATTN_BLOCK), lambda g: (g, 0, 0, 0)),
        out_shape=jax.ShapeDtypeStruct((3, 4, ATTN_BLOCK, 2 * ATTN_BLOCK), F32),
        compiler_params=_params(("parallel",)),
    )(rel_bias, buckets)


def _bias_grad(db0, db1, buckets, name):
    def body(a_ref, b_ref, bk_ref, o_ref):
        bk = bk_ref[...]
        for h in range(4):
            dv = a_ref[h] + b_ref[h]
            for b in range(NUM_BUCKETS):
                o_ref[h, b:b + 1, :] = jnp.sum(jnp.where(bk == b, dv, 0.0), axis=0, keepdims=True)

    tab = pl.BlockSpec((None, 4, ATTN_BLOCK, 2 * ATTN_BLOCK), lambda g: (g, 0, 0, 0))
    return pl.pallas_call(
        body, name=name, grid=(3,),
        in_specs=[tab, tab, pl.BlockSpec((None, ATTN_BLOCK, 2 * ATTN_BLOCK), lambda g: (g, 0, 0))],
        out_specs=pl.BlockSpec((None, 4, NUM_BUCKETS, 2 * ATTN_BLOCK), lambda g: (g, 0, 0, 0)),
        out_shape=jax.ShapeDtypeStruct((3, 4, NUM_BUCKETS, 2 * ATTN_BLOCK), F32),
        compiler_params=_params(("parallel",)),
    )(db0, db1, buckets)


_ATTN_SUB = {1: 4, 4: 1, 16: 1}
_UNROLL = 4


def _unit_rows(j, s, r):
    start = j * ATTN_BLOCK * r + s
    return pl.ds(start, ATTN_BLOCK, stride=r) if r > 1 else pl.ds(start, ATTN_BLOCK)


def _for_units(r, nsub, fn, after):
    if r * nsub <= _UNROLL:
        after([fn(j, s) for j in range(nsub) for s in range(r)])
    else:
        def four(i, c):
            after([fn(0, _UNROLL * i + k) for k in range(_UNROLL)])
            return c

        lax.fori_loop(0, r // _UNROLL, four, 0)


def _attn_cols(g):
    return tuple((_OFF[n] + ATTN_GROUP_WIDTH * g) // LANES for n in ("q", "k", "v"))


def _attn_fwd(proj, bias, g, name, job=None):
    r = ATTN_DILATIONS[g]
    nsub = _ATTN_SUB[r]
    T = proj.shape[0]
    sub = ATTN_BLOCK * r
    tb = sub * nsub
    qc, kc, vc = _attn_cols(g)
    scale = ATTN_HEAD_DIM ** -0.5

    def body(q_ref, kc_ref, kp_ref, vc_ref, vp_ref, bias_ref, o_ref, lse_ref):
        lane = lax.broadcasted_iota(jnp.int32, (ATTN_BLOCK, LANES), 1)
        kj = lax.broadcasted_iota(jnp.int32, (ATTN_BLOCK, 2 * ATTN_BLOCK), 1)
        dead = (pl.program_id(0) == 0) & (kj < ATTN_BLOCK)

        def one(j, s):
            rows = _unit_rows(j, s, r)
            before = _unit_rows(max(j - 1, 0), s, r)
            k_before = kc_ref[before, :] if j else kp_ref[before, :]
            v_before = vc_ref[before, :] if j else vp_ref[before, :]
            q = q_ref[rows, :]
            kcat = jnp.concatenate([k_before, kc_ref[rows, :]], axis=0).astype(BF16)
            vcat = jnp.concatenate([v_before, vc_ref[rows, :]], axis=0).astype(BF16)
            o_acc = jnp.zeros((ATTN_BLOCK, LANES), F32)
            l_acc = jnp.zeros((ATTN_BLOCK, LANES), F32)
            for hh in range(2):
                mine = (lane >= ATTN_HEAD_DIM) if hh else (lane < ATTN_HEAD_DIM)
                qm = jnp.where(mine, q, 0.0).astype(BF16)
                sc = _dot(qm, kcat, NT) * scale + bias_ref[hh]
                if j == 0:
                    sc = jnp.where(dead, NEG_INF, sc)
                m = jnp.max(sc, axis=-1, keepdims=True)
                p = jnp.exp(sc - m)
                l = jnp.sum(p, axis=-1, keepdims=True)
                o_acc = jnp.where(mine, _dot((p / l).astype(BF16), vcat), o_acc)
                l_acc = jnp.where(mine, m + jnp.log(l), l_acc)
            o_ref[rows, :] = o_acc
            lse_ref[rows, :] = l_acc

        _for_units(r, nsub, one, lambda results: None)

    cur = lambda c: pl.BlockSpec((tb, LANES), lambda b, p: (b, c + p))
    prev = lambda c: pl.BlockSpec((sub, LANES), lambda b, p: (jnp.maximum(b * nsub - 1, 0), c + p))
    out = pl.BlockSpec((tb, LANES), lambda b, p: (b, p))
    return _pc(
        body, job, name=name, grid=(T // tb, 2),
        in_specs=[cur(qc), cur(kc), prev(kc), cur(vc), prev(vc),
                  pl.BlockSpec((2, ATTN_BLOCK, 2 * ATTN_BLOCK), lambda b, p: (p, 0, 0))],
        out_specs=[out, out],
        out_shape=[jax.ShapeDtypeStruct((T, ATTN_GROUP_WIDTH), F32), jax.ShapeDtypeStruct((T, ATTN_GROUP_WIDTH), F32)],
        scratch_shapes=[], sem=("parallel", "parallel"), operands=(proj, proj, proj, proj, proj, bias))


def _attn_bwd(proj, do, corr, lse, bias, g, name):
    r = ATTN_DILATIONS[g]
    nsub = _ATTN_SUB[r]
    T = proj.shape[0]
    sub = ATTN_BLOCK * r
    tb = sub * nsub
    nb = T // tb
    qc, kc, vc = _attn_cols(g)
    dc = ATTN_GROUP_WIDTH * g // LANES
    scale = ATTN_HEAD_DIM ** -0.5

    def body(q_ref, kc_ref, kp_ref, vc_ref, vp_ref, do_ref, corr_ref, lse_ref, bias_ref,
             dq_ref, dk_ref, dv_ref, db_ref, dq_s, dkc_s, dkp_s, dvc_s, dvp_s, kacc, vacc):
        b = pl.program_id(1)

        @pl.when(b == 0)
        def _():
            db_ref[...] = jnp.zeros_like(db_ref)
            kacc[...] = jnp.zeros_like(kacc)
            vacc[...] = jnp.zeros_like(vacc)

        @pl.when(b == nb)
        def _():
            dk_ref[...] = kacc[...].astype(BF16)
            dv_ref[...] = vacc[...].astype(BF16)

        @pl.when(b < nb)
        def _():
            lane = lax.broadcasted_iota(jnp.int32, (ATTN_BLOCK, LANES), 1)
            kj = lax.broadcasted_iota(jnp.int32, (ATTN_BLOCK, 2 * ATTN_BLOCK), 1)
            dead = (b == 0) & (kj < ATTN_BLOCK)

            def one(j, s):
                rows = _unit_rows(j, s, r)
                before = _unit_rows(max(j - 1, 0), s, r)
                k_before = kc_ref[before, :] if j else kp_ref[before, :]
                v_before = vc_ref[before, :] if j else vp_ref[before, :]
                q = q_ref[rows, :]
                kcat = jnp.concatenate([k_before, kc_ref[rows, :]], axis=0).astype(BF16)
                vcat = jnp.concatenate([v_before, vc_ref[rows, :]], axis=0).astype(BF16)
                dov, corrv, lsev = do_ref[rows, :], corr_ref[rows, :], lse_ref[rows, :]
                dq_acc = jnp.zeros((ATTN_BLOCK, LANES), F32)
                dk_acc = jnp.zeros((2 * ATTN_BLOCK, LANES), F32)
                dv_acc = jnp.zeros((2 * ATTN_BLOCK, LANES), F32)
                dss = []
                for hh in range(2):
                    mine = (lane >= ATTN_HEAD_DIM) if hh else (lane < ATTN_HEAD_DIM)
                    col = slice(ATTN_HEAD_DIM * hh, ATTN_HEAD_DIM * hh + 1)
                    qm = jnp.where(mine, q, 0.0).astype(BF16)
                    dom = jnp.where(mine, dov, 0.0).astype(BF16)
                    sc = _dot(qm, kcat, NT) * scale + bias_ref[hh]
                    if j == 0:
                        sc = jnp.where(dead, NEG_INF, sc)
                    p = jnp.exp(sc - lsev[:, col])
                    ds = p * (_dot(dom, vcat, NT) - corrv[:, col])
                    dss.append(ds)
                    dsb = ds.astype(BF16)
                    dq_acc = jnp.where(mine, _dot(dsb, kcat) * scale, dq_acc)
                    dk_acc += _dot(dsb, qm, TN) * scale
                    dv_acc += _dot(p.astype(BF16), dom, TN)
                dq_s[rows, :] = dq_acc
                dkp_s[rows, :] = dk_acc[:ATTN_BLOCK]
                dkc_s[rows, :] = dk_acc[ATTN_BLOCK:]
                dvp_s[rows, :] = dv_acc[:ATTN_BLOCK]
                dvc_s[rows, :] = dv_acc[ATTN_BLOCK:]
                return dss

            def add_bias_grads(results):
                for hh in range(2):
                    db_ref[hh] += functools.reduce(lambda x, y: x + y, [dss[hh] for dss in results])

            _for_units(r, nsub, one, add_bias_grads)
            dq_ref[...] = dq_s[...].astype(BF16)
            tail = slice((nsub - 1) * sub, nsub * sub)
            for acc, before_s, cur_s, out_ref in ((kacc, dkp_s, dkc_s, dk_ref), (vacc, dvp_s, dvc_s, dv_ref)):
                acc[tail, :] += before_s[0:sub, :]
                out_ref[...] = acc[...].astype(BF16)
                acc[...] = cur_s[...]
                for j in range(nsub - 1):
                    acc[j * sub:(j + 1) * sub, :] += before_s[(j + 1) * sub:(j + 2) * sub, :]

    last = nb - 1
    blk = (tb, LANES)
    cur = lambda c: pl.BlockSpec(blk, lambda p, b: (jnp.minimum(b, last), c + p))
    prev = lambda c: pl.BlockSpec(blk, lambda p, b: (jnp.clip(b - 1, 0, last), c + p))
    before = lambda c: pl.BlockSpec((sub, LANES), lambda p, b: (jnp.clip(b * nsub - 1, 0, nb * nsub - 1), c + p))
    tab = pl.BlockSpec((2, ATTN_BLOCK, 2 * ATTN_BLOCK), lambda p, b: (p, 0, 0))
    return pl.pallas_call(
        body, name=name, grid=(2, nb + 1),
        in_specs=[cur(qc), cur(kc), before(kc), cur(vc), before(vc), cur(dc), cur(dc), cur(0), tab],
        out_specs=[cur(0), prev(0), prev(0), tab],
        out_shape=[jax.ShapeDtypeStruct((T, ATTN_GROUP_WIDTH), BF16)] * 3
        + [jax.ShapeDtypeStruct((4, ATTN_BLOCK, 2 * ATTN_BLOCK), F32)],
        scratch_shapes=[pltpu.VMEM(blk, F32)] * 7,
        compiler_params=_params(("arbitrary", "arbitrary")),
    )(proj, proj, proj, proj, proj, do, corr, lse, bias)


def _mix_weights(lses):
    m = jnp.maximum(jnp.maximum(lses[0], lses[1]), lses[2])
    es = [jnp.exp(l - m) for l in lses]
    inv = 1.0 / (es[0] + es[1] + es[2])
    return jnp.concatenate([e * inv for e in es], axis=1)


def _attn_mix(os, lses, proj, tm, name):
    T = proj.shape[0]
    zcol = _OFF["z_attn"] // D_ATTN

    def body(o0, o1, o2, l0, l1, l2, z_ref, out_ref):
        z = z_ref[...]
        o = jnp.concatenate([o0[...], o1[...], o2[...]], axis=1)
        alpha = _mix_weights([l0[...], l1[...], l2[...]])
        out_ref[...] = (o * alpha * (z * _sigmoid(z))).astype(BF16)

    row = lambda i: (i, 0)
    grp = pl.BlockSpec((tm, ATTN_GROUP_WIDTH), row)
    return pl.pallas_call(
        body, name=name, grid=(T // tm,),
        in_specs=[grp] * 6 + [pl.BlockSpec((tm, D_ATTN), lambda i: (i, zcol))],
        out_specs=pl.BlockSpec((tm, D_ATTN), row),
        out_shape=jax.ShapeDtypeStruct((T, D_ATTN), BF16),
        compiler_params=_params(("parallel",)),
    )(*os, *lses, proj)


def _attn_mix_bwd(d, os, lses, proj, tm, name):
    T = proj.shape[0]
    zcol = _OFF["z_attn"] // D_ATTN

    def body(d_ref, o0, o1, o2, l0, l1, l2, z_ref, do_ref, corr_ref, dz_ref):
        dv, z = d_ref[...], z_ref[...]
        ov = jnp.concatenate([o0[...], o1[...], o2[...]], axis=1)
        alpha = _mix_weights([l0[...], l1[...], l2[...]])
        sz = _sigmoid(z)
        oc = ov * alpha
        dz_ref[...] = (dv * oc * (sz * (1.0 + z * (1.0 - sz)))).astype(BF16)
        doc = dv * (z * sz)
        do_ref[...] = doc * alpha
        pr = doc * oc
        p3 = pr[:, 0:256] + pr[:, 256:512] + pr[:, 512:768]
        li = lax.broadcasted_iota(jnp.int32, (256, 256), 0) // ATTN_HEAD_DIM
        lj = lax.broadcasted_iota(jnp.int32, (256, 256), 1) // ATTN_HEAD_DIM
        ones = jnp.where(li == lj, 1.0, 0.0).astype(F32)
        s = lax.dot_general(p3, ones, NN, precision=lax.Precision.HIGHEST, preferred_element_type=F32)
        corr_ref[...] = alpha * jnp.concatenate([s, s, s], axis=1)

    row = lambda i: (i, 0)
    grp = pl.BlockSpec((tm, ATTN_GROUP_WIDTH), row)
    return pl.pallas_call(
        body, name=name, grid=(T // tm,),
        in_specs=[pl.BlockSpec((tm, D_ATTN), row)] + [grp] * 6 + [pl.BlockSpec((tm, D_ATTN), lambda i: (i, zcol))],
        out_specs=[pl.BlockSpec((tm, D_ATTN), row)] * 3,
        out_shape=[jax.ShapeDtypeStruct((T, D_ATTN), F32), jax.ShapeDtypeStruct((T, D_ATTN), F32),
                   jax.ShapeDtypeStruct((T, D_ATTN), BF16)],
        compiler_params=_params(("parallel",)),
    )(d, *os, *lses, proj)


def _mem_probs(q_ref, kv_ref, h):
    hs = slice(MEM_HEAD_DIM * h, MEM_HEAD_DIM * (h + 1))
    qh = q_ref[:, hs].astype(BF16)
    kh = kv_ref[:, hs]
    vh = kv_ref[:, D_MEM + MEM_HEAD_DIM * h:D_MEM + MEM_HEAD_DIM * (h + 1)]
    s = _dot(qh, kh, NT) * (MEM_HEAD_DIM ** -0.5)
    p = jnp.exp(s - jnp.max(s, axis=-1, keepdims=True))
    pn = p / jnp.sum(p, axis=-1, keepdims=True)
    return qh, kh, vh, pn


def _mem_fwd(proj, kv, tm, name):
    T = proj.shape[0]
    M = kv.shape[0]
    qcol, zcol = _OFF["q_mem"] // D_MEM, _OFF["z_mem"] // D_MEM

    def body(q_ref, z_ref, kv_ref, o_ref):
        outs = []
        for h in range(MEM_HEADS):
            _, _, vh, pn = _mem_probs(q_ref, kv_ref, h)
            outs.append(_dot(pn.astype(BF16), vh))
        z = z_ref[...]
        o_ref[...] = (jnp.concatenate(outs, axis=1) * (z * _sigmoid(z))).astype(BF16)

    return pl.pallas_call(
        body, name=name, grid=(T // tm,),
        in_specs=[pl.BlockSpec((tm, D_MEM), lambda i: (i, qcol)), pl.BlockSpec((tm, D_MEM), lambda i: (i, zcol)),
                  _full((M, 2 * D_MEM))],
        out_specs=pl.BlockSpec((tm, D_MEM), lambda i: (i, 0)),
        out_shape=jax.ShapeDtypeStruct((T, D_MEM), BF16),
        compiler_params=_params(("parallel",)),
    )(proj, proj, kv)


def _mem_bwd(d, proj, kv, tm, name):
    T = proj.shape[0]
    M = kv.shape[0]
    qcol, zcol = _OFF["q_mem"] // D_MEM, _OFF["z_mem"] // D_MEM

    def body(d_ref, q_ref, z_ref, kv_ref, dq_ref, dz_ref, dkv_ref):
        @pl.when(pl.program_id(0) == 0)
        def _():
            dkv_ref[...] = jnp.zeros_like(dkv_ref)

        z = z_ref[...]
        sz = _sigmoid(z)
        dv = d_ref[...]
        dov = dv * (z * sz)
        scale = MEM_HEAD_DIM ** -0.5
        outs, dqs = [], []
        for h in range(MEM_HEADS):
            hs = slice(MEM_HEAD_DIM * h, MEM_HEAD_DIM * (h + 1))
            qh, kh, vh, pn = _mem_probs(q_ref, kv_ref, h)
            pnb = pn.astype(BF16)
            oh = _dot(pnb, vh)
            outs.append(oh)
            doh = dov[:, hs]
            dohb = doh.astype(BF16)
            dp = _dot(dohb, vh, NT)
            ds = pn * (dp - jnp.sum(doh * oh, axis=-1, keepdims=True))
            dsb = ds.astype(BF16)
            dqs.append(_dot(dsb, kh) * scale)
            dkv_ref[:, hs] += _dot(dsb, qh, TN) * scale
            vs = slice(D_MEM + MEM_HEAD_DIM * h, D_MEM + MEM_HEAD_DIM * (h + 1))
            dkv_ref[:, vs] += _dot(pnb, dohb, TN)
        dq_ref[...] = jnp.concatenate(dqs, axis=1).astype(BF16)
        dz_ref[...] = (dv * jnp.concatenate(outs, axis=1) * (sz * (1.0 + z * (1.0 - sz)))).astype(BF16)

    row = lambda i: (i, 0)
    return pl.pallas_call(
        body, name=name, grid=(T // tm,),
        in_specs=[pl.BlockSpec((tm, D_MEM), row), pl.BlockSpec((tm, D_MEM), lambda i: (i, qcol)),
                  pl.BlockSpec((tm, D_MEM), lambda i: (i, zcol)), _full((M, 2 * D_MEM))],
        out_specs=[pl.BlockSpec((tm, D_MEM), row), pl.BlockSpec((tm, D_MEM), row), _full((M, 2 * D_MEM))],
        out_shape=[jax.ShapeDtypeStruct((T, D_MEM), BF16), jax.ShapeDtypeStruct((T, D_MEM), BF16),
                   jax.ShapeDtypeStruct((M, 2 * D_MEM), F32)],
        compiler_params=_params(("arbitrary",)),
    )(d, proj, proj, kv)


def _branches_and_gates(os_ref, oa_ref, om_ref, gl_refs, bg_ref, ws_ref, wa_ref, wm_ref):
    outs = (_dot(os_ref[...], ws_ref[...]), _dot(oa_ref[...], wa_ref[...]), _dot(om_ref[...], wm_ref[...]))
    gates = tuple(_sigmoid(gl_refs[k][...] + bg_ref[:, D_MODEL * k:D_MODEL * (k + 1)]) for k in range(3))
    return outs, gates


def _merge_specs(tm):
    row = lambda i: (i, 0)
    gate = [pl.BlockSpec((tm, D_MODEL), (lambda i, k=k: (i, k))) for k in range(3)]
    return ([pl.BlockSpec((tm, D_SSM), row), pl.BlockSpec((tm, D_ATTN), row), pl.BlockSpec((tm, D_MEM), row)] + gate
            + [_full((1, N_GATES)), _full((D_SSM, D_MODEL)), _full((D_ATTN, D_MODEL)), _full((D_MEM, D_MODEL)),
               _full((D_MODEL, D_MODEL))])


def _merge_fwd(x, o_ssm, o_attn, o_mem, proj, bg, ws, wa, wm, wo, tm, name):
    T = x.shape[0]

    def body(os_ref, oa_ref, om_ref, g0, g1, g2, bg_ref, ws_ref, wa_ref, wm_ref, wo_ref, x_ref, xo_ref, mg_ref):
        outs, gates = _branches_and_gates(os_ref, oa_ref, om_ref, (g0, g1, g2), bg_ref, ws_ref, wa_ref, wm_ref)
        merged = (gates[0] * outs[0] + gates[1] * outs[1] + gates[2] * outs[2]).astype(BF16)
        mg_ref[...] = merged
        xo_ref[...] = x_ref[...] + _dot(merged, wo_ref[...])

    row = lambda i: (i, 0)
    return pl.pallas_call(
        body, name=name, grid=(T // tm,),
        in_specs=_merge_specs(tm) + [pl.BlockSpec((tm, D_MODEL), row)],
        out_specs=[pl.BlockSpec((tm, D_MODEL), row), pl.BlockSpec((tm, D_MODEL), row)],
        out_shape=[jax.ShapeDtypeStruct((T, D_MODEL), F32), jax.ShapeDtypeStruct((T, D_MODEL), BF16)],
        compiler_params=_params(("parallel",)),
    )(o_ssm, o_attn, o_mem, proj, proj, proj, bg, ws, wa, wm, wo, x)


def _merge_bwd(dx, o_ssm, o_attn, o_mem, proj, bg, ws, wa, wm, wo, tm, name, job=None):
    T = dx.shape[0]

    def body(os_ref, oa_ref, om_ref, g0, g1, g2, bg_ref, ws_ref, wa_ref, wm_ref, wo_ref, dx_ref,
             dgl_ref, db_ref, dos_ref, doa_ref, dom_ref, dbg_ref):
        @pl.when(pl.program_id(0) == 0)
        def _():
            dbg_ref[...] = jnp.zeros_like(dbg_ref)

        outs, gates = _branches_and_gates(os_ref, oa_ref, om_ref, (g0, g1, g2), bg_ref, ws_ref, wa_ref, wm_ref)
        dm = _dot(dx_ref[...].astype(BF16), wo_ref[...], NT)
        w_refs = (ws_ref, wa_ref, wm_ref)
        do_refs = (dos_ref, doa_ref, dom_ref)
        for k in range(3):
            cols = slice(D_MODEL * k, D_MODEL * (k + 1))
            dgl = dm * outs[k] * (gates[k] * (1.0 - gates[k]))
            dgl_ref[:, cols] = dgl.astype(BF16)
            dbg_ref[:, cols] += jnp.sum(dgl, axis=0, keepdims=True)
            dbk = (dm * gates[k]).astype(BF16)
            db_ref[:, cols] = dbk
            do_refs[k][...] = _dot(dbk, w_refs[k][...], NT)

    row = lambda i: (i, 0)
    return _pc(
        body, job, name=name, grid=(T // tm,),
        in_specs=_merge_specs(tm) + [pl.BlockSpec((tm, D_MODEL), row)],
        out_specs=[pl.BlockSpec((tm, N_GATES), row), pl.BlockSpec((tm, N_GATES), row), pl.BlockSpec((tm, D_SSM), row),
                   pl.BlockSpec((tm, D_ATTN), row), pl.BlockSpec((tm, D_MEM), row), _full((1, N_GATES))],
        out_shape=[jax.ShapeDtypeStruct((T, N_GATES), BF16), jax.ShapeDtypeStruct((T, N_GATES), BF16),
                   jax.ShapeDtypeStruct((T, D_SSM), F32), jax.ShapeDtypeStruct((T, D_ATTN), F32),
                   jax.ShapeDtypeStruct((T, D_MEM), F32), jax.ShapeDtypeStruct((1, N_GATES), F32)],
        scratch_shapes=[], sem=("arbitrary",), operands=(o_ssm, o_attn, o_mem, proj, proj, proj, bg, ws, wa, wm, wo, dx))


def _loss_head(x, g, target, tm, name):
    T, D = x.shape

    def body(x_ref, g_ref, t_ref, loss_ref, dx_ref, dg_ref):
        @pl.when(pl.program_id(0) == 0)
        def _():
            loss_ref[...] = jnp.zeros_like(loss_ref)
            dg_ref[...] = jnp.zeros_like(dg_ref)

        xv = x_ref[...]
        r = lax.rsqrt(jnp.mean(xv * xv, axis=-1, keepdims=True) + EPS)
        xr = xv * r
        err = xr * g_ref[...] - t_ref[...]
        loss_ref[...] += 0.5 * jnp.sum(jnp.mean(err * err, axis=-1, keepdims=True), axis=0, keepdims=True)
        dy = err * (1.0 / D)
        dg_ref[...] += jnp.sum(dy * xr, axis=0, keepdims=True)
        wv = dy * g_ref[...]
        dx_ref[...] = r * (wv - xr * jnp.mean(wv * xr, axis=-1, keepdims=True))

    row = lambda i: (i, 0)
    return pl.pallas_call(
        body, name=name, grid=(T // tm,),
        in_specs=[pl.BlockSpec((tm, D), row), _full((1, D)), pl.BlockSpec((tm, D), row)],
        out_specs=[_full((1, 128)), pl.BlockSpec((tm, D), row), _full((1, D))],
        out_shape=[jax.ShapeDtypeStruct((1, 128), F32), jax.ShapeDtypeStruct((T, D), F32),
                   jax.ShapeDtypeStruct((1, D), F32)],
        compiler_params=_params(("arbitrary",)),
    )(x, g, target)


def _adamw(parts, w, m, v, tr, name):
    L, R, C = w.shape

    def body(p_ref, w_ref, m_ref, v_ref, g_ref, d_ref, mo_ref, vo_ref):
        g = p_ref[0].astype(F32)
        for s in range(1, N_DEV):
            g = g + p_ref[s].astype(F32)
        mn = ADAM_B1 * m_ref[...] + (1.0 - ADAM_B1) * g
        vn = ADAM_B2 * v_ref[...] + (1.0 - ADAM_B2) * (g * g)
        m_hat = mn / (1.0 - ADAM_B1 ** ADAM_STEP)
        v_hat = vn / (1.0 - ADAM_B2 ** ADAM_STEP)
        g_ref[...] = g
        d_ref[...] = -ADAM_LR * (m_hat / (jnp.sqrt(v_hat) + ADAM_EPS) + ADAM_WD * w_ref[...])
        mo_ref[...] = mn
        vo_ref[...] = vn

    one = pl.BlockSpec((None, tr, C), lambda l, i: (l, i, 0))
    return pl.pallas_call(
        body, name=name, grid=(L, R // tr),
        in_specs=[pl.BlockSpec((N_DEV, None, tr, C), lambda l, i: (0, l, i, 0)), one, one, one],
        out_specs=[one] * 4,
        out_shape=[jax.ShapeDtypeStruct((L, R, C), F32)] * 4,
        compiler_params=_params(("parallel", "parallel")),
    )(parts, w, m, v)


_SHARDED = (("w_in", (1024, 1088), 1), ("w_glu", (96, 768), 0), ("w_mem_kv", (128, 1024), 0),
            ("w_br_ssm", (768, 128), 1), ("w_br_attn", (768, 128), 1), ("w_br_mem", (512, 128), 1),
            ("w_out", (128, 1024), 0))
_W_IN = 0
_SMALL = tuple(range(1, len(_SHARDED)))


class _Job(NamedTuple):
    ins: list
    out_shape: list
    aliases: dict
    pairs: Callable
    n: int


def _peers():
    x, y, c = lax.axis_index("x"), lax.axis_index("y"), lax.axis_index("c")
    me = 4 * x + 2 * y + c
    out = []
    for k in range(1, N_DEV):
        px = 1 - x if k & 4 else x
        py = 1 - y if k & 2 else y
        pc = 1 - c if k & 1 else c
        out.append(((px, py, pc), 4 * px + 2 * py + pc))
    return me, out


def _copies(pairs, send_sems, recv_sems, local_sems, arrivals):
    me, peers = _peers()
    local = [pltpu.make_async_copy(src(me), dst(me), local_sems.at[j]) for j, (src, dst) in enumerate(pairs)]
    sends, recvs = [], []
    for k, (peer, lin) in enumerate(peers):
        for j, (src, dst) in enumerate(pairs):
            for to, out in ((dst(me), sends), (dst(lin), recvs)):
                if out is sends or arrivals:
                    out.append(pltpu.make_async_remote_copy(
                        src_ref=src(lin), dst_ref=to, send_sem=send_sems.at[j, k], recv_sem=recv_sems.at[j, k],
                        device_id=peer, device_id_type=pl.DeviceIdType.MESH))
    return local, sends, recvs


def _start_copies(pairs, *sems):
    local, sends, _ = _copies(pairs, *sems, arrivals=False)
    for cp in local + sends:
        cp.start()


def _wait_copies(pairs, *sems):
    local, sends, recvs = _copies(pairs, *sems, arrivals=True)
    for cp in recvs:
        cp.wait_recv()
    for cp in sends:
        cp.wait_send()
    for cp in local:
        cp.wait()


def _job_scratch(job):
    return [pltpu.SemaphoreType.DMA((job.n, N_DEV - 1)), pltpu.SemaphoreType.DMA((job.n, N_DEV - 1)),
            pltpu.SemaphoreType.DMA((job.n,))]


def _pc(body, job, *, name, grid, in_specs, out_specs, out_shape, scratch_shapes, sem, operands):
    if job is None:
        return pl.pallas_call(body, name=name, grid=grid, in_specs=in_specs, out_specs=out_specs, out_shape=out_shape,
                              scratch_shapes=scratch_shapes, compiler_params=_params(sem))(*operands)
    a = len(in_specs)
    b = a + len(job.ins)
    c = b + len(out_shape)
    d = c + len(job.out_shape)
    e = d + len(scratch_shapes)

    def carried(*refs):
        pairs = job.pairs(refs[a:b], refs[c:d])
        ids = [pl.program_id(k) for k in range(len(grid))]
        first = functools.reduce(jnp.logical_and, [i == 0 for i in ids])
        last = functools.reduce(jnp.logical_and, [i == n - 1 for i, n in zip(ids, grid)])

        @pl.when(first)
        def _():
            _start_copies(pairs, *refs[e:])

        body(*refs[:a], *refs[b:c], *refs[d:e])

        @pl.when(last)
        def _():
            _wait_copies(pairs, *refs[e:])

    hbm = pl.BlockSpec(memory_space=pl.ANY)
    outs = pl.pallas_call(
        carried, name=name, grid=grid,
        in_specs=list(in_specs) + [hbm] * len(job.ins), out_specs=list(out_specs) + [hbm] * len(job.out_shape),
        out_shape=list(out_shape) + list(job.out_shape),
        input_output_aliases={a + i: len(out_shape) + o for i, o in job.aliases.items()},
        scratch_shapes=list(scratch_shapes) + _job_scratch(job),
        compiler_params=_params(("arbitrary",) * len(grid)),
    )(*operands, *job.ins)
    return outs[:len(out_shape)], outs[len(out_shape):]


def _exchange_only(job, name):
    n_in = len(job.ins)

    def body(*refs):
        pairs = job.pairs(refs[:n_in], refs[n_in:n_in + len(job.out_shape)])
        sems = refs[n_in + len(job.out_shape):]
        _start_copies(pairs, *sems)
        _wait_copies(pairs, *sems)

    hbm = pl.BlockSpec(memory_space=pl.ANY)
    return pl.pallas_call(
        body, name=name, in_specs=[hbm] * n_in, out_specs=[hbm] * len(job.out_shape), out_shape=list(job.out_shape),
        input_output_aliases=dict(job.aliases), scratch_shapes=_job_scratch(job),
    )(*job.ins)


def _join_jobs(p, q):
    ni, no = len(p.ins), len(p.out_shape)
    return _Job(list(p.ins) + list(q.ins), list(p.out_shape) + list(q.out_shape),
                {**p.aliases, **{ni + i: no + o for i, o in q.aliases.items()}},
                lambda ins, outs: p.pairs(ins[:ni], outs[:no]) + q.pairs(ins[ni:], outs[no:]), p.n + q.n)


def _lane_window(ref, who):
    return ref.at[:, pl.ds(pl.multiple_of(who * LANES, LANES), LANES)]


def _gather_job(shards, items):
    out_shape = []
    for i, _ in items:
        _, s, axis = _SHARDED[i]
        whole = i != _W_IN and axis == 1
        out_shape.append(jax.ShapeDtypeStruct((s[0], N_DEV * s[1]) if whole else (N_DEV,) + s, BF16))

    def pairs(in_refs, out_refs):
        out = []
        for (i, l), src, dst in zip(items, in_refs, out_refs):
            if i != _W_IN and _SHARDED[i][2] == 1:
                out.append((lambda who, src=src, l=l: src.at[l], lambda who, dst=dst: _lane_window(dst, who)))
            else:
                out.append((lambda who, src=src, l=l: src.at[l], lambda who, dst=dst: dst.at[who]))
        return out

    return _Job([shards[i] for i, _ in items], out_shape, {}, pairs, len(items))


def _landed_weights(items, landed):
    out = {}
    for (i, _), a in zip(items, landed):
        n, s, axis = _SHARDED[i]
        if i == _W_IN:
            w = a.transpose(1, 0, 2).reshape(D_MODEL, D_IN)
            out[n] = jnp.concatenate([w[:, st:st + wd] for _, wd, st in _SEGS], axis=1)
        elif axis == 0:
            out[n] = a.reshape(N_DEV * s[0], s[1])
        else:
            out[n] = a
    return out


def _scatter_job(grads, items, layer, parts=None):
    ng = len(grads)
    out_shape = [jax.ShapeDtypeStruct((N_DEV, DEPTH) + _SHARDED[i][1], BF16) for i in items]

    def pairs(in_refs, out_refs):
        out = []
        for i, src, dst in zip(items, in_refs[:ng], out_refs):
            _, s, axis = _SHARDED[i]
            if i == _W_IN:
                take = lambda who, src=src: src.at[who]
            elif axis == 0:
                take = lambda who, src=src, s=s: src.at[pl.ds(pl.multiple_of(who * s[0], 16), s[0])]
            else:
                take = lambda who, src=src: _lane_window(src, who)
            out.append((take, lambda who, dst=dst: dst.at[who, layer]))
        return out

    aliases = {} if parts is None else {ng + j: j for j in range(len(items))}
    return _Job(list(grads) + ([] if parts is None else list(parts)), out_shape, aliases, pairs, len(items))


def _rows_job(src, row0, landing=None):
    n = src.shape[0]
    pairs = lambda in_refs, out_refs: [(lambda who: in_refs[0], lambda who: out_refs[0].at[who, pl.ds(row0, n)])]
    return _Job([src] + ([] if landing is None else [landing]), [jax.ShapeDtypeStruct((N_DEV, _REP_ROWS, LANES), F32)],
                {} if landing is None else {1: 0}, pairs, 1)


_REPLICATED = (("norm_g", (2, 1024)), ("mem_norm_g", (2, 1024)), ("b_gate", (2, 3072)),
               ("ssm_lambda_re", (2, 48, 64)), ("ssm_lambda_im", (2, 48, 64)), ("ssm_log_dt", (2, 48)),
               ("ssm_b_re", (2, 48, 64, 16)), ("ssm_b_im", (2, 48, 64, 16)), ("ssm_c_re", (2, 48, 16, 64)),
               ("ssm_c_im", (2, 48, 16, 64)), ("ssm_d", (2, 768)), ("b_glu", (2, 768)), ("rel_bias", (32, 12)),
               ("final_norm_g", (1024,)))
_PER_LAYER = tuple((n, s[1:]) for n, s in _REPLICATED if s[0] == DEPTH and len(s) > 1)
_SHARED = tuple((n, s) for n, s in _REPLICATED if (n, s[1:]) not in _PER_LAYER)
_REP_HALF_ROWS = 1664
_REP_ROWS = 2 * _REP_HALF_ROWS
assert sum(int(np.prod(s)) for _, s in _PER_LAYER + _SHARED) <= _REP_HALF_ROWS * LANES


def _pack_half(tree, layer, shared):
    flat = [tree[n][layer].reshape(-1) for n, _ in _PER_LAYER]
    if shared:
        flat += [tree[n].reshape(-1) for n, _ in _SHARED]
    flat = jnp.concatenate(flat)
    return jnp.pad(flat, (0, _REP_HALF_ROWS * LANES - flat.shape[0])).reshape(_REP_HALF_ROWS, LANES)


def _pack_replicated(tree):
    return jnp.concatenate([_pack_half(tree, 1, False), _pack_half(tree, 0, True)])[None]


def _unpack_replicated(packed):
    halves = packed.reshape(2, -1)
    out, r = {}, 0
    for n, s in _PER_LAYER:
        size = int(np.prod(s))
        out[n] = jnp.stack([halves[1, r:r + size].reshape(s), halves[0, r:r + size].reshape(s)])
        r += size
    for n, s in _SHARED:
        size = int(np.prod(s))
        out[n] = halves[1, r:r + size].reshape(s)
        r += size
    return out


def _owner_rows_w_in(dw):
    order = sorted(_SEGS, key=lambda t: t[2])
    dw = jnp.concatenate([dw[:, _OFF[n]:_OFF[n] + wd] for n, wd, _ in order], axis=1)
    return dw.reshape(D_MODEL, N_DEV, D_IN // N_DEV).transpose(1, 0, 2)


def _discretize(lam_re, lam_im, log_dt, b_re, b_im):
    dt = jnp.exp(log_dt)[:, None]
    mag = jnp.exp(lam_re * dt)
    abar_re, abar_im = mag * jnp.cos(lam_im * dt), mag * jnp.sin(lam_im * dt)
    den = lam_re * lam_re + lam_im * lam_im
    nr, ni = abar_re - 1.0, abar_im
    f_re = (nr * lam_re + ni * lam_im) / den
    f_im = (ni * lam_re - nr * lam_im) / den
    bbar_re = f_re[..., None] * b_re - f_im[..., None] * b_im
    bbar_im = f_re[..., None] * b_im + f_im[..., None] * b_re
    return abar_re, abar_im, bbar_re, bbar_im


def _block_diag(a):
    _, R, C = a.shape
    a = a.reshape(SSM_BLOCKS, 8, R, C)
    eye = jnp.eye(8, dtype=a.dtype)
    return (a[:, :, :, None, :] * eye[None, :, None, :, None]).reshape(SSM_BLOCKS, 8 * R, 8 * C)


def _diag_blocks(a, R, C):
    a = a.reshape(SSM_BLOCKS, 8, R, 8, C)
    eye = jnp.eye(8, dtype=a.dtype)
    return jnp.sum(a * eye[None, :, None, :, None], axis=3).reshape(SSM_GROUPS, R, C)


def _carried(result, job):
    return (result, None) if job is None else result


def _layer_fwd(x, mem, W, P, bias, layer, jobs):
    tag = f"l{layer}"
    abar_re, abar_im, bbar_re, bbar_im = _discretize(P["ssm_lambda_re"][layer], P["ssm_lambda_im"][layer],
                                                     P["ssm_log_dt"][layer], P["ssm_b_re"][layer], P["ssm_b_im"][layer])
    c_re, c_im = P["ssm_c_re"][layer], P["ssm_c_im"][layer]
    ssm = dict(
        are=abar_re.reshape(1, N_STATE), aim=abar_im.reshape(1, N_STATE),
        bre=_block_diag(bbar_re.transpose(0, 2, 1)).astype(BF16), bim=_block_diag(bbar_im.transpose(0, 2, 1)).astype(BF16),
        cre=_block_diag(c_re.transpose(0, 2, 1)).astype(BF16), cimn=_block_diag(-c_im.transpose(0, 2, 1)).astype(BF16),
        ctre=_block_diag(c_re).astype(BF16), ctimn=_block_diag(-c_im).astype(BF16),
        btre=_block_diag(bbar_re).astype(BF16), btim=_block_diag(bbar_im).astype(BF16),
        d=P["ssm_d"][layer].reshape(1, D_SSM))
    bglu = P["b_glu"][layer].reshape(1, D_SSM)
    bgate = P["b_gate"][layer].reshape(1, N_GATES)
    g = P["norm_g"][layer].reshape(1, D_MODEL)
    gm = P["mem_norm_g"][layer].reshape(1, D_MODEL)
    delivered = {}

    def carry(stage):
        return jobs[stage][0] if stage in jobs else None

    def deliver(stage, landed):
        if landed is not None:
            delivered[stage] = _landed_weights(jobs[stage][1], landed)

    T = x.shape[0]
    (proj, h), landed = _carried(_norm_proj(x, g, W["w_in"], min(T, 1024), 2176, f"{tag}_proj", job=carry("proj")),
                                 carry("proj"))
    deliver("proj", landed)
    W = {**W, **delivered.get("proj", {})}
    (xr, xi, y, o_ssm), landed = _carried(
        _ssm_fwd(proj, ssm["bre"], ssm["bim"], ssm["cre"], ssm["cimn"], ssm["are"], ssm["aim"], ssm["d"], W["w_glu"],
                 bglu, 256, f"{tag}_ssm", job=carry("ssm")), carry("ssm"))
    deliver("ssm", landed)
    os, lses = [], []
    for grp in range(3):
        stage = f"attn{grp}"
        (o_g, lse_g), landed = _carried(_attn_fwd(proj, bias[grp], grp, f"{tag}_{stage}", job=carry(stage)), carry(stage))
        deliver(stage, landed)
        os.append(o_g)
        lses.append(lse_g)
    o_attn = _attn_mix(os, lses, proj, 512, f"{tag}_attn_mix")
    kvb, hm = _norm_proj(mem, gm, W["w_mem_kv"], mem.shape[0], 1024, f"{tag}_mem_kv", out_dtype=BF16)
    o_mem = _mem_fwd(proj, kvb, 512, f"{tag}_mem")
    x_out, merged = _merge_fwd(x, o_ssm, o_attn, o_mem, proj, bgate, W["w_br_ssm"], W["w_br_attn"], W["w_br_mem"],
                               W["w_out"], 256, f"{tag}_merge")
    res = dict(x=x, mem=mem, proj=proj, h=h, xr=xr, xi=xi, y=y, o_ssm=o_ssm, os=os, lses=lses,
               o_attn=o_attn, kvb=kvb, hm=hm, o_mem=o_mem, merged=merged, ssm=ssm, bglu=bglu,
               bgate=bgate, g=g, gm=gm, W=W)
    return x_out, res, delivered


def _layer_bwd(dx, res, P, bias, layer, jobs):
    tag = f"l{layer}b"
    proj, ssm, W = res["proj"], res["ssm"], res["W"]
    T = dx.shape[0]
    landed = {}

    def run(stage, fn, job):
        out, landed[stage] = _carried(fn(job), job)
        if job is None:
            del landed[stage]
        return out

    dgl, dbr, do_ssm, do_attn, do_mem, dbg = run(
        "merge", lambda job: _merge_bwd(dx, res["o_ssm"], res["o_attn"], res["o_mem"], proj, res["bgate"], W["w_br_ssm"],
                                        W["w_br_attn"], W["w_br_mem"], W["w_out"], 256, f"{tag}_merge", job=job),
        jobs.get("merge"))
    gw = {}
    gw["w_out"] = _mm_tn(res["merged"], dx, 1024, 1024, 512, f"{tag}_dw_out")
    gw["w_br_ssm"] = _mm_tn(res["o_ssm"], dbr, 768, 1024, 512, f"{tag}_dw_br_ssm", b_col=0, n=1024)
    gw["w_br_attn"] = _mm_tn(res["o_attn"], dbr, 768, 1024, 512, f"{tag}_dw_br_attn", b_col=1024, n=1024)
    gw["w_br_mem"] = _mm_tn(res["o_mem"], dbr, 512, 1024, 512, f"{tag}_dw_br_mem", b_col=2048, n=1024)

    dqm, dzm, dkv = _mem_bwd(do_mem, proj, res["kvb"], 512, f"{tag}_mem")
    M = dkv.shape[0]
    gw["w_mem_kv"] = _mm_tn(res["hm"], dkv, 1024, 1024, M, f"{tag}_dw_mem_kv")
    _, dgm = _proj_bwd(dkv.astype(BF16), W["w_mem_kv"], res["mem"], res["gm"], jnp.zeros_like(res["mem"]), M, 1024,
                       f"{tag}_mem_norm")

    do_g, corr, dza = _attn_mix_bwd(do_attn, res["os"], res["lses"], proj, 512, f"{tag}_attn_mix")
    dqs, dks, dvs, dbs = [], [], [], []
    for grp in range(3):
        dq_g, dk_g, dv_g, db_g = _attn_bwd(proj, do_g, corr, res["lses"][grp], bias[grp], grp, f"{tag}_attn{grp}")
        dqs.append(dq_g)
        dks.append(dk_g)
        dvs.append(dv_g)
        dbs.append(db_g)
    dbias = jnp.stack(dbs)

    dy, dzs, gelu_b, dt_b, dbglu = _glu_bwd(do_ssm, res["y"], proj, W["w_glu"], res["bglu"], 512, f"{tag}_glu")
    gw["w_glu"] = _mm_tn(gelu_b, dt_b, 768, 768, 512, f"{tag}_dw_glu")
    du, dbre, dbim, dcre, dcim, dare, daim, dd = run(
        "ssm", lambda job: _ssm_bwd(dy, proj, res["xr"], res["xi"], ssm["ctre"], ssm["ctimn"], ssm["btre"], ssm["btim"],
                                    ssm["are"], ssm["aim"], ssm["d"], 256, f"{tag}_ssm", job=job), jobs.get("ssm"))
    _, disc_vjp = jax.vjp(_discretize, P["ssm_lambda_re"][layer], P["ssm_lambda_im"][layer], P["ssm_log_dt"][layer],
                          P["ssm_b_re"][layer], P["ssm_b_im"][layer])
    d_lre, d_lim, d_ldt, d_bre, d_bim = disc_vjp((dare.reshape(SSM_GROUPS, SSM_STATE), daim.reshape(SSM_GROUPS, SSM_STATE),
                                                  _diag_blocks(dbre, SSM_STATE, SSM_GROUP),
                                                  _diag_blocks(dbim, SSM_STATE, SSM_GROUP)))

    small = [gw[_SHARDED[i][0]] for i in _SMALL]
    dproj = jnp.concatenate([dgl, du, dzs] + dqs + dks + dvs + [dza, dqm, dzm], axis=1)
    dw_in = run("dw_in", lambda job: _mm_tn(res["h"], dproj, 1024, 2176, 512, f"{tag}_dw_in", job=job),
                jobs["dw_in"](small) if "dw_in" in jobs else None)
    dx_in, dg = run("proj", lambda job: _proj_bwd(dproj, W["w_in"], res["x"], res["g"], dx, min(T, 512), 2176,
                                                  f"{tag}_proj", job=job),
                    jobs["proj"](small) if "proj" in jobs else None)

    gp = dict(norm_g=dg[0], mem_norm_g=dgm[0], b_gate=dbg[0], ssm_lambda_re=d_lre, ssm_lambda_im=d_lim,
              ssm_log_dt=d_ldt, ssm_b_re=d_bre, ssm_b_im=d_bim,
              ssm_c_re=_diag_blocks(dcre, SSM_GROUP, SSM_STATE), ssm_c_im=_diag_blocks(dcim, SSM_GROUP, SSM_STATE),
              ssm_d=dd[0], b_glu=dbglu[0])
    return dx_in, _owner_rows_w_in(dw_in), gp, dbias, landed


def _train_step(x, mem, target, shards, P):
    rest0 = [(i, 0) for i in _SMALL]
    rest1 = [(i, 1) for i in _SMALL]
    first = [(_W_IN, 0)]
    W0 = _landed_weights(first, _exchange_only(_gather_job(shards, first), "gather_w_in0"))
    buckets = _bucket_tables()
    bias = _bias_tables(P["rel_bias"], buckets, "bias_tables")
    jobs0 = {"proj": (_gather_job(shards, rest0), rest0), "ssm": (_gather_job(shards, [(_W_IN, 1)]), [(_W_IN, 1)]),
             "attn0": (_gather_job(shards, rest1), rest1)}
    x, res0, delivered = _layer_fwd(x, mem, W0, P, bias, 0, jobs0)
    x, res1, _ = _layer_fwd(x, mem, {**delivered["ssm"], **delivered["attn0"]}, P, bias, 1, {})
    loss, dx, dgf = _loss_head(x, P["final_norm_g"].reshape(1, D_MODEL), target, 512, "loss_head")

    dx, dw_in1, gp1, dbias1, landed1 = _layer_bwd(
        dx, res1, P, bias, 1, {"proj": lambda small: _scatter_job(small, _SMALL, 1)})
    rep1 = _pack_half({n: a[None] for n, a in gp1.items()}, 0, False)
    dx, dw_in0, gp0, dbias0, landed0 = _layer_bwd(
        dx, res0, P, bias, 0,
        {"merge": _rows_job(rep1, 0), "ssm": _scatter_job([dw_in1], [_W_IN], 1),
         "dw_in": lambda small: _scatter_job(small, _SMALL, 0, parts=landed1["proj"])})
    d_rel = _bias_grad(dbias0, dbias1, buckets, "bias_grad")
    gp0 = {n: a[None] for n, a in gp0.items()}
    gp0["rel_bias"] = jnp.sum(d_rel, axis=-1).transpose(2, 0, 1).reshape(NUM_BUCKETS, 12)
    gp0["final_norm_g"] = dgf[0]
    rep0 = _pack_half(gp0, 0, True)
    parts_w_in, rparts = _exchange_only(
        _join_jobs(_scatter_job([dw_in0], [_W_IN], 0, parts=landed0["ssm"]),
                   _rows_job(rep0, _REP_HALF_ROWS, landing=landed0["merge"][0])), "exchange_tail")
    return loss[0, 0], dx, [parts_w_in] + list(landed0["dw_in"]), rparts


_WEIGHTS = ["norm_g", "mem_norm_g", "w_in", "b_gate", "ssm_lambda_re", "ssm_lambda_im", "ssm_log_dt", "ssm_b_re",
            "ssm_b_im", "ssm_c_re", "ssm_c_im", "ssm_d", "w_glu", "b_glu", "w_mem_kv", "w_br_ssm", "w_br_attn",
            "w_br_mem", "w_out", "rel_bias", "final_norm_g"]
_ADAM_ROWS = {"w_in": 128, "w_glu": 96, "w_mem_kv": 128, "w_br_ssm": 768, "w_br_attn": 768, "w_br_mem": 512,
              "w_out": 128}


def kernel(x, mem, norm_g, mem_norm_g, w_in, b_gate, ssm_lambda_re, ssm_lambda_im, ssm_log_dt, ssm_b_re, ssm_b_im, ssm_c_re, ssm_c_im, ssm_d, w_glu, b_glu, w_mem_kv, w_br_ssm, w_br_attn, w_br_mem, w_out, rel_bias, final_norm_g, loss_target, m_norm_g, m_mem_norm_g, m_w_in, m_b_gate, m_ssm_lambda_re, m_ssm_lambda_im, m_ssm_log_dt, m_ssm_b_re, m_ssm_b_im, m_ssm_c_re, m_ssm_c_im, m_ssm_d, m_w_glu, m_b_glu, m_w_mem_kv, m_w_br_ssm, m_w_br_attn, m_w_br_mem, m_w_out, m_rel_bias, m_final_norm_g, v_norm_g, v_mem_norm_g, v_w_in, v_b_gate, v_ssm_lambda_re, v_ssm_lambda_im, v_ssm_log_dt, v_ssm_b_re, v_ssm_b_im, v_ssm_c_re, v_ssm_c_im, v_ssm_d, v_w_glu, v_b_glu, v_w_mem_kv, v_w_br_ssm, v_w_br_attn, v_w_br_mem, v_w_out, v_rel_bias, v_final_norm_g):
    given = dict(locals())
    w = {n: given[n] for n in _WEIGHTS}
    m = {n: given["m_" + n] for n in _WEIGHTS}
    v = {n: given["v_" + n] for n in _WEIGHTS}

    shards = [w[n].astype(BF16) for n, _, _ in _SHARDED]
    loss, dx, parts, rparts = _train_step(x[0], mem[0], loss_target[0], shards, w)
    loss = lax.psum(loss, ("x", "y", "c"))

    new = {}
    for (n, _, _), p in zip(_SHARDED, parts):
        new[n] = _adamw(p, w[n], m[n], v[n], _ADAM_ROWS[n], f"adamw_{n}")
    rp = [_unpack_replicated(a) for a in _adamw(rparts[:, None], _pack_replicated(w), _pack_replicated(m),
                                                _pack_replicated(v), _REP_ROWS // 4, "adamw_replicated")]
    for n, _ in _REPLICATED:
        new[n] = [rp[kind][n] for kind in range(4)]
    outs = [loss, dx[None]]
    for kind in range(4):
        outs.extend(new[n][kind] for n in _WEIGHTS)
    return tuple(outs)
```

```python
import functools
import math
from typing import Callable, NamedTuple

import jax
import jax.numpy as jnp
import numpy as np
from jax import lax
from jax.experimental import pallas as pl
from jax.experimental.pallas import tpu as pltpu

F32 = jnp.float32
BF16 = jnp.bfloat16

D_MODEL = 1024
DEPTH = 2
EPS = 1e-6
D_SSM = 768
SSM_GROUP = 16
SSM_GROUPS = 48
SSM_STATE = 64
N_STATE = SSM_GROUPS * SSM_STATE
SSM_BLOCKS = 6
D_ATTN = 768
ATTN_HEAD_DIM = 64
ATTN_GROUP_WIDTH = 256
ATTN_DILATIONS = (1, 4, 16)
ATTN_SPAN = 128
ATTN_BLOCK = 128
NUM_BUCKETS = 32
REL_MAX_DISTANCE = 2048
NEG_INF = -1e30
MEM_HEADS = 4
MEM_HEAD_DIM = 128
D_MEM = 512
N_GATES = 3 * D_MODEL
D_IN = 8704
N_DEV = 8
LANES = 128
ADAM_LR = 0.001
ADAM_B1 = 0.9
ADAM_B2 = 0.999
ADAM_EPS = 1e-08
ADAM_WD = 0.01
ADAM_STEP = 10

_SEGS = (("gates", 3072, 5632), ("u", 768, 0), ("z_ssm", 768, 768), ("q", 768, 1536), ("k", 768, 2304),
         ("v", 768, 3072), ("z_attn", 768, 3840), ("q_mem", 512, 4608), ("z_mem", 512, 5120))
_OFF = {}
_o = 0
for _n, _w, _s in _SEGS:
    _OFF[_n] = _o
    _o += _w

NN = (((1,), (0,)), ((), ()))
NT = (((1,), (1,)), ((), ()))
TN = (((0,), (0,)), ((), ()))

VMEM_LIMIT = 56 * 1024 * 1024


def _dot(a, b, dims=NN):
    return lax.dot_general(a, b, dims, preferred_element_type=F32)


def _sigmoid(x):
    return 1.0 / (1.0 + jnp.exp(-x))


def _gelu_parts(x):
    k = math.sqrt(2.0 / math.pi)
    t = jnp.tanh(k * (x + 0.044715 * (x * x * x)))
    cdf = 0.5 * (1.0 + t)
    dcdf = 0.5 * (1.0 - t * t) * k * (1.0 + 3.0 * 0.044715 * (x * x))
    return x * cdf, cdf + x * dcdf


def _params(sem, vmem=VMEM_LIMIT):
    return pltpu.CompilerParams(dimension_semantics=sem, vmem_limit_bytes=vmem)


def _full(shape):
    return pl.BlockSpec(shape, lambda *_: (0,) * len(shape))


def _norm_proj(x, g, w, tm, tn, name, out_dtype=F32, job=None):
    T, D = x.shape
    N = w.shape[1]

    def body(x_ref, g_ref, w_ref, o_ref, h_ref, hs):
        @pl.when(pl.program_id(1) == 0)
        def _():
            xv = x_ref[...]
            r = lax.rsqrt(jnp.mean(xv * xv, axis=-1, keepdims=True) + EPS)
            hv = (xv * r * g_ref[...]).astype(BF16)
            hs[...] = hv
            h_ref[...] = hv

        o_ref[...] = _dot(hs[...], w_ref[...]).astype(out_dtype)

    return _pc(
        body, job, name=name, grid=(T // tm, N // tn),
        in_specs=[pl.BlockSpec((tm, D), lambda i, j: (i, 0)), _full((1, D)),
                  pl.BlockSpec((D, tn), lambda i, j: (0, j))],
        out_specs=[pl.BlockSpec((tm, tn), lambda i, j: (i, j)), pl.BlockSpec((tm, D), lambda i, j: (i, 0))],
        out_shape=[jax.ShapeDtypeStruct((T, N), out_dtype), jax.ShapeDtypeStruct((T, D), BF16)],
        scratch_shapes=[pltpu.VMEM((tm, D), BF16)], sem=("parallel", "arbitrary"), operands=(x, g, w))


def _mm_tn(a, b, tm, tn, tk, name, b_col=0, n=None, job=None):
    K, M = a.shape
    N = b.shape[1] if n is None else n
    nk = K // tk
    j0 = b_col // tn

    def body(a_ref, b_ref, o_ref, acc):
        k = pl.program_id(2)

        @pl.when(k == 0)
        def _():
            acc[...] = jnp.zeros_like(acc)

        acc[...] += _dot(a_ref[...].astype(BF16), b_ref[...].astype(BF16), TN)

        @pl.when(k == nk - 1)
        def _():
            o_ref[...] = acc[...].astype(BF16)

    out = _pc(
        body, job, name=name, grid=(M // tm, N // tn, nk),
        in_specs=[pl.BlockSpec((tk, tm), lambda i, j, k: (k, i)), pl.BlockSpec((tk, tn), lambda i, j, k: (k, j0 + j))],
        out_specs=[pl.BlockSpec((tm, tn), lambda i, j, k: (i, j))],
        out_shape=[jax.ShapeDtypeStruct((M, N), BF16)],
        scratch_shapes=[pltpu.VMEM((tm, tn), F32)], sem=("parallel", "parallel", "arbitrary"), operands=(a, b))
    return out[0] if job is None else (out[0][0], out[1])


def _proj_bwd(dp, w, x, g, dres, tm, tk, name, job=None):
    T, N = dp.shape
    D = w.shape[0]
    nk = N // tk

    def body(dp_ref, w_ref, x_ref, g_ref, dres_ref, dx_ref, dg_ref, acc):
        i, k = pl.program_id(0), pl.program_id(1)

        @pl.when(k == 0)
        def _():
            acc[...] = jnp.zeros_like(acc)

        @pl.when((i == 0) & (k == 0))
        def _():
            dg_ref[...] = jnp.zeros_like(dg_ref)

        acc[...] += _dot(dp_ref[...], w_ref[...], NT)

        @pl.when(k == nk - 1)
        def _():
            xv = x_ref[...]
            dh = acc[...]
            r = lax.rsqrt(jnp.mean(xv * xv, axis=-1, keepdims=True) + EPS)
            xr = xv * r
            dg_ref[...] += jnp.sum(dh * xr, axis=0, keepdims=True)
            wv = dh * g_ref[...]
            dx_ref[...] = dres_ref[...] + r * (wv - xr * jnp.mean(wv * xr, axis=-1, keepdims=True))

    return _pc(
        body, job, name=name, grid=(T // tm, nk),
        in_specs=[pl.BlockSpec((tm, tk), lambda i, k: (i, k)), pl.BlockSpec((D, tk), lambda i, k: (0, k)),
                  pl.BlockSpec((tm, D), lambda i, k: (i, 0)), _full((1, D)),
                  pl.BlockSpec((tm, D), lambda i, k: (i, 0))],
        out_specs=[pl.BlockSpec((tm, D), lambda i, k: (i, 0)), _full((1, D))],
        out_shape=[jax.ShapeDtypeStruct((T, D), F32), jax.ShapeDtypeStruct((1, D), F32)],
        scratch_shapes=[pltpu.VMEM((tm, D), F32)], sem=("arbitrary", "arbitrary"), operands=(dp, w, x, g, dres))


def _ssm_fwd(proj, bre, bim, cre, cimn, are, aim, d, wglu, bglu, tc, name, job=None):
    T = proj.shape[0]
    ucol, zcol = _OFF["u"] // D_SSM, _OFF["z_ssm"] // D_SSM

    def body(u_ref, z_ref, bre_ref, bim_ref, cre_ref, cim_ref, are_ref, aim_ref, d_ref, wg_ref, bg_ref,
             xr_ref, xi_ref, y_ref, o_ref, car_r, car_i):
        @pl.when(pl.program_id(0) == 0)
        def _():
            car_r[...] = jnp.zeros_like(car_r)
            car_i[...] = jnp.zeros_like(car_i)

        u = u_ref[...]
        ub = u.astype(BF16)
        for k in range(SSM_BLOCKS):
            uk = ub[:, 128 * k:128 * (k + 1)]
            xr_ref[:, 512 * k:512 * (k + 1)] = _dot(uk, bre_ref[k])
            xi_ref[:, 512 * k:512 * (k + 1)] = _dot(uk, bim_ref[k])
        ar, ai = are_ref[...], aim_ref[...]

        def step(t, c):
            pr, pi = c
            nr = ar * pr - ai * pi + xr_ref[pl.ds(t, 1), :]
            ni = ar * pi + ai * pr + xi_ref[pl.ds(t, 1), :]
            xr_ref[pl.ds(t, 1), :] = nr
            xi_ref[pl.ds(t, 1), :] = ni
            return nr, ni

        pr, pi = lax.fori_loop(0, tc, step, (car_r[...], car_i[...]))
        car_r[...] = pr
        car_i[...] = pi

        ys = []
        for k in range(SSM_BLOCKS):
            xrk = xr_ref[:, 512 * k:512 * (k + 1)].astype(BF16)
            xik = xi_ref[:, 512 * k:512 * (k + 1)].astype(BF16)
            ys.append(_dot(xrk, cre_ref[k]) + _dot(xik, cim_ref[k]))
        y = jnp.concatenate(ys, axis=1) + d_ref[...] * u
        y_ref[...] = y
        gl, _ = _gelu_parts(y)
        t = _dot(gl.astype(BF16), wg_ref[...]) + bg_ref[...]
        z = z_ref[...]
        o_ref[...] = (gl * _sigmoid(t) * (z * _sigmoid(z))).astype(BF16)

    return _pc(
        body, job, name=name, grid=(T // tc,),
        in_specs=[pl.BlockSpec((tc, D_SSM), lambda i: (i, ucol)), pl.BlockSpec((tc, D_SSM), lambda i: (i, zcol)),
                  _full((SSM_BLOCKS, 128, 512)), _full((SSM_BLOCKS, 128, 512)),
                  _full((SSM_BLOCKS, 512, 128)), _full((SSM_BLOCKS, 512, 128)),
                  _full((1, N_STATE)), _full((1, N_STATE)), _full((1, D_SSM)),
                  _full((D_SSM, D_SSM)), _full((1, D_SSM))],
        out_specs=[pl.BlockSpec((tc, N_STATE), lambda i: (i, 0)), pl.BlockSpec((tc, N_STATE), lambda i: (i, 0)),
                   pl.BlockSpec((tc, D_SSM), lambda i: (i, 0)), pl.BlockSpec((tc, D_SSM), lambda i: (i, 0))],
        out_shape=[jax.ShapeDtypeStruct((T, N_STATE), F32), jax.ShapeDtypeStruct((T, N_STATE), F32),
                   jax.ShapeDtypeStruct((T, D_SSM), F32), jax.ShapeDtypeStruct((T, D_SSM), BF16)],
        scratch_shapes=[pltpu.VMEM((1, N_STATE), F32), pltpu.VMEM((1, N_STATE), F32)], sem=("arbitrary",),
        operands=(proj, proj, bre, bim, cre, cimn, are, aim, d, wglu, bglu))


def _glu_bwd(do, y, proj, wglu, bglu, tm, name):
    T = y.shape[0]
    zcol = _OFF["z_ssm"] // D_SSM

    def body(do_ref, y_ref, z_ref, wg_ref, bg_ref, dy_ref, dz_ref, g_ref, dt_ref, db_ref):
        @pl.when(pl.program_id(0) == 0)
        def _():
            db_ref[...] = jnp.zeros_like(db_ref)

        dov = do_ref[...]
        gl, dgl = _gelu_parts(y_ref[...])
        glb = gl.astype(BF16)
        sg = _sigmoid(_dot(glb, wg_ref[...]) + bg_ref[...])
        z = z_ref[...]
        sz = _sigmoid(z)
        dz_ref[...] = (dov * (gl * sg) * (sz * (1.0 + z * (1.0 - sz)))).astype(BF16)
        dy2 = dov * (z * sz)
        dt = dy2 * gl * (sg * (1.0 - sg))
        dtb = dt.astype(BF16)
        dg = dy2 * sg + _dot(dtb, wg_ref[...], NT)
        dy_ref[...] = dg * dgl
        g_ref[...] = glb
        dt_ref[...] = dtb
        db_ref[...] += jnp.sum(dt, axis=0, keepdims=True)

    row = lambda i: (i, 0)
    return pl.pallas_call(
        body, name=name, grid=(T // tm,),
        in_specs=[pl.BlockSpec((tm, D_SSM), row), pl.BlockSpec((tm, D_SSM), row),
                  pl.BlockSpec((tm, D_SSM), lambda i: (i, zcol)), _full((D_SSM, D_SSM)), _full((1, D_SSM))],
        out_specs=[pl.BlockSpec((tm, D_SSM), row)] * 4 + [_full((1, D_SSM))],
        out_shape=[jax.ShapeDtypeStruct((T, D_SSM), F32), jax.ShapeDtypeStruct((T, D_SSM), BF16),
                   jax.ShapeDtypeStruct((T, D_SSM), BF16), jax.ShapeDtypeStruct((T, D_SSM), BF16),
                   jax.ShapeDtypeStruct((1, D_SSM), F32)],
        compiler_params=_params(("arbitrary",)),
    )(do, y, proj, wglu, bglu)


def _ssm_bwd(dy, proj, xr, xi, ctre, ctimn, btre, btim, are, aim, d, tc, name, job=None):
    T = dy.shape[0]
    nc = T // tc
    ucol = _OFF["u"] // D_SSM
    rb = tc // 8

    def body(dy_ref, u_ref, xr_ref, xi_ref, xpr_ref, xpi_ref, ctre_ref, ctim_ref, btre_ref, btim_ref,
             are_ref, aim_ref, d_ref,
             du_ref, dbre_ref, dbim_ref, dcre_ref, dcim_ref, dare_ref, daim_ref, dd_ref, gr, gi, car_r, car_i):
        i = pl.program_id(0)

        @pl.when(i == 0)
        def _():
            for ref in (car_r, car_i, dbre_ref, dbim_ref, dcre_ref, dcim_ref, dare_ref, daim_ref, dd_ref):
                ref[...] = jnp.zeros_like(ref)

        dyv = dy_ref[...]
        dyb = dyv.astype(BF16)
        u = u_ref[...]
        ub = u.astype(BF16)
        for k in range(SSM_BLOCKS):
            dk = dyb[:, 128 * k:128 * (k + 1)]
            gr[:, 512 * k:512 * (k + 1)] = _dot(dk, ctre_ref[k])
            gi[:, 512 * k:512 * (k + 1)] = _dot(dk, ctim_ref[k])
        ar, ai = are_ref[...], aim_ref[...]

        def step(s, c):
            pr, pi = c
            t = tc - 1 - s
            nr = gr[pl.ds(t, 1), :] + ar * pr + ai * pi
            ni = gi[pl.ds(t, 1), :] + ar * pi - ai * pr
            gr[pl.ds(t, 1), :] = nr
            gi[pl.ds(t, 1), :] = ni
            return nr, ni

        pr, pi = lax.fori_loop(0, tc, step, (car_r[...], car_i[...]))
        car_r[...] = pr
        car_i[...] = pi

        xrv, xiv = xr_ref[...], xi_ref[...]
        keep = jnp.where(i == nc - 1, 0.0, 1.0)
        row0 = lax.broadcasted_iota(jnp.int32, (tc, 1), 0) == 0
        xsr = jnp.where(row0, xpr_ref[7:8, :] * keep, pltpu.roll(xrv, 1, axis=0))
        xsi = jnp.where(row0, xpi_ref[7:8, :] * keep, pltpu.roll(xiv, 1, axis=0))
        grv, giv = gr[...], gi[...]
        dare_ref[...] += jnp.sum(grv * xsr + giv * xsi, axis=0, keepdims=True)
        daim_ref[...] += jnp.sum(giv * xsr - grv * xsi, axis=0, keepdims=True)
        dd_ref[...] += jnp.sum(dyv * u, axis=0, keepdims=True)

        dus = []
        for k in range(SSM_BLOCKS):
            sl = slice(512 * k, 512 * (k + 1))
            ch = slice(128 * k, 128 * (k + 1))
            grb, gib = grv[:, sl].astype(BF16), giv[:, sl].astype(BF16)
            dus.append(_dot(grb, btre_ref[k]) + _dot(gib, btim_ref[k]))
            dbre_ref[k] += _dot(grb, ub[:, ch], TN)
            dbim_ref[k] += _dot(gib, ub[:, ch], TN)
            dcre_ref[k] += _dot(dyb[:, ch], xrv[:, sl].astype(BF16), TN)
            dcim_ref[k] -= _dot(dyb[:, ch], xiv[:, sl].astype(BF16), TN)
        du_ref[...] = (jnp.concatenate(dus, axis=1) + d_ref[...] * dyv).astype(BF16)

    rev = lambda i: (nc - 1 - i, 0)
    prev = lambda i: (jnp.maximum((nc - 1 - i) * rb - 1, 0), 0)
    return _pc(
        body, job, name=name, grid=(nc,),
        in_specs=[pl.BlockSpec((tc, D_SSM), rev), pl.BlockSpec((tc, D_SSM), lambda i: (nc - 1 - i, ucol)),
                  pl.BlockSpec((tc, N_STATE), rev), pl.BlockSpec((tc, N_STATE), rev),
                  pl.BlockSpec((8, N_STATE), prev), pl.BlockSpec((8, N_STATE), prev),
                  _full((SSM_BLOCKS, 128, 512)), _full((SSM_BLOCKS, 128, 512)),
                  _full((SSM_BLOCKS, 512, 128)), _full((SSM_BLOCKS, 512, 128)),
                  _full((1, N_STATE)), _full((1, N_STATE)), _full((1, D_SSM))],
        out_specs=[pl.BlockSpec((tc, D_SSM), rev),
                   _full((SSM_BLOCKS, 512, 128)), _full((SSM_BLOCKS, 512, 128)),
                   _full((SSM_BLOCKS, 128, 512)), _full((SSM_BLOCKS, 128, 512)),
                   _full((1, N_STATE)), _full((1, N_STATE)), _full((1, D_SSM))],
        out_shape=[jax.ShapeDtypeStruct((T, D_SSM), BF16),
                   jax.ShapeDtypeStruct((SSM_BLOCKS, 512, 128), F32), jax.ShapeDtypeStruct((SSM_BLOCKS, 512, 128), F32),
                   jax.ShapeDtypeStruct((SSM_BLOCKS, 128, 512), F32), jax.ShapeDtypeStruct((SSM_BLOCKS, 128, 512), F32),
                   jax.ShapeDtypeStruct((1, N_STATE), F32), jax.ShapeDtypeStruct((1, N_STATE), F32),
                   jax.ShapeDtypeStruct((1, D_SSM), F32)],
        scratch_shapes=[pltpu.VMEM((tc, N_STATE), F32), pltpu.VMEM((tc, N_STATE), F32),
                        pltpu.VMEM((1, N_STATE), F32), pltpu.VMEM((1, N_STATE), F32)], sem=("arbitrary",),
        operands=(dy, proj, xr, xi, xr, xi, ctre, ctimn, btre, btim, are, aim, d))


def _rel_bucket(dist):
    n = jnp.maximum(dist, 0)
    max_exact = NUM_BUCKETS // 2
    n_f = jnp.maximum(n, 1).astype(F32)
    large = max_exact + (jnp.log(n_f / max_exact) / math.log(REL_MAX_DISTANCE / max_exact)
                         * (NUM_BUCKETS - max_exact)).astype(jnp.int32)
    large = jnp.minimum(large, NUM_BUCKETS - 1)
    return jnp.where(n < max_exact, n, large)


def _bucket_tables():
    qi = jnp.arange(ATTN_BLOCK)[:, None]
    kj = jnp.arange(2 * ATTN_BLOCK)[None, :]
    delta = jnp.maximum(ATTN_BLOCK + qi - kj, 0)
    return jnp.stack([_rel_bucket(delta * r) for r in ATTN_DILATIONS]).astype(jnp.int32)


def _bias_tables(rel_bias, buckets, name):
    def body(tab_ref, bk_ref, o_ref):
        g = pl.program_id(0)
        bk = bk_ref[...]
        qi = lax.broadcasted_iota(jnp.int32, bk.shape, 0)
        kj = lax.broadcasted_iota(jnp.int32, bk.shape, 1)
        delta = ATTN_BLOCK + qi - kj
        band = (delta >= 0) & (delta <= ATTN_SPAN)
        accs = [jnp.zeros(bk.shape, F32) for _ in range(4)]
        for b in range(NUM_BUCKETS):
            hit = bk == b
            for h in range(4):
                accs[h] = jnp.where(hit, tab_ref[b, 4 * g + h], accs[h])
        for h in range(4):
            o_ref[h] = jnp.where(band, accs[h], NEG_INF)

    return pl.pallas_call(
        body, name=name, grid=(3,),
        in_specs=[pl.BlockSpec(memory_space=pltpu.SMEM),
                  pl.BlockSpec((None, ATTN_BLOCK, 2 * ATTN_BLOCK), lambda g: (g, 0, 0))],
        out_specs=pl.BlockSpec((None, 4, ATTN_BLOCK, 2 * ATTN_BLOCK), lambda g: (g, 0, 0, 0)),
        out_shape=jax.ShapeDtypeStruct((3, 4, ATTN_BLOCK, 2 * ATTN_BLOCK), F32),
        compiler_params=_params(("parallel",)),
    )(rel_bias, buckets)


def _bias_grad(db0, db1, buckets, name):
    def body(a_ref, b_ref, bk_ref, o_ref):
        bk = bk_ref[...]
        for h in range(4):
            dv = a_ref[h] + b_ref[h]
            for b in range(NUM_BUCKETS):
                o_ref[h, b:b + 1, :] = jnp.sum(jnp.where(bk == b, dv, 0.0), axis=0, keepdims=True)

    tab = pl.BlockSpec((None, 4, ATTN_BLOCK, 2 * ATTN_BLOCK), lambda g: (g, 0, 0, 0))
    return pl.pallas_call(
        body, name=name, grid=(3,),
        in_specs=[tab, tab, pl.BlockSpec((None, ATTN_BLOCK, 2 * ATTN_BLOCK), lambda g: (g, 0, 0))],
        out_specs=pl.BlockSpec((None, 4, NUM_BUCKETS, 2 * ATTN_BLOCK), lambda g: (g, 0, 0, 0)),
        out_shape=jax.ShapeDtypeStruct((3, 4, NUM_BUCKETS, 2 * ATTN_BLOCK), F32),
        compiler_params=_params(("parallel",)),
    )(db0, db1, buckets)


_ATTN_SUB = {1: 4, 4: 1, 16: 1}
_UNROLL = 4


def _unit_rows(j, s, r):
    start = j * ATTN_BLOCK * r + s
    return pl.ds(start, ATTN_BLOCK, stride=r) if r > 1 else pl.ds(start, ATTN_BLOCK)


def _for_units(r, nsub, fn, after):
    if r * nsub <= _UNROLL:
        after([fn(j, s) for j in range(nsub) for s in range(r)])
    else:
        def four(i, c):
            after([fn(0, _UNROLL * i + k) for k in range(_UNROLL)])
            return c

        lax.fori_loop(0, r // _UNROLL, four, 0)


def _attn_cols(g):
    return tuple((_OFF[n] + ATTN_GROUP_WIDTH * g) // LANES for n in ("q", "k", "v"))


def _attn_fwd(proj, bias, g, name, job=None):
    r = ATTN_DILATIONS[g]
    nsub = _ATTN_SUB[r]
    T = proj.shape[0]
    sub = ATTN_BLOCK * r
    tb = sub * nsub
    qc, kc, vc = _attn_cols(g)
    scale = ATTN_HEAD_DIM ** -0.5

    def body(q_ref, kc_ref, kp_ref, vc_ref, vp_ref, bias_ref, o_ref, lse_ref):
        lane = lax.broadcasted_iota(jnp.int32, (ATTN_BLOCK, LANES), 1)
        kj = lax.broadcasted_iota(jnp.int32, (ATTN_BLOCK, 2 * ATTN_BLOCK), 1)
        dead = (pl.program_id(0) == 0) & (kj < ATTN_BLOCK)

        def one(j, s):
            rows = _unit_rows(j, s, r)
            before = _unit_rows(max(j - 1, 0), s, r)
            k_before = kc_ref[before, :] if j else kp_ref[before, :]
            v_before = vc_ref[before, :] if j else vp_ref[before, :]
            q = q_ref[rows, :]
            kcat = jnp.concatenate([k_before, kc_ref[rows, :]], axis=0).astype(BF16)
            vcat = jnp.concatenate([v_before, vc_ref[rows, :]], axis=0).astype(BF16)
            o_acc = jnp.zeros((ATTN_BLOCK, LANES), F32)
            l_acc = jnp.zeros((ATTN_BLOCK, LANES), F32)
            for hh in range(2):
                mine = (lane >= ATTN_HEAD_DIM) if hh else (lane < ATTN_HEAD_DIM)
                qm = jnp.where(mine, q, 0.0).astype(BF16)
                sc = _dot(qm, kcat, NT) * scale + bias_ref[hh]
                if j == 0:
                    sc = jnp.where(dead, NEG_INF, sc)
                m = jnp.max(sc, axis=-1, keepdims=True)
                p = jnp.exp(sc - m)
                l = jnp.sum(p, axis=-1, keepdims=True)
                o_acc = jnp.where(mine, _dot((p / l).astype(BF16), vcat), o_acc)
                l_acc = jnp.where(mine, m + jnp.log(l), l_acc)
            o_ref[rows, :] = o_acc
            lse_ref[rows, :] = l_acc

        _for_units(r, nsub, one, lambda results: None)

    cur = lambda c: pl.BlockSpec((tb, LANES), lambda b, p: (b, c + p))
    prev = lambda c: pl.BlockSpec((sub, LANES), lambda b, p: (jnp.maximum(b * nsub - 1, 0), c + p))
    out = pl.BlockSpec((tb, LANES), lambda b, p: (b, p))
    return _pc(
        body, job, name=name, grid=(T // tb, 2),
        in_specs=[cur(qc), cur(kc), prev(kc), cur(vc), prev(vc),
                  pl.BlockSpec((2, ATTN_BLOCK, 2 * ATTN_BLOCK), lambda b, p: (p, 0, 0))],
        out_specs=[out, out],
        out_shape=[jax.ShapeDtypeStruct((T, ATTN_GROUP_WIDTH), F32), jax.ShapeDtypeStruct((T, ATTN_GROUP_WIDTH), F32)],
        scratch_shapes=[], sem=("parallel", "parallel"), operands=(proj, proj, proj, proj, proj, bias))


def _attn_bwd(proj, do, corr, lse, bias, g, name):
    r = ATTN_DILATIONS[g]
    nsub = _ATTN_SUB[r]
    T = proj.shape[0]
    sub = ATTN_BLOCK * r
    tb = sub * nsub
    nb = T // tb
    qc, kc, vc = _attn_cols(g)
    dc = ATTN_GROUP_WIDTH * g // LANES
    scale = ATTN_HEAD_DIM ** -0.5

    def body(q_ref, kc_ref, kp_ref, vc_ref, vp_ref, do_ref, corr_ref, lse_ref, bias_ref,
             dq_ref, dk_ref, dv_ref, db_ref, dq_s, dkc_s, dkp_s, dvc_s, dvp_s, kacc, vacc):
        b = pl.program_id(1)

        @pl.when(b == 0)
        def _():
            db_ref[...] = jnp.zeros_like(db_ref)
            kacc[...] = jnp.zeros_like(kacc)
            vacc[...] = jnp.zeros_like(vacc)

        @pl.when(b == nb)
        def _():
            dk_ref[...] = kacc[...].astype(BF16)
            dv_ref[...] = vacc[...].astype(BF16)

        @pl.when(b < nb)
        def _():
            lane = lax.broadcasted_iota(jnp.int32, (ATTN_BLOCK, LANES), 1)
            kj = lax.broadcasted_iota(jnp.int32, (ATTN_BLOCK, 2 * ATTN_BLOCK), 1)
            dead = (b == 0) & (kj < ATTN_BLOCK)

            def one(j, s):
                rows = _unit_rows(j, s, r)
                before = _unit_rows(max(j - 1, 0), s, r)
                k_before = kc_ref[before, :] if j else kp_ref[before, :]
                v_before = vc_ref[before, :] if j else vp_ref[before, :]
                q = q_ref[rows, :]
                kcat = jnp.concatenate([k_before, kc_ref[rows, :]], axis=0).astype(BF16)
                vcat = jnp.concatenate([v_before, vc_ref[rows, :]], axis=0).astype(BF16)
                dov, corrv, lsev = do_ref[rows, :], corr_ref[rows, :], lse_ref[rows, :]
                dq_acc = jnp.zeros((ATTN_BLOCK, LANES), F32)
                dk_acc = jnp.zeros((2 * ATTN_BLOCK, LANES), F32)
                dv_acc = jnp.zeros((2 * ATTN_BLOCK, LANES), F32)
                dss = []
                for hh in range(2):
                    mine = (lane >= ATTN_HEAD_DIM) if hh else (lane < ATTN_HEAD_DIM)
                    col = slice(ATTN_HEAD_DIM * hh, ATTN_HEAD_DIM * hh + 1)
                    qm = jnp.where(mine, q, 0.0).astype(BF16)
                    dom = jnp.where(mine, dov, 0.0).astype(BF16)
                    sc = _dot(qm, kcat, NT) * scale + bias_ref[hh]
                    if j == 0:
                        sc = jnp.where(dead, NEG_INF, sc)
                    p = jnp.exp(sc - lsev[:, col])
                    ds = p * (_dot(dom, vcat, NT) - corrv[:, col])
                    dss.append(ds)
                    dsb = ds.astype(BF16)
                    dq_acc = jnp.where(mine, _dot(dsb, kcat) * scale, dq_acc)
                    dk_acc += _dot(dsb, qm, TN) * scale
                    dv_acc += _dot(p.astype(BF16), dom, TN)
                dq_s[rows, :] = dq_acc
                dkp_s[rows, :] = dk_acc[:ATTN_BLOCK]
                dkc_s[rows, :] = dk_acc[ATTN_BLOCK:]
                dvp_s[rows, :] = dv_acc[:ATTN_BLOCK]
                dvc_s[rows, :] = dv_acc[ATTN_BLOCK:]
                return dss

            def add_bias_grads(results):
                for hh in range(2):
                    db_ref[hh] += functools.reduce(lambda x, y: x + y, [dss[hh] for dss in results])

            _for_units(r, nsub, one, add_bias_grads)
            dq_ref[...] = dq_s[...].astype(BF16)
            tail = slice((nsub - 1) * sub, nsub * sub)
            for acc, before_s, cur_s, out_ref in ((kacc, dkp_s, dkc_s, dk_ref), (vacc, dvp_s, dvc_s, dv_ref)):
                acc[tail, :] += before_s[0:sub, :]
                out_ref[...] = acc[...].astype(BF16)
                acc[...] = cur_s[...]
                for j in range(nsub - 1):
                    acc[j * sub:(j + 1) * sub, :] += before_s[(j + 1) * sub:(j + 2) * sub, :]

    last = nb - 1
    blk = (tb, LANES)
    cur = lambda c: pl.BlockSpec(blk, lambda p, b: (jnp.minimum(b, last), c + p))
    prev = lambda c: pl.BlockSpec(blk, lambda p, b: (jnp.clip(b - 1, 0, last), c + p))
    before = lambda c: pl.BlockSpec((sub, LANES), lambda p, b: (jnp.clip(b * nsub - 1, 0, nb * nsub - 1), c + p))
    tab = pl.BlockSpec((2, ATTN_BLOCK, 2 * ATTN_BLOCK), lambda p, b: (p, 0, 0))
    return pl.pallas_call(
        body, name=name, grid=(2, nb + 1),
        in_specs=[cur(qc), cur(kc), before(kc), cur(vc), before(vc), cur(dc), cur(dc), cur(0), tab],
        out_specs=[cur(0), prev(0), prev(0), tab],
        out_shape=[jax.ShapeDtypeStruct((T, ATTN_GROUP_WIDTH), BF16)] * 3
        + [jax.ShapeDtypeStruct((4, ATTN_BLOCK, 2 * ATTN_BLOCK), F32)],
        scratch_shapes=[pltpu.VMEM(blk, F32)] * 7,
        compiler_params=_params(("arbitrary", "arbitrary")),
    )(proj, proj, proj, proj, proj, do, corr, lse, bias)


def _mix_weights(lses):
    m = jnp.maximum(jnp.maximum(lses[0], lses[1]), lses[2])
    es = [jnp.exp(l - m) for l in lses]
    inv = 1.0 / (es[0] + es[1] + es[2])
    return jnp.concatenate([e * inv for e in es], axis=1)


def _attn_mix(os, lses, proj, tm, name):
    T = proj.shape[0]
    zcol = _OFF["z_attn"] // D_ATTN

    def body(o0, o1, o2, l0, l1, l2, z_ref, out_ref):
        z = z_ref[...]
        o = jnp.concatenate([o0[...], o1[...], o2[...]], axis=1)
        alpha = _mix_weights([l0[...], l1[...], l2[...]])
        out_ref[...] = (o * alpha * (z * _sigmoid(z))).astype(BF16)

    row = lambda i: (i, 0)
    grp = pl.BlockSpec((tm, ATTN_GROUP_WIDTH), row)
    return pl.pallas_call(
        body, name=name, grid=(T // tm,),
        in_specs=[grp] * 6 + [pl.BlockSpec((tm, D_ATTN), lambda i: (i, zcol))],
        out_specs=pl.BlockSpec((tm, D_ATTN), row),
        out_shape=jax.ShapeDtypeStruct((T, D_ATTN), BF16),
        compiler_params=_params(("parallel",)),
    )(*os, *lses, proj)


def _attn_mix_bwd(d, os, lses, proj, tm, name):
    T = proj.shape[0]
    zcol = _OFF["z_attn"] // D_ATTN

    def body(d_ref, o0, o1, o2, l0, l1, l2, z_ref, do_ref, corr_ref, dz_ref):
        dv, z = d_ref[...], z_ref[...]
        ov = jnp.concatenate([o0[...], o1[...], o2[...]], axis=1)
        alpha = _mix_weights([l0[...], l1[...], l2[...]])
        sz = _sigmoid(z)
        oc = ov * alpha
        dz_ref[...] = (dv * oc * (sz * (1.0 + z * (1.0 - sz)))).astype(BF16)
        doc = dv * (z * sz)
        do_ref[...] = doc * alpha
        pr = doc * oc
        p3 = pr[:, 0:256] + pr[:, 256:512] + pr[:, 512:768]
        li = lax.broadcasted_iota(jnp.int32, (256, 256), 0) // ATTN_HEAD_DIM
        lj = lax.broadcasted_iota(jnp.int32, (256, 256), 1) // ATTN_HEAD_DIM
        ones = jnp.where(li == lj, 1.0, 0.0).astype(F32)
        s = lax.dot_general(p3, ones, NN, precision=lax.Precision.HIGHEST, preferred_element_type=F32)
        corr_ref[...] = alpha * jnp.concatenate([s, s, s], axis=1)

    row = lambda i: (i, 0)
    grp = pl.BlockSpec((tm, ATTN_GROUP_WIDTH), row)
    return pl.pallas_call(
        body, name=name, grid=(T // tm,),
        in_specs=[pl.BlockSpec((tm, D_ATTN), row)] + [grp] * 6 + [pl.BlockSpec((tm, D_ATTN), lambda i: (i, zcol))],
        out_specs=[pl.BlockSpec((tm, D_ATTN), row)] * 3,
        out_shape=[jax.ShapeDtypeStruct((T, D_ATTN), F32), jax.ShapeDtypeStruct((T, D_ATTN), F32),
                   jax.ShapeDtypeStruct((T, D_ATTN), BF16)],
        compiler_params=_params(("parallel",)),
    )(d, *os, *lses, proj)


def _mem_probs(q_ref, kv_ref, h):
    hs = slice(MEM_HEAD_DIM * h, MEM_HEAD_DIM * (h + 1))
    qh = q_ref[:, hs].astype(BF16)
    kh = kv_ref[:, hs]
    vh = kv_ref[:, D_MEM + MEM_HEAD_DIM * h:D_MEM + MEM_HEAD_DIM * (h + 1)]
    s = _dot(qh, kh, NT) * (MEM_HEAD_DIM ** -0.5)
    p = jnp.exp(s - jnp.max(s, axis=-1, keepdims=True))
    pn = p / jnp.sum(p, axis=-1, keepdims=True)
    return qh, kh, vh, pn


def _mem_fwd(proj, kv, tm, name):
    T = proj.shape[0]
    M = kv.shape[0]
    qcol, zcol = _OFF["q_mem"] // D_MEM, _OFF["z_mem"] // D_MEM

    def body(q_ref, z_ref, kv_ref, o_ref):
        outs = []
        for h in range(MEM_HEADS):
            _, _, vh, pn = _mem_probs(q_ref, kv_ref, h)
            outs.append(_dot(pn.astype(BF16), vh))
        z = z_ref[...]
        o_ref[...] = (jnp.concatenate(outs, axis=1) * (z * _sigmoid(z))).astype(BF16)

    return pl.pallas_call(
        body, name=name, grid=(T // tm,),
        in_specs=[pl.BlockSpec((tm, D_MEM), lambda i: (i, qcol)), pl.BlockSpec((tm, D_MEM), lambda i: (i, zcol)),
                  _full((M, 2 * D_MEM))],
        out_specs=pl.BlockSpec((tm, D_MEM), lambda i: (i, 0)),
        out_shape=jax.ShapeDtypeStruct((T, D_MEM), BF16),
        compiler_params=_params(("parallel",)),
    )(proj, proj, kv)


def _mem_bwd(d, proj, kv, tm, name):
    T = proj.shape[0]
    M = kv.shape[0]
    qcol, zcol = _OFF["q_mem"] // D_MEM, _OFF["z_mem"] // D_MEM

    def body(d_ref, q_ref, z_ref, kv_ref, dq_ref, dz_ref, dkv_ref):
        @pl.when(pl.program_id(0) == 0)
        def _():
            dkv_ref[...] = jnp.zeros_like(dkv_ref)

        z = z_ref[...]
        sz = _sigmoid(z)
        dv = d_ref[...]
        dov = dv * (z * sz)
        scale = MEM_HEAD_DIM ** -0.5
        outs, dqs = [], []
        for h in range(MEM_HEADS):
            hs = slice(MEM_HEAD_DIM * h, MEM_HEAD_DIM * (h + 1))
            qh, kh, vh, pn = _mem_probs(q_ref, kv_ref, h)
            pnb = pn.astype(BF16)
            oh = _dot(pnb, vh)
            outs.append(oh)
            doh = dov[:, hs]
            dohb = doh.astype(BF16)
            dp = _dot(dohb, vh, NT)
            ds = pn * (dp - jnp.sum(doh * oh, axis=-1, keepdims=True))
            dsb = ds.astype(BF16)
            dqs.append(_dot(dsb, kh) * scale)
            dkv_ref[:, hs] += _dot(dsb, qh, TN) * scale
            vs = slice(D_MEM + MEM_HEAD_DIM * h, D_MEM + MEM_HEAD_DIM * (h + 1))
            dkv_ref[:, vs] += _dot(pnb, dohb, TN)
        dq_ref[...] = jnp.concatenate(dqs, axis=1).astype(BF16)
        dz_ref[...] = (dv * jnp.concatenate(outs, axis=1) * (sz * (1.0 + z * (1.0 - sz)))).astype(BF16)

    row = lambda i: (i, 0)
    return pl.pallas_call(
        body, name=name, grid=(T // tm,),
        in_specs=[pl.BlockSpec((tm, D_MEM), row), pl.BlockSpec((tm, D_MEM), lambda i: (i, qcol)),
                  pl.BlockSpec((tm, D_MEM), lambda i: (i, zcol)), _full((M, 2 * D_MEM))],
        out_specs=[pl.BlockSpec((tm, D_MEM), row), pl.BlockSpec((tm, D_MEM), row), _full((M, 2 * D_MEM))],
        out_shape=[jax.ShapeDtypeStruct((T, D_MEM), BF16), jax.ShapeDtypeStruct((T, D_MEM), BF16),
                   jax.ShapeDtypeStruct((M, 2 * D_MEM), F32)],
        compiler_params=_params(("arbitrary",)),
    )(d, proj, proj, kv)


def _branches_and_gates(os_ref, oa_ref, om_ref, gl_refs, bg_ref, ws_ref, wa_ref, wm_ref):
    outs = (_dot(os_ref[...], ws_ref[...]), _dot(oa_ref[...], wa_ref[...]), _dot(om_ref[...], wm_ref[...]))
    gates = tuple(_sigmoid(gl_refs[k][...] + bg_ref[:, D_MODEL * k:D_MODEL * (k + 1)]) for k in range(3))
    return outs, gates


def _merge_specs(tm):
    row = lambda i: (i, 0)
    gate = [pl.BlockSpec((tm, D_MODEL), (lambda i, k=k: (i, k))) for k in range(3)]
    return ([pl.BlockSpec((tm, D_SSM), row), pl.BlockSpec((tm, D_ATTN), row), pl.BlockSpec((tm, D_MEM), row)] + gate
            + [_full((1, N_GATES)), _full((D_SSM, D_MODEL)), _full((D_ATTN, D_MODEL)), _full((D_MEM, D_MODEL)),
               _full((D_MODEL, D_MODEL))])


def _merge_fwd(x, o_ssm, o_attn, o_mem, proj, bg, ws, wa, wm, wo, tm, name):
    T = x.shape[0]

    def body(os_ref, oa_ref, om_ref, g0, g1, g2, bg_ref, ws_ref, wa_ref, wm_ref, wo_ref, x_ref, xo_ref, mg_ref):
        outs, gates = _branches_and_gates(os_ref, oa_ref, om_ref, (g0, g1, g2), bg_ref, ws_ref, wa_ref, wm_ref)
        merged = (gates[0] * outs[0] + gates[1] * outs[1] + gates[2] * outs[2]).astype(BF16)
        mg_ref[...] = merged
        xo_ref[...] = x_ref[...] + _dot(merged, wo_ref[...])

    row = lambda i: (i, 0)
    return pl.pallas_call(
        body, name=name, grid=(T // tm,),
        in_specs=_merge_specs(tm) + [pl.BlockSpec((tm, D_MODEL), row)],
        out_specs=[pl.BlockSpec((tm, D_MODEL), row), pl.BlockSpec((tm, D_MODEL), row)],
        out_shape=[jax.ShapeDtypeStruct((T, D_MODEL), F32), jax.ShapeDtypeStruct((T, D_MODEL), BF16)],
        compiler_params=_params(("parallel",)),
    )(o_ssm, o_attn, o_mem, proj, proj, proj, bg, ws, wa, wm, wo, x)


def _merge_bwd(dx, o_ssm, o_attn, o_mem, proj, bg, ws, wa, wm, wo, tm, name, job=None):
    T = dx.shape[0]

    def body(os_ref, oa_ref, om_ref, g0, g1, g2, bg_ref, ws_ref, wa_ref, wm_ref, wo_ref, dx_ref,
             dgl_ref, db_ref, dos_ref, doa_ref, dom_ref, dbg_ref):
        @pl.when(pl.program_id(0) == 0)
        def _():
            dbg_ref[...] = jnp.zeros_like(dbg_ref)

        outs, gates = _branches_and_gates(os_ref, oa_ref, om_ref, (g0, g1, g2), bg_ref, ws_ref, wa_ref, wm_ref)
        dm = _dot(dx_ref[...].astype(BF16), wo_ref[...], NT)
        w_refs = (ws_ref, wa_ref, wm_ref)
        do_refs = (dos_ref, doa_ref, dom_ref)
        for k in range(3):
            cols = slice(D_MODEL * k, D_MODEL * (k + 1))
            dgl = dm * outs[k] * (gates[k] * (1.0 - gates[k]))
            dgl_ref[:, cols] = dgl.astype(BF16)
            dbg_ref[:, cols] += jnp.sum(dgl, axis=0, keepdims=True)
            dbk = (dm * gates[k]).astype(BF16)
            db_ref[:, cols] = dbk
            do_refs[k][...] = _dot(dbk, w_refs[k][...], NT)

    row = lambda i: (i, 0)
    return _pc(
        body, job, name=name, grid=(T // tm,),
        in_specs=_merge_specs(tm) + [pl.BlockSpec((tm, D_MODEL), row)],
        out_specs=[pl.BlockSpec((tm, N_GATES), row), pl.BlockSpec((tm, N_GATES), row), pl.BlockSpec((tm, D_SSM), row),
                   pl.BlockSpec((tm, D_ATTN), row), pl.BlockSpec((tm, D_MEM), row), _full((1, N_GATES))],
        out_shape=[jax.ShapeDtypeStruct((T, N_GATES), BF16), jax.ShapeDtypeStruct((T, N_GATES), BF16),
                   jax.ShapeDtypeStruct((T, D_SSM), F32), jax.ShapeDtypeStruct((T, D_ATTN), F32),
                   jax.ShapeDtypeStruct((T, D_MEM), F32), jax.ShapeDtypeStruct((1, N_GATES), F32)],
        scratch_shapes=[], sem=("arbitrary",), operands=(o_ssm, o_attn, o_mem, proj, proj, proj, bg, ws, wa, wm, wo, dx))


def _loss_head(x, g, target, tm, name):
    T, D = x.shape

    def body(x_ref, g_ref, t_ref, loss_ref, dx_ref, dg_ref):
        @pl.when(pl.program_id(0) == 0)
        def _():
            loss_ref[...] = jnp.zeros_like(loss_ref)
            dg_ref[...] = jnp.zeros_like(dg_ref)

        xv = x_ref[...]
        r = lax.rsqrt(jnp.mean(xv * xv, axis=-1, keepdims=True) + EPS)
        xr = xv * r
        err = xr * g_ref[...] - t_ref[...]
        loss_ref[...] += 0.5 * jnp.sum(jnp.mean(err * err, axis=-1, keepdims=True), axis=0, keepdims=True)
        dy = err * (1.0 / D)
        dg_ref[...] += jnp.sum(dy * xr, axis=0, keepdims=True)
        wv = dy * g_ref[...]
        dx_ref[...] = r * (wv - xr * jnp.mean(wv * xr, axis=-1, keepdims=True))

    row = lambda i: (i, 0)
    return pl.pallas_call(
        body, name=name, grid=(T // tm,),
        in_specs=[pl.BlockSpec((tm, D), row), _full((1, D)), pl.BlockSpec((tm, D), row)],
        out_specs=[_full((1, 128)), pl.BlockSpec((tm, D), row), _full((1, D))],
        out_shape=[jax.ShapeDtypeStruct((1, 128), F32), jax.ShapeDtypeStruct((T, D), F32),
                   jax.ShapeDtypeStruct((1, D), F32)],
        compiler_params=_params(("arbitrary",)),
    )(x, g, target)


def _adamw(parts, w, m, v, tr, name):
    L, R, C = w.shape

    def body(p_ref, w_ref, m_ref, v_ref, g_ref, d_ref, mo_ref, vo_ref):
        g = p_ref[0].astype(F32)
        for s in range(1, N_DEV):
            g = g + p_ref[s].astype(F32)
        mn = ADAM_B1 * m_ref[...] + (1.0 - ADAM_B1) * g
        vn = ADAM_B2 * v_ref[...] + (1.0 - ADAM_B2) * (g * g)
        m_hat = mn / (1.0 - ADAM_B1 ** ADAM_STEP)
        v_hat = vn / (1.0 - ADAM_B2 ** ADAM_STEP)
        g_ref[...] = g
        d_ref[...] = -ADAM_LR * (m_hat / (jnp.sqrt(v_hat) + ADAM_EPS) + ADAM_WD * w_ref[...])
        mo_ref[...] = mn
        vo_ref[...] = vn

    one = pl.BlockSpec((None, tr, C), lambda l, i: (l, i, 0))
    return pl.pallas_call(
        body, name=name, grid=(L, R // tr),
        in_specs=[pl.BlockSpec((N_DEV, None, tr, C), lambda l, i: (0, l, i, 0)), one, one, one],
        out_specs=[one] * 4,
        out_shape=[jax.ShapeDtypeStruct((L, R, C), F32)] * 4,
        compiler_params=_params(("parallel", "parallel")),
    )(parts, w, m, v)


_SHARDED = (("w_in", (1024, 1088), 1), ("w_glu", (96, 768), 0), ("w_mem_kv", (128, 1024), 0),
            ("w_br_ssm", (768, 128), 1), ("w_br_attn", (768, 128), 1), ("w_br_mem", (512, 128), 1),
            ("w_out", (128, 1024), 0))
_W_IN = 0
_SMALL = tuple(range(1, len(_SHARDED)))


class _Job(NamedTuple):
    ins: list
    out_shape: list
    aliases: dict
    pairs: Callable
    n: int


def _peers():
    x, y, c = lax.axis_index("x"), lax.axis_index("y"), lax.axis_index("c")
    me = 4 * x + 2 * y + c
    out = []
    for k in range(1, N_DEV):
        px = 1 - x if k & 4 else x
        py = 1 - y if k & 2 else y
        pc = 1 - c if k & 1 else c
        out.append(((px, py, pc), 4 * px + 2 * py + pc))
    return me, out


def _copies(pairs, send_sems, recv_sems, local_sems, arrivals):
    me, peers = _peers()
    local = [pltpu.make_async_copy(src(me), dst(me), local_sems.at[j]) for j, (src, dst) in enumerate(pairs)]
    sends, recvs = [], []
    for k, (peer, lin) in enumerate(peers):
        for j, (src, dst) in enumerate(pairs):
            for to, out in ((dst(me), sends), (dst(lin), recvs)):
                if out is sends or arrivals:
                    out.append(pltpu.make_async_remote_copy(
                        src_ref=src(lin), dst_ref=to, send_sem=send_sems.at[j, k], recv_sem=recv_sems.at[j, k],
                        device_id=peer, device_id_type=pl.DeviceIdType.MESH))
    return local, sends, recvs


def _start_copies(pairs, *sems):
    local, sends, _ = _copies(pairs, *sems, arrivals=False)
    for cp in local + sends:
        cp.start()


def _wait_copies(pairs, *sems):
    local, sends, recvs = _copies(pairs, *sems, arrivals=True)
    for cp in recvs:
        cp.wait_recv()
    for cp in sends:
        cp.wait_send()
    for cp in local:
        cp.wait()


def _job_scratch(job):
    return [pltpu.SemaphoreType.DMA((job.n, N_DEV - 1)), pltpu.SemaphoreType.DMA((job.n, N_DEV - 1)),
            pltpu.SemaphoreType.DMA((job.n,))]


def _pc(body, job, *, name, grid, in_specs, out_specs, out_shape, scratch_shapes, sem, operands):
    if job is None:
        return pl.pallas_call(body, name=name, grid=grid, in_specs=in_specs, out_specs=out_specs, out_shape=out_shape,
                              scratch_shapes=scratch_shapes, compiler_params=_params(sem))(*operands)
    a = len(in_specs)
    b = a + len(job.ins)
    c = b + len(out_shape)
    d = c + len(job.out_shape)
    e = d + len(scratch_shapes)

    def carried(*refs):
        pairs = job.pairs(refs[a:b], refs[c:d])
        ids = [pl.program_id(k) for k in range(len(grid))]
        first = functools.reduce(jnp.logical_and, [i == 0 for i in ids])
        last = functools.reduce(jnp.logical_and, [i == n - 1 for i, n in zip(ids, grid)])

        @pl.when(first)
        def _():
            _start_copies(pairs, *refs[e:])

        body(*refs[:a], *refs[b:c], *refs[d:e])

        @pl.when(last)
        def _():
            _wait_copies(pairs, *refs[e:])

    hbm = pl.BlockSpec(memory_space=pl.ANY)
    outs = pl.pallas_call(
        carried, name=name, grid=grid,
        in_specs=list(in_specs) + [hbm] * len(job.ins), out_specs=list(out_specs) + [hbm] * len(job.out_shape),
        out_shape=list(out_shape) + list(job.out_shape),
        input_output_aliases={a + i: len(out_shape) + o for i, o in job.aliases.items()},
        scratch_shapes=list(scratch_shapes) + _job_scratch(job),
        compiler_params=_params(("arbitrary",) * len(grid)),
    )(*operands, *job.ins)
    return outs[:len(out_shape)], outs[len(out_shape):]


def _exchange_only(job, name):
    n_in = len(job.ins)

    def body(*refs):
        pairs = job.pairs(refs[:n_in], refs[n_in:n_in + len(job.out_shape)])
        sems = refs[n_in + len(job.out_shape):]
        _start_copies(pairs, *sems)
        _wait_copies(pairs, *sems)

    hbm = pl.BlockSpec(memory_space=pl.ANY)
    return pl.pallas_call(
        body, name=name, in_specs=[hbm] * n_in, out_specs=[hbm] * len(job.out_shape), out_shape=list(job.out_shape),
        input_output_aliases=dict(job.aliases), scratch_shapes=_job_scratch(job),
    )(*job.ins)


def _gather_via_sibling(shard, layer, name):
    def body(x_ref, o_ref, send_sems, recv_sems, local_sem):
        x, y, c = lax.axis_index("x"), lax.axis_index("y"), lax.axis_index("c")
        me, sibling = (x, y, c), (x, y, 1 - c)
        chips = [(1 - x, y), (x, 1 - y), (1 - x, 1 - y)]
        src = x_ref.at[layer]

        def slot(px, py, pc):
            return o_ref.at[4 * px + 2 * py + pc]

        def copy(k, block, to, first_hand):
            return pltpu.make_async_remote_copy(
                src_ref=src if first_hand else slot(*block), dst_ref=slot(*block), send_sem=send_sems.at[k],
                recv_sem=recv_sems.at[k], device_id=to, device_id_type=pl.DeviceIdType.MESH)

        mine = pltpu.make_async_copy(src, slot(*me), local_sem)
        mine.start()
        first = [copy(0, me, sibling, True)] + [copy(1 + j, me, (*chip, c), True) for j, chip in enumerate(chips)]
        for cp in first:
            cp.start()
        passed = []
        for j, chip in enumerate(chips):
            copy(1 + j, (*chip, c), me, True).wait_recv()
            passed.append(copy(4 + j, (*chip, c), sibling, False))
            passed[-1].start()
        copy(0, sibling, me, True).wait_recv()
        for j, chip in enumerate(chips):
            copy(4 + j, (*chip, 1 - c), me, False).wait_recv()
        for cp in first + passed:
            cp.wait_send()
        mine.wait()

    hbm = pl.BlockSpec(memory_space=pl.ANY)
    return pl.pallas_call(
        body, name=name, in_specs=[hbm], out_specs=hbm,
        out_shape=jax.ShapeDtypeStruct((N_DEV,) + shard.shape[1:], shard.dtype),
        scratch_shapes=[pltpu.SemaphoreType.DMA((N_DEV - 1,)), pltpu.SemaphoreType.DMA((N_DEV - 1,)),
                        pltpu.SemaphoreType.DMA],
    )(shard)


def _lane_window(ref, who):
    return ref.at[:, pl.ds(pl.multiple_of(who * LANES, LANES), LANES)]


def _gather_job(shards, items):
    out_shape = []
    for i, _ in items:
        _, s, axis = _SHARDED[i]
        whole = i != _W_IN and axis == 1
        out_shape.append(jax.ShapeDtypeStruct((s[0], N_DEV * s[1]) if whole else (N_DEV,) + s, BF16))

    def pairs(in_refs, out_refs):
        out = []
        for (i, l), src, dst in zip(items, in_refs, out_refs):
            if i != _W_IN and _SHARDED[i][2] == 1:
                out.append((lambda who, src=src, l=l: src.at[l], lambda who, dst=dst: _lane_window(dst, who)))
            else:
                out.append((lambda who, src=src, l=l: src.at[l], lambda who, dst=dst: dst.at[who]))
        return out

    return _Job([shards[i] for i, _ in items], out_shape, {}, pairs, len(items))


def _landed_weights(items, landed):
    out = {}
    for (i, _), a in zip(items, landed):
        n, s, axis = _SHARDED[i]
        if i == _W_IN:
            w = a.transpose(1, 0, 2).reshape(D_MODEL, D_IN)
            out[n] = jnp.concatenate([w[:, st:st + wd] for _, wd, st in _SEGS], axis=1)
        elif axis == 0:
            out[n] = a.reshape(N_DEV * s[0], s[1])
        else:
            out[n] = a
    return out


def _scatter_job(grads, items, layer, parts=None):
    ng = len(grads)
    out_shape = [jax.ShapeDtypeStruct((N_DEV, DEPTH) + _SHARDED[i][1], BF16) for i in items]

    def pairs(in_refs, out_refs):
        out = []
        for i, src, dst in zip(items, in_refs[:ng], out_refs):
            _, s, axis = _SHARDED[i]
            if i == _W_IN:
                take = lambda who, src=src: src.at[who]
            elif axis == 0:
                take = lambda who, src=src, s=s: src.at[pl.ds(pl.multiple_of(who * s[0], 16), s[0])]
            else:
                take = lambda who, src=src: _lane_window(src, who)
            out.append((take, lambda who, dst=dst: dst.at[who, layer]))
        return out

    aliases = {} if parts is None else {ng + j: j for j in range(len(items))}
    return _Job(list(grads) + ([] if parts is None else list(parts)), out_shape, aliases, pairs, len(items))


def _rows_job(src, row0, landing=None):
    n = src.shape[0]
    pairs = lambda in_refs, out_refs: [(lambda who: in_refs[0], lambda who: out_refs[0].at[who, pl.ds(row0, n)])]
    return _Job([src] + ([] if landing is None else [landing]), [jax.ShapeDtypeStruct((N_DEV, _REP_ROWS, LANES), F32)],
                {} if landing is None else {1: 0}, pairs, 1)


_REPLICATED = (("norm_g", (2, 1024)), ("mem_norm_g", (2, 1024)), ("b_gate", (2, 3072)),
               ("ssm_lambda_re", (2, 48, 64)), ("ssm_lambda_im", (2, 48, 64)), ("ssm_log_dt", (2, 48)),
               ("ssm_b_re", (2, 48, 64, 16)), ("ssm_b_im", (2, 48, 64, 16)), ("ssm_c_re", (2, 48, 16, 64)),
               ("ssm_c_im", (2, 48, 16, 64)), ("ssm_d", (2, 768)), ("b_glu", (2, 768)), ("rel_bias", (32, 12)),
               ("final_norm_g", (1024,)))
_PER_LAYER = tuple((n, s[1:]) for n, s in _REPLICATED if s[0] == DEPTH and len(s) > 1)
_SHARED = tuple((n, s) for n, s in _REPLICATED if (n, s[1:]) not in _PER_LAYER)
_REP_HALF_ROWS = 1664
_REP_ROWS = 2 * _REP_HALF_ROWS
assert sum(int(np.prod(s)) for _, s in _PER_LAYER + _SHARED) <= _REP_HALF_ROWS * LANES


def _pack_half(tree, layer, shared):
    flat = [tree[n][layer].reshape(-1) for n, _ in _PER_LAYER]
    if shared:
        flat += [tree[n].reshape(-1) for n, _ in _SHARED]
    flat = jnp.concatenate(flat)
    return jnp.pad(flat, (0, _REP_HALF_ROWS * LANES - flat.shape[0])).reshape(_REP_HALF_ROWS, LANES)


def _pack_replicated(tree):
    return jnp.concatenate([_pack_half(tree, 1, False), _pack_half(tree, 0, True)])[None]


def _unpack_replicated(packed):
    halves = packed.reshape(2, -1)
    out, r = {}, 0
    for n, s in _PER_LAYER:
        size = int(np.prod(s))
        out[n] = jnp.stack([halves[1, r:r + size].reshape(s), halves[0, r:r + size].reshape(s)])
        r += size
    for n, s in _SHARED:
        size = int(np.prod(s))
        out[n] = halves[1, r:r + size].reshape(s)
        r += size
    return out


def _owner_rows_w_in(dw):
    order = sorted(_SEGS, key=lambda t: t[2])
    dw = jnp.concatenate([dw[:, _OFF[n]:_OFF[n] + wd] for n, wd, _ in order], axis=1)
    return dw.reshape(D_MODEL, N_DEV, D_IN // N_DEV).transpose(1, 0, 2)


def _discretize(lam_re, lam_im, log_dt, b_re, b_im):
    dt = jnp.exp(log_dt)[:, None]
    mag = jnp.exp(lam_re * dt)
    abar_re, abar_im = mag * jnp.cos(lam_im * dt), mag * jnp.sin(lam_im * dt)
    den = lam_re * lam_re + lam_im * lam_im
    nr, ni = abar_re - 1.0, abar_im
    f_re = (nr * lam_re + ni * lam_im) / den
    f_im = (ni * lam_re - nr * lam_im) / den
    bbar_re = f_re[..., None] * b_re - f_im[..., None] * b_im
    bbar_im = f_re[..., None] * b_im + f_im[..., None] * b_re
    return abar_re, abar_im, bbar_re, bbar_im


def _block_diag(a):
    _, R, C = a.shape
    a = a.reshape(SSM_BLOCKS, 8, R, C)
    eye = jnp.eye(8, dtype=a.dtype)
    return (a[:, :, :, None, :] * eye[None, :, None, :, None]).reshape(SSM_BLOCKS, 8 * R, 8 * C)


def _diag_blocks(a, R, C):
    a = a.reshape(SSM_BLOCKS, 8, R, 8, C)
    eye = jnp.eye(8, dtype=a.dtype)
    return jnp.sum(a * eye[None, :, None, :, None], axis=3).reshape(SSM_GROUPS, R, C)


def _carried(result, job):
    return (result, None) if job is None else result


def _layer_fwd(x, mem, W, P, bias, layer, jobs):
    tag = f"l{layer}"
    abar_re, abar_im, bbar_re, bbar_im = _discretize(P["ssm_lambda_re"][layer], P["ssm_lambda_im"][layer],
                                                     P["ssm_log_dt"][layer], P["ssm_b_re"][layer], P["ssm_b_im"][layer])
    c_re, c_im = P["ssm_c_re"][layer], P["ssm_c_im"][layer]
    ssm = dict(
        are=abar_re.reshape(1, N_STATE), aim=abar_im.reshape(1, N_STATE),
        bre=_block_diag(bbar_re.transpose(0, 2, 1)).astype(BF16), bim=_block_diag(bbar_im.transpose(0, 2, 1)).astype(BF16),
        cre=_block_diag(c_re.transpose(0, 2, 1)).astype(BF16), cimn=_block_diag(-c_im.transpose(0, 2, 1)).astype(BF16),
        ctre=_block_diag(c_re).astype(BF16), ctimn=_block_diag(-c_im).astype(BF16),
        btre=_block_diag(bbar_re).astype(BF16), btim=_block_diag(bbar_im).astype(BF16),
        d=P["ssm_d"][layer].reshape(1, D_SSM))
    bglu = P["b_glu"][layer].reshape(1, D_SSM)
    bgate = P["b_gate"][layer].reshape(1, N_GATES)
    g = P["norm_g"][layer].reshape(1, D_MODEL)
    gm = P["mem_norm_g"][layer].reshape(1, D_MODEL)
    delivered = {}

    def carry(stage):
        return jobs[stage][0] if stage in jobs else None

    def deliver(stage, landed):
        if landed is not None:
            delivered[stage] = _landed_weights(jobs[stage][1], landed)

    T = x.shape[0]
    (proj, h), landed = _carried(_norm_proj(x, g, W["w_in"], min(T, 1024), 2176, f"{tag}_proj", job=carry("proj")),
                                 carry("proj"))
    deliver("proj", landed)
    W = {**W, **delivered.get("proj", {})}
    (xr, xi, y, o_ssm), landed = _carried(
        _ssm_fwd(proj, ssm["bre"], ssm["bim"], ssm["cre"], ssm["cimn"], ssm["are"], ssm["aim"], ssm["d"], W["w_glu"],
                 bglu, 256, f"{tag}_ssm", job=carry("ssm")), carry("ssm"))
    deliver("ssm", landed)
    os, lses = [], []
    for grp in range(3):
        stage = f"attn{grp}"
        (o_g, lse_g), landed = _carried(_attn_fwd(proj, bias[grp], grp, f"{tag}_{stage}", job=carry(stage)), carry(stage))
        deliver(stage, landed)
        os.append(o_g)
        lses.append(lse_g)
    o_attn = _attn_mix(os, lses, proj, 512, f"{tag}_attn_mix")
    kvb, hm = _norm_proj(mem, gm, W["w_mem_kv"], mem.shape[0], 1024, f"{tag}_mem_kv", out_dtype=BF16)
    o_mem = _mem_fwd(proj, kvb, 512, f"{tag}_mem")
    x_out, merged = _merge_fwd(x, o_ssm, o_attn, o_mem, proj, bgate, W["w_br_ssm"], W["w_br_attn"], W["w_br_mem"],
                               W["w_out"], 256, f"{tag}_merge")
    res = dict(x=x, mem=mem, proj=proj, h=h, xr=xr, xi=xi, y=y, o_ssm=o_ssm, os=os, lses=lses,
               o_attn=o_attn, kvb=kvb, hm=hm, o_mem=o_mem, merged=merged, ssm=ssm, bglu=bglu,
               bgate=bgate, g=g, gm=gm, W=W)
    return x_out, res, delivered


def _layer_bwd(dx, res, P, bias, layer, jobs):
    tag = f"l{layer}b"
    proj, ssm, W = res["proj"], res["ssm"], res["W"]
    T = dx.shape[0]
    landed = {}

    def run(stage, fn, job):
        out, landed[stage] = _carried(fn(job), job)
        if job is None:
            del landed[stage]
        return out

    dgl, dbr, do_ssm, do_attn, do_mem, dbg = run(
        "merge", lambda job: _merge_bwd(dx, res["o_ssm"], res["o_attn"], res["o_mem"], proj, res["bgate"], W["w_br_ssm"],
                                        W["w_br_attn"], W["w_br_mem"], W["w_out"], 256, f"{tag}_merge", job=job),
        jobs.get("merge"))
    gw = {}
    gw["w_out"] = _mm_tn(res["merged"], dx, 1024, 1024, 512, f"{tag}_dw_out")
    gw["w_br_ssm"] = _mm_tn(res["o_ssm"], dbr, 768, 1024, 512, f"{tag}_dw_br_ssm", b_col=0, n=1024)
    gw["w_br_attn"] = _mm_tn(res["o_attn"], dbr, 768, 1024, 512, f"{tag}_dw_br_attn", b_col=1024, n=1024)
    gw["w_br_mem"] = _mm_tn(res["o_mem"], dbr, 512, 1024, 512, f"{tag}_dw_br_mem", b_col=2048, n=1024)

    dqm, dzm, dkv = _mem_bwd(do_mem, proj, res["kvb"], 512, f"{tag}_mem")
    M = dkv.shape[0]
    gw["w_mem_kv"] = _mm_tn(res["hm"], dkv, 1024, 1024, M, f"{tag}_dw_mem_kv")
    _, dgm = _proj_bwd(dkv.astype(BF16), W["w_mem_kv"], res["mem"], res["gm"], jnp.zeros_like(res["mem"]), M, 1024,
                       f"{tag}_mem_norm")

    do_g, corr, dza = _attn_mix_bwd(do_attn, res["os"], res["lses"], proj, 512, f"{tag}_attn_mix")
    dqs, dks, dvs, dbs = [], [], [], []
    for grp in range(3):
        dq_g, dk_g, dv_g, db_g = _attn_bwd(proj, do_g, corr, res["lses"][grp], bias[grp], grp, f"{tag}_attn{grp}")
        dqs.append(dq_g)
        dks.append(dk_g)
        dvs.append(dv_g)
        dbs.append(db_g)
    dbias = jnp.stack(dbs)

    dy, dzs, gelu_b, dt_b, dbglu = _glu_bwd(do_ssm, res["y"], proj, W["w_glu"], res["bglu"], 512, f"{tag}_glu")
    gw["w_glu"] = _mm_tn(gelu_b, dt_b, 768, 768, 512, f"{tag}_dw_glu")
    du, dbre, dbim, dcre, dcim, dare, daim, dd = run(
        "ssm", lambda job: _ssm_bwd(dy, proj, res["xr"], res["xi"], ssm["ctre"], ssm["ctimn"], ssm["btre"], ssm["btim"],
                                    ssm["are"], ssm["aim"], ssm["d"], 256, f"{tag}_ssm", job=job), jobs.get("ssm"))
    _, disc_vjp = jax.vjp(_discretize, P["ssm_lambda_re"][layer], P["ssm_lambda_im"][layer], P["ssm_log_dt"][layer],
                          P["ssm_b_re"][layer], P["ssm_b_im"][layer])
    d_lre, d_lim, d_ldt, d_bre, d_bim = disc_vjp((dare.reshape(SSM_GROUPS, SSM_STATE), daim.reshape(SSM_GROUPS, SSM_STATE),
                                                  _diag_blocks(dbre, SSM_STATE, SSM_GROUP),
                                                  _diag_blocks(dbim, SSM_STATE, SSM_GROUP)))

    small = [gw[_SHARDED[i][0]] for i in _SMALL]
    dproj = jnp.concatenate([dgl, du, dzs] + dqs + dks + dvs + [dza, dqm, dzm], axis=1)
    dw_in = run("dw_in", lambda job: _mm_tn(res["h"], dproj, 1024, 2176, 512, f"{tag}_dw_in", job=job),
                jobs["dw_in"](small) if "dw_in" in jobs else None)
    dw_in = _owner_rows_w_in(dw_in)
    dx_in, dg = run("proj", lambda job: _proj_bwd(dproj, W["w_in"], res["x"], res["g"], dx, min(T, 512), 2176,
                                                  f"{tag}_proj", job=job),
                    jobs["proj"](small, dw_in, landed) if "proj" in jobs else None)

    gp = dict(norm_g=dg[0], mem_norm_g=dgm[0], b_gate=dbg[0], ssm_lambda_re=d_lre, ssm_lambda_im=d_lim,
              ssm_log_dt=d_ldt, ssm_b_re=d_bre, ssm_b_im=d_bim,
              ssm_c_re=_diag_blocks(dcre, SSM_GROUP, SSM_STATE), ssm_c_im=_diag_blocks(dcim, SSM_GROUP, SSM_STATE),
              ssm_d=dd[0], b_glu=dbglu[0])
    return dx_in, dw_in, gp, dbias, landed


def _train_step(x, mem, target, shards, P):
    rest0 = [(i, 0) for i in _SMALL]
    rows1 = [(i, 1) for i in _SMALL if _SHARDED[i][2] == 0]
    cols1 = [(i, 1) for i in _SMALL if _SHARDED[i][2] == 1]
    first = [(_W_IN, 0)]
    W0 = _landed_weights(first, [_gather_via_sibling(shards[_W_IN], 0, "gather_w_in0")])
    buckets = _bucket_tables()
    bias = _bias_tables(P["rel_bias"], buckets, "bias_tables")
    jobs0 = {"proj": (_gather_job(shards, rest0), rest0), "ssm": (_gather_job(shards, [(_W_IN, 1)]), [(_W_IN, 1)]),
             "attn0": (_gather_job(shards, rows1), rows1), "attn1": (_gather_job(shards, cols1), cols1)}
    x, res0, delivered = _layer_fwd(x, mem, W0, P, bias, 0, jobs0)
    W1 = {**delivered["ssm"], **delivered["attn0"], **delivered["attn1"]}
    x, res1, _ = _layer_fwd(x, mem, W1, P, bias, 1, {})
    loss, dx, dgf = _loss_head(x, P["final_norm_g"].reshape(1, D_MODEL), target, 512, "loss_head")

    dx, dw_in1, gp1, dbias1, landed1 = _layer_bwd(
        dx, res1, P, bias, 1, {"proj": lambda small, dw_in, landed: _scatter_job(small, _SMALL, 1)})
    rep1 = _pack_half({n: a[None] for n, a in gp1.items()}, 0, False)
    dx, _, gp0, dbias0, landed0 = _layer_bwd(
        dx, res0, P, bias, 0,
        {"merge": _rows_job(rep1, 0), "ssm": _scatter_job([dw_in1], [_W_IN], 1),
         "dw_in": lambda small: _scatter_job(small, _SMALL, 0, parts=landed1["proj"]),
         "proj": lambda small, dw_in, landed: _scatter_job([dw_in], [_W_IN], 0, parts=landed["ssm"])})
    d_rel = _bias_grad(dbias0, dbias1, buckets, "bias_grad")
    gp0 = {n: a[None] for n, a in gp0.items()}
    gp0["rel_bias"] = jnp.sum(d_rel, axis=-1).transpose(2, 0, 1).reshape(NUM_BUCKETS, 12)
    gp0["final_norm_g"] = dgf[0]
    rep0 = _pack_half(gp0, 0, True)
    rparts, = _exchange_only(_rows_job(rep0, _REP_HALF_ROWS, landing=landed0["merge"][0]), "gather_small_grads0")
    return loss[0, 0], dx, list(landed0["proj"]) + list(landed0["dw_in"]), rparts


_WEIGHTS = ["norm_g", "mem_norm_g", "w_in", "b_gate", "ssm_lambda_re", "ssm_lambda_im", "ssm_log_dt", "ssm_b_re",
            "ssm_b_im", "ssm_c_re", "ssm_c_im", "ssm_d", "w_glu", "b_glu", "w_mem_kv", "w_br_ssm", "w_br_attn",
            "w_br_mem", "w_out", "rel_bias", "final_norm_g"]
_ADAM_ROWS = {"w_in": 128, "w_glu": 96, "w_mem_kv": 128, "w_br_ssm": 768, "w_br_attn": 768, "w_br_mem": 512,
              "w_out": 128}


def kernel(x, mem, norm_g, mem_norm_g, w_in, b_gate, ssm_lambda_re, ssm_lambda_im, ssm_log_dt, ssm_b_re, ssm_b_im, ssm_c_re, ssm_c_im, ssm_d, w_glu, b_glu, w_mem_kv, w_br_ssm, w_br_attn, w_br_mem, w_out, rel_bias, final_norm_g, loss_target, m_norm_g, m_mem_norm_g, m_w_in, m_b_gate, m_ssm_lambda_re, m_ssm_lambda_im, m_ssm_log_dt, m_ssm_b_re, m_ssm_b_im, m_ssm_c_re, m_ssm_c_im, m_ssm_d, m_w_glu, m_b_glu, m_w_mem_kv, m_w_br_ssm, m_w_br_attn, m_w_br_mem, m_w_out, m_rel_bias, m_final_norm_g, v_norm_g, v_mem_norm_g, v_w_in, v_b_gate, v_ssm_lambda_re, v_ssm_lambda_im, v_ssm_log_dt, v_ssm_b_re, v_ssm_b_im, v_ssm_c_re, v_ssm_c_im, v_ssm_d, v_w_glu, v_b_glu, v_w_mem_kv, v_w_br_ssm, v_w_br_attn, v_w_br_mem, v_w_out, v_rel_bias, v_final_norm_g):
    given = dict(locals())
    w = {n: given[n] for n in _WEIGHTS}
    m = {n: given["m_" + n] for n in _WEIGHTS}
    v = {n: given["v_" + n] for n in _WEIGHTS}

    shards = [w[n].astype(BF16) for n, _, _ in _SHARDED]
    loss, dx, parts, rparts = _train_step(x[0], mem[0], loss_target[0], shards, w)
    loss = lax.psum(loss, ("x", "y", "c"))

    new = {}
    for (n, _, _), p in zip(_SHARDED, parts):
        new[n] = _adamw(p, w[n], m[n], v[n], _ADAM_ROWS[n], f"adamw_{n}")
    rp = [_unpack_replicated(a) for a in _adamw(rparts[:, None], _pack_replicated(w), _pack_replicated(m),
                                                _pack_replicated(v), _REP_ROWS // 4, "adamw_replicated")]
    for n, _ in _REPLICATED:
        new[n] = [rp[kind][n] for kind in range(4)]
    outs = [loss, dx[None]]
    for kind in range(4):
        outs.extend(new[n][kind] for n in _WEIGHTS)
    return tuple(outs)
```

```python
import functools
import math
from typing import Callable, NamedTuple

import jax
import jax.numpy as jnp
import numpy as np
from jax import lax
from jax.experimental import pallas as pl
from jax.experimental.pallas import tpu as pltpu

F32 = jnp.float32
BF16 = jnp.bfloat16

D_MODEL = 1024
DEPTH = 2
EPS = 1e-6
D_SSM = 768
SSM_GROUP = 16
SSM_GROUPS = 48
SSM_STATE = 64
N_STATE = SSM_GROUPS * SSM_STATE
SSM_BLOCKS = 6
D_ATTN = 768
ATTN_HEAD_DIM = 64
ATTN_GROUP_WIDTH = 256
ATTN_DILATIONS = (1, 4, 16)
ATTN_SPAN = 128
ATTN_BLOCK = 128
NUM_BUCKETS = 32
REL_MAX_DISTANCE = 2048
NEG_INF = -1e30
MEM_HEADS = 4
MEM_HEAD_DIM = 128
D_MEM = 512
N_GATES = 3 * D_MODEL
D_IN = 8704
N_DEV = 8
LANES = 128
ADAM_LR = 0.001
ADAM_B1 = 0.9
ADAM_B2 = 0.999
ADAM_EPS = 1e-08
ADAM_WD = 0.01
ADAM_STEP = 10

_SEGS = (("gates", 3072, 5632), ("u", 768, 0), ("z_ssm", 768, 768), ("q", 768, 1536), ("k", 768, 2304),
         ("v", 768, 3072), ("z_attn", 768, 3840), ("q_mem", 512, 4608), ("z_mem", 512, 5120))
_OFF = {}
_o = 0
for _n, _w, _s in _SEGS:
    _OFF[_n] = _o
    _o += _w

NN = (((1,), (0,)), ((), ()))
NT = (((1,), (1,)), ((), ()))
TN = (((0,), (0,)), ((), ()))

VMEM_LIMIT = 56 * 1024 * 1024


def _dot(a, b, dims=NN):
    return lax.dot_general(a, b, dims, preferred_element_type=F32)


def _sigmoid(x):
    return 1.0 / (1.0 + jnp.exp(-x))


def _gelu_parts(x):
    k = math.sqrt(2.0 / math.pi)
    t = jnp.tanh(k * (x + 0.044715 * (x * x * x)))
    cdf = 0.5 * (1.0 + t)
    dcdf = 0.5 * (1.0 - t * t) * k * (1.0 + 3.0 * 0.044715 * (x * x))
    return x * cdf, cdf + x * dcdf


def _params(sem, vmem=VMEM_LIMIT):
    return pltpu.CompilerParams(dimension_semantics=sem, vmem_limit_bytes=vmem)


def _full(shape):
    return pl.BlockSpec(shape, lambda *_: (0,) * len(shape))


def _norm_proj(x, g, w, tm, tn, name, out_dtype=F32, job=None):
    T, D = x.shape
    N = w.shape[1]

    def body(x_ref, g_ref, w_ref, o_ref, h_ref, hs):
        @pl.when(pl.program_id(1) == 0)
        def _():
            xv = x_ref[...]
            r = lax.rsqrt(jnp.mean(xv * xv, axis=-1, keepdims=True) + EPS)
            hv = (xv * r * g_ref[...]).astype(BF16)
            hs[...] = hv
            h_ref[...] = hv

        o_ref[...] = _dot(hs[...], w_ref[...]).astype(out_dtype)

    return _pc(
        body, job, name=name, grid=(T // tm, N // tn),
        in_specs=[pl.BlockSpec((tm, D), lambda i, j: (i, 0)), _full((1, D)),
                  pl.BlockSpec((D, tn), lambda i, j: (0, j))],
        out_specs=[pl.BlockSpec((tm, tn), lambda i, j: (i, j)), pl.BlockSpec((tm, D), lambda i, j: (i, 0))],
        out_shape=[jax.ShapeDtypeStruct((T, N), out_dtype), jax.ShapeDtypeStruct((T, D), BF16)],
        scratch_shapes=[pltpu.VMEM((tm, D), BF16)], sem=("parallel", "arbitrary"), operands=(x, g, w))


def _mm_tn(a, b, tm, tn, tk, name, b_col=0, n=None, job=None):
    K, M = a.shape
    N = b.shape[1] if n is None else n
    nk = K // tk
    j0 = b_col // tn

    def body(a_ref, b_ref, o_ref, acc):
        k = pl.program_id(2)

        @pl.when(k == 0)
        def _():
            acc[...] = jnp.zeros_like(acc)

        acc[...] += _dot(a_ref[...].astype(BF16), b_ref[...].astype(BF16), TN)

        @pl.when(k == nk - 1)
        def _():
            o_ref[...] = acc[...].astype(BF16)

    out = _pc(
        body, job, name=name, grid=(M // tm, N // tn, nk),
        in_specs=[pl.BlockSpec((tk, tm), lambda i, j, k: (k, i)), pl.BlockSpec((tk, tn), lambda i, j, k: (k, j0 + j))],
        out_specs=[pl.BlockSpec((tm, tn), lambda i, j, k: (i, j))],
        out_shape=[jax.ShapeDtypeStruct((M, N), BF16)],
        scratch_shapes=[pltpu.VMEM((tm, tn), F32)], sem=("parallel", "parallel", "arbitrary"), operands=(a, b))
    return out[0] if job is None else (out[0][0], out[1])


def _proj_bwd(dp, w, x, g, dres, tm, tk, name, job=None):
    T, N = dp.shape
    D = w.shape[0]
    nk = N // tk

    def body(dp_ref, w_ref, x_ref, g_ref, dres_ref, dx_ref, dg_ref, acc):
        i, k = pl.program_id(0), pl.program_id(1)

        @pl.when(k == 0)
        def _():
            acc[...] = jnp.zeros_like(acc)

        @pl.when((i == 0) & (k == 0))
        def _():
            dg_ref[...] = jnp.zeros_like(dg_ref)

        acc[...] += _dot(dp_ref[...], w_ref[...], NT)

        @pl.when(k == nk - 1)
        def _():
            xv = x_ref[...]
            dh = acc[...]
            r = lax.rsqrt(jnp.mean(xv * xv, axis=-1, keepdims=True) + EPS)
            xr = xv * r
            dg_ref[...] += jnp.sum(dh * xr, axis=0, keepdims=True)
            wv = dh * g_ref[...]
            dx_ref[...] = dres_ref[...] + r * (wv - xr * jnp.mean(wv * xr, axis=-1, keepdims=True))

    return _pc(
        body, job, name=name, grid=(T // tm, nk),
        in_specs=[pl.BlockSpec((tm, tk), lambda i, k: (i, k)), pl.BlockSpec((D, tk), lambda i, k: (0, k)),
                  pl.BlockSpec((tm, D), lambda i, k: (i, 0)), _full((1, D)),
                  pl.BlockSpec((tm, D), lambda i, k: (i, 0))],
        out_specs=[pl.BlockSpec((tm, D), lambda i, k: (i, 0)), _full((1, D))],
        out_shape=[jax.ShapeDtypeStruct((T, D), F32), jax.ShapeDtypeStruct((1, D), F32)],
        scratch_shapes=[pltpu.VMEM((tm, D), F32)], sem=("arbitrary", "arbitrary"), operands=(dp, w, x, g, dres))


def _ssm_fwd(proj, bre, bim, cre, cimn, are, aim, d, wglu, bglu, tc, name, job=None):
    T = proj.shape[0]
    ucol, zcol = _OFF["u"] // D_SSM, _OFF["z_ssm"] // D_SSM

    def body(u_ref, z_ref, bre_ref, bim_ref, cre_ref, cim_ref, are_ref, aim_ref, d_ref, wg_ref, bg_ref,
             xr_ref, xi_ref, y_ref, o_ref, car_r, car_i):
        @pl.when(pl.program_id(0) == 0)
        def _():
            car_r[...] = jnp.zeros_like(car_r)
            car_i[...] = jnp.zeros_like(car_i)

        u = u_ref[...]
        ub = u.astype(BF16)
        for k in range(SSM_BLOCKS):
            uk = ub[:, 128 * k:128 * (k + 1)]
            xr_ref[:, 512 * k:512 * (k + 1)] = _dot(uk, bre_ref[k])
            xi_ref[:, 512 * k:512 * (k + 1)] = _dot(uk, bim_ref[k])
        ar, ai = are_ref[...], aim_ref[...]

        def step(t, c):
            pr, pi = c
            nr = ar * pr - ai * pi + xr_ref[pl.ds(t, 1), :]
            ni = ar * pi + ai * pr + xi_ref[pl.ds(t, 1), :]
            xr_ref[pl.ds(t, 1), :] = nr
            xi_ref[pl.ds(t, 1), :] = ni
            return nr, ni

        pr, pi = lax.fori_loop(0, tc, step, (car_r[...], car_i[...]))
        car_r[...] = pr
        car_i[...] = pi

        ys = []
        for k in range(SSM_BLOCKS):
            xrk = xr_ref[:, 512 * k:512 * (k + 1)].astype(BF16)
            xik = xi_ref[:, 512 * k:512 * (k + 1)].astype(BF16)
            ys.append(_dot(xrk, cre_ref[k]) + _dot(xik, cim_ref[k]))
        y = jnp.concatenate(ys, axis=1) + d_ref[...] * u
        y_ref[...] = y
        gl, _ = _gelu_parts(y)
        t = _dot(gl.astype(BF16), wg_ref[...]) + bg_ref[...]
        z = z_ref[...]
        o_ref[...] = (gl * _sigmoid(t) * (z * _sigmoid(z))).astype(BF16)

    return _pc(
        body, job, name=name, grid=(T // tc,),
        in_specs=[pl.BlockSpec((tc, D_SSM), lambda i: (i, ucol)), pl.BlockSpec((tc, D_SSM), lambda i: (i, zcol)),
                  _full((SSM_BLOCKS, 128, 512)), _full((SSM_BLOCKS, 128, 512)),
                  _full((SSM_BLOCKS, 512, 128)), _full((SSM_BLOCKS, 512, 128)),
                  _full((1, N_STATE)), _full((1, N_STATE)), _full((1, D_SSM)),
                  _full((D_SSM, D_SSM)), _full((1, D_SSM))],
        out_specs=[pl.BlockSpec((tc, N_STATE), lambda i: (i, 0)), pl.BlockSpec((tc, N_STATE), lambda i: (i, 0)),
                   pl.BlockSpec((tc, D_SSM), lambda i: (i, 0)), pl.BlockSpec((tc, D_SSM), lambda i: (i, 0))],
        out_shape=[jax.ShapeDtypeStruct((T, N_STATE), F32), jax.ShapeDtypeStruct((T, N_STATE), F32),
                   jax.ShapeDtypeStruct((T, D_SSM), F32), jax.ShapeDtypeStruct((T, D_SSM), BF16)],
        scratch_shapes=[pltpu.VMEM((1, N_STATE), F32), pltpu.VMEM((1, N_STATE), F32)], sem=("arbitrary",),
        operands=(proj, proj, bre, bim, cre, cimn, are, aim, d, wglu, bglu))


def _glu_bwd(do, y, proj, wglu, bglu, tm, name):
    T = y.shape[0]
    zcol = _OFF["z_ssm"] // D_SSM

    def body(do_ref, y_ref, z_ref, wg_ref, bg_ref, dy_ref, dz_ref, g_ref, dt_ref, db_ref):
        @pl.when(pl.program_id(0) == 0)
        def _():
            db_ref[...] = jnp.zeros_like(db_ref)

        dov = do_ref[...]
        gl, dgl = _gelu_parts(y_ref[...])
        glb = gl.astype(BF16)
        sg = _sigmoid(_dot(glb, wg_ref[...]) + bg_ref[...])
        z = z_ref[...]
        sz = _sigmoid(z)
        dz_ref[...] = (dov * (gl * sg) * (sz * (1.0 + z * (1.0 - sz)))).astype(BF16)
        dy2 = dov * (z * sz)
        dt = dy2 * gl * (sg * (1.0 - sg))
        dtb = dt.astype(BF16)
        dg = dy2 * sg + _dot(dtb, wg_ref[...], NT)
        dy_ref[...] = dg * dgl
        g_ref[...] = glb
        dt_ref[...] = dtb
        db_ref[...] += jnp.sum(dt, axis=0, keepdims=True)

    row = lambda i: (i, 0)
    return pl.pallas_call(
        body, name=name, grid=(T // tm,),
        in_specs=[pl.BlockSpec((tm, D_SSM), row), pl.BlockSpec((tm, D_SSM), row),
                  pl.BlockSpec((tm, D_SSM), lambda i: (i, zcol)), _full((D_SSM, D_SSM)), _full((1, D_SSM))],
        out_specs=[pl.BlockSpec((tm, D_SSM), row)] * 4 + [_full((1, D_SSM))],
        out_shape=[jax.ShapeDtypeStruct((T, D_SSM), F32), jax.ShapeDtypeStruct((T, D_SSM), BF16),
                   jax.ShapeDtypeStruct((T, D_SSM), BF16), jax.ShapeDtypeStruct((T, D_SSM), BF16),
                   jax.ShapeDtypeStruct((1, D_SSM), F32)],
        compiler_params=_params(("arbitrary",)),
    )(do, y, proj, wglu, bglu)


def _ssm_bwd(dy, proj, xr, xi, ctre, ctimn, btre, btim, are, aim, d, tc, name, job=None):
    T = dy.shape[0]
    nc = T // tc
    ucol = _OFF["u"] // D_SSM
    rb = tc // 8

    def body(dy_ref, u_ref, xr_ref, xi_ref, xpr_ref, xpi_ref, ctre_ref, ctim_ref, btre_ref, btim_ref,
             are_ref, aim_ref, d_ref,
             du_ref, dbre_ref, dbim_ref, dcre_ref, dcim_ref, dare_ref, daim_ref, dd_ref, gr, gi, car_r, car_i):
        i = pl.program_id(0)

        @pl.when(i == 0)
        def _():
            for ref in (car_r, car_i, dbre_ref, dbim_ref, dcre_ref, dcim_ref, dare_ref, daim_ref, dd_ref):
                ref[...] = jnp.zeros_like(ref)

        dyv = dy_ref[...]
        dyb = dyv.astype(BF16)
        u = u_ref[...]
        ub = u.astype(BF16)
        for k in range(SSM_BLOCKS):
            dk = dyb[:, 128 * k:128 * (k + 1)]
            gr[:, 512 * k:512 * (k + 1)] = _dot(dk, ctre_ref[k])
            gi[:, 512 * k:512 * (k + 1)] = _dot(dk, ctim_ref[k])
        ar, ai = are_ref[...], aim_ref[...]

        def step(s, c):
            pr, pi = c
            t = tc - 1 - s
            nr = gr[pl.ds(t, 1), :] + ar * pr + ai * pi
            ni = gi[pl.ds(t, 1), :] + ar * pi - ai * pr
            gr[pl.ds(t, 1), :] = nr
            gi[pl.ds(t, 1), :] = ni
            return nr, ni

        pr, pi = lax.fori_loop(0, tc, step, (car_r[...], car_i[...]))
        car_r[...] = pr
        car_i[...] = pi

        xrv, xiv = xr_ref[...], xi_ref[...]
        keep = jnp.where(i == nc - 1, 0.0, 1.0)
        row0 = lax.broadcasted_iota(jnp.int32, (tc, 1), 0) == 0
        xsr = jnp.where(row0, xpr_ref[7:8, :] * keep, pltpu.roll(xrv, 1, axis=0))
        xsi = jnp.where(row0, xpi_ref[7:8, :] * keep, pltpu.roll(xiv, 1, axis=0))
        grv, giv = gr[...], gi[...]
        dare_ref[...] += jnp.sum(grv * xsr + giv * xsi, axis=0, keepdims=True)
        daim_ref[...] += jnp.sum(giv * xsr - grv * xsi, axis=0, keepdims=True)
        dd_ref[...] += jnp.sum(dyv * u, axis=0, keepdims=True)

        dus = []
        for k in range(SSM_BLOCKS):
            sl = slice(512 * k, 512 * (k + 1))
            ch = slice(128 * k, 128 * (k + 1))
            grb, gib = grv[:, sl].astype(BF16), giv[:, sl].astype(BF16)
            dus.append(_dot(grb, btre_ref[k]) + _dot(gib, btim_ref[k]))
            dbre_ref[k] += _dot(grb, ub[:, ch], TN)
            dbim_ref[k] += _dot(gib, ub[:, ch], TN)
            dcre_ref[k] += _dot(dyb[:, ch], xrv[:, sl].astype(BF16), TN)
            dcim_ref[k] -= _dot(dyb[:, ch], xiv[:, sl].astype(BF16), TN)
        du_ref[...] = (jnp.concatenate(dus, axis=1) + d_ref[...] * dyv).astype(BF16)

    rev = lambda i: (nc - 1 - i, 0)
    prev = lambda i: (jnp.maximum((nc - 1 - i) * rb - 1, 0), 0)
    return _pc(
        body, job, name=name, grid=(nc,),
        in_specs=[pl.BlockSpec((tc, D_SSM), rev), pl.BlockSpec((tc, D_SSM), lambda i: (nc - 1 - i, ucol)),
                  pl.BlockSpec((tc, N_STATE), rev), pl.BlockSpec((tc, N_STATE), rev),
                  pl.BlockSpec((8, N_STATE), prev), pl.BlockSpec((8, N_STATE), prev),
                  _full((SSM_BLOCKS, 128, 512)), _full((SSM_BLOCKS, 128, 512)),
                  _full((SSM_BLOCKS, 512, 128)), _full((SSM_BLOCKS, 512, 128)),
                  _full((1, N_STATE)), _full((1, N_STATE)), _full((1, D_SSM))],
        out_specs=[pl.BlockSpec((tc, D_SSM), rev),
                   _full((SSM_BLOCKS, 512, 128)), _full((SSM_BLOCKS, 512, 128)),
                   _full((SSM_BLOCKS, 128, 512)), _full((SSM_BLOCKS, 128, 512)),
                   _full((1, N_STATE)), _full((1, N_STATE)), _full((1, D_SSM))],
        out_shape=[jax.ShapeDtypeStruct((T, D_SSM), BF16),
                   jax.ShapeDtypeStruct((SSM_BLOCKS, 512, 128), F32), jax.ShapeDtypeStruct((SSM_BLOCKS, 512, 128), F32),
                   jax.ShapeDtypeStruct((SSM_BLOCKS, 128, 512), F32), jax.ShapeDtypeStruct((SSM_BLOCKS, 128, 512), F32),
                   jax.ShapeDtypeStruct((1, N_STATE), F32), jax.ShapeDtypeStruct((1, N_STATE), F32),
                   jax.ShapeDtypeStruct((1, D_SSM), F32)],
        scratch_shapes=[pltpu.VMEM((tc, N_STATE), F32), pltpu.VMEM((tc, N_STATE), F32),
                        pltpu.VMEM((1, N_STATE), F32), pltpu.VMEM((1, N_STATE), F32)], sem=("arbitrary",),
        operands=(dy, proj, xr, xi, xr, xi, ctre, ctimn, btre, btim, are, aim, d))


def _rel_bucket(dist):
    n = jnp.maximum(dist, 0)
    max_exact = NUM_BUCKETS // 2
    n_f = jnp.maximum(n, 1).astype(F32)
    large = max_exact + (jnp.log(n_f / max_exact) / math.log(REL_MAX_DISTANCE / max_exact)
                         * (NUM_BUCKETS - max_exact)).astype(jnp.int32)
    large = jnp.minimum(large, NUM_BUCKETS - 1)
    return jnp.where(n < max_exact, n, large)


def _bucket_tables():
    qi = jnp.arange(ATTN_BLOCK)[:, None]
    kj = jnp.arange(2 * ATTN_BLOCK)[None, :]
    delta = jnp.maximum(ATTN_BLOCK + qi - kj, 0)
    return jnp.stack([_rel_bucket(delta * r) for r in ATTN_DILATIONS]).astype(jnp.int32)


def _bias_tables(rel_bias, buckets, name):
    def body(tab_ref, bk_ref, o_ref):
        g = pl.program_id(0)
        bk = bk_ref[...]
        qi = lax.broadcasted_iota(jnp.int32, bk.shape, 0)
        kj = lax.broadcasted_iota(jnp.int32, bk.shape, 1)
        delta = ATTN_BLOCK + qi - kj
        band = (delta >= 0) & (delta <= ATTN_SPAN)
        accs = [jnp.zeros(bk.shape, F32) for _ in range(4)]
        for b in range(NUM_BUCKETS):
            hit = bk == b
            for h in range(4):
                accs[h] = jnp.where(hit, tab_ref[b, 4 * g + h], accs[h])
        for h in range(4):
            o_ref[h] = jnp.where(band, accs[h], NEG_INF)

    return pl.pallas_call(
        body, name=name, grid=(3,),
        in_specs=[pl.BlockSpec(memory_space=pltpu.SMEM),
                  pl.BlockSpec((None, ATTN_BLOCK, 2 * ATTN_BLOCK), lambda g: (g, 0, 0))],
        out_specs=pl.BlockSpec((None, 4, ATTN_BLOCK, 2 * ATTN_BLOCK), lambda g: (g, 0, 0, 0)),
        out_shape=jax.ShapeDtypeStruct((3, 4, ATTN_BLOCK, 2 * ATTN_BLOCK), F32),
        compiler_params=_params(("parallel",)),
    )(rel_bias, buckets)


def _bias_grad(db0, db1, buckets, name):
    def body(a_ref, b_ref, bk_ref, o_ref):
        bk = bk_ref[...]
        for h in range(4):
            dv = a_ref[h] + b_ref[h]
            for b in range(NUM_BUCKETS):
                o_ref[h, b:b + 1, :] = jnp.sum(jnp.where(bk == b, dv, 0.0), axis=0, keepdims=True)

    tab = pl.BlockSpec((None, 4, ATTN_BLOCK, 2 * ATTN_BLOCK), lambda g: (g, 0, 0, 0))
    return pl.pallas_call(
        body, name=name, grid=(3,),
        in_specs=[tab, tab, pl.BlockSpec((None, ATTN_BLOCK, 2 * ATTN_BLOCK), lambda g: (g, 0, 0))],
        out_specs=pl.BlockSpec((None, 4, NUM_BUCKETS, 2 * ATTN_BLOCK), lambda g: (g, 0, 0, 0)),
        out_shape=jax.ShapeDtypeStruct((3, 4, NUM_BUCKETS, 2 * ATTN_BLOCK), F32),
        compiler_params=_params(("parallel",)),
    )(db0, db1, buckets)


_ATTN_SUB = {1: 4, 4: 1, 16: 1}
_UNROLL = 4


def _unit_rows(j, s, r):
    start = j * ATTN_BLOCK * r + s
    return pl.ds(start, ATTN_BLOCK, stride=r) if r > 1 else pl.ds(start, ATTN_BLOCK)


def _for_units(r, nsub, fn, after):
    if r * nsub <= _UNROLL:
        after([fn(j, s) for j in range(nsub) for s in range(r)])
    else:
        def four(i, c):
            after([fn(0, _UNROLL * i + k) for k in range(_UNROLL)])
            return c

        lax.fori_loop(0, r // _UNROLL, four, 0)


def _attn_cols(g):
    return tuple((_OFF[n] + ATTN_GROUP_WIDTH * g) // LANES for n in ("q", "k", "v"))


def _attn_fwd(proj, bias, g, name, job=None):
    r = ATTN_DILATIONS[g]
    nsub = _ATTN_SUB[r]
    T = proj.shape[0]
    sub = ATTN_BLOCK * r
    tb = sub * nsub
    qc, kc, vc = _attn_cols(g)
    scale = ATTN_HEAD_DIM ** -0.5

    def body(q_ref, kc_ref, kp_ref, vc_ref, vp_ref, bias_ref, o_ref, lse_ref):
        lane = lax.broadcasted_iota(jnp.int32, (ATTN_BLOCK, LANES), 1)
        kj = lax.broadcasted_iota(jnp.int32, (ATTN_BLOCK, 2 * ATTN_BLOCK), 1)
        dead = (pl.program_id(0) == 0) & (kj < ATTN_BLOCK)

        def one(j, s):
            rows = _unit_rows(j, s, r)
            before = _unit_rows(max(j - 1, 0), s, r)
            k_before = kc_ref[before, :] if j else kp_ref[before, :]
            v_before = vc_ref[before, :] if j else vp_ref[before, :]
            q = q_ref[rows, :]
            kcat = jnp.concatenate([k_before, kc_ref[rows, :]], axis=0).astype(BF16)
            vcat = jnp.concatenate([v_before, vc_ref[rows, :]], axis=0).astype(BF16)
            o_acc = jnp.zeros((ATTN_BLOCK, LANES), F32)
            l_acc = jnp.zeros((ATTN_BLOCK, LANES), F32)
            for hh in range(2):
                mine = (lane >= ATTN_HEAD_DIM) if hh else (lane < ATTN_HEAD_DIM)
                qm = jnp.where(mine, q, 0.0).astype(BF16)
                sc = _dot(qm, kcat, NT) * scale + bias_ref[hh]
                if j == 0:
                    sc = jnp.where(dead, NEG_INF, sc)
                m = jnp.max(sc, axis=-1, keepdims=True)
                p = jnp.exp(sc - m)
                l = jnp.sum(p, axis=-1, keepdims=True)
                o_acc = jnp.where(mine, _dot((p / l).astype(BF16), vcat), o_acc)
                l_acc = jnp.where(mine, m + jnp.log(l), l_acc)
            o_ref[rows, :] = o_acc
            lse_ref[rows, :] = l_acc

        _for_units(r, nsub, one, lambda results: None)

    cur = lambda c: pl.BlockSpec((tb, LANES), lambda b, p: (b, c + p))
    prev = lambda c: pl.BlockSpec((sub, LANES), lambda b, p: (jnp.maximum(b * nsub - 1, 0), c + p))
    out = pl.BlockSpec((tb, LANES), lambda b, p: (b, p))
    return _pc(
        body, job, name=name, grid=(T // tb, 2),
        in_specs=[cur(qc), cur(kc), prev(kc), cur(vc), prev(vc),
                  pl.BlockSpec((2, ATTN_BLOCK, 2 * ATTN_BLOCK), lambda b, p: (p, 0, 0))],
        out_specs=[out, out],
        out_shape=[jax.ShapeDtypeStruct((T, ATTN_GROUP_WIDTH), F32), jax.ShapeDtypeStruct((T, ATTN_GROUP_WIDTH), F32)],
        scratch_shapes=[], sem=("parallel", "parallel"), operands=(proj, proj, proj, proj, proj, bias))


def _attn_bwd(proj, do, corr, lse, bias, g, name):
    r = ATTN_DILATIONS[g]
    nsub = _ATTN_SUB[r]
    T = proj.shape[0]
    sub = ATTN_BLOCK * r
    tb = sub * nsub
    nb = T // tb
    qc, kc, vc = _attn_cols(g)
    dc = ATTN_GROUP_WIDTH * g // LANES
    scale = ATTN_HEAD_DIM ** -0.5

    def body(q_ref, kc_ref, kp_ref, vc_ref, vp_ref, do_ref, corr_ref, lse_ref, bias_ref,
             dq_ref, dk_ref, dv_ref, db_ref, dq_s, dkc_s, dkp_s, dvc_s, dvp_s, kacc, vacc):
        b = pl.program_id(1)

        @pl.when(b == 0)
        def _():
            db_ref[...] = jnp.zeros_like(db_ref)
            kacc[...] = jnp.zeros_like(kacc)
            vacc[...] = jnp.zeros_like(vacc)

        @pl.when(b == nb)
        def _():
            dk_ref[...] = kacc[...].astype(BF16)
            dv_ref[...] = vacc[...].astype(BF16)

        @pl.when(b < nb)
        def _():
            lane = lax.broadcasted_iota(jnp.int32, (ATTN_BLOCK, LANES), 1)
            kj = lax.broadcasted_iota(jnp.int32, (ATTN_BLOCK, 2 * ATTN_BLOCK), 1)
            dead = (b == 0) & (kj < ATTN_BLOCK)

            def one(j, s):
                rows = _unit_rows(j, s, r)
                before = _unit_rows(max(j - 1, 0), s, r)
                k_before = kc_ref[before, :] if j else kp_ref[before, :]
                v_before = vc_ref[before, :] if j else vp_ref[before, :]
                q = q_ref[rows, :]
                kcat = jnp.concatenate([k_before, kc_ref[rows, :]], axis=0).astype(BF16)
                vcat = jnp.concatenate([v_before, vc_ref[rows, :]], axis=0).astype(BF16)
                dov, corrv, lsev = do_ref[rows, :], corr_ref[rows, :], lse_ref[rows, :]
                dq_acc = jnp.zeros((ATTN_BLOCK, LANES), F32)
                dk_acc = jnp.zeros((2 * ATTN_BLOCK, LANES), F32)
                dv_acc = jnp.zeros((2 * ATTN_BLOCK, LANES), F32)
                dss = []
                for hh in range(2):
                    mine = (lane >= ATTN_HEAD_DIM) if hh else (lane < ATTN_HEAD_DIM)
                    col = slice(ATTN_HEAD_DIM * hh, ATTN_HEAD_DIM * hh + 1)
                    qm = jnp.where(mine, q, 0.0).astype(BF16)
                    dom = jnp.where(mine, dov, 0.0).astype(BF16)
                    sc = _dot(qm, kcat, NT) * scale + bias_ref[hh]
                    if j == 0:
                        sc = jnp.where(dead, NEG_INF, sc)
                    p = jnp.exp(sc - lsev[:, col])
                    ds = p * (_dot(dom, vcat, NT) - corrv[:, col])
                    dss.append(ds)
                    dsb = ds.astype(BF16)
                    dq_acc = jnp.where(mine, _dot(dsb, kcat) * scale, dq_acc)
                    dk_acc += _dot(dsb, qm, TN) * scale
                    dv_acc += _dot(p.astype(BF16), dom, TN)
                dq_s[rows, :] = dq_acc
                dkp_s[rows, :] = dk_acc[:ATTN_BLOCK]
                dkc_s[rows, :] = dk_acc[ATTN_BLOCK:]
                dvp_s[rows, :] = dv_acc[:ATTN_BLOCK]
                dvc_s[rows, :] = dv_acc[ATTN_BLOCK:]
                return dss

            def add_bias_grads(results):
                for hh in range(2):
                    db_ref[hh] += functools.reduce(lambda x, y: x + y, [dss[hh] for dss in results])

            _for_units(r, nsub, one, add_bias_grads)
            dq_ref[...] = dq_s[...].astype(BF16)
            tail = slice((nsub - 1) * sub, nsub * sub)
            for acc, before_s, cur_s, out_ref in ((kacc, dkp_s, dkc_s, dk_ref), (vacc, dvp_s, dvc_s, dv_ref)):
                acc[tail, :] += before_s[0:sub, :]
                out_ref[...] = acc[...].astype(BF16)
                acc[...] = cur_s[...]
                for j in range(nsub - 1):
                    acc[j * sub:(j + 1) * sub, :] += before_s[(j + 1) * sub:(j + 2) * sub, :]

    last = nb - 1
    blk = (tb, LANES)
    cur = lambda c: pl.BlockSpec(blk, lambda p, b: (jnp.minimum(b, last), c + p))
    prev = lambda c: pl.BlockSpec(blk, lambda p, b: (jnp.clip(b - 1, 0, last), c + p))
    before = lambda c: pl.BlockSpec((sub, LANES), lambda p, b: (jnp.clip(b * nsub - 1, 0, nb * nsub - 1), c + p))
    tab = pl.BlockSpec((2, ATTN_BLOCK, 2 * ATTN_BLOCK), lambda p, b: (p, 0, 0))
    return pl.pallas_call(
        body, name=name, grid=(2, nb + 1),
        in_specs=[cur(qc), cur(kc), before(kc), cur(vc), before(vc), cur(dc), cur(dc), cur(0), tab],
        out_specs=[cur(0), prev(0), prev(0), tab],
        out_shape=[jax.ShapeDtypeStruct((T, ATTN_GROUP_WIDTH), BF16)] * 3
        + [jax.ShapeDtypeStruct((4, ATTN_BLOCK, 2 * ATTN_BLOCK), F32)],
        scratch_shapes=[pltpu.VMEM(blk, F32)] * 7,
        compiler_params=_params(("arbitrary", "arbitrary")),
    )(proj, proj, proj, proj, proj, do, corr, lse, bias)


def _mix_weights(lses):
    m = jnp.maximum(jnp.maximum(lses[0], lses[1]), lses[2])
    es = [jnp.exp(l - m) for l in lses]
    inv = 1.0 / (es[0] + es[1] + es[2])
    return jnp.concatenate([e * inv for e in es], axis=1)


def _attn_mix(os, lses, proj, tm, name):
    T = proj.shape[0]
    zcol = _OFF["z_attn"] // D_ATTN

    def body(o0, o1, o2, l0, l1, l2, z_ref, out_ref):
        z = z_ref[...]
        o = jnp.concatenate([o0[...], o1[...], o2[...]], axis=1)
        alpha = _mix_weights([l0[...], l1[...], l2[...]])
        out_ref[...] = (o * alpha * (z * _sigmoid(z))).astype(BF16)

    row = lambda i: (i, 0)
    grp = pl.BlockSpec((tm, ATTN_GROUP_WIDTH), row)
    return pl.pallas_call(
        body, name=name, grid=(T // tm,),
        in_specs=[grp] * 6 + [pl.BlockSpec((tm, D_ATTN), lambda i: (i, zcol))],
        out_specs=pl.BlockSpec((tm, D_ATTN), row),
        out_shape=jax.ShapeDtypeStruct((T, D_ATTN), BF16),
        compiler_params=_params(("parallel",)),
    )(*os, *lses, proj)


def _attn_mix_bwd(d, os, lses, proj, tm, name):
    T = proj.shape[0]
    zcol = _OFF["z_attn"] // D_ATTN

    def body(d_ref, o0, o1, o2, l0, l1, l2, z_ref, do_ref, corr_ref, dz_ref):
        dv, z = d_ref[...], z_ref[...]
        ov = jnp.concatenate([o0[...], o1[...], o2[...]], axis=1)
        alpha = _mix_weights([l0[...], l1[...], l2[...]])
        sz = _sigmoid(z)
        oc = ov * alpha
        dz_ref[...] = (dv * oc * (sz * (1.0 + z * (1.0 - sz)))).astype(BF16)
        doc = dv * (z * sz)
        do_ref[...] = doc * alpha
        pr = doc * oc
        p3 = pr[:, 0:256] + pr[:, 256:512] + pr[:, 512:768]
        li = lax.broadcasted_iota(jnp.int32, (256, 256), 0) // ATTN_HEAD_DIM
        lj = lax.broadcasted_iota(jnp.int32, (256, 256), 1) // ATTN_HEAD_DIM
        ones = jnp.where(li == lj, 1.0, 0.0).astype(F32)
        s = lax.dot_general(p3, ones, NN, precision=lax.Precision.HIGHEST, preferred_element_type=F32)
        corr_ref[...] = alpha * jnp.concatenate([s, s, s], axis=1)

    row = lambda i: (i, 0)
    grp = pl.BlockSpec((tm, ATTN_GROUP_WIDTH), row)
    return pl.pallas_call(
        body, name=name, grid=(T // tm,),
        in_specs=[pl.BlockSpec((tm, D_ATTN), row)] + [grp] * 6 + [pl.BlockSpec((tm, D_ATTN), lambda i: (i, zcol))],
        out_specs=[pl.BlockSpec((tm, D_ATTN), row)] * 3,
        out_shape=[jax.ShapeDtypeStruct((T, D_ATTN), F32), jax.ShapeDtypeStruct((T, D_ATTN), F32),
                   jax.ShapeDtypeStruct((T, D_ATTN), BF16)],
        compiler_params=_params(("parallel",)),
    )(d, *os, *lses, proj)


def _mem_probs(q_ref, kv_ref, h):
    hs = slice(MEM_HEAD_DIM * h, MEM_HEAD_DIM * (h + 1))
    qh = q_ref[:, hs].astype(BF16)
    kh = kv_ref[:, hs]
    vh = kv_ref[:, D_MEM + MEM_HEAD_DIM * h:D_MEM + MEM_HEAD_DIM * (h + 1)]
    s = _dot(qh, kh, NT) * (MEM_HEAD_DIM ** -0.5)
    p = jnp.exp(s - jnp.max(s, axis=-1, keepdims=True))
    pn = p / jnp.sum(p, axis=-1, keepdims=True)
    return qh, kh, vh, pn


def _mem_fwd(proj, kv, tm, name):
    T = proj.shape[0]
    M = kv.shape[0]
    qcol, zcol = _OFF["q_mem"] // D_MEM, _OFF["z_mem"] // D_MEM

    def body(q_ref, z_ref, kv_ref, o_ref):
        outs = []
        for h in range(MEM_HEADS):
            _, _, vh, pn = _mem_probs(q_ref, kv_ref, h)
            outs.append(_dot(pn.astype(BF16), vh))
        z = z_ref[...]
        o_ref[...] = (jnp.concatenate(outs, axis=1) * (z * _sigmoid(z))).astype(BF16)

    return pl.pallas_call(
        body, name=name, grid=(T // tm,),
        in_specs=[pl.BlockSpec((tm, D_MEM), lambda i: (i, qcol)), pl.BlockSpec((tm, D_MEM), lambda i: (i, zcol)),
                  _full((M, 2 * D_MEM))],
        out_specs=pl.BlockSpec((tm, D_MEM), lambda i: (i, 0)),
        out_shape=jax.ShapeDtypeStruct((T, D_MEM), BF16),
        compiler_params=_params(("parallel",)),
    )(proj, proj, kv)


def _mem_bwd(d, proj, kv, tm, name):
    T = proj.shape[0]
    M = kv.shape[0]
    qcol, zcol = _OFF["q_mem"] // D_MEM, _OFF["z_mem"] // D_MEM

    def body(d_ref, q_ref, z_ref, kv_ref, dq_ref, dz_ref, dkv_ref):
        @pl.when(pl.program_id(0) == 0)
        def _():
            dkv_ref[...] = jnp.zeros_like(dkv_ref)

        z = z_ref[...]
        sz = _sigmoid(z)
        dv = d_ref[...]
        dov = dv * (z * sz)
        scale = MEM_HEAD_DIM ** -0.5
        outs, dqs = [], []
        for h in range(MEM_HEADS):
            hs = slice(MEM_HEAD_DIM * h, MEM_HEAD_DIM * (h + 1))
            qh, kh, vh, pn = _mem_probs(q_ref, kv_ref, h)
            pnb = pn.astype(BF16)
            oh = _dot(pnb, vh)
            outs.append(oh)
            doh = dov[:, hs]
            dohb = doh.astype(BF16)
            dp = _dot(dohb, vh, NT)
            ds = pn * (dp - jnp.sum(doh * oh, axis=-1, keepdims=True))
            dsb = ds.astype(BF16)
            dqs.append(_dot(dsb, kh) * scale)
            dkv_ref[:, hs] += _dot(dsb, qh, TN) * scale
            vs = slice(D_MEM + MEM_HEAD_DIM * h, D_MEM + MEM_HEAD_DIM * (h + 1))
            dkv_ref[:, vs] += _dot(pnb, dohb, TN)
        dq_ref[...] = jnp.concatenate(dqs, axis=1).astype(BF16)
        dz_ref[...] = (dv * jnp.concatenate(outs, axis=1) * (sz * (1.0 + z * (1.0 - sz)))).astype(BF16)

    row = lambda i: (i, 0)
    return pl.pallas_call(
        body, name=name, grid=(T // tm,),
        in_specs=[pl.BlockSpec((tm, D_MEM), row), pl.BlockSpec((tm, D_MEM), lambda i: (i, qcol)),
                  pl.BlockSpec((tm, D_MEM), lambda i: (i, zcol)), _full((M, 2 * D_MEM))],
        out_specs=[pl.BlockSpec((tm, D_MEM), row), pl.BlockSpec((tm, D_MEM), row), _full((M, 2 * D_MEM))],
        out_shape=[jax.ShapeDtypeStruct((T, D_MEM), BF16), jax.ShapeDtypeStruct((T, D_MEM), BF16),
                   jax.ShapeDtypeStruct((M, 2 * D_MEM), F32)],
        compiler_params=_params(("arbitrary",)),
    )(d, proj, proj, kv)


def _branches_and_gates(os_ref, oa_ref, om_ref, gl_refs, bg_ref, ws_ref, wa_ref, wm_ref):
    outs = (_dot(os_ref[...], ws_ref[...]), _dot(oa_ref[...], wa_ref[...]), _dot(om_ref[...], wm_ref[...]))
    gates = tuple(_sigmoid(gl_refs[k][...] + bg_ref[:, D_MODEL * k:D_MODEL * (k + 1)]) for k in range(3))
    return outs, gates


def _merge_specs(tm):
    row = lambda i: (i, 0)
    gate = [pl.BlockSpec((tm, D_MODEL), (lambda i, k=k: (i, k))) for k in range(3)]
    return ([pl.BlockSpec((tm, D_SSM), row), pl.BlockSpec((tm, D_ATTN), row), pl.BlockSpec((tm, D_MEM), row)] + gate
            + [_full((1, N_GATES)), _full((D_SSM, D_MODEL)), _full((D_ATTN, D_MODEL)), _full((D_MEM, D_MODEL)),
               _full((D_MODEL, D_MODEL))])


def _merge_fwd(x, o_ssm, o_attn, o_mem, proj, bg, ws, wa, wm, wo, tm, name):
    T = x.shape[0]

    def body(os_ref, oa_ref, om_ref, g0, g1, g2, bg_ref, ws_ref, wa_ref, wm_ref, wo_ref, x_ref, xo_ref, mg_ref):
        outs, gates = _branches_and_gates(os_ref, oa_ref, om_ref, (g0, g1, g2), bg_ref, ws_ref, wa_ref, wm_ref)
        merged = (gates[0] * outs[0] + gates[1] * outs[1] + gates[2] * outs[2]).astype(BF16)
        mg_ref[...] = merged
        xo_ref[...] = x_ref[...] + _dot(merged, wo_ref[...])

    row = lambda i: (i, 0)
    return pl.pallas_call(
        body, name=name, grid=(T // tm,),
        in_specs=_merge_specs(tm) + [pl.BlockSpec((tm, D_MODEL), row)],
        out_specs=[pl.BlockSpec((tm, D_MODEL), row), pl.BlockSpec((tm, D_MODEL), row)],
        out_shape=[jax.ShapeDtypeStruct((T, D_MODEL), F32), jax.ShapeDtypeStruct((T, D_MODEL), BF16)],
        compiler_params=_params(("parallel",)),
    )(o_ssm, o_attn, o_mem, proj, proj, proj, bg, ws, wa, wm, wo, x)


def _merge_bwd(dx, o_ssm, o_attn, o_mem, proj, bg, ws, wa, wm, wo, tm, name, job=None):
    T = dx.shape[0]

    def body(os_ref, oa_ref, om_ref, g0, g1, g2, bg_ref, ws_ref, wa_ref, wm_ref, wo_ref, dx_ref,
             dgl_ref, db_ref, dos_ref, doa_ref, dom_ref, dbg_ref):
        @pl.when(pl.program_id(0) == 0)
        def _():
            dbg_ref[...] = jnp.zeros_like(dbg_ref)

        outs, gates = _branches_and_gates(os_ref, oa_ref, om_ref, (g0, g1, g2), bg_ref, ws_ref, wa_ref, wm_ref)
        dm = _dot(dx_ref[...].astype(BF16), wo_ref[...], NT)
        w_refs = (ws_ref, wa_ref, wm_ref)
        do_refs = (dos_ref, doa_ref, dom_ref)
        for k in range(3):
            cols = slice(D_MODEL * k, D_MODEL * (k + 1))
            dgl = dm * outs[k] * (gates[k] * (1.0 - gates[k]))
            dgl_ref[:, cols] = dgl.astype(BF16)
            dbg_ref[:, cols] += jnp.sum(dgl, axis=0, keepdims=True)
            dbk = (dm * gates[k]).astype(BF16)
            db_ref[:, cols] = dbk
            do_refs[k][...] = _dot(dbk, w_refs[k][...], NT)

    row = lambda i: (i, 0)
    return _pc(
        body, job, name=name, grid=(T // tm,),
        in_specs=_merge_specs(tm) + [pl.BlockSpec((tm, D_MODEL), row)],
        out_specs=[pl.BlockSpec((tm, N_GATES), row), pl.BlockSpec((tm, N_GATES), row), pl.BlockSpec((tm, D_SSM), row),
                   pl.BlockSpec((tm, D_ATTN), row), pl.BlockSpec((tm, D_MEM), row), _full((1, N_GATES))],
        out_shape=[jax.ShapeDtypeStruct((T, N_GATES), BF16), jax.ShapeDtypeStruct((T, N_GATES), BF16),
                   jax.ShapeDtypeStruct((T, D_SSM), F32), jax.ShapeDtypeStruct((T, D_ATTN), F32),
                   jax.ShapeDtypeStruct((T, D_MEM), F32), jax.ShapeDtypeStruct((1, N_GATES), F32)],
        scratch_shapes=[], sem=("arbitrary",), operands=(o_ssm, o_attn, o_mem, proj, proj, proj, bg, ws, wa, wm, wo, dx))


def _loss_head(x, g, target, tm, name):
    T, D = x.shape

    def body(x_ref, g_ref, t_ref, loss_ref, dx_ref, dg_ref):
        @pl.when(pl.program_id(0) == 0)
        def _():
            loss_ref[...] = jnp.zeros_like(loss_ref)
            dg_ref[...] = jnp.zeros_like(dg_ref)

        xv = x_ref[...]
        r = lax.rsqrt(jnp.mean(xv * xv, axis=-1, keepdims=True) + EPS)
        xr = xv * r
        err = xr * g_ref[...] - t_ref[...]
        loss_ref[...] += 0.5 * jnp.sum(jnp.mean(err * err, axis=-1, keepdims=True), axis=0, keepdims=True)
        dy = err * (1.0 / D)
        dg_ref[...] += jnp.sum(dy * xr, axis=0, keepdims=True)
        wv = dy * g_ref[...]
        dx_ref[...] = r * (wv - xr * jnp.mean(wv * xr, axis=-1, keepdims=True))

    row = lambda i: (i, 0)
    return pl.pallas_call(
        body, name=name, grid=(T // tm,),
        in_specs=[pl.BlockSpec((tm, D), row), _full((1, D)), pl.BlockSpec((tm, D), row)],
        out_specs=[_full((1, 128)), pl.BlockSpec((tm, D), row), _full((1, D))],
        out_shape=[jax.ShapeDtypeStruct((1, 128), F32), jax.ShapeDtypeStruct((T, D), F32),
                   jax.ShapeDtypeStruct((1, D), F32)],
        compiler_params=_params(("arbitrary",)),
    )(x, g, target)


def _adamw(parts, w, m, v, tr, name):
    L, R, C = w.shape

    def body(p_ref, w_ref, m_ref, v_ref, g_ref, d_ref, mo_ref, vo_ref):
        g = p_ref[0].astype(F32)
        for s in range(1, N_DEV):
            g = g + p_ref[s].astype(F32)
        mn = ADAM_B1 * m_ref[...] + (1.0 - ADAM_B1) * g
        vn = ADAM_B2 * v_ref[...] + (1.0 - ADAM_B2) * (g * g)
        m_hat = mn / (1.0 - ADAM_B1 ** ADAM_STEP)
        v_hat = vn / (1.0 - ADAM_B2 ** ADAM_STEP)
        g_ref[...] = g
        d_ref[...] = -ADAM_LR * (m_hat / (jnp.sqrt(v_hat) + ADAM_EPS) + ADAM_WD * w_ref[...])
        mo_ref[...] = mn
        vo_ref[...] = vn

    one = pl.BlockSpec((None, tr, C), lambda l, i: (l, i, 0))
    return pl.pallas_call(
        body, name=name, grid=(L, R // tr),
        in_specs=[pl.BlockSpec((N_DEV, None, tr, C), lambda l, i: (0, l, i, 0)), one, one, one],
        out_specs=[one] * 4,
        out_shape=[jax.ShapeDtypeStruct((L, R, C), F32)] * 4,
        compiler_params=_params(("parallel", "parallel")),
    )(parts, w, m, v)


_SHARDED = (("w_in", (1024, 1088), 1), ("w_glu", (96, 768), 0), ("w_mem_kv", (128, 1024), 0),
            ("w_br_ssm", (768, 128), 1), ("w_br_attn", (768, 128), 1), ("w_br_mem", (512, 128), 1),
            ("w_out", (128, 1024), 0))
_W_IN = 0
_SMALL = tuple(range(1, len(_SHARDED)))


class _Job(NamedTuple):
    ins: list
    out_shape: list
    aliases: dict
    pairs: Callable
    n: int


def _peers():
    x, y, c = lax.axis_index("x"), lax.axis_index("y"), lax.axis_index("c")
    me = 4 * x + 2 * y + c
    out = []
    for k in range(1, N_DEV):
        px = 1 - x if k & 4 else x
        py = 1 - y if k & 2 else y
        pc = 1 - c if k & 1 else c
        out.append(((px, py, pc), 4 * px + 2 * py + pc))
    return me, out


def _copies(pairs, send_sems, recv_sems, local_sems, arrivals):
    me, peers = _peers()
    local = [pltpu.make_async_copy(src(me), dst(me), local_sems.at[j]) for j, (src, dst) in enumerate(pairs)]
    sends, recvs = [], []
    for k, (peer, lin) in enumerate(peers):
        for j, (src, dst) in enumerate(pairs):
            for to, out in ((dst(me), sends), (dst(lin), recvs)):
                if out is sends or arrivals:
                    out.append(pltpu.make_async_remote_copy(
                        src_ref=src(lin), dst_ref=to, send_sem=send_sems.at[j, k], recv_sem=recv_sems.at[j, k],
                        device_id=peer, device_id_type=pl.DeviceIdType.MESH))
    return local, sends, recvs


def _start_copies(pairs, *sems):
    local, sends, _ = _copies(pairs, *sems, arrivals=False)
    for cp in local + sends:
        cp.start()


def _wait_copies(pairs, *sems):
    local, sends, recvs = _copies(pairs, *sems, arrivals=True)
    for cp in recvs:
        cp.wait_recv()
    for cp in sends:
        cp.wait_send()
    for cp in local:
        cp.wait()


def _job_scratch(job):
    return [pltpu.SemaphoreType.DMA((job.n, N_DEV - 1)), pltpu.SemaphoreType.DMA((job.n, N_DEV - 1)),
            pltpu.SemaphoreType.DMA((job.n,))]


def _pc(body, job, *, name, grid, in_specs, out_specs, out_shape, scratch_shapes, sem, operands):
    if job is None:
        return pl.pallas_call(body, name=name, grid=grid, in_specs=in_specs, out_specs=out_specs, out_shape=out_shape,
                              scratch_shapes=scratch_shapes, compiler_params=_params(sem))(*operands)
    a = len(in_specs)
    b = a + len(job.ins)
    c = b + len(out_shape)
    d = c + len(job.out_shape)
    e = d + len(scratch_shapes)

    def carried(*refs):
        pairs = job.pairs(refs[a:b], refs[c:d])
        ids = [pl.program_id(k) for k in range(len(grid))]
        first = functools.reduce(jnp.logical_and, [i == 0 for i in ids])
        last = functools.reduce(jnp.logical_and, [i == n - 1 for i, n in zip(ids, grid)])

        @pl.when(first)
        def _():
            _start_copies(pairs, *refs[e:])

        body(*refs[:a], *refs[b:c], *refs[d:e])

        @pl.when(last)
        def _():
            _wait_copies(pairs, *refs[e:])

    hbm = pl.BlockSpec(memory_space=pl.ANY)
    outs = pl.pallas_call(
        carried, name=name, grid=grid,
        in_specs=list(in_specs) + [hbm] * len(job.ins), out_specs=list(out_specs) + [hbm] * len(job.out_shape),
        out_shape=list(out_shape) + list(job.out_shape),
        input_output_aliases={a + i: len(out_shape) + o for i, o in job.aliases.items()},
        scratch_shapes=list(scratch_shapes) + _job_scratch(job),
        compiler_params=_params(("arbitrary",) * len(grid)),
    )(*operands, *job.ins)
    return outs[:len(out_shape)], outs[len(out_shape):]


def _gather_via_sibling(x, take, place, out_shape, name, landing=None):
    def body(*refs):
        x_ref, o_ref = refs[0], refs[-4]
        send_sems, recv_sems, local_sem = refs[-3:]
        x, y, c = lax.axis_index("x"), lax.axis_index("y"), lax.axis_index("c")
        me, sibling = (x, y, c), (x, y, 1 - c)
        chips = [(1 - x, y), (x, 1 - y), (1 - x, 1 - y)]
        src = take(x_ref)

        def slot(px, py, pc):
            return place(o_ref, 4 * px + 2 * py + pc)

        def copy(k, block, to, first_hand):
            return pltpu.make_async_remote_copy(
                src_ref=src if first_hand else slot(*block), dst_ref=slot(*block), send_sem=send_sems.at[k],
                recv_sem=recv_sems.at[k], device_id=to, device_id_type=pl.DeviceIdType.MESH)

        mine = pltpu.make_async_copy(src, slot(*me), local_sem)
        mine.start()
        first = [copy(0, me, sibling, True)] + [copy(1 + j, me, (*chip, c), True) for j, chip in enumerate(chips)]
        for cp in first:
            cp.start()
        passed = []
        for j, chip in enumerate(chips):
            copy(1 + j, (*chip, c), me, True).wait_recv()
            passed.append(copy(4 + j, (*chip, c), sibling, False))
            passed[-1].start()
        copy(0, sibling, me, True).wait_recv()
        for j, chip in enumerate(chips):
            copy(4 + j, (*chip, 1 - c), me, False).wait_recv()
        for cp in first + passed:
            cp.wait_send()
        mine.wait()

    hbm = pl.BlockSpec(memory_space=pl.ANY)
    ins = [x] if landing is None else [x, landing]
    return pl.pallas_call(
        body, name=name, in_specs=[hbm] * len(ins), out_specs=hbm, out_shape=out_shape,
        input_output_aliases={} if landing is None else {1: 0},
        scratch_shapes=[pltpu.SemaphoreType.DMA((N_DEV - 1,)), pltpu.SemaphoreType.DMA((N_DEV - 1,)),
                        pltpu.SemaphoreType.DMA],
    )(*ins)


def _lane_window(ref, who):
    return ref.at[:, pl.ds(pl.multiple_of(who * LANES, LANES), LANES)]


def _gather_job(shards, items):
    out_shape = []
    for i, _ in items:
        _, s, axis = _SHARDED[i]
        whole = i != _W_IN and axis == 1
        out_shape.append(jax.ShapeDtypeStruct((s[0], N_DEV * s[1]) if whole else (N_DEV,) + s, BF16))

    def pairs(in_refs, out_refs):
        out = []
        for (i, l), src, dst in zip(items, in_refs, out_refs):
            if i != _W_IN and _SHARDED[i][2] == 1:
                out.append((lambda who, src=src, l=l: src.at[l], lambda who, dst=dst: _lane_window(dst, who)))
            else:
                out.append((lambda who, src=src, l=l: src.at[l], lambda who, dst=dst: dst.at[who]))
        return out

    return _Job([shards[i] for i, _ in items], out_shape, {}, pairs, len(items))


def _landed_weights(items, landed):
    out = {}
    for (i, _), a in zip(items, landed):
        n, s, axis = _SHARDED[i]
        if i == _W_IN:
            w = a.transpose(1, 0, 2).reshape(D_MODEL, D_IN)
            out[n] = jnp.concatenate([w[:, st:st + wd] for _, wd, st in _SEGS], axis=1)
        elif axis == 0:
            out[n] = a.reshape(N_DEV * s[0], s[1])
        else:
            out[n] = a
    return out


def _scatter_job(grads, items, layer, parts=None):
    ng = len(grads)
    out_shape = [jax.ShapeDtypeStruct((N_DEV, DEPTH) + _SHARDED[i][1], BF16) for i in items]

    def pairs(in_refs, out_refs):
        out = []
        for i, src, dst in zip(items, in_refs[:ng], out_refs):
            _, s, axis = _SHARDED[i]
            if i == _W_IN:
                take = lambda who, src=src: src.at[who]
            elif axis == 0:
                take = lambda who, src=src, s=s: src.at[pl.ds(pl.multiple_of(who * s[0], 16), s[0])]
            else:
                take = lambda who, src=src: _lane_window(src, who)
            out.append((take, lambda who, dst=dst: dst.at[who, layer]))
        return out

    aliases = {} if parts is None else {ng + j: j for j in range(len(items))}
    return _Job(list(grads) + ([] if parts is None else list(parts)), out_shape, aliases, pairs, len(items))


def _rows_job(src, row0, landing=None):
    n = src.shape[0]
    pairs = lambda in_refs, out_refs: [(lambda who: in_refs[0], lambda who: out_refs[0].at[who, pl.ds(row0, n)])]
    return _Job([src] + ([] if landing is None else [landing]), [jax.ShapeDtypeStruct((N_DEV, _REP_ROWS, LANES), F32)],
                {} if landing is None else {1: 0}, pairs, 1)


_REPLICATED = (("norm_g", (2, 1024)), ("mem_norm_g", (2, 1024)), ("b_gate", (2, 3072)),
               ("ssm_lambda_re", (2, 48, 64)), ("ssm_lambda_im", (2, 48, 64)), ("ssm_log_dt", (2, 48)),
               ("ssm_b_re", (2, 48, 64, 16)), ("ssm_b_im", (2, 48, 64, 16)), ("ssm_c_re", (2, 48, 16, 64)),
               ("ssm_c_im", (2, 48, 16, 64)), ("ssm_d", (2, 768)), ("b_glu", (2, 768)), ("rel_bias", (32, 12)),
               ("final_norm_g", (1024,)))
_PER_LAYER = tuple((n, s[1:]) for n, s in _REPLICATED if s[0] == DEPTH and len(s) > 1)
_SHARED = tuple((n, s) for n, s in _REPLICATED if (n, s[1:]) not in _PER_LAYER)
_REP_HALF_ROWS = 1664
_REP_ROWS = 2 * _REP_HALF_ROWS
assert sum(int(np.prod(s)) for _, s in _PER_LAYER + _SHARED) <= _REP_HALF_ROWS * LANES


def _pack_half(tree, layer, shared):
    flat = [tree[n][layer].reshape(-1) for n, _ in _PER_LAYER]
    if shared:
        flat += [tree[n].reshape(-1) for n, _ in _SHARED]
    flat = jnp.concatenate(flat)
    return jnp.pad(flat, (0, _REP_HALF_ROWS * LANES - flat.shape[0])).reshape(_REP_HALF_ROWS, LANES)


def _pack_replicated(tree):
    return jnp.concatenate([_pack_half(tree, 1, False), _pack_half(tree, 0, True)])[None]


def _unpack_replicated(packed):
    halves = packed.reshape(2, -1)
    out, r = {}, 0
    for n, s in _PER_LAYER:
        size = int(np.prod(s))
        out[n] = jnp.stack([halves[1, r:r + size].reshape(s), halves[0, r:r + size].reshape(s)])
        r += size
    for n, s in _SHARED:
        size = int(np.prod(s))
        out[n] = halves[1, r:r + size].reshape(s)
        r += size
    return out


def _owner_rows_w_in(dw):
    order = sorted(_SEGS, key=lambda t: t[2])
    dw = jnp.concatenate([dw[:, _OFF[n]:_OFF[n] + wd] for n, wd, _ in order], axis=1)
    return dw.reshape(D_MODEL, N_DEV, D_IN // N_DEV).transpose(1, 0, 2)


def _discretize(lam_re, lam_im, log_dt, b_re, b_im):
    dt = jnp.exp(log_dt)[:, None]
    mag = jnp.exp(lam_re * dt)
    abar_re, abar_im = mag * jnp.cos(lam_im * dt), mag * jnp.sin(lam_im * dt)
    den = lam_re * lam_re + lam_im * lam_im
    nr, ni = abar_re - 1.0, abar_im
    f_re = (nr * lam_re + ni * lam_im) / den
    f_im = (ni * lam_re - nr * lam_im) / den
    bbar_re = f_re[..., None] * b_re - f_im[..., None] * b_im
    bbar_im = f_re[..., None] * b_im + f_im[..., None] * b_re
    return abar_re, abar_im, bbar_re, bbar_im


def _block_diag(a):
    _, R, C = a.shape
    a = a.reshape(SSM_BLOCKS, 8, R, C)
    eye = jnp.eye(8, dtype=a.dtype)
    return (a[:, :, :, None, :] * eye[None, :, None, :, None]).reshape(SSM_BLOCKS, 8 * R, 8 * C)


def _diag_blocks(a, R, C):
    a = a.reshape(SSM_BLOCKS, 8, R, 8, C)
    eye = jnp.eye(8, dtype=a.dtype)
    return jnp.sum(a * eye[None, :, None, :, None], axis=3).reshape(SSM_GROUPS, R, C)


def _carried(result, job):
    return (result, None) if job is None else result


def _layer_fwd(x, mem, W, P, bias, layer, jobs):
    tag = f"l{layer}"
    abar_re, abar_im, bbar_re, bbar_im = _discretize(P["ssm_lambda_re"][layer], P["ssm_lambda_im"][layer],
                                                     P["ssm_log_dt"][layer], P["ssm_b_re"][layer], P["ssm_b_im"][layer])
    c_re, c_im = P["ssm_c_re"][layer], P["ssm_c_im"][layer]
    ssm = dict(
        are=abar_re.reshape(1, N_STATE), aim=abar_im.reshape(1, N_STATE),
        bre=_block_diag(bbar_re.transpose(0, 2, 1)).astype(BF16), bim=_block_diag(bbar_im.transpose(0, 2, 1)).astype(BF16),
        cre=_block_diag(c_re.transpose(0, 2, 1)).astype(BF16), cimn=_block_diag(-c_im.transpose(0, 2, 1)).astype(BF16),
        ctre=_block_diag(c_re).astype(BF16), ctimn=_block_diag(-c_im).astype(BF16),
        btre=_block_diag(bbar_re).astype(BF16), btim=_block_diag(bbar_im).astype(BF16),
        d=P["ssm_d"][layer].reshape(1, D_SSM))
    bglu = P["b_glu"][layer].reshape(1, D_SSM)
    bgate = P["b_gate"][layer].reshape(1, N_GATES)
    g = P["norm_g"][layer].reshape(1, D_MODEL)
    gm = P["mem_norm_g"][layer].reshape(1, D_MODEL)
    delivered = {}

    def carry(stage):
        return jobs[stage][0] if stage in jobs else None

    def deliver(stage, landed):
        if landed is not None:
            delivered[stage] = _landed_weights(jobs[stage][1], landed)

    T = x.shape[0]
    (proj, h), landed = _carried(_norm_proj(x, g, W["w_in"], min(T, 1024), 2176, f"{tag}_proj", job=carry("proj")),
                                 carry("proj"))
    deliver("proj", landed)
    W = {**W, **delivered.get("proj", {})}
    (xr, xi, y, o_ssm), landed = _carried(
        _ssm_fwd(proj, ssm["bre"], ssm["bim"], ssm["cre"], ssm["cimn"], ssm["are"], ssm["aim"], ssm["d"], W["w_glu"],
                 bglu, 256, f"{tag}_ssm", job=carry("ssm")), carry("ssm"))
    deliver("ssm", landed)
    os, lses = [], []
    for grp in range(3):
        stage = f"attn{grp}"
        (o_g, lse_g), landed = _carried(_attn_fwd(proj, bias[grp], grp, f"{tag}_{stage}", job=carry(stage)), carry(stage))
        deliver(stage, landed)
        os.append(o_g)
        lses.append(lse_g)
    o_attn = _attn_mix(os, lses, proj, 512, f"{tag}_attn_mix")
    kvb, hm = _norm_proj(mem, gm, W["w_mem_kv"], mem.shape[0], 1024, f"{tag}_mem_kv", out_dtype=BF16)
    o_mem = _mem_fwd(proj, kvb, 512, f"{tag}_mem")
    x_out, merged = _merge_fwd(x, o_ssm, o_attn, o_mem, proj, bgate, W["w_br_ssm"], W["w_br_attn"], W["w_br_mem"],
                               W["w_out"], 256, f"{tag}_merge")
    res = dict(x=x, mem=mem, proj=proj, h=h, xr=xr, xi=xi, y=y, o_ssm=o_ssm, os=os, lses=lses,
               o_attn=o_attn, kvb=kvb, hm=hm, o_mem=o_mem, merged=merged, ssm=ssm, bglu=bglu,
               bgate=bgate, g=g, gm=gm, W=W)
    return x_out, res, delivered


def _layer_bwd(dx, res, P, bias, layer, jobs):
    tag = f"l{layer}b"
    proj, ssm, W = res["proj"], res["ssm"], res["W"]
    T = dx.shape[0]
    landed = {}

    def run(stage, fn, job):
        out, landed[stage] = _carried(fn(job), job)
        if job is None:
            del landed[stage]
        return out

    dgl, dbr, do_ssm, do_attn, do_mem, dbg = run(
        "merge", lambda job: _merge_bwd(dx, res["o_ssm"], res["o_attn"], res["o_mem"], proj, res["bgate"], W["w_br_ssm"],
                                        W["w_br_attn"], W["w_br_mem"], W["w_out"], 256, f"{tag}_merge", job=job),
        jobs.get("merge"))
    gw = {}
    gw["w_out"] = _mm_tn(res["merged"], dx, 1024, 1024, 512, f"{tag}_dw_out")
    gw["w_br_ssm"] = _mm_tn(res["o_ssm"], dbr, 768, 1024, 512, f"{tag}_dw_br_ssm", b_col=0, n=1024)
    gw["w_br_attn"] = _mm_tn(res["o_attn"], dbr, 768, 1024, 512, f"{tag}_dw_br_attn", b_col=1024, n=1024)
    gw["w_br_mem"] = _mm_tn(res["o_mem"], dbr, 512, 1024, 512, f"{tag}_dw_br_mem", b_col=2048, n=1024)

    dqm, dzm, dkv = _mem_bwd(do_mem, proj, res["kvb"], 512, f"{tag}_mem")
    M = dkv.shape[0]
    gw["w_mem_kv"] = _mm_tn(res["hm"], dkv, 1024, 1024, M, f"{tag}_dw_mem_kv")
    _, dgm = _proj_bwd(dkv.astype(BF16), W["w_mem_kv"], res["mem"], res["gm"], jnp.zeros_like(res["mem"]), M, 1024,
                       f"{tag}_mem_norm")

    do_g, corr, dza = _attn_mix_bwd(do_attn, res["os"], res["lses"], proj, 512, f"{tag}_attn_mix")
    dqs, dks, dvs, dbs = [], [], [], []
    for grp in range(3):
        dq_g, dk_g, dv_g, db_g = _attn_bwd(proj, do_g, corr, res["lses"][grp], bias[grp], grp, f"{tag}_attn{grp}")
        dqs.append(dq_g)
        dks.append(dk_g)
        dvs.append(dv_g)
        dbs.append(db_g)
    dbias = jnp.stack(dbs)

    dy, dzs, gelu_b, dt_b, dbglu = _glu_bwd(do_ssm, res["y"], proj, W["w_glu"], res["bglu"], 512, f"{tag}_glu")
    gw["w_glu"] = _mm_tn(gelu_b, dt_b, 768, 768, 512, f"{tag}_dw_glu")
    du, dbre, dbim, dcre, dcim, dare, daim, dd = run(
        "ssm", lambda job: _ssm_bwd(dy, proj, res["xr"], res["xi"], ssm["ctre"], ssm["ctimn"], ssm["btre"], ssm["btim"],
                                    ssm["are"], ssm["aim"], ssm["d"], 256, f"{tag}_ssm", job=job), jobs.get("ssm"))
    _, disc_vjp = jax.vjp(_discretize, P["ssm_lambda_re"][layer], P["ssm_lambda_im"][layer], P["ssm_log_dt"][layer],
                          P["ssm_b_re"][layer], P["ssm_b_im"][layer])
    d_lre, d_lim, d_ldt, d_bre, d_bim = disc_vjp((dare.reshape(SSM_GROUPS, SSM_STATE), daim.reshape(SSM_GROUPS, SSM_STATE),
                                                  _diag_blocks(dbre, SSM_STATE, SSM_GROUP),
                                                  _diag_blocks(dbim, SSM_STATE, SSM_GROUP)))

    small = [gw[_SHARDED[i][0]] for i in _SMALL]
    dproj = jnp.concatenate([dgl, du, dzs] + dqs + dks + dvs + [dza, dqm, dzm], axis=1)
    dw_in = run("dw_in", lambda job: _mm_tn(res["h"], dproj, 1024, 2176, min(T, 1024), f"{tag}_dw_in", job=job),
                jobs["dw_in"](small) if "dw_in" in jobs else None)
    dw_in = _owner_rows_w_in(dw_in)
    dx_in, dg = run("proj", lambda job: _proj_bwd(dproj, W["w_in"], res["x"], res["g"], dx, min(T, 1024), 2176,
                                                  f"{tag}_proj", job=job),
                    jobs["proj"](small, dw_in, landed) if "proj" in jobs else None)

    gp = dict(norm_g=dg[0], mem_norm_g=dgm[0], b_gate=dbg[0], ssm_lambda_re=d_lre, ssm_lambda_im=d_lim,
              ssm_log_dt=d_ldt, ssm_b_re=d_bre, ssm_b_im=d_bim,
              ssm_c_re=_diag_blocks(dcre, SSM_GROUP, SSM_STATE), ssm_c_im=_diag_blocks(dcim, SSM_GROUP, SSM_STATE),
              ssm_d=dd[0], b_glu=dbglu[0])
    return dx_in, dw_in, gp, dbias, landed


def _train_step(x, mem, target, shards, P):
    rest0 = [(i, 0) for i in _SMALL]
    rows1 = [(i, 1) for i in _SMALL if _SHARDED[i][2] == 0]
    cols1 = [(i, 1) for i in _SMALL if _SHARDED[i][2] == 1]
    first = [(_W_IN, 0)]
    w_in0 = _gather_via_sibling(shards[_W_IN], lambda ref: ref.at[0], lambda ref, s: ref.at[s],
                                jax.ShapeDtypeStruct((N_DEV,) + _SHARDED[_W_IN][1], BF16), "gather_w_in0")
    W0 = _landed_weights(first, [w_in0])
    buckets = _bucket_tables()
    bias = _bias_tables(P["rel_bias"], buckets, "bias_tables")
    jobs0 = {"proj": (_gather_job(shards, rest0), rest0), "ssm": (_gather_job(shards, [(_W_IN, 1)]), [(_W_IN, 1)]),
             "attn0": (_gather_job(shards, rows1), rows1), "attn1": (_gather_job(shards, cols1), cols1)}
    x, res0, delivered = _layer_fwd(x, mem, W0, P, bias, 0, jobs0)
    W1 = {**delivered["ssm"], **delivered["attn0"], **delivered["attn1"]}
    x, res1, _ = _layer_fwd(x, mem, W1, P, bias, 1, {})
    loss, dx, dgf = _loss_head(x, P["final_norm_g"].reshape(1, D_MODEL), target, 512, "loss_head")

    dx, dw_in1, gp1, dbias1, landed1 = _layer_bwd(
        dx, res1, P, bias, 1, {"proj": lambda small, dw_in, landed: _scatter_job(small, _SMALL, 1)})
    rep1 = _pack_half({n: a[None] for n, a in gp1.items()}, 0, False)
    dx, _, gp0, dbias0, landed0 = _layer_bwd(
        dx, res0, P, bias, 0,
        {"merge": _rows_job(rep1, 0), "ssm": _scatter_job([dw_in1], [_W_IN], 1),
         "dw_in": lambda small: _scatter_job(small, _SMALL, 0, parts=landed1["proj"]),
         "proj": lambda small, dw_in, landed: _scatter_job([dw_in], [_W_IN], 0, parts=landed["ssm"])})
    d_rel = _bias_grad(dbias0, dbias1, buckets, "bias_grad")
    gp0 = {n: a[None] for n, a in gp0.items()}
    gp0["rel_bias"] = jnp.sum(d_rel, axis=-1).transpose(2, 0, 1).reshape(NUM_BUCKETS, 12)
    gp0["final_norm_g"] = dgf[0]
    rep0 = _pack_half(gp0, 0, True)
    rparts = _gather_via_sibling(rep0, lambda ref: ref, lambda ref, s: ref.at[s, pl.ds(_REP_HALF_ROWS, _REP_HALF_ROWS)],
                                 jax.ShapeDtypeStruct((N_DEV, _REP_ROWS, LANES), F32), "gather_small_grads0",
                                 landing=landed0["merge"][0])
    return loss[0, 0], dx, list(landed0["proj"]) + list(landed0["dw_in"]), rparts


_WEIGHTS = ["norm_g", "mem_norm_g", "w_in", "b_gate", "ssm_lambda_re", "ssm_lambda_im", "ssm_log_dt", "ssm_b_re",
            "ssm_b_im", "ssm_c_re", "ssm_c_im", "ssm_d", "w_glu", "b_glu", "w_mem_kv", "w_br_ssm", "w_br_attn",
            "w_br_mem", "w_out", "rel_bias", "final_norm_g"]
_ADAM_ROWS = {"w_in": 128, "w_glu": 96, "w_mem_kv": 128, "w_br_ssm": 768, "w_br_attn": 768, "w_br_mem": 512,
              "w_out": 128}


def kernel(x, mem, norm_g, mem_norm_g, w_in, b_gate, ssm_lambda_re, ssm_lambda_im, ssm_log_dt, ssm_b_re, ssm_b_im, ssm_c_re, ssm_c_im, ssm_d, w_glu, b_glu, w_mem_kv, w_br_ssm, w_br_attn, w_br_mem, w_out, rel_bias, final_norm_g, loss_target, m_norm_g, m_mem_norm_g, m_w_in, m_b_gate, m_ssm_lambda_re, m_ssm_lambda_im, m_ssm_log_dt, m_ssm_b_re, m_ssm_b_im, m_ssm_c_re, m_ssm_c_im, m_ssm_d, m_w_glu, m_b_glu, m_w_mem_kv, m_w_br_ssm, m_w_br_attn, m_w_br_mem, m_w_out, m_rel_bias, m_final_norm_g, v_norm_g, v_mem_norm_g, v_w_in, v_b_gate, v_ssm_lambda_re, v_ssm_lambda_im, v_ssm_log_dt, v_ssm_b_re, v_ssm_b_im, v_ssm_c_re, v_ssm_c_im, v_ssm_d, v_w_glu, v_b_glu, v_w_mem_kv, v_w_br_ssm, v_w_br_attn, v_w_br_mem, v_w_out, v_rel_bias, v_final_norm_g):
    given = dict(locals())
    w = {n: given[n] for n in _WEIGHTS}
    m = {n: given["m_" + n] for n in _WEIGHTS}
    v = {n: given["v_" + n] for n in _WEIGHTS}

    shards = [w[n].astype(BF16) for n, _, _ in _SHARDED]
    loss, dx, parts, rparts = _train_step(x[0], mem[0], loss_target[0], shards, w)
    loss = lax.psum(loss, ("x", "y", "c"))

    new = {}
    for (n, _, _), p in zip(_SHARDED, parts):
        new[n] = _adamw(p, w[n], m[n], v[n], _ADAM_ROWS[n], f"adamw_{n}")
    rp = [_unpack_replicated(a) for a in _adamw(rparts[:, None], _pack_replicated(w), _pack_replicated(m),
                                                _pack_replicated(v), _REP_ROWS // 4, "adamw_replicated")]
    for n, _ in _REPLICATED:
        new[n] = [rp[kind][n] for kind in range(4)]
    outs = [loss, dx[None]]
    for kind in range(4):
        outs.extend(new[n][kind] for n in _WEIGHTS)
    return tuple(outs)
```

```python
import functools
import math
from typing import Callable, NamedTuple

import jax
import jax.numpy as jnp
import numpy as np
from jax import lax
from jax.experimental import pallas as pl
from jax.experimental.pallas import tpu as pltpu

F32 = jnp.float32
BF16 = jnp.bfloat16

D_MODEL = 1024
DEPTH = 2
EPS = 1e-6
D_SSM = 768
SSM_GROUP = 16
SSM_GROUPS = 48
SSM_STATE = 64
N_STATE = SSM_GROUPS * SSM_STATE
SSM_BLOCKS = 6
D_ATTN = 768
ATTN_HEAD_DIM = 64
ATTN_GROUP_WIDTH = 256
ATTN_DILATIONS = (1, 4, 16)
ATTN_SPAN = 128
ATTN_BLOCK = 128
NUM_BUCKETS = 32
REL_MAX_DISTANCE = 2048
NEG_INF = -1e30
MEM_HEADS = 4
MEM_HEAD_DIM = 128
D_MEM = 512
N_GATES = 3 * D_MODEL
D_IN = 8704
N_DEV = 8
LANES = 128
ADAM_LR = 0.001
ADAM_B1 = 0.9
ADAM_B2 = 0.999
ADAM_EPS = 1e-08
ADAM_WD = 0.01
ADAM_STEP = 10

_OFF = {"u": 0, "z_ssm": 768, "q": 1536, "k": 2304, "v": 3072, "z_attn": 3840, "q_mem": 4608, "z_mem": 5120,
        "gates": 5632}
GATE_BLOCK = 512

NN = (((1,), (0,)), ((), ()))
NT = (((1,), (1,)), ((), ()))
TN = (((0,), (0,)), ((), ()))

VMEM_LIMIT = 56 * 1024 * 1024


def _dot(a, b, dims=NN):
    return lax.dot_general(a, b, dims, preferred_element_type=F32)


def _sigmoid(x):
    return 1.0 / (1.0 + jnp.exp(-x))


def _gelu_parts(x):
    k = math.sqrt(2.0 / math.pi)
    t = jnp.tanh(k * (x + 0.044715 * (x * x * x)))
    cdf = 0.5 * (1.0 + t)
    dcdf = 0.5 * (1.0 - t * t) * k * (1.0 + 3.0 * 0.044715 * (x * x))
    return x * cdf, cdf + x * dcdf


def _params(sem, vmem=VMEM_LIMIT):
    return pltpu.CompilerParams(dimension_semantics=sem, vmem_limit_bytes=vmem)


def _full(shape):
    return pl.BlockSpec(shape, lambda *_: (0,) * len(shape))


def _norm_proj(x, g, w, tm, tn, name, out_dtype=F32, job=None):
    T, D = x.shape
    N = w.shape[1]

    def body(x_ref, g_ref, w_ref, o_ref, h_ref, hs):
        @pl.when(pl.program_id(1) == 0)
        def _():
            xv = x_ref[...]
            r = lax.rsqrt(jnp.mean(xv * xv, axis=-1, keepdims=True) + EPS)
            hv = (xv * r * g_ref[...]).astype(BF16)
            hs[...] = hv
            h_ref[...] = hv

        o_ref[...] = _dot(hs[...], w_ref[...]).astype(out_dtype)

    return _pc(
        body, job, name=name, grid=(T // tm, N // tn),
        in_specs=[pl.BlockSpec((tm, D), lambda i, j: (i, 0)), _full((1, D)),
                  pl.BlockSpec((D, tn), lambda i, j: (0, j))],
        out_specs=[pl.BlockSpec((tm, tn), lambda i, j: (i, j)), pl.BlockSpec((tm, D), lambda i, j: (i, 0))],
        out_shape=[jax.ShapeDtypeStruct((T, N), out_dtype), jax.ShapeDtypeStruct((T, D), BF16)],
        scratch_shapes=[pltpu.VMEM((tm, D), BF16)], sem=("parallel", "arbitrary"), operands=(x, g, w))


def _mm_tn(a, b, tm, tn, tk, name, b_col=0, n=None, job=None):
    K, M = a.shape
    N = b.shape[1] if n is None else n
    nk = K // tk
    j0 = b_col // tn

    def body(a_ref, b_ref, o_ref, acc):
        k = pl.program_id(2)

        @pl.when(k == 0)
        def _():
            acc[...] = jnp.zeros_like(acc)

        acc[...] += _dot(a_ref[...].astype(BF16), b_ref[...].astype(BF16), TN)

        @pl.when(k == nk - 1)
        def _():
            o_ref[...] = acc[...].astype(BF16)

    out = _pc(
        body, job, name=name, grid=(M // tm, N // tn, nk),
        in_specs=[pl.BlockSpec((tk, tm), lambda i, j, k: (k, i)), pl.BlockSpec((tk, tn), lambda i, j, k: (k, j0 + j))],
        out_specs=[pl.BlockSpec((tm, tn), lambda i, j, k: (i, j))],
        out_shape=[jax.ShapeDtypeStruct((M, N), BF16)],
        scratch_shapes=[pltpu.VMEM((tm, tn), F32)], sem=("parallel", "parallel", "arbitrary"), operands=(a, b))
    return out[0] if job is None else (out[0][0], out[1])


def _proj_bwd(dp, w, x, g, dres, tm, tk, name, job=None):
    T, N = dp.shape
    D = w.shape[0]
    nk = N // tk

    def body(dp_ref, w_ref, x_ref, g_ref, dres_ref, dx_ref, dg_ref, acc):
        i, k = pl.program_id(0), pl.program_id(1)

        @pl.when(k == 0)
        def _():
            acc[...] = jnp.zeros_like(acc)

        @pl.when((i == 0) & (k == 0))
        def _():
            dg_ref[...] = jnp.zeros_like(dg_ref)

        acc[...] += _dot(dp_ref[...], w_ref[...], NT)

        @pl.when(k == nk - 1)
        def _():
            xv = x_ref[...]
            dh = acc[...]
            r = lax.rsqrt(jnp.mean(xv * xv, axis=-1, keepdims=True) + EPS)
            xr = xv * r
            dg_ref[...] += jnp.sum(dh * xr, axis=0, keepdims=True)
            wv = dh * g_ref[...]
            dx_ref[...] = dres_ref[...] + r * (wv - xr * jnp.mean(wv * xr, axis=-1, keepdims=True))

    return _pc(
        body, job, name=name, grid=(T // tm, nk),
        in_specs=[pl.BlockSpec((tm, tk), lambda i, k: (i, k)), pl.BlockSpec((D, tk), lambda i, k: (0, k)),
                  pl.BlockSpec((tm, D), lambda i, k: (i, 0)), _full((1, D)),
                  pl.BlockSpec((tm, D), lambda i, k: (i, 0))],
        out_specs=[pl.BlockSpec((tm, D), lambda i, k: (i, 0)), _full((1, D))],
        out_shape=[jax.ShapeDtypeStruct((T, D), F32), jax.ShapeDtypeStruct((1, D), F32)],
        scratch_shapes=[pltpu.VMEM((tm, D), F32)], sem=("arbitrary", "arbitrary"), operands=(dp, w, x, g, dres))


def _ssm_fwd(proj, bre, bim, cre, cimn, are, aim, d, wglu, bglu, tc, name, job=None):
    T = proj.shape[0]
    ucol, zcol = _OFF["u"] // D_SSM, _OFF["z_ssm"] // D_SSM

    def body(u_ref, z_ref, bre_ref, bim_ref, cre_ref, cim_ref, are_ref, aim_ref, d_ref, wg_ref, bg_ref,
             xr_ref, xi_ref, y_ref, o_ref, car_r, car_i):
        @pl.when(pl.program_id(0) == 0)
        def _():
            car_r[...] = jnp.zeros_like(car_r)
            car_i[...] = jnp.zeros_like(car_i)

        u = u_ref[...]
        ub = u.astype(BF16)
        for k in range(SSM_BLOCKS):
            uk = ub[:, 128 * k:128 * (k + 1)]
            xr_ref[:, 512 * k:512 * (k + 1)] = _dot(uk, bre_ref[k])
            xi_ref[:, 512 * k:512 * (k + 1)] = _dot(uk, bim_ref[k])
        ar, ai = are_ref[...], aim_ref[...]

        def step(t, c):
            pr, pi = c
            nr = ar * pr - ai * pi + xr_ref[pl.ds(t, 1), :]
            ni = ar * pi + ai * pr + xi_ref[pl.ds(t, 1), :]
            xr_ref[pl.ds(t, 1), :] = nr
            xi_ref[pl.ds(t, 1), :] = ni
            return nr, ni

        pr, pi = lax.fori_loop(0, tc, step, (car_r[...], car_i[...]))
        car_r[...] = pr
        car_i[...] = pi

        ys = []
        for k in range(SSM_BLOCKS):
            xrk = xr_ref[:, 512 * k:512 * (k + 1)].astype(BF16)
            xik = xi_ref[:, 512 * k:512 * (k + 1)].astype(BF16)
            ys.append(_dot(xrk, cre_ref[k]) + _dot(xik, cim_ref[k]))
        y = jnp.concatenate(ys, axis=1) + d_ref[...] * u
        y_ref[...] = y
        gl, _ = _gelu_parts(y)
        t = _dot(gl.astype(BF16), wg_ref[...]) + bg_ref[...]
        z = z_ref[...]
        o_ref[...] = (gl * _sigmoid(t) * (z * _sigmoid(z))).astype(BF16)

    return _pc(
        body, job, name=name, grid=(T // tc,),
        in_specs=[pl.BlockSpec((tc, D_SSM), lambda i: (i, ucol)), pl.BlockSpec((tc, D_SSM), lambda i: (i, zcol)),
                  _full((SSM_BLOCKS, 128, 512)), _full((SSM_BLOCKS, 128, 512)),
                  _full((SSM_BLOCKS, 512, 128)), _full((SSM_BLOCKS, 512, 128)),
                  _full((1, N_STATE)), _full((1, N_STATE)), _full((1, D_SSM)),
                  _full((D_SSM, D_SSM)), _full((1, D_SSM))],
        out_specs=[pl.BlockSpec((tc, N_STATE), lambda i: (i, 0)), pl.BlockSpec((tc, N_STATE), lambda i: (i, 0)),
                   pl.BlockSpec((tc, D_SSM), lambda i: (i, 0)), pl.BlockSpec((tc, D_SSM), lambda i: (i, 0))],
        out_shape=[jax.ShapeDtypeStruct((T, N_STATE), F32), jax.ShapeDtypeStruct((T, N_STATE), F32),
                   jax.ShapeDtypeStruct((T, D_SSM), F32), jax.ShapeDtypeStruct((T, D_SSM), BF16)],
        scratch_shapes=[pltpu.VMEM((1, N_STATE), F32), pltpu.VMEM((1, N_STATE), F32)], sem=("arbitrary",),
        operands=(proj, proj, bre, bim, cre, cimn, are, aim, d, wglu, bglu))


def _glu_bwd(do, y, proj, wglu, bglu, tm, name):
    T = y.shape[0]
    zcol = _OFF["z_ssm"] // D_SSM

    def body(do_ref, y_ref, z_ref, wg_ref, bg_ref, dy_ref, dz_ref, g_ref, dt_ref, db_ref):
        @pl.when(pl.program_id(0) == 0)
        def _():
            db_ref[...] = jnp.zeros_like(db_ref)

        dov = do_ref[...]
        gl, dgl = _gelu_parts(y_ref[...])
        glb = gl.astype(BF16)
        sg = _sigmoid(_dot(glb, wg_ref[...]) + bg_ref[...])
        z = z_ref[...]
        sz = _sigmoid(z)
        dz_ref[...] = (dov * (gl * sg) * (sz * (1.0 + z * (1.0 - sz)))).astype(BF16)
        dy2 = dov * (z * sz)
        dt = dy2 * gl * (sg * (1.0 - sg))
        dtb = dt.astype(BF16)
        dg = dy2 * sg + _dot(dtb, wg_ref[...], NT)
        dy_ref[...] = dg * dgl
        g_ref[...] = glb
        dt_ref[...] = dtb
        db_ref[...] += jnp.sum(dt, axis=0, keepdims=True)

    row = lambda i: (i, 0)
    return pl.pallas_call(
        body, name=name, grid=(T // tm,),
        in_specs=[pl.BlockSpec((tm, D_SSM), row), pl.BlockSpec((tm, D_SSM), row),
                  pl.BlockSpec((tm, D_SSM), lambda i: (i, zcol)), _full((D_SSM, D_SSM)), _full((1, D_SSM))],
        out_specs=[pl.BlockSpec((tm, D_SSM), row)] * 4 + [_full((1, D_SSM))],
        out_shape=[jax.ShapeDtypeStruct((T, D_SSM), F32), jax.ShapeDtypeStruct((T, D_SSM), BF16),
                   jax.ShapeDtypeStruct((T, D_SSM), BF16), jax.ShapeDtypeStruct((T, D_SSM), BF16),
                   jax.ShapeDtypeStruct((1, D_SSM), F32)],
        compiler_params=_params(("arbitrary",)),
    )(do, y, proj, wglu, bglu)


def _ssm_bwd(dy, proj, xr, xi, ctre, ctimn, btre, btim, are, aim, d, tc, name, job=None):
    T = dy.shape[0]
    nc = T // tc
    ucol = _OFF["u"] // D_SSM
    rb = tc // 8

    def body(dy_ref, u_ref, xr_ref, xi_ref, xpr_ref, xpi_ref, ctre_ref, ctim_ref, btre_ref, btim_ref,
             are_ref, aim_ref, d_ref,
             du_ref, dbre_ref, dbim_ref, dcre_ref, dcim_ref, dare_ref, daim_ref, dd_ref, gr, gi, car_r, car_i):
        i = pl.program_id(0)

        @pl.when(i == 0)
        def _():
            for ref in (car_r, car_i, dbre_ref, dbim_ref, dcre_ref, dcim_ref, dare_ref, daim_ref, dd_ref):
                ref[...] = jnp.zeros_like(ref)

        dyv = dy_ref[...]
        dyb = dyv.astype(BF16)
        u = u_ref[...]
        ub = u.astype(BF16)
        for k in range(SSM_BLOCKS):
            dk = dyb[:, 128 * k:128 * (k + 1)]
            gr[:, 512 * k:512 * (k + 1)] = _dot(dk, ctre_ref[k])
            gi[:, 512 * k:512 * (k + 1)] = _dot(dk, ctim_ref[k])
        ar, ai = are_ref[...], aim_ref[...]

        def step(s, c):
            pr, pi = c
            t = tc - 1 - s
            nr = gr[pl.ds(t, 1), :] + ar * pr + ai * pi
            ni = gi[pl.ds(t, 1), :] + ar * pi - ai * pr
            gr[pl.ds(t, 1), :] = nr
            gi[pl.ds(t, 1), :] = ni
            return nr, ni

        pr, pi = lax.fori_loop(0, tc, step, (car_r[...], car_i[...]))
        car_r[...] = pr
        car_i[...] = pi

        xrv, xiv = xr_ref[...], xi_ref[...]
        keep = jnp.where(i == nc - 1, 0.0, 1.0)
        row0 = lax.broadcasted_iota(jnp.int32, (tc, 1), 0) == 0
        xsr = jnp.where(row0, xpr_ref[7:8, :] * keep, pltpu.roll(xrv, 1, axis=0))
        xsi = jnp.where(row0, xpi_ref[7:8, :] * keep, pltpu.roll(xiv, 1, axis=0))
        grv, giv = gr[...], gi[...]
        dare_ref[...] += jnp.sum(grv * xsr + giv * xsi, axis=0, keepdims=True)
        daim_ref[...] += jnp.sum(giv * xsr - grv * xsi, axis=0, keepdims=True)
        dd_ref[...] += jnp.sum(dyv * u, axis=0, keepdims=True)

        dus = []
        for k in range(SSM_BLOCKS):
            sl = slice(512 * k, 512 * (k + 1))
            ch = slice(128 * k, 128 * (k + 1))
            grb, gib = grv[:, sl].astype(BF16), giv[:, sl].astype(BF16)
            dus.append(_dot(grb, btre_ref[k]) + _dot(gib, btim_ref[k]))
            dbre_ref[k] += _dot(grb, ub[:, ch], TN)
            dbim_ref[k] += _dot(gib, ub[:, ch], TN)
            dcre_ref[k] += _dot(dyb[:, ch], xrv[:, sl].astype(BF16), TN)
            dcim_ref[k] -= _dot(dyb[:, ch], xiv[:, sl].astype(BF16), TN)
        du_ref[...] = (jnp.concatenate(dus, axis=1) + d_ref[...] * dyv).astype(BF16)

    rev = lambda i: (nc - 1 - i, 0)
    prev = lambda i: (jnp.maximum((nc - 1 - i) * rb - 1, 0), 0)
    return _pc(
        body, job, name=name, grid=(nc,),
        in_specs=[pl.BlockSpec((tc, D_SSM), rev), pl.BlockSpec((tc, D_SSM), lambda i: (nc - 1 - i, ucol)),
                  pl.BlockSpec((tc, N_STATE), rev), pl.BlockSpec((tc, N_STATE), rev),
                  pl.BlockSpec((8, N_STATE), prev), pl.BlockSpec((8, N_STATE), prev),
                  _full((SSM_BLOCKS, 128, 512)), _full((SSM_BLOCKS, 128, 512)),
                  _full((SSM_BLOCKS, 512, 128)), _full((SSM_BLOCKS, 512, 128)),
                  _full((1, N_STATE)), _full((1, N_STATE)), _full((1, D_SSM))],
        out_specs=[pl.BlockSpec((tc, D_SSM), rev),
                   _full((SSM_BLOCKS, 512, 128)), _full((SSM_BLOCKS, 512, 128)),
                   _full((SSM_BLOCKS, 128, 512)), _full((SSM_BLOCKS, 128, 512)),
                   _full((1, N_STATE)), _full((1, N_STATE)), _full((1, D_SSM))],
        out_shape=[jax.ShapeDtypeStruct((T, D_SSM), BF16),
                   jax.ShapeDtypeStruct((SSM_BLOCKS, 512, 128), F32), jax.ShapeDtypeStruct((SSM_BLOCKS, 512, 128), F32),
                   jax.ShapeDtypeStruct((SSM_BLOCKS, 128, 512), F32), jax.ShapeDtypeStruct((SSM_BLOCKS, 128, 512), F32),
                   jax.ShapeDtypeStruct((1, N_STATE), F32), jax.ShapeDtypeStruct((1, N_STATE), F32),
                   jax.ShapeDtypeStruct((1, D_SSM), F32)],
        scratch_shapes=[pltpu.VMEM((tc, N_STATE), F32), pltpu.VMEM((tc, N_STATE), F32),
                        pltpu.VMEM((1, N_STATE), F32), pltpu.VMEM((1, N_STATE), F32)], sem=("arbitrary",),
        operands=(dy, proj, xr, xi, xr, xi, ctre, ctimn, btre, btim, are, aim, d))


def _rel_bucket(dist):
    n = jnp.maximum(dist, 0)
    max_exact = NUM_BUCKETS // 2
    n_f = jnp.maximum(n, 1).astype(F32)
    large = max_exact + (jnp.log(n_f / max_exact) / math.log(REL_MAX_DISTANCE / max_exact)
                         * (NUM_BUCKETS - max_exact)).astype(jnp.int32)
    large = jnp.minimum(large, NUM_BUCKETS - 1)
    return jnp.where(n < max_exact, n, large)


def _bucket_tables():
    qi = jnp.arange(ATTN_BLOCK)[:, None]
    kj = jnp.arange(2 * ATTN_BLOCK)[None, :]
    delta = jnp.maximum(ATTN_BLOCK + qi - kj, 0)
    return jnp.stack([_rel_bucket(delta * r) for r in ATTN_DILATIONS]).astype(jnp.int32)


def _bias_tables(rel_bias, buckets, name):
    def body(tab_ref, bk_ref, o_ref):
        g = pl.program_id(0)
        bk = bk_ref[...]
        qi = lax.broadcasted_iota(jnp.int32, bk.shape, 0)
        kj = lax.broadcasted_iota(jnp.int32, bk.shape, 1)
        delta = ATTN_BLOCK + qi - kj
        band = (delta >= 0) & (delta <= ATTN_SPAN)
        accs = [jnp.zeros(bk.shape, F32) for _ in range(4)]
        for b in range(NUM_BUCKETS):
            hit = bk == b
            for h in range(4):
                accs[h] = jnp.where(hit, tab_ref[b, 4 * g + h], accs[h])
        for h in range(4):
            o_ref[h] = jnp.where(band, accs[h], NEG_INF)

    return pl.pallas_call(
        body, name=name, grid=(3,),
        in_specs=[pl.BlockSpec(memory_space=pltpu.SMEM),
                  pl.BlockSpec((None, ATTN_BLOCK, 2 * ATTN_BLOCK), lambda g: (g, 0, 0))],
        out_specs=pl.BlockSpec((None, 4, ATTN_BLOCK, 2 * ATTN_BLOCK), lambda g: (g, 0, 0, 0)),
        out_shape=jax.ShapeDtypeStruct((3, 4, ATTN_BLOCK, 2 * ATTN_BLOCK), F32),
        compiler_params=_params(("parallel",)),
    )(rel_bias, buckets)


def _bias_grad(db0, db1, buckets, name):
    def body(a_ref, b_ref, bk_ref, o_ref):
        bk = bk_ref[...]
        for h in range(4):
            dv = a_ref[h] + b_ref[h]
            for b in range(NUM_BUCKETS):
                o_ref[h, b:b + 1, :] = jnp.sum(jnp.where(bk == b, dv, 0.0), axis=0, keepdims=True)

    tab = pl.BlockSpec((None, 4, ATTN_BLOCK, 2 * ATTN_BLOCK), lambda g: (g, 0, 0, 0))
    return pl.pallas_call(
        body, name=name, grid=(3,),
        in_specs=[tab, tab, pl.BlockSpec((None, ATTN_BLOCK, 2 * ATTN_BLOCK), lambda g: (g, 0, 0))],
        out_specs=pl.BlockSpec((None, 4, NUM_BUCKETS, 2 * ATTN_BLOCK), lambda g: (g, 0, 0, 0)),
        out_shape=jax.ShapeDtypeStruct((3, 4, NUM_BUCKETS, 2 * ATTN_BLOCK), F32),
        compiler_params=_params(("parallel",)),
    )(db0, db1, buckets)


_ATTN_SUB = {1: 4, 4: 1, 16: 1}
_UNROLL = 4


def _unit_rows(j, s, r):
    start = j * ATTN_BLOCK * r + s
    return pl.ds(start, ATTN_BLOCK, stride=r) if r > 1 else pl.ds(start, ATTN_BLOCK)


def _for_units(r, nsub, fn, after):
    if r * nsub <= _UNROLL:
        after([fn(j, s) for j in range(nsub) for s in range(r)])
    else:
        def four(i, c):
            after([fn(0, _UNROLL * i + k) for k in range(_UNROLL)])
            return c

        lax.fori_loop(0, r // _UNROLL, four, 0)


def _attn_cols(g):
    return tuple((_OFF[n] + ATTN_GROUP_WIDTH * g) // LANES for n in ("q", "k", "v"))


def _attn_fwd(proj, bias, g, name, job=None):
    r = ATTN_DILATIONS[g]
    nsub = _ATTN_SUB[r]
    T = proj.shape[0]
    sub = ATTN_BLOCK * r
    tb = sub * nsub
    qc, kc, vc = _attn_cols(g)
    scale = ATTN_HEAD_DIM ** -0.5

    def body(q_ref, kc_ref, kp_ref, vc_ref, vp_ref, bias_ref, o_ref, lse_ref):
        lane = lax.broadcasted_iota(jnp.int32, (ATTN_BLOCK, LANES), 1)
        kj = lax.broadcasted_iota(jnp.int32, (ATTN_BLOCK, 2 * ATTN_BLOCK), 1)
        dead = (pl.program_id(0) == 0) & (kj < ATTN_BLOCK)

        def one(j, s):
            rows = _unit_rows(j, s, r)
            before = _unit_rows(max(j - 1, 0), s, r)
            k_before = kc_ref[before, :] if j else kp_ref[before, :]
            v_before = vc_ref[before, :] if j else vp_ref[before, :]
            q = q_ref[rows, :]
            kcat = jnp.concatenate([k_before, kc_ref[rows, :]], axis=0).astype(BF16)
            vcat = jnp.concatenate([v_before, vc_ref[rows, :]], axis=0).astype(BF16)
            o_acc = jnp.zeros((ATTN_BLOCK, LANES), F32)
            l_acc = jnp.zeros((ATTN_BLOCK, LANES), F32)
            for hh in range(2):
                mine = (lane >= ATTN_HEAD_DIM) if hh else (lane < ATTN_HEAD_DIM)
                qm = jnp.where(mine, q, 0.0).astype(BF16)
                sc = _dot(qm, kcat, NT) * scale + bias_ref[hh]
                if j == 0:
                    sc = jnp.where(dead, NEG_INF, sc)
                m = jnp.max(sc, axis=-1, keepdims=True)
                p = jnp.exp(sc - m)
                l = jnp.sum(p, axis=-1, keepdims=True)
                o_acc = jnp.where(mine, _dot((p / l).astype(BF16), vcat), o_acc)
                l_acc = jnp.where(mine, m + jnp.log(l), l_acc)
            o_ref[rows, :] = o_acc
            lse_ref[rows, :] = l_acc

        _for_units(r, nsub, one, lambda results: None)

    cur = lambda c: pl.BlockSpec((tb, LANES), lambda b, p: (b, c + p))
    prev = lambda c: pl.BlockSpec((sub, LANES), lambda b, p: (jnp.maximum(b * nsub - 1, 0), c + p))
    out = pl.BlockSpec((tb, LANES), lambda b, p: (b, p))
    return _pc(
        body, job, name=name, grid=(T // tb, 2),
        in_specs=[cur(qc), cur(kc), prev(kc), cur(vc), prev(vc),
                  pl.BlockSpec((2, ATTN_BLOCK, 2 * ATTN_BLOCK), lambda b, p: (p, 0, 0))],
        out_specs=[out, out],
        out_shape=[jax.ShapeDtypeStruct((T, ATTN_GROUP_WIDTH), F32), jax.ShapeDtypeStruct((T, ATTN_GROUP_WIDTH), F32)],
        scratch_shapes=[], sem=("parallel", "parallel"), operands=(proj, proj, proj, proj, proj, bias))


def _attn_bwd(proj, do, corr, lse, bias, g, name):
    r = ATTN_DILATIONS[g]
    nsub = _ATTN_SUB[r]
    T = proj.shape[0]
    sub = ATTN_BLOCK * r
    tb = sub * nsub
    nb = T // tb
    qc, kc, vc = _attn_cols(g)
    dc = ATTN_GROUP_WIDTH * g // LANES
    scale = ATTN_HEAD_DIM ** -0.5

    def body(q_ref, kc_ref, kp_ref, vc_ref, vp_ref, do_ref, corr_ref, lse_ref, bias_ref,
             dq_ref, dk_ref, dv_ref, db_ref, dq_s, dkc_s, dkp_s, dvc_s, dvp_s, kacc, vacc):
        b = pl.program_id(1)

        @pl.when(b == 0)
        def _():
            db_ref[...] = jnp.zeros_like(db_ref)
            kacc[...] = jnp.zeros_like(kacc)
            vacc[...] = jnp.zeros_like(vacc)

        @pl.when(b == nb)
        def _():
            dk_ref[...] = kacc[...].astype(BF16)
            dv_ref[...] = vacc[...].astype(BF16)

        @pl.when(b < nb)
        def _():
            lane = lax.broadcasted_iota(jnp.int32, (ATTN_BLOCK, LANES), 1)
            kj = lax.broadcasted_iota(jnp.int32, (ATTN_BLOCK, 2 * ATTN_BLOCK), 1)
            dead = (b == 0) & (kj < ATTN_BLOCK)

            def one(j, s):
                rows = _unit_rows(j, s, r)
                before = _unit_rows(max(j - 1, 0), s, r)
                k_before = kc_ref[before, :] if j else kp_ref[before, :]
                v_before = vc_ref[before, :] if j else vp_ref[before, :]
                q = q_ref[rows, :]
                kcat = jnp.concatenate([k_before, kc_ref[rows, :]], axis=0).astype(BF16)
                vcat = jnp.concatenate([v_before, vc_ref[rows, :]], axis=0).astype(BF16)
                dov, corrv, lsev = do_ref[rows, :], corr_ref[rows, :], lse_ref[rows, :]
                dq_acc = jnp.zeros((ATTN_BLOCK, LANES), F32)
                dk_acc = jnp.zeros((2 * ATTN_BLOCK, LANES), F32)
                dv_acc = jnp.zeros((2 * ATTN_BLOCK, LANES), F32)
                dss = []
                for hh in range(2):
                    mine = (lane >= ATTN_HEAD_DIM) if hh else (lane < ATTN_HEAD_DIM)
                    col = slice(ATTN_HEAD_DIM * hh, ATTN_HEAD_DIM * hh + 1)
                    qm = jnp.where(mine, q, 0.0).astype(BF16)
                    dom = jnp.where(mine, dov, 0.0).astype(BF16)
                    sc = _dot(qm, kcat, NT) * scale + bias_ref[hh]
                    if j == 0:
                        sc = jnp.where(dead, NEG_INF, sc)
                    p = jnp.exp(sc - lsev[:, col])
                    ds = p * (_dot(dom, vcat, NT) - corrv[:, col])
                    dss.append(ds)
                    dsb = ds.astype(BF16)
                    dq_acc = jnp.where(mine, _dot(dsb, kcat) * scale, dq_acc)
                    dk_acc += _dot(dsb, qm, TN) * scale
                    dv_acc += _dot(p.astype(BF16), dom, TN)
                dq_s[rows, :] = dq_acc
                dkp_s[rows, :] = dk_acc[:ATTN_BLOCK]
                dkc_s[rows, :] = dk_acc[ATTN_BLOCK:]
                dvp_s[rows, :] = dv_acc[:ATTN_BLOCK]
                dvc_s[rows, :] = dv_acc[ATTN_BLOCK:]
                return dss

            def add_bias_grads(results):
                for hh in range(2):
                    db_ref[hh] += functools.reduce(lambda x, y: x + y, [dss[hh] for dss in results])

            _for_units(r, nsub, one, add_bias_grads)
            dq_ref[...] = dq_s[...].astype(BF16)
            tail = slice((nsub - 1) * sub, nsub * sub)
            for acc, before_s, cur_s, out_ref in ((kacc, dkp_s, dkc_s, dk_ref), (vacc, dvp_s, dvc_s, dv_ref)):
                acc[tail, :] += before_s[0:sub, :]
                out_ref[...] = acc[...].astype(BF16)
                acc[...] = cur_s[...]
                for j in range(nsub - 1):
                    acc[j * sub:(j + 1) * sub, :] += before_s[(j + 1) * sub:(j + 2) * sub, :]

    last = nb - 1
    blk = (tb, LANES)
    cur = lambda c: pl.BlockSpec(blk, lambda p, b: (jnp.minimum(b, last), c + p))
    prev = lambda c: pl.BlockSpec(blk, lambda p, b: (jnp.clip(b - 1, 0, last), c + p))
    before = lambda c: pl.BlockSpec((sub, LANES), lambda p, b: (jnp.clip(b * nsub - 1, 0, nb * nsub - 1), c + p))
    tab = pl.BlockSpec((2, ATTN_BLOCK, 2 * ATTN_BLOCK), lambda p, b: (p, 0, 0))
    return pl.pallas_call(
        body, name=name, grid=(2, nb + 1),
        in_specs=[cur(qc), cur(kc), before(kc), cur(vc), before(vc), cur(dc), cur(dc), cur(0), tab],
        out_specs=[cur(0), prev(0), prev(0), tab],
        out_shape=[jax.ShapeDtypeStruct((T, ATTN_GROUP_WIDTH), BF16)] * 3
        + [jax.ShapeDtypeStruct((4, ATTN_BLOCK, 2 * ATTN_BLOCK), F32)],
        scratch_shapes=[pltpu.VMEM(blk, F32)] * 7,
        compiler_params=_params(("arbitrary", "arbitrary")),
    )(proj, proj, proj, proj, proj, do, corr, lse, bias)


def _mix_weights(lses):
    m = jnp.maximum(jnp.maximum(lses[0], lses[1]), lses[2])
    es = [jnp.exp(l - m) for l in lses]
    inv = 1.0 / (es[0] + es[1] + es[2])
    return jnp.concatenate([e * inv for e in es], axis=1)


def _attn_mix(os, lses, proj, tm, name):
    T = proj.shape[0]
    zcol = _OFF["z_attn"] // D_ATTN

    def body(o0, o1, o2, l0, l1, l2, z_ref, out_ref):
        z = z_ref[...]
        o = jnp.concatenate([o0[...], o1[...], o2[...]], axis=1)
        alpha = _mix_weights([l0[...], l1[...], l2[...]])
        out_ref[...] = (o * alpha * (z * _sigmoid(z))).astype(BF16)

    row = lambda i: (i, 0)
    grp = pl.BlockSpec((tm, ATTN_GROUP_WIDTH), row)
    return pl.pallas_call(
        body, name=name, grid=(T // tm,),
        in_specs=[grp] * 6 + [pl.BlockSpec((tm, D_ATTN), lambda i: (i, zcol))],
        out_specs=pl.BlockSpec((tm, D_ATTN), row),
        out_shape=jax.ShapeDtypeStruct((T, D_ATTN), BF16),
        compiler_params=_params(("parallel",)),
    )(*os, *lses, proj)


def _attn_mix_bwd(d, os, lses, proj, tm, name):
    T = proj.shape[0]
    zcol = _OFF["z_attn"] // D_ATTN

    def body(d_ref, o0, o1, o2, l0, l1, l2, z_ref, do_ref, corr_ref, dz_ref):
        dv, z = d_ref[...], z_ref[...]
        ov = jnp.concatenate([o0[...], o1[...], o2[...]], axis=1)
        alpha = _mix_weights([l0[...], l1[...], l2[...]])
        sz = _sigmoid(z)
        oc = ov * alpha
        dz_ref[...] = (dv * oc * (sz * (1.0 + z * (1.0 - sz)))).astype(BF16)
        doc = dv * (z * sz)
        do_ref[...] = doc * alpha
        pr = doc * oc
        p3 = pr[:, 0:256] + pr[:, 256:512] + pr[:, 512:768]
        li = lax.broadcasted_iota(jnp.int32, (256, 256), 0) // ATTN_HEAD_DIM
        lj = lax.broadcasted_iota(jnp.int32, (256, 256), 1) // ATTN_HEAD_DIM
        ones = jnp.where(li == lj, 1.0, 0.0).astype(F32)
        s = lax.dot_general(p3, ones, NN, precision=lax.Precision.HIGHEST, preferred_element_type=F32)
        corr_ref[...] = alpha * jnp.concatenate([s, s, s], axis=1)

    row = lambda i: (i, 0)
    grp = pl.BlockSpec((tm, ATTN_GROUP_WIDTH), row)
    return pl.pallas_call(
        body, name=name, grid=(T // tm,),
        in_specs=[pl.BlockSpec((tm, D_ATTN), row)] + [grp] * 6 + [pl.BlockSpec((tm, D_ATTN), lambda i: (i, zcol))],
        out_specs=[pl.BlockSpec((tm, D_ATTN), row)] * 3,
        out_shape=[jax.ShapeDtypeStruct((T, D_ATTN), F32), jax.ShapeDtypeStruct((T, D_ATTN), F32),
                   jax.ShapeDtypeStruct((T, D_ATTN), BF16)],
        compiler_params=_params(("parallel",)),
    )(d, *os, *lses, proj)


def _mem_probs(q_ref, kv_ref, h):
    hs = slice(MEM_HEAD_DIM * h, MEM_HEAD_DIM * (h + 1))
    qh = q_ref[:, hs].astype(BF16)
    kh = kv_ref[:, hs]
    vh = kv_ref[:, D_MEM + MEM_HEAD_DIM * h:D_MEM + MEM_HEAD_DIM * (h + 1)]
    s = _dot(qh, kh, NT) * (MEM_HEAD_DIM ** -0.5)
    p = jnp.exp(s - jnp.max(s, axis=-1, keepdims=True))
    pn = p / jnp.sum(p, axis=-1, keepdims=True)
    return qh, kh, vh, pn


def _mem_fwd(proj, kv, tm, name):
    T = proj.shape[0]
    M = kv.shape[0]
    qcol, zcol = _OFF["q_mem"] // D_MEM, _OFF["z_mem"] // D_MEM

    def body(q_ref, z_ref, kv_ref, o_ref):
        outs = []
        for h in range(MEM_HEADS):
            _, _, vh, pn = _mem_probs(q_ref, kv_ref, h)
            outs.append(_dot(pn.astype(BF16), vh))
        z = z_ref[...]
        o_ref[...] = (jnp.concatenate(outs, axis=1) * (z * _sigmoid(z))).astype(BF16)

    return pl.pallas_call(
        body, name=name, grid=(T // tm,),
        in_specs=[pl.BlockSpec((tm, D_MEM), lambda i: (i, qcol)), pl.BlockSpec((tm, D_MEM), lambda i: (i, zcol)),
                  _full((M, 2 * D_MEM))],
        out_specs=pl.BlockSpec((tm, D_MEM), lambda i: (i, 0)),
        out_shape=jax.ShapeDtypeStruct((T, D_MEM), BF16),
        compiler_params=_params(("parallel",)),
    )(proj, proj, kv)


def _mem_bwd(d, proj, kv, tm, name):
    T = proj.shape[0]
    M = kv.shape[0]
    qcol, zcol = _OFF["q_mem"] // D_MEM, _OFF["z_mem"] // D_MEM

    def body(d_ref, q_ref, z_ref, kv_ref, dq_ref, dz_ref, dkv_ref):
        @pl.when(pl.program_id(0) == 0)
        def _():
            dkv_ref[...] = jnp.zeros_like(dkv_ref)

        z = z_ref[...]
        sz = _sigmoid(z)
        dv = d_ref[...]
        dov = dv * (z * sz)
        scale = MEM_HEAD_DIM ** -0.5
        outs, dqs = [], []
        for h in range(MEM_HEADS):
            hs = slice(MEM_HEAD_DIM * h, MEM_HEAD_DIM * (h + 1))
            qh, kh, vh, pn = _mem_probs(q_ref, kv_ref, h)
            pnb = pn.astype(BF16)
            oh = _dot(pnb, vh)
            outs.append(oh)
            doh = dov[:, hs]
            dohb = doh.astype(BF16)
            dp = _dot(dohb, vh, NT)
            ds = pn * (dp - jnp.sum(doh * oh, axis=-1, keepdims=True))
            dsb = ds.astype(BF16)
            dqs.append(_dot(dsb, kh) * scale)
            dkv_ref[:, hs] += _dot(dsb, qh, TN) * scale
            vs = slice(D_MEM + MEM_HEAD_DIM * h, D_MEM + MEM_HEAD_DIM * (h + 1))
            dkv_ref[:, vs] += _dot(pnb, dohb, TN)
        dq_ref[...] = jnp.concatenate(dqs, axis=1).astype(BF16)
        dz_ref[...] = (dv * jnp.concatenate(outs, axis=1) * (sz * (1.0 + z * (1.0 - sz)))).astype(BF16)

    row = lambda i: (i, 0)
    return pl.pallas_call(
        body, name=name, grid=(T // tm,),
        in_specs=[pl.BlockSpec((tm, D_MEM), row), pl.BlockSpec((tm, D_MEM), lambda i: (i, qcol)),
                  pl.BlockSpec((tm, D_MEM), lambda i: (i, zcol)), _full((M, 2 * D_MEM))],
        out_specs=[pl.BlockSpec((tm, D_MEM), row), pl.BlockSpec((tm, D_MEM), row), _full((M, 2 * D_MEM))],
        out_shape=[jax.ShapeDtypeStruct((T, D_MEM), BF16), jax.ShapeDtypeStruct((T, D_MEM), BF16),
                   jax.ShapeDtypeStruct((M, 2 * D_MEM), F32)],
        compiler_params=_params(("arbitrary",)),
    )(d, proj, proj, kv)


def _branches_and_gates(os_ref, oa_ref, om_ref, gl_refs, bg_ref, ws_ref, wa_ref, wm_ref):
    outs = (_dot(os_ref[...], ws_ref[...]), _dot(oa_ref[...], wa_ref[...]), _dot(om_ref[...], wm_ref[...]))
    gates = tuple(_sigmoid(jnp.concatenate([gl_refs[2 * k][...], gl_refs[2 * k + 1][...]], axis=1)
                           + bg_ref[:, D_MODEL * k:D_MODEL * (k + 1)]) for k in range(3))
    return outs, gates


def _merge_specs(tm):
    row = lambda i: (i, 0)
    first = _OFF["gates"] // GATE_BLOCK
    gate = [pl.BlockSpec((tm, GATE_BLOCK), (lambda i, k=k: (i, first + k))) for k in range(N_GATES // GATE_BLOCK)]
    return ([pl.BlockSpec((tm, D_SSM), row), pl.BlockSpec((tm, D_ATTN), row), pl.BlockSpec((tm, D_MEM), row)] + gate
            + [_full((1, N_GATES)), _full((D_SSM, D_MODEL)), _full((D_ATTN, D_MODEL)), _full((D_MEM, D_MODEL)),
               _full((D_MODEL, D_MODEL))])


def _merge_fwd(x, o_ssm, o_attn, o_mem, proj, bg, ws, wa, wm, wo, tm, name):
    T = x.shape[0]

    def body(os_ref, oa_ref, om_ref, g0, g1, g2, g3, g4, g5, bg_ref, ws_ref, wa_ref, wm_ref, wo_ref, x_ref,
             xo_ref, mg_ref):
        outs, gates = _branches_and_gates(os_ref, oa_ref, om_ref, (g0, g1, g2, g3, g4, g5), bg_ref, ws_ref, wa_ref,
                                          wm_ref)
        merged = (gates[0] * outs[0] + gates[1] * outs[1] + gates[2] * outs[2]).astype(BF16)
        mg_ref[...] = merged
        xo_ref[...] = x_ref[...] + _dot(merged, wo_ref[...])

    row = lambda i: (i, 0)
    return pl.pallas_call(
        body, name=name, grid=(T // tm,),
        in_specs=_merge_specs(tm) + [pl.BlockSpec((tm, D_MODEL), row)],
        out_specs=[pl.BlockSpec((tm, D_MODEL), row), pl.BlockSpec((tm, D_MODEL), row)],
        out_shape=[jax.ShapeDtypeStruct((T, D_MODEL), F32), jax.ShapeDtypeStruct((T, D_MODEL), BF16)],
        compiler_params=_params(("parallel",)),
    )(o_ssm, o_attn, o_mem, *([proj] * (N_GATES // GATE_BLOCK)), bg, ws, wa, wm, wo, x)


def _merge_bwd(dx, o_ssm, o_attn, o_mem, proj, bg, ws, wa, wm, wo, tm, name, job=None):
    T = dx.shape[0]

    def body(os_ref, oa_ref, om_ref, g0, g1, g2, g3, g4, g5, bg_ref, ws_ref, wa_ref, wm_ref, wo_ref, dx_ref,
             dgl_ref, db_ref, dos_ref, doa_ref, dom_ref, dbg_ref):
        @pl.when(pl.program_id(0) == 0)
        def _():
            dbg_ref[...] = jnp.zeros_like(dbg_ref)

        outs, gates = _branches_and_gates(os_ref, oa_ref, om_ref, (g0, g1, g2, g3, g4, g5), bg_ref, ws_ref, wa_ref,
                                          wm_ref)
        dm = _dot(dx_ref[...].astype(BF16), wo_ref[...], NT)
        w_refs = (ws_ref, wa_ref, wm_ref)
        do_refs = (dos_ref, doa_ref, dom_ref)
        for k in range(3):
            cols = slice(D_MODEL * k, D_MODEL * (k + 1))
            dgl = dm * outs[k] * (gates[k] * (1.0 - gates[k]))
            dgl_ref[:, cols] = dgl.astype(BF16)
            dbg_ref[:, cols] += jnp.sum(dgl, axis=0, keepdims=True)
            dbk = (dm * gates[k]).astype(BF16)
            db_ref[:, cols] = dbk
            do_refs[k][...] = _dot(dbk, w_refs[k][...], NT)

    row = lambda i: (i, 0)
    return _pc(
        body, job, name=name, grid=(T // tm,),
        in_specs=_merge_specs(tm) + [pl.BlockSpec((tm, D_MODEL), row)],
        out_specs=[pl.BlockSpec((tm, N_GATES), row), pl.BlockSpec((tm, N_GATES), row), pl.BlockSpec((tm, D_SSM), row),
                   pl.BlockSpec((tm, D_ATTN), row), pl.BlockSpec((tm, D_MEM), row), _full((1, N_GATES))],
        out_shape=[jax.ShapeDtypeStruct((T, N_GATES), BF16), jax.ShapeDtypeStruct((T, N_GATES), BF16),
                   jax.ShapeDtypeStruct((T, D_SSM), F32), jax.ShapeDtypeStruct((T, D_ATTN), F32),
                   jax.ShapeDtypeStruct((T, D_MEM), F32), jax.ShapeDtypeStruct((1, N_GATES), F32)],
        scratch_shapes=[], sem=("arbitrary",),
        operands=(o_ssm, o_attn, o_mem, *([proj] * (N_GATES // GATE_BLOCK)), bg, ws, wa, wm, wo, dx))


def _loss_head(x, g, target, tm, name):
    T, D = x.shape

    def body(x_ref, g_ref, t_ref, loss_ref, dx_ref, dg_ref):
        @pl.when(pl.program_id(0) == 0)
        def _():
            loss_ref[...] = jnp.zeros_like(loss_ref)
            dg_ref[...] = jnp.zeros_like(dg_ref)

        xv = x_ref[...]
        r = lax.rsqrt(jnp.mean(xv * xv, axis=-1, keepdims=True) + EPS)
        xr = xv * r
        err = xr * g_ref[...] - t_ref[...]
        loss_ref[...] += 0.5 * jnp.sum(jnp.mean(err * err, axis=-1, keepdims=True), axis=0, keepdims=True)
        dy = err * (1.0 / D)
        dg_ref[...] += jnp.sum(dy * xr, axis=0, keepdims=True)
        wv = dy * g_ref[...]
        dx_ref[...] = r * (wv - xr * jnp.mean(wv * xr, axis=-1, keepdims=True))

    row = lambda i: (i, 0)
    return pl.pallas_call(
        body, name=name, grid=(T // tm,),
        in_specs=[pl.BlockSpec((tm, D), row), _full((1, D)), pl.BlockSpec((tm, D), row)],
        out_specs=[_full((1, 128)), pl.BlockSpec((tm, D), row), _full((1, D))],
        out_shape=[jax.ShapeDtypeStruct((1, 128), F32), jax.ShapeDtypeStruct((T, D), F32),
                   jax.ShapeDtypeStruct((1, D), F32)],
        compiler_params=_params(("arbitrary",)),
    )(x, g, target)


def _adamw(parts, w, m, v, tr, name):
    L, R, C = w.shape

    def body(p_ref, w_ref, m_ref, v_ref, g_ref, d_ref, mo_ref, vo_ref):
        g = p_ref[0].astype(F32)
        for s in range(1, N_DEV):
            g = g + p_ref[s].astype(F32)
        mn = ADAM_B1 * m_ref[...] + (1.0 - ADAM_B1) * g
        vn = ADAM_B2 * v_ref[...] + (1.0 - ADAM_B2) * (g * g)
        m_hat = mn / (1.0 - ADAM_B1 ** ADAM_STEP)
        v_hat = vn / (1.0 - ADAM_B2 ** ADAM_STEP)
        g_ref[...] = g
        d_ref[...] = -ADAM_LR * (m_hat / (jnp.sqrt(v_hat) + ADAM_EPS) + ADAM_WD * w_ref[...])
        mo_ref[...] = mn
        vo_ref[...] = vn

    one = pl.BlockSpec((None, tr, C), lambda l, i: (l, i, 0))
    return pl.pallas_call(
        body, name=name, grid=(L, R // tr),
        in_specs=[pl.BlockSpec((N_DEV, None, tr, C), lambda l, i: (0, l, i, 0)), one, one, one],
        out_specs=[one] * 4,
        out_shape=[jax.ShapeDtypeStruct((L, R, C), F32)] * 4,
        compiler_params=_params(("parallel", "parallel")),
    )(parts, w, m, v)


_SHARDED = (("w_in", (1088, 1024), 1), ("w_glu", (96, 768), 0), ("w_mem_kv", (128, 1024), 0),
            ("w_br_ssm", (768, 128), 1), ("w_br_attn", (768, 128), 1), ("w_br_mem", (512, 128), 1),
            ("w_out", (128, 1024), 0))
_W_IN = 0
_SMALL = tuple(range(1, len(_SHARDED)))


class _Job(NamedTuple):
    ins: list
    out_shape: list
    aliases: dict
    pairs: Callable
    n: int


def _peers():
    x, y, c = lax.axis_index("x"), lax.axis_index("y"), lax.axis_index("c")
    me = 4 * x + 2 * y + c
    out = []
    for k in range(1, N_DEV):
        px = 1 - x if k & 4 else x
        py = 1 - y if k & 2 else y
        pc = 1 - c if k & 1 else c
        out.append(((px, py, pc), 4 * px + 2 * py + pc))
    return me, out


def _copies(pairs, send_sems, recv_sems, local_sems, arrivals):
    me, peers = _peers()
    local = [pltpu.make_async_copy(src(me), dst(me), local_sems.at[j]) for j, (src, dst) in enumerate(pairs)]
    sends, recvs = [], []
    for k, (peer, lin) in enumerate(peers):
        for j, (src, dst) in enumerate(pairs):
            for to, out in ((dst(me), sends), (dst(lin), recvs)):
                if out is sends or arrivals:
                    out.append(pltpu.make_async_remote_copy(
                        src_ref=src(lin), dst_ref=to, send_sem=send_sems.at[j, k], recv_sem=recv_sems.at[j, k],
                        device_id=peer, device_id_type=pl.DeviceIdType.MESH))
    return local, sends, recvs


def _start_copies(pairs, *sems):
    local, sends, _ = _copies(pairs, *sems, arrivals=False)
    for cp in local + sends:
        cp.start()


def _wait_copies(pairs, *sems):
    local, sends, recvs = _copies(pairs, *sems, arrivals=True)
    for cp in recvs:
        cp.wait_recv()
    for cp in sends:
        cp.wait_send()
    for cp in local:
        cp.wait()


def _job_scratch(job):
    return [pltpu.SemaphoreType.DMA((job.n, N_DEV - 1)), pltpu.SemaphoreType.DMA((job.n, N_DEV - 1)),
            pltpu.SemaphoreType.DMA((job.n,))]


def _pc(body, job, *, name, grid, in_specs, out_specs, out_shape, scratch_shapes, sem, operands):
    if job is None:
        return pl.pallas_call(body, name=name, grid=grid, in_specs=in_specs, out_specs=out_specs, out_shape=out_shape,
                              scratch_shapes=scratch_shapes, compiler_params=_params(sem))(*operands)
    a = len(in_specs)
    b = a + len(job.ins)
    c = b + len(out_shape)
    d = c + len(job.out_shape)
    e = d + len(scratch_shapes)

    def carried(*refs):
        pairs = job.pairs(refs[a:b], refs[c:d])
        ids = [pl.program_id(k) for k in range(len(grid))]
        first = functools.reduce(jnp.logical_and, [i == 0 for i in ids])
        last = functools.reduce(jnp.logical_and, [i == n - 1 for i, n in zip(ids, grid)])

        @pl.when(first)
        def _():
            _start_copies(pairs, *refs[e:])

        body(*refs[:a], *refs[b:c], *refs[d:e])

        @pl.when(last)
        def _():
            _wait_copies(pairs, *refs[e:])

    hbm = pl.BlockSpec(memory_space=pl.ANY)
    outs = pl.pallas_call(
        carried, name=name, grid=grid,
        in_specs=list(in_specs) + [hbm] * len(job.ins), out_specs=list(out_specs) + [hbm] * len(job.out_shape),
        out_shape=list(out_shape) + list(job.out_shape),
        input_output_aliases={a + i: len(out_shape) + o for i, o in job.aliases.items()},
        scratch_shapes=list(scratch_shapes) + _job_scratch(job),
        compiler_params=_params(("arbitrary",) * len(grid)),
    )(*operands, *job.ins)
    return outs[:len(out_shape)], outs[len(out_shape):]


def _gather_via_sibling(x, take, place, out_shape, name, landing=None):
    def body(*refs):
        x_ref, o_ref = refs[0], refs[-4]
        send_sems, recv_sems, local_sem = refs[-3:]
        x, y, c = lax.axis_index("x"), lax.axis_index("y"), lax.axis_index("c")
        me, sibling = (x, y, c), (x, y, 1 - c)
        chips = [(1 - x, y), (x, 1 - y), (1 - x, 1 - y)]
        src = take(x_ref)

        def slot(px, py, pc):
            return place(o_ref, 4 * px + 2 * py + pc)

        def copy(k, block, to, first_hand):
            return pltpu.make_async_remote_copy(
                src_ref=src if first_hand else slot(*block), dst_ref=slot(*block), send_sem=send_sems.at[k],
                recv_sem=recv_sems.at[k], device_id=to, device_id_type=pl.DeviceIdType.MESH)

        mine = pltpu.make_async_copy(src, slot(*me), local_sem)
        mine.start()
        first = [copy(0, me, sibling, True)] + [copy(1 + j, me, (*chip, c), True) for j, chip in enumerate(chips)]
        for cp in first:
            cp.start()
        passed = []
        for j, chip in enumerate(chips):
            copy(1 + j, (*chip, c), me, True).wait_recv()
            passed.append(copy(4 + j, (*chip, c), sibling, False))
            passed[-1].start()
        copy(0, sibling, me, True).wait_recv()
        for j, chip in enumerate(chips):
            copy(4 + j, (*chip, 1 - c), me, False).wait_recv()
        for cp in first + passed:
            cp.wait_send()
        mine.wait()

    hbm = pl.BlockSpec(memory_space=pl.ANY)
    ins = [x] if landing is None else [x, landing]
    return pl.pallas_call(
        body, name=name, in_specs=[hbm] * len(ins), out_specs=hbm, out_shape=out_shape,
        input_output_aliases={} if landing is None else {1: 0},
        scratch_shapes=[pltpu.SemaphoreType.DMA((N_DEV - 1,)), pltpu.SemaphoreType.DMA((N_DEV - 1,)),
                        pltpu.SemaphoreType.DMA],
    )(*ins)


def _lane_window(ref, who):
    return ref.at[:, pl.ds(pl.multiple_of(who * LANES, LANES), LANES)]


def _gather_job(shards, items):
    out_shape = []
    for i, _ in items:
        _, s, axis = _SHARDED[i]
        whole = i != _W_IN and axis == 1
        out_shape.append(jax.ShapeDtypeStruct((s[0], N_DEV * s[1]) if whole else (N_DEV,) + s, BF16))

    def pairs(in_refs, out_refs):
        out = []
        for (i, l), src, dst in zip(items, in_refs, out_refs):
            if i != _W_IN and _SHARDED[i][2] == 1:
                out.append((lambda who, src=src, l=l: src.at[l], lambda who, dst=dst: _lane_window(dst, who)))
            else:
                out.append((lambda who, src=src, l=l: src.at[l], lambda who, dst=dst: dst.at[who]))
        return out

    return _Job([shards[i] for i, _ in items], out_shape, {}, pairs, len(items))


def _landed_weights(items, landed):
    out = {}
    for (i, _), a in zip(items, landed):
        n, s, axis = _SHARDED[i]
        if i == _W_IN:
            out[n] = a.transpose(2, 0, 1).reshape(D_MODEL, D_IN)
        elif axis == 0:
            out[n] = a.reshape(N_DEV * s[0], s[1])
        else:
            out[n] = a
    return out


def _scatter_job(grads, items, layer, parts=None):
    ng = len(grads)
    out_shape = [jax.ShapeDtypeStruct((N_DEV, DEPTH) + _SHARDED[i][1], BF16) for i in items]

    def pairs(in_refs, out_refs):
        out = []
        for i, src, dst in zip(items, in_refs[:ng], out_refs):
            _, s, axis = _SHARDED[i]
            if i == _W_IN:
                take = lambda who, src=src: src.at[who]
            elif axis == 0:
                take = lambda who, src=src, s=s: src.at[pl.ds(pl.multiple_of(who * s[0], 16), s[0])]
            else:
                take = lambda who, src=src: _lane_window(src, who)
            out.append((take, lambda who, dst=dst: dst.at[who, layer]))
        return out

    aliases = {} if parts is None else {ng + j: j for j in range(len(items))}
    return _Job(list(grads) + ([] if parts is None else list(parts)), out_shape, aliases, pairs, len(items))


def _rows_job(src, row0, landing=None):
    n = src.shape[0]
    pairs = lambda in_refs, out_refs: [(lambda who: in_refs[0], lambda who: out_refs[0].at[who, pl.ds(row0, n)])]
    return _Job([src] + ([] if landing is None else [landing]), [jax.ShapeDtypeStruct((N_DEV, _REP_ROWS, LANES), F32)],
                {} if landing is None else {1: 0}, pairs, 1)


_REPLICATED = (("norm_g", (2, 1024)), ("mem_norm_g", (2, 1024)), ("b_gate", (2, 3072)),
               ("ssm_lambda_re", (2, 48, 64)), ("ssm_lambda_im", (2, 48, 64)), ("ssm_log_dt", (2, 48)),
               ("ssm_b_re", (2, 48, 64, 16)), ("ssm_b_im", (2, 48, 64, 16)), ("ssm_c_re", (2, 48, 16, 64)),
               ("ssm_c_im", (2, 48, 16, 64)), ("ssm_d", (2, 768)), ("b_glu", (2, 768)), ("rel_bias", (32, 12)),
               ("final_norm_g", (1024,)))
_PER_LAYER = tuple((n, s[1:]) for n, s in _REPLICATED if s[0] == DEPTH and len(s) > 1)
_SHARED = tuple((n, s) for n, s in _REPLICATED if (n, s[1:]) not in _PER_LAYER)
_REP_HALF_ROWS = 1664
_REP_ROWS = 2 * _REP_HALF_ROWS
assert sum(int(np.prod(s)) for _, s in _PER_LAYER + _SHARED) <= _REP_HALF_ROWS * LANES


def _pack_half(tree, layer, shared):
    flat = [tree[n][layer].reshape(-1) for n, _ in _PER_LAYER]
    if shared:
        flat += [tree[n].reshape(-1) for n, _ in _SHARED]
    flat = jnp.concatenate(flat)
    return jnp.pad(flat, (0, _REP_HALF_ROWS * LANES - flat.shape[0])).reshape(_REP_HALF_ROWS, LANES)


def _pack_replicated(tree):
    return jnp.concatenate([_pack_half(tree, 1, False), _pack_half(tree, 0, True)])[None]


def _unpack_replicated(packed):
    halves = packed.reshape(2, -1)
    out, r = {}, 0
    for n, s in _PER_LAYER:
        size = int(np.prod(s))
        out[n] = jnp.stack([halves[1, r:r + size].reshape(s), halves[0, r:r + size].reshape(s)])
        r += size
    for n, s in _SHARED:
        size = int(np.prod(s))
        out[n] = halves[1, r:r + size].reshape(s)
        r += size
    return out


def _owner_rows_w_in(dw):
    return dw.reshape(D_MODEL, N_DEV, D_IN // N_DEV).transpose(1, 2, 0)


def _discretize(lam_re, lam_im, log_dt, b_re, b_im):
    dt = jnp.exp(log_dt)[:, None]
    mag = jnp.exp(lam_re * dt)
    abar_re, abar_im = mag * jnp.cos(lam_im * dt), mag * jnp.sin(lam_im * dt)
    den = lam_re * lam_re + lam_im * lam_im
    nr, ni = abar_re - 1.0, abar_im
    f_re = (nr * lam_re + ni * lam_im) / den
    f_im = (ni * lam_re - nr * lam_im) / den
    bbar_re = f_re[..., None] * b_re - f_im[..., None] * b_im
    bbar_im = f_re[..., None] * b_im + f_im[..., None] * b_re
    return abar_re, abar_im, bbar_re, bbar_im


def _block_diag(a):
    _, R, C = a.shape
    a = a.reshape(SSM_BLOCKS, 8, R, C)
    eye = jnp.eye(8, dtype=a.dtype)
    return (a[:, :, :, None, :] * eye[None, :, None, :, None]).reshape(SSM_BLOCKS, 8 * R, 8 * C)


def _diag_blocks(a, R, C):
    a = a.reshape(SSM_BLOCKS, 8, R, 8, C)
    eye = jnp.eye(8, dtype=a.dtype)
    return jnp.sum(a * eye[None, :, None, :, None], axis=3).reshape(SSM_GROUPS, R, C)


def _carried(result, job):
    return (result, None) if job is None else result


def _layer_fwd(x, mem, W, P, bias, layer, jobs):
    tag = f"l{layer}"
    abar_re, abar_im, bbar_re, bbar_im = _discretize(P["ssm_lambda_re"][layer], P["ssm_lambda_im"][layer],
                                                     P["ssm_log_dt"][layer], P["ssm_b_re"][layer], P["ssm_b_im"][layer])
    c_re, c_im = P["ssm_c_re"][layer], P["ssm_c_im"][layer]
    ssm = dict(
        are=abar_re.reshape(1, N_STATE), aim=abar_im.reshape(1, N_STATE),
        bre=_block_diag(bbar_re.transpose(0, 2, 1)).astype(BF16), bim=_block_diag(bbar_im.transpose(0, 2, 1)).astype(BF16),
        cre=_block_diag(c_re.transpose(0, 2, 1)).astype(BF16), cimn=_block_diag(-c_im.transpose(0, 2, 1)).astype(BF16),
        ctre=_block_diag(c_re).astype(BF16), ctimn=_block_diag(-c_im).astype(BF16),
        btre=_block_diag(bbar_re).astype(BF16), btim=_block_diag(bbar_im).astype(BF16),
        d=P["ssm_d"][layer].reshape(1, D_SSM))
    bglu = P["b_glu"][layer].reshape(1, D_SSM)
    bgate = P["b_gate"][layer].reshape(1, N_GATES)
    g = P["norm_g"][layer].reshape(1, D_MODEL)
    gm = P["mem_norm_g"][layer].reshape(1, D_MODEL)
    delivered = {}

    def carry(stage):
        return jobs[stage][0] if stage in jobs else None

    def deliver(stage, landed):
        if landed is not None:
            delivered[stage] = _landed_weights(jobs[stage][1], landed)

    T = x.shape[0]
    (proj, h), landed = _carried(_norm_proj(x, g, W["w_in"], min(T, 1024), 2176, f"{tag}_proj", job=carry("proj")),
                                 carry("proj"))
    deliver("proj", landed)
    W = {**W, **delivered.get("proj", {})}
    (xr, xi, y, o_ssm), landed = _carried(
        _ssm_fwd(proj, ssm["bre"], ssm["bim"], ssm["cre"], ssm["cimn"], ssm["are"], ssm["aim"], ssm["d"], W["w_glu"],
                 bglu, 256, f"{tag}_ssm", job=carry("ssm")), carry("ssm"))
    deliver("ssm", landed)
    os, lses = [], []
    for grp in range(3):
        stage = f"attn{grp}"
        (o_g, lse_g), landed = _carried(_attn_fwd(proj, bias[grp], grp, f"{tag}_{stage}", job=carry(stage)), carry(stage))
        deliver(stage, landed)
        os.append(o_g)
        lses.append(lse_g)
    o_attn = _attn_mix(os, lses, proj, 512, f"{tag}_attn_mix")
    kvb, hm = _norm_proj(mem, gm, W["w_mem_kv"], mem.shape[0], 1024, f"{tag}_mem_kv", out_dtype=BF16)
    o_mem = _mem_fwd(proj, kvb, 512, f"{tag}_mem")
    x_out, merged = _merge_fwd(x, o_ssm, o_attn, o_mem, proj, bgate, W["w_br_ssm"], W["w_br_attn"], W["w_br_mem"],
                               W["w_out"], 256, f"{tag}_merge")
    res = dict(x=x, mem=mem, proj=proj, h=h, xr=xr, xi=xi, y=y, o_ssm=o_ssm, os=os, lses=lses,
               o_attn=o_attn, kvb=kvb, hm=hm, o_mem=o_mem, merged=merged, ssm=ssm, bglu=bglu,
               bgate=bgate, g=g, gm=gm, W=W)
    return x_out, res, delivered


def _layer_bwd(dx, res, P, bias, layer, jobs):
    tag = f"l{layer}b"
    proj, ssm, W = res["proj"], res["ssm"], res["W"]
    T = dx.shape[0]
    landed = {}

    def run(stage, fn, job):
        out, landed[stage] = _carried(fn(job), job)
        if job is None:
            del landed[stage]
        return out

    dgl, dbr, do_ssm, do_attn, do_mem, dbg = run(
        "merge", lambda job: _merge_bwd(dx, res["o_ssm"], res["o_attn"], res["o_mem"], proj, res["bgate"], W["w_br_ssm"],
                                        W["w_br_attn"], W["w_br_mem"], W["w_out"], 256, f"{tag}_merge", job=job),
        jobs.get("merge"))
    gw = {}
    gw["w_out"] = _mm_tn(res["merged"], dx, 1024, 1024, 512, f"{tag}_dw_out")
    gw["w_br_ssm"] = _mm_tn(res["o_ssm"], dbr, 768, 1024, 512, f"{tag}_dw_br_ssm", b_col=0, n=1024)
    gw["w_br_attn"] = _mm_tn(res["o_attn"], dbr, 768, 1024, 512, f"{tag}_dw_br_attn", b_col=1024, n=1024)
    gw["w_br_mem"] = _mm_tn(res["o_mem"], dbr, 512, 1024, 512, f"{tag}_dw_br_mem", b_col=2048, n=1024)

    dqm, dzm, dkv = _mem_bwd(do_mem, proj, res["kvb"], 512, f"{tag}_mem")
    M = dkv.shape[0]
    gw["w_mem_kv"] = _mm_tn(res["hm"], dkv, 1024, 1024, M, f"{tag}_dw_mem_kv")
    _, dgm = _proj_bwd(dkv.astype(BF16), W["w_mem_kv"], res["mem"], res["gm"], jnp.zeros_like(res["mem"]), M, 1024,
                       f"{tag}_mem_norm")

    do_g, corr, dza = _attn_mix_bwd(do_attn, res["os"], res["lses"], proj, 512, f"{tag}_attn_mix")
    dqs, dks, dvs, dbs = [], [], [], []
    for grp in range(3):
        dq_g, dk_g, dv_g, db_g = _attn_bwd(proj, do_g, corr, res["lses"][grp], bias[grp], grp, f"{tag}_attn{grp}")
        dqs.append(dq_g)
        dks.append(dk_g)
        dvs.append(dv_g)
        dbs.append(db_g)
    dbias = jnp.stack(dbs)

    dy, dzs, gelu_b, dt_b, dbglu = _glu_bwd(do_ssm, res["y"], proj, W["w_glu"], res["bglu"], 512, f"{tag}_glu")
    gw["w_glu"] = _mm_tn(gelu_b, dt_b, 768, 768, 512, f"{tag}_dw_glu")
    du, dbre, dbim, dcre, dcim, dare, daim, dd = run(
        "ssm", lambda job: _ssm_bwd(dy, proj, res["xr"], res["xi"], ssm["ctre"], ssm["ctimn"], ssm["btre"], ssm["btim"],
                                    ssm["are"], ssm["aim"], ssm["d"], 256, f"{tag}_ssm", job=job), jobs.get("ssm"))
    _, disc_vjp = jax.vjp(_discretize, P["ssm_lambda_re"][layer], P["ssm_lambda_im"][layer], P["ssm_log_dt"][layer],
                          P["ssm_b_re"][layer], P["ssm_b_im"][layer])
    d_lre, d_lim, d_ldt, d_bre, d_bim = disc_vjp((dare.reshape(SSM_GROUPS, SSM_STATE), daim.reshape(SSM_GROUPS, SSM_STATE),
                                                  _diag_blocks(dbre, SSM_STATE, SSM_GROUP),
                                                  _diag_blocks(dbim, SSM_STATE, SSM_GROUP)))

    small = [gw[_SHARDED[i][0]] for i in _SMALL]
    dproj = jnp.concatenate([du, dzs] + dqs + dks + dvs + [dza, dqm, dzm, dgl], axis=1)
    dw_in = run("dw_in", lambda job: _mm_tn(res["h"], dproj, 1024, 2176, min(T, 1024), f"{tag}_dw_in", job=job),
                jobs["dw_in"](small) if "dw_in" in jobs else None)
    dw_in = _owner_rows_w_in(dw_in)
    dx_in, dg = run("proj", lambda job: _proj_bwd(dproj, W["w_in"], res["x"], res["g"], dx, min(T, 1024), 2176,
                                                  f"{tag}_proj", job=job),
                    jobs["proj"](small, dw_in, landed) if "proj" in jobs else None)

    gp = dict(norm_g=dg[0], mem_norm_g=dgm[0], b_gate=dbg[0], ssm_lambda_re=d_lre, ssm_lambda_im=d_lim,
              ssm_log_dt=d_ldt, ssm_b_re=d_bre, ssm_b_im=d_bim,
              ssm_c_re=_diag_blocks(dcre, SSM_GROUP, SSM_STATE), ssm_c_im=_diag_blocks(dcim, SSM_GROUP, SSM_STATE),
              ssm_d=dd[0], b_glu=dbglu[0])
    return dx_in, dw_in, gp, dbias, landed


def _train_step(x, mem, target, shards, P):
    rest0 = [(i, 0) for i in _SMALL]
    rows1 = [(i, 1) for i in _SMALL if _SHARDED[i][2] == 0]
    cols1 = [(i, 1) for i in _SMALL if _SHARDED[i][2] == 1]
    first = [(_W_IN, 0)]
    w_in0 = _gather_via_sibling(shards[_W_IN], lambda ref: ref.at[0], lambda ref, s: ref.at[s],
                                jax.ShapeDtypeStruct((N_DEV,) + _SHARDED[_W_IN][1], BF16), "gather_w_in0")
    W0 = _landed_weights(first, [w_in0])
    buckets = _bucket_tables()
    bias = _bias_tables(P["rel_bias"], buckets, "bias_tables")
    jobs0 = {"proj": (_gather_job(shards, rest0), rest0), "ssm": (_gather_job(shards, [(_W_IN, 1)]), [(_W_IN, 1)]),
             "attn0": (_gather_job(shards, rows1), rows1), "attn1": (_gather_job(shards, cols1), cols1)}
    x, res0, delivered = _layer_fwd(x, mem, W0, P, bias, 0, jobs0)
    W1 = {**delivered["ssm"], **delivered["attn0"], **delivered["attn1"]}
    x, res1, _ = _layer_fwd(x, mem, W1, P, bias, 1, {})
    loss, dx, dgf = _loss_head(x, P["final_norm_g"].reshape(1, D_MODEL), target, 512, "loss_head")

    dx, dw_in1, gp1, dbias1, landed1 = _layer_bwd(
        dx, res1, P, bias, 1, {"proj": lambda small, dw_in, landed: _scatter_job(small, _SMALL, 1)})
    rep1 = _pack_half({n: a[None] for n, a in gp1.items()}, 0, False)
    dx, _, gp0, dbias0, landed0 = _layer_bwd(
        dx, res0, P, bias, 0,
        {"merge": _rows_job(rep1, 0), "ssm": _scatter_job([dw_in1], [_W_IN], 1),
         "dw_in": lambda small: _scatter_job(small, _SMALL, 0, parts=landed1["proj"]),
         "proj": lambda small, dw_in, landed: _scatter_job([dw_in], [_W_IN], 0, parts=landed["ssm"])})
    d_rel = _bias_grad(dbias0, dbias1, buckets, "bias_grad")
    gp0 = {n: a[None] for n, a in gp0.items()}
    gp0["rel_bias"] = jnp.sum(d_rel, axis=-1).transpose(2, 0, 1).reshape(NUM_BUCKETS, 12)
    gp0["final_norm_g"] = dgf[0]
    rep0 = _pack_half(gp0, 0, True)
    rparts = _gather_via_sibling(rep0, lambda ref: ref, lambda ref, s: ref.at[s, pl.ds(_REP_HALF_ROWS, _REP_HALF_ROWS)],
                                 jax.ShapeDtypeStruct((N_DEV, _REP_ROWS, LANES), F32), "gather_small_grads0",
                                 landing=landed0["merge"][0])
    return loss[0, 0], dx, list(landed0["proj"]) + list(landed0["dw_in"]), rparts


_WEIGHTS = ["norm_g", "mem_norm_g", "w_in", "b_gate", "ssm_lambda_re", "ssm_lambda_im", "ssm_log_dt", "ssm_b_re",
            "ssm_b_im", "ssm_c_re", "ssm_c_im", "ssm_d", "w_glu", "b_glu", "w_mem_kv", "w_br_ssm", "w_br_attn",
            "w_br_mem", "w_out", "rel_bias", "final_norm_g"]
_ADAM_ROWS = {"w_in": 136,"w_glu": 96, "w_mem_kv": 128, "w_br_ssm": 768, "w_br_attn": 768, "w_br_mem": 512,
              "w_out": 128}


def kernel(x, mem, norm_g, mem_norm_g, w_in, b_gate, ssm_lambda_re, ssm_lambda_im, ssm_log_dt, ssm_b_re, ssm_b_im, ssm_c_re, ssm_c_im, ssm_d, w_glu, b_glu, w_mem_kv, w_br_ssm, w_br_attn, w_br_mem, w_out, rel_bias, final_norm_g, loss_target, m_norm_g, m_mem_norm_g, m_w_in, m_b_gate, m_ssm_lambda_re, m_ssm_lambda_im, m_ssm_log_dt, m_ssm_b_re, m_ssm_b_im, m_ssm_c_re, m_ssm_c_im, m_ssm_d, m_w_glu, m_b_glu, m_w_mem_kv, m_w_br_ssm, m_w_br_attn, m_w_br_mem, m_w_out, m_rel_bias, m_final_norm_g, v_norm_g, v_mem_norm_g, v_w_in, v_b_gate, v_ssm_lambda_re, v_ssm_lambda_im, v_ssm_log_dt, v_ssm_b_re, v_ssm_b_im, v_ssm_c_re, v_ssm_c_im, v_ssm_d, v_w_glu, v_b_glu, v_w_mem_kv, v_w_br_ssm, v_w_br_attn, v_w_br_mem, v_w_out, v_rel_bias, v_final_norm_g):
    given = dict(locals())
    w = {n: given[n] for n in _WEIGHTS}
    m = {n: given["m_" + n] for n in _WEIGHTS}
    v = {n: given["v_" + n] for n in _WEIGHTS}

    turned = lambda n, a: a.swapaxes(1, 2) if n == "w_in" else a
    shards = [turned(n, w[n]).astype(BF16) for n, _, _ in _SHARDED]
    loss, dx, parts, rparts = _train_step(x[0], mem[0], loss_target[0], shards, w)
    loss = lax.psum(loss, ("x", "y", "c"))

    new = {}
    for (n, _, _), p in zip(_SHARDED, parts):
        new[n] = [turned(n, a) for a in _adamw(p, turned(n, w[n]), turned(n, m[n]), turned(n, v[n]), _ADAM_ROWS[n],
                                               f"adamw_{n}")]
    rp = [_unpack_replicated(a) for a in _adamw(rparts[:, None], _pack_replicated(w), _pack_replicated(m),
                                                _pack_replicated(v), _REP_ROWS // 4, "adamw_replicated")]
    for n, _ in _REPLICATED:
        new[n] = [rp[kind][n] for kind in range(4)]
    outs = [loss, dx[None]]
    for kind in range(4):
        outs.extend(new[n][kind] for n in _WEIGHTS)
    return tuple(outs)
```

```python
import functools
import math
from typing import Callable, NamedTuple

import jax
import jax.numpy as jnp
import numpy as np
from jax import lax
from jax.experimental import pallas as pl
from jax.experimental.pallas import tpu as pltpu

F32 = jnp.float32
BF16 = jnp.bfloat16

D_MODEL = 1024
DEPTH = 2
EPS = 1e-6
D_SSM = 768
SSM_GROUP = 16
SSM_GROUPS = 48
SSM_STATE = 64
N_STATE = SSM_GROUPS * SSM_STATE
SSM_BLOCKS = 6
D_ATTN = 768
ATTN_HEAD_DIM = 64
ATTN_GROUP_WIDTH = 256
ATTN_DILATIONS = (1, 4, 16)
ATTN_SPAN = 128
ATTN_BLOCK = 128
NUM_BUCKETS = 32
REL_MAX_DISTANCE = 2048
NEG_INF = -1e30
MEM_HEADS = 4
MEM_HEAD_DIM = 128
D_MEM = 512
N_GATES = 3 * D_MODEL
D_IN = 8704
N_DEV = 8
LANES = 128
ADAM_LR = 0.001
ADAM_B1 = 0.9
ADAM_B2 = 0.999
ADAM_EPS = 1e-08
ADAM_WD = 0.01
ADAM_STEP = 10

_OFF = {"u": 0, "z_ssm": 768, "q": 1536, "k": 2304, "v": 3072, "z_attn": 3840, "q_mem": 4608, "z_mem": 5120,
        "gates": 5632}
GATE_BLOCK = 512

NN = (((1,), (0,)), ((), ()))
NT = (((1,), (1,)), ((), ()))
TN = (((0,), (0,)), ((), ()))

VMEM_LIMIT = 56 * 1024 * 1024


def _dot(a, b, dims=NN):
    return lax.dot_general(a, b, dims, preferred_element_type=F32)


def _sigmoid(x):
    return 1.0 / (1.0 + jnp.exp(-x))


def _gelu_parts(x):
    k = math.sqrt(2.0 / math.pi)
    t = jnp.tanh(k * (x + 0.044715 * (x * x * x)))
    cdf = 0.5 * (1.0 + t)
    dcdf = 0.5 * (1.0 - t * t) * k * (1.0 + 3.0 * 0.044715 * (x * x))
    return x * cdf, cdf + x * dcdf


def _params(sem, vmem=VMEM_LIMIT):
    return pltpu.CompilerParams(dimension_semantics=sem, vmem_limit_bytes=vmem)


def _full(shape):
    return pl.BlockSpec(shape, lambda *_: (0,) * len(shape))


def _norm_proj(x, g, w, tm, tn, name, out_dtype=F32, job=None, w_turned=False):
    T, D = x.shape
    N = w.shape[0] if w_turned else w.shape[1]
    w_spec = pl.BlockSpec((tn, D), lambda i, j: (j, 0)) if w_turned else pl.BlockSpec((D, tn), lambda i, j: (0, j))
    dims = NT if w_turned else NN

    def body(x_ref, g_ref, w_ref, o_ref, h_ref, hs):
        @pl.when(pl.program_id(1) == 0)
        def _():
            xv = x_ref[...]
            r = lax.rsqrt(jnp.mean(xv * xv, axis=-1, keepdims=True) + EPS)
            hv = (xv * r * g_ref[...]).astype(BF16)
            hs[...] = hv
            h_ref[...] = hv

        o_ref[...] = _dot(hs[...], w_ref[...], dims).astype(out_dtype)

    return _pc(
        body, job, name=name, grid=(T // tm, N // tn),
        in_specs=[pl.BlockSpec((tm, D), lambda i, j: (i, 0)), _full((1, D)), w_spec],
        out_specs=[pl.BlockSpec((tm, tn), lambda i, j: (i, j)), pl.BlockSpec((tm, D), lambda i, j: (i, 0))],
        out_shape=[jax.ShapeDtypeStruct((T, N), out_dtype), jax.ShapeDtypeStruct((T, D), BF16)],
        scratch_shapes=[pltpu.VMEM((tm, D), BF16)], sem=("parallel", "arbitrary"), operands=(x, g, w))


def _mm_tn(a, b, tm, tn, tk, name, b_col=0, n=None, job=None):
    K, M = a.shape
    N = b.shape[1] if n is None else n
    nk = K // tk
    j0 = b_col // tn

    def body(a_ref, b_ref, o_ref, acc):
        k = pl.program_id(2)

        @pl.when(k == 0)
        def _():
            acc[...] = jnp.zeros_like(acc)

        acc[...] += _dot(a_ref[...].astype(BF16), b_ref[...].astype(BF16), TN)

        @pl.when(k == nk - 1)
        def _():
            o_ref[...] = acc[...].astype(BF16)

    out = _pc(
        body, job, name=name, grid=(M // tm, N // tn, nk),
        in_specs=[pl.BlockSpec((tk, tm), lambda i, j, k: (k, i)), pl.BlockSpec((tk, tn), lambda i, j, k: (k, j0 + j))],
        out_specs=[pl.BlockSpec((tm, tn), lambda i, j, k: (i, j))],
        out_shape=[jax.ShapeDtypeStruct((M, N), BF16)],
        scratch_shapes=[pltpu.VMEM((tm, tn), F32)], sem=("parallel", "parallel", "arbitrary"), operands=(a, b))
    return out[0] if job is None else (out[0][0], out[1])


def _proj_bwd(dp, w, x, g, dres, tm, tk, name, job=None, w_turned=False):
    T, N = dp.shape
    D = x.shape[1]
    nk = N // tk
    w_spec = pl.BlockSpec((tk, D), lambda i, k: (k, 0)) if w_turned else pl.BlockSpec((D, tk), lambda i, k: (0, k))
    dims = NN if w_turned else NT

    def body(dp_ref, w_ref, x_ref, g_ref, dres_ref, dx_ref, dg_ref, acc):
        i, k = pl.program_id(0), pl.program_id(1)

        @pl.when(k == 0)
        def _():
            acc[...] = jnp.zeros_like(acc)

        @pl.when((i == 0) & (k == 0))
        def _():
            dg_ref[...] = jnp.zeros_like(dg_ref)

        acc[...] += _dot(dp_ref[...], w_ref[...], dims)

        @pl.when(k == nk - 1)
        def _():
            xv = x_ref[...]
            dh = acc[...]
            r = lax.rsqrt(jnp.mean(xv * xv, axis=-1, keepdims=True) + EPS)
            xr = xv * r
            dg_ref[...] += jnp.sum(dh * xr, axis=0, keepdims=True)
            wv = dh * g_ref[...]
            dx_ref[...] = dres_ref[...] + r * (wv - xr * jnp.mean(wv * xr, axis=-1, keepdims=True))

    return _pc(
        body, job, name=name, grid=(T // tm, nk),
        in_specs=[pl.BlockSpec((tm, tk), lambda i, k: (i, k)), w_spec,
                  pl.BlockSpec((tm, D), lambda i, k: (i, 0)), _full((1, D)),
                  pl.BlockSpec((tm, D), lambda i, k: (i, 0))],
        out_specs=[pl.BlockSpec((tm, D), lambda i, k: (i, 0)), _full((1, D))],
        out_shape=[jax.ShapeDtypeStruct((T, D), F32), jax.ShapeDtypeStruct((1, D), F32)],
        scratch_shapes=[pltpu.VMEM((tm, D), F32)], sem=("arbitrary", "arbitrary"), operands=(dp, w, x, g, dres))


def _ssm_fwd(proj, bre, bim, cre, cimn, are, aim, d, wglu, bglu, tc, name, job=None):
    T = proj.shape[0]
    ucol, zcol = _OFF["u"] // D_SSM, _OFF["z_ssm"] // D_SSM

    def body(u_ref, z_ref, bre_ref, bim_ref, cre_ref, cim_ref, are_ref, aim_ref, d_ref, wg_ref, bg_ref,
             xr_ref, xi_ref, y_ref, o_ref, car_r, car_i):
        @pl.when(pl.program_id(0) == 0)
        def _():
            car_r[...] = jnp.zeros_like(car_r)
            car_i[...] = jnp.zeros_like(car_i)

        u = u_ref[...]
        ub = u.astype(BF16)
        for k in range(SSM_BLOCKS):
            uk = ub[:, 128 * k:128 * (k + 1)]
            xr_ref[:, 512 * k:512 * (k + 1)] = _dot(uk, bre_ref[k])
            xi_ref[:, 512 * k:512 * (k + 1)] = _dot(uk, bim_ref[k])
        ar, ai = are_ref[...], aim_ref[...]

        def step(t, c):
            pr, pi = c
            nr = ar * pr - ai * pi + xr_ref[pl.ds(t, 1), :]
            ni = ar * pi + ai * pr + xi_ref[pl.ds(t, 1), :]
            xr_ref[pl.ds(t, 1), :] = nr
            xi_ref[pl.ds(t, 1), :] = ni
            return nr, ni

        pr, pi = lax.fori_loop(0, tc, step, (car_r[...], car_i[...]))
        car_r[...] = pr
        car_i[...] = pi

        ys = []
        for k in range(SSM_BLOCKS):
            xrk = xr_ref[:, 512 * k:512 * (k + 1)].astype(BF16)
            xik = xi_ref[:, 512 * k:512 * (k + 1)].astype(BF16)
            ys.append(_dot(xrk, cre_ref[k]) + _dot(xik, cim_ref[k]))
        y = jnp.concatenate(ys, axis=1) + d_ref[...] * u
        y_ref[...] = y
        gl, _ = _gelu_parts(y)
        t = _dot(gl.astype(BF16), wg_ref[...]) + bg_ref[...]
        z = z_ref[...]
        o_ref[...] = (gl * _sigmoid(t) * (z * _sigmoid(z))).astype(BF16)

    return _pc(
        body, job, name=name, grid=(T // tc,),
        in_specs=[pl.BlockSpec((tc, D_SSM), lambda i: (i, ucol)), pl.BlockSpec((tc, D_SSM), lambda i: (i, zcol)),
                  _full((SSM_BLOCKS, 128, 512)), _full((SSM_BLOCKS, 128, 512)),
                  _full((SSM_BLOCKS, 512, 128)), _full((SSM_BLOCKS, 512, 128)),
                  _full((1, N_STATE)), _full((1, N_STATE)), _full((1, D_SSM)),
                  _full((D_SSM, D_SSM)), _full((1, D_SSM))],
        out_specs=[pl.BlockSpec((tc, N_STATE), lambda i: (i, 0)), pl.BlockSpec((tc, N_STATE), lambda i: (i, 0)),
                   pl.BlockSpec((tc, D_SSM), lambda i: (i, 0)), pl.BlockSpec((tc, D_SSM), lambda i: (i, 0))],
        out_shape=[jax.ShapeDtypeStruct((T, N_STATE), F32), jax.ShapeDtypeStruct((T, N_STATE), F32),
                   jax.ShapeDtypeStruct((T, D_SSM), F32), jax.ShapeDtypeStruct((T, D_SSM), BF16)],
        scratch_shapes=[pltpu.VMEM((1, N_STATE), F32), pltpu.VMEM((1, N_STATE), F32)], sem=("arbitrary",),
        operands=(proj, proj, bre, bim, cre, cimn, are, aim, d, wglu, bglu))


def _glu_bwd(do, y, proj, wglu, bglu, tm, name):
    T = y.shape[0]
    zcol = _OFF["z_ssm"] // D_SSM

    def body(do_ref, y_ref, z_ref, wg_ref, bg_ref, dy_ref, dz_ref, g_ref, dt_ref, db_ref):
        @pl.when(pl.program_id(0) == 0)
        def _():
            db_ref[...] = jnp.zeros_like(db_ref)

        dov = do_ref[...]
        gl, dgl = _gelu_parts(y_ref[...])
        glb = gl.astype(BF16)
        sg = _sigmoid(_dot(glb, wg_ref[...]) + bg_ref[...])
        z = z_ref[...]
        sz = _sigmoid(z)
        dz_ref[...] = (dov * (gl * sg) * (sz * (1.0 + z * (1.0 - sz)))).astype(BF16)
        dy2 = dov * (z * sz)
        dt = dy2 * gl * (sg * (1.0 - sg))
        dtb = dt.astype(BF16)
        dg = dy2 * sg + _dot(dtb, wg_ref[...], NT)
        dy_ref[...] = dg * dgl
        g_ref[...] = glb
        dt_ref[...] = dtb
        db_ref[...] += jnp.sum(dt, axis=0, keepdims=True)

    row = lambda i: (i, 0)
    return pl.pallas_call(
        body, name=name, grid=(T // tm,),
        in_specs=[pl.BlockSpec((tm, D_SSM), row), pl.BlockSpec((tm, D_SSM), row),
                  pl.BlockSpec((tm, D_SSM), lambda i: (i, zcol)), _full((D_SSM, D_SSM)), _full((1, D_SSM))],
        out_specs=[pl.BlockSpec((tm, D_SSM), row)] * 4 + [_full((1, D_SSM))],
        out_shape=[jax.ShapeDtypeStruct((T, D_SSM), F32), jax.ShapeDtypeStruct((T, D_SSM), BF16),
                   jax.ShapeDtypeStruct((T, D_SSM), BF16), jax.ShapeDtypeStruct((T, D_SSM), BF16),
                   jax.ShapeDtypeStruct((1, D_SSM), F32)],
        compiler_params=_params(("arbitrary",)),
    )(do, y, proj, wglu, bglu)


def _ssm_bwd(dy, proj, xr, xi, ctre, ctimn, btre, btim, are, aim, d, tc, name, job=None):
    T = dy.shape[0]
    nc = T // tc
    ucol = _OFF["u"] // D_SSM
    rb = tc // 8

    def body(dy_ref, u_ref, xr_ref, xi_ref, xpr_ref, xpi_ref, ctre_ref, ctim_ref, btre_ref, btim_ref,
             are_ref, aim_ref, d_ref,
             du_ref, dbre_ref, dbim_ref, dcre_ref, dcim_ref, dare_ref, daim_ref, dd_ref, gr, gi, car_r, car_i):
        i = pl.program_id(0)

        @pl.when(i == 0)
        def _():
            for ref in (car_r, car_i, dbre_ref, dbim_ref, dcre_ref, dcim_ref, dare_ref, daim_ref, dd_ref):
                ref[...] = jnp.zeros_like(ref)

        dyv = dy_ref[...]
        dyb = dyv.astype(BF16)
        u = u_ref[...]
        ub = u.astype(BF16)
        for k in range(SSM_BLOCKS):
            dk = dyb[:, 128 * k:128 * (k + 1)]
            gr[:, 512 * k:512 * (k + 1)] = _dot(dk, ctre_ref[k])
            gi[:, 512 * k:512 * (k + 1)] = _dot(dk, ctim_ref[k])
        ar, ai = are_ref[...], aim_ref[...]

        def step(s, c):
            pr, pi = c
            t = tc - 1 - s
            nr = gr[pl.ds(t, 1), :] + ar * pr + ai * pi
            ni = gi[pl.ds(t, 1), :] + ar * pi - ai * pr
            gr[pl.ds(t, 1), :] = nr
            gi[pl.ds(t, 1), :] = ni
            return nr, ni

        pr, pi = lax.fori_loop(0, tc, step, (car_r[...], car_i[...]))
        car_r[...] = pr
        car_i[...] = pi

        xrv, xiv = xr_ref[...], xi_ref[...]
        keep = jnp.where(i == nc - 1, 0.0, 1.0)
        row0 = lax.broadcasted_iota(jnp.int32, (tc, 1), 0) == 0
        xsr = jnp.where(row0, xpr_ref[7:8, :] * keep, pltpu.roll(xrv, 1, axis=0))
        xsi = jnp.where(row0, xpi_ref[7:8, :] * keep, pltpu.roll(xiv, 1, axis=0))
        grv, giv = gr[...], gi[...]
        dare_ref[...] += jnp.sum(grv * xsr + giv * xsi, axis=0, keepdims=True)
        daim_ref[...] += jnp.sum(giv * xsr - grv * xsi, axis=0, keepdims=True)
        dd_ref[...] += jnp.sum(dyv * u, axis=0, keepdims=True)

        dus = []
        for k in range(SSM_BLOCKS):
            sl = slice(512 * k, 512 * (k + 1))
            ch = slice(128 * k, 128 * (k + 1))
            grb, gib = grv[:, sl].astype(BF16), giv[:, sl].astype(BF16)
            dus.append(_dot(grb, btre_ref[k]) + _dot(gib, btim_ref[k]))
            dbre_ref[k] += _dot(grb, ub[:, ch], TN)
            dbim_ref[k] += _dot(gib, ub[:, ch], TN)
            dcre_ref[k] += _dot(dyb[:, ch], xrv[:, sl].astype(BF16), TN)
            dcim_ref[k] -= _dot(dyb[:, ch], xiv[:, sl].astype(BF16), TN)
        du_ref[...] = (jnp.concatenate(dus, axis=1) + d_ref[...] * dyv).astype(BF16)

    rev = lambda i: (nc - 1 - i, 0)
    prev = lambda i: (jnp.maximum((nc - 1 - i) * rb - 1, 0), 0)
    return _pc(
        body, job, name=name, grid=(nc,),
        in_specs=[pl.BlockSpec((tc, D_SSM), rev), pl.BlockSpec((tc, D_SSM), lambda i: (nc - 1 - i, ucol)),
                  pl.BlockSpec((tc, N_STATE), rev), pl.BlockSpec((tc, N_STATE), rev),
                  pl.BlockSpec((8, N_STATE), prev), pl.BlockSpec((8, N_STATE), prev),
                  _full((SSM_BLOCKS, 128, 512)), _full((SSM_BLOCKS, 128, 512)),
                  _full((SSM_BLOCKS, 512, 128)), _full((SSM_BLOCKS, 512, 128)),
                  _full((1, N_STATE)), _full((1, N_STATE)), _full((1, D_SSM))],
        out_specs=[pl.BlockSpec((tc, D_SSM), rev),
                   _full((SSM_BLOCKS, 512, 128)), _full((SSM_BLOCKS, 512, 128)),
                   _full((SSM_BLOCKS, 128, 512)), _full((SSM_BLOCKS, 128, 512)),
                   _full((1, N_STATE)), _full((1, N_STATE)), _full((1, D_SSM))],
        out_shape=[jax.ShapeDtypeStruct((T, D_SSM), BF16),
                   jax.ShapeDtypeStruct((SSM_BLOCKS, 512, 128), F32), jax.ShapeDtypeStruct((SSM_BLOCKS, 512, 128), F32),
                   jax.ShapeDtypeStruct((SSM_BLOCKS, 128, 512), F32), jax.ShapeDtypeStruct((SSM_BLOCKS, 128, 512), F32),
                   jax.ShapeDtypeStruct((1, N_STATE), F32), jax.ShapeDtypeStruct((1, N_STATE), F32),
                   jax.ShapeDtypeStruct((1, D_SSM), F32)],
        scratch_shapes=[pltpu.VMEM((tc, N_STATE), F32), pltpu.VMEM((tc, N_STATE), F32),
                        pltpu.VMEM((1, N_STATE), F32), pltpu.VMEM((1, N_STATE), F32)], sem=("arbitrary",),
        operands=(dy, proj, xr, xi, xr, xi, ctre, ctimn, btre, btim, are, aim, d))


def _rel_bucket(dist):
    n = jnp.maximum(dist, 0)
    max_exact = NUM_BUCKETS // 2
    n_f = jnp.maximum(n, 1).astype(F32)
    large = max_exact + (jnp.log(n_f / max_exact) / math.log(REL_MAX_DISTANCE / max_exact)
                         * (NUM_BUCKETS - max_exact)).astype(jnp.int32)
    large = jnp.minimum(large, NUM_BUCKETS - 1)
    return jnp.where(n < max_exact, n, large)


def _bucket_tables():
    qi = jnp.arange(ATTN_BLOCK)[:, None]
    kj = jnp.arange(2 * ATTN_BLOCK)[None, :]
    delta = jnp.maximum(ATTN_BLOCK + qi - kj, 0)
    return jnp.stack([_rel_bucket(delta * r) for r in ATTN_DILATIONS]).astype(jnp.int32)


def _bias_tables(rel_bias, buckets, name):
    def body(tab_ref, bk_ref, o_ref):
        g = pl.program_id(0)
        bk = bk_ref[...]
        qi = lax.broadcasted_iota(jnp.int32, bk.shape, 0)
        kj = lax.broadcasted_iota(jnp.int32, bk.shape, 1)
        delta = ATTN_BLOCK + qi - kj
        band = (delta >= 0) & (delta <= ATTN_SPAN)
        accs = [jnp.zeros(bk.shape, F32) for _ in range(4)]
        for b in range(NUM_BUCKETS):
            hit = bk == b
            for h in range(4):
                accs[h] = jnp.where(hit, tab_ref[b, 4 * g + h], accs[h])
        for h in range(4):
            o_ref[h] = jnp.where(band, accs[h], NEG_INF)

    return pl.pallas_call(
        body, name=name, grid=(3,),
        in_specs=[pl.BlockSpec(memory_space=pltpu.SMEM),
                  pl.BlockSpec((None, ATTN_BLOCK, 2 * ATTN_BLOCK), lambda g: (g, 0, 0))],
        out_specs=pl.BlockSpec((None, 4, ATTN_BLOCK, 2 * ATTN_BLOCK), lambda g: (g, 0, 0, 0)),
        out_shape=jax.ShapeDtypeStruct((3, 4, ATTN_BLOCK, 2 * ATTN_BLOCK), F32),
        compiler_params=_params(("parallel",)),
    )(rel_bias, buckets)


def _bias_grad(db0, db1, buckets, name):
    def body(a_ref, b_ref, bk_ref, o_ref):
        bk = bk_ref[...]
        for h in range(4):
            dv = a_ref[h] + b_ref[h]
            for b in range(NUM_BUCKETS):
                o_ref[h, b:b + 1, :] = jnp.sum(jnp.where(bk == b, dv, 0.0), axis=0, keepdims=True)

    tab = pl.BlockSpec((None, 4, ATTN_BLOCK, 2 * ATTN_BLOCK), lambda g: (g, 0, 0, 0))
    return pl.pallas_call(
        body, name=name, grid=(3,),
        in_specs=[tab, tab, pl.BlockSpec((None, ATTN_BLOCK, 2 * ATTN_BLOCK), lambda g: (g, 0, 0))],
        out_specs=pl.BlockSpec((None, 4, NUM_BUCKETS, 2 * ATTN_BLOCK), lambda g: (g, 0, 0, 0)),
        out_shape=jax.ShapeDtypeStruct((3, 4, NUM_BUCKETS, 2 * ATTN_BLOCK), F32),
        compiler_params=_params(("parallel",)),
    )(db0, db1, buckets)


_ATTN_SUB = {1: 4, 4: 1, 16: 1}
_UNROLL = 4


def _unit_rows(j, s, r):
    start = j * ATTN_BLOCK * r + s
    return pl.ds(start, ATTN_BLOCK, stride=r) if r > 1 else pl.ds(start, ATTN_BLOCK)


def _for_units(r, nsub, fn, after):
    if r * nsub <= _UNROLL:
        after([fn(j, s) for j in range(nsub) for s in range(r)])
    else:
        def four(i, c):
            after([fn(0, _UNROLL * i + k) for k in range(_UNROLL)])
            return c

        lax.fori_loop(0, r // _UNROLL, four, 0)


def _attn_cols(g):
    return tuple((_OFF[n] + ATTN_GROUP_WIDTH * g) // LANES for n in ("q", "k", "v"))


def _attn_fwd(proj, bias, g, name, job=None):
    r = ATTN_DILATIONS[g]
    nsub = _ATTN_SUB[r]
    T = proj.shape[0]
    sub = ATTN_BLOCK * r
    tb = sub * nsub
    qc, kc, vc = _attn_cols(g)
    scale = ATTN_HEAD_DIM ** -0.5

    def body(q_ref, kc_ref, kp_ref, vc_ref, vp_ref, bias_ref, o_ref, lse_ref):
        lane = lax.broadcasted_iota(jnp.int32, (ATTN_BLOCK, LANES), 1)
        kj = lax.broadcasted_iota(jnp.int32, (ATTN_BLOCK, 2 * ATTN_BLOCK), 1)
        dead = (pl.program_id(0) == 0) & (kj < ATTN_BLOCK)

        def one(j, s):
            rows = _unit_rows(j, s, r)
            before = _unit_rows(max(j - 1, 0), s, r)
            k_before = kc_ref[before, :] if j else kp_ref[before, :]
            v_before = vc_ref[before, :] if j else vp_ref[before, :]
            q = q_ref[rows, :]
            kcat = jnp.concatenate([k_before, kc_ref[rows, :]], axis=0).astype(BF16)
            vcat = jnp.concatenate([v_before, vc_ref[rows, :]], axis=0).astype(BF16)
            o_acc = jnp.zeros((ATTN_BLOCK, LANES), F32)
            l_acc = jnp.zeros((ATTN_BLOCK, LANES), F32)
            for hh in range(2):
                mine = (lane >= ATTN_HEAD_DIM) if hh else (lane < ATTN_HEAD_DIM)
                qm = jnp.where(mine, q, 0.0).astype(BF16)
                sc = _dot(qm, kcat, NT) * scale + bias_ref[hh]
                if j == 0:
                    sc = jnp.where(dead, NEG_INF, sc)
                m = jnp.max(sc, axis=-1, keepdims=True)
                p = jnp.exp(sc - m)
                l = jnp.sum(p, axis=-1, keepdims=True)
                o_acc = jnp.where(mine, _dot((p / l).astype(BF16), vcat), o_acc)
                l_acc = jnp.where(mine, m + jnp.log(l), l_acc)
            o_ref[rows, :] = o_acc
            lse_ref[rows, :] = l_acc

        _for_units(r, nsub, one, lambda results: None)

    cur = lambda c: pl.BlockSpec((tb, LANES), lambda b, p: (b, c + p))
    prev = lambda c: pl.BlockSpec((sub, LANES), lambda b, p: (jnp.maximum(b * nsub - 1, 0), c + p))
    out = pl.BlockSpec((tb, LANES), lambda b, p: (b, p))
    return _pc(
        body, job, name=name, grid=(T // tb, 2),
        in_specs=[cur(qc), cur(kc), prev(kc), cur(vc), prev(vc),
                  pl.BlockSpec((2, ATTN_BLOCK, 2 * ATTN_BLOCK), lambda b, p: (p, 0, 0))],
        out_specs=[out, out],
        out_shape=[jax.ShapeDtypeStruct((T, ATTN_GROUP_WIDTH), F32), jax.ShapeDtypeStruct((T, ATTN_GROUP_WIDTH), F32)],
        scratch_shapes=[], sem=("parallel", "parallel"), operands=(proj, proj, proj, proj, proj, bias))


def _attn_bwd(proj, do, corr, lse, bias, g, name):
    r = ATTN_DILATIONS[g]
    nsub = _ATTN_SUB[r]
    T = proj.shape[0]
    sub = ATTN_BLOCK * r
    tb = sub * nsub
    nb = T // tb
    qc, kc, vc = _attn_cols(g)
    dc = ATTN_GROUP_WIDTH * g // LANES
    scale = ATTN_HEAD_DIM ** -0.5

    def body(q_ref, kc_ref, kp_ref, vc_ref, vp_ref, do_ref, corr_ref, lse_ref, bias_ref,
             dq_ref, dk_ref, dv_ref, db_ref, dq_s, dkc_s, dkp_s, dvc_s, dvp_s, kacc, vacc):
        b = pl.program_id(1)

        @pl.when(b == 0)
        def _():
            db_ref[...] = jnp.zeros_like(db_ref)
            kacc[...] = jnp.zeros_like(kacc)
            vacc[...] = jnp.zeros_like(vacc)

        @pl.when(b == nb)
        def _():
            dk_ref[...] = kacc[...].astype(BF16)
            dv_ref[...] = vacc[...].astype(BF16)

        @pl.when(b < nb)
        def _():
            lane = lax.broadcasted_iota(jnp.int32, (ATTN_BLOCK, LANES), 1)
            kj = lax.broadcasted_iota(jnp.int32, (ATTN_BLOCK, 2 * ATTN_BLOCK), 1)
            dead = (b == 0) & (kj < ATTN_BLOCK)

            def one(j, s):
                rows = _unit_rows(j, s, r)
                before = _unit_rows(max(j - 1, 0), s, r)
                k_before = kc_ref[before, :] if j else kp_ref[before, :]
                v_before = vc_ref[before, :] if j else vp_ref[before, :]
                q = q_ref[rows, :]
                kcat = jnp.concatenate([k_before, kc_ref[rows, :]], axis=0).astype(BF16)
                vcat = jnp.concatenate([v_before, vc_ref[rows, :]], axis=0).astype(BF16)
                dov, corrv, lsev = do_ref[rows, :], corr_ref[rows, :], lse_ref[rows, :]
                dq_acc = jnp.zeros((ATTN_BLOCK, LANES), F32)
                dk_acc = jnp.zeros((2 * ATTN_BLOCK, LANES), F32)
                dv_acc = jnp.zeros((2 * ATTN_BLOCK, LANES), F32)
                dss = []
                for hh in range(2):
                    mine = (lane >= ATTN_HEAD_DIM) if hh else (lane < ATTN_HEAD_DIM)
                    col = slice(ATTN_HEAD_DIM * hh, ATTN_HEAD_DIM * hh + 1)
                    qm = jnp.where(mine, q, 0.0).astype(BF16)
                    dom = jnp.where(mine, dov, 0.0).astype(BF16)
                    sc = _dot(qm, kcat, NT) * scale + bias_ref[hh]
                    if j == 0:
                        sc = jnp.where(dead, NEG_INF, sc)
                    p = jnp.exp(sc - lsev[:, col])
                    ds = p * (_dot(dom, vcat, NT) - corrv[:, col])
                    dss.append(ds)
                    dsb = ds.astype(BF16)
                    dq_acc = jnp.where(mine, _dot(dsb, kcat) * scale, dq_acc)
                    dk_acc += _dot(dsb, qm, TN) * scale
                    dv_acc += _dot(p.astype(BF16), dom, TN)
                dq_s[rows, :] = dq_acc
                dkp_s[rows, :] = dk_acc[:ATTN_BLOCK]
                dkc_s[rows, :] = dk_acc[ATTN_BLOCK:]
                dvp_s[rows, :] = dv_acc[:ATTN_BLOCK]
                dvc_s[rows, :] = dv_acc[ATTN_BLOCK:]
                return dss

            def add_bias_grads(results):
                for hh in range(2):
                    db_ref[hh] += functools.reduce(lambda x, y: x + y, [dss[hh] for dss in results])

            _for_units(r, nsub, one, add_bias_grads)
            dq_ref[...] = dq_s[...].astype(BF16)
            tail = slice((nsub - 1) * sub, nsub * sub)
            for acc, before_s, cur_s, out_ref in ((kacc, dkp_s, dkc_s, dk_ref), (vacc, dvp_s, dvc_s, dv_ref)):
                acc[tail, :] += before_s[0:sub, :]
                out_ref[...] = acc[...].astype(BF16)
                acc[...] = cur_s[...]
                for j in range(nsub - 1):
                    acc[j * sub:(j + 1) * sub, :] += before_s[(j + 1) * sub:(j + 2) * sub, :]

    last = nb - 1
    blk = (tb, LANES)
    cur = lambda c: pl.BlockSpec(blk, lambda p, b: (jnp.minimum(b, last), c + p))
    prev = lambda c: pl.BlockSpec(blk, lambda p, b: (jnp.clip(b - 1, 0, last), c + p))
    before = lambda c: pl.BlockSpec((sub, LANES), lambda p, b: (jnp.clip(b * nsub - 1, 0, nb * nsub - 1), c + p))
    tab = pl.BlockSpec((2, ATTN_BLOCK, 2 * ATTN_BLOCK), lambda p, b: (p, 0, 0))
    return pl.pallas_call(
        body, name=name, grid=(2, nb + 1),
        in_specs=[cur(qc), cur(kc), before(kc), cur(vc), before(vc), cur(dc), cur(dc), cur(0), tab],
        out_specs=[cur(0), prev(0), prev(0), tab],
        out_shape=[jax.ShapeDtypeStruct((T, ATTN_GROUP_WIDTH), BF16)] * 3
        + [jax.ShapeDtypeStruct((4, ATTN_BLOCK, 2 * ATTN_BLOCK), F32)],
        scratch_shapes=[pltpu.VMEM(blk, F32)] * 7,
        compiler_params=_params(("arbitrary", "arbitrary")),
    )(proj, proj, proj, proj, proj, do, corr, lse, bias)


def _mix_weights(lses):
    m = jnp.maximum(jnp.maximum(lses[0], lses[1]), lses[2])
    es = [jnp.exp(l - m) for l in lses]
    inv = 1.0 / (es[0] + es[1] + es[2])
    return jnp.concatenate([e * inv for e in es], axis=1)


def _attn_mix(os, lses, proj, tm, name):
    T = proj.shape[0]
    zcol = _OFF["z_attn"] // D_ATTN

    def body(o0, o1, o2, l0, l1, l2, z_ref, out_ref):
        z = z_ref[...]
        o = jnp.concatenate([o0[...], o1[...], o2[...]], axis=1)
        alpha = _mix_weights([l0[...], l1[...], l2[...]])
        out_ref[...] = (o * alpha * (z * _sigmoid(z))).astype(BF16)

    row = lambda i: (i, 0)
    grp = pl.BlockSpec((tm, ATTN_GROUP_WIDTH), row)
    return pl.pallas_call(
        body, name=name, grid=(T // tm,),
        in_specs=[grp] * 6 + [pl.BlockSpec((tm, D_ATTN), lambda i: (i, zcol))],
        out_specs=pl.BlockSpec((tm, D_ATTN), row),
        out_shape=jax.ShapeDtypeStruct((T, D_ATTN), BF16),
        compiler_params=_params(("parallel",)),
    )(*os, *lses, proj)


def _attn_mix_bwd(d, os, lses, proj, tm, name):
    T = proj.shape[0]
    zcol = _OFF["z_attn"] // D_ATTN

    def body(d_ref, o0, o1, o2, l0, l1, l2, z_ref, do_ref, corr_ref, dz_ref):
        dv, z = d_ref[...], z_ref[...]
        ov = jnp.concatenate([o0[...], o1[...], o2[...]], axis=1)
        alpha = _mix_weights([l0[...], l1[...], l2[...]])
        sz = _sigmoid(z)
        oc = ov * alpha
        dz_ref[...] = (dv * oc * (sz * (1.0 + z * (1.0 - sz)))).astype(BF16)
        doc = dv * (z * sz)
        do_ref[...] = doc * alpha
        pr = doc * oc
        p3 = pr[:, 0:256] + pr[:, 256:512] + pr[:, 512:768]
        li = lax.broadcasted_iota(jnp.int32, (256, 256), 0) // ATTN_HEAD_DIM
        lj = lax.broadcasted_iota(jnp.int32, (256, 256), 1) // ATTN_HEAD_DIM
        ones = jnp.where(li == lj, 1.0, 0.0).astype(F32)
        s = lax.dot_general(p3, ones, NN, precision=lax.Precision.HIGHEST, preferred_element_type=F32)
        corr_ref[...] = alpha * jnp.concatenate([s, s, s], axis=1)

    row = lambda i: (i, 0)
    grp = pl.BlockSpec((tm, ATTN_GROUP_WIDTH), row)
    return pl.pallas_call(
        body, name=name, grid=(T // tm,),
        in_specs=[pl.BlockSpec((tm, D_ATTN), row)] + [grp] * 6 + [pl.BlockSpec((tm, D_ATTN), lambda i: (i, zcol))],
        out_specs=[pl.BlockSpec((tm, D_ATTN), row)] * 3,
        out_shape=[jax.ShapeDtypeStruct((T, D_ATTN), F32), jax.ShapeDtypeStruct((T, D_ATTN), F32),
                   jax.ShapeDtypeStruct((T, D_ATTN), BF16)],
        compiler_params=_params(("parallel",)),
    )(d, *os, *lses, proj)


def _mem_probs(q_ref, kv_ref, h):
    hs = slice(MEM_HEAD_DIM * h, MEM_HEAD_DIM * (h + 1))
    qh = q_ref[:, hs].astype(BF16)
    kh = kv_ref[:, hs]
    vh = kv_ref[:, D_MEM + MEM_HEAD_DIM * h:D_MEM + MEM_HEAD_DIM * (h + 1)]
    s = _dot(qh, kh, NT) * (MEM_HEAD_DIM ** -0.5)
    p = jnp.exp(s - jnp.max(s, axis=-1, keepdims=True))
    pn = p / jnp.sum(p, axis=-1, keepdims=True)
    return qh, kh, vh, pn


def _mem_fwd(proj, kv, tm, name):
    T = proj.shape[0]
    M = kv.shape[0]
    qcol, zcol = _OFF["q_mem"] // D_MEM, _OFF["z_mem"] // D_MEM

    def body(q_ref, z_ref, kv_ref, o_ref):
        outs = []
        for h in range(MEM_HEADS):
            _, _, vh, pn = _mem_probs(q_ref, kv_ref, h)
            outs.append(_dot(pn.astype(BF16), vh))
        z = z_ref[...]
        o_ref[...] = (jnp.concatenate(outs, axis=1) * (z * _sigmoid(z))).astype(BF16)

    return pl.pallas_call(
        body, name=name, grid=(T // tm,),
        in_specs=[pl.BlockSpec((tm, D_MEM), lambda i: (i, qcol)), pl.BlockSpec((tm, D_MEM), lambda i: (i, zcol)),
                  _full((M, 2 * D_MEM))],
        out_specs=pl.BlockSpec((tm, D_MEM), lambda i: (i, 0)),
        out_shape=jax.ShapeDtypeStruct((T, D_MEM), BF16),
        compiler_params=_params(("parallel",)),
    )(proj, proj, kv)


def _mem_bwd(d, proj, kv, tm, name):
    T = proj.shape[0]
    M = kv.shape[0]
    qcol, zcol = _OFF["q_mem"] // D_MEM, _OFF["z_mem"] // D_MEM

    def body(d_ref, q_ref, z_ref, kv_ref, dq_ref, dz_ref, dkv_ref):
        @pl.when(pl.program_id(0) == 0)
        def _():
            dkv_ref[...] = jnp.zeros_like(dkv_ref)

        z = z_ref[...]
        sz = _sigmoid(z)
        dv = d_ref[...]
        dov = dv * (z * sz)
        scale = MEM_HEAD_DIM ** -0.5
        outs, dqs = [], []
        for h in range(MEM_HEADS):
            hs = slice(MEM_HEAD_DIM * h, MEM_HEAD_DIM * (h + 1))
            qh, kh, vh, pn = _mem_probs(q_ref, kv_ref, h)
            pnb = pn.astype(BF16)
            oh = _dot(pnb, vh)
            outs.append(oh)
            doh = dov[:, hs]
            dohb = doh.astype(BF16)
            dp = _dot(dohb, vh, NT)
            ds = pn * (dp - jnp.sum(doh * oh, axis=-1, keepdims=True))
            dsb = ds.astype(BF16)
            dqs.append(_dot(dsb, kh) * scale)
            dkv_ref[:, hs] += _dot(dsb, qh, TN) * scale
            vs = slice(D_MEM + MEM_HEAD_DIM * h, D_MEM + MEM_HEAD_DIM * (h + 1))
            dkv_ref[:, vs] += _dot(pnb, dohb, TN)
        dq_ref[...] = jnp.concatenate(dqs, axis=1).astype(BF16)
        dz_ref[...] = (dv * jnp.concatenate(outs, axis=1) * (sz * (1.0 + z * (1.0 - sz)))).astype(BF16)

    row = lambda i: (i, 0)
    return pl.pallas_call(
        body, name=name, grid=(T // tm,),
        in_specs=[pl.BlockSpec((tm, D_MEM), row), pl.BlockSpec((tm, D_MEM), lambda i: (i, qcol)),
                  pl.BlockSpec((tm, D_MEM), lambda i: (i, zcol)), _full((M, 2 * D_MEM))],
        out_specs=[pl.BlockSpec((tm, D_MEM), row), pl.BlockSpec((tm, D_MEM), row), _full((M, 2 * D_MEM))],
        out_shape=[jax.ShapeDtypeStruct((T, D_MEM), BF16), jax.ShapeDtypeStruct((T, D_MEM), BF16),
                   jax.ShapeDtypeStruct((M, 2 * D_MEM), F32)],
        compiler_params=_params(("arbitrary",)),
    )(d, proj, proj, kv)


def _branches_and_gates(os_ref, oa_ref, om_ref, gl_refs, bg_ref, ws_ref, wa_ref, wm_ref):
    outs = (_dot(os_ref[...], ws_ref[...]), _dot(oa_ref[...], wa_ref[...]), _dot(om_ref[...], wm_ref[...]))
    gates = tuple(_sigmoid(jnp.concatenate([gl_refs[2 * k][...], gl_refs[2 * k + 1][...]], axis=1)
                           + bg_ref[:, D_MODEL * k:D_MODEL * (k + 1)]) for k in range(3))
    return outs, gates


def _merge_specs(tm):
    row = lambda i: (i, 0)
    first = _OFF["gates"] // GATE_BLOCK
    gate = [pl.BlockSpec((tm, GATE_BLOCK), (lambda i, k=k: (i, first + k))) for k in range(N_GATES // GATE_BLOCK)]
    return ([pl.BlockSpec((tm, D_SSM), row), pl.BlockSpec((tm, D_ATTN), row), pl.BlockSpec((tm, D_MEM), row)] + gate
            + [_full((1, N_GATES)), _full((D_SSM, D_MODEL)), _full((D_ATTN, D_MODEL)), _full((D_MEM, D_MODEL)),
               _full((D_MODEL, D_MODEL))])


def _merge_fwd(x, o_ssm, o_attn, o_mem, proj, bg, ws, wa, wm, wo, tm, name):
    T = x.shape[0]

    def body(os_ref, oa_ref, om_ref, g0, g1, g2, g3, g4, g5, bg_ref, ws_ref, wa_ref, wm_ref, wo_ref, x_ref,
             xo_ref, mg_ref):
        outs, gates = _branches_and_gates(os_ref, oa_ref, om_ref, (g0, g1, g2, g3, g4, g5), bg_ref, ws_ref, wa_ref,
                                          wm_ref)
        merged = (gates[0] * outs[0] + gates[1] * outs[1] + gates[2] * outs[2]).astype(BF16)
        mg_ref[...] = merged
        xo_ref[...] = x_ref[...] + _dot(merged, wo_ref[...])

    row = lambda i: (i, 0)
    return pl.pallas_call(
        body, name=name, grid=(T // tm,),
        in_specs=_merge_specs(tm) + [pl.BlockSpec((tm, D_MODEL), row)],
        out_specs=[pl.BlockSpec((tm, D_MODEL), row), pl.BlockSpec((tm, D_MODEL), row)],
        out_shape=[jax.ShapeDtypeStruct((T, D_MODEL), F32), jax.ShapeDtypeStruct((T, D_MODEL), BF16)],
        compiler_params=_params(("parallel",)),
    )(o_ssm, o_attn, o_mem, *([proj] * (N_GATES // GATE_BLOCK)), bg, ws, wa, wm, wo, x)


def _merge_bwd(dx, o_ssm, o_attn, o_mem, proj, bg, ws, wa, wm, wo, tm, name, job=None):
    T = dx.shape[0]

    def body(os_ref, oa_ref, om_ref, g0, g1, g2, g3, g4, g5, bg_ref, ws_ref, wa_ref, wm_ref, wo_ref, dx_ref,
             dgl_ref, db_ref, dos_ref, doa_ref, dom_ref, dbg_ref):
        @pl.when(pl.program_id(0) == 0)
        def _():
            dbg_ref[...] = jnp.zeros_like(dbg_ref)

        outs, gates = _branches_and_gates(os_ref, oa_ref, om_ref, (g0, g1, g2, g3, g4, g5), bg_ref, ws_ref, wa_ref,
                                          wm_ref)
        dm = _dot(dx_ref[...].astype(BF16), wo_ref[...], NT)
        w_refs = (ws_ref, wa_ref, wm_ref)
        do_refs = (dos_ref, doa_ref, dom_ref)
        for k in range(3):
            cols = slice(D_MODEL * k, D_MODEL * (k + 1))
            dgl = dm * outs[k] * (gates[k] * (1.0 - gates[k]))
            dgl_ref[:, cols] = dgl.astype(BF16)
            dbg_ref[:, cols] += jnp.sum(dgl, axis=0, keepdims=True)
            dbk = (dm * gates[k]).astype(BF16)
            db_ref[:, cols] = dbk
            do_refs[k][...] = _dot(dbk, w_refs[k][...], NT)

    row = lambda i: (i, 0)
    return _pc(
        body, job, name=name, grid=(T // tm,),
        in_specs=_merge_specs(tm) + [pl.BlockSpec((tm, D_MODEL), row)],
        out_specs=[pl.BlockSpec((tm, N_GATES), row), pl.BlockSpec((tm, N_GATES), row), pl.BlockSpec((tm, D_SSM), row),
                   pl.BlockSpec((tm, D_ATTN), row), pl.BlockSpec((tm, D_MEM), row), _full((1, N_GATES))],
        out_shape=[jax.ShapeDtypeStruct((T, N_GATES), BF16), jax.ShapeDtypeStruct((T, N_GATES), BF16),
                   jax.ShapeDtypeStruct((T, D_SSM), F32), jax.ShapeDtypeStruct((T, D_ATTN), F32),
                   jax.ShapeDtypeStruct((T, D_MEM), F32), jax.ShapeDtypeStruct((1, N_GATES), F32)],
        scratch_shapes=[], sem=("arbitrary",),
        operands=(o_ssm, o_attn, o_mem, *([proj] * (N_GATES // GATE_BLOCK)), bg, ws, wa, wm, wo, dx))


def _loss_head(x, g, target, tm, name):
    T, D = x.shape

    def body(x_ref, g_ref, t_ref, loss_ref, dx_ref, dg_ref):
        @pl.when(pl.program_id(0) == 0)
        def _():
            loss_ref[...] = jnp.zeros_like(loss_ref)
            dg_ref[...] = jnp.zeros_like(dg_ref)

        xv = x_ref[...]
        r = lax.rsqrt(jnp.mean(xv * xv, axis=-1, keepdims=True) + EPS)
        xr = xv * r
        err = xr * g_ref[...] - t_ref[...]
        loss_ref[...] += 0.5 * jnp.sum(jnp.mean(err * err, axis=-1, keepdims=True), axis=0, keepdims=True)
        dy = err * (1.0 / D)
        dg_ref[...] += jnp.sum(dy * xr, axis=0, keepdims=True)
        wv = dy * g_ref[...]
        dx_ref[...] = r * (wv - xr * jnp.mean(wv * xr, axis=-1, keepdims=True))

    row = lambda i: (i, 0)
    return pl.pallas_call(
        body, name=name, grid=(T // tm,),
        in_specs=[pl.BlockSpec((tm, D), row), _full((1, D)), pl.BlockSpec((tm, D), row)],
        out_specs=[_full((1, 128)), pl.BlockSpec((tm, D), row), _full((1, D))],
        out_shape=[jax.ShapeDtypeStruct((1, 128), F32), jax.ShapeDtypeStruct((T, D), F32),
                   jax.ShapeDtypeStruct((1, D), F32)],
        compiler_params=_params(("arbitrary",)),
    )(x, g, target)


def _adamw(parts, w, m, v, tr, name):
    L, R, C = w.shape

    def body(p_ref, w_ref, m_ref, v_ref, g_ref, d_ref, mo_ref, vo_ref):
        g = p_ref[0].astype(F32)
        for s in range(1, N_DEV):
            g = g + p_ref[s].astype(F32)
        mn = ADAM_B1 * m_ref[...] + (1.0 - ADAM_B1) * g
        vn = ADAM_B2 * v_ref[...] + (1.0 - ADAM_B2) * (g * g)
        m_hat = mn / (1.0 - ADAM_B1 ** ADAM_STEP)
        v_hat = vn / (1.0 - ADAM_B2 ** ADAM_STEP)
        g_ref[...] = g
        d_ref[...] = -ADAM_LR * (m_hat / (jnp.sqrt(v_hat) + ADAM_EPS) + ADAM_WD * w_ref[...])
        mo_ref[...] = mn
        vo_ref[...] = vn

    one = pl.BlockSpec((None, tr, C), lambda l, i: (l, i, 0))
    return pl.pallas_call(
        body, name=name, grid=(L, R // tr),
        in_specs=[pl.BlockSpec((N_DEV, None, tr, C), lambda l, i: (0, l, i, 0)), one, one, one],
        out_specs=[one] * 4,
        out_shape=[jax.ShapeDtypeStruct((L, R, C), F32)] * 4,
        compiler_params=_params(("parallel", "parallel")),
    )(parts, w, m, v)


_SHARDED = (("w_in", (1088, 1024), 1), ("w_glu", (96, 768), 0), ("w_mem_kv", (128, 1024), 0),
            ("w_br_ssm", (768, 128), 1), ("w_br_attn", (768, 128), 1), ("w_br_mem", (512, 128), 1),
            ("w_out", (128, 1024), 0))
_W_IN = 0
_SMALL = tuple(range(1, len(_SHARDED)))


class _Job(NamedTuple):
    ins: list
    out_shape: list
    aliases: dict
    pairs: Callable
    n: int


def _peers():
    x, y, c = lax.axis_index("x"), lax.axis_index("y"), lax.axis_index("c")
    me = 4 * x + 2 * y + c
    out = []
    for k in range(1, N_DEV):
        px = 1 - x if k & 4 else x
        py = 1 - y if k & 2 else y
        pc = 1 - c if k & 1 else c
        out.append(((px, py, pc), 4 * px + 2 * py + pc))
    return me, out


def _copies(pairs, send_sems, recv_sems, local_sems, arrivals):
    me, peers = _peers()
    local = [pltpu.make_async_copy(src(me), dst(me), local_sems.at[j]) for j, (src, dst) in enumerate(pairs)]
    sends, recvs = [], []
    for k, (peer, lin) in enumerate(peers):
        for j, (src, dst) in enumerate(pairs):
            for to, out in ((dst(me), sends), (dst(lin), recvs)):
                if out is sends or arrivals:
                    out.append(pltpu.make_async_remote_copy(
                        src_ref=src(lin), dst_ref=to, send_sem=send_sems.at[j, k], recv_sem=recv_sems.at[j, k],
                        device_id=peer, device_id_type=pl.DeviceIdType.MESH))
    return local, sends, recvs


def _start_copies(pairs, *sems):
    local, sends, _ = _copies(pairs, *sems, arrivals=False)
    for cp in local + sends:
        cp.start()


def _wait_copies(pairs, *sems):
    local, sends, recvs = _copies(pairs, *sems, arrivals=True)
    for cp in recvs:
        cp.wait_recv()
    for cp in sends:
        cp.wait_send()
    for cp in local:
        cp.wait()


def _job_scratch(job):
    return [pltpu.SemaphoreType.DMA((job.n, N_DEV - 1)), pltpu.SemaphoreType.DMA((job.n, N_DEV - 1)),
            pltpu.SemaphoreType.DMA((job.n,))]


def _pc(body, job, *, name, grid, in_specs, out_specs, out_shape, scratch_shapes, sem, operands):
    if job is None:
        return pl.pallas_call(body, name=name, grid=grid, in_specs=in_specs, out_specs=out_specs, out_shape=out_shape,
                              scratch_shapes=scratch_shapes, compiler_params=_params(sem))(*operands)
    a = len(in_specs)
    b = a + len(job.ins)
    c = b + len(out_shape)
    d = c + len(job.out_shape)
    e = d + len(scratch_shapes)

    def carried(*refs):
        pairs = job.pairs(refs[a:b], refs[c:d])
        ids = [pl.program_id(k) for k in range(len(grid))]
        first = functools.reduce(jnp.logical_and, [i == 0 for i in ids])
        last = functools.reduce(jnp.logical_and, [i == n - 1 for i, n in zip(ids, grid)])

        @pl.when(first)
        def _():
            _start_copies(pairs, *refs[e:])

        body(*refs[:a], *refs[b:c], *refs[d:e])

        @pl.when(last)
        def _():
            _wait_copies(pairs, *refs[e:])

    hbm = pl.BlockSpec(memory_space=pl.ANY)
    outs = pl.pallas_call(
        carried, name=name, grid=grid,
        in_specs=list(in_specs) + [hbm] * len(job.ins), out_specs=list(out_specs) + [hbm] * len(job.out_shape),
        out_shape=list(out_shape) + list(job.out_shape),
        input_output_aliases={a + i: len(out_shape) + o for i, o in job.aliases.items()},
        scratch_shapes=list(scratch_shapes) + _job_scratch(job),
        compiler_params=_params(("arbitrary",) * len(grid)),
    )(*operands, *job.ins)
    return outs[:len(out_shape)], outs[len(out_shape):]


def _gather_via_sibling(x, take, place, out_shape, name, landing=None):
    def body(*refs):
        x_ref, o_ref = refs[0], refs[-4]
        send_sems, recv_sems, local_sem = refs[-3:]
        x, y, c = lax.axis_index("x"), lax.axis_index("y"), lax.axis_index("c")
        me, sibling = (x, y, c), (x, y, 1 - c)
        chips = [(1 - x, y), (x, 1 - y), (1 - x, 1 - y)]
        src = take(x_ref)

        def slot(px, py, pc):
            return place(o_ref, 4 * px + 2 * py + pc)

        def copy(k, block, to, first_hand):
            return pltpu.make_async_remote_copy(
                src_ref=src if first_hand else slot(*block), dst_ref=slot(*block), send_sem=send_sems.at[k],
                recv_sem=recv_sems.at[k], device_id=to, device_id_type=pl.DeviceIdType.MESH)

        mine = pltpu.make_async_copy(src, slot(*me), local_sem)
        mine.start()
        first = [copy(0, me, sibling, True)] + [copy(1 + j, me, (*chip, c), True) for j, chip in enumerate(chips)]
        for cp in first:
            cp.start()
        passed = []
        for j, chip in enumerate(chips):
            copy(1 + j, (*chip, c), me, True).wait_recv()
            passed.append(copy(4 + j, (*chip, c), sibling, False))
            passed[-1].start()
        copy(0, sibling, me, True).wait_recv()
        for j, chip in enumerate(chips):
            copy(4 + j, (*chip, 1 - c), me, False).wait_recv()
        for cp in first + passed:
            cp.wait_send()
        mine.wait()

    hbm = pl.BlockSpec(memory_space=pl.ANY)
    ins = [x] if landing is None else [x, landing]
    return pl.pallas_call(
        body, name=name, in_specs=[hbm] * len(ins), out_specs=hbm, out_shape=out_shape,
        input_output_aliases={} if landing is None else {1: 0},
        scratch_shapes=[pltpu.SemaphoreType.DMA((N_DEV - 1,)), pltpu.SemaphoreType.DMA((N_DEV - 1,)),
                        pltpu.SemaphoreType.DMA],
    )(*ins)


def _lane_window(ref, who):
    return ref.at[:, pl.ds(pl.multiple_of(who * LANES, LANES), LANES)]


def _gather_job(shards, items):
    out_shape = []
    for i, _ in items:
        _, s, axis = _SHARDED[i]
        whole = i != _W_IN and axis == 1
        out_shape.append(jax.ShapeDtypeStruct((s[0], N_DEV * s[1]) if whole else (N_DEV,) + s, BF16))

    def pairs(in_refs, out_refs):
        out = []
        for (i, l), src, dst in zip(items, in_refs, out_refs):
            if i != _W_IN and _SHARDED[i][2] == 1:
                out.append((lambda who, src=src, l=l: src.at[l], lambda who, dst=dst: _lane_window(dst, who)))
            else:
                out.append((lambda who, src=src, l=l: src.at[l], lambda who, dst=dst: dst.at[who]))
        return out

    return _Job([shards[i] for i, _ in items], out_shape, {}, pairs, len(items))


def _landed_weights(items, landed):
    out = {}
    for (i, _), a in zip(items, landed):
        n, s, axis = _SHARDED[i]
        if i == _W_IN:
            out[n] = a.reshape(D_IN, D_MODEL)
        elif axis == 0:
            out[n] = a.reshape(N_DEV * s[0], s[1])
        else:
            out[n] = a
    return out


def _scatter_job(grads, items, layer, parts=None):
    ng = len(grads)
    out_shape = [jax.ShapeDtypeStruct((N_DEV, DEPTH) + _SHARDED[i][1], BF16) for i in items]

    def pairs(in_refs, out_refs):
        out = []
        for i, src, dst in zip(items, in_refs[:ng], out_refs):
            _, s, axis = _SHARDED[i]
            if i == _W_IN:
                take = lambda who, src=src: src.at[who]
            elif axis == 0:
                take = lambda who, src=src, s=s: src.at[pl.ds(pl.multiple_of(who * s[0], 16), s[0])]
            else:
                take = lambda who, src=src: _lane_window(src, who)
            out.append((take, lambda who, dst=dst: dst.at[who, layer]))
        return out

    aliases = {} if parts is None else {ng + j: j for j in range(len(items))}
    return _Job(list(grads) + ([] if parts is None else list(parts)), out_shape, aliases, pairs, len(items))


def _rows_job(src, row0, landing=None):
    n = src.shape[0]
    pairs = lambda in_refs, out_refs: [(lambda who: in_refs[0], lambda who: out_refs[0].at[who, pl.ds(row0, n)])]
    return _Job([src] + ([] if landing is None else [landing]), [jax.ShapeDtypeStruct((N_DEV, _REP_ROWS, LANES), F32)],
                {} if landing is None else {1: 0}, pairs, 1)


_REPLICATED = (("norm_g", (2, 1024)), ("mem_norm_g", (2, 1024)), ("b_gate", (2, 3072)),
               ("ssm_lambda_re", (2, 48, 64)), ("ssm_lambda_im", (2, 48, 64)), ("ssm_log_dt", (2, 48)),
               ("ssm_b_re", (2, 48, 64, 16)), ("ssm_b_im", (2, 48, 64, 16)), ("ssm_c_re", (2, 48, 16, 64)),
               ("ssm_c_im", (2, 48, 16, 64)), ("ssm_d", (2, 768)), ("b_glu", (2, 768)), ("rel_bias", (32, 12)),
               ("final_norm_g", (1024,)))
_PER_LAYER = tuple((n, s[1:]) for n, s in _REPLICATED if s[0] == DEPTH and len(s) > 1)
_SHARED = tuple((n, s) for n, s in _REPLICATED if (n, s[1:]) not in _PER_LAYER)
_REP_HALF_ROWS = 1664
_REP_ROWS = 2 * _REP_HALF_ROWS
assert sum(int(np.prod(s)) for _, s in _PER_LAYER + _SHARED) <= _REP_HALF_ROWS * LANES


def _pack_half(tree, layer, shared):
    flat = [tree[n][layer].reshape(-1) for n, _ in _PER_LAYER]
    if shared:
        flat += [tree[n].reshape(-1) for n, _ in _SHARED]
    flat = jnp.concatenate(flat)
    return jnp.pad(flat, (0, _REP_HALF_ROWS * LANES - flat.shape[0])).reshape(_REP_HALF_ROWS, LANES)


def _pack_replicated(tree):
    return jnp.concatenate([_pack_half(tree, 1, False), _pack_half(tree, 0, True)])[None]


def _unpack_replicated(packed):
    halves = packed.reshape(2, -1)
    out, r = {}, 0
    for n, s in _PER_LAYER:
        size = int(np.prod(s))
        out[n] = jnp.stack([halves[1, r:r + size].reshape(s), halves[0, r:r + size].reshape(s)])
        r += size
    for n, s in _SHARED:
        size = int(np.prod(s))
        out[n] = halves[1, r:r + size].reshape(s)
        r += size
    return out


def _discretize(lam_re, lam_im, log_dt, b_re, b_im):
    dt = jnp.exp(log_dt)[:, None]
    mag = jnp.exp(lam_re * dt)
    abar_re, abar_im = mag * jnp.cos(lam_im * dt), mag * jnp.sin(lam_im * dt)
    den = lam_re * lam_re + lam_im * lam_im
    nr, ni = abar_re - 1.0, abar_im
    f_re = (nr * lam_re + ni * lam_im) / den
    f_im = (ni * lam_re - nr * lam_im) / den
    bbar_re = f_re[..., None] * b_re - f_im[..., None] * b_im
    bbar_im = f_re[..., None] * b_im + f_im[..., None] * b_re
    return abar_re, abar_im, bbar_re, bbar_im


def _block_diag(a):
    _, R, C = a.shape
    a = a.reshape(SSM_BLOCKS, 8, R, C)
    eye = jnp.eye(8, dtype=a.dtype)
    return (a[:, :, :, None, :] * eye[None, :, None, :, None]).reshape(SSM_BLOCKS, 8 * R, 8 * C)


def _diag_blocks(a, R, C):
    a = a.reshape(SSM_BLOCKS, 8, R, 8, C)
    eye = jnp.eye(8, dtype=a.dtype)
    return jnp.sum(a * eye[None, :, None, :, None], axis=3).reshape(SSM_GROUPS, R, C)


def _carried(result, job):
    return (result, None) if job is None else result


def _layer_fwd(x, mem, W, P, bias, layer, jobs):
    tag = f"l{layer}"
    abar_re, abar_im, bbar_re, bbar_im = _discretize(P["ssm_lambda_re"][layer], P["ssm_lambda_im"][layer],
                                                     P["ssm_log_dt"][layer], P["ssm_b_re"][layer], P["ssm_b_im"][layer])
    c_re, c_im = P["ssm_c_re"][layer], P["ssm_c_im"][layer]
    ssm = dict(
        are=abar_re.reshape(1, N_STATE), aim=abar_im.reshape(1, N_STATE),
        bre=_block_diag(bbar_re.transpose(0, 2, 1)).astype(BF16), bim=_block_diag(bbar_im.transpose(0, 2, 1)).astype(BF16),
        cre=_block_diag(c_re.transpose(0, 2, 1)).astype(BF16), cimn=_block_diag(-c_im.transpose(0, 2, 1)).astype(BF16),
        ctre=_block_diag(c_re).astype(BF16), ctimn=_block_diag(-c_im).astype(BF16),
        btre=_block_diag(bbar_re).astype(BF16), btim=_block_diag(bbar_im).astype(BF16),
        d=P["ssm_d"][layer].reshape(1, D_SSM))
    bglu = P["b_glu"][layer].reshape(1, D_SSM)
    bgate = P["b_gate"][layer].reshape(1, N_GATES)
    g = P["norm_g"][layer].reshape(1, D_MODEL)
    gm = P["mem_norm_g"][layer].reshape(1, D_MODEL)
    delivered = {}

    def carry(stage):
        return jobs[stage][0] if stage in jobs else None

    def deliver(stage, landed):
        if landed is not None:
            delivered[stage] = _landed_weights(jobs[stage][1], landed)

    T = x.shape[0]
    (proj, h), landed = _carried(_norm_proj(x, g, W["w_in"], min(T, 1024), 2176, f"{tag}_proj", job=carry("proj"),
                                            w_turned=True), carry("proj"))
    deliver("proj", landed)
    W = {**W, **delivered.get("proj", {})}
    (xr, xi, y, o_ssm), landed = _carried(
        _ssm_fwd(proj, ssm["bre"], ssm["bim"], ssm["cre"], ssm["cimn"], ssm["are"], ssm["aim"], ssm["d"], W["w_glu"],
                 bglu, 256, f"{tag}_ssm", job=carry("ssm")), carry("ssm"))
    deliver("ssm", landed)
    os, lses = [], []
    for grp in range(3):
        stage = f"attn{grp}"
        (o_g, lse_g), landed = _carried(_attn_fwd(proj, bias[grp], grp, f"{tag}_{stage}", job=carry(stage)), carry(stage))
        deliver(stage, landed)
        os.append(o_g)
        lses.append(lse_g)
    o_attn = _attn_mix(os, lses, proj, 512, f"{tag}_attn_mix")
    kvb, hm = _norm_proj(mem, gm, W["w_mem_kv"], mem.shape[0], 1024, f"{tag}_mem_kv", out_dtype=BF16)
    o_mem = _mem_fwd(proj, kvb, 512, f"{tag}_mem")
    x_out, merged = _merge_fwd(x, o_ssm, o_attn, o_mem, proj, bgate, W["w_br_ssm"], W["w_br_attn"], W["w_br_mem"],
                               W["w_out"], 256, f"{tag}_merge")
    res = dict(x=x, mem=mem, proj=proj, h=h, xr=xr, xi=xi, y=y, o_ssm=o_ssm, os=os, lses=lses,
               o_attn=o_attn, kvb=kvb, hm=hm, o_mem=o_mem, merged=merged, ssm=ssm, bglu=bglu,
               bgate=bgate, g=g, gm=gm, W=W)
    return x_out, res, delivered


def _layer_bwd(dx, res, P, bias, layer, jobs):
    tag = f"l{layer}b"
    proj, ssm, W = res["proj"], res["ssm"], res["W"]
    T = dx.shape[0]
    landed = {}

    def run(stage, fn, job):
        out, landed[stage] = _carried(fn(job), job)
        if job is None:
            del landed[stage]
        return out

    dgl, dbr, do_ssm, do_attn, do_mem, dbg = run(
        "merge", lambda job: _merge_bwd(dx, res["o_ssm"], res["o_attn"], res["o_mem"], proj, res["bgate"], W["w_br_ssm"],
                                        W["w_br_attn"], W["w_br_mem"], W["w_out"], 256, f"{tag}_merge", job=job),
        jobs.get("merge"))
    gw = {}
    gw["w_out"] = _mm_tn(res["merged"], dx, 1024, 1024, 512, f"{tag}_dw_out")
    gw["w_br_ssm"] = _mm_tn(res["o_ssm"], dbr, 768, 1024, 512, f"{tag}_dw_br_ssm", b_col=0, n=1024)
    gw["w_br_attn"] = _mm_tn(res["o_attn"], dbr, 768, 1024, 512, f"{tag}_dw_br_attn", b_col=1024, n=1024)
    gw["w_br_mem"] = _mm_tn(res["o_mem"], dbr, 512, 1024, 512, f"{tag}_dw_br_mem", b_col=2048, n=1024)

    dqm, dzm, dkv = _mem_bwd(do_mem, proj, res["kvb"], 512, f"{tag}_mem")
    M = dkv.shape[0]
    gw["w_mem_kv"] = _mm_tn(res["hm"], dkv, 1024, 1024, M, f"{tag}_dw_mem_kv")
    _, dgm = _proj_bwd(dkv.astype(BF16), W["w_mem_kv"], res["mem"], res["gm"], jnp.zeros_like(res["mem"]), M, 1024,
                       f"{tag}_mem_norm")

    do_g, corr, dza = _attn_mix_bwd(do_attn, res["os"], res["lses"], proj, 512, f"{tag}_attn_mix")
    dqs, dks, dvs, dbs = [], [], [], []
    for grp in range(3):
        dq_g, dk_g, dv_g, db_g = _attn_bwd(proj, do_g, corr, res["lses"][grp], bias[grp], grp, f"{tag}_attn{grp}")
        dqs.append(dq_g)
        dks.append(dk_g)
        dvs.append(dv_g)
        dbs.append(db_g)
    dbias = jnp.stack(dbs)

    dy, dzs, gelu_b, dt_b, dbglu = _glu_bwd(do_ssm, res["y"], proj, W["w_glu"], res["bglu"], 512, f"{tag}_glu")
    gw["w_glu"] = _mm_tn(gelu_b, dt_b, 768, 768, 512, f"{tag}_dw_glu")
    du, dbre, dbim, dcre, dcim, dare, daim, dd = run(
        "ssm", lambda job: _ssm_bwd(dy, proj, res["xr"], res["xi"], ssm["ctre"], ssm["ctimn"], ssm["btre"], ssm["btim"],
                                    ssm["are"], ssm["aim"], ssm["d"], 256, f"{tag}_ssm", job=job), jobs.get("ssm"))
    _, disc_vjp = jax.vjp(_discretize, P["ssm_lambda_re"][layer], P["ssm_lambda_im"][layer], P["ssm_log_dt"][layer],
                          P["ssm_b_re"][layer], P["ssm_b_im"][layer])
    d_lre, d_lim, d_ldt, d_bre, d_bim = disc_vjp((dare.reshape(SSM_GROUPS, SSM_STATE), daim.reshape(SSM_GROUPS, SSM_STATE),
                                                  _diag_blocks(dbre, SSM_STATE, SSM_GROUP),
                                                  _diag_blocks(dbim, SSM_STATE, SSM_GROUP)))

    small = [gw[_SHARDED[i][0]] for i in _SMALL]
    dproj = jnp.concatenate([du, dzs] + dqs + dks + dvs + [dza, dqm, dzm, dgl], axis=1)
    dw_in = run("dw_in", lambda job: _mm_tn(dproj, res["h"], 2176, 1024, min(T, 1024), f"{tag}_dw_in", job=job),
                jobs["dw_in"](small) if "dw_in" in jobs else None)
    dw_in = dw_in.reshape((N_DEV,) + _SHARDED[_W_IN][1])
    dx_in, dg = run("proj", lambda job: _proj_bwd(dproj, W["w_in"], res["x"], res["g"], dx, min(T, 1024), 2176,
                                                  f"{tag}_proj", job=job, w_turned=True),
                    jobs["proj"](small, dw_in, landed) if "proj" in jobs else None)

    gp = dict(norm_g=dg[0], mem_norm_g=dgm[0], b_gate=dbg[0], ssm_lambda_re=d_lre, ssm_lambda_im=d_lim,
              ssm_log_dt=d_ldt, ssm_b_re=d_bre, ssm_b_im=d_bim,
              ssm_c_re=_diag_blocks(dcre, SSM_GROUP, SSM_STATE), ssm_c_im=_diag_blocks(dcim, SSM_GROUP, SSM_STATE),
              ssm_d=dd[0], b_glu=dbglu[0])
    return dx_in, dw_in, gp, dbias, landed


def _train_step(x, mem, target, shards, P):
    rest0 = [(i, 0) for i in _SMALL]
    rows1 = [(i, 1) for i in _SMALL if _SHARDED[i][2] == 0]
    cols1 = [(i, 1) for i in _SMALL if _SHARDED[i][2] == 1]
    first = [(_W_IN, 0)]
    w_in0 = _gather_via_sibling(shards[_W_IN], lambda ref: ref.at[0], lambda ref, s: ref.at[s],
                                jax.ShapeDtypeStruct((N_DEV,) + _SHARDED[_W_IN][1], BF16), "gather_w_in0")
    W0 = _landed_weights(first, [w_in0])
    buckets = _bucket_tables()
    bias = _bias_tables(P["rel_bias"], buckets, "bias_tables")
    jobs0 = {"proj": (_gather_job(shards, rest0), rest0), "ssm": (_gather_job(shards, [(_W_IN, 1)]), [(_W_IN, 1)]),
             "attn0": (_gather_job(shards, rows1), rows1), "attn1": (_gather_job(shards, cols1), cols1)}
    x, res0, delivered = _layer_fwd(x, mem, W0, P, bias, 0, jobs0)
    W1 = {**delivered["ssm"], **delivered["attn0"], **delivered["attn1"]}
    x, res1, _ = _layer_fwd(x, mem, W1, P, bias, 1, {})
    loss, dx, dgf = _loss_head(x, P["final_norm_g"].reshape(1, D_MODEL), target, 512, "loss_head")

    dx, dw_in1, gp1, dbias1, landed1 = _layer_bwd(
        dx, res1, P, bias, 1, {"proj": lambda small, dw_in, landed: _scatter_job(small, _SMALL, 1)})
    rep1 = _pack_half({n: a[None] for n, a in gp1.items()}, 0, False)
    dx, _, gp0, dbias0, landed0 = _layer_bwd(
        dx, res0, P, bias, 0,
        {"merge": _rows_job(rep1, 0), "ssm": _scatter_job([dw_in1], [_W_IN], 1),
         "dw_in": lambda small: _scatter_job(small, _SMALL, 0, parts=landed1["proj"]),
         "proj": lambda small, dw_in, landed: _scatter_job([dw_in], [_W_IN], 0, parts=landed["ssm"])})
    d_rel = _bias_grad(dbias0, dbias1, buckets, "bias_grad")
    gp0 = {n: a[None] for n, a in gp0.items()}
    gp0["rel_bias"] = jnp.sum(d_rel, axis=-1).transpose(2, 0, 1).reshape(NUM_BUCKETS, 12)
    gp0["final_norm_g"] = dgf[0]
    rep0 = _pack_half(gp0, 0, True)
    rparts = _gather_via_sibling(rep0, lambda ref: ref, lambda ref, s: ref.at[s, pl.ds(_REP_HALF_ROWS, _REP_HALF_ROWS)],
                                 jax.ShapeDtypeStruct((N_DEV, _REP_ROWS, LANES), F32), "gather_small_grads0",
                                 landing=landed0["merge"][0])
    return loss[0, 0], dx, list(landed0["proj"]) + list(landed0["dw_in"]), rparts


_WEIGHTS = ["norm_g", "mem_norm_g", "w_in", "b_gate", "ssm_lambda_re", "ssm_lambda_im", "ssm_log_dt", "ssm_b_re",
            "ssm_b_im", "ssm_c_re", "ssm_c_im", "ssm_d", "w_glu", "b_glu", "w_mem_kv", "w_br_ssm", "w_br_attn",
            "w_br_mem", "w_out", "rel_bias", "final_norm_g"]
_ADAM_ROWS = {"w_in": 136,"w_glu": 96, "w_mem_kv": 128, "w_br_ssm": 768, "w_br_attn": 768, "w_br_mem": 512,
              "w_out": 128}


def kernel(x, mem, norm_g, mem_norm_g, w_in, b_gate, ssm_lambda_re, ssm_lambda_im, ssm_log_dt, ssm_b_re, ssm_b_im, ssm_c_re, ssm_c_im, ssm_d, w_glu, b_glu, w_mem_kv, w_br_ssm, w_br_attn, w_br_mem, w_out, rel_bias, final_norm_g, loss_target, m_norm_g, m_mem_norm_g, m_w_in, m_b_gate, m_ssm_lambda_re, m_ssm_lambda_im, m_ssm_log_dt, m_ssm_b_re, m_ssm_b_im, m_ssm_c_re, m_ssm_c_im, m_ssm_d, m_w_glu, m_b_glu, m_w_mem_kv, m_w_br_ssm, m_w_br_attn, m_w_br_mem, m_w_out, m_rel_bias, m_final_norm_g, v_norm_g, v_mem_norm_g, v_w_in, v_b_gate, v_ssm_lambda_re, v_ssm_lambda_im, v_ssm_log_dt, v_ssm_b_re, v_ssm_b_im, v_ssm_c_re, v_ssm_c_im, v_ssm_d, v_w_glu, v_b_glu, v_w_mem_kv, v_w_br_ssm, v_w_br_attn, v_w_br_mem, v_w_out, v_rel_bias, v_final_norm_g):
    given = dict(locals())
    w = {n: given[n] for n in _WEIGHTS}
    m = {n: given["m_" + n] for n in _WEIGHTS}
    v = {n: given["v_" + n] for n in _WEIGHTS}

    turned = lambda n, a: a.swapaxes(1, 2) if n == "w_in" else a
    shards = [turned(n, w[n]).astype(BF16) for n, _, _ in _SHARDED]
    loss, dx, parts, rparts = _train_step(x[0], mem[0], loss_target[0], shards, w)
    loss = lax.psum(loss, ("x", "y", "c"))

    new = {}
    for (n, _, _), p in zip(_SHARDED, parts):
        new[n] = [turned(n, a) for a in _adamw(p, turned(n, w[n]), turned(n, m[n]), turned(n, v[n]), _ADAM_ROWS[n],
                                               f"adamw_{n}")]
    rp = [_unpack_replicated(a) for a in _adamw(rparts[:, None], _pack_replicated(w), _pack_replicated(m),
                                                _pack_replicated(v), _REP_ROWS // 4, "adamw_replicated")]
    for n, _ in _REPLICATED:
        new[n] = [rp[kind][n] for kind in range(4)]
    outs = [loss, dx[None]]
    for kind in range(4):
        outs.extend(new[n][kind] for n in _WEIGHTS)
    return tuple(outs)
```

```python
import functools
import math
from typing import Callable, NamedTuple

import jax
import jax.numpy as jnp
import numpy as np
from jax import lax
from jax.experimental import pallas as pl
from jax.experimental.pallas import tpu as pltpu

F32 = jnp.float32
BF16 = jnp.bfloat16

D_MODEL = 1024
DEPTH = 2
EPS = 1e-6
D_SSM = 768
SSM_GROUP = 16
SSM_GROUPS = 48
SSM_STATE = 64
N_STATE = SSM_GROUPS * SSM_STATE
SSM_BLOCKS = 6
D_ATTN = 768
ATTN_HEAD_DIM = 64
ATTN_GROUP_WIDTH = 256
ATTN_DILATIONS = (1, 4, 16)
ATTN_SPAN = 128
ATTN_BLOCK = 128
NUM_BUCKETS = 32
REL_MAX_DISTANCE = 2048
NEG_INF = -1e30
MEM_HEADS = 4
MEM_HEAD_DIM = 128
D_MEM = 512
N_GATES = 3 * D_MODEL
D_IN = 8704
N_DEV = 8
LANES = 128
ADAM_LR = 0.001
ADAM_B1 = 0.9
ADAM_B2 = 0.999
ADAM_EPS = 1e-08
ADAM_WD = 0.01
ADAM_STEP = 10

_OFF = {"u": 0, "z_ssm": 768, "q": 1536, "k": 2304, "v": 3072, "z_attn": 3840, "q_mem": 4608, "z_mem": 5120,
        "gates": 5632}
GATE_BLOCK = 512

NN = (((1,), (0,)), ((), ()))
NT = (((1,), (1,)), ((), ()))
TN = (((0,), (0,)), ((), ()))

VMEM_LIMIT = 56 * 1024 * 1024


def _dot(a, b, dims=NN):
    return lax.dot_general(a, b, dims, preferred_element_type=F32)


def _sigmoid(x):
    return 1.0 / (1.0 + jnp.exp(-x))


def _gelu_parts(x):
    k = math.sqrt(2.0 / math.pi)
    t = jnp.tanh(k * (x + 0.044715 * (x * x * x)))
    cdf = 0.5 * (1.0 + t)
    dcdf = 0.5 * (1.0 - t * t) * k * (1.0 + 3.0 * 0.044715 * (x * x))
    return x * cdf, cdf + x * dcdf


def _params(sem, vmem=VMEM_LIMIT):
    return pltpu.CompilerParams(dimension_semantics=sem, vmem_limit_bytes=vmem)


def _full(shape):
    return pl.BlockSpec(shape, lambda *_: (0,) * len(shape))


def _norm_proj(x, g, w, tm, tn, name, out_dtype=F32, job=None, w_turned=False):
    T, D = x.shape
    N = w.shape[0] if w_turned else w.shape[1]
    w_spec = pl.BlockSpec((tn, D), lambda i, j: (j, 0)) if w_turned else pl.BlockSpec((D, tn), lambda i, j: (0, j))
    dims = NT if w_turned else NN

    def body(x_ref, g_ref, w_ref, o_ref, h_ref, hs):
        @pl.when(pl.program_id(1) == 0)
        def _():
            xv = x_ref[...]
            r = lax.rsqrt(jnp.mean(xv * xv, axis=-1, keepdims=True) + EPS)
            hv = (xv * r * g_ref[...]).astype(BF16)
            hs[...] = hv
            h_ref[...] = hv

        o_ref[...] = _dot(hs[...], w_ref[...], dims).astype(out_dtype)

    return _pc(
        body, job, name=name, grid=(T // tm, N // tn),
        in_specs=[pl.BlockSpec((tm, D), lambda i, j: (i, 0)), _full((1, D)), w_spec],
        out_specs=[pl.BlockSpec((tm, tn), lambda i, j: (i, j)), pl.BlockSpec((tm, D), lambda i, j: (i, 0))],
        out_shape=[jax.ShapeDtypeStruct((T, N), out_dtype), jax.ShapeDtypeStruct((T, D), BF16)],
        scratch_shapes=[pltpu.VMEM((tm, D), BF16)], sem=("parallel", "arbitrary"), operands=(x, g, w))


def _mm_tn(a, b, tm, tn, tk, name, b_col=0, n=None, job=None):
    K, M = a.shape
    N = b.shape[1] if n is None else n
    nk = K // tk
    j0 = b_col // tn

    def body(a_ref, b_ref, o_ref, acc):
        k = pl.program_id(2)

        @pl.when(k == 0)
        def _():
            acc[...] = jnp.zeros_like(acc)

        acc[...] += _dot(a_ref[...].astype(BF16), b_ref[...].astype(BF16), TN)

        @pl.when(k == nk - 1)
        def _():
            o_ref[...] = acc[...].astype(BF16)

    out = _pc(
        body, job, name=name, grid=(M // tm, N // tn, nk),
        in_specs=[pl.BlockSpec((tk, tm), lambda i, j, k: (k, i)), pl.BlockSpec((tk, tn), lambda i, j, k: (k, j0 + j))],
        out_specs=[pl.BlockSpec((tm, tn), lambda i, j, k: (i, j))],
        out_shape=[jax.ShapeDtypeStruct((M, N), BF16)],
        scratch_shapes=[pltpu.VMEM((tm, tn), F32)], sem=("parallel", "parallel", "arbitrary"), operands=(a, b))
    return out[0] if job is None else (out[0][0], out[1])


def _proj_bwd(dp, w, x, g, dres, tm, tk, name, job=None, w_turned=False):
    T, N = dp.shape
    D = x.shape[1]
    nk = N // tk
    w_spec = pl.BlockSpec((tk, D), lambda i, k: (k, 0)) if w_turned else pl.BlockSpec((D, tk), lambda i, k: (0, k))
    dims = NN if w_turned else NT

    def body(dp_ref, w_ref, x_ref, g_ref, dres_ref, dx_ref, dg_ref, acc):
        i, k = pl.program_id(0), pl.program_id(1)

        @pl.when(k == 0)
        def _():
            acc[...] = jnp.zeros_like(acc)

        @pl.when((i == 0) & (k == 0))
        def _():
            dg_ref[...] = jnp.zeros_like(dg_ref)

        acc[...] += _dot(dp_ref[...], w_ref[...], dims)

        @pl.when(k == nk - 1)
        def _():
            xv = x_ref[...]
            dh = acc[...]
            r = lax.rsqrt(jnp.mean(xv * xv, axis=-1, keepdims=True) + EPS)
            xr = xv * r
            dg_ref[...] += jnp.sum(dh * xr, axis=0, keepdims=True)
            wv = dh * g_ref[...]
            dx_ref[...] = dres_ref[...] + r * (wv - xr * jnp.mean(wv * xr, axis=-1, keepdims=True))

    return _pc(
        body, job, name=name, grid=(T // tm, nk),
        in_specs=[pl.BlockSpec((tm, tk), lambda i, k: (i, k)), w_spec,
                  pl.BlockSpec((tm, D), lambda i, k: (i, 0)), _full((1, D)),
                  pl.BlockSpec((tm, D), lambda i, k: (i, 0))],
        out_specs=[pl.BlockSpec((tm, D), lambda i, k: (i, 0)), _full((1, D))],
        out_shape=[jax.ShapeDtypeStruct((T, D), F32), jax.ShapeDtypeStruct((1, D), F32)],
        scratch_shapes=[pltpu.VMEM((tm, D), F32)], sem=("arbitrary", "arbitrary"), operands=(dp, w, x, g, dres))


def _ssm_fwd(proj, bre, bim, cre, cimn, are, aim, d, wglu, bglu, tc, name, job=None):
    T = proj.shape[0]
    ucol, zcol = _OFF["u"] // D_SSM, _OFF["z_ssm"] // D_SSM

    def body(u_ref, z_ref, bre_ref, bim_ref, cre_ref, cim_ref, are_ref, aim_ref, d_ref, wg_ref, bg_ref,
             xr_ref, xi_ref, y_ref, o_ref, car_r, car_i):
        @pl.when(pl.program_id(0) == 0)
        def _():
            car_r[...] = jnp.zeros_like(car_r)
            car_i[...] = jnp.zeros_like(car_i)

        u = u_ref[...]
        ub = u.astype(BF16)
        for k in range(SSM_BLOCKS):
            uk = ub[:, 128 * k:128 * (k + 1)]
            xr_ref[:, 512 * k:512 * (k + 1)] = _dot(uk, bre_ref[k])
            xi_ref[:, 512 * k:512 * (k + 1)] = _dot(uk, bim_ref[k])
        ar, ai = are_ref[...], aim_ref[...]

        def step(t, c):
            pr, pi = c
            nr = ar * pr - ai * pi + xr_ref[pl.ds(t, 1), :]
            ni = ar * pi + ai * pr + xi_ref[pl.ds(t, 1), :]
            xr_ref[pl.ds(t, 1), :] = nr
            xi_ref[pl.ds(t, 1), :] = ni
            return nr, ni

        pr, pi = lax.fori_loop(0, tc, step, (car_r[...], car_i[...]))
        car_r[...] = pr
        car_i[...] = pi

        ys = []
        for k in range(SSM_BLOCKS):
            xrk = xr_ref[:, 512 * k:512 * (k + 1)].astype(BF16)
            xik = xi_ref[:, 512 * k:512 * (k + 1)].astype(BF16)
            ys.append(_dot(xrk, cre_ref[k]) + _dot(xik, cim_ref[k]))
        y = jnp.concatenate(ys, axis=1) + d_ref[...] * u
        y_ref[...] = y
        gl, _ = _gelu_parts(y)
        t = _dot(gl.astype(BF16), wg_ref[...]) + bg_ref[...]
        z = z_ref[...]
        o_ref[...] = (gl * _sigmoid(t) * (z * _sigmoid(z))).astype(BF16)

    return _pc(
        body, job, name=name, grid=(T // tc,),
        in_specs=[pl.BlockSpec((tc, D_SSM), lambda i: (i, ucol)), pl.BlockSpec((tc, D_SSM), lambda i: (i, zcol)),
                  _full((SSM_BLOCKS, 128, 512)), _full((SSM_BLOCKS, 128, 512)),
                  _full((SSM_BLOCKS, 512, 128)), _full((SSM_BLOCKS, 512, 128)),
                  _full((1, N_STATE)), _full((1, N_STATE)), _full((1, D_SSM)),
                  _full((D_SSM, D_SSM)), _full((1, D_SSM))],
        out_specs=[pl.BlockSpec((tc, N_STATE), lambda i: (i, 0)), pl.BlockSpec((tc, N_STATE), lambda i: (i, 0)),
                   pl.BlockSpec((tc, D_SSM), lambda i: (i, 0)), pl.BlockSpec((tc, D_SSM), lambda i: (i, 0))],
        out_shape=[jax.ShapeDtypeStruct((T, N_STATE), F32), jax.ShapeDtypeStruct((T, N_STATE), F32),
                   jax.ShapeDtypeStruct((T, D_SSM), F32), jax.ShapeDtypeStruct((T, D_SSM), BF16)],
        scratch_shapes=[pltpu.VMEM((1, N_STATE), F32), pltpu.VMEM((1, N_STATE), F32)], sem=("arbitrary",),
        operands=(proj, proj, bre, bim, cre, cimn, are, aim, d, wglu, bglu))


def _glu_bwd(do, y, proj, wglu, bglu, dproj, tm, name):
    T = y.shape[0]
    zcol = _OFF["z_ssm"] // D_SSM

    def body(do_ref, y_ref, z_ref, wg_ref, bg_ref, _, dy_ref, dz_ref, g_ref, dt_ref, db_ref):
        @pl.when(pl.program_id(0) == 0)
        def _():
            db_ref[...] = jnp.zeros_like(db_ref)

        dov = do_ref[...]
        gl, dgl = _gelu_parts(y_ref[...])
        glb = gl.astype(BF16)
        sg = _sigmoid(_dot(glb, wg_ref[...]) + bg_ref[...])
        z = z_ref[...]
        sz = _sigmoid(z)
        dz_ref[...] = (dov * (gl * sg) * (sz * (1.0 + z * (1.0 - sz)))).astype(BF16)
        dy2 = dov * (z * sz)
        dt = dy2 * gl * (sg * (1.0 - sg))
        dtb = dt.astype(BF16)
        dg = dy2 * sg + _dot(dtb, wg_ref[...], NT)
        dy_ref[...] = dg * dgl
        g_ref[...] = glb
        dt_ref[...] = dtb
        db_ref[...] += jnp.sum(dt, axis=0, keepdims=True)

    row = lambda i: (i, 0)
    return pl.pallas_call(
        body, name=name, grid=(T // tm,),
        in_specs=[pl.BlockSpec((tm, D_SSM), row), pl.BlockSpec((tm, D_SSM), row),
                  pl.BlockSpec((tm, D_SSM), lambda i: (i, zcol)), _full((D_SSM, D_SSM)), _full((1, D_SSM)),
                  pl.BlockSpec(memory_space=pl.ANY)],
        out_specs=[pl.BlockSpec((tm, D_SSM), row), pl.BlockSpec((tm, D_SSM), lambda i: (i, zcol)),
                   pl.BlockSpec((tm, D_SSM), row), pl.BlockSpec((tm, D_SSM), row), _full((1, D_SSM))],
        out_shape=[jax.ShapeDtypeStruct((T, D_SSM), F32), jax.ShapeDtypeStruct(dproj.shape, BF16),
                   jax.ShapeDtypeStruct((T, D_SSM), BF16), jax.ShapeDtypeStruct((T, D_SSM), BF16),
                   jax.ShapeDtypeStruct((1, D_SSM), F32)],
        input_output_aliases={5: 1},
        compiler_params=_params(("arbitrary",)),
    )(do, y, proj, wglu, bglu, dproj)


def _ssm_bwd(dy, proj, xr, xi, ctre, ctimn, btre, btim, are, aim, d, dproj, tc, name, job=None):
    T = dy.shape[0]
    nc = T // tc
    ucol = _OFF["u"] // D_SSM
    rb = tc // 8

    def body(dy_ref, u_ref, xr_ref, xi_ref, xpr_ref, xpi_ref, ctre_ref, ctim_ref, btre_ref, btim_ref,
             are_ref, aim_ref, d_ref, _,
             du_ref, dbre_ref, dbim_ref, dcre_ref, dcim_ref, dare_ref, daim_ref, dd_ref, gr, gi, car_r, car_i):
        i = pl.program_id(0)

        @pl.when(i == 0)
        def _():
            for ref in (car_r, car_i, dbre_ref, dbim_ref, dcre_ref, dcim_ref, dare_ref, daim_ref, dd_ref):
                ref[...] = jnp.zeros_like(ref)

        dyv = dy_ref[...]
        dyb = dyv.astype(BF16)
        u = u_ref[...]
        ub = u.astype(BF16)
        for k in range(SSM_BLOCKS):
            dk = dyb[:, 128 * k:128 * (k + 1)]
            gr[:, 512 * k:512 * (k + 1)] = _dot(dk, ctre_ref[k])
            gi[:, 512 * k:512 * (k + 1)] = _dot(dk, ctim_ref[k])
        ar, ai = are_ref[...], aim_ref[...]

        def step(s, c):
            pr, pi = c
            t = tc - 1 - s
            nr = gr[pl.ds(t, 1), :] + ar * pr + ai * pi
            ni = gi[pl.ds(t, 1), :] + ar * pi - ai * pr
            gr[pl.ds(t, 1), :] = nr
            gi[pl.ds(t, 1), :] = ni
            return nr, ni

        pr, pi = lax.fori_loop(0, tc, step, (car_r[...], car_i[...]))
        car_r[...] = pr
        car_i[...] = pi

        xrv, xiv = xr_ref[...], xi_ref[...]
        keep = jnp.where(i == nc - 1, 0.0, 1.0)
        row0 = lax.broadcasted_iota(jnp.int32, (tc, 1), 0) == 0
        xsr = jnp.where(row0, xpr_ref[7:8, :] * keep, pltpu.roll(xrv, 1, axis=0))
        xsi = jnp.where(row0, xpi_ref[7:8, :] * keep, pltpu.roll(xiv, 1, axis=0))
        grv, giv = gr[...], gi[...]
        dare_ref[...] += jnp.sum(grv * xsr + giv * xsi, axis=0, keepdims=True)
        daim_ref[...] += jnp.sum(giv * xsr - grv * xsi, axis=0, keepdims=True)
        dd_ref[...] += jnp.sum(dyv * u, axis=0, keepdims=True)

        dus = []
        for k in range(SSM_BLOCKS):
            sl = slice(512 * k, 512 * (k + 1))
            ch = slice(128 * k, 128 * (k + 1))
            grb, gib = grv[:, sl].astype(BF16), giv[:, sl].astype(BF16)
            dus.append(_dot(grb, btre_ref[k]) + _dot(gib, btim_ref[k]))
            dbre_ref[k] += _dot(grb, ub[:, ch], TN)
            dbim_ref[k] += _dot(gib, ub[:, ch], TN)
            dcre_ref[k] += _dot(dyb[:, ch], xrv[:, sl].astype(BF16), TN)
            dcim_ref[k] -= _dot(dyb[:, ch], xiv[:, sl].astype(BF16), TN)
        du_ref[...] = (jnp.concatenate(dus, axis=1) + d_ref[...] * dyv).astype(BF16)

    rev = lambda i: (nc - 1 - i, 0)
    prev = lambda i: (jnp.maximum((nc - 1 - i) * rb - 1, 0), 0)
    return _pc(
        body, job, name=name, grid=(nc,),
        in_specs=[pl.BlockSpec((tc, D_SSM), rev), pl.BlockSpec((tc, D_SSM), lambda i: (nc - 1 - i, ucol)),
                  pl.BlockSpec((tc, N_STATE), rev), pl.BlockSpec((tc, N_STATE), rev),
                  pl.BlockSpec((8, N_STATE), prev), pl.BlockSpec((8, N_STATE), prev),
                  _full((SSM_BLOCKS, 128, 512)), _full((SSM_BLOCKS, 128, 512)),
                  _full((SSM_BLOCKS, 512, 128)), _full((SSM_BLOCKS, 512, 128)),
                  _full((1, N_STATE)), _full((1, N_STATE)), _full((1, D_SSM)), pl.BlockSpec(memory_space=pl.ANY)],
        out_specs=[pl.BlockSpec((tc, D_SSM), lambda i: (nc - 1 - i, ucol)),
                   _full((SSM_BLOCKS, 512, 128)), _full((SSM_BLOCKS, 512, 128)),
                   _full((SSM_BLOCKS, 128, 512)), _full((SSM_BLOCKS, 128, 512)),
                   _full((1, N_STATE)), _full((1, N_STATE)), _full((1, D_SSM))],
        out_shape=[jax.ShapeDtypeStruct(dproj.shape, BF16),
                   jax.ShapeDtypeStruct((SSM_BLOCKS, 512, 128), F32), jax.ShapeDtypeStruct((SSM_BLOCKS, 512, 128), F32),
                   jax.ShapeDtypeStruct((SSM_BLOCKS, 128, 512), F32), jax.ShapeDtypeStruct((SSM_BLOCKS, 128, 512), F32),
                   jax.ShapeDtypeStruct((1, N_STATE), F32), jax.ShapeDtypeStruct((1, N_STATE), F32),
                   jax.ShapeDtypeStruct((1, D_SSM), F32)],
        scratch_shapes=[pltpu.VMEM((tc, N_STATE), F32), pltpu.VMEM((tc, N_STATE), F32),
                        pltpu.VMEM((1, N_STATE), F32), pltpu.VMEM((1, N_STATE), F32)], sem=("arbitrary",),
        operands=(dy, proj, xr, xi, xr, xi, ctre, ctimn, btre, btim, are, aim, d, dproj), aliases={13: 0})


def _rel_bucket(dist):
    n = jnp.maximum(dist, 0)
    max_exact = NUM_BUCKETS // 2
    n_f = jnp.maximum(n, 1).astype(F32)
    large = max_exact + (jnp.log(n_f / max_exact) / math.log(REL_MAX_DISTANCE / max_exact)
                         * (NUM_BUCKETS - max_exact)).astype(jnp.int32)
    large = jnp.minimum(large, NUM_BUCKETS - 1)
    return jnp.where(n < max_exact, n, large)


def _bucket_tables():
    qi = jnp.arange(ATTN_BLOCK)[:, None]
    kj = jnp.arange(2 * ATTN_BLOCK)[None, :]
    delta = jnp.maximum(ATTN_BLOCK + qi - kj, 0)
    return jnp.stack([_rel_bucket(delta * r) for r in ATTN_DILATIONS]).astype(jnp.int32)


def _bias_tables(rel_bias, buckets, name):
    def body(tab_ref, bk_ref, o_ref):
        g = pl.program_id(0)
        bk = bk_ref[...]
        qi = lax.broadcasted_iota(jnp.int32, bk.shape, 0)
        kj = lax.broadcasted_iota(jnp.int32, bk.shape, 1)
        delta = ATTN_BLOCK + qi - kj
        band = (delta >= 0) & (delta <= ATTN_SPAN)
        accs = [jnp.zeros(bk.shape, F32) for _ in range(4)]
        for b in range(NUM_BUCKETS):
            hit = bk == b
            for h in range(4):
                accs[h] = jnp.where(hit, tab_ref[b, 4 * g + h], accs[h])
        for h in range(4):
            o_ref[h] = jnp.where(band, accs[h], NEG_INF)

    return pl.pallas_call(
        body, name=name, grid=(3,),
        in_specs=[pl.BlockSpec(memory_space=pltpu.SMEM),
                  pl.BlockSpec((None, ATTN_BLOCK, 2 * ATTN_BLOCK), lambda g: (g, 0, 0))],
        out_specs=pl.BlockSpec((None, 4, ATTN_BLOCK, 2 * ATTN_BLOCK), lambda g: (g, 0, 0, 0)),
        out_shape=jax.ShapeDtypeStruct((3, 4, ATTN_BLOCK, 2 * ATTN_BLOCK), F32),
        compiler_params=_params(("parallel",)),
    )(rel_bias, buckets)


def _bias_grad(db0, db1, buckets, name):
    def body(a_ref, b_ref, bk_ref, o_ref):
        bk = bk_ref[...]
        for h in range(4):
            dv = a_ref[h] + b_ref[h]
            for b in range(NUM_BUCKETS):
                o_ref[h, b:b + 1, :] = jnp.sum(jnp.where(bk == b, dv, 0.0), axis=0, keepdims=True)

    tab = pl.BlockSpec((None, 4, ATTN_BLOCK, 2 * ATTN_BLOCK), lambda g: (g, 0, 0, 0))
    return pl.pallas_call(
        body, name=name, grid=(3,),
        in_specs=[tab, tab, pl.BlockSpec((None, ATTN_BLOCK, 2 * ATTN_BLOCK), lambda g: (g, 0, 0))],
        out_specs=pl.BlockSpec((None, 4, NUM_BUCKETS, 2 * ATTN_BLOCK), lambda g: (g, 0, 0, 0)),
        out_shape=jax.ShapeDtypeStruct((3, 4, NUM_BUCKETS, 2 * ATTN_BLOCK), F32),
        compiler_params=_params(("parallel",)),
    )(db0, db1, buckets)


_ATTN_SUB = {1: 4, 4: 1, 16: 1}
_UNROLL = 4


def _unit_rows(j, s, r):
    start = j * ATTN_BLOCK * r + s
    return pl.ds(start, ATTN_BLOCK, stride=r) if r > 1 else pl.ds(start, ATTN_BLOCK)


def _for_units(r, nsub, fn, after):
    if r * nsub <= _UNROLL:
        after([fn(j, s) for j in range(nsub) for s in range(r)])
    else:
        def four(i, c):
            after([fn(0, _UNROLL * i + k) for k in range(_UNROLL)])
            return c

        lax.fori_loop(0, r // _UNROLL, four, 0)


def _attn_cols(g):
    return tuple((_OFF[n] + ATTN_GROUP_WIDTH * g) // LANES for n in ("q", "k", "v"))


def _attn_fwd(proj, bias, g, name, job=None):
    r = ATTN_DILATIONS[g]
    nsub = _ATTN_SUB[r]
    T = proj.shape[0]
    sub = ATTN_BLOCK * r
    tb = sub * nsub
    qc, kc, vc = _attn_cols(g)
    scale = ATTN_HEAD_DIM ** -0.5

    def body(q_ref, kc_ref, kp_ref, vc_ref, vp_ref, bias_ref, o_ref, lse_ref):
        lane = lax.broadcasted_iota(jnp.int32, (ATTN_BLOCK, LANES), 1)
        kj = lax.broadcasted_iota(jnp.int32, (ATTN_BLOCK, 2 * ATTN_BLOCK), 1)
        dead = (pl.program_id(0) == 0) & (kj < ATTN_BLOCK)

        def one(j, s):
            rows = _unit_rows(j, s, r)
            before = _unit_rows(max(j - 1, 0), s, r)
            k_before = kc_ref[before, :] if j else kp_ref[before, :]
            v_before = vc_ref[before, :] if j else vp_ref[before, :]
            q = q_ref[rows, :]
            kcat = jnp.concatenate([k_before, kc_ref[rows, :]], axis=0).astype(BF16)
            vcat = jnp.concatenate([v_before, vc_ref[rows, :]], axis=0).astype(BF16)
            o_acc = jnp.zeros((ATTN_BLOCK, LANES), F32)
            l_acc = jnp.zeros((ATTN_BLOCK, LANES), F32)
            for hh in range(2):
                mine = (lane >= ATTN_HEAD_DIM) if hh else (lane < ATTN_HEAD_DIM)
                qm = jnp.where(mine, q, 0.0).astype(BF16)
                sc = _dot(qm, kcat, NT) * scale + bias_ref[hh]
                if j == 0:
                    sc = jnp.where(dead, NEG_INF, sc)
                m = jnp.max(sc, axis=-1, keepdims=True)
                p = jnp.exp(sc - m)
                l = jnp.sum(p, axis=-1, keepdims=True)
                o_acc = jnp.where(mine, _dot((p / l).astype(BF16), vcat), o_acc)
                l_acc = jnp.where(mine, m + jnp.log(l), l_acc)
            o_ref[rows, :] = o_acc
            lse_ref[rows, :] = l_acc

        _for_units(r, nsub, one, lambda results: None)

    cur = lambda c: pl.BlockSpec((tb, LANES), lambda b, p: (b, c + p))
    prev = lambda c: pl.BlockSpec((sub, LANES), lambda b, p: (jnp.maximum(b * nsub - 1, 0), c + p))
    out = pl.BlockSpec((tb, LANES), lambda b, p: (b, p))
    return _pc(
        body, job, name=name, grid=(T // tb, 2),
        in_specs=[cur(qc), cur(kc), prev(kc), cur(vc), prev(vc),
                  pl.BlockSpec((2, ATTN_BLOCK, 2 * ATTN_BLOCK), lambda b, p: (p, 0, 0))],
        out_specs=[out, out],
        out_shape=[jax.ShapeDtypeStruct((T, ATTN_GROUP_WIDTH), F32), jax.ShapeDtypeStruct((T, ATTN_GROUP_WIDTH), F32)],
        scratch_shapes=[], sem=("parallel", "parallel"), operands=(proj, proj, proj, proj, proj, bias))


def _attn_bwd(proj, do, corr, lse, bias, g, name):
    r = ATTN_DILATIONS[g]
    nsub = _ATTN_SUB[r]
    T = proj.shape[0]
    sub = ATTN_BLOCK * r
    tb = sub * nsub
    nb = T // tb
    qc, kc, vc = _attn_cols(g)
    dc = ATTN_GROUP_WIDTH * g // LANES
    scale = ATTN_HEAD_DIM ** -0.5

    def body(q_ref, kc_ref, kp_ref, vc_ref, vp_ref, do_ref, corr_ref, lse_ref, bias_ref,
             dq_ref, dk_ref, dv_ref, db_ref, dq_s, dkc_s, dkp_s, dvc_s, dvp_s, kacc, vacc):
        b = pl.program_id(1)

        @pl.when(b == 0)
        def _():
            db_ref[...] = jnp.zeros_like(db_ref)
            kacc[...] = jnp.zeros_like(kacc)
            vacc[...] = jnp.zeros_like(vacc)

        @pl.when(b == nb)
        def _():
            dk_ref[...] = kacc[...].astype(BF16)
            dv_ref[...] = vacc[...].astype(BF16)

        @pl.when(b < nb)
        def _():
            lane = lax.broadcasted_iota(jnp.int32, (ATTN_BLOCK, LANES), 1)
            kj = lax.broadcasted_iota(jnp.int32, (ATTN_BLOCK, 2 * ATTN_BLOCK), 1)
            dead = (b == 0) & (kj < ATTN_BLOCK)

            def one(j, s):
                rows = _unit_rows(j, s, r)
                before = _unit_rows(max(j - 1, 0), s, r)
                k_before = kc_ref[before, :] if j else kp_ref[before, :]
                v_before = vc_ref[before, :] if j else vp_ref[before, :]
                q = q_ref[rows, :]
                kcat = jnp.concatenate([k_before, kc_ref[rows, :]], axis=0).astype(BF16)
                vcat = jnp.concatenate([v_before, vc_ref[rows, :]], axis=0).astype(BF16)
                dov, corrv, lsev = do_ref[rows, :], corr_ref[rows, :], lse_ref[rows, :]
                dq_acc = jnp.zeros((ATTN_BLOCK, LANES), F32)
                dk_acc = jnp.zeros((2 * ATTN_BLOCK, LANES), F32)
                dv_acc = jnp.zeros((2 * ATTN_BLOCK, LANES), F32)
                dss = []
                for hh in range(2):
                    mine = (lane >= ATTN_HEAD_DIM) if hh else (lane < ATTN_HEAD_DIM)
                    col = slice(ATTN_HEAD_DIM * hh, ATTN_HEAD_DIM * hh + 1)
                    qm = jnp.where(mine, q, 0.0).astype(BF16)
                    dom = jnp.where(mine, dov, 0.0).astype(BF16)
                    sc = _dot(qm, kcat, NT) * scale + bias_ref[hh]
                    if j == 0:
                        sc = jnp.where(dead, NEG_INF, sc)
                    p = jnp.exp(sc - lsev[:, col])
                    ds = p * (_dot(dom, vcat, NT) - corrv[:, col])
                    dss.append(ds)
                    dsb = ds.astype(BF16)
                    dq_acc = jnp.where(mine, _dot(dsb, kcat) * scale, dq_acc)
                    dk_acc += _dot(dsb, qm, TN) * scale
                    dv_acc += _dot(p.astype(BF16), dom, TN)
                dq_s[rows, :] = dq_acc
                dkp_s[rows, :] = dk_acc[:ATTN_BLOCK]
                dkc_s[rows, :] = dk_acc[ATTN_BLOCK:]
                dvp_s[rows, :] = dv_acc[:ATTN_BLOCK]
                dvc_s[rows, :] = dv_acc[ATTN_BLOCK:]
                return dss

            def add_bias_grads(results):
                for hh in range(2):
                    db_ref[hh] += functools.reduce(lambda x, y: x + y, [dss[hh] for dss in results])

            _for_units(r, nsub, one, add_bias_grads)
            dq_ref[...] = dq_s[...].astype(BF16)
            tail = slice((nsub - 1) * sub, nsub * sub)
            for acc, before_s, cur_s, out_ref in ((kacc, dkp_s, dkc_s, dk_ref), (vacc, dvp_s, dvc_s, dv_ref)):
                acc[tail, :] += before_s[0:sub, :]
                out_ref[...] = acc[...].astype(BF16)
                acc[...] = cur_s[...]
                for j in range(nsub - 1):
                    acc[j * sub:(j + 1) * sub, :] += before_s[(j + 1) * sub:(j + 2) * sub, :]

    last = nb - 1
    blk = (tb, LANES)
    cur = lambda c: pl.BlockSpec(blk, lambda p, b: (jnp.minimum(b, last), c + p))
    prev = lambda c: pl.BlockSpec(blk, lambda p, b: (jnp.clip(b - 1, 0, last), c + p))
    before = lambda c: pl.BlockSpec((sub, LANES), lambda p, b: (jnp.clip(b * nsub - 1, 0, nb * nsub - 1), c + p))
    tab = pl.BlockSpec((2, ATTN_BLOCK, 2 * ATTN_BLOCK), lambda p, b: (p, 0, 0))
    return pl.pallas_call(
        body, name=name, grid=(2, nb + 1),
        in_specs=[cur(qc), cur(kc), before(kc), cur(vc), before(vc), cur(dc), cur(dc), cur(0), tab],
        out_specs=[cur(0), prev(0), prev(0), tab],
        out_shape=[jax.ShapeDtypeStruct((T, ATTN_GROUP_WIDTH), BF16)] * 3
        + [jax.ShapeDtypeStruct((4, ATTN_BLOCK, 2 * ATTN_BLOCK), F32)],
        scratch_shapes=[pltpu.VMEM(blk, F32)] * 7,
        compiler_params=_params(("arbitrary", "arbitrary")),
    )(proj, proj, proj, proj, proj, do, corr, lse, bias)


def _mix_weights(lses):
    m = jnp.maximum(jnp.maximum(lses[0], lses[1]), lses[2])
    es = [jnp.exp(l - m) for l in lses]
    inv = 1.0 / (es[0] + es[1] + es[2])
    return jnp.concatenate([e * inv for e in es], axis=1)


def _attn_mix(os, lses, proj, tm, name):
    T = proj.shape[0]
    zcol = _OFF["z_attn"] // D_ATTN

    def body(o0, o1, o2, l0, l1, l2, z_ref, out_ref):
        z = z_ref[...]
        o = jnp.concatenate([o0[...], o1[...], o2[...]], axis=1)
        alpha = _mix_weights([l0[...], l1[...], l2[...]])
        out_ref[...] = (o * alpha * (z * _sigmoid(z))).astype(BF16)

    row = lambda i: (i, 0)
    grp = pl.BlockSpec((tm, ATTN_GROUP_WIDTH), row)
    return pl.pallas_call(
        body, name=name, grid=(T // tm,),
        in_specs=[grp] * 6 + [pl.BlockSpec((tm, D_ATTN), lambda i: (i, zcol))],
        out_specs=pl.BlockSpec((tm, D_ATTN), row),
        out_shape=jax.ShapeDtypeStruct((T, D_ATTN), BF16),
        compiler_params=_params(("parallel",)),
    )(*os, *lses, proj)


def _attn_mix_bwd(d, os, lses, proj, dproj, tm, name):
    T = proj.shape[0]
    zcol = _OFF["z_attn"] // D_ATTN

    def body(d_ref, o0, o1, o2, l0, l1, l2, z_ref, _, do_ref, corr_ref, dz_ref):
        dv, z = d_ref[...], z_ref[...]
        ov = jnp.concatenate([o0[...], o1[...], o2[...]], axis=1)
        alpha = _mix_weights([l0[...], l1[...], l2[...]])
        sz = _sigmoid(z)
        oc = ov * alpha
        dz_ref[...] = (dv * oc * (sz * (1.0 + z * (1.0 - sz)))).astype(BF16)
        doc = dv * (z * sz)
        do_ref[...] = doc * alpha
        pr = doc * oc
        p3 = pr[:, 0:256] + pr[:, 256:512] + pr[:, 512:768]
        li = lax.broadcasted_iota(jnp.int32, (256, 256), 0) // ATTN_HEAD_DIM
        lj = lax.broadcasted_iota(jnp.int32, (256, 256), 1) // ATTN_HEAD_DIM
        ones = jnp.where(li == lj, 1.0, 0.0).astype(F32)
        s = lax.dot_general(p3, ones, NN, precision=lax.Precision.HIGHEST, preferred_element_type=F32)
        corr_ref[...] = alpha * jnp.concatenate([s, s, s], axis=1)

    row = lambda i: (i, 0)
    grp = pl.BlockSpec((tm, ATTN_GROUP_WIDTH), row)
    return pl.pallas_call(
        body, name=name, grid=(T // tm,),
        in_specs=[pl.BlockSpec((tm, D_ATTN), row)] + [grp] * 6 + [pl.BlockSpec((tm, D_ATTN), lambda i: (i, zcol)),
                                                                    pl.BlockSpec(memory_space=pl.ANY)],
        out_specs=[pl.BlockSpec((tm, D_ATTN), row)] * 2 + [pl.BlockSpec((tm, D_ATTN), lambda i: (i, zcol))],
        out_shape=[jax.ShapeDtypeStruct((T, D_ATTN), F32), jax.ShapeDtypeStruct((T, D_ATTN), F32),
                   jax.ShapeDtypeStruct(dproj.shape, BF16)],
        input_output_aliases={8: 2},
        compiler_params=_params(("parallel",)),
    )(d, *os, *lses, proj, dproj)


def _mem_probs(q_ref, kv_ref, h):
    hs = slice(MEM_HEAD_DIM * h, MEM_HEAD_DIM * (h + 1))
    qh = q_ref[:, hs].astype(BF16)
    kh = kv_ref[:, hs]
    vh = kv_ref[:, D_MEM + MEM_HEAD_DIM * h:D_MEM + MEM_HEAD_DIM * (h + 1)]
    s = _dot(qh, kh, NT) * (MEM_HEAD_DIM ** -0.5)
    p = jnp.exp(s - jnp.max(s, axis=-1, keepdims=True))
    pn = p / jnp.sum(p, axis=-1, keepdims=True)
    return qh, kh, vh, pn


def _mem_fwd(proj, kv, tm, name):
    T = proj.shape[0]
    M = kv.shape[0]
    qcol, zcol = _OFF["q_mem"] // D_MEM, _OFF["z_mem"] // D_MEM

    def body(q_ref, z_ref, kv_ref, o_ref):
        outs = []
        for h in range(MEM_HEADS):
            _, _, vh, pn = _mem_probs(q_ref, kv_ref, h)
            outs.append(_dot(pn.astype(BF16), vh))
        z = z_ref[...]
        o_ref[...] = (jnp.concatenate(outs, axis=1) * (z * _sigmoid(z))).astype(BF16)

    return pl.pallas_call(
        body, name=name, grid=(T // tm,),
        in_specs=[pl.BlockSpec((tm, D_MEM), lambda i: (i, qcol)), pl.BlockSpec((tm, D_MEM), lambda i: (i, zcol)),
                  _full((M, 2 * D_MEM))],
        out_specs=pl.BlockSpec((tm, D_MEM), lambda i: (i, 0)),
        out_shape=jax.ShapeDtypeStruct((T, D_MEM), BF16),
        compiler_params=_params(("parallel",)),
    )(proj, proj, kv)


def _mem_bwd(d, proj, kv, tm, name):
    T = proj.shape[0]
    M = kv.shape[0]
    qcol, zcol = _OFF["q_mem"] // D_MEM, _OFF["z_mem"] // D_MEM

    def body(d_ref, q_ref, z_ref, kv_ref, dq_ref, dz_ref, dkv_ref):
        @pl.when(pl.program_id(0) == 0)
        def _():
            dkv_ref[...] = jnp.zeros_like(dkv_ref)

        z = z_ref[...]
        sz = _sigmoid(z)
        dv = d_ref[...]
        dov = dv * (z * sz)
        scale = MEM_HEAD_DIM ** -0.5
        outs, dqs = [], []
        for h in range(MEM_HEADS):
            hs = slice(MEM_HEAD_DIM * h, MEM_HEAD_DIM * (h + 1))
            qh, kh, vh, pn = _mem_probs(q_ref, kv_ref, h)
            pnb = pn.astype(BF16)
            oh = _dot(pnb, vh)
            outs.append(oh)
            doh = dov[:, hs]
            dohb = doh.astype(BF16)
            dp = _dot(dohb, vh, NT)
            ds = pn * (dp - jnp.sum(doh * oh, axis=-1, keepdims=True))
            dsb = ds.astype(BF16)
            dqs.append(_dot(dsb, kh) * scale)
            dkv_ref[:, hs] += _dot(dsb, qh, TN) * scale
            vs = slice(D_MEM + MEM_HEAD_DIM * h, D_MEM + MEM_HEAD_DIM * (h + 1))
            dkv_ref[:, vs] += _dot(pnb, dohb, TN)
        dq_ref[...] = jnp.concatenate(dqs, axis=1).astype(BF16)
        dz_ref[...] = (dv * jnp.concatenate(outs, axis=1) * (sz * (1.0 + z * (1.0 - sz)))).astype(BF16)

    row = lambda i: (i, 0)
    return pl.pallas_call(
        body, name=name, grid=(T // tm,),
        in_specs=[pl.BlockSpec((tm, D_MEM), row), pl.BlockSpec((tm, D_MEM), lambda i: (i, qcol)),
                  pl.BlockSpec((tm, D_MEM), lambda i: (i, zcol)), _full((M, 2 * D_MEM))],
        out_specs=[pl.BlockSpec((tm, D_MEM), row), pl.BlockSpec((tm, D_MEM), row), _full((M, 2 * D_MEM))],
        out_shape=[jax.ShapeDtypeStruct((T, D_MEM), BF16), jax.ShapeDtypeStruct((T, D_MEM), BF16),
                   jax.ShapeDtypeStruct((M, 2 * D_MEM), F32)],
        compiler_params=_params(("arbitrary",)),
    )(d, proj, proj, kv)


def _branches_and_gates(os_ref, oa_ref, om_ref, gl_refs, bg_ref, ws_ref, wa_ref, wm_ref):
    outs = (_dot(os_ref[...], ws_ref[...]), _dot(oa_ref[...], wa_ref[...]), _dot(om_ref[...], wm_ref[...]))
    gates = tuple(_sigmoid(jnp.concatenate([gl_refs[2 * k][...], gl_refs[2 * k + 1][...]], axis=1)
                           + bg_ref[:, D_MODEL * k:D_MODEL * (k + 1)]) for k in range(3))
    return outs, gates


def _merge_specs(tm):
    row = lambda i: (i, 0)
    first = _OFF["gates"] // GATE_BLOCK
    gate = [pl.BlockSpec((tm, GATE_BLOCK), (lambda i, k=k: (i, first + k))) for k in range(N_GATES // GATE_BLOCK)]
    return ([pl.BlockSpec((tm, D_SSM), row), pl.BlockSpec((tm, D_ATTN), row), pl.BlockSpec((tm, D_MEM), row)] + gate
            + [_full((1, N_GATES)), _full((D_SSM, D_MODEL)), _full((D_ATTN, D_MODEL)), _full((D_MEM, D_MODEL)),
               _full((D_MODEL, D_MODEL))])


def _merge_fwd(x, o_ssm, o_attn, o_mem, proj, bg, ws, wa, wm, wo, tm, name):
    T = x.shape[0]

    def body(os_ref, oa_ref, om_ref, g0, g1, g2, g3, g4, g5, bg_ref, ws_ref, wa_ref, wm_ref, wo_ref, x_ref,
             xo_ref, mg_ref):
        outs, gates = _branches_and_gates(os_ref, oa_ref, om_ref, (g0, g1, g2, g3, g4, g5), bg_ref, ws_ref, wa_ref,
                                          wm_ref)
        merged = (gates[0] * outs[0] + gates[1] * outs[1] + gates[2] * outs[2]).astype(BF16)
        mg_ref[...] = merged
        xo_ref[...] = x_ref[...] + _dot(merged, wo_ref[...])

    row = lambda i: (i, 0)
    return pl.pallas_call(
        body, name=name, grid=(T // tm,),
        in_specs=_merge_specs(tm) + [pl.BlockSpec((tm, D_MODEL), row)],
        out_specs=[pl.BlockSpec((tm, D_MODEL), row), pl.BlockSpec((tm, D_MODEL), row)],
        out_shape=[jax.ShapeDtypeStruct((T, D_MODEL), F32), jax.ShapeDtypeStruct((T, D_MODEL), BF16)],
        compiler_params=_params(("parallel",)),
    )(o_ssm, o_attn, o_mem, *([proj] * (N_GATES // GATE_BLOCK)), bg, ws, wa, wm, wo, x)


def _merge_bwd(dx, o_ssm, o_attn, o_mem, proj, bg, ws, wa, wm, wo, tm, name, job=None):
    T = dx.shape[0]

    n = T // tm

    def body(os_ref, oa_ref, om_ref, g0, g1, g2, g3, g4, g5, bg_ref, ws_ref, wa_ref, wm_ref, wo_ref, dx_ref,
             dproj_ref, db_ref, dos_ref, doa_ref, dom_ref, dbg_ref, dgl_buf, dgl_sems):
        i = pl.program_id(0)
        slot = i % 2

        def to_dproj(s, row0):
            return pltpu.make_async_copy(dgl_buf.at[s], dproj_ref.at[pl.ds(row0, tm), pl.ds(_OFF["gates"], N_GATES)],
                                         dgl_sems.at[s])

        @pl.when(i == 0)
        def _():
            dbg_ref[...] = jnp.zeros_like(dbg_ref)

        @pl.when(i >= 2)
        def _():
            to_dproj(slot, 0).wait()

        outs, gates = _branches_and_gates(os_ref, oa_ref, om_ref, (g0, g1, g2, g3, g4, g5), bg_ref, ws_ref, wa_ref,
                                          wm_ref)
        dm = _dot(dx_ref[...].astype(BF16), wo_ref[...], NT)
        w_refs = (ws_ref, wa_ref, wm_ref)
        do_refs = (dos_ref, doa_ref, dom_ref)
        for k in range(3):
            cols = slice(D_MODEL * k, D_MODEL * (k + 1))
            dgl = dm * outs[k] * (gates[k] * (1.0 - gates[k]))
            dgl_buf[slot, :, cols] = dgl.astype(BF16)
            dbg_ref[:, cols] += jnp.sum(dgl, axis=0, keepdims=True)
            dbk = (dm * gates[k]).astype(BF16)
            db_ref[:, cols] = dbk
            do_refs[k][...] = _dot(dbk, w_refs[k][...], NT)
        to_dproj(slot, pl.multiple_of(i * tm, tm)).start()

        @pl.when(i == n - 1)
        def _():
            for s in range(min(2, n)):
                to_dproj(s, 0).wait()

    row = lambda i: (i, 0)
    return _pc(
        body, job, name=name, grid=(n,),
        in_specs=_merge_specs(tm) + [pl.BlockSpec((tm, D_MODEL), row)],
        out_specs=[pl.BlockSpec(memory_space=pl.ANY), pl.BlockSpec((tm, N_GATES), row), pl.BlockSpec((tm, D_SSM), row),
                   pl.BlockSpec((tm, D_ATTN), row), pl.BlockSpec((tm, D_MEM), row), _full((1, N_GATES))],
        out_shape=[jax.ShapeDtypeStruct((T, D_IN), BF16), jax.ShapeDtypeStruct((T, N_GATES), BF16),
                   jax.ShapeDtypeStruct((T, D_SSM), F32), jax.ShapeDtypeStruct((T, D_ATTN), F32),
                   jax.ShapeDtypeStruct((T, D_MEM), F32), jax.ShapeDtypeStruct((1, N_GATES), F32)],
        scratch_shapes=[pltpu.VMEM((2, tm, N_GATES), BF16), pltpu.SemaphoreType.DMA((2,))], sem=("arbitrary",),
        operands=(o_ssm, o_attn, o_mem, *([proj] * (N_GATES // GATE_BLOCK)), bg, ws, wa, wm, wo, dx))


def _loss_head(x, g, target, tm, name):
    T, D = x.shape

    def body(x_ref, g_ref, t_ref, loss_ref, dx_ref, dg_ref):
        @pl.when(pl.program_id(0) == 0)
        def _():
            loss_ref[...] = jnp.zeros_like(loss_ref)
            dg_ref[...] = jnp.zeros_like(dg_ref)

        xv = x_ref[...]
        r = lax.rsqrt(jnp.mean(xv * xv, axis=-1, keepdims=True) + EPS)
        xr = xv * r
        err = xr * g_ref[...] - t_ref[...]
        loss_ref[...] += 0.5 * jnp.sum(jnp.mean(err * err, axis=-1, keepdims=True), axis=0, keepdims=True)
        dy = err * (1.0 / D)
        dg_ref[...] += jnp.sum(dy * xr, axis=0, keepdims=True)
        wv = dy * g_ref[...]
        dx_ref[...] = r * (wv - xr * jnp.mean(wv * xr, axis=-1, keepdims=True))

    row = lambda i: (i, 0)
    return pl.pallas_call(
        body, name=name, grid=(T // tm,),
        in_specs=[pl.BlockSpec((tm, D), row), _full((1, D)), pl.BlockSpec((tm, D), row)],
        out_specs=[_full((1, 128)), pl.BlockSpec((tm, D), row), _full((1, D))],
        out_shape=[jax.ShapeDtypeStruct((1, 128), F32), jax.ShapeDtypeStruct((T, D), F32),
                   jax.ShapeDtypeStruct((1, D), F32)],
        compiler_params=_params(("arbitrary",)),
    )(x, g, target)


def _adamw(parts, w, m, v, tr, name):
    L, R, C = w.shape

    def body(p_ref, w_ref, m_ref, v_ref, g_ref, d_ref, mo_ref, vo_ref):
        g = p_ref[0].astype(F32)
        for s in range(1, N_DEV):
            g = g + p_ref[s].astype(F32)
        mn = ADAM_B1 * m_ref[...] + (1.0 - ADAM_B1) * g
        vn = ADAM_B2 * v_ref[...] + (1.0 - ADAM_B2) * (g * g)
        m_hat = mn / (1.0 - ADAM_B1 ** ADAM_STEP)
        v_hat = vn / (1.0 - ADAM_B2 ** ADAM_STEP)
        g_ref[...] = g
        d_ref[...] = -ADAM_LR * (m_hat / (jnp.sqrt(v_hat) + ADAM_EPS) + ADAM_WD * w_ref[...])
        mo_ref[...] = mn
        vo_ref[...] = vn

    one = pl.BlockSpec((None, tr, C), lambda l, i: (l, i, 0))
    return pl.pallas_call(
        body, name=name, grid=(L, R // tr),
        in_specs=[pl.BlockSpec((N_DEV, None, tr, C), lambda l, i: (0, l, i, 0)), one, one, one],
        out_specs=[one] * 4,
        out_shape=[jax.ShapeDtypeStruct((L, R, C), F32)] * 4,
        compiler_params=_params(("parallel", "parallel")),
    )(parts, w, m, v)


_SHARDED = (("w_in", (1088, 1024), 1), ("w_glu", (96, 768), 0), ("w_mem_kv", (128, 1024), 0),
            ("w_br_ssm", (768, 128), 1), ("w_br_attn", (768, 128), 1), ("w_br_mem", (512, 128), 1),
            ("w_out", (128, 1024), 0))
_W_IN = 0
_SMALL = tuple(range(1, len(_SHARDED)))


class _Job(NamedTuple):
    ins: list
    out_shape: list
    aliases: dict
    pairs: Callable
    n: int


def _peers():
    x, y, c = lax.axis_index("x"), lax.axis_index("y"), lax.axis_index("c")
    me = 4 * x + 2 * y + c
    out = []
    for k in range(1, N_DEV):
        px = 1 - x if k & 4 else x
        py = 1 - y if k & 2 else y
        pc = 1 - c if k & 1 else c
        out.append(((px, py, pc), 4 * px + 2 * py + pc))
    return me, out


def _copies(pairs, send_sems, recv_sems, local_sems, arrivals):
    me, peers = _peers()
    local = [pltpu.make_async_copy(src(me), dst(me), local_sems.at[j]) for j, (src, dst) in enumerate(pairs)]
    sends, recvs = [], []
    for k, (peer, lin) in enumerate(peers):
        for j, (src, dst) in enumerate(pairs):
            for to, out in ((dst(me), sends), (dst(lin), recvs)):
                if out is sends or arrivals:
                    out.append(pltpu.make_async_remote_copy(
                        src_ref=src(lin), dst_ref=to, send_sem=send_sems.at[j, k], recv_sem=recv_sems.at[j, k],
                        device_id=peer, device_id_type=pl.DeviceIdType.MESH))
    return local, sends, recvs


def _start_copies(pairs, *sems):
    local, sends, _ = _copies(pairs, *sems, arrivals=False)
    for cp in local + sends:
        cp.start()


def _wait_copies(pairs, *sems):
    local, sends, recvs = _copies(pairs, *sems, arrivals=True)
    for cp in recvs:
        cp.wait_recv()
    for cp in sends:
        cp.wait_send()
    for cp in local:
        cp.wait()


def _job_scratch(job):
    return [pltpu.SemaphoreType.DMA((job.n, N_DEV - 1)), pltpu.SemaphoreType.DMA((job.n, N_DEV - 1)),
            pltpu.SemaphoreType.DMA((job.n,))]


def _pc(body, job, *, name, grid, in_specs, out_specs, out_shape, scratch_shapes, sem, operands, aliases=None):
    aliases = aliases or {}
    if job is None:
        return pl.pallas_call(body, name=name, grid=grid, in_specs=in_specs, out_specs=out_specs, out_shape=out_shape,
                              scratch_shapes=scratch_shapes, input_output_aliases=aliases,
                              compiler_params=_params(sem))(*operands)
    a = len(in_specs)
    b = a + len(job.ins)
    c = b + len(out_shape)
    d = c + len(job.out_shape)
    e = d + len(scratch_shapes)

    def carried(*refs):
        pairs = job.pairs(refs[a:b], refs[c:d])
        ids = [pl.program_id(k) for k in range(len(grid))]
        first = functools.reduce(jnp.logical_and, [i == 0 for i in ids])
        last = functools.reduce(jnp.logical_and, [i == n - 1 for i, n in zip(ids, grid)])

        @pl.when(first)
        def _():
            _start_copies(pairs, *refs[e:])

        body(*refs[:a], *refs[b:c], *refs[d:e])

        @pl.when(last)
        def _():
            _wait_copies(pairs, *refs[e:])

    hbm = pl.BlockSpec(memory_space=pl.ANY)
    outs = pl.pallas_call(
        carried, name=name, grid=grid,
        in_specs=list(in_specs) + [hbm] * len(job.ins), out_specs=list(out_specs) + [hbm] * len(job.out_shape),
        out_shape=list(out_shape) + list(job.out_shape),
        input_output_aliases={**aliases, **{a + i: len(out_shape) + o for i, o in job.aliases.items()}},
        scratch_shapes=list(scratch_shapes) + _job_scratch(job),
        compiler_params=_params(("arbitrary",) * len(grid)),
    )(*operands, *job.ins)
    return outs[:len(out_shape)], outs[len(out_shape):]


def _gather_via_sibling(x, take, place, out_shape, name, landing=None):
    def body(*refs):
        x_ref, o_ref = refs[0], refs[-4]
        send_sems, recv_sems, local_sem = refs[-3:]
        x, y, c = lax.axis_index("x"), lax.axis_index("y"), lax.axis_index("c")
        me, sibling = (x, y, c), (x, y, 1 - c)
        chips = [(1 - x, y), (x, 1 - y), (1 - x, 1 - y)]
        src = take(x_ref)

        def slot(px, py, pc):
            return place(o_ref, 4 * px + 2 * py + pc)

        def copy(k, block, to, first_hand):
            return pltpu.make_async_remote_copy(
                src_ref=src if first_hand else slot(*block), dst_ref=slot(*block), send_sem=send_sems.at[k],
                recv_sem=recv_sems.at[k], device_id=to, device_id_type=pl.DeviceIdType.MESH)

        mine = pltpu.make_async_copy(src, slot(*me), local_sem)
        mine.start()
        first = [copy(0, me, sibling, True)] + [copy(1 + j, me, (*chip, c), True) for j, chip in enumerate(chips)]
        for cp in first:
            cp.start()
        passed = []
        for j, chip in enumerate(chips):
            copy(1 + j, (*chip, c), me, True).wait_recv()
            passed.append(copy(4 + j, (*chip, c), sibling, False))
            passed[-1].start()
        copy(0, sibling, me, True).wait_recv()
        for j, chip in enumerate(chips):
            copy(4 + j, (*chip, 1 - c), me, False).wait_recv()
        for cp in first + passed:
            cp.wait_send()
        mine.wait()

    hbm = pl.BlockSpec(memory_space=pl.ANY)
    ins = [x] if landing is None else [x, landing]
    return pl.pallas_call(
        body, name=name, in_specs=[hbm] * len(ins), out_specs=hbm, out_shape=out_shape,
        input_output_aliases={} if landing is None else {1: 0},
        scratch_shapes=[pltpu.SemaphoreType.DMA((N_DEV - 1,)), pltpu.SemaphoreType.DMA((N_DEV - 1,)),
                        pltpu.SemaphoreType.DMA],
    )(*ins)


def _lane_window(ref, who):
    return ref.at[:, pl.ds(pl.multiple_of(who * LANES, LANES), LANES)]


def _gather_job(shards, items):
    out_shape = []
    for i, _ in items:
        _, s, axis = _SHARDED[i]
        whole = i != _W_IN and axis == 1
        out_shape.append(jax.ShapeDtypeStruct((s[0], N_DEV * s[1]) if whole else (N_DEV,) + s, BF16))

    def pairs(in_refs, out_refs):
        out = []
        for (i, l), src, dst in zip(items, in_refs, out_refs):
            if i != _W_IN and _SHARDED[i][2] == 1:
                out.append((lambda who, src=src, l=l: src.at[l], lambda who, dst=dst: _lane_window(dst, who)))
            else:
                out.append((lambda who, src=src, l=l: src.at[l], lambda who, dst=dst: dst.at[who]))
        return out

    return _Job([shards[i] for i, _ in items], out_shape, {}, pairs, len(items))


def _landed_weights(items, landed):
    out = {}
    for (i, _), a in zip(items, landed):
        n, s, axis = _SHARDED[i]
        if i == _W_IN:
            out[n] = a.reshape(D_IN, D_MODEL)
        elif axis == 0:
            out[n] = a.reshape(N_DEV * s[0], s[1])
        else:
            out[n] = a
    return out


def _scatter_job(grads, items, layer, parts=None):
    ng = len(grads)
    out_shape = [jax.ShapeDtypeStruct((N_DEV, DEPTH) + _SHARDED[i][1], BF16) for i in items]

    def pairs(in_refs, out_refs):
        out = []
        for i, src, dst in zip(items, in_refs[:ng], out_refs):
            _, s, axis = _SHARDED[i]
            if i == _W_IN:
                take = lambda who, src=src: src.at[who]
            elif axis == 0:
                take = lambda who, src=src, s=s: src.at[pl.ds(pl.multiple_of(who * s[0], 16), s[0])]
            else:
                take = lambda who, src=src: _lane_window(src, who)
            out.append((take, lambda who, dst=dst: dst.at[who, layer]))
        return out

    aliases = {} if parts is None else {ng + j: j for j in range(len(items))}
    return _Job(list(grads) + ([] if parts is None else list(parts)), out_shape, aliases, pairs, len(items))


def _rows_job(src, row0, landing=None):
    n = src.shape[0]
    pairs = lambda in_refs, out_refs: [(lambda who: in_refs[0], lambda who: out_refs[0].at[who, pl.ds(row0, n)])]
    return _Job([src] + ([] if landing is None else [landing]), [jax.ShapeDtypeStruct((N_DEV, _REP_ROWS, LANES), F32)],
                {} if landing is None else {1: 0}, pairs, 1)


_REPLICATED = (("norm_g", (2, 1024)), ("mem_norm_g", (2, 1024)), ("b_gate", (2, 3072)),
               ("ssm_lambda_re", (2, 48, 64)), ("ssm_lambda_im", (2, 48, 64)), ("ssm_log_dt", (2, 48)),
               ("ssm_b_re", (2, 48, 64, 16)), ("ssm_b_im", (2, 48, 64, 16)), ("ssm_c_re", (2, 48, 16, 64)),
               ("ssm_c_im", (2, 48, 16, 64)), ("ssm_d", (2, 768)), ("b_glu", (2, 768)), ("rel_bias", (32, 12)),
               ("final_norm_g", (1024,)))
_PER_LAYER = tuple((n, s[1:]) for n, s in _REPLICATED if s[0] == DEPTH and len(s) > 1)
_SHARED = tuple((n, s) for n, s in _REPLICATED if (n, s[1:]) not in _PER_LAYER)
_REP_HALF_ROWS = 1664
_REP_ROWS = 2 * _REP_HALF_ROWS
assert sum(int(np.prod(s)) for _, s in _PER_LAYER + _SHARED) <= _REP_HALF_ROWS * LANES


def _pack_half(tree, layer, shared):
    flat = [tree[n][layer].reshape(-1) for n, _ in _PER_LAYER]
    if shared:
        flat += [tree[n].reshape(-1) for n, _ in _SHARED]
    flat = jnp.concatenate(flat)
    return jnp.pad(flat, (0, _REP_HALF_ROWS * LANES - flat.shape[0])).reshape(_REP_HALF_ROWS, LANES)


def _pack_replicated(tree):
    return jnp.concatenate([_pack_half(tree, 1, False), _pack_half(tree, 0, True)])[None]


def _unpack_replicated(packed):
    halves = packed.reshape(2, -1)
    out, r = {}, 0
    for n, s in _PER_LAYER:
        size = int(np.prod(s))
        out[n] = jnp.stack([halves[1, r:r + size].reshape(s), halves[0, r:r + size].reshape(s)])
        r += size
    for n, s in _SHARED:
        size = int(np.prod(s))
        out[n] = halves[1, r:r + size].reshape(s)
        r += size
    return out


def _discretize(lam_re, lam_im, log_dt, b_re, b_im):
    dt = jnp.exp(log_dt)[:, None]
    mag = jnp.exp(lam_re * dt)
    abar_re, abar_im = mag * jnp.cos(lam_im * dt), mag * jnp.sin(lam_im * dt)
    den = lam_re * lam_re + lam_im * lam_im
    nr, ni = abar_re - 1.0, abar_im
    f_re = (nr * lam_re + ni * lam_im) / den
    f_im = (ni * lam_re - nr * lam_im) / den
    bbar_re = f_re[..., None] * b_re - f_im[..., None] * b_im
    bbar_im = f_re[..., None] * b_im + f_im[..., None] * b_re
    return abar_re, abar_im, bbar_re, bbar_im


def _block_diag(a):
    _, R, C = a.shape
    a = a.reshape(SSM_BLOCKS, 8, R, C)
    eye = jnp.eye(8, dtype=a.dtype)
    return (a[:, :, :, None, :] * eye[None, :, None, :, None]).reshape(SSM_BLOCKS, 8 * R, 8 * C)


def _diag_blocks(a, R, C):
    a = a.reshape(SSM_BLOCKS, 8, R, 8, C)
    eye = jnp.eye(8, dtype=a.dtype)
    return jnp.sum(a * eye[None, :, None, :, None], axis=3).reshape(SSM_GROUPS, R, C)


def _carried(result, job):
    return (result, None) if job is None else result


def _layer_fwd(x, mem, W, P, bias, layer, jobs):
    tag = f"l{layer}"
    abar_re, abar_im, bbar_re, bbar_im = _discretize(P["ssm_lambda_re"][layer], P["ssm_lambda_im"][layer],
                                                     P["ssm_log_dt"][layer], P["ssm_b_re"][layer], P["ssm_b_im"][layer])
    c_re, c_im = P["ssm_c_re"][layer], P["ssm_c_im"][layer]
    ssm = dict(
        are=abar_re.reshape(1, N_STATE), aim=abar_im.reshape(1, N_STATE),
        bre=_block_diag(bbar_re.transpose(0, 2, 1)).astype(BF16), bim=_block_diag(bbar_im.transpose(0, 2, 1)).astype(BF16),
        cre=_block_diag(c_re.transpose(0, 2, 1)).astype(BF16), cimn=_block_diag(-c_im.transpose(0, 2, 1)).astype(BF16),
        ctre=_block_diag(c_re).astype(BF16), ctimn=_block_diag(-c_im).astype(BF16),
        btre=_block_diag(bbar_re).astype(BF16), btim=_block_diag(bbar_im).astype(BF16),
        d=P["ssm_d"][layer].reshape(1, D_SSM))
    bglu = P["b_glu"][layer].reshape(1, D_SSM)
    bgate = P["b_gate"][layer].reshape(1, N_GATES)
    g = P["norm_g"][layer].reshape(1, D_MODEL)
    gm = P["mem_norm_g"][layer].reshape(1, D_MODEL)
    delivered = {}

    def carry(stage):
        return jobs[stage][0] if stage in jobs else None

    def deliver(stage, landed):
        if landed is not None:
            delivered[stage] = _landed_weights(jobs[stage][1], landed)

    T = x.shape[0]
    (proj, h), landed = _carried(_norm_proj(x, g, W["w_in"], min(T, 1024), 2176, f"{tag}_proj", job=carry("proj"),
                                            w_turned=True), carry("proj"))
    deliver("proj", landed)
    W = {**W, **delivered.get("proj", {})}
    (xr, xi, y, o_ssm), landed = _carried(
        _ssm_fwd(proj, ssm["bre"], ssm["bim"], ssm["cre"], ssm["cimn"], ssm["are"], ssm["aim"], ssm["d"], W["w_glu"],
                 bglu, 256, f"{tag}_ssm", job=carry("ssm")), carry("ssm"))
    deliver("ssm", landed)
    os, lses = [], []
    for grp in range(3):
        stage = f"attn{grp}"
        (o_g, lse_g), landed = _carried(_attn_fwd(proj, bias[grp], grp, f"{tag}_{stage}", job=carry(stage)), carry(stage))
        deliver(stage, landed)
        os.append(o_g)
        lses.append(lse_g)
    o_attn = _attn_mix(os, lses, proj, 512, f"{tag}_attn_mix")
    kvb, hm = _norm_proj(mem, gm, W["w_mem_kv"], mem.shape[0], 1024, f"{tag}_mem_kv", out_dtype=BF16)
    o_mem = _mem_fwd(proj, kvb, 512, f"{tag}_mem")
    x_out, merged = _merge_fwd(x, o_ssm, o_attn, o_mem, proj, bgate, W["w_br_ssm"], W["w_br_attn"], W["w_br_mem"],
                               W["w_out"], 256, f"{tag}_merge")
    res = dict(x=x, mem=mem, proj=proj, h=h, xr=xr, xi=xi, y=y, o_ssm=o_ssm, os=os, lses=lses,
               o_attn=o_attn, kvb=kvb, hm=hm, o_mem=o_mem, merged=merged, ssm=ssm, bglu=bglu,
               bgate=bgate, g=g, gm=gm, W=W)
    return x_out, res, delivered


def _layer_bwd(dx, res, P, bias, layer, jobs):
    tag = f"l{layer}b"
    proj, ssm, W = res["proj"], res["ssm"], res["W"]
    T = dx.shape[0]
    landed = {}

    def run(stage, fn, job):
        out, landed[stage] = _carried(fn(job), job)
        if job is None:
            del landed[stage]
        return out

    dproj, dbr, do_ssm, do_attn, do_mem, dbg = run(
        "merge", lambda job: _merge_bwd(dx, res["o_ssm"], res["o_attn"], res["o_mem"], proj, res["bgate"], W["w_br_ssm"],
                                        W["w_br_attn"], W["w_br_mem"], W["w_out"], 256, f"{tag}_merge", job=job),
        jobs.get("merge"))
    gw = {}
    gw["w_out"] = _mm_tn(res["merged"], dx, 1024, 1024, 512, f"{tag}_dw_out")
    gw["w_br_ssm"] = _mm_tn(res["o_ssm"], dbr, 768, 1024, 512, f"{tag}_dw_br_ssm", b_col=0, n=1024)
    gw["w_br_attn"] = _mm_tn(res["o_attn"], dbr, 768, 1024, 512, f"{tag}_dw_br_attn", b_col=1024, n=1024)
    gw["w_br_mem"] = _mm_tn(res["o_mem"], dbr, 512, 1024, 512, f"{tag}_dw_br_mem", b_col=2048, n=1024)

    dqm, dzm, dkv = _mem_bwd(do_mem, proj, res["kvb"], 512, f"{tag}_mem")
    M = dkv.shape[0]
    gw["w_mem_kv"] = _mm_tn(res["hm"], dkv, 1024, 1024, M, f"{tag}_dw_mem_kv")
    _, dgm = _proj_bwd(dkv.astype(BF16), W["w_mem_kv"], res["mem"], res["gm"], jnp.zeros_like(res["mem"]), M, 1024,
                       f"{tag}_mem_norm")

    do_g, corr, dproj = _attn_mix_bwd(do_attn, res["os"], res["lses"], proj, dproj, 512, f"{tag}_attn_mix")
    dqs, dks, dvs, dbs = [], [], [], []
    for grp in range(3):
        dq_g, dk_g, dv_g, db_g = _attn_bwd(proj, do_g, corr, res["lses"][grp], bias[grp], grp, f"{tag}_attn{grp}")
        dqs.append(dq_g)
        dks.append(dk_g)
        dvs.append(dv_g)
        dbs.append(db_g)
    dbias = jnp.stack(dbs)

    dy, dproj, gelu_b, dt_b, dbglu = _glu_bwd(do_ssm, res["y"], proj, W["w_glu"], res["bglu"], dproj, 512, f"{tag}_glu")
    gw["w_glu"] = _mm_tn(gelu_b, dt_b, 768, 768, 512, f"{tag}_dw_glu")
    dproj, dbre, dbim, dcre, dcim, dare, daim, dd = run(
        "ssm", lambda job: _ssm_bwd(dy, proj, res["xr"], res["xi"], ssm["ctre"], ssm["ctimn"], ssm["btre"], ssm["btim"],
                                    ssm["are"], ssm["aim"], ssm["d"], dproj, 256, f"{tag}_ssm", job=job),
        jobs.get("ssm"))
    _, disc_vjp = jax.vjp(_discretize, P["ssm_lambda_re"][layer], P["ssm_lambda_im"][layer], P["ssm_log_dt"][layer],
                          P["ssm_b_re"][layer], P["ssm_b_im"][layer])
    d_lre, d_lim, d_ldt, d_bre, d_bim = disc_vjp((dare.reshape(SSM_GROUPS, SSM_STATE), daim.reshape(SSM_GROUPS, SSM_STATE),
                                                  _diag_blocks(dbre, SSM_STATE, SSM_GROUP),
                                                  _diag_blocks(dbim, SSM_STATE, SSM_GROUP)))

    small = [gw[_SHARDED[i][0]] for i in _SMALL]
    for seg, pieces in (("q", dqs), ("k", dks), ("v", dvs), ("q_mem", [dqm]), ("z_mem", [dzm])):
        for j, piece in enumerate(pieces):
            dproj = lax.dynamic_update_slice(dproj, piece, (0, _OFF[seg] + j * piece.shape[1]))
    dw_in = run("dw_in", lambda job: _mm_tn(dproj, res["h"], 2176, 1024, min(T, 1024), f"{tag}_dw_in", job=job),
                jobs["dw_in"](small) if "dw_in" in jobs else None)
    dw_in = dw_in.reshape((N_DEV,) + _SHARDED[_W_IN][1])
    dx_in, dg = run("proj", lambda job: _proj_bwd(dproj, W["w_in"], res["x"], res["g"], dx, min(T, 1024), 2176,
                                                  f"{tag}_proj", job=job, w_turned=True),
                    jobs["proj"](small, dw_in, landed) if "proj" in jobs else None)

    gp = dict(norm_g=dg[0], mem_norm_g=dgm[0], b_gate=dbg[0], ssm_lambda_re=d_lre, ssm_lambda_im=d_lim,
              ssm_log_dt=d_ldt, ssm_b_re=d_bre, ssm_b_im=d_bim,
              ssm_c_re=_diag_blocks(dcre, SSM_GROUP, SSM_STATE), ssm_c_im=_diag_blocks(dcim, SSM_GROUP, SSM_STATE),
              ssm_d=dd[0], b_glu=dbglu[0])
    return dx_in, dw_in, gp, dbias, landed


def _train_step(x, mem, target, shards, P):
    rest0 = [(i, 0) for i in _SMALL]
    rows1 = [(i, 1) for i in _SMALL if _SHARDED[i][2] == 0]
    cols1 = [(i, 1) for i in _SMALL if _SHARDED[i][2] == 1]
    first = [(_W_IN, 0)]
    w_in0 = _gather_via_sibling(shards[_W_IN], lambda ref: ref.at[0], lambda ref, s: ref.at[s],
                                jax.ShapeDtypeStruct((N_DEV,) + _SHARDED[_W_IN][1], BF16), "gather_w_in0")
    W0 = _landed_weights(first, [w_in0])
    buckets = _bucket_tables()
    bias = _bias_tables(P["rel_bias"], buckets, "bias_tables")
    jobs0 = {"proj": (_gather_job(shards, rest0), rest0), "ssm": (_gather_job(shards, [(_W_IN, 1)]), [(_W_IN, 1)]),
             "attn0": (_gather_job(shards, rows1), rows1), "attn1": (_gather_job(shards, cols1), cols1)}
    x, res0, delivered = _layer_fwd(x, mem, W0, P, bias, 0, jobs0)
    W1 = {**delivered["ssm"], **delivered["attn0"], **delivered["attn1"]}
    x, res1, _ = _layer_fwd(x, mem, W1, P, bias, 1, {})
    loss, dx, dgf = _loss_head(x, P["final_norm_g"].reshape(1, D_MODEL), target, 512, "loss_head")

    dx, dw_in1, gp1, dbias1, landed1 = _layer_bwd(
        dx, res1, P, bias, 1, {"proj": lambda small, dw_in, landed: _scatter_job(small, _SMALL, 1)})
    rep1 = _pack_half({n: a[None] for n, a in gp1.items()}, 0, False)
    dx, _, gp0, dbias0, landed0 = _layer_bwd(
        dx, res0, P, bias, 0,
        {"merge": _rows_job(rep1, 0), "ssm": _scatter_job([dw_in1], [_W_IN], 1),
         "dw_in": lambda small: _scatter_job(small, _SMALL, 0, parts=landed1["proj"]),
         "proj": lambda small, dw_in, landed: _scatter_job([dw_in], [_W_IN], 0, parts=landed["ssm"])})
    d_rel = _bias_grad(dbias0, dbias1, buckets, "bias_grad")
    gp0 = {n: a[None] for n, a in gp0.items()}
    gp0["rel_bias"] = jnp.sum(d_rel, axis=-1).transpose(2, 0, 1).reshape(NUM_BUCKETS, 12)
    gp0["final_norm_g"] = dgf[0]
    rep0 = _pack_half(gp0, 0, True)
    rparts = _gather_via_sibling(rep0, lambda ref: ref, lambda ref, s: ref.at[s, pl.ds(_REP_HALF_ROWS, _REP_HALF_ROWS)],
                                 jax.ShapeDtypeStruct((N_DEV, _REP_ROWS, LANES), F32), "gather_small_grads0",
                                 landing=landed0["merge"][0])
    return loss[0, 0], dx, list(landed0["proj"]) + list(landed0["dw_in"]), rparts


_WEIGHTS = ["norm_g", "mem_norm_g", "w_in", "b_gate", "ssm_lambda_re", "ssm_lambda_im", "ssm_log_dt", "ssm_b_re",
            "ssm_b_im", "ssm_c_re", "ssm_c_im", "ssm_d", "w_glu", "b_glu", "w_mem_kv", "w_br_ssm", "w_br_attn",
            "w_br_mem", "w_out", "rel_bias", "final_norm_g"]
_ADAM_ROWS = {"w_in": 136,"w_glu": 96, "w_mem_kv": 128, "w_br_ssm": 768, "w_br_attn": 768, "w_br_mem": 512,
              "w_out": 128}


def kernel(x, mem, norm_g, mem_norm_g, w_in, b_gate, ssm_lambda_re, ssm_lambda_im, ssm_log_dt, ssm_b_re, ssm_b_im, ssm_c_re, ssm_c_im, ssm_d, w_glu, b_glu, w_mem_kv, w_br_ssm, w_br_attn, w_br_mem, w_out, rel_bias, final_norm_g, loss_target, m_norm_g, m_mem_norm_g, m_w_in, m_b_gate, m_ssm_lambda_re, m_ssm_lambda_im, m_ssm_log_dt, m_ssm_b_re, m_ssm_b_im, m_ssm_c_re, m_ssm_c_im, m_ssm_d, m_w_glu, m_b_glu, m_w_mem_kv, m_w_br_ssm, m_w_br_attn, m_w_br_mem, m_w_out, m_rel_bias, m_final_norm_g, v_norm_g, v_mem_norm_g, v_w_in, v_b_gate, v_ssm_lambda_re, v_ssm_lambda_im, v_ssm_log_dt, v_ssm_b_re, v_ssm_b_im, v_ssm_c_re, v_ssm_c_im, v_ssm_d, v_w_glu, v_b_glu, v_w_mem_kv, v_w_br_ssm, v_w_br_attn, v_w_br_mem, v_w_out, v_rel_bias, v_final_norm_g):
    given = dict(locals())
    w = {n: given[n] for n in _WEIGHTS}
    m = {n: given["m_" + n] for n in _WEIGHTS}
    v = {n: given["v_" + n] for n in _WEIGHTS}

    turned = lambda n, a: a.swapaxes(1, 2) if n == "w_in" else a
    shards = [turned(n, w[n]).astype(BF16) for n, _, _ in _SHARDED]
    loss, dx, parts, rparts = _train_step(x[0], mem[0], loss_target[0], shards, w)
    loss = lax.psum(loss, ("x", "y", "c"))

    new = {}
    for (n, _, _), p in zip(_SHARDED, parts):
        new[n] = [turned(n, a) for a in _adamw(p, turned(n, w[n]), turned(n, m[n]), turned(n, v[n]), _ADAM_ROWS[n],
                                               f"adamw_{n}")]
    rp = [_unpack_replicated(a) for a in _adamw(rparts[:, None], _pack_replicated(w), _pack_replicated(m),
                                                _pack_replicated(v), _REP_ROWS // 4, "adamw_replicated")]
    for n, _ in _REPLICATED:
        new[n] = [rp[kind][n] for kind in range(4)]
    outs = [loss, dx[None]]
    for kind in range(4):
        outs.extend(new[n][kind] for n in _WEIGHTS)
    return tuple(outs)
```

```python
import functools
import math
from typing import Callable, NamedTuple

import jax
import jax.numpy as jnp
import numpy as np
from jax import lax
from jax.experimental import pallas as pl
from jax.experimental.pallas import tpu as pltpu

F32 = jnp.float32
BF16 = jnp.bfloat16

D_MODEL = 1024
DEPTH = 2
EPS = 1e-6
D_SSM = 768
SSM_GROUP = 16
SSM_GROUPS = 48
SSM_STATE = 64
N_STATE = SSM_GROUPS * SSM_STATE
SSM_BLOCKS = 6
D_ATTN = 768
ATTN_HEAD_DIM = 64
ATTN_GROUP_WIDTH = 256
ATTN_DILATIONS = (1, 4, 16)
ATTN_SPAN = 128
ATTN_BLOCK = 128
NUM_BUCKETS = 32
REL_MAX_DISTANCE = 2048
NEG_INF = -1e30
MEM_HEADS = 4
MEM_HEAD_DIM = 128
D_MEM = 512
N_GATES = 3 * D_MODEL
D_IN = 8704
N_DEV = 8
LANES = 128
ADAM_LR = 0.001
ADAM_B1 = 0.9
ADAM_B2 = 0.999
ADAM_EPS = 1e-08
ADAM_WD = 0.01
ADAM_STEP = 10

_OFF = {"u": 0, "z_ssm": 768, "q": 1536, "k": 2304, "v": 3072, "z_attn": 3840, "q_mem": 4608, "z_mem": 5120,
        "gates": 5632}
GATE_BLOCK = 512

NN = (((1,), (0,)), ((), ()))
NT = (((1,), (1,)), ((), ()))
TN = (((0,), (0,)), ((), ()))

VMEM_LIMIT = 56 * 1024 * 1024


def _dot(a, b, dims=NN):
    return lax.dot_general(a, b, dims, preferred_element_type=F32)


def _sigmoid(x):
    return 1.0 / (1.0 + jnp.exp(-x))


def _gelu_parts(x):
    k = math.sqrt(2.0 / math.pi)
    t = jnp.tanh(k * (x + 0.044715 * (x * x * x)))
    cdf = 0.5 * (1.0 + t)
    dcdf = 0.5 * (1.0 - t * t) * k * (1.0 + 3.0 * 0.044715 * (x * x))
    return x * cdf, cdf + x * dcdf


def _params(sem, vmem=VMEM_LIMIT):
    return pltpu.CompilerParams(dimension_semantics=sem, vmem_limit_bytes=vmem)


def _full(shape):
    return pl.BlockSpec(shape, lambda *_: (0,) * len(shape))


def _norm_proj(x, g, w, tm, tn, name, out_dtype=F32, job=None, w_turned=False):
    T, D = x.shape
    N = w.shape[0] if w_turned else w.shape[1]
    w_spec = pl.BlockSpec((tn, D), lambda i, j: (j, 0)) if w_turned else pl.BlockSpec((D, tn), lambda i, j: (0, j))
    dims = NT if w_turned else NN

    def body(x_ref, g_ref, w_ref, o_ref, h_ref, hs):
        @pl.when(pl.program_id(1) == 0)
        def _():
            xv = x_ref[...]
            r = lax.rsqrt(jnp.mean(xv * xv, axis=-1, keepdims=True) + EPS)
            hv = (xv * r * g_ref[...]).astype(BF16)
            hs[...] = hv
            h_ref[...] = hv

        o_ref[...] = _dot(hs[...], w_ref[...], dims).astype(out_dtype)

    return _pc(
        body, job, name=name, grid=(T // tm, N // tn),
        in_specs=[pl.BlockSpec((tm, D), lambda i, j: (i, 0)), _full((1, D)), w_spec],
        out_specs=[pl.BlockSpec((tm, tn), lambda i, j: (i, j)), pl.BlockSpec((tm, D), lambda i, j: (i, 0))],
        out_shape=[jax.ShapeDtypeStruct((T, N), out_dtype), jax.ShapeDtypeStruct((T, D), BF16)],
        scratch_shapes=[pltpu.VMEM((tm, D), BF16)], sem=("parallel", "arbitrary"), operands=(x, g, w))


def _mm_tn(a, b, tm, tn, tk, name, b_col=0, n=None, job=None):
    K, M = a.shape
    N = b.shape[1] if n is None else n
    nk = K // tk
    j0 = b_col // tn

    def body(a_ref, b_ref, o_ref, acc):
        k = pl.program_id(2)

        @pl.when(k == 0)
        def _():
            acc[...] = jnp.zeros_like(acc)

        acc[...] += _dot(a_ref[...].astype(BF16), b_ref[...].astype(BF16), TN)

        @pl.when(k == nk - 1)
        def _():
            o_ref[...] = acc[...].astype(BF16)

    out = _pc(
        body, job, name=name, grid=(M // tm, N // tn, nk),
        in_specs=[pl.BlockSpec((tk, tm), lambda i, j, k: (k, i)), pl.BlockSpec((tk, tn), lambda i, j, k: (k, j0 + j))],
        out_specs=[pl.BlockSpec((tm, tn), lambda i, j, k: (i, j))],
        out_shape=[jax.ShapeDtypeStruct((M, N), BF16)],
        scratch_shapes=[pltpu.VMEM((tm, tn), F32)], sem=("parallel", "parallel", "arbitrary"), operands=(a, b))
    return out[0] if job is None else (out[0][0], out[1])


def _proj_bwd(dp, w, x, g, dres, tm, tk, name, job=None, w_turned=False):
    T, N = dp.shape
    D = x.shape[1]
    nk = N // tk
    w_spec = pl.BlockSpec((tk, D), lambda i, k: (k, 0)) if w_turned else pl.BlockSpec((D, tk), lambda i, k: (0, k))
    dims = NN if w_turned else NT

    def body(dp_ref, w_ref, x_ref, g_ref, dres_ref, dx_ref, dg_ref, acc):
        i, k = pl.program_id(0), pl.program_id(1)

        @pl.when(k == 0)
        def _():
            acc[...] = jnp.zeros_like(acc)

        @pl.when((i == 0) & (k == 0))
        def _():
            dg_ref[...] = jnp.zeros_like(dg_ref)

        acc[...] += _dot(dp_ref[...], w_ref[...], dims)

        @pl.when(k == nk - 1)
        def _():
            xv = x_ref[...]
            dh = acc[...]
            r = lax.rsqrt(jnp.mean(xv * xv, axis=-1, keepdims=True) + EPS)
            xr = xv * r
            dg_ref[...] += jnp.sum(dh * xr, axis=0, keepdims=True)
            wv = dh * g_ref[...]
            dx_ref[...] = dres_ref[...] + r * (wv - xr * jnp.mean(wv * xr, axis=-1, keepdims=True))

    return _pc(
        body, job, name=name, grid=(T // tm, nk),
        in_specs=[pl.BlockSpec((tm, tk), lambda i, k: (i, k)), w_spec,
                  pl.BlockSpec((tm, D), lambda i, k: (i, 0)), _full((1, D)),
                  pl.BlockSpec((tm, D), lambda i, k: (i, 0))],
        out_specs=[pl.BlockSpec((tm, D), lambda i, k: (i, 0)), _full((1, D))],
        out_shape=[jax.ShapeDtypeStruct((T, D), F32), jax.ShapeDtypeStruct((1, D), F32)],
        scratch_shapes=[pltpu.VMEM((tm, D), F32)], sem=("arbitrary", "arbitrary"), operands=(dp, w, x, g, dres))


def _ssm_fwd(proj, bre, bim, cre, cimn, are, aim, d, wglu, bglu, tc, name, job=None):
    T = proj.shape[0]
    ucol, zcol = _OFF["u"] // D_SSM, _OFF["z_ssm"] // D_SSM

    def body(u_ref, z_ref, bre_ref, bim_ref, cre_ref, cim_ref, are_ref, aim_ref, d_ref, wg_ref, bg_ref,
             xr_ref, xi_ref, y_ref, o_ref, car_r, car_i):
        @pl.when(pl.program_id(0) == 0)
        def _():
            car_r[...] = jnp.zeros_like(car_r)
            car_i[...] = jnp.zeros_like(car_i)

        u = u_ref[...]
        ub = u.astype(BF16)
        for k in range(SSM_BLOCKS):
            uk = ub[:, 128 * k:128 * (k + 1)]
            xr_ref[:, 512 * k:512 * (k + 1)] = _dot(uk, bre_ref[k])
            xi_ref[:, 512 * k:512 * (k + 1)] = _dot(uk, bim_ref[k])
        ar, ai = are_ref[...], aim_ref[...]

        def step(t, c):
            pr, pi = c
            nr = ar * pr - ai * pi + xr_ref[pl.ds(t, 1), :]
            ni = ar * pi + ai * pr + xi_ref[pl.ds(t, 1), :]
            xr_ref[pl.ds(t, 1), :] = nr
            xi_ref[pl.ds(t, 1), :] = ni
            return nr, ni

        pr, pi = lax.fori_loop(0, tc, step, (car_r[...], car_i[...]))
        car_r[...] = pr
        car_i[...] = pi

        ys = []
        for k in range(SSM_BLOCKS):
            xrk = xr_ref[:, 512 * k:512 * (k + 1)].astype(BF16)
            xik = xi_ref[:, 512 * k:512 * (k + 1)].astype(BF16)
            ys.append(_dot(xrk, cre_ref[k]) + _dot(xik, cim_ref[k]))
        y = jnp.concatenate(ys, axis=1) + d_ref[...] * u
        y_ref[...] = y
        gl, _ = _gelu_parts(y)
        t = _dot(gl.astype(BF16), wg_ref[...]) + bg_ref[...]
        z = z_ref[...]
        o_ref[...] = (gl * _sigmoid(t) * (z * _sigmoid(z))).astype(BF16)

    return _pc(
        body, job, name=name, grid=(T // tc,),
        in_specs=[pl.BlockSpec((tc, D_SSM), lambda i: (i, ucol)), pl.BlockSpec((tc, D_SSM), lambda i: (i, zcol)),
                  _full((SSM_BLOCKS, 128, 512)), _full((SSM_BLOCKS, 128, 512)),
                  _full((SSM_BLOCKS, 512, 128)), _full((SSM_BLOCKS, 512, 128)),
                  _full((1, N_STATE)), _full((1, N_STATE)), _full((1, D_SSM)),
                  _full((D_SSM, D_SSM)), _full((1, D_SSM))],
        out_specs=[pl.BlockSpec((tc, N_STATE), lambda i: (i, 0)), pl.BlockSpec((tc, N_STATE), lambda i: (i, 0)),
                   pl.BlockSpec((tc, D_SSM), lambda i: (i, 0)), pl.BlockSpec((tc, D_SSM), lambda i: (i, 0))],
        out_shape=[jax.ShapeDtypeStruct((T, N_STATE), F32), jax.ShapeDtypeStruct((T, N_STATE), F32),
                   jax.ShapeDtypeStruct((T, D_SSM), F32), jax.ShapeDtypeStruct((T, D_SSM), BF16)],
        scratch_shapes=[pltpu.VMEM((1, N_STATE), F32), pltpu.VMEM((1, N_STATE), F32)], sem=("arbitrary",),
        operands=(proj, proj, bre, bim, cre, cimn, are, aim, d, wglu, bglu))


def _glu_bwd(do, y, proj, wglu, bglu, dproj, tm, name):
    T = y.shape[0]
    zcol = _OFF["z_ssm"] // D_SSM

    def body(do_ref, y_ref, z_ref, wg_ref, bg_ref, _, dy_ref, dz_ref, g_ref, dt_ref, db_ref):
        @pl.when(pl.program_id(0) == 0)
        def _():
            db_ref[...] = jnp.zeros_like(db_ref)

        dov = do_ref[...]
        gl, dgl = _gelu_parts(y_ref[...])
        glb = gl.astype(BF16)
        sg = _sigmoid(_dot(glb, wg_ref[...]) + bg_ref[...])
        z = z_ref[...]
        sz = _sigmoid(z)
        dz_ref[...] = (dov * (gl * sg) * (sz * (1.0 + z * (1.0 - sz)))).astype(BF16)
        dy2 = dov * (z * sz)
        dt = dy2 * gl * (sg * (1.0 - sg))
        dtb = dt.astype(BF16)
        dg = dy2 * sg + _dot(dtb, wg_ref[...], NT)
        dy_ref[...] = dg * dgl
        g_ref[...] = glb
        dt_ref[...] = dtb
        db_ref[...] += jnp.sum(dt, axis=0, keepdims=True)

    row = lambda i: (i, 0)
    return pl.pallas_call(
        body, name=name, grid=(T // tm,),
        in_specs=[pl.BlockSpec((tm, D_SSM), row), pl.BlockSpec((tm, D_SSM), row),
                  pl.BlockSpec((tm, D_SSM), lambda i: (i, zcol)), _full((D_SSM, D_SSM)), _full((1, D_SSM)),
                  pl.BlockSpec(memory_space=pl.ANY)],
        out_specs=[pl.BlockSpec((tm, D_SSM), row), pl.BlockSpec((tm, D_SSM), lambda i: (i, zcol)),
                   pl.BlockSpec((tm, D_SSM), row), pl.BlockSpec((tm, D_SSM), row), _full((1, D_SSM))],
        out_shape=[jax.ShapeDtypeStruct((T, D_SSM), F32), jax.ShapeDtypeStruct(dproj.shape, BF16),
                   jax.ShapeDtypeStruct((T, D_SSM), BF16), jax.ShapeDtypeStruct((T, D_SSM), BF16),
                   jax.ShapeDtypeStruct((1, D_SSM), F32)],
        input_output_aliases={5: 1},
        compiler_params=_params(("arbitrary",)),
    )(do, y, proj, wglu, bglu, dproj)


def _ssm_bwd(dy, proj, xr, xi, ctre, ctimn, btre, btim, are, aim, d, dproj, tc, name, job=None):
    T = dy.shape[0]
    nc = T // tc
    ucol = _OFF["u"] // D_SSM
    rb = tc // 8

    def body(dy_ref, u_ref, xr_ref, xi_ref, xpr_ref, xpi_ref, ctre_ref, ctim_ref, btre_ref, btim_ref,
             are_ref, aim_ref, d_ref, _,
             du_ref, dbre_ref, dbim_ref, dcre_ref, dcim_ref, dare_ref, daim_ref, dd_ref, gr, gi, car_r, car_i):
        i = pl.program_id(0)

        @pl.when(i == 0)
        def _():
            for ref in (car_r, car_i, dbre_ref, dbim_ref, dcre_ref, dcim_ref, dare_ref, daim_ref, dd_ref):
                ref[...] = jnp.zeros_like(ref)

        dyv = dy_ref[...]
        dyb = dyv.astype(BF16)
        u = u_ref[...]
        ub = u.astype(BF16)
        for k in range(SSM_BLOCKS):
            dk = dyb[:, 128 * k:128 * (k + 1)]
            gr[:, 512 * k:512 * (k + 1)] = _dot(dk, ctre_ref[k])
            gi[:, 512 * k:512 * (k + 1)] = _dot(dk, ctim_ref[k])
        ar, ai = are_ref[...], aim_ref[...]

        def step(s, c):
            pr, pi = c
            t = tc - 1 - s
            nr = gr[pl.ds(t, 1), :] + ar * pr + ai * pi
            ni = gi[pl.ds(t, 1), :] + ar * pi - ai * pr
            gr[pl.ds(t, 1), :] = nr
            gi[pl.ds(t, 1), :] = ni
            return nr, ni

        pr, pi = lax.fori_loop(0, tc, step, (car_r[...], car_i[...]))
        car_r[...] = pr
        car_i[...] = pi

        xrv, xiv = xr_ref[...], xi_ref[...]
        keep = jnp.where(i == nc - 1, 0.0, 1.0)
        row0 = lax.broadcasted_iota(jnp.int32, (tc, 1), 0) == 0
        xsr = jnp.where(row0, xpr_ref[7:8, :] * keep, pltpu.roll(xrv, 1, axis=0))
        xsi = jnp.where(row0, xpi_ref[7:8, :] * keep, pltpu.roll(xiv, 1, axis=0))
        grv, giv = gr[...], gi[...]
        dare_ref[...] += jnp.sum(grv * xsr + giv * xsi, axis=0, keepdims=True)
        daim_ref[...] += jnp.sum(giv * xsr - grv * xsi, axis=0, keepdims=True)
        dd_ref[...] += jnp.sum(dyv * u, axis=0, keepdims=True)

        dus = []
        for k in range(SSM_BLOCKS):
            sl = slice(512 * k, 512 * (k + 1))
            ch = slice(128 * k, 128 * (k + 1))
            grb, gib = grv[:, sl].astype(BF16), giv[:, sl].astype(BF16)
            dus.append(_dot(grb, btre_ref[k]) + _dot(gib, btim_ref[k]))
            dbre_ref[k] += _dot(grb, ub[:, ch], TN)
            dbim_ref[k] += _dot(gib, ub[:, ch], TN)
            dcre_ref[k] += _dot(dyb[:, ch], xrv[:, sl].astype(BF16), TN)
            dcim_ref[k] -= _dot(dyb[:, ch], xiv[:, sl].astype(BF16), TN)
        du_ref[...] = (jnp.concatenate(dus, axis=1) + d_ref[...] * dyv).astype(BF16)

    rev = lambda i: (nc - 1 - i, 0)
    prev = lambda i: (jnp.maximum((nc - 1 - i) * rb - 1, 0), 0)
    return _pc(
        body, job, name=name, grid=(nc,),
        in_specs=[pl.BlockSpec((tc, D_SSM), rev), pl.BlockSpec((tc, D_SSM), lambda i: (nc - 1 - i, ucol)),
                  pl.BlockSpec((tc, N_STATE), rev), pl.BlockSpec((tc, N_STATE), rev),
                  pl.BlockSpec((8, N_STATE), prev), pl.BlockSpec((8, N_STATE), prev),
                  _full((SSM_BLOCKS, 128, 512)), _full((SSM_BLOCKS, 128, 512)),
                  _full((SSM_BLOCKS, 512, 128)), _full((SSM_BLOCKS, 512, 128)),
                  _full((1, N_STATE)), _full((1, N_STATE)), _full((1, D_SSM)), pl.BlockSpec(memory_space=pl.ANY)],
        out_specs=[pl.BlockSpec((tc, D_SSM), lambda i: (nc - 1 - i, ucol)),
                   _full((SSM_BLOCKS, 512, 128)), _full((SSM_BLOCKS, 512, 128)),
                   _full((SSM_BLOCKS, 128, 512)), _full((SSM_BLOCKS, 128, 512)),
                   _full((1, N_STATE)), _full((1, N_STATE)), _full((1, D_SSM))],
        out_shape=[jax.ShapeDtypeStruct(dproj.shape, BF16),
                   jax.ShapeDtypeStruct((SSM_BLOCKS, 512, 128), F32), jax.ShapeDtypeStruct((SSM_BLOCKS, 512, 128), F32),
                   jax.ShapeDtypeStruct((SSM_BLOCKS, 128, 512), F32), jax.ShapeDtypeStruct((SSM_BLOCKS, 128, 512), F32),
                   jax.ShapeDtypeStruct((1, N_STATE), F32), jax.ShapeDtypeStruct((1, N_STATE), F32),
                   jax.ShapeDtypeStruct((1, D_SSM), F32)],
        scratch_shapes=[pltpu.VMEM((tc, N_STATE), F32), pltpu.VMEM((tc, N_STATE), F32),
                        pltpu.VMEM((1, N_STATE), F32), pltpu.VMEM((1, N_STATE), F32)], sem=("arbitrary",),
        operands=(dy, proj, xr, xi, xr, xi, ctre, ctimn, btre, btim, are, aim, d, dproj), aliases={13: 0})


def _rel_bucket(dist):
    n = jnp.maximum(dist, 0)
    max_exact = NUM_BUCKETS // 2
    n_f = jnp.maximum(n, 1).astype(F32)
    large = max_exact + (jnp.log(n_f / max_exact) / math.log(REL_MAX_DISTANCE / max_exact)
                         * (NUM_BUCKETS - max_exact)).astype(jnp.int32)
    large = jnp.minimum(large, NUM_BUCKETS - 1)
    return jnp.where(n < max_exact, n, large)


def _bucket_tables():
    qi = jnp.arange(ATTN_BLOCK)[:, None]
    kj = jnp.arange(2 * ATTN_BLOCK)[None, :]
    delta = jnp.maximum(ATTN_BLOCK + qi - kj, 0)
    return jnp.stack([_rel_bucket(delta * r) for r in ATTN_DILATIONS]).astype(jnp.int32)


def _bias_tables(rel_bias, buckets, name):
    def body(tab_ref, bk_ref, o_ref):
        g = pl.program_id(0)
        bk = bk_ref[...]
        qi = lax.broadcasted_iota(jnp.int32, bk.shape, 0)
        kj = lax.broadcasted_iota(jnp.int32, bk.shape, 1)
        delta = ATTN_BLOCK + qi - kj
        band = (delta >= 0) & (delta <= ATTN_SPAN)
        accs = [jnp.zeros(bk.shape, F32) for _ in range(4)]
        for b in range(NUM_BUCKETS):
            hit = bk == b
            for h in range(4):
                accs[h] = jnp.where(hit, tab_ref[b, 4 * g + h], accs[h])
        for h in range(4):
            o_ref[h] = jnp.where(band, accs[h], NEG_INF)

    return pl.pallas_call(
        body, name=name, grid=(3,),
        in_specs=[pl.BlockSpec(memory_space=pltpu.SMEM),
                  pl.BlockSpec((None, ATTN_BLOCK, 2 * ATTN_BLOCK), lambda g: (g, 0, 0))],
        out_specs=pl.BlockSpec((None, 4, ATTN_BLOCK, 2 * ATTN_BLOCK), lambda g: (g, 0, 0, 0)),
        out_shape=jax.ShapeDtypeStruct((3, 4, ATTN_BLOCK, 2 * ATTN_BLOCK), F32),
        compiler_params=_params(("parallel",)),
    )(rel_bias, buckets)


def _bias_grad(db0, db1, buckets, name):
    def body(a_ref, b_ref, bk_ref, o_ref):
        bk = bk_ref[...]
        for h in range(4):
            dv = a_ref[h] + b_ref[h]
            for b in range(NUM_BUCKETS):
                o_ref[h, b:b + 1, :] = jnp.sum(jnp.where(bk == b, dv, 0.0), axis=0, keepdims=True)

    tab = pl.BlockSpec((None, 4, ATTN_BLOCK, 2 * ATTN_BLOCK), lambda g: (g, 0, 0, 0))
    return pl.pallas_call(
        body, name=name, grid=(3,),
        in_specs=[tab, tab, pl.BlockSpec((None, ATTN_BLOCK, 2 * ATTN_BLOCK), lambda g: (g, 0, 0))],
        out_specs=pl.BlockSpec((None, 4, NUM_BUCKETS, 2 * ATTN_BLOCK), lambda g: (g, 0, 0, 0)),
        out_shape=jax.ShapeDtypeStruct((3, 4, NUM_BUCKETS, 2 * ATTN_BLOCK), F32),
        compiler_params=_params(("parallel",)),
    )(db0, db1, buckets)


_ATTN_SUB = {1: 4, 4: 1, 16: 1}
_UNROLL = 4


def _unit_rows(j, s, r):
    start = j * ATTN_BLOCK * r + s
    return pl.ds(start, ATTN_BLOCK, stride=r) if r > 1 else pl.ds(start, ATTN_BLOCK)


def _for_units(r, nsub, fn, after):
    if r * nsub <= _UNROLL:
        after([fn(j, s) for j in range(nsub) for s in range(r)])
    else:
        def four(i, c):
            after([fn(0, _UNROLL * i + k) for k in range(_UNROLL)])
            return c

        lax.fori_loop(0, r // _UNROLL, four, 0)


def _attn_cols(g):
    return tuple((_OFF[n] + ATTN_GROUP_WIDTH * g) // LANES for n in ("q", "k", "v"))


def _attn_fwd(proj, bias, g, name, job=None):
    r = ATTN_DILATIONS[g]
    nsub = _ATTN_SUB[r]
    T = proj.shape[0]
    sub = ATTN_BLOCK * r
    tb = sub * nsub
    qc, kc, vc = _attn_cols(g)
    scale = ATTN_HEAD_DIM ** -0.5

    def body(q_ref, kc_ref, kp_ref, vc_ref, vp_ref, bias_ref, o_ref, lse_ref):
        lane = lax.broadcasted_iota(jnp.int32, (ATTN_BLOCK, LANES), 1)
        kj = lax.broadcasted_iota(jnp.int32, (ATTN_BLOCK, 2 * ATTN_BLOCK), 1)
        dead = (pl.program_id(0) == 0) & (kj < ATTN_BLOCK)

        def one(j, s):
            rows = _unit_rows(j, s, r)
            before = _unit_rows(max(j - 1, 0), s, r)
            k_before = kc_ref[before, :] if j else kp_ref[before, :]
            v_before = vc_ref[before, :] if j else vp_ref[before, :]
            q = q_ref[rows, :]
            kcat = jnp.concatenate([k_before, kc_ref[rows, :]], axis=0).astype(BF16)
            vcat = jnp.concatenate([v_before, vc_ref[rows, :]], axis=0).astype(BF16)
            o_acc = jnp.zeros((ATTN_BLOCK, LANES), F32)
            l_acc = jnp.zeros((ATTN_BLOCK, LANES), F32)
            for hh in range(2):
                mine = (lane >= ATTN_HEAD_DIM) if hh else (lane < ATTN_HEAD_DIM)
                qm = jnp.where(mine, q, 0.0).astype(BF16)
                sc = _dot(qm, kcat, NT) * scale + bias_ref[hh]
                if j == 0:
                    sc = jnp.where(dead, NEG_INF, sc)
                m = jnp.max(sc, axis=-1, keepdims=True)
                p = jnp.exp(sc - m)
                l = jnp.sum(p, axis=-1, keepdims=True)
                o_acc = jnp.where(mine, _dot((p / l).astype(BF16), vcat), o_acc)
                l_acc = jnp.where(mine, m + jnp.log(l), l_acc)
            o_ref[rows, :] = o_acc
            lse_ref[rows, :] = l_acc

        _for_units(r, nsub, one, lambda results: None)

    cur = lambda c: pl.BlockSpec((tb, LANES), lambda b, p: (b, c + p))
    prev = lambda c: pl.BlockSpec((sub, LANES), lambda b, p: (jnp.maximum(b * nsub - 1, 0), c + p))
    out = pl.BlockSpec((tb, LANES), lambda b, p: (b, p))
    return _pc(
        body, job, name=name, grid=(T // tb, 2),
        in_specs=[cur(qc), cur(kc), prev(kc), cur(vc), prev(vc),
                  pl.BlockSpec((2, ATTN_BLOCK, 2 * ATTN_BLOCK), lambda b, p: (p, 0, 0))],
        out_specs=[out, out],
        out_shape=[jax.ShapeDtypeStruct((T, ATTN_GROUP_WIDTH), F32), jax.ShapeDtypeStruct((T, ATTN_GROUP_WIDTH), F32)],
        scratch_shapes=[], sem=("parallel", "parallel"), operands=(proj, proj, proj, proj, proj, bias))


def _attn_bwd(proj, do, corr, lse, bias, dproj, g, name):
    r = ATTN_DILATIONS[g]
    nsub = _ATTN_SUB[r]
    T = proj.shape[0]
    sub = ATTN_BLOCK * r
    tb = sub * nsub
    nb = T // tb
    qc, kc, vc = _attn_cols(g)
    dc = ATTN_GROUP_WIDTH * g // LANES
    scale = ATTN_HEAD_DIM ** -0.5

    def body(q_ref, kc_ref, kp_ref, vc_ref, vp_ref, do_ref, corr_ref, lse_ref, bias_ref, _,
             dproj_ref, db_ref, dq_s, dkc_s, dkp_s, dvc_s, dvp_s, kacc, vacc, stage, stage_sems):
        p, b = pl.program_id(0), pl.program_id(1)

        def to_dproj(e, slot, block, col):
            rows = pl.ds(pl.multiple_of(block * tb, tb), tb)
            cols = pl.ds(pl.multiple_of((col + p) * LANES, LANES), LANES)
            return pltpu.make_async_copy(stage.at[e, slot], dproj_ref.at[rows, cols], stage_sems.at[e, slot])

        def emit(e, block, col, value):
            count = p * nb + block
            slot = count % 2

            @pl.when(count >= 2)
            def _():
                to_dproj(e, slot, 0, col).wait()

            stage[e, slot] = value.astype(BF16)
            to_dproj(e, slot, block, col).start()

        def emit_keys(block):
            emit(1, block, kc, kacc[...])
            emit(2, block, vc, vacc[...])

        @pl.when(b == 0)
        def _():
            db_ref[...] = jnp.zeros_like(db_ref)
            kacc[...] = jnp.zeros_like(kacc)
            vacc[...] = jnp.zeros_like(vacc)

        @pl.when(b == nb)
        def _():
            emit_keys(nb - 1)

        @pl.when((b == nb) & (p == 1))
        def _():
            for e, col in enumerate((qc, kc, vc)):
                for slot in range(2):
                    to_dproj(e, slot, 0, col).wait()

        @pl.when(b < nb)
        def _():
            lane = lax.broadcasted_iota(jnp.int32, (ATTN_BLOCK, LANES), 1)
            kj = lax.broadcasted_iota(jnp.int32, (ATTN_BLOCK, 2 * ATTN_BLOCK), 1)
            dead = (b == 0) & (kj < ATTN_BLOCK)

            def one(j, s):
                rows = _unit_rows(j, s, r)
                before = _unit_rows(max(j - 1, 0), s, r)
                k_before = kc_ref[before, :] if j else kp_ref[before, :]
                v_before = vc_ref[before, :] if j else vp_ref[before, :]
                q = q_ref[rows, :]
                kcat = jnp.concatenate([k_before, kc_ref[rows, :]], axis=0).astype(BF16)
                vcat = jnp.concatenate([v_before, vc_ref[rows, :]], axis=0).astype(BF16)
                dov, corrv, lsev = do_ref[rows, :], corr_ref[rows, :], lse_ref[rows, :]
                dq_acc = jnp.zeros((ATTN_BLOCK, LANES), F32)
                dk_acc = jnp.zeros((2 * ATTN_BLOCK, LANES), F32)
                dv_acc = jnp.zeros((2 * ATTN_BLOCK, LANES), F32)
                dss = []
                for hh in range(2):
                    mine = (lane >= ATTN_HEAD_DIM) if hh else (lane < ATTN_HEAD_DIM)
                    col = slice(ATTN_HEAD_DIM * hh, ATTN_HEAD_DIM * hh + 1)
                    qm = jnp.where(mine, q, 0.0).astype(BF16)
                    dom = jnp.where(mine, dov, 0.0).astype(BF16)
                    sc = _dot(qm, kcat, NT) * scale + bias_ref[hh]
                    if j == 0:
                        sc = jnp.where(dead, NEG_INF, sc)
                    p = jnp.exp(sc - lsev[:, col])
                    ds = p * (_dot(dom, vcat, NT) - corrv[:, col])
                    dss.append(ds)
                    dsb = ds.astype(BF16)
                    dq_acc = jnp.where(mine, _dot(dsb, kcat) * scale, dq_acc)
                    dk_acc += _dot(dsb, qm, TN) * scale
                    dv_acc += _dot(p.astype(BF16), dom, TN)
                dq_s[rows, :] = dq_acc
                dkp_s[rows, :] = dk_acc[:ATTN_BLOCK]
                dkc_s[rows, :] = dk_acc[ATTN_BLOCK:]
                dvp_s[rows, :] = dv_acc[:ATTN_BLOCK]
                dvc_s[rows, :] = dv_acc[ATTN_BLOCK:]
                return dss

            def add_bias_grads(results):
                for hh in range(2):
                    db_ref[hh] += functools.reduce(lambda x, y: x + y, [dss[hh] for dss in results])

            _for_units(r, nsub, one, add_bias_grads)
            emit(0, b, qc, dq_s[...])
            tail = slice((nsub - 1) * sub, nsub * sub)
            kacc[tail, :] += dkp_s[0:sub, :]
            vacc[tail, :] += dvp_s[0:sub, :]

            @pl.when(b >= 1)
            def _():
                emit_keys(b - 1)

            for acc, before_s, cur_s in ((kacc, dkp_s, dkc_s), (vacc, dvp_s, dvc_s)):
                acc[...] = cur_s[...]
                for j in range(nsub - 1):
                    acc[j * sub:(j + 1) * sub, :] += before_s[(j + 1) * sub:(j + 2) * sub, :]

    last = nb - 1
    blk = (tb, LANES)
    cur = lambda c: pl.BlockSpec(blk, lambda p, b: (jnp.minimum(b, last), c + p))
    before = lambda c: pl.BlockSpec((sub, LANES), lambda p, b: (jnp.clip(b * nsub - 1, 0, nb * nsub - 1), c + p))
    tab = pl.BlockSpec((2, ATTN_BLOCK, 2 * ATTN_BLOCK), lambda p, b: (p, 0, 0))
    hbm = pl.BlockSpec(memory_space=pl.ANY)
    return pl.pallas_call(
        body, name=name, grid=(2, nb + 1),
        in_specs=[cur(qc), cur(kc), before(kc), cur(vc), before(vc), cur(dc), cur(dc), cur(0), tab, hbm],
        out_specs=[hbm, tab],
        out_shape=[jax.ShapeDtypeStruct(dproj.shape, BF16), jax.ShapeDtypeStruct((4, ATTN_BLOCK, 2 * ATTN_BLOCK), F32)],
        input_output_aliases={9: 0},
        scratch_shapes=[pltpu.VMEM(blk, F32)] * 7 + [pltpu.VMEM((3, 2) + blk, BF16), pltpu.SemaphoreType.DMA((3, 2))],
        compiler_params=_params(("arbitrary", "arbitrary")),
    )(proj, proj, proj, proj, proj, do, corr, lse, bias, dproj)


def _mix_weights(lses):
    m = jnp.maximum(jnp.maximum(lses[0], lses[1]), lses[2])
    es = [jnp.exp(l - m) for l in lses]
    inv = 1.0 / (es[0] + es[1] + es[2])
    return jnp.concatenate([e * inv for e in es], axis=1)


def _attn_mix(os, lses, proj, tm, name):
    T = proj.shape[0]
    zcol = _OFF["z_attn"] // D_ATTN

    def body(o0, o1, o2, l0, l1, l2, z_ref, out_ref):
        z = z_ref[...]
        o = jnp.concatenate([o0[...], o1[...], o2[...]], axis=1)
        alpha = _mix_weights([l0[...], l1[...], l2[...]])
        out_ref[...] = (o * alpha * (z * _sigmoid(z))).astype(BF16)

    row = lambda i: (i, 0)
    grp = pl.BlockSpec((tm, ATTN_GROUP_WIDTH), row)
    return pl.pallas_call(
        body, name=name, grid=(T // tm,),
        in_specs=[grp] * 6 + [pl.BlockSpec((tm, D_ATTN), lambda i: (i, zcol))],
        out_specs=pl.BlockSpec((tm, D_ATTN), row),
        out_shape=jax.ShapeDtypeStruct((T, D_ATTN), BF16),
        compiler_params=_params(("parallel",)),
    )(*os, *lses, proj)


def _attn_mix_bwd(d, os, lses, proj, dproj, tm, name):
    T = proj.shape[0]
    zcol = _OFF["z_attn"] // D_ATTN

    def body(d_ref, o0, o1, o2, l0, l1, l2, z_ref, _, do_ref, corr_ref, dz_ref):
        dv, z = d_ref[...], z_ref[...]
        ov = jnp.concatenate([o0[...], o1[...], o2[...]], axis=1)
        alpha = _mix_weights([l0[...], l1[...], l2[...]])
        sz = _sigmoid(z)
        oc = ov * alpha
        dz_ref[...] = (dv * oc * (sz * (1.0 + z * (1.0 - sz)))).astype(BF16)
        doc = dv * (z * sz)
        do_ref[...] = doc * alpha
        pr = doc * oc
        p3 = pr[:, 0:256] + pr[:, 256:512] + pr[:, 512:768]
        li = lax.broadcasted_iota(jnp.int32, (256, 256), 0) // ATTN_HEAD_DIM
        lj = lax.broadcasted_iota(jnp.int32, (256, 256), 1) // ATTN_HEAD_DIM
        ones = jnp.where(li == lj, 1.0, 0.0).astype(F32)
        s = lax.dot_general(p3, ones, NN, precision=lax.Precision.HIGHEST, preferred_element_type=F32)
        corr_ref[...] = alpha * jnp.concatenate([s, s, s], axis=1)

    row = lambda i: (i, 0)
    grp = pl.BlockSpec((tm, ATTN_GROUP_WIDTH), row)
    return pl.pallas_call(
        body, name=name, grid=(T // tm,),
        in_specs=[pl.BlockSpec((tm, D_ATTN), row)] + [grp] * 6 + [pl.BlockSpec((tm, D_ATTN), lambda i: (i, zcol)),
                                                                    pl.BlockSpec(memory_space=pl.ANY)],
        out_specs=[pl.BlockSpec((tm, D_ATTN), row)] * 2 + [pl.BlockSpec((tm, D_ATTN), lambda i: (i, zcol))],
        out_shape=[jax.ShapeDtypeStruct((T, D_ATTN), F32), jax.ShapeDtypeStruct((T, D_ATTN), F32),
                   jax.ShapeDtypeStruct(dproj.shape, BF16)],
        input_output_aliases={8: 2},
        compiler_params=_params(("parallel",)),
    )(d, *os, *lses, proj, dproj)


def _mem_probs(q_ref, kv_ref, h):
    hs = slice(MEM_HEAD_DIM * h, MEM_HEAD_DIM * (h + 1))
    qh = q_ref[:, hs].astype(BF16)
    kh = kv_ref[:, hs]
    vh = kv_ref[:, D_MEM + MEM_HEAD_DIM * h:D_MEM + MEM_HEAD_DIM * (h + 1)]
    s = _dot(qh, kh, NT) * (MEM_HEAD_DIM ** -0.5)
    p = jnp.exp(s - jnp.max(s, axis=-1, keepdims=True))
    pn = p / jnp.sum(p, axis=-1, keepdims=True)
    return qh, kh, vh, pn


def _mem_fwd(proj, kv, tm, name):
    T = proj.shape[0]
    M = kv.shape[0]
    qcol, zcol = _OFF["q_mem"] // D_MEM, _OFF["z_mem"] // D_MEM

    def body(q_ref, z_ref, kv_ref, o_ref):
        outs = []
        for h in range(MEM_HEADS):
            _, _, vh, pn = _mem_probs(q_ref, kv_ref, h)
            outs.append(_dot(pn.astype(BF16), vh))
        z = z_ref[...]
        o_ref[...] = (jnp.concatenate(outs, axis=1) * (z * _sigmoid(z))).astype(BF16)

    return pl.pallas_call(
        body, name=name, grid=(T // tm,),
        in_specs=[pl.BlockSpec((tm, D_MEM), lambda i: (i, qcol)), pl.BlockSpec((tm, D_MEM), lambda i: (i, zcol)),
                  _full((M, 2 * D_MEM))],
        out_specs=pl.BlockSpec((tm, D_MEM), lambda i: (i, 0)),
        out_shape=jax.ShapeDtypeStruct((T, D_MEM), BF16),
        compiler_params=_params(("parallel",)),
    )(proj, proj, kv)


def _mem_bwd(d, proj, kv, tm, name):
    T = proj.shape[0]
    M = kv.shape[0]
    qcol, zcol = _OFF["q_mem"] // D_MEM, _OFF["z_mem"] // D_MEM

    def body(d_ref, q_ref, z_ref, kv_ref, dq_ref, dz_ref, dkv_ref):
        @pl.when(pl.program_id(0) == 0)
        def _():
            dkv_ref[...] = jnp.zeros_like(dkv_ref)

        z = z_ref[...]
        sz = _sigmoid(z)
        dv = d_ref[...]
        dov = dv * (z * sz)
        scale = MEM_HEAD_DIM ** -0.5
        outs, dqs = [], []
        for h in range(MEM_HEADS):
            hs = slice(MEM_HEAD_DIM * h, MEM_HEAD_DIM * (h + 1))
            qh, kh, vh, pn = _mem_probs(q_ref, kv_ref, h)
            pnb = pn.astype(BF16)
            oh = _dot(pnb, vh)
            outs.append(oh)
            doh = dov[:, hs]
            dohb = doh.astype(BF16)
            dp = _dot(dohb, vh, NT)
            ds = pn * (dp - jnp.sum(doh * oh, axis=-1, keepdims=True))
            dsb = ds.astype(BF16)
            dqs.append(_dot(dsb, kh) * scale)
            dkv_ref[:, hs] += _dot(dsb, qh, TN) * scale
            vs = slice(D_MEM + MEM_HEAD_DIM * h, D_MEM + MEM_HEAD_DIM * (h + 1))
            dkv_ref[:, vs] += _dot(pnb, dohb, TN)
        dq_ref[...] = jnp.concatenate(dqs, axis=1).astype(BF16)
        dz_ref[...] = (dv * jnp.concatenate(outs, axis=1) * (sz * (1.0 + z * (1.0 - sz)))).astype(BF16)

    row = lambda i: (i, 0)
    return pl.pallas_call(
        body, name=name, grid=(T // tm,),
        in_specs=[pl.BlockSpec((tm, D_MEM), row), pl.BlockSpec((tm, D_MEM), lambda i: (i, qcol)),
                  pl.BlockSpec((tm, D_MEM), lambda i: (i, zcol)), _full((M, 2 * D_MEM))],
        out_specs=[pl.BlockSpec((tm, D_MEM), row), pl.BlockSpec((tm, D_MEM), row), _full((M, 2 * D_MEM))],
        out_shape=[jax.ShapeDtypeStruct((T, D_MEM), BF16), jax.ShapeDtypeStruct((T, D_MEM), BF16),
                   jax.ShapeDtypeStruct((M, 2 * D_MEM), F32)],
        compiler_params=_params(("arbitrary",)),
    )(d, proj, proj, kv)


def _branches_and_gates(os_ref, oa_ref, om_ref, gl_refs, bg_ref, ws_ref, wa_ref, wm_ref):
    outs = (_dot(os_ref[...], ws_ref[...]), _dot(oa_ref[...], wa_ref[...]), _dot(om_ref[...], wm_ref[...]))
    gates = tuple(_sigmoid(jnp.concatenate([gl_refs[2 * k][...], gl_refs[2 * k + 1][...]], axis=1)
                           + bg_ref[:, D_MODEL * k:D_MODEL * (k + 1)]) for k in range(3))
    return outs, gates


def _merge_specs(tm):
    row = lambda i: (i, 0)
    first = _OFF["gates"] // GATE_BLOCK
    gate = [pl.BlockSpec((tm, GATE_BLOCK), (lambda i, k=k: (i, first + k))) for k in range(N_GATES // GATE_BLOCK)]
    return ([pl.BlockSpec((tm, D_SSM), row), pl.BlockSpec((tm, D_ATTN), row), pl.BlockSpec((tm, D_MEM), row)] + gate
            + [_full((1, N_GATES)), _full((D_SSM, D_MODEL)), _full((D_ATTN, D_MODEL)), _full((D_MEM, D_MODEL)),
               _full((D_MODEL, D_MODEL))])


def _merge_fwd(x, o_ssm, o_attn, o_mem, proj, bg, ws, wa, wm, wo, tm, name):
    T = x.shape[0]

    def body(os_ref, oa_ref, om_ref, g0, g1, g2, g3, g4, g5, bg_ref, ws_ref, wa_ref, wm_ref, wo_ref, x_ref,
             xo_ref, mg_ref):
        outs, gates = _branches_and_gates(os_ref, oa_ref, om_ref, (g0, g1, g2, g3, g4, g5), bg_ref, ws_ref, wa_ref,
                                          wm_ref)
        merged = (gates[0] * outs[0] + gates[1] * outs[1] + gates[2] * outs[2]).astype(BF16)
        mg_ref[...] = merged
        xo_ref[...] = x_ref[...] + _dot(merged, wo_ref[...])

    row = lambda i: (i, 0)
    return pl.pallas_call(
        body, name=name, grid=(T // tm,),
        in_specs=_merge_specs(tm) + [pl.BlockSpec((tm, D_MODEL), row)],
        out_specs=[pl.BlockSpec((tm, D_MODEL), row), pl.BlockSpec((tm, D_MODEL), row)],
        out_shape=[jax.ShapeDtypeStruct((T, D_MODEL), F32), jax.ShapeDtypeStruct((T, D_MODEL), BF16)],
        compiler_params=_params(("parallel",)),
    )(o_ssm, o_attn, o_mem, *([proj] * (N_GATES // GATE_BLOCK)), bg, ws, wa, wm, wo, x)


def _merge_bwd(dx, o_ssm, o_attn, o_mem, proj, bg, ws, wa, wm, wo, tm, name, job=None):
    T = dx.shape[0]

    n = T // tm

    def body(os_ref, oa_ref, om_ref, g0, g1, g2, g3, g4, g5, bg_ref, ws_ref, wa_ref, wm_ref, wo_ref, dx_ref,
             dproj_ref, db_ref, dos_ref, doa_ref, dom_ref, dbg_ref, dgl_buf, dgl_sems):
        i = pl.program_id(0)
        slot = i % 2

        def to_dproj(s, row0):
            return pltpu.make_async_copy(dgl_buf.at[s], dproj_ref.at[pl.ds(row0, tm), pl.ds(_OFF["gates"], N_GATES)],
                                         dgl_sems.at[s])

        @pl.when(i == 0)
        def _():
            dbg_ref[...] = jnp.zeros_like(dbg_ref)

        @pl.when(i >= 2)
        def _():
            to_dproj(slot, 0).wait()

        outs, gates = _branches_and_gates(os_ref, oa_ref, om_ref, (g0, g1, g2, g3, g4, g5), bg_ref, ws_ref, wa_ref,
                                          wm_ref)
        dm = _dot(dx_ref[...].astype(BF16), wo_ref[...], NT)
        w_refs = (ws_ref, wa_ref, wm_ref)
        do_refs = (dos_ref, doa_ref, dom_ref)
        for k in range(3):
            cols = slice(D_MODEL * k, D_MODEL * (k + 1))
            dgl = dm * outs[k] * (gates[k] * (1.0 - gates[k]))
            dgl_buf[slot, :, cols] = dgl.astype(BF16)
            dbg_ref[:, cols] += jnp.sum(dgl, axis=0, keepdims=True)
            dbk = (dm * gates[k]).astype(BF16)
            db_ref[:, cols] = dbk
            do_refs[k][...] = _dot(dbk, w_refs[k][...], NT)
        to_dproj(slot, pl.multiple_of(i * tm, tm)).start()

        @pl.when(i == n - 1)
        def _():
            for s in range(min(2, n)):
                to_dproj(s, 0).wait()

    row = lambda i: (i, 0)
    return _pc(
        body, job, name=name, grid=(n,),
        in_specs=_merge_specs(tm) + [pl.BlockSpec((tm, D_MODEL), row)],
        out_specs=[pl.BlockSpec(memory_space=pl.ANY), pl.BlockSpec((tm, N_GATES), row), pl.BlockSpec((tm, D_SSM), row),
                   pl.BlockSpec((tm, D_ATTN), row), pl.BlockSpec((tm, D_MEM), row), _full((1, N_GATES))],
        out_shape=[jax.ShapeDtypeStruct((T, D_IN), BF16), jax.ShapeDtypeStruct((T, N_GATES), BF16),
                   jax.ShapeDtypeStruct((T, D_SSM), F32), jax.ShapeDtypeStruct((T, D_ATTN), F32),
                   jax.ShapeDtypeStruct((T, D_MEM), F32), jax.ShapeDtypeStruct((1, N_GATES), F32)],
        scratch_shapes=[pltpu.VMEM((2, tm, N_GATES), BF16), pltpu.SemaphoreType.DMA((2,))], sem=("arbitrary",),
        operands=(o_ssm, o_attn, o_mem, *([proj] * (N_GATES // GATE_BLOCK)), bg, ws, wa, wm, wo, dx))


def _loss_head(x, g, target, tm, name):
    T, D = x.shape

    def body(x_ref, g_ref, t_ref, loss_ref, dx_ref, dg_ref):
        @pl.when(pl.program_id(0) == 0)
        def _():
            loss_ref[...] = jnp.zeros_like(loss_ref)
            dg_ref[...] = jnp.zeros_like(dg_ref)

        xv = x_ref[...]
        r = lax.rsqrt(jnp.mean(xv * xv, axis=-1, keepdims=True) + EPS)
        xr = xv * r
        err = xr * g_ref[...] - t_ref[...]
        loss_ref[...] += 0.5 * jnp.sum(jnp.mean(err * err, axis=-1, keepdims=True), axis=0, keepdims=True)
        dy = err * (1.0 / D)
        dg_ref[...] += jnp.sum(dy * xr, axis=0, keepdims=True)
        wv = dy * g_ref[...]
        dx_ref[...] = r * (wv - xr * jnp.mean(wv * xr, axis=-1, keepdims=True))

    row = lambda i: (i, 0)
    return pl.pallas_call(
        body, name=name, grid=(T // tm,),
        in_specs=[pl.BlockSpec((tm, D), row), _full((1, D)), pl.BlockSpec((tm, D), row)],
        out_specs=[_full((1, 128)), pl.BlockSpec((tm, D), row), _full((1, D))],
        out_shape=[jax.ShapeDtypeStruct((1, 128), F32), jax.ShapeDtypeStruct((T, D), F32),
                   jax.ShapeDtypeStruct((1, D), F32)],
        compiler_params=_params(("arbitrary",)),
    )(x, g, target)


def _adamw(parts, w, m, v, tr, name):
    L, R, C = w.shape

    def body(p_ref, w_ref, m_ref, v_ref, g_ref, d_ref, mo_ref, vo_ref):
        g = p_ref[0].astype(F32)
        for s in range(1, N_DEV):
            g = g + p_ref[s].astype(F32)
        mn = ADAM_B1 * m_ref[...] + (1.0 - ADAM_B1) * g
        vn = ADAM_B2 * v_ref[...] + (1.0 - ADAM_B2) * (g * g)
        m_hat = mn / (1.0 - ADAM_B1 ** ADAM_STEP)
        v_hat = vn / (1.0 - ADAM_B2 ** ADAM_STEP)
        g_ref[...] = g
        d_ref[...] = -ADAM_LR * (m_hat / (jnp.sqrt(v_hat) + ADAM_EPS) + ADAM_WD * w_ref[...])
        mo_ref[...] = mn
        vo_ref[...] = vn

    one = pl.BlockSpec((None, tr, C), lambda l, i: (l, i, 0))
    return pl.pallas_call(
        body, name=name, grid=(L, R // tr),
        in_specs=[pl.BlockSpec((N_DEV, None, tr, C), lambda l, i: (0, l, i, 0)), one, one, one],
        out_specs=[one] * 4,
        out_shape=[jax.ShapeDtypeStruct((L, R, C), F32)] * 4,
        compiler_params=_params(("parallel", "parallel")),
    )(parts, w, m, v)


_SHARDED = (("w_in", (1088, 1024), 1), ("w_glu", (96, 768), 0), ("w_mem_kv", (128, 1024), 0),
            ("w_br_ssm", (768, 128), 1), ("w_br_attn", (768, 128), 1), ("w_br_mem", (512, 128), 1),
            ("w_out", (128, 1024), 0))
_W_IN = 0
_SMALL = tuple(range(1, len(_SHARDED)))


class _Job(NamedTuple):
    ins: list
    out_shape: list
    aliases: dict
    pairs: Callable
    n: int


def _peers():
    x, y, c = lax.axis_index("x"), lax.axis_index("y"), lax.axis_index("c")
    me = 4 * x + 2 * y + c
    out = []
    for k in range(1, N_DEV):
        px = 1 - x if k & 4 else x
        py = 1 - y if k & 2 else y
        pc = 1 - c if k & 1 else c
        out.append(((px, py, pc), 4 * px + 2 * py + pc))
    return me, out


def _copies(pairs, send_sems, recv_sems, local_sems, arrivals):
    me, peers = _peers()
    local = [pltpu.make_async_copy(src(me), dst(me), local_sems.at[j]) for j, (src, dst) in enumerate(pairs)]
    sends, recvs = [], []
    for k, (peer, lin) in enumerate(peers):
        for j, (src, dst) in enumerate(pairs):
            for to, out in ((dst(me), sends), (dst(lin), recvs)):
                if out is sends or arrivals:
                    out.append(pltpu.make_async_remote_copy(
                        src_ref=src(lin), dst_ref=to, send_sem=send_sems.at[j, k], recv_sem=recv_sems.at[j, k],
                        device_id=peer, device_id_type=pl.DeviceIdType.MESH))
    return local, sends, recvs


def _start_copies(pairs, *sems):
    local, sends, _ = _copies(pairs, *sems, arrivals=False)
    for cp in local + sends:
        cp.start()


def _wait_copies(pairs, *sems):
    local, sends, recvs = _copies(pairs, *sems, arrivals=True)
    for cp in recvs:
        cp.wait_recv()
    for cp in sends:
        cp.wait_send()
    for cp in local:
        cp.wait()


def _job_scratch(job):
    return [pltpu.SemaphoreType.DMA((job.n, N_DEV - 1)), pltpu.SemaphoreType.DMA((job.n, N_DEV - 1)),
            pltpu.SemaphoreType.DMA((job.n,))]


def _pc(body, job, *, name, grid, in_specs, out_specs, out_shape, scratch_shapes, sem, operands, aliases=None):
    aliases = aliases or {}
    if job is None:
        return pl.pallas_call(body, name=name, grid=grid, in_specs=in_specs, out_specs=out_specs, out_shape=out_shape,
                              scratch_shapes=scratch_shapes, input_output_aliases=aliases,
                              compiler_params=_params(sem))(*operands)
    a = len(in_specs)
    b = a + len(job.ins)
    c = b + len(out_shape)
    d = c + len(job.out_shape)
    e = d + len(scratch_shapes)

    def carried(*refs):
        pairs = job.pairs(refs[a:b], refs[c:d])
        ids = [pl.program_id(k) for k in range(len(grid))]
        first = functools.reduce(jnp.logical_and, [i == 0 for i in ids])
        last = functools.reduce(jnp.logical_and, [i == n - 1 for i, n in zip(ids, grid)])

        @pl.when(first)
        def _():
            _start_copies(pairs, *refs[e:])

        body(*refs[:a], *refs[b:c], *refs[d:e])

        @pl.when(last)
        def _():
            _wait_copies(pairs, *refs[e:])

    hbm = pl.BlockSpec(memory_space=pl.ANY)
    outs = pl.pallas_call(
        carried, name=name, grid=grid,
        in_specs=list(in_specs) + [hbm] * len(job.ins), out_specs=list(out_specs) + [hbm] * len(job.out_shape),
        out_shape=list(out_shape) + list(job.out_shape),
        input_output_aliases={**aliases, **{a + i: len(out_shape) + o for i, o in job.aliases.items()}},
        scratch_shapes=list(scratch_shapes) + _job_scratch(job),
        compiler_params=_params(("arbitrary",) * len(grid)),
    )(*operands, *job.ins)
    return outs[:len(out_shape)], outs[len(out_shape):]


def _gather_via_sibling(x, take, place, out_shape, name, landing=None):
    def body(*refs):
        x_ref, o_ref = refs[0], refs[-4]
        send_sems, recv_sems, local_sem = refs[-3:]
        x, y, c = lax.axis_index("x"), lax.axis_index("y"), lax.axis_index("c")
        me, sibling = (x, y, c), (x, y, 1 - c)
        chips = [(1 - x, y), (x, 1 - y), (1 - x, 1 - y)]
        src = take(x_ref)

        def slot(px, py, pc):
            return place(o_ref, 4 * px + 2 * py + pc)

        def copy(k, block, to, first_hand):
            return pltpu.make_async_remote_copy(
                src_ref=src if first_hand else slot(*block), dst_ref=slot(*block), send_sem=send_sems.at[k],
                recv_sem=recv_sems.at[k], device_id=to, device_id_type=pl.DeviceIdType.MESH)

        mine = pltpu.make_async_copy(src, slot(*me), local_sem)
        mine.start()
        first = [copy(0, me, sibling, True)] + [copy(1 + j, me, (*chip, c), True) for j, chip in enumerate(chips)]
        for cp in first:
            cp.start()
        passed = []
        for j, chip in enumerate(chips):
            copy(1 + j, (*chip, c), me, True).wait_recv()
            passed.append(copy(4 + j, (*chip, c), sibling, False))
            passed[-1].start()
        copy(0, sibling, me, True).wait_recv()
        for j, chip in enumerate(chips):
            copy(4 + j, (*chip, 1 - c), me, False).wait_recv()
        for cp in first + passed:
            cp.wait_send()
        mine.wait()

    hbm = pl.BlockSpec(memory_space=pl.ANY)
    ins = [x] if landing is None else [x, landing]
    return pl.pallas_call(
        body, name=name, in_specs=[hbm] * len(ins), out_specs=hbm, out_shape=out_shape,
        input_output_aliases={} if landing is None else {1: 0},
        scratch_shapes=[pltpu.SemaphoreType.DMA((N_DEV - 1,)), pltpu.SemaphoreType.DMA((N_DEV - 1,)),
                        pltpu.SemaphoreType.DMA],
    )(*ins)


def _lane_window(ref, who):
    return ref.at[:, pl.ds(pl.multiple_of(who * LANES, LANES), LANES)]


def _gather_job(shards, items):
    out_shape = []
    for i, _ in items:
        _, s, axis = _SHARDED[i]
        whole = i != _W_IN and axis == 1
        out_shape.append(jax.ShapeDtypeStruct((s[0], N_DEV * s[1]) if whole else (N_DEV,) + s, BF16))

    def pairs(in_refs, out_refs):
        out = []
        for (i, l), src, dst in zip(items, in_refs, out_refs):
            if i != _W_IN and _SHARDED[i][2] == 1:
                out.append((lambda who, src=src, l=l: src.at[l], lambda who, dst=dst: _lane_window(dst, who)))
            else:
                out.append((lambda who, src=src, l=l: src.at[l], lambda who, dst=dst: dst.at[who]))
        return out

    return _Job([shards[i] for i, _ in items], out_shape, {}, pairs, len(items))


def _landed_weights(items, landed):
    out = {}
    for (i, _), a in zip(items, landed):
        n, s, axis = _SHARDED[i]
        if i == _W_IN:
            out[n] = a.reshape(D_IN, D_MODEL)
        elif axis == 0:
            out[n] = a.reshape(N_DEV * s[0], s[1])
        else:
            out[n] = a
    return out


def _scatter_job(grads, items, layer, parts=None):
    ng = len(grads)
    out_shape = [jax.ShapeDtypeStruct((N_DEV, DEPTH) + _SHARDED[i][1], BF16) for i in items]

    def pairs(in_refs, out_refs):
        out = []
        for i, src, dst in zip(items, in_refs[:ng], out_refs):
            _, s, axis = _SHARDED[i]
            if i == _W_IN:
                take = lambda who, src=src: src.at[who]
            elif axis == 0:
                take = lambda who, src=src, s=s: src.at[pl.ds(pl.multiple_of(who * s[0], 16), s[0])]
            else:
                take = lambda who, src=src: _lane_window(src, who)
            out.append((take, lambda who, dst=dst: dst.at[who, layer]))
        return out

    aliases = {} if parts is None else {ng + j: j for j in range(len(items))}
    return _Job(list(grads) + ([] if parts is None else list(parts)), out_shape, aliases, pairs, len(items))


def _rows_job(src, row0, landing=None):
    n = src.shape[0]
    pairs = lambda in_refs, out_refs: [(lambda who: in_refs[0], lambda who: out_refs[0].at[who, pl.ds(row0, n)])]
    return _Job([src] + ([] if landing is None else [landing]), [jax.ShapeDtypeStruct((N_DEV, _REP_ROWS, LANES), F32)],
                {} if landing is None else {1: 0}, pairs, 1)


_REPLICATED = (("norm_g", (2, 1024)), ("mem_norm_g", (2, 1024)), ("b_gate", (2, 3072)),
               ("ssm_lambda_re", (2, 48, 64)), ("ssm_lambda_im", (2, 48, 64)), ("ssm_log_dt", (2, 48)),
               ("ssm_b_re", (2, 48, 64, 16)), ("ssm_b_im", (2, 48, 64, 16)), ("ssm_c_re", (2, 48, 16, 64)),
               ("ssm_c_im", (2, 48, 16, 64)), ("ssm_d", (2, 768)), ("b_glu", (2, 768)), ("rel_bias", (32, 12)),
               ("final_norm_g", (1024,)))
_PER_LAYER = tuple((n, s[1:]) for n, s in _REPLICATED if s[0] == DEPTH and len(s) > 1)
_SHARED = tuple((n, s) for n, s in _REPLICATED if (n, s[1:]) not in _PER_LAYER)
_REP_HALF_ROWS = 1664
_REP_ROWS = 2 * _REP_HALF_ROWS
assert sum(int(np.prod(s)) for _, s in _PER_LAYER + _SHARED) <= _REP_HALF_ROWS * LANES


def _pack_half(tree, layer, shared):
    flat = [tree[n][layer].reshape(-1) for n, _ in _PER_LAYER]
    if shared:
        flat += [tree[n].reshape(-1) for n, _ in _SHARED]
    flat = jnp.concatenate(flat)
    return jnp.pad(flat, (0, _REP_HALF_ROWS * LANES - flat.shape[0])).reshape(_REP_HALF_ROWS, LANES)


def _pack_replicated(tree):
    return jnp.concatenate([_pack_half(tree, 1, False), _pack_half(tree, 0, True)])[None]


def _unpack_replicated(packed):
    halves = packed.reshape(2, -1)
    out, r = {}, 0
    for n, s in _PER_LAYER:
        size = int(np.prod(s))
        out[n] = jnp.stack([halves[1, r:r + size].reshape(s), halves[0, r:r + size].reshape(s)])
        r += size
    for n, s in _SHARED:
        size = int(np.prod(s))
        out[n] = halves[1, r:r + size].reshape(s)
        r += size
    return out


def _discretize(lam_re, lam_im, log_dt, b_re, b_im):
    dt = jnp.exp(log_dt)[:, None]
    mag = jnp.exp(lam_re * dt)
    abar_re, abar_im = mag * jnp.cos(lam_im * dt), mag * jnp.sin(lam_im * dt)
    den = lam_re * lam_re + lam_im * lam_im
    nr, ni = abar_re - 1.0, abar_im
    f_re = (nr * lam_re + ni * lam_im) / den
    f_im = (ni * lam_re - nr * lam_im) / den
    bbar_re = f_re[..., None] * b_re - f_im[..., None] * b_im
    bbar_im = f_re[..., None] * b_im + f_im[..., None] * b_re
    return abar_re, abar_im, bbar_re, bbar_im


def _block_diag(a):
    _, R, C = a.shape
    a = a.reshape(SSM_BLOCKS, 8, R, C)
    eye = jnp.eye(8, dtype=a.dtype)
    return (a[:, :, :, None, :] * eye[None, :, None, :, None]).reshape(SSM_BLOCKS, 8 * R, 8 * C)


def _diag_blocks(a, R, C):
    a = a.reshape(SSM_BLOCKS, 8, R, 8, C)
    eye = jnp.eye(8, dtype=a.dtype)
    return jnp.sum(a * eye[None, :, None, :, None], axis=3).reshape(SSM_GROUPS, R, C)


def _carried(result, job):
    return (result, None) if job is None else result


def _layer_fwd(x, mem, W, P, bias, layer, jobs):
    tag = f"l{layer}"
    abar_re, abar_im, bbar_re, bbar_im = _discretize(P["ssm_lambda_re"][layer], P["ssm_lambda_im"][layer],
                                                     P["ssm_log_dt"][layer], P["ssm_b_re"][layer], P["ssm_b_im"][layer])
    c_re, c_im = P["ssm_c_re"][layer], P["ssm_c_im"][layer]
    ssm = dict(
        are=abar_re.reshape(1, N_STATE), aim=abar_im.reshape(1, N_STATE),
        bre=_block_diag(bbar_re.transpose(0, 2, 1)).astype(BF16), bim=_block_diag(bbar_im.transpose(0, 2, 1)).astype(BF16),
        cre=_block_diag(c_re.transpose(0, 2, 1)).astype(BF16), cimn=_block_diag(-c_im.transpose(0, 2, 1)).astype(BF16),
        ctre=_block_diag(c_re).astype(BF16), ctimn=_block_diag(-c_im).astype(BF16),
        btre=_block_diag(bbar_re).astype(BF16), btim=_block_diag(bbar_im).astype(BF16),
        d=P["ssm_d"][layer].reshape(1, D_SSM))
    bglu = P["b_glu"][layer].reshape(1, D_SSM)
    bgate = P["b_gate"][layer].reshape(1, N_GATES)
    g = P["norm_g"][layer].reshape(1, D_MODEL)
    gm = P["mem_norm_g"][layer].reshape(1, D_MODEL)
    delivered = {}

    def carry(stage):
        return jobs[stage][0] if stage in jobs else None

    def deliver(stage, landed):
        if landed is not None:
            delivered[stage] = _landed_weights(jobs[stage][1], landed)

    T = x.shape[0]
    (proj, h), landed = _carried(_norm_proj(x, g, W["w_in"], min(T, 1024), 2176, f"{tag}_proj", job=carry("proj"),
                                            w_turned=True), carry("proj"))
    deliver("proj", landed)
    W = {**W, **delivered.get("proj", {})}
    (xr, xi, y, o_ssm), landed = _carried(
        _ssm_fwd(proj, ssm["bre"], ssm["bim"], ssm["cre"], ssm["cimn"], ssm["are"], ssm["aim"], ssm["d"], W["w_glu"],
                 bglu, 256, f"{tag}_ssm", job=carry("ssm")), carry("ssm"))
    deliver("ssm", landed)
    os, lses = [], []
    for grp in range(3):
        stage = f"attn{grp}"
        (o_g, lse_g), landed = _carried(_attn_fwd(proj, bias[grp], grp, f"{tag}_{stage}", job=carry(stage)), carry(stage))
        deliver(stage, landed)
        os.append(o_g)
        lses.append(lse_g)
    o_attn = _attn_mix(os, lses, proj, 512, f"{tag}_attn_mix")
    kvb, hm = _norm_proj(mem, gm, W["w_mem_kv"], mem.shape[0], 1024, f"{tag}_mem_kv", out_dtype=BF16)
    o_mem = _mem_fwd(proj, kvb, 512, f"{tag}_mem")
    x_out, merged = _merge_fwd(x, o_ssm, o_attn, o_mem, proj, bgate, W["w_br_ssm"], W["w_br_attn"], W["w_br_mem"],
                               W["w_out"], 256, f"{tag}_merge")
    res = dict(x=x, mem=mem, proj=proj, h=h, xr=xr, xi=xi, y=y, o_ssm=o_ssm, os=os, lses=lses,
               o_attn=o_attn, kvb=kvb, hm=hm, o_mem=o_mem, merged=merged, ssm=ssm, bglu=bglu,
               bgate=bgate, g=g, gm=gm, W=W)
    return x_out, res, delivered


def _layer_bwd(dx, res, P, bias, layer, jobs):
    tag = f"l{layer}b"
    proj, ssm, W = res["proj"], res["ssm"], res["W"]
    T = dx.shape[0]
    landed = {}

    def run(stage, fn, job):
        out, landed[stage] = _carried(fn(job), job)
        if job is None:
            del landed[stage]
        return out

    dproj, dbr, do_ssm, do_attn, do_mem, dbg = run(
        "merge", lambda job: _merge_bwd(dx, res["o_ssm"], res["o_attn"], res["o_mem"], proj, res["bgate"], W["w_br_ssm"],
                                        W["w_br_attn"], W["w_br_mem"], W["w_out"], 256, f"{tag}_merge", job=job),
        jobs.get("merge"))
    gw = {}
    gw["w_out"] = _mm_tn(res["merged"], dx, 1024, 1024, 512, f"{tag}_dw_out")
    gw["w_br_ssm"] = _mm_tn(res["o_ssm"], dbr, 768, 1024, 512, f"{tag}_dw_br_ssm", b_col=0, n=1024)
    gw["w_br_attn"] = _mm_tn(res["o_attn"], dbr, 768, 1024, 512, f"{tag}_dw_br_attn", b_col=1024, n=1024)
    gw["w_br_mem"] = _mm_tn(res["o_mem"], dbr, 512, 1024, 512, f"{tag}_dw_br_mem", b_col=2048, n=1024)

    dqm, dzm, dkv = _mem_bwd(do_mem, proj, res["kvb"], 512, f"{tag}_mem")
    M = dkv.shape[0]
    gw["w_mem_kv"] = _mm_tn(res["hm"], dkv, 1024, 1024, M, f"{tag}_dw_mem_kv")
    _, dgm = _proj_bwd(dkv.astype(BF16), W["w_mem_kv"], res["mem"], res["gm"], jnp.zeros_like(res["mem"]), M, 1024,
                       f"{tag}_mem_norm")

    do_g, corr, dproj = _attn_mix_bwd(do_attn, res["os"], res["lses"], proj, dproj, 512, f"{tag}_attn_mix")
    dbs = []
    for grp in range(3):
        dproj, db_g = _attn_bwd(proj, do_g, corr, res["lses"][grp], bias[grp], dproj, grp, f"{tag}_attn{grp}")
        dbs.append(db_g)
    dbias = jnp.stack(dbs)

    dy, dproj, gelu_b, dt_b, dbglu = _glu_bwd(do_ssm, res["y"], proj, W["w_glu"], res["bglu"], dproj, 512, f"{tag}_glu")
    gw["w_glu"] = _mm_tn(gelu_b, dt_b, 768, 768, 512, f"{tag}_dw_glu")
    dproj, dbre, dbim, dcre, dcim, dare, daim, dd = run(
        "ssm", lambda job: _ssm_bwd(dy, proj, res["xr"], res["xi"], ssm["ctre"], ssm["ctimn"], ssm["btre"], ssm["btim"],
                                    ssm["are"], ssm["aim"], ssm["d"], dproj, 256, f"{tag}_ssm", job=job),
        jobs.get("ssm"))
    _, disc_vjp = jax.vjp(_discretize, P["ssm_lambda_re"][layer], P["ssm_lambda_im"][layer], P["ssm_log_dt"][layer],
                          P["ssm_b_re"][layer], P["ssm_b_im"][layer])
    d_lre, d_lim, d_ldt, d_bre, d_bim = disc_vjp((dare.reshape(SSM_GROUPS, SSM_STATE), daim.reshape(SSM_GROUPS, SSM_STATE),
                                                  _diag_blocks(dbre, SSM_STATE, SSM_GROUP),
                                                  _diag_blocks(dbim, SSM_STATE, SSM_GROUP)))

    small = [gw[_SHARDED[i][0]] for i in _SMALL]
    for seg, piece in (("q_mem", dqm), ("z_mem", dzm)):
        dproj = lax.dynamic_update_slice(dproj, piece, (0, _OFF[seg]))
    dw_in = run("dw_in", lambda job: _mm_tn(dproj, res["h"], 2176, 1024, min(T, 1024), f"{tag}_dw_in", job=job),
                jobs["dw_in"](small) if "dw_in" in jobs else None)
    dw_in = dw_in.reshape((N_DEV,) + _SHARDED[_W_IN][1])
    dx_in, dg = run("proj", lambda job: _proj_bwd(dproj, W["w_in"], res["x"], res["g"], dx, min(T, 1024), 2176,
                                                  f"{tag}_proj", job=job, w_turned=True),
                    jobs["proj"](small, dw_in, landed) if "proj" in jobs else None)

    gp = dict(norm_g=dg[0], mem_norm_g=dgm[0], b_gate=dbg[0], ssm_lambda_re=d_lre, ssm_lambda_im=d_lim,
              ssm_log_dt=d_ldt, ssm_b_re=d_bre, ssm_b_im=d_bim,
              ssm_c_re=_diag_blocks(dcre, SSM_GROUP, SSM_STATE), ssm_c_im=_diag_blocks(dcim, SSM_GROUP, SSM_STATE),
              ssm_d=dd[0], b_glu=dbglu[0])
    return dx_in, dw_in, gp, dbias, landed


def _train_step(x, mem, target, shards, P):
    rest0 = [(i, 0) for i in _SMALL]
    rows1 = [(i, 1) for i in _SMALL if _SHARDED[i][2] == 0]
    cols1 = [(i, 1) for i in _SMALL if _SHARDED[i][2] == 1]
    first = [(_W_IN, 0)]
    w_in0 = _gather_via_sibling(shards[_W_IN], lambda ref: ref.at[0], lambda ref, s: ref.at[s],
                                jax.ShapeDtypeStruct((N_DEV,) + _SHARDED[_W_IN][1], BF16), "gather_w_in0")
    W0 = _landed_weights(first, [w_in0])
    buckets = _bucket_tables()
    bias = _bias_tables(P["rel_bias"], buckets, "bias_tables")
    jobs0 = {"proj": (_gather_job(shards, rest0), rest0), "ssm": (_gather_job(shards, [(_W_IN, 1)]), [(_W_IN, 1)]),
             "attn0": (_gather_job(shards, rows1), rows1), "attn1": (_gather_job(shards, cols1), cols1)}
    x, res0, delivered = _layer_fwd(x, mem, W0, P, bias, 0, jobs0)
    W1 = {**delivered["ssm"], **delivered["attn0"], **delivered["attn1"]}
    x, res1, _ = _layer_fwd(x, mem, W1, P, bias, 1, {})
    loss, dx, dgf = _loss_head(x, P["final_norm_g"].reshape(1, D_MODEL), target, 512, "loss_head")

    dx, dw_in1, gp1, dbias1, landed1 = _layer_bwd(
        dx, res1, P, bias, 1, {"proj": lambda small, dw_in, landed: _scatter_job(small, _SMALL, 1)})
    rep1 = _pack_half({n: a[None] for n, a in gp1.items()}, 0, False)
    dx, _, gp0, dbias0, landed0 = _layer_bwd(
        dx, res0, P, bias, 0,
        {"merge": _rows_job(rep1, 0), "ssm": _scatter_job([dw_in1], [_W_IN], 1),
         "dw_in": lambda small: _scatter_job(small, _SMALL, 0, parts=landed1["proj"]),
         "proj": lambda small, dw_in, landed: _scatter_job([dw_in], [_W_IN], 0, parts=landed["ssm"])})
    d_rel = _bias_grad(dbias0, dbias1, buckets, "bias_grad")
    gp0 = {n: a[None] for n, a in gp0.items()}
    gp0["rel_bias"] = jnp.sum(d_rel, axis=-1).transpose(2, 0, 1).reshape(NUM_BUCKETS, 12)
    gp0["final_norm_g"] = dgf[0]
    rep0 = _pack_half(gp0, 0, True)
    rparts = _gather_via_sibling(rep0, lambda ref: ref, lambda ref, s: ref.at[s, pl.ds(_REP_HALF_ROWS, _REP_HALF_ROWS)],
                                 jax.ShapeDtypeStruct((N_DEV, _REP_ROWS, LANES), F32), "gather_small_grads0",
                                 landing=landed0["merge"][0])
    return loss[0, 0], dx, list(landed0["proj"]) + list(landed0["dw_in"]), rparts


_WEIGHTS = ["norm_g", "mem_norm_g", "w_in", "b_gate", "ssm_lambda_re", "ssm_lambda_im", "ssm_log_dt", "ssm_b_re",
            "ssm_b_im", "ssm_c_re", "ssm_c_im", "ssm_d", "w_glu", "b_glu", "w_mem_kv", "w_br_ssm", "w_br_attn",
            "w_br_mem", "w_out", "rel_bias", "final_norm_g"]
_ADAM_ROWS = {"w_in": 136,"w_glu": 96, "w_mem_kv": 128, "w_br_ssm": 768, "w_br_attn": 768, "w_br_mem": 512,
              "w_out": 128}


def kernel(x, mem, norm_g, mem_norm_g, w_in, b_gate, ssm_lambda_re, ssm_lambda_im, ssm_log_dt, ssm_b_re, ssm_b_im, ssm_c_re, ssm_c_im, ssm_d, w_glu, b_glu, w_mem_kv, w_br_ssm, w_br_attn, w_br_mem, w_out, rel_bias, final_norm_g, loss_target, m_norm_g, m_mem_norm_g, m_w_in, m_b_gate, m_ssm_lambda_re, m_ssm_lambda_im, m_ssm_log_dt, m_ssm_b_re, m_ssm_b_im, m_ssm_c_re, m_ssm_c_im, m_ssm_d, m_w_glu, m_b_glu, m_w_mem_kv, m_w_br_ssm, m_w_br_attn, m_w_br_mem, m_w_out, m_rel_bias, m_final_norm_g, v_norm_g, v_mem_norm_g, v_w_in, v_b_gate, v_ssm_lambda_re, v_ssm_lambda_im, v_ssm_log_dt, v_ssm_b_re, v_ssm_b_im, v_ssm_c_re, v_ssm_c_im, v_ssm_d, v_w_glu, v_b_glu, v_w_mem_kv, v_w_br_ssm, v_w_br_attn, v_w_br_mem, v_w_out, v_rel_bias, v_final_norm_g):
    given = dict(locals())
    w = {n: given[n] for n in _WEIGHTS}
    m = {n: given["m_" + n] for n in _WEIGHTS}
    v = {n: given["v_" + n] for n in _WEIGHTS}

    turned = lambda n, a: a.swapaxes(1, 2) if n == "w_in" else a
    shards = [turned(n, w[n]).astype(BF16) for n, _, _ in _SHARDED]
    loss, dx, parts, rparts = _train_step(x[0], mem[0], loss_target[0], shards, w)
    loss = lax.psum(loss, ("x", "y", "c"))

    new = {}
    for (n, _, _), p in zip(_SHARDED, parts):
        new[n] = [turned(n, a) for a in _adamw(p, turned(n, w[n]), turned(n, m[n]), turned(n, v[n]), _ADAM_ROWS[n],
                                               f"adamw_{n}")]
    rp = [_unpack_replicated(a) for a in _adamw(rparts[:, None], _pack_replicated(w), _pack_replicated(m),
                                                _pack_replicated(v), _REP_ROWS // 4, "adamw_replicated")]
    for n, _ in _REPLICATED:
        new[n] = [rp[kind][n] for kind in range(4)]
    outs = [loss, dx[None]]
    for kind in range(4):
        outs.extend(new[n][kind] for n in _WEIGHTS)
    return tuple(outs)
```

```python
import functools
import math
from typing import Callable, NamedTuple

import jax
import jax.numpy as jnp
import numpy as np
from jax import lax
from jax.experimental import pallas as pl
from jax.experimental.pallas import tpu as pltpu

F32 = jnp.float32
BF16 = jnp.bfloat16

D_MODEL = 1024
DEPTH = 2
EPS = 1e-6
D_SSM = 768
SSM_GROUP = 16
SSM_GROUPS = 48
SSM_STATE = 64
N_STATE = SSM_GROUPS * SSM_STATE
SSM_BLOCKS = 6
D_ATTN = 768
ATTN_HEAD_DIM = 64
ATTN_GROUP_WIDTH = 256
ATTN_DILATIONS = (1, 4, 16)
ATTN_SPAN = 128
ATTN_BLOCK = 128
NUM_BUCKETS = 32
REL_MAX_DISTANCE = 2048
NEG_INF = -1e30
MEM_HEADS = 4
MEM_HEAD_DIM = 128
D_MEM = 512
N_GATES = 3 * D_MODEL
D_IN = 8704
N_DEV = 8
LANES = 128
ADAM_LR = 0.001
ADAM_B1 = 0.9
ADAM_B2 = 0.999
ADAM_EPS = 1e-08
ADAM_WD = 0.01
ADAM_STEP = 10

_OFF = {"u": 0, "z_ssm": 768, "q": 1536, "k": 2304, "v": 3072, "z_attn": 3840, "q_mem": 4608, "z_mem": 5120,
        "gates": 5632}
GATE_BLOCK = 512

NN = (((1,), (0,)), ((), ()))
NT = (((1,), (1,)), ((), ()))
TN = (((0,), (0,)), ((), ()))

VMEM_LIMIT = 56 * 1024 * 1024


def _dot(a, b, dims=NN):
    return lax.dot_general(a, b, dims, preferred_element_type=F32)


def _sigmoid(x):
    return 1.0 / (1.0 + jnp.exp(-x))


def _gelu_parts(x):
    k = math.sqrt(2.0 / math.pi)
    t = jnp.tanh(k * (x + 0.044715 * (x * x * x)))
    cdf = 0.5 * (1.0 + t)
    dcdf = 0.5 * (1.0 - t * t) * k * (1.0 + 3.0 * 0.044715 * (x * x))
    return x * cdf, cdf + x * dcdf


def _params(sem, vmem=VMEM_LIMIT):
    return pltpu.CompilerParams(dimension_semantics=sem, vmem_limit_bytes=vmem)


def _full(shape):
    return pl.BlockSpec(shape, lambda *_: (0,) * len(shape))


def _norm_proj(x, g, w, tm, tn, name, out_dtype=F32, job=None, w_turned=False):
    T, D = x.shape
    N = w.shape[0] if w_turned else w.shape[1]
    w_spec = pl.BlockSpec((tn, D), lambda i, j: (j, 0)) if w_turned else pl.BlockSpec((D, tn), lambda i, j: (0, j))
    dims = NT if w_turned else NN

    def body(x_ref, g_ref, w_ref, o_ref, h_ref, hs):
        @pl.when(pl.program_id(1) == 0)
        def _():
            xv = x_ref[...]
            r = lax.rsqrt(jnp.mean(xv * xv, axis=-1, keepdims=True) + EPS)
            hv = (xv * r * g_ref[...]).astype(BF16)
            hs[...] = hv
            h_ref[...] = hv

        o_ref[...] = _dot(hs[...], w_ref[...], dims).astype(out_dtype)

    return _pc(
        body, job, name=name, grid=(T // tm, N // tn),
        in_specs=[pl.BlockSpec((tm, D), lambda i, j: (i, 0)), _full((1, D)), w_spec],
        out_specs=[pl.BlockSpec((tm, tn), lambda i, j: (i, j)), pl.BlockSpec((tm, D), lambda i, j: (i, 0))],
        out_shape=[jax.ShapeDtypeStruct((T, N), out_dtype), jax.ShapeDtypeStruct((T, D), BF16)],
        scratch_shapes=[pltpu.VMEM((tm, D), BF16)], sem=("parallel", "arbitrary"), operands=(x, g, w))


def _mm_tn(a, b, tm, tn, tk, name, b_col=0, n=None, job=None):
    K, M = a.shape
    N = b.shape[1] if n is None else n
    nk = K // tk
    j0 = b_col // tn

    def body(a_ref, b_ref, o_ref, acc):
        k = pl.program_id(2)

        @pl.when(k == 0)
        def _():
            acc[...] = jnp.zeros_like(acc)

        acc[...] += _dot(a_ref[...].astype(BF16), b_ref[...].astype(BF16), TN)

        @pl.when(k == nk - 1)
        def _():
            o_ref[...] = acc[...].astype(BF16)

    out = _pc(
        body, job, name=name, grid=(M // tm, N // tn, nk),
        in_specs=[pl.BlockSpec((tk, tm), lambda i, j, k: (k, i)), pl.BlockSpec((tk, tn), lambda i, j, k: (k, j0 + j))],
        out_specs=[pl.BlockSpec((tm, tn), lambda i, j, k: (i, j))],
        out_shape=[jax.ShapeDtypeStruct((M, N), BF16)],
        scratch_shapes=[pltpu.VMEM((tm, tn), F32)], sem=("parallel", "parallel", "arbitrary"), operands=(a, b))
    return out[0] if job is None else (out[0][0], out[1])


def _proj_bwd(dp, w, x, g, dres, tm, tk, name, job=None, w_turned=False):
    T, N = dp.shape
    D = x.shape[1]
    nk = N // tk
    w_spec = pl.BlockSpec((tk, D), lambda i, k: (k, 0)) if w_turned else pl.BlockSpec((D, tk), lambda i, k: (0, k))
    dims = NN if w_turned else NT

    def body(dp_ref, w_ref, x_ref, g_ref, dres_ref, dx_ref, dg_ref, acc):
        i, k = pl.program_id(0), pl.program_id(1)

        @pl.when(k == 0)
        def _():
            acc[...] = jnp.zeros_like(acc)

        @pl.when((i == 0) & (k == 0))
        def _():
            dg_ref[...] = jnp.zeros_like(dg_ref)

        acc[...] += _dot(dp_ref[...], w_ref[...], dims)

        @pl.when(k == nk - 1)
        def _():
            xv = x_ref[...]
            dh = acc[...]
            r = lax.rsqrt(jnp.mean(xv * xv, axis=-1, keepdims=True) + EPS)
            xr = xv * r
            dg_ref[...] += jnp.sum(dh * xr, axis=0, keepdims=True)
            wv = dh * g_ref[...]
            dx_ref[...] = dres_ref[...] + r * (wv - xr * jnp.mean(wv * xr, axis=-1, keepdims=True))

    return _pc(
        body, job, name=name, grid=(T // tm, nk),
        in_specs=[pl.BlockSpec((tm, tk), lambda i, k: (i, k)), w_spec,
                  pl.BlockSpec((tm, D), lambda i, k: (i, 0)), _full((1, D)),
                  pl.BlockSpec((tm, D), lambda i, k: (i, 0))],
        out_specs=[pl.BlockSpec((tm, D), lambda i, k: (i, 0)), _full((1, D))],
        out_shape=[jax.ShapeDtypeStruct((T, D), F32), jax.ShapeDtypeStruct((1, D), F32)],
        scratch_shapes=[pltpu.VMEM((tm, D), F32)], sem=("arbitrary", "arbitrary"), operands=(dp, w, x, g, dres))


def _ssm_fwd(proj, bre, bim, cre, cimn, are, aim, d, wglu, bglu, tc, name, job=None):
    T = proj.shape[0]
    ucol, zcol = _OFF["u"] // D_SSM, _OFF["z_ssm"] // D_SSM

    def body(u_ref, z_ref, bre_ref, bim_ref, cre_ref, cim_ref, are_ref, aim_ref, d_ref, wg_ref, bg_ref,
             xr_ref, xi_ref, y_ref, o_ref, car_r, car_i):
        @pl.when(pl.program_id(0) == 0)
        def _():
            car_r[...] = jnp.zeros_like(car_r)
            car_i[...] = jnp.zeros_like(car_i)

        u = u_ref[...]
        ub = u.astype(BF16)
        for k in range(SSM_BLOCKS):
            uk = ub[:, 128 * k:128 * (k + 1)]
            xr_ref[:, 512 * k:512 * (k + 1)] = _dot(uk, bre_ref[k])
            xi_ref[:, 512 * k:512 * (k + 1)] = _dot(uk, bim_ref[k])
        ar, ai = are_ref[...], aim_ref[...]

        def step(t, c):
            pr, pi = c
            nr = ar * pr - ai * pi + xr_ref[pl.ds(t, 1), :]
            ni = ar * pi + ai * pr + xi_ref[pl.ds(t, 1), :]
            xr_ref[pl.ds(t, 1), :] = nr
            xi_ref[pl.ds(t, 1), :] = ni
            return nr, ni

        pr, pi = lax.fori_loop(0, tc, step, (car_r[...], car_i[...]))
        car_r[...] = pr
        car_i[...] = pi

        ys = []
        for k in range(SSM_BLOCKS):
            xrk = xr_ref[:, 512 * k:512 * (k + 1)].astype(BF16)
            xik = xi_ref[:, 512 * k:512 * (k + 1)].astype(BF16)
            ys.append(_dot(xrk, cre_ref[k]) + _dot(xik, cim_ref[k]))
        y = jnp.concatenate(ys, axis=1) + d_ref[...] * u
        y_ref[...] = y
        gl, _ = _gelu_parts(y)
        t = _dot(gl.astype(BF16), wg_ref[...]) + bg_ref[...]
        z = z_ref[...]
        o_ref[...] = (gl * _sigmoid(t) * (z * _sigmoid(z))).astype(BF16)

    return _pc(
        body, job, name=name, grid=(T // tc,),
        in_specs=[pl.BlockSpec((tc, D_SSM), lambda i: (i, ucol)), pl.BlockSpec((tc, D_SSM), lambda i: (i, zcol)),
                  _full((SSM_BLOCKS, 128, 512)), _full((SSM_BLOCKS, 128, 512)),
                  _full((SSM_BLOCKS, 512, 128)), _full((SSM_BLOCKS, 512, 128)),
                  _full((1, N_STATE)), _full((1, N_STATE)), _full((1, D_SSM)),
                  _full((D_SSM, D_SSM)), _full((1, D_SSM))],
        out_specs=[pl.BlockSpec((tc, N_STATE), lambda i: (i, 0)), pl.BlockSpec((tc, N_STATE), lambda i: (i, 0)),
                   pl.BlockSpec((tc, D_SSM), lambda i: (i, 0)), pl.BlockSpec((tc, D_SSM), lambda i: (i, 0))],
        out_shape=[jax.ShapeDtypeStruct((T, N_STATE), F32), jax.ShapeDtypeStruct((T, N_STATE), F32),
                   jax.ShapeDtypeStruct((T, D_SSM), F32), jax.ShapeDtypeStruct((T, D_SSM), BF16)],
        scratch_shapes=[pltpu.VMEM((1, N_STATE), F32), pltpu.VMEM((1, N_STATE), F32)], sem=("arbitrary",),
        operands=(proj, proj, bre, bim, cre, cimn, are, aim, d, wglu, bglu))


def _glu_bwd(do, y, proj, wglu, bglu, dproj, tm, name):
    T = y.shape[0]
    zcol = _OFF["z_ssm"] // D_SSM

    def body(do_ref, y_ref, z_ref, wg_ref, bg_ref, _, dy_ref, dz_ref, g_ref, dt_ref, db_ref):
        @pl.when(pl.program_id(0) == 0)
        def _():
            db_ref[...] = jnp.zeros_like(db_ref)

        dov = do_ref[...]
        gl, dgl = _gelu_parts(y_ref[...])
        glb = gl.astype(BF16)
        sg = _sigmoid(_dot(glb, wg_ref[...]) + bg_ref[...])
        z = z_ref[...]
        sz = _sigmoid(z)
        dz_ref[...] = (dov * (gl * sg) * (sz * (1.0 + z * (1.0 - sz)))).astype(BF16)
        dy2 = dov * (z * sz)
        dt = dy2 * gl * (sg * (1.0 - sg))
        dtb = dt.astype(BF16)
        dg = dy2 * sg + _dot(dtb, wg_ref[...], NT)
        dy_ref[...] = dg * dgl
        g_ref[...] = glb
        dt_ref[...] = dtb
        db_ref[...] += jnp.sum(dt, axis=0, keepdims=True)

    row = lambda i: (i, 0)
    return pl.pallas_call(
        body, name=name, grid=(T // tm,),
        in_specs=[pl.BlockSpec((tm, D_SSM), row), pl.BlockSpec((tm, D_SSM), row),
                  pl.BlockSpec((tm, D_SSM), lambda i: (i, zcol)), _full((D_SSM, D_SSM)), _full((1, D_SSM)),
                  pl.BlockSpec(memory_space=pl.ANY)],
        out_specs=[pl.BlockSpec((tm, D_SSM), row), pl.BlockSpec((tm, D_SSM), lambda i: (i, zcol)),
                   pl.BlockSpec((tm, D_SSM), row), pl.BlockSpec((tm, D_SSM), row), _full((1, D_SSM))],
        out_shape=[jax.ShapeDtypeStruct((T, D_SSM), F32), jax.ShapeDtypeStruct(dproj.shape, BF16),
                   jax.ShapeDtypeStruct((T, D_SSM), BF16), jax.ShapeDtypeStruct((T, D_SSM), BF16),
                   jax.ShapeDtypeStruct((1, D_SSM), F32)],
        input_output_aliases={5: 1},
        compiler_params=_params(("arbitrary",)),
    )(do, y, proj, wglu, bglu, dproj)


def _ssm_bwd(dy, proj, xr, xi, ctre, ctimn, btre, btim, are, aim, d, dproj, tc, name, job=None):
    T = dy.shape[0]
    nc = T // tc
    ucol = _OFF["u"] // D_SSM
    rb = tc // 8

    def body(dy_ref, u_ref, xr_ref, xi_ref, xpr_ref, xpi_ref, ctre_ref, ctim_ref, btre_ref, btim_ref,
             are_ref, aim_ref, d_ref, _,
             du_ref, dbre_ref, dbim_ref, dcre_ref, dcim_ref, dare_ref, daim_ref, dd_ref, gr, gi, car_r, car_i):
        i = pl.program_id(0)

        @pl.when(i == 0)
        def _():
            for ref in (car_r, car_i, dbre_ref, dbim_ref, dcre_ref, dcim_ref, dare_ref, daim_ref, dd_ref):
                ref[...] = jnp.zeros_like(ref)

        dyv = dy_ref[...]
        dyb = dyv.astype(BF16)
        u = u_ref[...]
        ub = u.astype(BF16)
        for k in range(SSM_BLOCKS):
            dk = dyb[:, 128 * k:128 * (k + 1)]
            gr[:, 512 * k:512 * (k + 1)] = _dot(dk, ctre_ref[k])
            gi[:, 512 * k:512 * (k + 1)] = _dot(dk, ctim_ref[k])
        ar, ai = are_ref[...], aim_ref[...]

        def step(s, c):
            pr, pi = c
            t = tc - 1 - s
            nr = gr[pl.ds(t, 1), :] + ar * pr + ai * pi
            ni = gi[pl.ds(t, 1), :] + ar * pi - ai * pr
            gr[pl.ds(t, 1), :] = nr
            gi[pl.ds(t, 1), :] = ni
            return nr, ni

        pr, pi = lax.fori_loop(0, tc, step, (car_r[...], car_i[...]))
        car_r[...] = pr
        car_i[...] = pi

        xrv, xiv = xr_ref[...], xi_ref[...]
        keep = jnp.where(i == nc - 1, 0.0, 1.0)
        row0 = lax.broadcasted_iota(jnp.int32, (tc, 1), 0) == 0
        xsr = jnp.where(row0, xpr_ref[7:8, :] * keep, pltpu.roll(xrv, 1, axis=0))
        xsi = jnp.where(row0, xpi_ref[7:8, :] * keep, pltpu.roll(xiv, 1, axis=0))
        grv, giv = gr[...], gi[...]
        dare_ref[...] += jnp.sum(grv * xsr + giv * xsi, axis=0, keepdims=True)
        daim_ref[...] += jnp.sum(giv * xsr - grv * xsi, axis=0, keepdims=True)
        dd_ref[...] += jnp.sum(dyv * u, axis=0, keepdims=True)

        dus = []
        for k in range(SSM_BLOCKS):
            sl = slice(512 * k, 512 * (k + 1))
            ch = slice(128 * k, 128 * (k + 1))
            grb, gib = grv[:, sl].astype(BF16), giv[:, sl].astype(BF16)
            dus.append(_dot(grb, btre_ref[k]) + _dot(gib, btim_ref[k]))
            dbre_ref[k] += _dot(grb, ub[:, ch], TN)
            dbim_ref[k] += _dot(gib, ub[:, ch], TN)
            dcre_ref[k] += _dot(dyb[:, ch], xrv[:, sl].astype(BF16), TN)
            dcim_ref[k] -= _dot(dyb[:, ch], xiv[:, sl].astype(BF16), TN)
        du_ref[...] = (jnp.concatenate(dus, axis=1) + d_ref[...] * dyv).astype(BF16)

    rev = lambda i: (nc - 1 - i, 0)
    prev = lambda i: (jnp.maximum((nc - 1 - i) * rb - 1, 0), 0)
    return _pc(
        body, job, name=name, grid=(nc,),
        in_specs=[pl.BlockSpec((tc, D_SSM), rev), pl.BlockSpec((tc, D_SSM), lambda i: (nc - 1 - i, ucol)),
                  pl.BlockSpec((tc, N_STATE), rev), pl.BlockSpec((tc, N_STATE), rev),
                  pl.BlockSpec((8, N_STATE), prev), pl.BlockSpec((8, N_STATE), prev),
                  _full((SSM_BLOCKS, 128, 512)), _full((SSM_BLOCKS, 128, 512)),
                  _full((SSM_BLOCKS, 512, 128)), _full((SSM_BLOCKS, 512, 128)),
                  _full((1, N_STATE)), _full((1, N_STATE)), _full((1, D_SSM)), pl.BlockSpec(memory_space=pl.ANY)],
        out_specs=[pl.BlockSpec((tc, D_SSM), lambda i: (nc - 1 - i, ucol)),
                   _full((SSM_BLOCKS, 512, 128)), _full((SSM_BLOCKS, 512, 128)),
                   _full((SSM_BLOCKS, 128, 512)), _full((SSM_BLOCKS, 128, 512)),
                   _full((1, N_STATE)), _full((1, N_STATE)), _full((1, D_SSM))],
        out_shape=[jax.ShapeDtypeStruct(dproj.shape, BF16),
                   jax.ShapeDtypeStruct((SSM_BLOCKS, 512, 128), F32), jax.ShapeDtypeStruct((SSM_BLOCKS, 512, 128), F32),
                   jax.ShapeDtypeStruct((SSM_BLOCKS, 128, 512), F32), jax.ShapeDtypeStruct((SSM_BLOCKS, 128, 512), F32),
                   jax.ShapeDtypeStruct((1, N_STATE), F32), jax.ShapeDtypeStruct((1, N_STATE), F32),
                   jax.ShapeDtypeStruct((1, D_SSM), F32)],
        scratch_shapes=[pltpu.VMEM((tc, N_STATE), F32), pltpu.VMEM((tc, N_STATE), F32),
                        pltpu.VMEM((1, N_STATE), F32), pltpu.VMEM((1, N_STATE), F32)], sem=("arbitrary",),
        operands=(dy, proj, xr, xi, xr, xi, ctre, ctimn, btre, btim, are, aim, d, dproj), aliases={13: 0})


def _rel_bucket(dist):
    n = jnp.maximum(dist, 0)
    max_exact = NUM_BUCKETS // 2
    n_f = jnp.maximum(n, 1).astype(F32)
    large = max_exact + (jnp.log(n_f / max_exact) / math.log(REL_MAX_DISTANCE / max_exact)
                         * (NUM_BUCKETS - max_exact)).astype(jnp.int32)
    large = jnp.minimum(large, NUM_BUCKETS - 1)
    return jnp.where(n < max_exact, n, large)


def _bucket_tables():
    qi = jnp.arange(ATTN_BLOCK)[:, None]
    kj = jnp.arange(2 * ATTN_BLOCK)[None, :]
    delta = jnp.maximum(ATTN_BLOCK + qi - kj, 0)
    return jnp.stack([_rel_bucket(delta * r) for r in ATTN_DILATIONS]).astype(jnp.int32)


def _bias_tables(rel_bias, buckets, name):
    def body(tab_ref, bk_ref, o_ref):
        g = pl.program_id(0)
        bk = bk_ref[...]
        qi = lax.broadcasted_iota(jnp.int32, bk.shape, 0)
        kj = lax.broadcasted_iota(jnp.int32, bk.shape, 1)
        delta = ATTN_BLOCK + qi - kj
        band = (delta >= 0) & (delta <= ATTN_SPAN)
        accs = [jnp.zeros(bk.shape, F32) for _ in range(4)]
        for b in range(NUM_BUCKETS):
            hit = bk == b
            for h in range(4):
                accs[h] = jnp.where(hit, tab_ref[b, 4 * g + h], accs[h])
        for h in range(4):
            o_ref[h] = jnp.where(band, accs[h], NEG_INF)

    return pl.pallas_call(
        body, name=name, grid=(3,),
        in_specs=[pl.BlockSpec(memory_space=pltpu.SMEM),
                  pl.BlockSpec((None, ATTN_BLOCK, 2 * ATTN_BLOCK), lambda g: (g, 0, 0))],
        out_specs=pl.BlockSpec((None, 4, ATTN_BLOCK, 2 * ATTN_BLOCK), lambda g: (g, 0, 0, 0)),
        out_shape=jax.ShapeDtypeStruct((3, 4, ATTN_BLOCK, 2 * ATTN_BLOCK), F32),
        compiler_params=_params(("parallel",)),
    )(rel_bias, buckets)


def _bias_grad(db0, db1, buckets, name):
    def body(a_ref, b_ref, bk_ref, o_ref):
        bk = bk_ref[...]
        for h in range(4):
            dv = a_ref[h] + b_ref[h]
            for b in range(NUM_BUCKETS):
                o_ref[h, b:b + 1, :] = jnp.sum(jnp.where(bk == b, dv, 0.0), axis=0, keepdims=True)

    tab = pl.BlockSpec((None, 4, ATTN_BLOCK, 2 * ATTN_BLOCK), lambda g: (g, 0, 0, 0))
    return pl.pallas_call(
        body, name=name, grid=(3,),
        in_specs=[tab, tab, pl.BlockSpec((None, ATTN_BLOCK, 2 * ATTN_BLOCK), lambda g: (g, 0, 0))],
        out_specs=pl.BlockSpec((None, 4, NUM_BUCKETS, 2 * ATTN_BLOCK), lambda g: (g, 0, 0, 0)),
        out_shape=jax.ShapeDtypeStruct((3, 4, NUM_BUCKETS, 2 * ATTN_BLOCK), F32),
        compiler_params=_params(("parallel",)),
    )(db0, db1, buckets)


_ATTN_SUB = {1: 4, 4: 1, 16: 1}
_UNROLL = 4


def _unit_rows(j, s, r):
    start = j * ATTN_BLOCK * r + s
    return pl.ds(start, ATTN_BLOCK, stride=r) if r > 1 else pl.ds(start, ATTN_BLOCK)


def _for_units(r, nsub, fn, after):
    if r * nsub <= _UNROLL:
        after([fn(j, s) for j in range(nsub) for s in range(r)])
    else:
        def four(i, c):
            after([fn(0, _UNROLL * i + k) for k in range(_UNROLL)])
            return c

        lax.fori_loop(0, r // _UNROLL, four, 0)


def _attn_cols(g):
    return tuple((_OFF[n] + ATTN_GROUP_WIDTH * g) // LANES for n in ("q", "k", "v"))


def _attn_fwd(proj, bias, g, name, job=None):
    r = ATTN_DILATIONS[g]
    nsub = _ATTN_SUB[r]
    T = proj.shape[0]
    sub = ATTN_BLOCK * r
    tb = sub * nsub
    qc, kc, vc = _attn_cols(g)
    scale = ATTN_HEAD_DIM ** -0.5

    def body(q_ref, kc_ref, kp_ref, vc_ref, vp_ref, bias_ref, o_ref, lse_ref):
        lane = lax.broadcasted_iota(jnp.int32, (ATTN_BLOCK, LANES), 1)
        kj = lax.broadcasted_iota(jnp.int32, (ATTN_BLOCK, 2 * ATTN_BLOCK), 1)
        dead = (pl.program_id(0) == 0) & (kj < ATTN_BLOCK)

        def one(j, s):
            rows = _unit_rows(j, s, r)
            before = _unit_rows(max(j - 1, 0), s, r)
            k_before = kc_ref[before, :] if j else kp_ref[before, :]
            v_before = vc_ref[before, :] if j else vp_ref[before, :]
            q = q_ref[rows, :]
            kcat = jnp.concatenate([k_before, kc_ref[rows, :]], axis=0).astype(BF16)
            vcat = jnp.concatenate([v_before, vc_ref[rows, :]], axis=0).astype(BF16)
            o_acc = jnp.zeros((ATTN_BLOCK, LANES), F32)
            l_acc = jnp.zeros((ATTN_BLOCK, LANES), F32)
            for hh in range(2):
                mine = (lane >= ATTN_HEAD_DIM) if hh else (lane < ATTN_HEAD_DIM)
                qm = jnp.where(mine, q, 0.0).astype(BF16)
                sc = _dot(qm, kcat, NT) * scale + bias_ref[hh]
                if j == 0:
                    sc = jnp.where(dead, NEG_INF, sc)
                m = jnp.max(sc, axis=-1, keepdims=True)
                p = jnp.exp(sc - m)
                l = jnp.sum(p, axis=-1, keepdims=True)
                o_acc = jnp.where(mine, _dot((p / l).astype(BF16), vcat), o_acc)
                l_acc = jnp.where(mine, m + jnp.log(l), l_acc)
            o_ref[rows, :] = o_acc
            lse_ref[rows, :] = l_acc

        _for_units(r, nsub, one, lambda results: None)

    cur = lambda c: pl.BlockSpec((tb, LANES), lambda b, p: (b, c + p))
    prev = lambda c: pl.BlockSpec((sub, LANES), lambda b, p: (jnp.maximum(b * nsub - 1, 0), c + p))
    out = pl.BlockSpec((tb, LANES), lambda b, p: (b, p))
    return _pc(
        body, job, name=name, grid=(T // tb, 2),
        in_specs=[cur(qc), cur(kc), prev(kc), cur(vc), prev(vc),
                  pl.BlockSpec((2, ATTN_BLOCK, 2 * ATTN_BLOCK), lambda b, p: (p, 0, 0))],
        out_specs=[out, out],
        out_shape=[jax.ShapeDtypeStruct((T, ATTN_GROUP_WIDTH), F32), jax.ShapeDtypeStruct((T, ATTN_GROUP_WIDTH), F32)],
        scratch_shapes=[], sem=("parallel", "parallel"), operands=(proj, proj, proj, proj, proj, bias))


def _attn_bwd(proj, do, corr, lse, bias, dproj, g, name):
    r = ATTN_DILATIONS[g]
    nsub = _ATTN_SUB[r]
    T = proj.shape[0]
    sub = ATTN_BLOCK * r
    tb = sub * nsub
    nb = T // tb
    qc, kc, vc = _attn_cols(g)
    dc = ATTN_GROUP_WIDTH * g // LANES
    scale = ATTN_HEAD_DIM ** -0.5

    def body(q_ref, kc_ref, kp_ref, vc_ref, vp_ref, do_ref, corr_ref, lse_ref, bias_ref, _,
             dproj_ref, db_ref, dq_s, dkc_s, dkp_s, dvc_s, dvp_s, kacc, vacc, stage, stage_sems):
        p, b = pl.program_id(0), pl.program_id(1)

        def to_dproj(e, slot, block, col):
            rows = pl.ds(pl.multiple_of(block * tb, tb), tb)
            cols = pl.ds(pl.multiple_of((col + p) * LANES, LANES), LANES)
            return pltpu.make_async_copy(stage.at[e, slot], dproj_ref.at[rows, cols], stage_sems.at[e, slot])

        def emit(e, block, col, value):
            count = p * nb + block
            slot = count % 2

            @pl.when(count >= 2)
            def _():
                to_dproj(e, slot, 0, col).wait()

            stage[e, slot] = value.astype(BF16)
            to_dproj(e, slot, block, col).start()

        def emit_keys(block):
            emit(1, block, kc, kacc[...])
            emit(2, block, vc, vacc[...])

        @pl.when(b == 0)
        def _():
            db_ref[...] = jnp.zeros_like(db_ref)
            kacc[...] = jnp.zeros_like(kacc)
            vacc[...] = jnp.zeros_like(vacc)

        @pl.when(b == nb)
        def _():
            emit_keys(nb - 1)

        @pl.when((b == nb) & (p == 1))
        def _():
            for e, col in enumerate((qc, kc, vc)):
                for slot in range(2):
                    to_dproj(e, slot, 0, col).wait()

        @pl.when(b < nb)
        def _():
            lane = lax.broadcasted_iota(jnp.int32, (ATTN_BLOCK, LANES), 1)
            kj = lax.broadcasted_iota(jnp.int32, (ATTN_BLOCK, 2 * ATTN_BLOCK), 1)
            dead = (b == 0) & (kj < ATTN_BLOCK)

            def one(j, s):
                rows = _unit_rows(j, s, r)
                before = _unit_rows(max(j - 1, 0), s, r)
                k_before = kc_ref[before, :] if j else kp_ref[before, :]
                v_before = vc_ref[before, :] if j else vp_ref[before, :]
                q = q_ref[rows, :]
                kcat = jnp.concatenate([k_before, kc_ref[rows, :]], axis=0).astype(BF16)
                vcat = jnp.concatenate([v_before, vc_ref[rows, :]], axis=0).astype(BF16)
                dov, corrv, lsev = do_ref[rows, :], corr_ref[rows, :], lse_ref[rows, :]
                dq_acc = jnp.zeros((ATTN_BLOCK, LANES), F32)
                dk_acc = jnp.zeros((2 * ATTN_BLOCK, LANES), F32)
                dv_acc = jnp.zeros((2 * ATTN_BLOCK, LANES), F32)
                dss = []
                for hh in range(2):
                    mine = (lane >= ATTN_HEAD_DIM) if hh else (lane < ATTN_HEAD_DIM)
                    col = slice(ATTN_HEAD_DIM * hh, ATTN_HEAD_DIM * hh + 1)
                    qm = jnp.where(mine, q, 0.0).astype(BF16)
                    dom = jnp.where(mine, dov, 0.0).astype(BF16)
                    sc = _dot(qm, kcat, NT) * scale + bias_ref[hh]
                    if j == 0:
                        sc = jnp.where(dead, NEG_INF, sc)
                    p = jnp.exp(sc - lsev[:, col])
                    ds = p * (_dot(dom, vcat, NT) - corrv[:, col])
                    dss.append(ds)
                    dsb = ds.astype(BF16)
                    dq_acc = jnp.where(mine, _dot(dsb, kcat) * scale, dq_acc)
                    dk_acc += _dot(dsb, qm, TN) * scale
                    dv_acc += _dot(p.astype(BF16), dom, TN)
                dq_s[rows, :] = dq_acc
                dkp_s[rows, :] = dk_acc[:ATTN_BLOCK]
                dkc_s[rows, :] = dk_acc[ATTN_BLOCK:]
                dvp_s[rows, :] = dv_acc[:ATTN_BLOCK]
                dvc_s[rows, :] = dv_acc[ATTN_BLOCK:]
                return dss

            def add_bias_grads(results):
                for hh in range(2):
                    db_ref[hh] += functools.reduce(lambda x, y: x + y, [dss[hh] for dss in results])

            _for_units(r, nsub, one, add_bias_grads)
            emit(0, b, qc, dq_s[...])
            tail = slice((nsub - 1) * sub, nsub * sub)
            kacc[tail, :] += dkp_s[0:sub, :]
            vacc[tail, :] += dvp_s[0:sub, :]

            @pl.when(b >= 1)
            def _():
                emit_keys(b - 1)

            for acc, before_s, cur_s in ((kacc, dkp_s, dkc_s), (vacc, dvp_s, dvc_s)):
                acc[...] = cur_s[...]
                for j in range(nsub - 1):
                    acc[j * sub:(j + 1) * sub, :] += before_s[(j + 1) * sub:(j + 2) * sub, :]

    last = nb - 1
    blk = (tb, LANES)
    cur = lambda c: pl.BlockSpec(blk, lambda p, b: (jnp.minimum(b, last), c + p))
    before = lambda c: pl.BlockSpec((sub, LANES), lambda p, b: (jnp.clip(b * nsub - 1, 0, nb * nsub - 1), c + p))
    tab = pl.BlockSpec((2, ATTN_BLOCK, 2 * ATTN_BLOCK), lambda p, b: (p, 0, 0))
    hbm = pl.BlockSpec(memory_space=pl.ANY)
    return pl.pallas_call(
        body, name=name, grid=(2, nb + 1),
        in_specs=[cur(qc), cur(kc), before(kc), cur(vc), before(vc), cur(dc), cur(dc), cur(0), tab, hbm],
        out_specs=[hbm, tab],
        out_shape=[jax.ShapeDtypeStruct(dproj.shape, BF16), jax.ShapeDtypeStruct((4, ATTN_BLOCK, 2 * ATTN_BLOCK), F32)],
        input_output_aliases={9: 0},
        scratch_shapes=[pltpu.VMEM(blk, F32)] * 7 + [pltpu.VMEM((3, 2) + blk, BF16), pltpu.SemaphoreType.DMA((3, 2))],
        compiler_params=_params(("arbitrary", "arbitrary")),
    )(proj, proj, proj, proj, proj, do, corr, lse, bias, dproj)


def _mix_weights(lses):
    m = jnp.maximum(jnp.maximum(lses[0], lses[1]), lses[2])
    es = [jnp.exp(l - m) for l in lses]
    inv = 1.0 / (es[0] + es[1] + es[2])
    return jnp.concatenate([e * inv for e in es], axis=1)


def _attn_mix(os, lses, proj, tm, name):
    T = proj.shape[0]
    zcol = _OFF["z_attn"] // D_ATTN

    def body(o0, o1, o2, l0, l1, l2, z_ref, out_ref):
        z = z_ref[...]
        o = jnp.concatenate([o0[...], o1[...], o2[...]], axis=1)
        alpha = _mix_weights([l0[...], l1[...], l2[...]])
        out_ref[...] = (o * alpha * (z * _sigmoid(z))).astype(BF16)

    row = lambda i: (i, 0)
    grp = pl.BlockSpec((tm, ATTN_GROUP_WIDTH), row)
    return pl.pallas_call(
        body, name=name, grid=(T // tm,),
        in_specs=[grp] * 6 + [pl.BlockSpec((tm, D_ATTN), lambda i: (i, zcol))],
        out_specs=pl.BlockSpec((tm, D_ATTN), row),
        out_shape=jax.ShapeDtypeStruct((T, D_ATTN), BF16),
        compiler_params=_params(("parallel",)),
    )(*os, *lses, proj)


def _attn_mix_bwd(d, os, lses, proj, dproj, tm, name):
    T = proj.shape[0]
    zcol = _OFF["z_attn"] // D_ATTN

    def body(d_ref, o0, o1, o2, l0, l1, l2, z_ref, _, do_ref, corr_ref, dz_ref):
        dv, z = d_ref[...], z_ref[...]
        ov = jnp.concatenate([o0[...], o1[...], o2[...]], axis=1)
        alpha = _mix_weights([l0[...], l1[...], l2[...]])
        sz = _sigmoid(z)
        oc = ov * alpha
        dz_ref[...] = (dv * oc * (sz * (1.0 + z * (1.0 - sz)))).astype(BF16)
        doc = dv * (z * sz)
        do_ref[...] = doc * alpha
        pr = doc * oc
        p3 = pr[:, 0:256] + pr[:, 256:512] + pr[:, 512:768]
        li = lax.broadcasted_iota(jnp.int32, (256, 256), 0) // ATTN_HEAD_DIM
        lj = lax.broadcasted_iota(jnp.int32, (256, 256), 1) // ATTN_HEAD_DIM
        ones = jnp.where(li == lj, 1.0, 0.0).astype(F32)
        s = lax.dot_general(p3, ones, NN, precision=lax.Precision.HIGHEST, preferred_element_type=F32)
        corr_ref[...] = alpha * jnp.concatenate([s, s, s], axis=1)

    row = lambda i: (i, 0)
    grp = pl.BlockSpec((tm, ATTN_GROUP_WIDTH), row)
    return pl.pallas_call(
        body, name=name, grid=(T // tm,),
        in_specs=[pl.BlockSpec((tm, D_ATTN), row)] + [grp] * 6 + [pl.BlockSpec((tm, D_ATTN), lambda i: (i, zcol)),
                                                                    pl.BlockSpec(memory_space=pl.ANY)],
        out_specs=[pl.BlockSpec((tm, D_ATTN), row)] * 2 + [pl.BlockSpec((tm, D_ATTN), lambda i: (i, zcol))],
        out_shape=[jax.ShapeDtypeStruct((T, D_ATTN), F32), jax.ShapeDtypeStruct((T, D_ATTN), F32),
                   jax.ShapeDtypeStruct(dproj.shape, BF16)],
        input_output_aliases={8: 2},
        compiler_params=_params(("parallel",)),
    )(d, *os, *lses, proj, dproj)


def _mem_probs(q_ref, kv_ref, h):
    hs = slice(MEM_HEAD_DIM * h, MEM_HEAD_DIM * (h + 1))
    qh = q_ref[:, hs].astype(BF16)
    kh = kv_ref[:, hs]
    vh = kv_ref[:, D_MEM + MEM_HEAD_DIM * h:D_MEM + MEM_HEAD_DIM * (h + 1)]
    s = _dot(qh, kh, NT) * (MEM_HEAD_DIM ** -0.5)
    p = jnp.exp(s - jnp.max(s, axis=-1, keepdims=True))
    pn = p / jnp.sum(p, axis=-1, keepdims=True)
    return qh, kh, vh, pn


def _mem_fwd(proj, kv, tm, name):
    T = proj.shape[0]
    M = kv.shape[0]
    qcol, zcol = _OFF["q_mem"] // D_MEM, _OFF["z_mem"] // D_MEM

    def body(q_ref, z_ref, kv_ref, o_ref):
        outs = []
        for h in range(MEM_HEADS):
            _, _, vh, pn = _mem_probs(q_ref, kv_ref, h)
            outs.append(_dot(pn.astype(BF16), vh))
        z = z_ref[...]
        o_ref[...] = (jnp.concatenate(outs, axis=1) * (z * _sigmoid(z))).astype(BF16)

    return pl.pallas_call(
        body, name=name, grid=(T // tm,),
        in_specs=[pl.BlockSpec((tm, D_MEM), lambda i: (i, qcol)), pl.BlockSpec((tm, D_MEM), lambda i: (i, zcol)),
                  _full((M, 2 * D_MEM))],
        out_specs=pl.BlockSpec((tm, D_MEM), lambda i: (i, 0)),
        out_shape=jax.ShapeDtypeStruct((T, D_MEM), BF16),
        compiler_params=_params(("parallel",)),
    )(proj, proj, kv)


def _mem_bwd(d, proj, kv, tm, name):
    T = proj.shape[0]
    M = kv.shape[0]
    qcol, zcol = _OFF["q_mem"] // D_MEM, _OFF["z_mem"] // D_MEM

    def body(d_ref, q_ref, z_ref, kv_ref, dq_ref, dz_ref, dkv_ref):
        @pl.when(pl.program_id(0) == 0)
        def _():
            dkv_ref[...] = jnp.zeros_like(dkv_ref)

        z = z_ref[...]
        sz = _sigmoid(z)
        dv = d_ref[...]
        dov = dv * (z * sz)
        scale = MEM_HEAD_DIM ** -0.5
        outs, dqs = [], []
        for h in range(MEM_HEADS):
            hs = slice(MEM_HEAD_DIM * h, MEM_HEAD_DIM * (h + 1))
            qh, kh, vh, pn = _mem_probs(q_ref, kv_ref, h)
            pnb = pn.astype(BF16)
            oh = _dot(pnb, vh)
            outs.append(oh)
            doh = dov[:, hs]
            dohb = doh.astype(BF16)
            dp = _dot(dohb, vh, NT)
            ds = pn * (dp - jnp.sum(doh * oh, axis=-1, keepdims=True))
            dsb = ds.astype(BF16)
            dqs.append(_dot(dsb, kh) * scale)
            dkv_ref[:, hs] += _dot(dsb, qh, TN) * scale
            vs = slice(D_MEM + MEM_HEAD_DIM * h, D_MEM + MEM_HEAD_DIM * (h + 1))
            dkv_ref[:, vs] += _dot(pnb, dohb, TN)
        dq_ref[...] = jnp.concatenate(dqs, axis=1).astype(BF16)
        dz_ref[...] = (dv * jnp.concatenate(outs, axis=1) * (sz * (1.0 + z * (1.0 - sz)))).astype(BF16)

    row = lambda i: (i, 0)
    return pl.pallas_call(
        body, name=name, grid=(T // tm,),
        in_specs=[pl.BlockSpec((tm, D_MEM), row), pl.BlockSpec((tm, D_MEM), lambda i: (i, qcol)),
                  pl.BlockSpec((tm, D_MEM), lambda i: (i, zcol)), _full((M, 2 * D_MEM))],
        out_specs=[pl.BlockSpec((tm, D_MEM), row), pl.BlockSpec((tm, D_MEM), row), _full((M, 2 * D_MEM))],
        out_shape=[jax.ShapeDtypeStruct((T, D_MEM), BF16), jax.ShapeDtypeStruct((T, D_MEM), BF16),
                   jax.ShapeDtypeStruct((M, 2 * D_MEM), F32)],
        compiler_params=_params(("arbitrary",)),
    )(d, proj, proj, kv)


def _branches_and_gates(os_ref, oa_ref, om_ref, gl_refs, bg_ref, ws_ref, wa_ref, wm_ref):
    outs = (_dot(os_ref[...], ws_ref[...]), _dot(oa_ref[...], wa_ref[...]), _dot(om_ref[...], wm_ref[...]))
    gates = tuple(_sigmoid(jnp.concatenate([gl_refs[2 * k][...], gl_refs[2 * k + 1][...]], axis=1)
                           + bg_ref[:, D_MODEL * k:D_MODEL * (k + 1)]) for k in range(3))
    return outs, gates


def _merge_specs(tm):
    row = lambda i: (i, 0)
    first = _OFF["gates"] // GATE_BLOCK
    gate = [pl.BlockSpec((tm, GATE_BLOCK), (lambda i, k=k: (i, first + k))) for k in range(N_GATES // GATE_BLOCK)]
    return ([pl.BlockSpec((tm, D_SSM), row), pl.BlockSpec((tm, D_ATTN), row), pl.BlockSpec((tm, D_MEM), row)] + gate
            + [_full((1, N_GATES)), _full((D_SSM, D_MODEL)), _full((D_ATTN, D_MODEL)), _full((D_MEM, D_MODEL)),
               _full((D_MODEL, D_MODEL))])


def _merge_fwd(x, o_ssm, o_attn, o_mem, proj, bg, ws, wa, wm, wo, tm, name):
    T = x.shape[0]

    def body(os_ref, oa_ref, om_ref, g0, g1, g2, g3, g4, g5, bg_ref, ws_ref, wa_ref, wm_ref, wo_ref, x_ref,
             xo_ref, mg_ref):
        outs, gates = _branches_and_gates(os_ref, oa_ref, om_ref, (g0, g1, g2, g3, g4, g5), bg_ref, ws_ref, wa_ref,
                                          wm_ref)
        merged = (gates[0] * outs[0] + gates[1] * outs[1] + gates[2] * outs[2]).astype(BF16)
        mg_ref[...] = merged
        xo_ref[...] = x_ref[...] + _dot(merged, wo_ref[...])

    row = lambda i: (i, 0)
    return pl.pallas_call(
        body, name=name, grid=(T // tm,),
        in_specs=_merge_specs(tm) + [pl.BlockSpec((tm, D_MODEL), row)],
        out_specs=[pl.BlockSpec((tm, D_MODEL), row), pl.BlockSpec((tm, D_MODEL), row)],
        out_shape=[jax.ShapeDtypeStruct((T, D_MODEL), F32), jax.ShapeDtypeStruct((T, D_MODEL), BF16)],
        compiler_params=_params(("parallel",)),
    )(o_ssm, o_attn, o_mem, *([proj] * (N_GATES // GATE_BLOCK)), bg, ws, wa, wm, wo, x)


def _merge_bwd(dx, o_ssm, o_attn, o_mem, proj, bg, ws, wa, wm, wo, tm, name, job=None):
    T = dx.shape[0]

    n = T // tm

    def body(os_ref, oa_ref, om_ref, g0, g1, g2, g3, g4, g5, bg_ref, ws_ref, wa_ref, wm_ref, wo_ref, dx_ref,
             dproj_ref, db_ref, dos_ref, doa_ref, dom_ref, dbg_ref, dgl_buf, dgl_sems):
        i = pl.program_id(0)
        slot = i % 2

        def to_dproj(s, row0):
            return pltpu.make_async_copy(dgl_buf.at[s], dproj_ref.at[pl.ds(row0, tm), pl.ds(_OFF["gates"], N_GATES)],
                                         dgl_sems.at[s])

        @pl.when(i == 0)
        def _():
            dbg_ref[...] = jnp.zeros_like(dbg_ref)

        @pl.when(i >= 2)
        def _():
            to_dproj(slot, 0).wait()

        outs, gates = _branches_and_gates(os_ref, oa_ref, om_ref, (g0, g1, g2, g3, g4, g5), bg_ref, ws_ref, wa_ref,
                                          wm_ref)
        dm = _dot(dx_ref[...].astype(BF16), wo_ref[...], NT)
        w_refs = (ws_ref, wa_ref, wm_ref)
        do_refs = (dos_ref, doa_ref, dom_ref)
        for k in range(3):
            cols = slice(D_MODEL * k, D_MODEL * (k + 1))
            dgl = dm * outs[k] * (gates[k] * (1.0 - gates[k]))
            dgl_buf[slot, :, cols] = dgl.astype(BF16)
            dbg_ref[:, cols] += jnp.sum(dgl, axis=0, keepdims=True)
            dbk = (dm * gates[k]).astype(BF16)
            db_ref[:, cols] = dbk
            do_refs[k][...] = _dot(dbk, w_refs[k][...], NT)
        to_dproj(slot, pl.multiple_of(i * tm, tm)).start()

        @pl.when(i == n - 1)
        def _():
            for s in range(min(2, n)):
                to_dproj(s, 0).wait()

    row = lambda i: (i, 0)
    return _pc(
        body, job, name=name, grid=(n,),
        in_specs=_merge_specs(tm) + [pl.BlockSpec((tm, D_MODEL), row)],
        out_specs=[pl.BlockSpec(memory_space=pl.ANY), pl.BlockSpec((tm, N_GATES), row), pl.BlockSpec((tm, D_SSM), row),
                   pl.BlockSpec((tm, D_ATTN), row), pl.BlockSpec((tm, D_MEM), row), _full((1, N_GATES))],
        out_shape=[jax.ShapeDtypeStruct((T, D_IN), BF16), jax.ShapeDtypeStruct((T, N_GATES), BF16),
                   jax.ShapeDtypeStruct((T, D_SSM), F32), jax.ShapeDtypeStruct((T, D_ATTN), F32),
                   jax.ShapeDtypeStruct((T, D_MEM), F32), jax.ShapeDtypeStruct((1, N_GATES), F32)],
        scratch_shapes=[pltpu.VMEM((2, tm, N_GATES), BF16), pltpu.SemaphoreType.DMA((2,))], sem=("arbitrary",),
        operands=(o_ssm, o_attn, o_mem, *([proj] * (N_GATES // GATE_BLOCK)), bg, ws, wa, wm, wo, dx))


def _loss_head(x, g, target, tm, name):
    T, D = x.shape

    def body(x_ref, g_ref, t_ref, loss_ref, dx_ref, dg_ref):
        @pl.when(pl.program_id(0) == 0)
        def _():
            loss_ref[...] = jnp.zeros_like(loss_ref)
            dg_ref[...] = jnp.zeros_like(dg_ref)

        xv = x_ref[...]
        r = lax.rsqrt(jnp.mean(xv * xv, axis=-1, keepdims=True) + EPS)
        xr = xv * r
        err = xr * g_ref[...] - t_ref[...]
        loss_ref[...] += 0.5 * jnp.sum(jnp.mean(err * err, axis=-1, keepdims=True), axis=0, keepdims=True)
        dy = err * (1.0 / D)
        dg_ref[...] += jnp.sum(dy * xr, axis=0, keepdims=True)
        wv = dy * g_ref[...]
        dx_ref[...] = r * (wv - xr * jnp.mean(wv * xr, axis=-1, keepdims=True))

    row = lambda i: (i, 0)
    return pl.pallas_call(
        body, name=name, grid=(T // tm,),
        in_specs=[pl.BlockSpec((tm, D), row), _full((1, D)), pl.BlockSpec((tm, D), row)],
        out_specs=[_full((1, 128)), pl.BlockSpec((tm, D), row), _full((1, D))],
        out_shape=[jax.ShapeDtypeStruct((1, 128), F32), jax.ShapeDtypeStruct((T, D), F32),
                   jax.ShapeDtypeStruct((1, D), F32)],
        compiler_params=_params(("arbitrary",)),
    )(x, g, target)


def _adamw(parts, w, m, v, tr, name):
    L, R, C = w.shape

    def body(p_ref, w_ref, m_ref, v_ref, g_ref, d_ref, mo_ref, vo_ref):
        g = p_ref[0].astype(F32)
        for s in range(1, N_DEV):
            g = g + p_ref[s].astype(F32)
        mn = ADAM_B1 * m_ref[...] + (1.0 - ADAM_B1) * g
        vn = ADAM_B2 * v_ref[...] + (1.0 - ADAM_B2) * (g * g)
        m_hat = mn / (1.0 - ADAM_B1 ** ADAM_STEP)
        v_hat = vn / (1.0 - ADAM_B2 ** ADAM_STEP)
        g_ref[...] = g
        d_ref[...] = -ADAM_LR * (m_hat / (jnp.sqrt(v_hat) + ADAM_EPS) + ADAM_WD * w_ref[...])
        mo_ref[...] = mn
        vo_ref[...] = vn

    one = pl.BlockSpec((None, tr, C), lambda l, i: (l, i, 0))
    return pl.pallas_call(
        body, name=name, grid=(L, R // tr),
        in_specs=[pl.BlockSpec((N_DEV, None, tr, C), lambda l, i: (0, l, i, 0)), one, one, one],
        out_specs=[one] * 4,
        out_shape=[jax.ShapeDtypeStruct((L, R, C), F32)] * 4,
        compiler_params=_params(("parallel", "parallel")),
    )(parts, w, m, v)


_SHARDED = (("w_in", (1088, 1024), 1), ("w_glu", (96, 768), 0), ("w_mem_kv", (128, 1024), 0),
            ("w_br_ssm", (768, 128), 1), ("w_br_attn", (768, 128), 1), ("w_br_mem", (512, 128), 1),
            ("w_out", (128, 1024), 0))
_W_IN = 0
_SMALL = tuple(range(1, len(_SHARDED)))


class _Job(NamedTuple):
    ins: list
    out_shape: list
    aliases: dict
    pairs: Callable
    n: int


def _peers():
    x, y, c = lax.axis_index("x"), lax.axis_index("y"), lax.axis_index("c")
    me = 4 * x + 2 * y + c
    out = []
    for k in range(1, N_DEV):
        px = 1 - x if k & 4 else x
        py = 1 - y if k & 2 else y
        pc = 1 - c if k & 1 else c
        out.append(((px, py, pc), 4 * px + 2 * py + pc))
    return me, out


def _copies(pairs, send_sems, recv_sems, local_sems, arrivals):
    me, peers = _peers()
    local = [pltpu.make_async_copy(src(me), dst(me), local_sems.at[j]) for j, (src, dst) in enumerate(pairs)]
    sends, recvs = [], []
    for k, (peer, lin) in enumerate(peers):
        for j, (src, dst) in enumerate(pairs):
            for to, out in ((dst(me), sends), (dst(lin), recvs)):
                if out is sends or arrivals:
                    out.append(pltpu.make_async_remote_copy(
                        src_ref=src(lin), dst_ref=to, send_sem=send_sems.at[j, k], recv_sem=recv_sems.at[j, k],
                        device_id=peer, device_id_type=pl.DeviceIdType.MESH))
    return local, sends, recvs


def _start_copies(pairs, *sems):
    local, sends, _ = _copies(pairs, *sems, arrivals=False)
    for cp in local + sends:
        cp.start()


def _wait_copies(pairs, *sems):
    local, sends, recvs = _copies(pairs, *sems, arrivals=True)
    for cp in recvs:
        cp.wait_recv()
    for cp in sends:
        cp.wait_send()
    for cp in local:
        cp.wait()


def _job_scratch(job):
    return [pltpu.SemaphoreType.DMA((job.n, N_DEV - 1)), pltpu.SemaphoreType.DMA((job.n, N_DEV - 1)),
            pltpu.SemaphoreType.DMA((job.n,))]


def _pc(body, job, *, name, grid, in_specs, out_specs, out_shape, scratch_shapes, sem, operands, aliases=None):
    aliases = aliases or {}
    if job is None:
        return pl.pallas_call(body, name=name, grid=grid, in_specs=in_specs, out_specs=out_specs, out_shape=out_shape,
                              scratch_shapes=scratch_shapes, input_output_aliases=aliases,
                              compiler_params=_params(sem))(*operands)
    a = len(in_specs)
    b = a + len(job.ins)
    c = b + len(out_shape)
    d = c + len(job.out_shape)
    e = d + len(scratch_shapes)

    def carried(*refs):
        pairs = job.pairs(refs[a:b], refs[c:d])
        ids = [pl.program_id(k) for k in range(len(grid))]
        first = functools.reduce(jnp.logical_and, [i == 0 for i in ids])
        last = functools.reduce(jnp.logical_and, [i == n - 1 for i, n in zip(ids, grid)])

        @pl.when(first)
        def _():
            _start_copies(pairs, *refs[e:])

        body(*refs[:a], *refs[b:c], *refs[d:e])

        @pl.when(last)
        def _():
            _wait_copies(pairs, *refs[e:])

    hbm = pl.BlockSpec(memory_space=pl.ANY)
    outs = pl.pallas_call(
        carried, name=name, grid=grid,
        in_specs=list(in_specs) + [hbm] * len(job.ins), out_specs=list(out_specs) + [hbm] * len(job.out_shape),
        out_shape=list(out_shape) + list(job.out_shape),
        input_output_aliases={**aliases, **{a + i: len(out_shape) + o for i, o in job.aliases.items()}},
        scratch_shapes=list(scratch_shapes) + _job_scratch(job),
        compiler_params=_params(("arbitrary",) * len(grid)),
    )(*operands, *job.ins)
    return outs[:len(out_shape)], outs[len(out_shape):]


def _gather_via_sibling(x, take, place, out_shape, name, landing=None):
    def body(*refs):
        x_ref, o_ref = refs[0], refs[-4]
        send_sems, recv_sems, local_sem = refs[-3:]
        x, y, c = lax.axis_index("x"), lax.axis_index("y"), lax.axis_index("c")
        me, sibling = (x, y, c), (x, y, 1 - c)
        chips = [(1 - x, y), (x, 1 - y), (1 - x, 1 - y)]
        src = take(x_ref)

        def slot(px, py, pc):
            return place(o_ref, 4 * px + 2 * py + pc)

        def copy(k, block, to, first_hand):
            return pltpu.make_async_remote_copy(
                src_ref=src if first_hand else slot(*block), dst_ref=slot(*block), send_sem=send_sems.at[k],
                recv_sem=recv_sems.at[k], device_id=to, device_id_type=pl.DeviceIdType.MESH)

        mine = pltpu.make_async_copy(src, slot(*me), local_sem)
        mine.start()
        first = [copy(0, me, sibling, True)] + [copy(1 + j, me, (*chip, c), True) for j, chip in enumerate(chips)]
        for cp in first:
            cp.start()
        passed = []
        for j, chip in enumerate(chips):
            copy(1 + j, (*chip, c), me, True).wait_recv()
            passed.append(copy(4 + j, (*chip, c), sibling, False))
            passed[-1].start()
        copy(0, sibling, me, True).wait_recv()
        for j, chip in enumerate(chips):
            copy(4 + j, (*chip, 1 - c), me, False).wait_recv()
        for cp in first + passed:
            cp.wait_send()
        mine.wait()

    hbm = pl.BlockSpec(memory_space=pl.ANY)
    ins = [x] if landing is None else [x, landing]
    return pl.pallas_call(
        body, name=name, in_specs=[hbm] * len(ins), out_specs=hbm, out_shape=out_shape,
        input_output_aliases={} if landing is None else {1: 0},
        scratch_shapes=[pltpu.SemaphoreType.DMA((N_DEV - 1,)), pltpu.SemaphoreType.DMA((N_DEV - 1,)),
                        pltpu.SemaphoreType.DMA],
    )(*ins)


def _lane_window(ref, who):
    return ref.at[:, pl.ds(pl.multiple_of(who * LANES, LANES), LANES)]


def _gather_job(shards, items):
    out_shape = []
    for i, _ in items:
        _, s, axis = _SHARDED[i]
        whole = i != _W_IN and axis == 1
        out_shape.append(jax.ShapeDtypeStruct((s[0], N_DEV * s[1]) if whole else (N_DEV,) + s, BF16))

    def pairs(in_refs, out_refs):
        out = []
        for (i, l), src, dst in zip(items, in_refs, out_refs):
            if i != _W_IN and _SHARDED[i][2] == 1:
                out.append((lambda who, src=src, l=l: src.at[l], lambda who, dst=dst: _lane_window(dst, who)))
            else:
                out.append((lambda who, src=src, l=l: src.at[l], lambda who, dst=dst: dst.at[who]))
        return out

    return _Job([shards[i] for i, _ in items], out_shape, {}, pairs, len(items))


def _landed_weights(items, landed):
    out = {}
    for (i, _), a in zip(items, landed):
        n, s, axis = _SHARDED[i]
        if i == _W_IN:
            out[n] = a.reshape(D_IN, D_MODEL)
        elif axis == 0:
            out[n] = a.reshape(N_DEV * s[0], s[1])
        else:
            out[n] = a
    return out


def _scatter_job(grads, items, layer, parts=None):
    ng = len(grads)
    out_shape = [jax.ShapeDtypeStruct((N_DEV, DEPTH) + _SHARDED[i][1], BF16) for i in items]

    def pairs(in_refs, out_refs):
        out = []
        for i, src, dst in zip(items, in_refs[:ng], out_refs):
            _, s, axis = _SHARDED[i]
            if i == _W_IN:
                take = lambda who, src=src: src.at[who]
            elif axis == 0:
                take = lambda who, src=src, s=s: src.at[pl.ds(pl.multiple_of(who * s[0], 16), s[0])]
            else:
                take = lambda who, src=src: _lane_window(src, who)
            out.append((take, lambda who, dst=dst: dst.at[who, layer]))
        return out

    aliases = {} if parts is None else {ng + j: j for j in range(len(items))}
    return _Job(list(grads) + ([] if parts is None else list(parts)), out_shape, aliases, pairs, len(items))


def _rows_job(src, row0, landing=None):
    n = src.shape[0]
    pairs = lambda in_refs, out_refs: [(lambda who: in_refs[0], lambda who: out_refs[0].at[who, pl.ds(row0, n)])]
    return _Job([src] + ([] if landing is None else [landing]), [jax.ShapeDtypeStruct((N_DEV, _REP_ROWS, LANES), F32)],
                {} if landing is None else {1: 0}, pairs, 1)


_REPLICATED = (("norm_g", (2, 1024)), ("mem_norm_g", (2, 1024)), ("b_gate", (2, 3072)),
               ("ssm_lambda_re", (2, 48, 64)), ("ssm_lambda_im", (2, 48, 64)), ("ssm_log_dt", (2, 48)),
               ("ssm_b_re", (2, 48, 64, 16)), ("ssm_b_im", (2, 48, 64, 16)), ("ssm_c_re", (2, 48, 16, 64)),
               ("ssm_c_im", (2, 48, 16, 64)), ("ssm_d", (2, 768)), ("b_glu", (2, 768)), ("rel_bias", (32, 12)),
               ("final_norm_g", (1024,)))
_PER_LAYER = tuple((n, s[1:]) for n, s in _REPLICATED if s[0] == DEPTH and len(s) > 1)
_SHARED = tuple((n, s) for n, s in _REPLICATED if (n, s[1:]) not in _PER_LAYER)
_REP_HALF_ROWS = 1664
_REP_ROWS = 2 * _REP_HALF_ROWS
assert sum(int(np.prod(s)) for _, s in _PER_LAYER + _SHARED) <= _REP_HALF_ROWS * LANES


def _pack_half(tree, layer, shared):
    flat = [tree[n][layer].reshape(-1) for n, _ in _PER_LAYER]
    if shared:
        flat += [tree[n].reshape(-1) for n, _ in _SHARED]
    flat = jnp.concatenate(flat)
    return jnp.pad(flat, (0, _REP_HALF_ROWS * LANES - flat.shape[0])).reshape(_REP_HALF_ROWS, LANES)


def _pack_replicated(tree):
    return jnp.concatenate([_pack_half(tree, 1, False), _pack_half(tree, 0, True)])[None]


def _unpack_replicated(packed):
    halves = packed.reshape(2, -1)
    out, r = {}, 0
    for n, s in _PER_LAYER:
        size = int(np.prod(s))
        out[n] = jnp.stack([halves[1, r:r + size].reshape(s), halves[0, r:r + size].reshape(s)])
        r += size
    for n, s in _SHARED:
        size = int(np.prod(s))
        out[n] = halves[1, r:r + size].reshape(s)
        r += size
    return out


def _discretize(lam_re, lam_im, log_dt, b_re, b_im):
    dt = jnp.exp(log_dt)[:, None]
    mag = jnp.exp(lam_re * dt)
    abar_re, abar_im = mag * jnp.cos(lam_im * dt), mag * jnp.sin(lam_im * dt)
    den = lam_re * lam_re + lam_im * lam_im
    nr, ni = abar_re - 1.0, abar_im
    f_re = (nr * lam_re + ni * lam_im) / den
    f_im = (ni * lam_re - nr * lam_im) / den
    bbar_re = f_re[..., None] * b_re - f_im[..., None] * b_im
    bbar_im = f_re[..., None] * b_im + f_im[..., None] * b_re
    return abar_re, abar_im, bbar_re, bbar_im


def _block_diag(a):
    _, R, C = a.shape
    a = a.reshape(SSM_BLOCKS, 8, R, C)
    eye = jnp.eye(8, dtype=a.dtype)
    return (a[:, :, :, None, :] * eye[None, :, None, :, None]).reshape(SSM_BLOCKS, 8 * R, 8 * C)


def _diag_blocks(a, R, C):
    a = a.reshape(SSM_BLOCKS, 8, R, 8, C)
    eye = jnp.eye(8, dtype=a.dtype)
    return jnp.sum(a * eye[None, :, None, :, None], axis=3).reshape(SSM_GROUPS, R, C)


def _carried(result, job):
    return (result, None) if job is None else result


def _layer_fwd(x, mem, W, P, bias, layer, jobs):
    tag = f"l{layer}"
    abar_re, abar_im, bbar_re, bbar_im = _discretize(P["ssm_lambda_re"][layer], P["ssm_lambda_im"][layer],
                                                     P["ssm_log_dt"][layer], P["ssm_b_re"][layer], P["ssm_b_im"][layer])
    c_re, c_im = P["ssm_c_re"][layer], P["ssm_c_im"][layer]
    ssm = dict(
        are=abar_re.reshape(1, N_STATE), aim=abar_im.reshape(1, N_STATE),
        bre=_block_diag(bbar_re.transpose(0, 2, 1)).astype(BF16), bim=_block_diag(bbar_im.transpose(0, 2, 1)).astype(BF16),
        cre=_block_diag(c_re.transpose(0, 2, 1)).astype(BF16), cimn=_block_diag(-c_im.transpose(0, 2, 1)).astype(BF16),
        ctre=_block_diag(c_re).astype(BF16), ctimn=_block_diag(-c_im).astype(BF16),
        btre=_block_diag(bbar_re).astype(BF16), btim=_block_diag(bbar_im).astype(BF16),
        d=P["ssm_d"][layer].reshape(1, D_SSM))
    bglu = P["b_glu"][layer].reshape(1, D_SSM)
    bgate = P["b_gate"][layer].reshape(1, N_GATES)
    g = P["norm_g"][layer].reshape(1, D_MODEL)
    gm = P["mem_norm_g"][layer].reshape(1, D_MODEL)
    delivered = {}

    def carry(stage):
        return jobs[stage][0] if stage in jobs else None

    def deliver(stage, landed):
        if landed is not None:
            delivered[stage] = _landed_weights(jobs[stage][1], landed)

    T = x.shape[0]
    (proj, h), landed = _carried(_norm_proj(x, g, W["w_in"], min(T, 1024), 2176, f"{tag}_proj", job=carry("proj"),
                                            w_turned=True), carry("proj"))
    deliver("proj", landed)
    W = {**W, **delivered.get("proj", {})}
    (xr, xi, y, o_ssm), landed = _carried(
        _ssm_fwd(proj, ssm["bre"], ssm["bim"], ssm["cre"], ssm["cimn"], ssm["are"], ssm["aim"], ssm["d"], W["w_glu"],
                 bglu, 256, f"{tag}_ssm", job=carry("ssm")), carry("ssm"))
    deliver("ssm", landed)
    os, lses = [], []
    for grp in range(3):
        stage = f"attn{grp}"
        (o_g, lse_g), landed = _carried(_attn_fwd(proj, bias[grp], grp, f"{tag}_{stage}", job=carry(stage)), carry(stage))
        deliver(stage, landed)
        os.append(o_g)
        lses.append(lse_g)
    o_attn = _attn_mix(os, lses, proj, 512, f"{tag}_attn_mix")
    kvb, hm = _norm_proj(mem, gm, W["w_mem_kv"], mem.shape[0], 1024, f"{tag}_mem_kv", out_dtype=BF16)
    o_mem = _mem_fwd(proj, kvb, 512, f"{tag}_mem")
    x_out, merged = _merge_fwd(x, o_ssm, o_attn, o_mem, proj, bgate, W["w_br_ssm"], W["w_br_attn"], W["w_br_mem"],
                               W["w_out"], 256, f"{tag}_merge")
    res = dict(x=x, mem=mem, proj=proj, h=h, xr=xr, xi=xi, y=y, o_ssm=o_ssm, os=os, lses=lses,
               o_attn=o_attn, kvb=kvb, hm=hm, o_mem=o_mem, merged=merged, ssm=ssm, bglu=bglu,
               bgate=bgate, g=g, gm=gm, W=W)
    return x_out, res, delivered


def _layer_bwd(dx, res, P, bias, layer, jobs):
    tag = f"l{layer}b"
    proj, ssm, W = res["proj"], res["ssm"], res["W"]
    T = dx.shape[0]
    landed = {}

    def run(stage, fn, job):
        out, landed[stage] = _carried(fn(job), job)
        if job is None:
            del landed[stage]
        return out

    dproj, dbr, do_ssm, do_attn, do_mem, dbg = run(
        "merge", lambda job: _merge_bwd(dx, res["o_ssm"], res["o_attn"], res["o_mem"], proj, res["bgate"], W["w_br_ssm"],
                                        W["w_br_attn"], W["w_br_mem"], W["w_out"], 256, f"{tag}_merge", job=job),
        jobs.get("merge"))
    gw = {}
    tk = min(T, 1024)
    gw["w_out"] = _mm_tn(res["merged"], dx, 1024, 1024, tk, f"{tag}_dw_out")
    gw["w_br_ssm"] = _mm_tn(res["o_ssm"], dbr, 768, 1024, tk, f"{tag}_dw_br_ssm", b_col=0, n=1024)
    gw["w_br_attn"] = _mm_tn(res["o_attn"], dbr, 768, 1024, tk, f"{tag}_dw_br_attn", b_col=1024, n=1024)
    gw["w_br_mem"] = _mm_tn(res["o_mem"], dbr, 512, 1024, tk, f"{tag}_dw_br_mem", b_col=2048, n=1024)

    dqm, dzm, dkv = _mem_bwd(do_mem, proj, res["kvb"], 512, f"{tag}_mem")
    M = dkv.shape[0]
    gw["w_mem_kv"] = _mm_tn(res["hm"], dkv, 1024, 1024, M, f"{tag}_dw_mem_kv")
    _, dgm = _proj_bwd(dkv.astype(BF16), W["w_mem_kv"], res["mem"], res["gm"], jnp.zeros_like(res["mem"]), M, 1024,
                       f"{tag}_mem_norm")

    do_g, corr, dproj = _attn_mix_bwd(do_attn, res["os"], res["lses"], proj, dproj, 512, f"{tag}_attn_mix")
    dbs = []
    for grp in range(3):
        dproj, db_g = _attn_bwd(proj, do_g, corr, res["lses"][grp], bias[grp], dproj, grp, f"{tag}_attn{grp}")
        dbs.append(db_g)
    dbias = jnp.stack(dbs)

    dy, dproj, gelu_b, dt_b, dbglu = _glu_bwd(do_ssm, res["y"], proj, W["w_glu"], res["bglu"], dproj, 512, f"{tag}_glu")
    gw["w_glu"] = _mm_tn(gelu_b, dt_b, 768, 768, tk, f"{tag}_dw_glu")
    dproj, dbre, dbim, dcre, dcim, dare, daim, dd = run(
        "ssm", lambda job: _ssm_bwd(dy, proj, res["xr"], res["xi"], ssm["ctre"], ssm["ctimn"], ssm["btre"], ssm["btim"],
                                    ssm["are"], ssm["aim"], ssm["d"], dproj, 256, f"{tag}_ssm", job=job),
        jobs.get("ssm"))
    _, disc_vjp = jax.vjp(_discretize, P["ssm_lambda_re"][layer], P["ssm_lambda_im"][layer], P["ssm_log_dt"][layer],
                          P["ssm_b_re"][layer], P["ssm_b_im"][layer])
    d_lre, d_lim, d_ldt, d_bre, d_bim = disc_vjp((dare.reshape(SSM_GROUPS, SSM_STATE), daim.reshape(SSM_GROUPS, SSM_STATE),
                                                  _diag_blocks(dbre, SSM_STATE, SSM_GROUP),
                                                  _diag_blocks(dbim, SSM_STATE, SSM_GROUP)))

    small = [gw[_SHARDED[i][0]] for i in _SMALL]
    for seg, piece in (("q_mem", dqm), ("z_mem", dzm)):
        dproj = lax.dynamic_update_slice(dproj, piece, (0, _OFF[seg]))
    dw_in = run("dw_in", lambda job: _mm_tn(dproj, res["h"], 2176, 1024, min(T, 1024), f"{tag}_dw_in", job=job),
                jobs["dw_in"](small) if "dw_in" in jobs else None)
    dw_in = dw_in.reshape((N_DEV,) + _SHARDED[_W_IN][1])
    dx_in, dg = run("proj", lambda job: _proj_bwd(dproj, W["w_in"], res["x"], res["g"], dx, min(T, 1024), 2176,
                                                  f"{tag}_proj", job=job, w_turned=True),
                    jobs["proj"](small, dw_in, landed) if "proj" in jobs else None)

    gp = dict(norm_g=dg[0], mem_norm_g=dgm[0], b_gate=dbg[0], ssm_lambda_re=d_lre, ssm_lambda_im=d_lim,
              ssm_log_dt=d_ldt, ssm_b_re=d_bre, ssm_b_im=d_bim,
              ssm_c_re=_diag_blocks(dcre, SSM_GROUP, SSM_STATE), ssm_c_im=_diag_blocks(dcim, SSM_GROUP, SSM_STATE),
              ssm_d=dd[0], b_glu=dbglu[0])
    return dx_in, dw_in, gp, dbias, landed


def _train_step(x, mem, target, shards, P):
    rest0 = [(i, 0) for i in _SMALL]
    thirds1 = [[(i, 1) for i in _SMALL[k::3]] for k in range(3)]
    first = [(_W_IN, 0)]
    w_in0 = _gather_via_sibling(shards[_W_IN], lambda ref: ref.at[0], lambda ref, s: ref.at[s],
                                jax.ShapeDtypeStruct((N_DEV,) + _SHARDED[_W_IN][1], BF16), "gather_w_in0")
    W0 = _landed_weights(first, [w_in0])
    buckets = _bucket_tables()
    bias = _bias_tables(P["rel_bias"], buckets, "bias_tables")
    jobs0 = {"proj": (_gather_job(shards, rest0), rest0), "ssm": (_gather_job(shards, [(_W_IN, 1)]), [(_W_IN, 1)]),
             **{f"attn{k}": (_gather_job(shards, items), items) for k, items in enumerate(thirds1)}}
    x, res0, delivered = _layer_fwd(x, mem, W0, P, bias, 0, jobs0)
    W1 = {**delivered["ssm"], **delivered["attn0"], **delivered["attn1"], **delivered["attn2"]}
    x, res1, _ = _layer_fwd(x, mem, W1, P, bias, 1, {})
    loss, dx, dgf = _loss_head(x, P["final_norm_g"].reshape(1, D_MODEL), target, 512, "loss_head")

    dx, dw_in1, gp1, dbias1, landed1 = _layer_bwd(
        dx, res1, P, bias, 1, {"proj": lambda small, dw_in, landed: _scatter_job(small, _SMALL, 1)})
    rep1 = _pack_half({n: a[None] for n, a in gp1.items()}, 0, False)
    dx, _, gp0, dbias0, landed0 = _layer_bwd(
        dx, res0, P, bias, 0,
        {"merge": _rows_job(rep1, 0), "ssm": _scatter_job([dw_in1], [_W_IN], 1),
         "dw_in": lambda small: _scatter_job(small, _SMALL, 0, parts=landed1["proj"]),
         "proj": lambda small, dw_in, landed: _scatter_job([dw_in], [_W_IN], 0, parts=landed["ssm"])})
    d_rel = _bias_grad(dbias0, dbias1, buckets, "bias_grad")
    gp0 = {n: a[None] for n, a in gp0.items()}
    gp0["rel_bias"] = jnp.sum(d_rel, axis=-1).transpose(2, 0, 1).reshape(NUM_BUCKETS, 12)
    gp0["final_norm_g"] = dgf[0]
    rep0 = _pack_half(gp0, 0, True)
    rparts = _gather_via_sibling(rep0, lambda ref: ref, lambda ref, s: ref.at[s, pl.ds(_REP_HALF_ROWS, _REP_HALF_ROWS)],
                                 jax.ShapeDtypeStruct((N_DEV, _REP_ROWS, LANES), F32), "gather_small_grads0",
                                 landing=landed0["merge"][0])
    return loss[0, 0], dx, list(landed0["proj"]) + list(landed0["dw_in"]), rparts


_WEIGHTS = ["norm_g", "mem_norm_g", "w_in", "b_gate", "ssm_lambda_re", "ssm_lambda_im", "ssm_log_dt", "ssm_b_re",
            "ssm_b_im", "ssm_c_re", "ssm_c_im", "ssm_d", "w_glu", "b_glu", "w_mem_kv", "w_br_ssm", "w_br_attn",
            "w_br_mem", "w_out", "rel_bias", "final_norm_g"]
_ADAM_ROWS = {"w_in": 136,"w_glu": 96, "w_mem_kv": 128, "w_br_ssm": 768, "w_br_attn": 768, "w_br_mem": 512,
              "w_out": 128}


def kernel(x, mem, norm_g, mem_norm_g, w_in, b_gate, ssm_lambda_re, ssm_lambda_im, ssm_log_dt, ssm_b_re, ssm_b_im, ssm_c_re, ssm_c_im, ssm_d, w_glu, b_glu, w_mem_kv, w_br_ssm, w_br_attn, w_br_mem, w_out, rel_bias, final_norm_g, loss_target, m_norm_g, m_mem_norm_g, m_w_in, m_b_gate, m_ssm_lambda_re, m_ssm_lambda_im, m_ssm_log_dt, m_ssm_b_re, m_ssm_b_im, m_ssm_c_re, m_ssm_c_im, m_ssm_d, m_w_glu, m_b_glu, m_w_mem_kv, m_w_br_ssm, m_w_br_attn, m_w_br_mem, m_w_out, m_rel_bias, m_final_norm_g, v_norm_g, v_mem_norm_g, v_w_in, v_b_gate, v_ssm_lambda_re, v_ssm_lambda_im, v_ssm_log_dt, v_ssm_b_re, v_ssm_b_im, v_ssm_c_re, v_ssm_c_im, v_ssm_d, v_w_glu, v_b_glu, v_w_mem_kv, v_w_br_ssm, v_w_br_attn, v_w_br_mem, v_w_out, v_rel_bias, v_final_norm_g):
    given = dict(locals())
    w = {n: given[n] for n in _WEIGHTS}
    m = {n: given["m_" + n] for n in _WEIGHTS}
    v = {n: given["v_" + n] for n in _WEIGHTS}

    turned = lambda n, a: a.swapaxes(1, 2) if n == "w_in" else a
    shards = [turned(n, w[n]).astype(BF16) for n, _, _ in _SHARDED]
    loss, dx, parts, rparts = _train_step(x[0], mem[0], loss_target[0], shards, w)
    loss = lax.psum(loss, ("x", "y", "c"))

    new = {}
    for (n, _, _), p in zip(_SHARDED, parts):
        new[n] = [turned(n, a) for a in _adamw(p, turned(n, w[n]), turned(n, m[n]), turned(n, v[n]), _ADAM_ROWS[n],
                                               f"adamw_{n}")]
    rp = [_unpack_replicated(a) for a in _adamw(rparts[:, None], _pack_replicated(w), _pack_replicated(m),
                                                _pack_replicated(v), _REP_ROWS // 4, "adamw_replicated")]
    for n, _ in _REPLICATED:
        new[n] = [rp[kind][n] for kind in range(4)]
    outs = [loss, dx[None]]
    for kind in range(4):
        outs.extend(new[n][kind] for n in _WEIGHTS)
    return tuple(outs)
```

```python
import functools
import math
from typing import Callable, NamedTuple

import jax
import jax.numpy as jnp
import numpy as np
from jax import lax
from jax.experimental import pallas as pl
from jax.experimental.pallas import tpu as pltpu

F32 = jnp.float32
BF16 = jnp.bfloat16

D_MODEL = 1024
DEPTH = 2
EPS = 1e-6
D_SSM = 768
SSM_GROUP = 16
SSM_GROUPS = 48
SSM_STATE = 64
N_STATE = SSM_GROUPS * SSM_STATE
SSM_BLOCKS = 6
D_ATTN = 768
ATTN_HEAD_DIM = 64
ATTN_GROUP_WIDTH = 256
ATTN_DILATIONS = (1, 4, 16)
ATTN_SPAN = 128
ATTN_BLOCK = 128
NUM_BUCKETS = 32
REL_MAX_DISTANCE = 2048
NEG_INF = -1e30
MEM_HEADS = 4
MEM_HEAD_DIM = 128
D_MEM = 512
N_GATES = 3 * D_MODEL
D_IN = 8704
N_DEV = 8
LANES = 128
ADAM_LR = 0.001
ADAM_B1 = 0.9
ADAM_B2 = 0.999
ADAM_EPS = 1e-08
ADAM_WD = 0.01
ADAM_STEP = 10

_OFF = {"u": 0, "z_ssm": 768, "q": 1536, "k": 2304, "v": 3072, "z_attn": 3840, "q_mem": 4608, "z_mem": 5120,
        "gates": 5632}
GATE_BLOCK = 512

NN = (((1,), (0,)), ((), ()))
NT = (((1,), (1,)), ((), ()))
TN = (((0,), (0,)), ((), ()))

VMEM_LIMIT = 56 * 1024 * 1024


def _dot(a, b, dims=NN):
    return lax.dot_general(a, b, dims, preferred_element_type=F32)


def _sigmoid(x):
    return 1.0 / (1.0 + jnp.exp(-x))


def _gelu_parts(x):
    k = math.sqrt(2.0 / math.pi)
    t = jnp.tanh(k * (x + 0.044715 * (x * x * x)))
    cdf = 0.5 * (1.0 + t)
    dcdf = 0.5 * (1.0 - t * t) * k * (1.0 + 3.0 * 0.044715 * (x * x))
    return x * cdf, cdf + x * dcdf


def _params(sem, vmem=VMEM_LIMIT):
    return pltpu.CompilerParams(dimension_semantics=sem, vmem_limit_bytes=vmem)


def _full(shape):
    return pl.BlockSpec(shape, lambda *_: (0,) * len(shape))


def _norm_proj(x, g, w, tm, tn, name, out_dtype=F32, job=None, w_turned=False):
    T, D = x.shape
    N = w.shape[0] if w_turned else w.shape[1]
    w_spec = pl.BlockSpec((tn, D), lambda i, j: (j, 0)) if w_turned else pl.BlockSpec((D, tn), lambda i, j: (0, j))
    dims = NT if w_turned else NN

    def body(x_ref, g_ref, w_ref, o_ref, h_ref, hs):
        @pl.when(pl.program_id(1) == 0)
        def _():
            xv = x_ref[...]
            r = lax.rsqrt(jnp.mean(xv * xv, axis=-1, keepdims=True) + EPS)
            hv = (xv * r * g_ref[...]).astype(BF16)
            hs[...] = hv
            h_ref[...] = hv

        o_ref[...] = _dot(hs[...], w_ref[...], dims).astype(out_dtype)

    return _pc(
        body, job, name=name, grid=(T // tm, N // tn),
        in_specs=[pl.BlockSpec((tm, D), lambda i, j: (i, 0)), _full((1, D)), w_spec],
        out_specs=[pl.BlockSpec((tm, tn), lambda i, j: (i, j)), pl.BlockSpec((tm, D), lambda i, j: (i, 0))],
        out_shape=[jax.ShapeDtypeStruct((T, N), out_dtype), jax.ShapeDtypeStruct((T, D), BF16)],
        scratch_shapes=[pltpu.VMEM((tm, D), BF16)], sem=("parallel", "arbitrary"), operands=(x, g, w))


def _mm_tn(a, b, tm, tn, tk, name, b_col=0, n=None, job=None):
    K, M = a.shape
    N = b.shape[1] if n is None else n
    nk = K // tk
    j0 = b_col // tn

    def body(a_ref, b_ref, o_ref, acc):
        k = pl.program_id(2)

        @pl.when(k == 0)
        def _():
            acc[...] = jnp.zeros_like(acc)

        acc[...] += _dot(a_ref[...].astype(BF16), b_ref[...].astype(BF16), TN)

        @pl.when(k == nk - 1)
        def _():
            o_ref[...] = acc[...].astype(BF16)

    out = _pc(
        body, job, name=name, grid=(M // tm, N // tn, nk),
        in_specs=[pl.BlockSpec((tk, tm), lambda i, j, k: (k, i)), pl.BlockSpec((tk, tn), lambda i, j, k: (k, j0 + j))],
        out_specs=[pl.BlockSpec((tm, tn), lambda i, j, k: (i, j))],
        out_shape=[jax.ShapeDtypeStruct((M, N), BF16)],
        scratch_shapes=[pltpu.VMEM((tm, tn), F32)], sem=("parallel", "parallel", "arbitrary"), operands=(a, b))
    return out[0] if job is None else (out[0][0], out[1])


def _proj_bwd(dp, w, x, g, dres, tm, tk, name, job=None, w_turned=False):
    T, N = dp.shape
    D = x.shape[1]
    nk = N // tk
    w_spec = pl.BlockSpec((tk, D), lambda i, k: (k, 0)) if w_turned else pl.BlockSpec((D, tk), lambda i, k: (0, k))
    dims = NN if w_turned else NT

    def body(dp_ref, w_ref, x_ref, g_ref, dres_ref, dx_ref, dg_ref, acc):
        i, k = pl.program_id(0), pl.program_id(1)

        @pl.when(k == 0)
        def _():
            acc[...] = jnp.zeros_like(acc)

        @pl.when((i == 0) & (k == 0))
        def _():
            dg_ref[...] = jnp.zeros_like(dg_ref)

        acc[...] += _dot(dp_ref[...], w_ref[...], dims)

        @pl.when(k == nk - 1)
        def _():
            xv = x_ref[...]
            dh = acc[...]
            r = lax.rsqrt(jnp.mean(xv * xv, axis=-1, keepdims=True) + EPS)
            xr = xv * r
            dg_ref[...] += jnp.sum(dh * xr, axis=0, keepdims=True)
            wv = dh * g_ref[...]
            dx_ref[...] = dres_ref[...] + r * (wv - xr * jnp.mean(wv * xr, axis=-1, keepdims=True))

    return _pc(
        body, job, name=name, grid=(T // tm, nk),
        in_specs=[pl.BlockSpec((tm, tk), lambda i, k: (i, k)), w_spec,
                  pl.BlockSpec((tm, D), lambda i, k: (i, 0)), _full((1, D)),
                  pl.BlockSpec((tm, D), lambda i, k: (i, 0))],
        out_specs=[pl.BlockSpec((tm, D), lambda i, k: (i, 0)), _full((1, D))],
        out_shape=[jax.ShapeDtypeStruct((T, D), F32), jax.ShapeDtypeStruct((1, D), F32)],
        scratch_shapes=[pltpu.VMEM((tm, D), F32)], sem=("arbitrary", "arbitrary"), operands=(dp, w, x, g, dres))


def _ssm_fwd(proj, bre, bim, cre, cimn, are, aim, d, wglu, bglu, tc, name, job=None):
    T = proj.shape[0]
    ucol, zcol = _OFF["u"] // D_SSM, _OFF["z_ssm"] // D_SSM

    def body(u_ref, z_ref, bre_ref, bim_ref, cre_ref, cim_ref, are_ref, aim_ref, d_ref, wg_ref, bg_ref,
             xr_ref, xi_ref, y_ref, o_ref, car_r, car_i):
        @pl.when(pl.program_id(0) == 0)
        def _():
            car_r[...] = jnp.zeros_like(car_r)
            car_i[...] = jnp.zeros_like(car_i)

        u = u_ref[...]
        ub = u.astype(BF16)
        for k in range(SSM_BLOCKS):
            uk = ub[:, 128 * k:128 * (k + 1)]
            xr_ref[:, 512 * k:512 * (k + 1)] = _dot(uk, bre_ref[k])
            xi_ref[:, 512 * k:512 * (k + 1)] = _dot(uk, bim_ref[k])
        ar, ai = are_ref[...], aim_ref[...]

        def step(t, c):
            pr, pi = c
            nr = ar * pr - ai * pi + xr_ref[pl.ds(t, 1), :]
            ni = ar * pi + ai * pr + xi_ref[pl.ds(t, 1), :]
            xr_ref[pl.ds(t, 1), :] = nr
            xi_ref[pl.ds(t, 1), :] = ni
            return nr, ni

        pr, pi = lax.fori_loop(0, tc, step, (car_r[...], car_i[...]))
        car_r[...] = pr
        car_i[...] = pi

        ys = []
        for k in range(SSM_BLOCKS):
            xrk = xr_ref[:, 512 * k:512 * (k + 1)].astype(BF16)
            xik = xi_ref[:, 512 * k:512 * (k + 1)].astype(BF16)
            ys.append(_dot(xrk, cre_ref[k]) + _dot(xik, cim_ref[k]))
        y = jnp.concatenate(ys, axis=1) + d_ref[...] * u
        y_ref[...] = y
        gl, _ = _gelu_parts(y)
        t = _dot(gl.astype(BF16), wg_ref[...]) + bg_ref[...]
        z = z_ref[...]
        o_ref[...] = (gl * _sigmoid(t) * (z * _sigmoid(z))).astype(BF16)

    return _pc(
        body, job, name=name, grid=(T // tc,),
        in_specs=[pl.BlockSpec((tc, D_SSM), lambda i: (i, ucol)), pl.BlockSpec((tc, D_SSM), lambda i: (i, zcol)),
                  _full((SSM_BLOCKS, 128, 512)), _full((SSM_BLOCKS, 128, 512)),
                  _full((SSM_BLOCKS, 512, 128)), _full((SSM_BLOCKS, 512, 128)),
                  _full((1, N_STATE)), _full((1, N_STATE)), _full((1, D_SSM)),
                  _full((D_SSM, D_SSM)), _full((1, D_SSM))],
        out_specs=[pl.BlockSpec((tc, N_STATE), lambda i: (i, 0)), pl.BlockSpec((tc, N_STATE), lambda i: (i, 0)),
                   pl.BlockSpec((tc, D_SSM), lambda i: (i, 0)), pl.BlockSpec((tc, D_SSM), lambda i: (i, 0))],
        out_shape=[jax.ShapeDtypeStruct((T, N_STATE), F32), jax.ShapeDtypeStruct((T, N_STATE), F32),
                   jax.ShapeDtypeStruct((T, D_SSM), F32), jax.ShapeDtypeStruct((T, D_SSM), BF16)],
        scratch_shapes=[pltpu.VMEM((1, N_STATE), F32), pltpu.VMEM((1, N_STATE), F32)], sem=("arbitrary",),
        operands=(proj, proj, bre, bim, cre, cimn, are, aim, d, wglu, bglu))


def _glu_bwd(do, y, proj, wglu, bglu, dproj, tm, name):
    T = y.shape[0]
    zcol = _OFF["z_ssm"] // D_SSM

    def body(do_ref, y_ref, z_ref, wg_ref, bg_ref, _, dy_ref, dz_ref, g_ref, dt_ref, db_ref):
        @pl.when(pl.program_id(0) == 0)
        def _():
            db_ref[...] = jnp.zeros_like(db_ref)

        dov = do_ref[...]
        gl, dgl = _gelu_parts(y_ref[...])
        glb = gl.astype(BF16)
        sg = _sigmoid(_dot(glb, wg_ref[...]) + bg_ref[...])
        z = z_ref[...]
        sz = _sigmoid(z)
        dz_ref[...] = (dov * (gl * sg) * (sz * (1.0 + z * (1.0 - sz)))).astype(BF16)
        dy2 = dov * (z * sz)
        dt = dy2 * gl * (sg * (1.0 - sg))
        dtb = dt.astype(BF16)
        dg = dy2 * sg + _dot(dtb, wg_ref[...], NT)
        dy_ref[...] = dg * dgl
        g_ref[...] = glb
        dt_ref[...] = dtb
        db_ref[...] += jnp.sum(dt, axis=0, keepdims=True)

    row = lambda i: (i, 0)
    return pl.pallas_call(
        body, name=name, grid=(T // tm,),
        in_specs=[pl.BlockSpec((tm, D_SSM), row), pl.BlockSpec((tm, D_SSM), row),
                  pl.BlockSpec((tm, D_SSM), lambda i: (i, zcol)), _full((D_SSM, D_SSM)), _full((1, D_SSM)),
                  pl.BlockSpec(memory_space=pl.ANY)],
        out_specs=[pl.BlockSpec((tm, D_SSM), row), pl.BlockSpec((tm, D_SSM), lambda i: (i, zcol)),
                   pl.BlockSpec((tm, D_SSM), row), pl.BlockSpec((tm, D_SSM), row), _full((1, D_SSM))],
        out_shape=[jax.ShapeDtypeStruct((T, D_SSM), F32), jax.ShapeDtypeStruct(dproj.shape, BF16),
                   jax.ShapeDtypeStruct((T, D_SSM), BF16), jax.ShapeDtypeStruct((T, D_SSM), BF16),
                   jax.ShapeDtypeStruct((1, D_SSM), F32)],
        input_output_aliases={5: 1},
        compiler_params=_params(("arbitrary",)),
    )(do, y, proj, wglu, bglu, dproj)


def _ssm_bwd(dy, proj, xr, xi, ctre, ctimn, btre, btim, are, aim, d, dproj, tc, name, job=None):
    T = dy.shape[0]
    nc = T // tc
    ucol = _OFF["u"] // D_SSM
    rb = tc // 8

    def body(dy_ref, u_ref, xr_ref, xi_ref, xpr_ref, xpi_ref, ctre_ref, ctim_ref, btre_ref, btim_ref,
             are_ref, aim_ref, d_ref, _,
             du_ref, dbre_ref, dbim_ref, dcre_ref, dcim_ref, dare_ref, daim_ref, dd_ref, gr, gi, car_r, car_i):
        i = pl.program_id(0)

        @pl.when(i == 0)
        def _():
            for ref in (car_r, car_i, dbre_ref, dbim_ref, dcre_ref, dcim_ref, dare_ref, daim_ref, dd_ref):
                ref[...] = jnp.zeros_like(ref)

        dyv = dy_ref[...]
        dyb = dyv.astype(BF16)
        u = u_ref[...]
        ub = u.astype(BF16)
        for k in range(SSM_BLOCKS):
            dk = dyb[:, 128 * k:128 * (k + 1)]
            gr[:, 512 * k:512 * (k + 1)] = _dot(dk, ctre_ref[k])
            gi[:, 512 * k:512 * (k + 1)] = _dot(dk, ctim_ref[k])
        ar, ai = are_ref[...], aim_ref[...]

        def step(s, c):
            pr, pi = c
            t = tc - 1 - s
            nr = gr[pl.ds(t, 1), :] + ar * pr + ai * pi
            ni = gi[pl.ds(t, 1), :] + ar * pi - ai * pr
            gr[pl.ds(t, 1), :] = nr
            gi[pl.ds(t, 1), :] = ni
            return nr, ni

        pr, pi = lax.fori_loop(0, tc, step, (car_r[...], car_i[...]))
        car_r[...] = pr
        car_i[...] = pi

        keep = jnp.where(i == nc - 1, 0.0, 1.0)
        row0 = lax.broadcasted_iota(jnp.int32, (tc, 1), 0) == 0
        dd_ref[...] += jnp.sum(dyv * u, axis=0, keepdims=True)
        for k in range(SSM_BLOCKS):
            sl = slice(512 * k, 512 * (k + 1))
            ch = slice(128 * k, 128 * (k + 1))
            xrk, xik, grk, gik = xr_ref[:, sl], xi_ref[:, sl], gr[:, sl], gi[:, sl]
            xsr = jnp.where(row0, xpr_ref[7:8, sl] * keep, pltpu.roll(xrk, 1, axis=0))
            xsi = jnp.where(row0, xpi_ref[7:8, sl] * keep, pltpu.roll(xik, 1, axis=0))
            dare_ref[:, sl] += jnp.sum(grk * xsr + gik * xsi, axis=0, keepdims=True)
            daim_ref[:, sl] += jnp.sum(gik * xsr - grk * xsi, axis=0, keepdims=True)
            grb, gib = grk.astype(BF16), gik.astype(BF16)
            du_ref[:, ch] = (_dot(grb, btre_ref[k]) + _dot(gib, btim_ref[k])
                             + d_ref[:, ch] * dyv[:, ch]).astype(BF16)
            dbre_ref[k] += _dot(grb, ub[:, ch], TN)
            dbim_ref[k] += _dot(gib, ub[:, ch], TN)
            dcre_ref[k] += _dot(dyb[:, ch], xrk.astype(BF16), TN)
            dcim_ref[k] -= _dot(dyb[:, ch], xik.astype(BF16), TN)

    rev = lambda i: (nc - 1 - i, 0)
    prev = lambda i: (jnp.maximum((nc - 1 - i) * rb - 1, 0), 0)
    return _pc(
        body, job, name=name, grid=(nc,),
        in_specs=[pl.BlockSpec((tc, D_SSM), rev), pl.BlockSpec((tc, D_SSM), lambda i: (nc - 1 - i, ucol)),
                  pl.BlockSpec((tc, N_STATE), rev), pl.BlockSpec((tc, N_STATE), rev),
                  pl.BlockSpec((8, N_STATE), prev), pl.BlockSpec((8, N_STATE), prev),
                  _full((SSM_BLOCKS, 128, 512)), _full((SSM_BLOCKS, 128, 512)),
                  _full((SSM_BLOCKS, 512, 128)), _full((SSM_BLOCKS, 512, 128)),
                  _full((1, N_STATE)), _full((1, N_STATE)), _full((1, D_SSM)), pl.BlockSpec(memory_space=pl.ANY)],
        out_specs=[pl.BlockSpec((tc, D_SSM), lambda i: (nc - 1 - i, ucol)),
                   _full((SSM_BLOCKS, 512, 128)), _full((SSM_BLOCKS, 512, 128)),
                   _full((SSM_BLOCKS, 128, 512)), _full((SSM_BLOCKS, 128, 512)),
                   _full((1, N_STATE)), _full((1, N_STATE)), _full((1, D_SSM))],
        out_shape=[jax.ShapeDtypeStruct(dproj.shape, BF16),
                   jax.ShapeDtypeStruct((SSM_BLOCKS, 512, 128), F32), jax.ShapeDtypeStruct((SSM_BLOCKS, 512, 128), F32),
                   jax.ShapeDtypeStruct((SSM_BLOCKS, 128, 512), F32), jax.ShapeDtypeStruct((SSM_BLOCKS, 128, 512), F32),
                   jax.ShapeDtypeStruct((1, N_STATE), F32), jax.ShapeDtypeStruct((1, N_STATE), F32),
                   jax.ShapeDtypeStruct((1, D_SSM), F32)],
        scratch_shapes=[pltpu.VMEM((tc, N_STATE), F32), pltpu.VMEM((tc, N_STATE), F32),
                        pltpu.VMEM((1, N_STATE), F32), pltpu.VMEM((1, N_STATE), F32)], sem=("arbitrary",),
        operands=(dy, proj, xr, xi, xr, xi, ctre, ctimn, btre, btim, are, aim, d, dproj), aliases={13: 0})


def _rel_bucket(dist):
    n = jnp.maximum(dist, 0)
    max_exact = NUM_BUCKETS // 2
    n_f = jnp.maximum(n, 1).astype(F32)
    large = max_exact + (jnp.log(n_f / max_exact) / math.log(REL_MAX_DISTANCE / max_exact)
                         * (NUM_BUCKETS - max_exact)).astype(jnp.int32)
    large = jnp.minimum(large, NUM_BUCKETS - 1)
    return jnp.where(n < max_exact, n, large)


def _bucket_tables():
    qi = jnp.arange(ATTN_BLOCK)[:, None]
    kj = jnp.arange(2 * ATTN_BLOCK)[None, :]
    delta = jnp.maximum(ATTN_BLOCK + qi - kj, 0)
    return jnp.stack([_rel_bucket(delta * r) for r in ATTN_DILATIONS]).astype(jnp.int32)


def _bias_tables(rel_bias, buckets, name):
    def body(tab_ref, bk_ref, o_ref):
        g = pl.program_id(0)
        bk = bk_ref[...]
        qi = lax.broadcasted_iota(jnp.int32, bk.shape, 0)
        kj = lax.broadcasted_iota(jnp.int32, bk.shape, 1)
        delta = ATTN_BLOCK + qi - kj
        band = (delta >= 0) & (delta <= ATTN_SPAN)
        accs = [jnp.zeros(bk.shape, F32) for _ in range(4)]
        for b in range(NUM_BUCKETS):
            hit = bk == b
            for h in range(4):
                accs[h] = jnp.where(hit, tab_ref[b, 4 * g + h], accs[h])
        for h in range(4):
            o_ref[h] = jnp.where(band, accs[h], NEG_INF)

    return pl.pallas_call(
        body, name=name, grid=(3,),
        in_specs=[pl.BlockSpec(memory_space=pltpu.SMEM),
                  pl.BlockSpec((None, ATTN_BLOCK, 2 * ATTN_BLOCK), lambda g: (g, 0, 0))],
        out_specs=pl.BlockSpec((None, 4, ATTN_BLOCK, 2 * ATTN_BLOCK), lambda g: (g, 0, 0, 0)),
        out_shape=jax.ShapeDtypeStruct((3, 4, ATTN_BLOCK, 2 * ATTN_BLOCK), F32),
        compiler_params=_params(("parallel",)),
    )(rel_bias, buckets)


def _bias_grad(db0, db1, buckets, name):
    def body(a_ref, b_ref, bk_ref, o_ref):
        bk = bk_ref[...]
        for h in range(4):
            dv = a_ref[h] + b_ref[h]
            for b in range(NUM_BUCKETS):
                o_ref[h, b:b + 1, :] = jnp.sum(jnp.where(bk == b, dv, 0.0), axis=0, keepdims=True)

    tab = pl.BlockSpec((None, 4, ATTN_BLOCK, 2 * ATTN_BLOCK), lambda g: (g, 0, 0, 0))
    return pl.pallas_call(
        body, name=name, grid=(3,),
        in_specs=[tab, tab, pl.BlockSpec((None, ATTN_BLOCK, 2 * ATTN_BLOCK), lambda g: (g, 0, 0))],
        out_specs=pl.BlockSpec((None, 4, NUM_BUCKETS, 2 * ATTN_BLOCK), lambda g: (g, 0, 0, 0)),
        out_shape=jax.ShapeDtypeStruct((3, 4, NUM_BUCKETS, 2 * ATTN_BLOCK), F32),
        compiler_params=_params(("parallel",)),
    )(db0, db1, buckets)


_ATTN_SUB = {1: 4, 4: 1, 16: 1}
_UNROLL = 4


def _unit_rows(j, s, r):
    start = j * ATTN_BLOCK * r + s
    return pl.ds(start, ATTN_BLOCK, stride=r) if r > 1 else pl.ds(start, ATTN_BLOCK)


def _for_units(r, nsub, fn, after):
    if r * nsub <= _UNROLL:
        after([fn(j, s) for j in range(nsub) for s in range(r)])
    else:
        def four(i, c):
            after([fn(0, _UNROLL * i + k) for k in range(_UNROLL)])
            return c

        lax.fori_loop(0, r // _UNROLL, four, 0)


def _attn_cols(g):
    return tuple((_OFF[n] + ATTN_GROUP_WIDTH * g) // LANES for n in ("q", "k", "v"))


def _attn_fwd(proj, bias, g, name, job=None):
    r = ATTN_DILATIONS[g]
    nsub = _ATTN_SUB[r]
    T = proj.shape[0]
    sub = ATTN_BLOCK * r
    tb = sub * nsub
    qc, kc, vc = _attn_cols(g)
    scale = ATTN_HEAD_DIM ** -0.5

    def body(q_ref, kc_ref, kp_ref, vc_ref, vp_ref, bias_ref, o_ref, lse_ref):
        lane = lax.broadcasted_iota(jnp.int32, (ATTN_BLOCK, LANES), 1)
        kj = lax.broadcasted_iota(jnp.int32, (ATTN_BLOCK, 2 * ATTN_BLOCK), 1)
        dead = (pl.program_id(0) == 0) & (kj < ATTN_BLOCK)

        def one(j, s):
            rows = _unit_rows(j, s, r)
            before = _unit_rows(max(j - 1, 0), s, r)
            k_before = kc_ref[before, :] if j else kp_ref[before, :]
            v_before = vc_ref[before, :] if j else vp_ref[before, :]
            q = q_ref[rows, :]
            kcat = jnp.concatenate([k_before, kc_ref[rows, :]], axis=0).astype(BF16)
            vcat = jnp.concatenate([v_before, vc_ref[rows, :]], axis=0).astype(BF16)
            o_acc = jnp.zeros((ATTN_BLOCK, LANES), F32)
            l_acc = jnp.zeros((ATTN_BLOCK, LANES), F32)
            for hh in range(2):
                mine = (lane >= ATTN_HEAD_DIM) if hh else (lane < ATTN_HEAD_DIM)
                qm = jnp.where(mine, q, 0.0).astype(BF16)
                sc = _dot(qm, kcat, NT) * scale + bias_ref[hh]
                if j == 0:
                    sc = jnp.where(dead, NEG_INF, sc)
                m = jnp.max(sc, axis=-1, keepdims=True)
                p = jnp.exp(sc - m)
                l = jnp.sum(p, axis=-1, keepdims=True)
                o_acc = jnp.where(mine, _dot((p / l).astype(BF16), vcat), o_acc)
                l_acc = jnp.where(mine, m + jnp.log(l), l_acc)
            o_ref[rows, :] = o_acc
            lse_ref[rows, :] = l_acc

        _for_units(r, nsub, one, lambda results: None)

    cur = lambda c: pl.BlockSpec((tb, LANES), lambda b, p: (b, c + p))
    prev = lambda c: pl.BlockSpec((sub, LANES), lambda b, p: (jnp.maximum(b * nsub - 1, 0), c + p))
    out = pl.BlockSpec((tb, LANES), lambda b, p: (b, p))
    return _pc(
        body, job, name=name, grid=(T // tb, 2),
        in_specs=[cur(qc), cur(kc), prev(kc), cur(vc), prev(vc),
                  pl.BlockSpec((2, ATTN_BLOCK, 2 * ATTN_BLOCK), lambda b, p: (p, 0, 0))],
        out_specs=[out, out],
        out_shape=[jax.ShapeDtypeStruct((T, ATTN_GROUP_WIDTH), F32), jax.ShapeDtypeStruct((T, ATTN_GROUP_WIDTH), F32)],
        scratch_shapes=[], sem=("parallel", "parallel"), operands=(proj, proj, proj, proj, proj, bias))


def _attn_bwd(proj, do, corr, lse, bias, dproj, g, name):
    r = ATTN_DILATIONS[g]
    nsub = _ATTN_SUB[r]
    T = proj.shape[0]
    sub = ATTN_BLOCK * r
    tb = sub * nsub
    nb = T // tb
    qc, kc, vc = _attn_cols(g)
    dc = ATTN_GROUP_WIDTH * g // LANES
    scale = ATTN_HEAD_DIM ** -0.5

    def body(q_ref, kc_ref, kp_ref, vc_ref, vp_ref, do_ref, corr_ref, lse_ref, bias_ref, _,
             dproj_ref, db_ref, dq_s, dkc_s, dkp_s, dvc_s, dvp_s, kacc, vacc, stage, stage_sems):
        p, b = pl.program_id(0), pl.program_id(1)

        def to_dproj(e, slot, block, col):
            rows = pl.ds(pl.multiple_of(block * tb, tb), tb)
            cols = pl.ds(pl.multiple_of((col + p) * LANES, LANES), LANES)
            return pltpu.make_async_copy(stage.at[e, slot], dproj_ref.at[rows, cols], stage_sems.at[e, slot])

        def emit(e, block, col, value):
            count = p * nb + block
            slot = count % 2

            @pl.when(count >= 2)
            def _():
                to_dproj(e, slot, 0, col).wait()

            stage[e, slot] = value.astype(BF16)
            to_dproj(e, slot, block, col).start()

        def emit_keys(block):
            emit(1, block, kc, kacc[...])
            emit(2, block, vc, vacc[...])

        @pl.when(b == 0)
        def _():
            db_ref[...] = jnp.zeros_like(db_ref)
            kacc[...] = jnp.zeros_like(kacc)
            vacc[...] = jnp.zeros_like(vacc)

        @pl.when(b == nb)
        def _():
            emit_keys(nb - 1)

        @pl.when((b == nb) & (p == 1))
        def _():
            for e, col in enumerate((qc, kc, vc)):
                for slot in range(2):
                    to_dproj(e, slot, 0, col).wait()

        @pl.when(b < nb)
        def _():
            lane = lax.broadcasted_iota(jnp.int32, (ATTN_BLOCK, LANES), 1)
            kj = lax.broadcasted_iota(jnp.int32, (ATTN_BLOCK, 2 * ATTN_BLOCK), 1)
            dead = (b == 0) & (kj < ATTN_BLOCK)

            def one(j, s):
                rows = _unit_rows(j, s, r)
                before = _unit_rows(max(j - 1, 0), s, r)
                k_before = kc_ref[before, :] if j else kp_ref[before, :]
                v_before = vc_ref[before, :] if j else vp_ref[before, :]
                q = q_ref[rows, :]
                kcat = jnp.concatenate([k_before, kc_ref[rows, :]], axis=0).astype(BF16)
                vcat = jnp.concatenate([v_before, vc_ref[rows, :]], axis=0).astype(BF16)
                dov, corrv, lsev = do_ref[rows, :], corr_ref[rows, :], lse_ref[rows, :]
                dq_acc = jnp.zeros((ATTN_BLOCK, LANES), F32)
                dk_acc = jnp.zeros((2 * ATTN_BLOCK, LANES), F32)
                dv_acc = jnp.zeros((2 * ATTN_BLOCK, LANES), F32)
                dss = []
                for hh in range(2):
                    mine = (lane >= ATTN_HEAD_DIM) if hh else (lane < ATTN_HEAD_DIM)
                    col = slice(ATTN_HEAD_DIM * hh, ATTN_HEAD_DIM * hh + 1)
                    qm = jnp.where(mine, q, 0.0).astype(BF16)
                    dom = jnp.where(mine, dov, 0.0).astype(BF16)
                    sc = _dot(qm, kcat, NT) * scale + bias_ref[hh]
                    if j == 0:
                        sc = jnp.where(dead, NEG_INF, sc)
                    p = jnp.exp(sc - lsev[:, col])
                    ds = p * (_dot(dom, vcat, NT) - corrv[:, col])
                    dss.append(ds)
                    dsb = ds.astype(BF16)
                    dq_acc = jnp.where(mine, _dot(dsb, kcat) * scale, dq_acc)
                    dk_acc += _dot(dsb, qm, TN) * scale
                    dv_acc += _dot(p.astype(BF16), dom, TN)
                dq_s[rows, :] = dq_acc
                dkp_s[rows, :] = dk_acc[:ATTN_BLOCK]
                dkc_s[rows, :] = dk_acc[ATTN_BLOCK:]
                dvp_s[rows, :] = dv_acc[:ATTN_BLOCK]
                dvc_s[rows, :] = dv_acc[ATTN_BLOCK:]
                return dss

            def add_bias_grads(results):
                for hh in range(2):
                    db_ref[hh] += functools.reduce(lambda x, y: x + y, [dss[hh] for dss in results])

            _for_units(r, nsub, one, add_bias_grads)
            emit(0, b, qc, dq_s[...])
            tail = slice((nsub - 1) * sub, nsub * sub)
            kacc[tail, :] += dkp_s[0:sub, :]
            vacc[tail, :] += dvp_s[0:sub, :]

            @pl.when(b >= 1)
            def _():
                emit_keys(b - 1)

            for acc, before_s, cur_s in ((kacc, dkp_s, dkc_s), (vacc, dvp_s, dvc_s)):
                acc[...] = cur_s[...]
                for j in range(nsub - 1):
                    acc[j * sub:(j + 1) * sub, :] += before_s[(j + 1) * sub:(j + 2) * sub, :]

    last = nb - 1
    blk = (tb, LANES)
    cur = lambda c: pl.BlockSpec(blk, lambda p, b: (jnp.minimum(b, last), c + p))
    before = lambda c: pl.BlockSpec((sub, LANES), lambda p, b: (jnp.clip(b * nsub - 1, 0, nb * nsub - 1), c + p))
    tab = pl.BlockSpec((2, ATTN_BLOCK, 2 * ATTN_BLOCK), lambda p, b: (p, 0, 0))
    hbm = pl.BlockSpec(memory_space=pl.ANY)
    return pl.pallas_call(
        body, name=name, grid=(2, nb + 1),
        in_specs=[cur(qc), cur(kc), before(kc), cur(vc), before(vc), cur(dc), cur(dc), cur(0), tab, hbm],
        out_specs=[hbm, tab],
        out_shape=[jax.ShapeDtypeStruct(dproj.shape, BF16), jax.ShapeDtypeStruct((4, ATTN_BLOCK, 2 * ATTN_BLOCK), F32)],
        input_output_aliases={9: 0},
        scratch_shapes=[pltpu.VMEM(blk, F32)] * 7 + [pltpu.VMEM((3, 2) + blk, BF16), pltpu.SemaphoreType.DMA((3, 2))],
        compiler_params=_params(("arbitrary", "arbitrary")),
    )(proj, proj, proj, proj, proj, do, corr, lse, bias, dproj)


def _mix_weights(lses):
    m = jnp.maximum(jnp.maximum(lses[0], lses[1]), lses[2])
    es = [jnp.exp(l - m) for l in lses]
    inv = 1.0 / (es[0] + es[1] + es[2])
    return jnp.concatenate([e * inv for e in es], axis=1)


def _attn_mix(os, lses, proj, tm, name):
    T = proj.shape[0]
    zcol = _OFF["z_attn"] // D_ATTN

    def body(o0, o1, o2, l0, l1, l2, z_ref, out_ref):
        z = z_ref[...]
        o = jnp.concatenate([o0[...], o1[...], o2[...]], axis=1)
        alpha = _mix_weights([l0[...], l1[...], l2[...]])
        out_ref[...] = (o * alpha * (z * _sigmoid(z))).astype(BF16)

    row = lambda i: (i, 0)
    grp = pl.BlockSpec((tm, ATTN_GROUP_WIDTH), row)
    return pl.pallas_call(
        body, name=name, grid=(T // tm,),
        in_specs=[grp] * 6 + [pl.BlockSpec((tm, D_ATTN), lambda i: (i, zcol))],
        out_specs=pl.BlockSpec((tm, D_ATTN), row),
        out_shape=jax.ShapeDtypeStruct((T, D_ATTN), BF16),
        compiler_params=_params(("parallel",)),
    )(*os, *lses, proj)


def _attn_mix_bwd(d, os, lses, proj, dproj, tm, name):
    T = proj.shape[0]
    zcol = _OFF["z_attn"] // D_ATTN

    def body(d_ref, o0, o1, o2, l0, l1, l2, z_ref, _, do_ref, corr_ref, dz_ref):
        dv, z = d_ref[...], z_ref[...]
        ov = jnp.concatenate([o0[...], o1[...], o2[...]], axis=1)
        alpha = _mix_weights([l0[...], l1[...], l2[...]])
        sz = _sigmoid(z)
        oc = ov * alpha
        dz_ref[...] = (dv * oc * (sz * (1.0 + z * (1.0 - sz)))).astype(BF16)
        doc = dv * (z * sz)
        do_ref[...] = doc * alpha
        pr = doc * oc
        p3 = pr[:, 0:256] + pr[:, 256:512] + pr[:, 512:768]
        li = lax.broadcasted_iota(jnp.int32, (256, 256), 0) // ATTN_HEAD_DIM
        lj = lax.broadcasted_iota(jnp.int32, (256, 256), 1) // ATTN_HEAD_DIM
        ones = jnp.where(li == lj, 1.0, 0.0).astype(F32)
        s = lax.dot_general(p3, ones, NN, precision=lax.Precision.HIGHEST, preferred_element_type=F32)
        corr_ref[...] = alpha * jnp.concatenate([s, s, s], axis=1)

    row = lambda i: (i, 0)
    grp = pl.BlockSpec((tm, ATTN_GROUP_WIDTH), row)
    return pl.pallas_call(
        body, name=name, grid=(T // tm,),
        in_specs=[pl.BlockSpec((tm, D_ATTN), row)] + [grp] * 6 + [pl.BlockSpec((tm, D_ATTN), lambda i: (i, zcol)),
                                                                    pl.BlockSpec(memory_space=pl.ANY)],
        out_specs=[pl.BlockSpec((tm, D_ATTN), row)] * 2 + [pl.BlockSpec((tm, D_ATTN), lambda i: (i, zcol))],
        out_shape=[jax.ShapeDtypeStruct((T, D_ATTN), F32), jax.ShapeDtypeStruct((T, D_ATTN), F32),
                   jax.ShapeDtypeStruct(dproj.shape, BF16)],
        input_output_aliases={8: 2},
        compiler_params=_params(("parallel",)),
    )(d, *os, *lses, proj, dproj)


def _mem_probs(q_ref, kv_ref, h):
    hs = slice(MEM_HEAD_DIM * h, MEM_HEAD_DIM * (h + 1))
    qh = q_ref[:, hs].astype(BF16)
    kh = kv_ref[:, hs]
    vh = kv_ref[:, D_MEM + MEM_HEAD_DIM * h:D_MEM + MEM_HEAD_DIM * (h + 1)]
    s = _dot(qh, kh, NT) * (MEM_HEAD_DIM ** -0.5)
    p = jnp.exp(s - jnp.max(s, axis=-1, keepdims=True))
    pn = p / jnp.sum(p, axis=-1, keepdims=True)
    return qh, kh, vh, pn


def _mem_fwd(proj, kv, tm, name):
    T = proj.shape[0]
    M = kv.shape[0]
    qcol, zcol = _OFF["q_mem"] // D_MEM, _OFF["z_mem"] // D_MEM

    def body(q_ref, z_ref, kv_ref, o_ref):
        outs = []
        for h in range(MEM_HEADS):
            _, _, vh, pn = _mem_probs(q_ref, kv_ref, h)
            outs.append(_dot(pn.astype(BF16), vh))
        z = z_ref[...]
        o_ref[...] = (jnp.concatenate(outs, axis=1) * (z * _sigmoid(z))).astype(BF16)

    return pl.pallas_call(
        body, name=name, grid=(T // tm,),
        in_specs=[pl.BlockSpec((tm, D_MEM), lambda i: (i, qcol)), pl.BlockSpec((tm, D_MEM), lambda i: (i, zcol)),
                  _full((M, 2 * D_MEM))],
        out_specs=pl.BlockSpec((tm, D_MEM), lambda i: (i, 0)),
        out_shape=jax.ShapeDtypeStruct((T, D_MEM), BF16),
        compiler_params=_params(("parallel",)),
    )(proj, proj, kv)


def _mem_bwd(d, proj, kv, tm, name):
    T = proj.shape[0]
    M = kv.shape[0]
    qcol, zcol = _OFF["q_mem"] // D_MEM, _OFF["z_mem"] // D_MEM

    def body(d_ref, q_ref, z_ref, kv_ref, dq_ref, dz_ref, dkv_ref):
        @pl.when(pl.program_id(0) == 0)
        def _():
            dkv_ref[...] = jnp.zeros_like(dkv_ref)

        z = z_ref[...]
        sz = _sigmoid(z)
        dv = d_ref[...]
        dov = dv * (z * sz)
        scale = MEM_HEAD_DIM ** -0.5
        outs, dqs = [], []
        for h in range(MEM_HEADS):
            hs = slice(MEM_HEAD_DIM * h, MEM_HEAD_DIM * (h + 1))
            qh, kh, vh, pn = _mem_probs(q_ref, kv_ref, h)
            pnb = pn.astype(BF16)
            oh = _dot(pnb, vh)
            outs.append(oh)
            doh = dov[:, hs]
            dohb = doh.astype(BF16)
            dp = _dot(dohb, vh, NT)
            ds = pn * (dp - jnp.sum(doh * oh, axis=-1, keepdims=True))
            dsb = ds.astype(BF16)
            dqs.append(_dot(dsb, kh) * scale)
            dkv_ref[:, hs] += _dot(dsb, qh, TN) * scale
            vs = slice(D_MEM + MEM_HEAD_DIM * h, D_MEM + MEM_HEAD_DIM * (h + 1))
            dkv_ref[:, vs] += _dot(pnb, dohb, TN)
        dq_ref[...] = jnp.concatenate(dqs, axis=1).astype(BF16)
        dz_ref[...] = (dv * jnp.concatenate(outs, axis=1) * (sz * (1.0 + z * (1.0 - sz)))).astype(BF16)

    row = lambda i: (i, 0)
    return pl.pallas_call(
        body, name=name, grid=(T // tm,),
        in_specs=[pl.BlockSpec((tm, D_MEM), row), pl.BlockSpec((tm, D_MEM), lambda i: (i, qcol)),
                  pl.BlockSpec((tm, D_MEM), lambda i: (i, zcol)), _full((M, 2 * D_MEM))],
        out_specs=[pl.BlockSpec((tm, D_MEM), row), pl.BlockSpec((tm, D_MEM), row), _full((M, 2 * D_MEM))],
        out_shape=[jax.ShapeDtypeStruct((T, D_MEM), BF16), jax.ShapeDtypeStruct((T, D_MEM), BF16),
                   jax.ShapeDtypeStruct((M, 2 * D_MEM), F32)],
        compiler_params=_params(("arbitrary",)),
    )(d, proj, proj, kv)


def _branches_and_gates(os_ref, oa_ref, om_ref, gl_refs, bg_ref, ws_ref, wa_ref, wm_ref):
    outs = (_dot(os_ref[...], ws_ref[...]), _dot(oa_ref[...], wa_ref[...]), _dot(om_ref[...], wm_ref[...]))
    gates = tuple(_sigmoid(jnp.concatenate([gl_refs[2 * k][...], gl_refs[2 * k + 1][...]], axis=1)
                           + bg_ref[:, D_MODEL * k:D_MODEL * (k + 1)]) for k in range(3))
    return outs, gates


def _merge_specs(tm):
    row = lambda i: (i, 0)
    first = _OFF["gates"] // GATE_BLOCK
    gate = [pl.BlockSpec((tm, GATE_BLOCK), (lambda i, k=k: (i, first + k))) for k in range(N_GATES // GATE_BLOCK)]
    return ([pl.BlockSpec((tm, D_SSM), row), pl.BlockSpec((tm, D_ATTN), row), pl.BlockSpec((tm, D_MEM), row)] + gate
            + [_full((1, N_GATES)), _full((D_SSM, D_MODEL)), _full((D_ATTN, D_MODEL)), _full((D_MEM, D_MODEL)),
               _full((D_MODEL, D_MODEL))])


def _merge_fwd(x, o_ssm, o_attn, o_mem, proj, bg, ws, wa, wm, wo, tm, name):
    T = x.shape[0]

    def body(os_ref, oa_ref, om_ref, g0, g1, g2, g3, g4, g5, bg_ref, ws_ref, wa_ref, wm_ref, wo_ref, x_ref,
             xo_ref, mg_ref):
        outs, gates = _branches_and_gates(os_ref, oa_ref, om_ref, (g0, g1, g2, g3, g4, g5), bg_ref, ws_ref, wa_ref,
                                          wm_ref)
        merged = (gates[0] * outs[0] + gates[1] * outs[1] + gates[2] * outs[2]).astype(BF16)
        mg_ref[...] = merged
        xo_ref[...] = x_ref[...] + _dot(merged, wo_ref[...])

    row = lambda i: (i, 0)
    return pl.pallas_call(
        body, name=name, grid=(T // tm,),
        in_specs=_merge_specs(tm) + [pl.BlockSpec((tm, D_MODEL), row)],
        out_specs=[pl.BlockSpec((tm, D_MODEL), row), pl.BlockSpec((tm, D_MODEL), row)],
        out_shape=[jax.ShapeDtypeStruct((T, D_MODEL), F32), jax.ShapeDtypeStruct((T, D_MODEL), BF16)],
        compiler_params=_params(("parallel",)),
    )(o_ssm, o_attn, o_mem, *([proj] * (N_GATES // GATE_BLOCK)), bg, ws, wa, wm, wo, x)


def _merge_bwd(dx, o_ssm, o_attn, o_mem, proj, bg, ws, wa, wm, wo, tm, name, job=None):
    T = dx.shape[0]

    n = T // tm

    def body(os_ref, oa_ref, om_ref, g0, g1, g2, g3, g4, g5, bg_ref, ws_ref, wa_ref, wm_ref, wo_ref, dx_ref,
             dproj_ref, db_ref, dos_ref, doa_ref, dom_ref, dbg_ref, dgl_buf, dgl_sems):
        i = pl.program_id(0)
        slot = i % 2

        def to_dproj(s, row0):
            return pltpu.make_async_copy(dgl_buf.at[s], dproj_ref.at[pl.ds(row0, tm), pl.ds(_OFF["gates"], N_GATES)],
                                         dgl_sems.at[s])

        @pl.when(i == 0)
        def _():
            dbg_ref[...] = jnp.zeros_like(dbg_ref)

        @pl.when(i >= 2)
        def _():
            to_dproj(slot, 0).wait()

        outs, gates = _branches_and_gates(os_ref, oa_ref, om_ref, (g0, g1, g2, g3, g4, g5), bg_ref, ws_ref, wa_ref,
                                          wm_ref)
        dm = _dot(dx_ref[...].astype(BF16), wo_ref[...], NT)
        w_refs = (ws_ref, wa_ref, wm_ref)
        do_refs = (dos_ref, doa_ref, dom_ref)
        for k in range(3):
            cols = slice(D_MODEL * k, D_MODEL * (k + 1))
            dgl = dm * outs[k] * (gates[k] * (1.0 - gates[k]))
            dgl_buf[slot, :, cols] = dgl.astype(BF16)
            dbg_ref[:, cols] += jnp.sum(dgl, axis=0, keepdims=True)
            dbk = (dm * gates[k]).astype(BF16)
            db_ref[:, cols] = dbk
            do_refs[k][...] = _dot(dbk, w_refs[k][...], NT)
        to_dproj(slot, pl.multiple_of(i * tm, tm)).start()

        @pl.when(i == n - 1)
        def _():
            for s in range(min(2, n)):
                to_dproj(s, 0).wait()

    row = lambda i: (i, 0)
    return _pc(
        body, job, name=name, grid=(n,),
        in_specs=_merge_specs(tm) + [pl.BlockSpec((tm, D_MODEL), row)],
        out_specs=[pl.BlockSpec(memory_space=pl.ANY), pl.BlockSpec((tm, N_GATES), row), pl.BlockSpec((tm, D_SSM), row),
                   pl.BlockSpec((tm, D_ATTN), row), pl.BlockSpec((tm, D_MEM), row), _full((1, N_GATES))],
        out_shape=[jax.ShapeDtypeStruct((T, D_IN), BF16), jax.ShapeDtypeStruct((T, N_GATES), BF16),
                   jax.ShapeDtypeStruct((T, D_SSM), F32), jax.ShapeDtypeStruct((T, D_ATTN), F32),
                   jax.ShapeDtypeStruct((T, D_MEM), F32), jax.ShapeDtypeStruct((1, N_GATES), F32)],
        scratch_shapes=[pltpu.VMEM((2, tm, N_GATES), BF16), pltpu.SemaphoreType.DMA((2,))], sem=("arbitrary",),
        operands=(o_ssm, o_attn, o_mem, *([proj] * (N_GATES // GATE_BLOCK)), bg, ws, wa, wm, wo, dx))


def _loss_head(x, g, target, tm, name):
    T, D = x.shape

    def body(x_ref, g_ref, t_ref, loss_ref, dx_ref, dg_ref):
        @pl.when(pl.program_id(0) == 0)
        def _():
            loss_ref[...] = jnp.zeros_like(loss_ref)
            dg_ref[...] = jnp.zeros_like(dg_ref)

        xv = x_ref[...]
        r = lax.rsqrt(jnp.mean(xv * xv, axis=-1, keepdims=True) + EPS)
        xr = xv * r
        err = xr * g_ref[...] - t_ref[...]
        loss_ref[...] += 0.5 * jnp.sum(jnp.mean(err * err, axis=-1, keepdims=True), axis=0, keepdims=True)
        dy = err * (1.0 / D)
        dg_ref[...] += jnp.sum(dy * xr, axis=0, keepdims=True)
        wv = dy * g_ref[...]
        dx_ref[...] = r * (wv - xr * jnp.mean(wv * xr, axis=-1, keepdims=True))

    row = lambda i: (i, 0)
    return pl.pallas_call(
        body, name=name, grid=(T // tm,),
        in_specs=[pl.BlockSpec((tm, D), row), _full((1, D)), pl.BlockSpec((tm, D), row)],
        out_specs=[_full((1, 128)), pl.BlockSpec((tm, D), row), _full((1, D))],
        out_shape=[jax.ShapeDtypeStruct((1, 128), F32), jax.ShapeDtypeStruct((T, D), F32),
                   jax.ShapeDtypeStruct((1, D), F32)],
        compiler_params=_params(("arbitrary",)),
    )(x, g, target)


def _adamw(parts, w, m, v, tr, name):
    L, R, C = w.shape

    def body(p_ref, w_ref, m_ref, v_ref, g_ref, d_ref, mo_ref, vo_ref):
        g = p_ref[0].astype(F32)
        for s in range(1, N_DEV):
            g = g + p_ref[s].astype(F32)
        mn = ADAM_B1 * m_ref[...] + (1.0 - ADAM_B1) * g
        vn = ADAM_B2 * v_ref[...] + (1.0 - ADAM_B2) * (g * g)
        m_hat = mn / (1.0 - ADAM_B1 ** ADAM_STEP)
        v_hat = vn / (1.0 - ADAM_B2 ** ADAM_STEP)
        g_ref[...] = g
        d_ref[...] = -ADAM_LR * (m_hat / (jnp.sqrt(v_hat) + ADAM_EPS) + ADAM_WD * w_ref[...])
        mo_ref[...] = mn
        vo_ref[...] = vn

    one = pl.BlockSpec((None, tr, C), lambda l, i: (l, i, 0))
    return pl.pallas_call(
        body, name=name, grid=(L, R // tr),
        in_specs=[pl.BlockSpec((N_DEV, None, tr, C), lambda l, i: (0, l, i, 0)), one, one, one],
        out_specs=[one] * 4,
        out_shape=[jax.ShapeDtypeStruct((L, R, C), F32)] * 4,
        compiler_params=_params(("parallel", "parallel")),
    )(parts, w, m, v)


_SHARDED = (("w_in", (1088, 1024), 1), ("w_glu", (96, 768), 0), ("w_mem_kv", (128, 1024), 0),
            ("w_br_ssm", (768, 128), 1), ("w_br_attn", (768, 128), 1), ("w_br_mem", (512, 128), 1),
            ("w_out", (128, 1024), 0))
_W_IN = 0
_SMALL = tuple(range(1, len(_SHARDED)))


class _Job(NamedTuple):
    ins: list
    out_shape: list
    aliases: dict
    pairs: Callable
    n: int


def _peers():
    x, y, c = lax.axis_index("x"), lax.axis_index("y"), lax.axis_index("c")
    me = 4 * x + 2 * y + c
    out = []
    for k in range(1, N_DEV):
        px = 1 - x if k & 4 else x
        py = 1 - y if k & 2 else y
        pc = 1 - c if k & 1 else c
        out.append(((px, py, pc), 4 * px + 2 * py + pc))
    return me, out


def _copies(pairs, send_sems, recv_sems, local_sems, arrivals):
    me, peers = _peers()
    local = [pltpu.make_async_copy(src(me), dst(me), local_sems.at[j]) for j, (src, dst) in enumerate(pairs)]
    sends, recvs = [], []
    for k, (peer, lin) in enumerate(peers):
        for j, (src, dst) in enumerate(pairs):
            for to, out in ((dst(me), sends), (dst(lin), recvs)):
                if out is sends or arrivals:
                    out.append(pltpu.make_async_remote_copy(
                        src_ref=src(lin), dst_ref=to, send_sem=send_sems.at[j, k], recv_sem=recv_sems.at[j, k],
                        device_id=peer, device_id_type=pl.DeviceIdType.MESH))
    return local, sends, recvs


def _start_copies(pairs, *sems):
    local, sends, _ = _copies(pairs, *sems, arrivals=False)
    for cp in local + sends:
        cp.start()


def _wait_copies(pairs, *sems):
    local, sends, recvs = _copies(pairs, *sems, arrivals=True)
    for cp in recvs:
        cp.wait_recv()
    for cp in sends:
        cp.wait_send()
    for cp in local:
        cp.wait()


def _job_scratch(job):
    return [pltpu.SemaphoreType.DMA((job.n, N_DEV - 1)), pltpu.SemaphoreType.DMA((job.n, N_DEV - 1)),
            pltpu.SemaphoreType.DMA((job.n,))]


def _pc(body, job, *, name, grid, in_specs, out_specs, out_shape, scratch_shapes, sem, operands, aliases=None):
    aliases = aliases or {}
    if job is None:
        return pl.pallas_call(body, name=name, grid=grid, in_specs=in_specs, out_specs=out_specs, out_shape=out_shape,
                              scratch_shapes=scratch_shapes, input_output_aliases=aliases,
                              compiler_params=_params(sem))(*operands)
    a = len(in_specs)
    b = a + len(job.ins)
    c = b + len(out_shape)
    d = c + len(job.out_shape)
    e = d + len(scratch_shapes)

    def carried(*refs):
        pairs = job.pairs(refs[a:b], refs[c:d])
        ids = [pl.program_id(k) for k in range(len(grid))]
        first = functools.reduce(jnp.logical_and, [i == 0 for i in ids])
        last = functools.reduce(jnp.logical_and, [i == n - 1 for i, n in zip(ids, grid)])

        @pl.when(first)
        def _():
            _start_copies(pairs, *refs[e:])

        body(*refs[:a], *refs[b:c], *refs[d:e])

        @pl.when(last)
        def _():
            _wait_copies(pairs, *refs[e:])

    hbm = pl.BlockSpec(memory_space=pl.ANY)
    outs = pl.pallas_call(
        carried, name=name, grid=grid,
        in_specs=list(in_specs) + [hbm] * len(job.ins), out_specs=list(out_specs) + [hbm] * len(job.out_shape),
        out_shape=list(out_shape) + list(job.out_shape),
        input_output_aliases={**aliases, **{a + i: len(out_shape) + o for i, o in job.aliases.items()}},
        scratch_shapes=list(scratch_shapes) + _job_scratch(job),
        compiler_params=_params(("arbitrary",) * len(grid)),
    )(*operands, *job.ins)
    return outs[:len(out_shape)], outs[len(out_shape):]


def _gather_via_sibling(x, take, place, out_shape, name, landing=None):
    def body(*refs):
        x_ref, o_ref = refs[0], refs[-4]
        send_sems, recv_sems, local_sem = refs[-3:]
        x, y, c = lax.axis_index("x"), lax.axis_index("y"), lax.axis_index("c")
        me, sibling = (x, y, c), (x, y, 1 - c)
        chips = [(1 - x, y), (x, 1 - y), (1 - x, 1 - y)]
        src = take(x_ref)

        def slot(px, py, pc):
            return place(o_ref, 4 * px + 2 * py + pc)

        def copy(k, block, to, first_hand):
            return pltpu.make_async_remote_copy(
                src_ref=src if first_hand else slot(*block), dst_ref=slot(*block), send_sem=send_sems.at[k],
                recv_sem=recv_sems.at[k], device_id=to, device_id_type=pl.DeviceIdType.MESH)

        mine = pltpu.make_async_copy(src, slot(*me), local_sem)
        mine.start()
        first = [copy(0, me, sibling, True)] + [copy(1 + j, me, (*chip, c), True) for j, chip in enumerate(chips)]
        for cp in first:
            cp.start()
        passed = []
        for j, chip in enumerate(chips):
            copy(1 + j, (*chip, c), me, True).wait_recv()
            passed.append(copy(4 + j, (*chip, c), sibling, False))
            passed[-1].start()
        copy(0, sibling, me, True).wait_recv()
        for j, chip in enumerate(chips):
            copy(4 + j, (*chip, 1 - c), me, False).wait_recv()
        for cp in first + passed:
            cp.wait_send()
        mine.wait()

    hbm = pl.BlockSpec(memory_space=pl.ANY)
    ins = [x] if landing is None else [x, landing]
    return pl.pallas_call(
        body, name=name, in_specs=[hbm] * len(ins), out_specs=hbm, out_shape=out_shape,
        input_output_aliases={} if landing is None else {1: 0},
        scratch_shapes=[pltpu.SemaphoreType.DMA((N_DEV - 1,)), pltpu.SemaphoreType.DMA((N_DEV - 1,)),
                        pltpu.SemaphoreType.DMA],
    )(*ins)


def _lane_window(ref, who):
    return ref.at[:, pl.ds(pl.multiple_of(who * LANES, LANES), LANES)]


def _gather_job(shards, items):
    out_shape = []
    for i, _ in items:
        _, s, axis = _SHARDED[i]
        whole = i != _W_IN and axis == 1
        out_shape.append(jax.ShapeDtypeStruct((s[0], N_DEV * s[1]) if whole else (N_DEV,) + s, BF16))

    def pairs(in_refs, out_refs):
        out = []
        for (i, l), src, dst in zip(items, in_refs, out_refs):
            if i != _W_IN and _SHARDED[i][2] == 1:
                out.append((lambda who, src=src, l=l: src.at[l], lambda who, dst=dst: _lane_window(dst, who)))
            else:
                out.append((lambda who, src=src, l=l: src.at[l], lambda who, dst=dst: dst.at[who]))
        return out

    return _Job([shards[i] for i, _ in items], out_shape, {}, pairs, len(items))


def _landed_weights(items, landed):
    out = {}
    for (i, _), a in zip(items, landed):
        n, s, axis = _SHARDED[i]
        if i == _W_IN:
            out[n] = a.reshape(D_IN, D_MODEL)
        elif axis == 0:
            out[n] = a.reshape(N_DEV * s[0], s[1])
        else:
            out[n] = a
    return out


def _scatter_job(grads, items, layer, parts=None):
    ng = len(grads)
    out_shape = [jax.ShapeDtypeStruct((N_DEV, DEPTH) + _SHARDED[i][1], BF16) for i in items]

    def pairs(in_refs, out_refs):
        out = []
        for i, src, dst in zip(items, in_refs[:ng], out_refs):
            _, s, axis = _SHARDED[i]
            if i == _W_IN:
                take = lambda who, src=src: src.at[who]
            elif axis == 0:
                take = lambda who, src=src, s=s: src.at[pl.ds(pl.multiple_of(who * s[0], 16), s[0])]
            else:
                take = lambda who, src=src: _lane_window(src, who)
            out.append((take, lambda who, dst=dst: dst.at[who, layer]))
        return out

    aliases = {} if parts is None else {ng + j: j for j in range(len(items))}
    return _Job(list(grads) + ([] if parts is None else list(parts)), out_shape, aliases, pairs, len(items))


def _rows_job(src, row0, landing=None):
    n = src.shape[0]
    pairs = lambda in_refs, out_refs: [(lambda who: in_refs[0], lambda who: out_refs[0].at[who, pl.ds(row0, n)])]
    return _Job([src] + ([] if landing is None else [landing]), [jax.ShapeDtypeStruct((N_DEV, _REP_ROWS, LANES), F32)],
                {} if landing is None else {1: 0}, pairs, 1)


_REPLICATED = (("norm_g", (2, 1024)), ("mem_norm_g", (2, 1024)), ("b_gate", (2, 3072)),
               ("ssm_lambda_re", (2, 48, 64)), ("ssm_lambda_im", (2, 48, 64)), ("ssm_log_dt", (2, 48)),
               ("ssm_b_re", (2, 48, 64, 16)), ("ssm_b_im", (2, 48, 64, 16)), ("ssm_c_re", (2, 48, 16, 64)),
               ("ssm_c_im", (2, 48, 16, 64)), ("ssm_d", (2, 768)), ("b_glu", (2, 768)), ("rel_bias", (32, 12)),
               ("final_norm_g", (1024,)))
_PER_LAYER = tuple((n, s[1:]) for n, s in _REPLICATED if s[0] == DEPTH and len(s) > 1)
_SHARED = tuple((n, s) for n, s in _REPLICATED if (n, s[1:]) not in _PER_LAYER)
_REP_HALF_ROWS = 1664
_REP_ROWS = 2 * _REP_HALF_ROWS
assert sum(int(np.prod(s)) for _, s in _PER_LAYER + _SHARED) <= _REP_HALF_ROWS * LANES


def _pack_half(tree, layer, shared):
    flat = [tree[n][layer].reshape(-1) for n, _ in _PER_LAYER]
    if shared:
        flat += [tree[n].reshape(-1) for n, _ in _SHARED]
    flat = jnp.concatenate(flat)
    return jnp.pad(flat, (0, _REP_HALF_ROWS * LANES - flat.shape[0])).reshape(_REP_HALF_ROWS, LANES)


def _pack_replicated(tree):
    return jnp.concatenate([_pack_half(tree, 1, False), _pack_half(tree, 0, True)])[None]


def _unpack_replicated(packed):
    halves = packed.reshape(2, -1)
    out, r = {}, 0
    for n, s in _PER_LAYER:
        size = int(np.prod(s))
        out[n] = jnp.stack([halves[1, r:r + size].reshape(s), halves[0, r:r + size].reshape(s)])
        r += size
    for n, s in _SHARED:
        size = int(np.prod(s))
        out[n] = halves[1, r:r + size].reshape(s)
        r += size
    return out


def _discretize(lam_re, lam_im, log_dt, b_re, b_im):
    dt = jnp.exp(log_dt)[:, None]
    mag = jnp.exp(lam_re * dt)
    abar_re, abar_im = mag * jnp.cos(lam_im * dt), mag * jnp.sin(lam_im * dt)
    den = lam_re * lam_re + lam_im * lam_im
    nr, ni = abar_re - 1.0, abar_im
    f_re = (nr * lam_re + ni * lam_im) / den
    f_im = (ni * lam_re - nr * lam_im) / den
    bbar_re = f_re[..., None] * b_re - f_im[..., None] * b_im
    bbar_im = f_re[..., None] * b_im + f_im[..., None] * b_re
    return abar_re, abar_im, bbar_re, bbar_im


def _block_diag(a):
    _, R, C = a.shape
    a = a.reshape(SSM_BLOCKS, 8, R, C)
    eye = jnp.eye(8, dtype=a.dtype)
    return (a[:, :, :, None, :] * eye[None, :, None, :, None]).reshape(SSM_BLOCKS, 8 * R, 8 * C)


def _diag_blocks(a, R, C):
    a = a.reshape(SSM_BLOCKS, 8, R, 8, C)
    eye = jnp.eye(8, dtype=a.dtype)
    return jnp.sum(a * eye[None, :, None, :, None], axis=3).reshape(SSM_GROUPS, R, C)


def _carried(result, job):
    return (result, None) if job is None else result


def _layer_fwd(x, mem, W, P, bias, layer, jobs):
    tag = f"l{layer}"
    abar_re, abar_im, bbar_re, bbar_im = _discretize(P["ssm_lambda_re"][layer], P["ssm_lambda_im"][layer],
                                                     P["ssm_log_dt"][layer], P["ssm_b_re"][layer], P["ssm_b_im"][layer])
    c_re, c_im = P["ssm_c_re"][layer], P["ssm_c_im"][layer]
    ssm = dict(
        are=abar_re.reshape(1, N_STATE), aim=abar_im.reshape(1, N_STATE),
        bre=_block_diag(bbar_re.transpose(0, 2, 1)).astype(BF16), bim=_block_diag(bbar_im.transpose(0, 2, 1)).astype(BF16),
        cre=_block_diag(c_re.transpose(0, 2, 1)).astype(BF16), cimn=_block_diag(-c_im.transpose(0, 2, 1)).astype(BF16),
        ctre=_block_diag(c_re).astype(BF16), ctimn=_block_diag(-c_im).astype(BF16),
        btre=_block_diag(bbar_re).astype(BF16), btim=_block_diag(bbar_im).astype(BF16),
        d=P["ssm_d"][layer].reshape(1, D_SSM))
    bglu = P["b_glu"][layer].reshape(1, D_SSM)
    bgate = P["b_gate"][layer].reshape(1, N_GATES)
    g = P["norm_g"][layer].reshape(1, D_MODEL)
    gm = P["mem_norm_g"][layer].reshape(1, D_MODEL)
    delivered = {}

    def carry(stage):
        return jobs[stage][0] if stage in jobs else None

    def deliver(stage, landed):
        if landed is not None:
            delivered[stage] = _landed_weights(jobs[stage][1], landed)

    T = x.shape[0]
    (proj, h), landed = _carried(_norm_proj(x, g, W["w_in"], min(T, 1024), 2176, f"{tag}_proj", job=carry("proj"),
                                            w_turned=True), carry("proj"))
    deliver("proj", landed)
    W = {**W, **delivered.get("proj", {})}
    (xr, xi, y, o_ssm), landed = _carried(
        _ssm_fwd(proj, ssm["bre"], ssm["bim"], ssm["cre"], ssm["cimn"], ssm["are"], ssm["aim"], ssm["d"], W["w_glu"],
                 bglu, 512, f"{tag}_ssm", job=carry("ssm")), carry("ssm"))
    deliver("ssm", landed)
    os, lses = [], []
    for grp in range(3):
        stage = f"attn{grp}"
        (o_g, lse_g), landed = _carried(_attn_fwd(proj, bias[grp], grp, f"{tag}_{stage}", job=carry(stage)), carry(stage))
        deliver(stage, landed)
        os.append(o_g)
        lses.append(lse_g)
    o_attn = _attn_mix(os, lses, proj, 512, f"{tag}_attn_mix")
    kvb, hm = _norm_proj(mem, gm, W["w_mem_kv"], mem.shape[0], 1024, f"{tag}_mem_kv", out_dtype=BF16)
    o_mem = _mem_fwd(proj, kvb, 512, f"{tag}_mem")
    x_out, merged = _merge_fwd(x, o_ssm, o_attn, o_mem, proj, bgate, W["w_br_ssm"], W["w_br_attn"], W["w_br_mem"],
                               W["w_out"], 512, f"{tag}_merge")
    res = dict(x=x, mem=mem, proj=proj, h=h, xr=xr, xi=xi, y=y, o_ssm=o_ssm, os=os, lses=lses,
               o_attn=o_attn, kvb=kvb, hm=hm, o_mem=o_mem, merged=merged, ssm=ssm, bglu=bglu,
               bgate=bgate, g=g, gm=gm, W=W)
    return x_out, res, delivered


def _layer_bwd(dx, res, P, bias, layer, jobs):
    tag = f"l{layer}b"
    proj, ssm, W = res["proj"], res["ssm"], res["W"]
    T = dx.shape[0]
    landed = {}

    def run(stage, fn, job):
        out, landed[stage] = _carried(fn(job), job)
        if job is None:
            del landed[stage]
        return out

    dproj, dbr, do_ssm, do_attn, do_mem, dbg = run(
        "merge", lambda job: _merge_bwd(dx, res["o_ssm"], res["o_attn"], res["o_mem"], proj, res["bgate"], W["w_br_ssm"],
                                        W["w_br_attn"], W["w_br_mem"], W["w_out"], 256, f"{tag}_merge", job=job),
        jobs.get("merge"))
    gw = {}
    tk = min(T, 1024)
    gw["w_out"] = _mm_tn(res["merged"], dx, 1024, 1024, tk, f"{tag}_dw_out")
    gw["w_br_ssm"] = _mm_tn(res["o_ssm"], dbr, 768, 1024, tk, f"{tag}_dw_br_ssm", b_col=0, n=1024)
    gw["w_br_attn"] = _mm_tn(res["o_attn"], dbr, 768, 1024, tk, f"{tag}_dw_br_attn", b_col=1024, n=1024)
    gw["w_br_mem"] = _mm_tn(res["o_mem"], dbr, 512, 1024, tk, f"{tag}_dw_br_mem", b_col=2048, n=1024)

    dqm, dzm, dkv = _mem_bwd(do_mem, proj, res["kvb"], 512, f"{tag}_mem")
    M = dkv.shape[0]
    gw["w_mem_kv"] = _mm_tn(res["hm"], dkv, 1024, 1024, M, f"{tag}_dw_mem_kv")
    _, dgm = _proj_bwd(dkv.astype(BF16), W["w_mem_kv"], res["mem"], res["gm"], jnp.zeros_like(res["mem"]), M, 1024,
                       f"{tag}_mem_norm")

    do_g, corr, dproj = _attn_mix_bwd(do_attn, res["os"], res["lses"], proj, dproj, 512, f"{tag}_attn_mix")
    dbs = []
    for grp in range(3):
        dproj, db_g = _attn_bwd(proj, do_g, corr, res["lses"][grp], bias[grp], dproj, grp, f"{tag}_attn{grp}")
        dbs.append(db_g)
    dbias = jnp.stack(dbs)

    dy, dproj, gelu_b, dt_b, dbglu = _glu_bwd(do_ssm, res["y"], proj, W["w_glu"], res["bglu"], dproj, 512, f"{tag}_glu")
    gw["w_glu"] = _mm_tn(gelu_b, dt_b, 768, 768, tk, f"{tag}_dw_glu")
    dproj, dbre, dbim, dcre, dcim, dare, daim, dd = run(
        "ssm", lambda job: _ssm_bwd(dy, proj, res["xr"], res["xi"], ssm["ctre"], ssm["ctimn"], ssm["btre"], ssm["btim"],
                                    ssm["are"], ssm["aim"], ssm["d"], dproj, 256, f"{tag}_ssm", job=job),
        jobs.get("ssm"))
    _, disc_vjp = jax.vjp(_discretize, P["ssm_lambda_re"][layer], P["ssm_lambda_im"][layer], P["ssm_log_dt"][layer],
                          P["ssm_b_re"][layer], P["ssm_b_im"][layer])
    d_lre, d_lim, d_ldt, d_bre, d_bim = disc_vjp((dare.reshape(SSM_GROUPS, SSM_STATE), daim.reshape(SSM_GROUPS, SSM_STATE),
                                                  _diag_blocks(dbre, SSM_STATE, SSM_GROUP),
                                                  _diag_blocks(dbim, SSM_STATE, SSM_GROUP)))

    small = [gw[_SHARDED[i][0]] for i in _SMALL]
    for seg, piece in (("q_mem", dqm), ("z_mem", dzm)):
        dproj = lax.dynamic_update_slice(dproj, piece, (0, _OFF[seg]))
    dw_in = run("dw_in", lambda job: _mm_tn(dproj, res["h"], 2176, 1024, min(T, 1024), f"{tag}_dw_in", job=job),
                jobs["dw_in"](small) if "dw_in" in jobs else None)
    dw_in = dw_in.reshape((N_DEV,) + _SHARDED[_W_IN][1])
    dx_in, dg = run("proj", lambda job: _proj_bwd(dproj, W["w_in"], res["x"], res["g"], dx, min(T, 1024), 2176,
                                                  f"{tag}_proj", job=job, w_turned=True),
                    jobs["proj"](small, dw_in, landed) if "proj" in jobs else None)

    gp = dict(norm_g=dg[0], mem_norm_g=dgm[0], b_gate=dbg[0], ssm_lambda_re=d_lre, ssm_lambda_im=d_lim,
              ssm_log_dt=d_ldt, ssm_b_re=d_bre, ssm_b_im=d_bim,
              ssm_c_re=_diag_blocks(dcre, SSM_GROUP, SSM_STATE), ssm_c_im=_diag_blocks(dcim, SSM_GROUP, SSM_STATE),
              ssm_d=dd[0], b_glu=dbglu[0])
    return dx_in, dw_in, gp, dbias, landed


def _train_step(x, mem, target, shards, P):
    rest0 = [(i, 0) for i in _SMALL]
    thirds1 = [[(i, 1) for i in _SMALL[k::3]] for k in range(3)]
    first = [(_W_IN, 0)]
    w_in0 = _gather_via_sibling(shards[_W_IN], lambda ref: ref.at[0], lambda ref, s: ref.at[s],
                                jax.ShapeDtypeStruct((N_DEV,) + _SHARDED[_W_IN][1], BF16), "gather_w_in0")
    W0 = _landed_weights(first, [w_in0])
    buckets = _bucket_tables()
    bias = _bias_tables(P["rel_bias"], buckets, "bias_tables")
    jobs0 = {"proj": (_gather_job(shards, rest0), rest0), "ssm": (_gather_job(shards, [(_W_IN, 1)]), [(_W_IN, 1)]),
             **{f"attn{k}": (_gather_job(shards, items), items) for k, items in enumerate(thirds1)}}
    x, res0, delivered = _layer_fwd(x, mem, W0, P, bias, 0, jobs0)
    W1 = {**delivered["ssm"], **delivered["attn0"], **delivered["attn1"], **delivered["attn2"]}
    x, res1, _ = _layer_fwd(x, mem, W1, P, bias, 1, {})
    loss, dx, dgf = _loss_head(x, P["final_norm_g"].reshape(1, D_MODEL), target, 512, "loss_head")

    dx, dw_in1, gp1, dbias1, landed1 = _layer_bwd(
        dx, res1, P, bias, 1, {"proj": lambda small, dw_in, landed: _scatter_job(small, _SMALL, 1)})
    rep1 = _pack_half({n: a[None] for n, a in gp1.items()}, 0, False)
    dx, _, gp0, dbias0, landed0 = _layer_bwd(
        dx, res0, P, bias, 0,
        {"merge": _rows_job(rep1, 0), "ssm": _scatter_job([dw_in1], [_W_IN], 1),
         "dw_in": lambda small: _scatter_job(small, _SMALL, 0, parts=landed1["proj"]),
         "proj": lambda small, dw_in, landed: _scatter_job([dw_in], [_W_IN], 0, parts=landed["ssm"])})
    d_rel = _bias_grad(dbias0, dbias1, buckets, "bias_grad")
    gp0 = {n: a[None] for n, a in gp0.items()}
    gp0["rel_bias"] = jnp.sum(d_rel, axis=-1).transpose(2, 0, 1).reshape(NUM_BUCKETS, 12)
    gp0["final_norm_g"] = dgf[0]
    rep0 = _pack_half(gp0, 0, True)
    rparts = _gather_via_sibling(rep0, lambda ref: ref, lambda ref, s: ref.at[s, pl.ds(_REP_HALF_ROWS, _REP_HALF_ROWS)],
                                 jax.ShapeDtypeStruct((N_DEV, _REP_ROWS, LANES), F32), "gather_small_grads0",
                                 landing=landed0["merge"][0])
    return loss[0, 0], dx, list(landed0["proj"]) + list(landed0["dw_in"]), rparts


_WEIGHTS = ["norm_g", "mem_norm_g", "w_in", "b_gate", "ssm_lambda_re", "ssm_lambda_im", "ssm_log_dt", "ssm_b_re",
            "ssm_b_im", "ssm_c_re", "ssm_c_im", "ssm_d", "w_glu", "b_glu", "w_mem_kv", "w_br_ssm", "w_br_attn",
            "w_br_mem", "w_out", "rel_bias", "final_norm_g"]
_ADAM_ROWS = {"w_in": 136,"w_glu": 96, "w_mem_kv": 128, "w_br_ssm": 768, "w_br_attn": 768, "w_br_mem": 512,
              "w_out": 128}


def kernel(x, mem, norm_g, mem_norm_g, w_in, b_gate, ssm_lambda_re, ssm_lambda_im, ssm_log_dt, ssm_b_re, ssm_b_im, ssm_c_re, ssm_c_im, ssm_d, w_glu, b_glu, w_mem_kv, w_br_ssm, w_br_attn, w_br_mem, w_out, rel_bias, final_norm_g, loss_target, m_norm_g, m_mem_norm_g, m_w_in, m_b_gate, m_ssm_lambda_re, m_ssm_lambda_im, m_ssm_log_dt, m_ssm_b_re, m_ssm_b_im, m_ssm_c_re, m_ssm_c_im, m_ssm_d, m_w_glu, m_b_glu, m_w_mem_kv, m_w_br_ssm, m_w_br_attn, m_w_br_mem, m_w_out, m_rel_bias, m_final_norm_g, v_norm_g, v_mem_norm_g, v_w_in, v_b_gate, v_ssm_lambda_re, v_ssm_lambda_im, v_ssm_log_dt, v_ssm_b_re, v_ssm_b_im, v_ssm_c_re, v_ssm_c_im, v_ssm_d, v_w_glu, v_b_glu, v_w_mem_kv, v_w_br_ssm, v_w_br_attn, v_w_br_mem, v_w_out, v_rel_bias, v_final_norm_g):
    given = dict(locals())
    w = {n: given[n] for n in _WEIGHTS}
    m = {n: given["m_" + n] for n in _WEIGHTS}
    v = {n: given["v_" + n] for n in _WEIGHTS}

    turned = lambda n, a: a.swapaxes(1, 2) if n == "w_in" else a
    shards = [turned(n, w[n]).astype(BF16) for n, _, _ in _SHARDED]
    loss, dx, parts, rparts = _train_step(x[0], mem[0], loss_target[0], shards, w)
    loss = lax.psum(loss, ("x", "y", "c"))

    new = {}
    for (n, _, _), p in zip(_SHARDED, parts):
        new[n] = [turned(n, a) for a in _adamw(p, turned(n, w[n]), turned(n, m[n]), turned(n, v[n]), _ADAM_ROWS[n],
                                               f"adamw_{n}")]
    rp = [_unpack_replicated(a) for a in _adamw(rparts[:, None], _pack_replicated(w), _pack_replicated(m),
                                                _pack_replicated(v), _REP_ROWS // 4, "adamw_replicated")]
    for n, _ in _REPLICATED:
        new[n] = [rp[kind][n] for kind in range(4)]
    outs = [loss, dx[None]]
    for kind in range(4):
        outs.extend(new[n][kind] for n in _WEIGHTS)
    return tuple(outs)
```

```python
import functools
import math
from typing import Callable, NamedTuple

import jax
import jax.numpy as jnp
import numpy as np
from jax import lax
from jax.experimental import pallas as pl
from jax.experimental.pallas import tpu as pltpu

F32 = jnp.float32
BF16 = jnp.bfloat16

D_MODEL = 1024
DEPTH = 2
EPS = 1e-6
D_SSM = 768
SSM_GROUP = 16
SSM_GROUPS = 48
SSM_STATE = 64
N_STATE = SSM_GROUPS * SSM_STATE
SSM_BLOCKS = 6
D_ATTN = 768
ATTN_HEAD_DIM = 64
ATTN_GROUP_WIDTH = 256
ATTN_DILATIONS = (1, 4, 16)
ATTN_SPAN = 128
ATTN_BLOCK = 128
NUM_BUCKETS = 32
REL_MAX_DISTANCE = 2048
NEG_INF = -1e30
MEM_HEADS = 4
MEM_HEAD_DIM = 128
D_MEM = 512
N_GATES = 3 * D_MODEL
D_IN = 8704
N_DEV = 8
LANES = 128
ADAM_LR = 0.001
ADAM_B1 = 0.9
ADAM_B2 = 0.999
ADAM_EPS = 1e-08
ADAM_WD = 0.01
ADAM_STEP = 10

_OFF = {"u": 0, "z_ssm": 768, "q": 1536, "k": 2304, "v": 3072, "z_attn": 3840, "q_mem": 4608, "z_mem": 5120,
        "gates": 5632}
GATE_BLOCK = 512

NN = (((1,), (0,)), ((), ()))
NT = (((1,), (1,)), ((), ()))
TN = (((0,), (0,)), ((), ()))

VMEM_LIMIT = 56 * 1024 * 1024


def _dot(a, b, dims=NN):
    return lax.dot_general(a, b, dims, preferred_element_type=F32)


def _sigmoid(x):
    return 1.0 / (1.0 + jnp.exp(-x))


def _gelu_parts(x):
    k = math.sqrt(2.0 / math.pi)
    t = jnp.tanh(k * (x + 0.044715 * (x * x * x)))
    cdf = 0.5 * (1.0 + t)
    dcdf = 0.5 * (1.0 - t * t) * k * (1.0 + 3.0 * 0.044715 * (x * x))
    return x * cdf, cdf + x * dcdf


def _params(sem, vmem=VMEM_LIMIT):
    return pltpu.CompilerParams(dimension_semantics=sem, vmem_limit_bytes=vmem)


def _full(shape):
    return pl.BlockSpec(shape, lambda *_: (0,) * len(shape))


def _norm_proj(x, g, w, tm, tn, name, out_dtype=F32, job=None, w_turned=False):
    T, D = x.shape
    N = w.shape[0] if w_turned else w.shape[1]
    w_spec = pl.BlockSpec((tn, D), lambda i, j: (j, 0)) if w_turned else pl.BlockSpec((D, tn), lambda i, j: (0, j))
    dims = NT if w_turned else NN

    def body(x_ref, g_ref, w_ref, o_ref, h_ref, hs):
        @pl.when(pl.program_id(1) == 0)
        def _():
            xv = x_ref[...]
            r = lax.rsqrt(jnp.mean(xv * xv, axis=-1, keepdims=True) + EPS)
            hv = (xv * r * g_ref[...]).astype(BF16)
            hs[...] = hv
            h_ref[...] = hv

        o_ref[...] = _dot(hs[...], w_ref[...], dims).astype(out_dtype)

    return _pc(
        body, job, name=name, grid=(T // tm, N // tn),
        in_specs=[pl.BlockSpec((tm, D), lambda i, j: (i, 0)), _full((1, D)), w_spec],
        out_specs=[pl.BlockSpec((tm, tn), lambda i, j: (i, j)), pl.BlockSpec((tm, D), lambda i, j: (i, 0))],
        out_shape=[jax.ShapeDtypeStruct((T, N), out_dtype), jax.ShapeDtypeStruct((T, D), BF16)],
        scratch_shapes=[pltpu.VMEM((tm, D), BF16)], sem=("parallel", "arbitrary"), operands=(x, g, w))


def _mm_tn(a, b, tm, tn, tk, name, b_col=0, n=None, job=None):
    K, M = a.shape
    N = b.shape[1] if n is None else n
    nk = K // tk
    j0 = b_col // tn

    def body(a_ref, b_ref, o_ref, acc):
        k = pl.program_id(2)

        @pl.when(k == 0)
        def _():
            acc[...] = jnp.zeros_like(acc)

        acc[...] += _dot(a_ref[...].astype(BF16), b_ref[...].astype(BF16), TN)

        @pl.when(k == nk - 1)
        def _():
            o_ref[...] = acc[...].astype(BF16)

    out = _pc(
        body, job, name=name, grid=(M // tm, N // tn, nk),
        in_specs=[pl.BlockSpec((tk, tm), lambda i, j, k: (k, i)), pl.BlockSpec((tk, tn), lambda i, j, k: (k, j0 + j))],
        out_specs=[pl.BlockSpec((tm, tn), lambda i, j, k: (i, j))],
        out_shape=[jax.ShapeDtypeStruct((M, N), BF16)],
        scratch_shapes=[pltpu.VMEM((tm, tn), F32)], sem=("parallel", "parallel", "arbitrary"), operands=(a, b))
    return out[0] if job is None else (out[0][0], out[1])


def _proj_bwd(dp, w, x, g, dres, tm, tk, name, job=None, w_turned=False):
    T, N = dp.shape
    D = x.shape[1]
    nk = N // tk
    w_spec = pl.BlockSpec((tk, D), lambda i, k: (k, 0)) if w_turned else pl.BlockSpec((D, tk), lambda i, k: (0, k))
    dims = NN if w_turned else NT

    def body(dp_ref, w_ref, x_ref, g_ref, dres_ref, dx_ref, dg_ref, acc):
        i, k = pl.program_id(0), pl.program_id(1)

        @pl.when(k == 0)
        def _():
            acc[...] = jnp.zeros_like(acc)

        @pl.when((i == 0) & (k == 0))
        def _():
            dg_ref[...] = jnp.zeros_like(dg_ref)

        acc[...] += _dot(dp_ref[...], w_ref[...], dims)

        @pl.when(k == nk - 1)
        def _():
            xv = x_ref[...]
            dh = acc[...]
            r = lax.rsqrt(jnp.mean(xv * xv, axis=-1, keepdims=True) + EPS)
            xr = xv * r
            dg_ref[...] += jnp.sum(dh * xr, axis=0, keepdims=True)
            wv = dh * g_ref[...]
            dx_ref[...] = dres_ref[...] + r * (wv - xr * jnp.mean(wv * xr, axis=-1, keepdims=True))

    return _pc(
        body, job, name=name, grid=(T // tm, nk),
        in_specs=[pl.BlockSpec((tm, tk), lambda i, k: (i, k)), w_spec,
                  pl.BlockSpec((tm, D), lambda i, k: (i, 0)), _full((1, D)),
                  pl.BlockSpec((tm, D), lambda i, k: (i, 0))],
        out_specs=[pl.BlockSpec((tm, D), lambda i, k: (i, 0)), _full((1, D))],
        out_shape=[jax.ShapeDtypeStruct((T, D), F32), jax.ShapeDtypeStruct((1, D), F32)],
        scratch_shapes=[pltpu.VMEM((tm, D), F32)], sem=("arbitrary", "arbitrary"), operands=(dp, w, x, g, dres))


def _ssm_fwd(proj, bre, bim, cre, cimn, are, aim, d, wglu, bglu, tc, name, job=None):
    T = proj.shape[0]
    ucol, zcol = _OFF["u"] // D_SSM, _OFF["z_ssm"] // D_SSM

    def body(u_ref, z_ref, bre_ref, bim_ref, cre_ref, cim_ref, are_ref, aim_ref, d_ref, wg_ref, bg_ref,
             xr_ref, xi_ref, y_ref, o_ref, car_r, car_i):
        @pl.when(pl.program_id(0) == 0)
        def _():
            car_r[...] = jnp.zeros_like(car_r)
            car_i[...] = jnp.zeros_like(car_i)

        u = u_ref[...]
        ub = u.astype(BF16)
        for k in range(SSM_BLOCKS):
            uk = ub[:, 128 * k:128 * (k + 1)]
            xr_ref[:, 512 * k:512 * (k + 1)] = _dot(uk, bre_ref[k])
            xi_ref[:, 512 * k:512 * (k + 1)] = _dot(uk, bim_ref[k])
        ar, ai = are_ref[...], aim_ref[...]

        def step(t, c):
            pr, pi = c
            nr = ar * pr - ai * pi + xr_ref[pl.ds(t, 1), :]
            ni = ar * pi + ai * pr + xi_ref[pl.ds(t, 1), :]
            xr_ref[pl.ds(t, 1), :] = nr
            xi_ref[pl.ds(t, 1), :] = ni
            return nr, ni

        pr, pi = lax.fori_loop(0, tc, step, (car_r[...], car_i[...]))
        car_r[...] = pr
        car_i[...] = pi

        ys = []
        for k in range(SSM_BLOCKS):
            xrk = xr_ref[:, 512 * k:512 * (k + 1)].astype(BF16)
            xik = xi_ref[:, 512 * k:512 * (k + 1)].astype(BF16)
            ys.append(_dot(xrk, cre_ref[k]) + _dot(xik, cim_ref[k]))
        y = jnp.concatenate(ys, axis=1) + d_ref[...] * u
        y_ref[...] = y
        gl, _ = _gelu_parts(y)
        t = _dot(gl.astype(BF16), wg_ref[...]) + bg_ref[...]
        z = z_ref[...]
        o_ref[...] = (gl * _sigmoid(t) * (z * _sigmoid(z))).astype(BF16)

    return _pc(
        body, job, name=name, grid=(T // tc,),
        in_specs=[pl.BlockSpec((tc, D_SSM), lambda i: (i, ucol)), pl.BlockSpec((tc, D_SSM), lambda i: (i, zcol)),
                  _full((SSM_BLOCKS, 128, 512)), _full((SSM_BLOCKS, 128, 512)),
                  _full((SSM_BLOCKS, 512, 128)), _full((SSM_BLOCKS, 512, 128)),
                  _full((1, N_STATE)), _full((1, N_STATE)), _full((1, D_SSM)),
                  _full((D_SSM, D_SSM)), _full((1, D_SSM))],
        out_specs=[pl.BlockSpec((tc, N_STATE), lambda i: (i, 0)), pl.BlockSpec((tc, N_STATE), lambda i: (i, 0)),
                   pl.BlockSpec((tc, D_SSM), lambda i: (i, 0)), pl.BlockSpec((tc, D_SSM), lambda i: (i, 0))],
        out_shape=[jax.ShapeDtypeStruct((T, N_STATE), F32), jax.ShapeDtypeStruct((T, N_STATE), F32),
                   jax.ShapeDtypeStruct((T, D_SSM), F32), jax.ShapeDtypeStruct((T, D_SSM), BF16)],
        scratch_shapes=[pltpu.VMEM((1, N_STATE), F32), pltpu.VMEM((1, N_STATE), F32)], sem=("arbitrary",),
        operands=(proj, proj, bre, bim, cre, cimn, are, aim, d, wglu, bglu))


def _glu_bwd(do, y, proj, wglu, bglu, dproj, tm, name):
    T = y.shape[0]
    zcol = _OFF["z_ssm"] // D_SSM

    def body(do_ref, y_ref, z_ref, wg_ref, bg_ref, _, dy_ref, dz_ref, g_ref, dt_ref, db_ref):
        @pl.when(pl.program_id(0) == 0)
        def _():
            db_ref[...] = jnp.zeros_like(db_ref)

        dov = do_ref[...]
        gl, dgl = _gelu_parts(y_ref[...])
        glb = gl.astype(BF16)
        sg = _sigmoid(_dot(glb, wg_ref[...]) + bg_ref[...])
        z = z_ref[...]
        sz = _sigmoid(z)
        dz_ref[...] = (dov * (gl * sg) * (sz * (1.0 + z * (1.0 - sz)))).astype(BF16)
        dy2 = dov * (z * sz)
        dt = dy2 * gl * (sg * (1.0 - sg))
        dtb = dt.astype(BF16)
        dg = dy2 * sg + _dot(dtb, wg_ref[...], NT)
        dy_ref[...] = dg * dgl
        g_ref[...] = glb
        dt_ref[...] = dtb
        db_ref[...] += jnp.sum(dt, axis=0, keepdims=True)

    row = lambda i: (i, 0)
    return pl.pallas_call(
        body, name=name, grid=(T // tm,),
        in_specs=[pl.BlockSpec((tm, D_SSM), row), pl.BlockSpec((tm, D_SSM), row),
                  pl.BlockSpec((tm, D_SSM), lambda i: (i, zcol)), _full((D_SSM, D_SSM)), _full((1, D_SSM)),
                  pl.BlockSpec(memory_space=pl.ANY)],
        out_specs=[pl.BlockSpec((tm, D_SSM), row), pl.BlockSpec((tm, D_SSM), lambda i: (i, zcol)),
                   pl.BlockSpec((tm, D_SSM), row), pl.BlockSpec((tm, D_SSM), row), _full((1, D_SSM))],
        out_shape=[jax.ShapeDtypeStruct((T, D_SSM), F32), jax.ShapeDtypeStruct(dproj.shape, BF16),
                   jax.ShapeDtypeStruct((T, D_SSM), BF16), jax.ShapeDtypeStruct((T, D_SSM), BF16),
                   jax.ShapeDtypeStruct((1, D_SSM), F32)],
        input_output_aliases={5: 1},
        compiler_params=_params(("arbitrary",)),
    )(do, y, proj, wglu, bglu, dproj)


def _ssm_bwd(dy, proj, xr, xi, ctre, ctimn, btre, btim, are, aim, d, dproj, tc, name, job=None):
    T = dy.shape[0]
    nc = T // tc
    ucol = _OFF["u"] // D_SSM
    rb = tc // 8

    def body(dy_ref, u_ref, xr_ref, xi_ref, xpr_ref, xpi_ref, ctre_ref, ctim_ref, btre_ref, btim_ref,
             are_ref, aim_ref, d_ref, _,
             du_ref, dbre_ref, dbim_ref, dcre_ref, dcim_ref, dare_ref, daim_ref, dd_ref, gr, gi, car_r, car_i):
        i = pl.program_id(0)

        @pl.when(i == 0)
        def _():
            for ref in (car_r, car_i, dbre_ref, dbim_ref, dcre_ref, dcim_ref, dare_ref, daim_ref, dd_ref):
                ref[...] = jnp.zeros_like(ref)

        dyv = dy_ref[...]
        dyb = dyv.astype(BF16)
        u = u_ref[...]
        ub = u.astype(BF16)
        for k in range(SSM_BLOCKS):
            dk = dyb[:, 128 * k:128 * (k + 1)]
            gr[:, 512 * k:512 * (k + 1)] = _dot(dk, ctre_ref[k])
            gi[:, 512 * k:512 * (k + 1)] = _dot(dk, ctim_ref[k])
        ar, ai = are_ref[...], aim_ref[...]

        def step(s, c):
            pr, pi = c
            t = tc - 1 - s
            nr = gr[pl.ds(t, 1), :] + ar * pr + ai * pi
            ni = gi[pl.ds(t, 1), :] + ar * pi - ai * pr
            gr[pl.ds(t, 1), :] = nr
            gi[pl.ds(t, 1), :] = ni
            return nr, ni

        pr, pi = lax.fori_loop(0, tc, step, (car_r[...], car_i[...]))
        car_r[...] = pr
        car_i[...] = pi

        keep = jnp.where(i == nc - 1, 0.0, 1.0)
        row0 = lax.broadcasted_iota(jnp.int32, (tc, 1), 0) == 0
        dd_ref[...] += jnp.sum(dyv * u, axis=0, keepdims=True)
        for k in range(SSM_BLOCKS):
            sl = slice(512 * k, 512 * (k + 1))
            ch = slice(128 * k, 128 * (k + 1))
            xrk, xik, grk, gik = xr_ref[:, sl], xi_ref[:, sl], gr[:, sl], gi[:, sl]
            xsr = jnp.where(row0, xpr_ref[7:8, sl] * keep, pltpu.roll(xrk, 1, axis=0))
            xsi = jnp.where(row0, xpi_ref[7:8, sl] * keep, pltpu.roll(xik, 1, axis=0))
            dare_ref[:, sl] += jnp.sum(grk * xsr + gik * xsi, axis=0, keepdims=True)
            daim_ref[:, sl] += jnp.sum(gik * xsr - grk * xsi, axis=0, keepdims=True)
            grb, gib = grk.astype(BF16), gik.astype(BF16)
            du_ref[:, ch] = (_dot(grb, btre_ref[k]) + _dot(gib, btim_ref[k])
                             + d_ref[:, ch] * dyv[:, ch]).astype(BF16)
            dbre_ref[k] += _dot(grb, ub[:, ch], TN)
            dbim_ref[k] += _dot(gib, ub[:, ch], TN)
            dcre_ref[k] += _dot(dyb[:, ch], xrk.astype(BF16), TN)
            dcim_ref[k] -= _dot(dyb[:, ch], xik.astype(BF16), TN)

    rev = lambda i: (nc - 1 - i, 0)
    prev = lambda i: (jnp.maximum((nc - 1 - i) * rb - 1, 0), 0)
    return _pc(
        body, job, name=name, grid=(nc,),
        in_specs=[pl.BlockSpec((tc, D_SSM), rev), pl.BlockSpec((tc, D_SSM), lambda i: (nc - 1 - i, ucol)),
                  pl.BlockSpec((tc, N_STATE), rev), pl.BlockSpec((tc, N_STATE), rev),
                  pl.BlockSpec((8, N_STATE), prev), pl.BlockSpec((8, N_STATE), prev),
                  _full((SSM_BLOCKS, 128, 512)), _full((SSM_BLOCKS, 128, 512)),
                  _full((SSM_BLOCKS, 512, 128)), _full((SSM_BLOCKS, 512, 128)),
                  _full((1, N_STATE)), _full((1, N_STATE)), _full((1, D_SSM)), pl.BlockSpec(memory_space=pl.ANY)],
        out_specs=[pl.BlockSpec((tc, D_SSM), lambda i: (nc - 1 - i, ucol)),
                   _full((SSM_BLOCKS, 512, 128)), _full((SSM_BLOCKS, 512, 128)),
                   _full((SSM_BLOCKS, 128, 512)), _full((SSM_BLOCKS, 128, 512)),
                   _full((1, N_STATE)), _full((1, N_STATE)), _full((1, D_SSM))],
        out_shape=[jax.ShapeDtypeStruct(dproj.shape, BF16),
                   jax.ShapeDtypeStruct((SSM_BLOCKS, 512, 128), F32), jax.ShapeDtypeStruct((SSM_BLOCKS, 512, 128), F32),
                   jax.ShapeDtypeStruct((SSM_BLOCKS, 128, 512), F32), jax.ShapeDtypeStruct((SSM_BLOCKS, 128, 512), F32),
                   jax.ShapeDtypeStruct((1, N_STATE), F32), jax.ShapeDtypeStruct((1, N_STATE), F32),
                   jax.ShapeDtypeStruct((1, D_SSM), F32)],
        scratch_shapes=[pltpu.VMEM((tc, N_STATE), F32), pltpu.VMEM((tc, N_STATE), F32),
                        pltpu.VMEM((1, N_STATE), F32), pltpu.VMEM((1, N_STATE), F32)], sem=("arbitrary",),
        operands=(dy, proj, xr, xi, xr, xi, ctre, ctimn, btre, btim, are, aim, d, dproj), aliases={13: 0})


def _rel_bucket(dist):
    n = jnp.maximum(dist, 0)
    max_exact = NUM_BUCKETS // 2
    n_f = jnp.maximum(n, 1).astype(F32)
    large = max_exact + (jnp.log(n_f / max_exact) / math.log(REL_MAX_DISTANCE / max_exact)
                         * (NUM_BUCKETS - max_exact)).astype(jnp.int32)
    large = jnp.minimum(large, NUM_BUCKETS - 1)
    return jnp.where(n < max_exact, n, large)


def _bucket_tables():
    qi = jnp.arange(ATTN_BLOCK)[:, None]
    kj = jnp.arange(2 * ATTN_BLOCK)[None, :]
    delta = jnp.maximum(ATTN_BLOCK + qi - kj, 0)
    return jnp.stack([_rel_bucket(delta * r) for r in ATTN_DILATIONS]).astype(jnp.int32)


def _bias_tables(rel_bias, buckets, name):
    def body(tab_ref, bk_ref, o_ref):
        g = pl.program_id(0)
        bk = bk_ref[...]
        qi = lax.broadcasted_iota(jnp.int32, bk.shape, 0)
        kj = lax.broadcasted_iota(jnp.int32, bk.shape, 1)
        delta = ATTN_BLOCK + qi - kj
        band = (delta >= 0) & (delta <= ATTN_SPAN)
        accs = [jnp.zeros(bk.shape, F32) for _ in range(4)]
        for b in range(NUM_BUCKETS):
            hit = bk == b
            for h in range(4):
                accs[h] = jnp.where(hit, tab_ref[b, 4 * g + h], accs[h])
        for h in range(4):
            o_ref[h] = jnp.where(band, accs[h], NEG_INF)

    return pl.pallas_call(
        body, name=name, grid=(3,),
        in_specs=[pl.BlockSpec(memory_space=pltpu.SMEM),
                  pl.BlockSpec((None, ATTN_BLOCK, 2 * ATTN_BLOCK), lambda g: (g, 0, 0))],
        out_specs=pl.BlockSpec((None, 4, ATTN_BLOCK, 2 * ATTN_BLOCK), lambda g: (g, 0, 0, 0)),
        out_shape=jax.ShapeDtypeStruct((3, 4, ATTN_BLOCK, 2 * ATTN_BLOCK), F32),
        compiler_params=_params(("parallel",)),
    )(rel_bias, buckets)


def _bias_grad(db0, db1, buckets, name):
    def body(a_ref, b_ref, bk_ref, o_ref):
        bk = bk_ref[...]
        for h in range(4):
            dv = a_ref[h] + b_ref[h]
            for b in range(NUM_BUCKETS):
                o_ref[h, b:b + 1, :] = jnp.sum(jnp.where(bk == b, dv, 0.0), axis=0, keepdims=True)

    tab = pl.BlockSpec((None, 4, ATTN_BLOCK, 2 * ATTN_BLOCK), lambda g: (g, 0, 0, 0))
    return pl.pallas_call(
        body, name=name, grid=(3,),
        in_specs=[tab, tab, pl.BlockSpec((None, ATTN_BLOCK, 2 * ATTN_BLOCK), lambda g: (g, 0, 0))],
        out_specs=pl.BlockSpec((None, 4, NUM_BUCKETS, 2 * ATTN_BLOCK), lambda g: (g, 0, 0, 0)),
        out_shape=jax.ShapeDtypeStruct((3, 4, NUM_BUCKETS, 2 * ATTN_BLOCK), F32),
        compiler_params=_params(("parallel",)),
    )(db0, db1, buckets)


_ATTN_SUB = {1: 4, 4: 1, 16: 1}
_UNROLL = 4


def _unit_rows(j, s, r):
    start = j * ATTN_BLOCK * r + s
    return pl.ds(start, ATTN_BLOCK, stride=r) if r > 1 else pl.ds(start, ATTN_BLOCK)


def _for_units(r, nsub, fn, after):
    if r * nsub <= _UNROLL:
        after([fn(j, s) for j in range(nsub) for s in range(r)])
    else:
        def four(i, c):
            after([fn(0, _UNROLL * i + k) for k in range(_UNROLL)])
            return c

        lax.fori_loop(0, r // _UNROLL, four, 0)


def _attn_cols(g):
    return tuple((_OFF[n] + ATTN_GROUP_WIDTH * g) // LANES for n in ("q", "k", "v"))


def _attn_fwd(proj, bias, g, name, job=None):
    r = ATTN_DILATIONS[g]
    nsub = _ATTN_SUB[r]
    T = proj.shape[0]
    sub = ATTN_BLOCK * r
    tb = sub * nsub
    qc, kc, vc = _attn_cols(g)
    scale = ATTN_HEAD_DIM ** -0.5

    def body(q_ref, kc_ref, kp_ref, vc_ref, vp_ref, bias_ref, o_ref, lse_ref):
        lane = lax.broadcasted_iota(jnp.int32, (ATTN_BLOCK, LANES), 1)
        kj = lax.broadcasted_iota(jnp.int32, (ATTN_BLOCK, 2 * ATTN_BLOCK), 1)
        dead = (pl.program_id(0) == 0) & (kj < ATTN_BLOCK)

        def one(j, s):
            rows = _unit_rows(j, s, r)
            before = _unit_rows(max(j - 1, 0), s, r)
            k_before = kc_ref[before, :] if j else kp_ref[before, :]
            v_before = vc_ref[before, :] if j else vp_ref[before, :]
            q = q_ref[rows, :]
            kcat = jnp.concatenate([k_before, kc_ref[rows, :]], axis=0).astype(BF16)
            vcat = jnp.concatenate([v_before, vc_ref[rows, :]], axis=0).astype(BF16)
            o_acc = jnp.zeros((ATTN_BLOCK, LANES), F32)
            l_acc = jnp.zeros((ATTN_BLOCK, LANES), F32)
            for hh in range(2):
                mine = (lane >= ATTN_HEAD_DIM) if hh else (lane < ATTN_HEAD_DIM)
                qm = jnp.where(mine, q, 0.0).astype(BF16)
                sc = _dot(qm, kcat, NT) * scale + bias_ref[hh]
                if j == 0:
                    sc = jnp.where(dead, NEG_INF, sc)
                m = jnp.max(sc, axis=-1, keepdims=True)
                p = jnp.exp(sc - m)
                l = jnp.sum(p, axis=-1, keepdims=True)
                o_acc = jnp.where(mine, _dot((p / l).astype(BF16), vcat), o_acc)
                l_acc = jnp.where(mine, m + jnp.log(l), l_acc)
            o_ref[rows, :] = o_acc
            lse_ref[rows, :] = l_acc

        _for_units(r, nsub, one, lambda results: None)

    cur = lambda c: pl.BlockSpec((tb, LANES), lambda b, p: (b, c + p))
    prev = lambda c: pl.BlockSpec((sub, LANES), lambda b, p: (jnp.maximum(b * nsub - 1, 0), c + p))
    out = pl.BlockSpec((tb, LANES), lambda b, p: (b, p))
    return _pc(
        body, job, name=name, grid=(T // tb, 2),
        in_specs=[cur(qc), cur(kc), prev(kc), cur(vc), prev(vc),
                  pl.BlockSpec((2, ATTN_BLOCK, 2 * ATTN_BLOCK), lambda b, p: (p, 0, 0))],
        out_specs=[out, out],
        out_shape=[jax.ShapeDtypeStruct((T, ATTN_GROUP_WIDTH), F32), jax.ShapeDtypeStruct((T, ATTN_GROUP_WIDTH), F32)],
        scratch_shapes=[], sem=("parallel", "parallel"), operands=(proj, proj, proj, proj, proj, bias))


def _attn_bwd(proj, do, corr, lse, bias, dproj, g, name):
    r = ATTN_DILATIONS[g]
    nsub = _ATTN_SUB[r]
    T = proj.shape[0]
    sub = ATTN_BLOCK * r
    tb = sub * nsub
    nb = T // tb
    qc, kc, vc = _attn_cols(g)
    dc = ATTN_GROUP_WIDTH * g // LANES
    scale = ATTN_HEAD_DIM ** -0.5

    def body(q_ref, kc_ref, kp_ref, vc_ref, vp_ref, do_ref, corr_ref, lse_ref, bias_ref, _,
             dproj_ref, db_ref, dq_s, dkc_s, dkp_s, dvc_s, dvp_s, kacc, vacc, stage, stage_sems):
        p, b = pl.program_id(0), pl.program_id(1)

        def to_dproj(e, slot, block, col):
            rows = pl.ds(pl.multiple_of(block * tb, tb), tb)
            cols = pl.ds(pl.multiple_of((col + p) * LANES, LANES), LANES)
            return pltpu.make_async_copy(stage.at[e, slot], dproj_ref.at[rows, cols], stage_sems.at[e, slot])

        def emit(e, block, col, value):
            count = p * nb + block
            slot = count % 2

            @pl.when(count >= 2)
            def _():
                to_dproj(e, slot, 0, col).wait()

            stage[e, slot] = value.astype(BF16)
            to_dproj(e, slot, block, col).start()

        def emit_keys(block):
            emit(1, block, kc, kacc[...])
            emit(2, block, vc, vacc[...])

        @pl.when(b == 0)
        def _():
            db_ref[...] = jnp.zeros_like(db_ref)
            kacc[...] = jnp.zeros_like(kacc)
            vacc[...] = jnp.zeros_like(vacc)

        @pl.when(b == nb)
        def _():
            emit_keys(nb - 1)

        @pl.when((b == nb) & (p == 1))
        def _():
            for e, col in enumerate((qc, kc, vc)):
                for slot in range(2):
                    to_dproj(e, slot, 0, col).wait()

        @pl.when(b < nb)
        def _():
            lane = lax.broadcasted_iota(jnp.int32, (ATTN_BLOCK, LANES), 1)
            kj = lax.broadcasted_iota(jnp.int32, (ATTN_BLOCK, 2 * ATTN_BLOCK), 1)
            dead = (b == 0) & (kj < ATTN_BLOCK)

            def one(j, s):
                rows = _unit_rows(j, s, r)
                before = _unit_rows(max(j - 1, 0), s, r)
                k_before = kc_ref[before, :] if j else kp_ref[before, :]
                v_before = vc_ref[before, :] if j else vp_ref[before, :]
                q = q_ref[rows, :]
                kcat = jnp.concatenate([k_before, kc_ref[rows, :]], axis=0).astype(BF16)
                vcat = jnp.concatenate([v_before, vc_ref[rows, :]], axis=0).astype(BF16)
                dov, corrv, lsev = do_ref[rows, :], corr_ref[rows, :], lse_ref[rows, :]
                dq_acc = jnp.zeros((ATTN_BLOCK, LANES), F32)
                dk_acc = jnp.zeros((2 * ATTN_BLOCK, LANES), F32)
                dv_acc = jnp.zeros((2 * ATTN_BLOCK, LANES), F32)
                dss = []
                for hh in range(2):
                    mine = (lane >= ATTN_HEAD_DIM) if hh else (lane < ATTN_HEAD_DIM)
                    col = slice(ATTN_HEAD_DIM * hh, ATTN_HEAD_DIM * hh + 1)
                    qm = jnp.where(mine, q, 0.0).astype(BF16)
                    dom = jnp.where(mine, dov, 0.0).astype(BF16)
                    sc = _dot(qm, kcat, NT) * scale + bias_ref[hh]
                    if j == 0:
                        sc = jnp.where(dead, NEG_INF, sc)
                    p = jnp.exp(sc - lsev[:, col])
                    ds = p * (_dot(dom, vcat, NT) - corrv[:, col])
                    dss.append(ds)
                    dsb = ds.astype(BF16)
                    dq_acc = jnp.where(mine, _dot(dsb, kcat) * scale, dq_acc)
                    dk_acc += _dot(dsb, qm, TN) * scale
                    dv_acc += _dot(p.astype(BF16), dom, TN)
                dq_s[rows, :] = dq_acc
                dkp_s[rows, :] = dk_acc[:ATTN_BLOCK]
                dkc_s[rows, :] = dk_acc[ATTN_BLOCK:]
                dvp_s[rows, :] = dv_acc[:ATTN_BLOCK]
                dvc_s[rows, :] = dv_acc[ATTN_BLOCK:]
                return dss

            def add_bias_grads(results):
                for hh in range(2):
                    db_ref[hh] += functools.reduce(lambda x, y: x + y, [dss[hh] for dss in results])

            _for_units(r, nsub, one, add_bias_grads)
            emit(0, b, qc, dq_s[...])
            tail = slice((nsub - 1) * sub, nsub * sub)
            kacc[tail, :] += dkp_s[0:sub, :]
            vacc[tail, :] += dvp_s[0:sub, :]

            @pl.when(b >= 1)
            def _():
                emit_keys(b - 1)

            for acc, before_s, cur_s in ((kacc, dkp_s, dkc_s), (vacc, dvp_s, dvc_s)):
                acc[...] = cur_s[...]
                for j in range(nsub - 1):
                    acc[j * sub:(j + 1) * sub, :] += before_s[(j + 1) * sub:(j + 2) * sub, :]

    last = nb - 1
    blk = (tb, LANES)
    cur = lambda c: pl.BlockSpec(blk, lambda p, b: (jnp.minimum(b, last), c + p))
    before = lambda c: pl.BlockSpec((sub, LANES), lambda p, b: (jnp.clip(b * nsub - 1, 0, nb * nsub - 1), c + p))
    tab = pl.BlockSpec((2, ATTN_BLOCK, 2 * ATTN_BLOCK), lambda p, b: (p, 0, 0))
    hbm = pl.BlockSpec(memory_space=pl.ANY)
    return pl.pallas_call(
        body, name=name, grid=(2, nb + 1),
        in_specs=[cur(qc), cur(kc), before(kc), cur(vc), before(vc), cur(dc), cur(dc), cur(0), tab, hbm],
        out_specs=[hbm, tab],
        out_shape=[jax.ShapeDtypeStruct(dproj.shape, BF16), jax.ShapeDtypeStruct((4, ATTN_BLOCK, 2 * ATTN_BLOCK), F32)],
        input_output_aliases={9: 0},
        scratch_shapes=[pltpu.VMEM(blk, F32)] * 7 + [pltpu.VMEM((3, 2) + blk, BF16), pltpu.SemaphoreType.DMA((3, 2))],
        compiler_params=_params(("arbitrary", "arbitrary")),
    )(proj, proj, proj, proj, proj, do, corr, lse, bias, dproj)


def _mix_weights(lses):
    m = jnp.maximum(jnp.maximum(lses[0], lses[1]), lses[2])
    es = [jnp.exp(l - m) for l in lses]
    inv = 1.0 / (es[0] + es[1] + es[2])
    return jnp.concatenate([e * inv for e in es], axis=1)


def _attn_mix(os, lses, proj, tm, name):
    T = proj.shape[0]
    zcol = _OFF["z_attn"] // D_ATTN

    def body(o0, o1, o2, l0, l1, l2, z_ref, out_ref):
        z = z_ref[...]
        o = jnp.concatenate([o0[...], o1[...], o2[...]], axis=1)
        alpha = _mix_weights([l0[...], l1[...], l2[...]])
        out_ref[...] = (o * alpha * (z * _sigmoid(z))).astype(BF16)

    row = lambda i: (i, 0)
    grp = pl.BlockSpec((tm, ATTN_GROUP_WIDTH), row)
    return pl.pallas_call(
        body, name=name, grid=(T // tm,),
        in_specs=[grp] * 6 + [pl.BlockSpec((tm, D_ATTN), lambda i: (i, zcol))],
        out_specs=pl.BlockSpec((tm, D_ATTN), row),
        out_shape=jax.ShapeDtypeStruct((T, D_ATTN), BF16),
        compiler_params=_params(("parallel",)),
    )(*os, *lses, proj)


def _attn_mix_bwd(d, os, lses, proj, dproj, tm, name):
    T = proj.shape[0]
    zcol = _OFF["z_attn"] // D_ATTN

    def body(d_ref, o0, o1, o2, l0, l1, l2, z_ref, _, do_ref, corr_ref, dz_ref):
        dv, z = d_ref[...], z_ref[...]
        ov = jnp.concatenate([o0[...], o1[...], o2[...]], axis=1)
        alpha = _mix_weights([l0[...], l1[...], l2[...]])
        sz = _sigmoid(z)
        oc = ov * alpha
        dz_ref[...] = (dv * oc * (sz * (1.0 + z * (1.0 - sz)))).astype(BF16)
        doc = dv * (z * sz)
        do_ref[...] = doc * alpha
        pr = doc * oc
        p3 = pr[:, 0:256] + pr[:, 256:512] + pr[:, 512:768]
        li = lax.broadcasted_iota(jnp.int32, (256, 256), 0) // ATTN_HEAD_DIM
        lj = lax.broadcasted_iota(jnp.int32, (256, 256), 1) // ATTN_HEAD_DIM
        ones = jnp.where(li == lj, 1.0, 0.0).astype(F32)
        s = lax.dot_general(p3, ones, NN, precision=lax.Precision.HIGHEST, preferred_element_type=F32)
        corr_ref[...] = alpha * jnp.concatenate([s, s, s], axis=1)

    row = lambda i: (i, 0)
    grp = pl.BlockSpec((tm, ATTN_GROUP_WIDTH), row)
    return pl.pallas_call(
        body, name=name, grid=(T // tm,),
        in_specs=[pl.BlockSpec((tm, D_ATTN), row)] + [grp] * 6 + [pl.BlockSpec((tm, D_ATTN), lambda i: (i, zcol)),
                                                                    pl.BlockSpec(memory_space=pl.ANY)],
        out_specs=[pl.BlockSpec((tm, D_ATTN), row)] * 2 + [pl.BlockSpec((tm, D_ATTN), lambda i: (i, zcol))],
        out_shape=[jax.ShapeDtypeStruct((T, D_ATTN), F32), jax.ShapeDtypeStruct((T, D_ATTN), F32),
                   jax.ShapeDtypeStruct(dproj.shape, BF16)],
        input_output_aliases={8: 2},
        compiler_params=_params(("parallel",)),
    )(d, *os, *lses, proj, dproj)


def _mem_probs(q_ref, kv_ref, h):
    hs = slice(MEM_HEAD_DIM * h, MEM_HEAD_DIM * (h + 1))
    qh = q_ref[:, hs].astype(BF16)
    kh = kv_ref[:, hs]
    vh = kv_ref[:, D_MEM + MEM_HEAD_DIM * h:D_MEM + MEM_HEAD_DIM * (h + 1)]
    s = _dot(qh, kh, NT) * (MEM_HEAD_DIM ** -0.5)
    p = jnp.exp(s - jnp.max(s, axis=-1, keepdims=True))
    pn = p / jnp.sum(p, axis=-1, keepdims=True)
    return qh, kh, vh, pn


def _mem_fwd(proj, kv, tm, name):
    T = proj.shape[0]
    M = kv.shape[0]
    qcol, zcol = _OFF["q_mem"] // D_MEM, _OFF["z_mem"] // D_MEM

    def body(q_ref, z_ref, kv_ref, o_ref):
        outs = []
        for h in range(MEM_HEADS):
            _, _, vh, pn = _mem_probs(q_ref, kv_ref, h)
            outs.append(_dot(pn.astype(BF16), vh))
        z = z_ref[...]
        o_ref[...] = (jnp.concatenate(outs, axis=1) * (z * _sigmoid(z))).astype(BF16)

    return pl.pallas_call(
        body, name=name, grid=(T // tm,),
        in_specs=[pl.BlockSpec((tm, D_MEM), lambda i: (i, qcol)), pl.BlockSpec((tm, D_MEM), lambda i: (i, zcol)),
                  _full((M, 2 * D_MEM))],
        out_specs=pl.BlockSpec((tm, D_MEM), lambda i: (i, 0)),
        out_shape=jax.ShapeDtypeStruct((T, D_MEM), BF16),
        compiler_params=_params(("parallel",)),
    )(proj, proj, kv)


def _mem_bwd(d, proj, kv, tm, name):
    T = proj.shape[0]
    M = kv.shape[0]
    qcol, zcol = _OFF["q_mem"] // D_MEM, _OFF["z_mem"] // D_MEM

    def body(d_ref, q_ref, z_ref, kv_ref, dq_ref, dz_ref, dkv_ref):
        @pl.when(pl.program_id(0) == 0)
        def _():
            dkv_ref[...] = jnp.zeros_like(dkv_ref)

        z = z_ref[...]
        sz = _sigmoid(z)
        dv = d_ref[...]
        dov = dv * (z * sz)
        scale = MEM_HEAD_DIM ** -0.5
        outs, dqs = [], []
        for h in range(MEM_HEADS):
            hs = slice(MEM_HEAD_DIM * h, MEM_HEAD_DIM * (h + 1))
            qh, kh, vh, pn = _mem_probs(q_ref, kv_ref, h)
            pnb = pn.astype(BF16)
            oh = _dot(pnb, vh)
            outs.append(oh)
            doh = dov[:, hs]
            dohb = doh.astype(BF16)
            dp = _dot(dohb, vh, NT)
            ds = pn * (dp - jnp.sum(doh * oh, axis=-1, keepdims=True))
            dsb = ds.astype(BF16)
            dqs.append(_dot(dsb, kh) * scale)
            dkv_ref[:, hs] += _dot(dsb, qh, TN) * scale
            vs = slice(D_MEM + MEM_HEAD_DIM * h, D_MEM + MEM_HEAD_DIM * (h + 1))
            dkv_ref[:, vs] += _dot(pnb, dohb, TN)
        dq_ref[...] = jnp.concatenate(dqs, axis=1).astype(BF16)
        dz_ref[...] = (dv * jnp.concatenate(outs, axis=1) * (sz * (1.0 + z * (1.0 - sz)))).astype(BF16)

    row = lambda i: (i, 0)
    return pl.pallas_call(
        body, name=name, grid=(T // tm,),
        in_specs=[pl.BlockSpec((tm, D_MEM), row), pl.BlockSpec((tm, D_MEM), lambda i: (i, qcol)),
                  pl.BlockSpec((tm, D_MEM), lambda i: (i, zcol)), _full((M, 2 * D_MEM))],
        out_specs=[pl.BlockSpec((tm, D_MEM), row), pl.BlockSpec((tm, D_MEM), row), _full((M, 2 * D_MEM))],
        out_shape=[jax.ShapeDtypeStruct((T, D_MEM), BF16), jax.ShapeDtypeStruct((T, D_MEM), BF16),
                   jax.ShapeDtypeStruct((M, 2 * D_MEM), F32)],
        compiler_params=_params(("arbitrary",)),
    )(d, proj, proj, kv)


def _branches_and_gates(os_ref, oa_ref, om_ref, gl_refs, bg_ref, ws_ref, wa_ref, wm_ref):
    outs = (_dot(os_ref[...], ws_ref[...]), _dot(oa_ref[...], wa_ref[...]), _dot(om_ref[...], wm_ref[...]))
    gates = tuple(_sigmoid(jnp.concatenate([gl_refs[2 * k][...], gl_refs[2 * k + 1][...]], axis=1)
                           + bg_ref[:, D_MODEL * k:D_MODEL * (k + 1)]) for k in range(3))
    return outs, gates


def _merge_specs(tm):
    row = lambda i: (i, 0)
    first = _OFF["gates"] // GATE_BLOCK
    gate = [pl.BlockSpec((tm, GATE_BLOCK), (lambda i, k=k: (i, first + k))) for k in range(N_GATES // GATE_BLOCK)]
    return ([pl.BlockSpec((tm, D_SSM), row), pl.BlockSpec((tm, D_ATTN), row), pl.BlockSpec((tm, D_MEM), row)] + gate
            + [_full((1, N_GATES)), _full((D_SSM, D_MODEL)), _full((D_ATTN, D_MODEL)), _full((D_MEM, D_MODEL)),
               _full((D_MODEL, D_MODEL))])


def _merge_fwd(x, o_ssm, o_attn, o_mem, proj, bg, ws, wa, wm, wo, tm, name):
    T = x.shape[0]

    def body(os_ref, oa_ref, om_ref, g0, g1, g2, g3, g4, g5, bg_ref, ws_ref, wa_ref, wm_ref, wo_ref, x_ref,
             xo_ref, mg_ref):
        outs, gates = _branches_and_gates(os_ref, oa_ref, om_ref, (g0, g1, g2, g3, g4, g5), bg_ref, ws_ref, wa_ref,
                                          wm_ref)
        merged = (gates[0] * outs[0] + gates[1] * outs[1] + gates[2] * outs[2]).astype(BF16)
        mg_ref[...] = merged
        xo_ref[...] = x_ref[...] + _dot(merged, wo_ref[...])

    row = lambda i: (i, 0)
    return pl.pallas_call(
        body, name=name, grid=(T // tm,),
        in_specs=_merge_specs(tm) + [pl.BlockSpec((tm, D_MODEL), row)],
        out_specs=[pl.BlockSpec((tm, D_MODEL), row), pl.BlockSpec((tm, D_MODEL), row)],
        out_shape=[jax.ShapeDtypeStruct((T, D_MODEL), F32), jax.ShapeDtypeStruct((T, D_MODEL), BF16)],
        compiler_params=_params(("parallel",)),
    )(o_ssm, o_attn, o_mem, *([proj] * (N_GATES // GATE_BLOCK)), bg, ws, wa, wm, wo, x)


def _merge_bwd(dx, o_ssm, o_attn, o_mem, proj, bg, ws, wa, wm, wo, tm, name, job=None):
    T = dx.shape[0]

    n = T // tm

    def body(os_ref, oa_ref, om_ref, g0, g1, g2, g3, g4, g5, bg_ref, ws_ref, wa_ref, wm_ref, wo_ref, dx_ref,
             dproj_ref, db_ref, dos_ref, doa_ref, dom_ref, dbg_ref, dgl_buf, dgl_sems):
        i = pl.program_id(0)
        slot = i % 2

        def to_dproj(s, row0):
            return pltpu.make_async_copy(dgl_buf.at[s], dproj_ref.at[pl.ds(row0, tm), pl.ds(_OFF["gates"], N_GATES)],
                                         dgl_sems.at[s])

        @pl.when(i == 0)
        def _():
            dbg_ref[...] = jnp.zeros_like(dbg_ref)

        @pl.when(i >= 2)
        def _():
            to_dproj(slot, 0).wait()

        outs, gates = _branches_and_gates(os_ref, oa_ref, om_ref, (g0, g1, g2, g3, g4, g5), bg_ref, ws_ref, wa_ref,
                                          wm_ref)
        dm = _dot(dx_ref[...].astype(BF16), wo_ref[...], NT)
        w_refs = (ws_ref, wa_ref, wm_ref)
        do_refs = (dos_ref, doa_ref, dom_ref)
        for k in range(3):
            cols = slice(D_MODEL * k, D_MODEL * (k + 1))
            dgl = dm * outs[k] * (gates[k] * (1.0 - gates[k]))
            dgl_buf[slot, :, cols] = dgl.astype(BF16)
            dbg_ref[:, cols] += jnp.sum(dgl, axis=0, keepdims=True)
            dbk = (dm * gates[k]).astype(BF16)
            db_ref[:, cols] = dbk
            do_refs[k][...] = _dot(dbk, w_refs[k][...], NT)
        to_dproj(slot, pl.multiple_of(i * tm, tm)).start()

        @pl.when(i == n - 1)
        def _():
            for s in range(min(2, n)):
                to_dproj(s, 0).wait()

    row = lambda i: (i, 0)
    return _pc(
        body, job, name=name, grid=(n,),
        in_specs=_merge_specs(tm) + [pl.BlockSpec((tm, D_MODEL), row)],
        out_specs=[pl.BlockSpec(memory_space=pl.ANY), pl.BlockSpec((tm, N_GATES), row), pl.BlockSpec((tm, D_SSM), row),
                   pl.BlockSpec((tm, D_ATTN), row), pl.BlockSpec((tm, D_MEM), row), _full((1, N_GATES))],
        out_shape=[jax.ShapeDtypeStruct((T, D_IN), BF16), jax.ShapeDtypeStruct((T, N_GATES), BF16),
                   jax.ShapeDtypeStruct((T, D_SSM), F32), jax.ShapeDtypeStruct((T, D_ATTN), F32),
                   jax.ShapeDtypeStruct((T, D_MEM), F32), jax.ShapeDtypeStruct((1, N_GATES), F32)],
        scratch_shapes=[pltpu.VMEM((2, tm, N_GATES), BF16), pltpu.SemaphoreType.DMA((2,))], sem=("arbitrary",),
        operands=(o_ssm, o_attn, o_mem, *([proj] * (N_GATES // GATE_BLOCK)), bg, ws, wa, wm, wo, dx))


def _loss_head(x, g, target, tm, name):
    T, D = x.shape

    def body(x_ref, g_ref, t_ref, loss_ref, dx_ref, dg_ref):
        @pl.when(pl.program_id(0) == 0)
        def _():
            loss_ref[...] = jnp.zeros_like(loss_ref)
            dg_ref[...] = jnp.zeros_like(dg_ref)

        xv = x_ref[...]
        r = lax.rsqrt(jnp.mean(xv * xv, axis=-1, keepdims=True) + EPS)
        xr = xv * r
        err = xr * g_ref[...] - t_ref[...]
        loss_ref[...] += 0.5 * jnp.sum(jnp.mean(err * err, axis=-1, keepdims=True), axis=0, keepdims=True)
        dy = err * (1.0 / D)
        dg_ref[...] += jnp.sum(dy * xr, axis=0, keepdims=True)
        wv = dy * g_ref[...]
        dx_ref[...] = r * (wv - xr * jnp.mean(wv * xr, axis=-1, keepdims=True))

    row = lambda i: (i, 0)
    return pl.pallas_call(
        body, name=name, grid=(T // tm,),
        in_specs=[pl.BlockSpec((tm, D), row), _full((1, D)), pl.BlockSpec((tm, D), row)],
        out_specs=[_full((1, 128)), pl.BlockSpec((tm, D), row), _full((1, D))],
        out_shape=[jax.ShapeDtypeStruct((1, 128), F32), jax.ShapeDtypeStruct((T, D), F32),
                   jax.ShapeDtypeStruct((1, D), F32)],
        compiler_params=_params(("arbitrary",)),
    )(x, g, target)


def _adamw(parts, w, m, v, tr, name):
    L, R, C = w.shape

    def body(p_ref, w_ref, m_ref, v_ref, g_ref, d_ref, mo_ref, vo_ref):
        g = p_ref[0].astype(F32)
        for s in range(1, N_DEV):
            g = g + p_ref[s].astype(F32)
        mn = ADAM_B1 * m_ref[...] + (1.0 - ADAM_B1) * g
        vn = ADAM_B2 * v_ref[...] + (1.0 - ADAM_B2) * (g * g)
        m_hat = mn / (1.0 - ADAM_B1 ** ADAM_STEP)
        v_hat = vn / (1.0 - ADAM_B2 ** ADAM_STEP)
        g_ref[...] = g
        d_ref[...] = -ADAM_LR * (m_hat / (jnp.sqrt(v_hat) + ADAM_EPS) + ADAM_WD * w_ref[...])
        mo_ref[...] = mn
        vo_ref[...] = vn

    one = pl.BlockSpec((None, tr, C), lambda l, i: (l, i, 0))
    return pl.pallas_call(
        body, name=name, grid=(L, R // tr),
        in_specs=[pl.BlockSpec((N_DEV, None, tr, C), lambda l, i: (0, l, i, 0)), one, one, one],
        out_specs=[one] * 4,
        out_shape=[jax.ShapeDtypeStruct((L, R, C), F32)] * 4,
        compiler_params=_params(("parallel", "parallel")),
    )(parts, w, m, v)


_SHARDED = (("w_in", (1088, 1024), 1), ("w_glu", (96, 768), 0), ("w_mem_kv", (128, 1024), 0),
            ("w_br_ssm", (768, 128), 1), ("w_br_attn", (768, 128), 1), ("w_br_mem", (512, 128), 1),
            ("w_out", (128, 1024), 0))
_W_IN = 0
_SMALL = tuple(range(1, len(_SHARDED)))


class _Job(NamedTuple):
    ins: list
    out_shape: list
    aliases: dict
    pairs: Callable
    n: int


def _peers():
    x, y, c = lax.axis_index("x"), lax.axis_index("y"), lax.axis_index("c")
    me = 4 * x + 2 * y + c
    out = []
    for k in range(1, N_DEV):
        px = 1 - x if k & 4 else x
        py = 1 - y if k & 2 else y
        pc = 1 - c if k & 1 else c
        out.append(((px, py, pc), 4 * px + 2 * py + pc))
    return me, out


def _copies(pairs, send_sems, recv_sems, local_sems, arrivals):
    me, peers = _peers()
    local = [pltpu.make_async_copy(src(me), dst(me), local_sems.at[j]) for j, (src, dst) in enumerate(pairs)]
    sends, recvs = [], []
    for k, (peer, lin) in enumerate(peers):
        for j, (src, dst) in enumerate(pairs):
            for to, out in ((dst(me), sends), (dst(lin), recvs)):
                if out is sends or arrivals:
                    out.append(pltpu.make_async_remote_copy(
                        src_ref=src(lin), dst_ref=to, send_sem=send_sems.at[j, k], recv_sem=recv_sems.at[j, k],
                        device_id=peer, device_id_type=pl.DeviceIdType.MESH))
    return local, sends, recvs


def _start_copies(pairs, *sems):
    local, sends, _ = _copies(pairs, *sems, arrivals=False)
    for cp in local + sends:
        cp.start()


def _wait_copies(pairs, *sems):
    local, sends, recvs = _copies(pairs, *sems, arrivals=True)
    for cp in recvs:
        cp.wait_recv()
    for cp in sends:
        cp.wait_send()
    for cp in local:
        cp.wait()


def _job_scratch(job):
    return [pltpu.SemaphoreType.DMA((job.n, N_DEV - 1)), pltpu.SemaphoreType.DMA((job.n, N_DEV - 1)),
            pltpu.SemaphoreType.DMA((job.n,))]


def _pc(body, job, *, name, grid, in_specs, out_specs, out_shape, scratch_shapes, sem, operands, aliases=None):
    aliases = aliases or {}
    if job is None:
        return pl.pallas_call(body, name=name, grid=grid, in_specs=in_specs, out_specs=out_specs, out_shape=out_shape,
                              scratch_shapes=scratch_shapes, input_output_aliases=aliases,
                              compiler_params=_params(sem))(*operands)
    a = len(in_specs)
    b = a + len(job.ins)
    c = b + len(out_shape)
    d = c + len(job.out_shape)
    e = d + len(scratch_shapes)

    def carried(*refs):
        pairs = job.pairs(refs[a:b], refs[c:d])
        ids = [pl.program_id(k) for k in range(len(grid))]
        first = functools.reduce(jnp.logical_and, [i == 0 for i in ids])
        last = functools.reduce(jnp.logical_and, [i == n - 1 for i, n in zip(ids, grid)])

        @pl.when(first)
        def _():
            _start_copies(pairs, *refs[e:])

        body(*refs[:a], *refs[b:c], *refs[d:e])

        @pl.when(last)
        def _():
            _wait_copies(pairs, *refs[e:])

    hbm = pl.BlockSpec(memory_space=pl.ANY)
    outs = pl.pallas_call(
        carried, name=name, grid=grid,
        in_specs=list(in_specs) + [hbm] * len(job.ins), out_specs=list(out_specs) + [hbm] * len(job.out_shape),
        out_shape=list(out_shape) + list(job.out_shape),
        input_output_aliases={**aliases, **{a + i: len(out_shape) + o for i, o in job.aliases.items()}},
        scratch_shapes=list(scratch_shapes) + _job_scratch(job),
        compiler_params=_params(("arbitrary",) * len(grid)),
    )(*operands, *job.ins)
    return outs[:len(out_shape)], outs[len(out_shape):]


def _gather_via_sibling(x, take, place, out_shape, name, landing=None):
    def body(*refs):
        x_ref, o_ref = refs[0], refs[-4]
        send_sems, recv_sems, local_sem = refs[-3:]
        x, y, c = lax.axis_index("x"), lax.axis_index("y"), lax.axis_index("c")
        me, sibling = (x, y, c), (x, y, 1 - c)
        chips = [(1 - x, y), (x, 1 - y), (1 - x, 1 - y)]
        src = take(x_ref)

        def slot(px, py, pc):
            return place(o_ref, 4 * px + 2 * py + pc)

        def copy(k, block, to, first_hand):
            return pltpu.make_async_remote_copy(
                src_ref=src if first_hand else slot(*block), dst_ref=slot(*block), send_sem=send_sems.at[k],
                recv_sem=recv_sems.at[k], device_id=to, device_id_type=pl.DeviceIdType.MESH)

        mine = pltpu.make_async_copy(src, slot(*me), local_sem)
        mine.start()
        first = [copy(0, me, sibling, True)] + [copy(1 + j, me, (*chip, c), True) for j, chip in enumerate(chips)]
        for cp in first:
            cp.start()
        passed = []
        for j, chip in enumerate(chips):
            copy(1 + j, (*chip, c), me, True).wait_recv()
            passed.append(copy(4 + j, (*chip, c), sibling, False))
            passed[-1].start()
        copy(0, sibling, me, True).wait_recv()
        for j, chip in enumerate(chips):
            copy(4 + j, (*chip, 1 - c), me, False).wait_recv()
        for cp in first + passed:
            cp.wait_send()
        mine.wait()

    hbm = pl.BlockSpec(memory_space=pl.ANY)
    ins = [x] if landing is None else [x, landing]
    return pl.pallas_call(
        body, name=name, in_specs=[hbm] * len(ins), out_specs=hbm, out_shape=out_shape,
        input_output_aliases={} if landing is None else {1: 0},
        scratch_shapes=[pltpu.SemaphoreType.DMA((N_DEV - 1,)), pltpu.SemaphoreType.DMA((N_DEV - 1,)),
                        pltpu.SemaphoreType.DMA],
    )(*ins)


def _lane_window(ref, who):
    return ref.at[:, pl.ds(pl.multiple_of(who * LANES, LANES), LANES)]


def _gather_job(shards, items):
    out_shape = []
    for i, _ in items:
        _, s, axis = _SHARDED[i]
        whole = i != _W_IN and axis == 1
        out_shape.append(jax.ShapeDtypeStruct((s[0], N_DEV * s[1]) if whole else (N_DEV,) + s, BF16))

    def pairs(in_refs, out_refs):
        out = []
        for (i, l), src, dst in zip(items, in_refs, out_refs):
            if i != _W_IN and _SHARDED[i][2] == 1:
                out.append((lambda who, src=src, l=l: src.at[l], lambda who, dst=dst: _lane_window(dst, who)))
            else:
                out.append((lambda who, src=src, l=l: src.at[l], lambda who, dst=dst: dst.at[who]))
        return out

    return _Job([shards[i] for i, _ in items], out_shape, {}, pairs, len(items))


def _landed_weights(items, landed):
    out = {}
    for (i, _), a in zip(items, landed):
        n, s, axis = _SHARDED[i]
        if i == _W_IN:
            out[n] = a.reshape(D_IN, D_MODEL)
        elif axis == 0:
            out[n] = a.reshape(N_DEV * s[0], s[1])
        else:
            out[n] = a
    return out


def _scatter_job(grads, items, layer, parts=None):
    ng = len(grads)
    out_shape = [jax.ShapeDtypeStruct((N_DEV, DEPTH) + _SHARDED[i][1], BF16) for i in items]

    def pairs(in_refs, out_refs):
        out = []
        for i, src, dst in zip(items, in_refs[:ng], out_refs):
            _, s, axis = _SHARDED[i]
            if i == _W_IN:
                take = lambda who, src=src: src.at[who]
            elif axis == 0:
                take = lambda who, src=src, s=s: src.at[pl.ds(pl.multiple_of(who * s[0], 16), s[0])]
            else:
                take = lambda who, src=src: _lane_window(src, who)
            out.append((take, lambda who, dst=dst: dst.at[who, layer]))
        return out

    aliases = {} if parts is None else {ng + j: j for j in range(len(items))}
    return _Job(list(grads) + ([] if parts is None else list(parts)), out_shape, aliases, pairs, len(items))


def _rows_job(src, row0, landing=None):
    n = src.shape[0]
    pairs = lambda in_refs, out_refs: [(lambda who: in_refs[0], lambda who: out_refs[0].at[who, pl.ds(row0, n)])]
    return _Job([src] + ([] if landing is None else [landing]), [jax.ShapeDtypeStruct((N_DEV, _REP_ROWS, LANES), F32)],
                {} if landing is None else {1: 0}, pairs, 1)


_REPLICATED = (("norm_g", (2, 1024)), ("mem_norm_g", (2, 1024)), ("b_gate", (2, 3072)),
               ("ssm_lambda_re", (2, 48, 64)), ("ssm_lambda_im", (2, 48, 64)), ("ssm_log_dt", (2, 48)),
               ("ssm_b_re", (2, 48, 64, 16)), ("ssm_b_im", (2, 48, 64, 16)), ("ssm_c_re", (2, 48, 16, 64)),
               ("ssm_c_im", (2, 48, 16, 64)), ("ssm_d", (2, 768)), ("b_glu", (2, 768)), ("rel_bias", (32, 12)),
               ("final_norm_g", (1024,)))
_PER_LAYER = tuple((n, s[1:]) for n, s in _REPLICATED if s[0] == DEPTH and len(s) > 1)
_SHARED = tuple((n, s) for n, s in _REPLICATED if (n, s[1:]) not in _PER_LAYER)
_REP_HALF_ROWS = 1664
_REP_ROWS = 2 * _REP_HALF_ROWS
assert sum(int(np.prod(s)) for _, s in _PER_LAYER + _SHARED) <= _REP_HALF_ROWS * LANES


def _pack_half(tree, layer, shared):
    flat = [tree[n][layer].reshape(-1) for n, _ in _PER_LAYER]
    if shared:
        flat += [tree[n].reshape(-1) for n, _ in _SHARED]
    flat = jnp.concatenate(flat)
    return jnp.pad(flat, (0, _REP_HALF_ROWS * LANES - flat.shape[0])).reshape(_REP_HALF_ROWS, LANES)


def _pack_replicated(tree):
    return jnp.concatenate([_pack_half(tree, 1, False), _pack_half(tree, 0, True)])[None]


def _unpack_replicated(packed):
    halves = packed.reshape(2, -1)
    out, r = {}, 0
    for n, s in _PER_LAYER:
        size = int(np.prod(s))
        out[n] = jnp.stack([halves[1, r:r + size].reshape(s), halves[0, r:r + size].reshape(s)])
        r += size
    for n, s in _SHARED:
        size = int(np.prod(s))
        out[n] = halves[1, r:r + size].reshape(s)
        r += size
    return out


def _discretize(lam_re, lam_im, log_dt, b_re, b_im):
    dt = jnp.exp(log_dt)[:, None]
    mag = jnp.exp(lam_re * dt)
    abar_re, abar_im = mag * jnp.cos(lam_im * dt), mag * jnp.sin(lam_im * dt)
    den = lam_re * lam_re + lam_im * lam_im
    nr, ni = abar_re - 1.0, abar_im
    f_re = (nr * lam_re + ni * lam_im) / den
    f_im = (ni * lam_re - nr * lam_im) / den
    bbar_re = f_re[..., None] * b_re - f_im[..., None] * b_im
    bbar_im = f_re[..., None] * b_im + f_im[..., None] * b_re
    return abar_re, abar_im, bbar_re, bbar_im


def _block_diag(a):
    _, R, C = a.shape
    a = a.reshape(SSM_BLOCKS, 8, R, C)
    eye = jnp.eye(8, dtype=a.dtype)
    return (a[:, :, :, None, :] * eye[None, :, None, :, None]).reshape(SSM_BLOCKS, 8 * R, 8 * C)


def _diag_blocks(a, R, C):
    a = a.reshape(SSM_BLOCKS, 8, R, 8, C)
    eye = jnp.eye(8, dtype=a.dtype)
    return jnp.sum(a * eye[None, :, None, :, None], axis=3).reshape(SSM_GROUPS, R, C)


def _carried(result, job):
    return (result, None) if job is None else result


def _layer_fwd(x, mem, W, P, bias, layer, jobs):
    tag = f"l{layer}"
    abar_re, abar_im, bbar_re, bbar_im = _discretize(P["ssm_lambda_re"][layer], P["ssm_lambda_im"][layer],
                                                     P["ssm_log_dt"][layer], P["ssm_b_re"][layer], P["ssm_b_im"][layer])
    c_re, c_im = P["ssm_c_re"][layer], P["ssm_c_im"][layer]
    ssm = dict(
        are=abar_re.reshape(1, N_STATE), aim=abar_im.reshape(1, N_STATE),
        bre=_block_diag(bbar_re.transpose(0, 2, 1)).astype(BF16), bim=_block_diag(bbar_im.transpose(0, 2, 1)).astype(BF16),
        cre=_block_diag(c_re.transpose(0, 2, 1)).astype(BF16), cimn=_block_diag(-c_im.transpose(0, 2, 1)).astype(BF16),
        ctre=_block_diag(c_re).astype(BF16), ctimn=_block_diag(-c_im).astype(BF16),
        btre=_block_diag(bbar_re).astype(BF16), btim=_block_diag(bbar_im).astype(BF16),
        d=P["ssm_d"][layer].reshape(1, D_SSM))
    bglu = P["b_glu"][layer].reshape(1, D_SSM)
    bgate = P["b_gate"][layer].reshape(1, N_GATES)
    g = P["norm_g"][layer].reshape(1, D_MODEL)
    gm = P["mem_norm_g"][layer].reshape(1, D_MODEL)
    delivered = {}

    def carry(stage):
        return jobs[stage][0] if stage in jobs else None

    def deliver(stage, landed):
        if landed is not None:
            delivered[stage] = _landed_weights(jobs[stage][1], landed)

    T = x.shape[0]
    (proj, h), landed = _carried(_norm_proj(x, g, W["w_in"], min(T, 1024), 2176, f"{tag}_proj", job=carry("proj"),
                                            w_turned=True), carry("proj"))
    deliver("proj", landed)
    W = {**W, **delivered.get("proj", {})}
    (xr, xi, y, o_ssm), landed = _carried(
        _ssm_fwd(proj, ssm["bre"], ssm["bim"], ssm["cre"], ssm["cimn"], ssm["are"], ssm["aim"], ssm["d"], W["w_glu"],
                 bglu, 512, f"{tag}_ssm", job=carry("ssm")), carry("ssm"))
    deliver("ssm", landed)
    os, lses = [], []
    for grp in range(3):
        stage = f"attn{grp}"
        (o_g, lse_g), landed = _carried(_attn_fwd(proj, bias[grp], grp, f"{tag}_{stage}", job=carry(stage)), carry(stage))
        deliver(stage, landed)
        os.append(o_g)
        lses.append(lse_g)
    o_attn = _attn_mix(os, lses, proj, 512, f"{tag}_attn_mix")
    kvb, hm = _norm_proj(mem, gm, W["w_mem_kv"], mem.shape[0], 1024, f"{tag}_mem_kv", out_dtype=BF16)
    o_mem = _mem_fwd(proj, kvb, 512, f"{tag}_mem")
    x_out, merged = _merge_fwd(x, o_ssm, o_attn, o_mem, proj, bgate, W["w_br_ssm"], W["w_br_attn"], W["w_br_mem"],
                               W["w_out"], 512, f"{tag}_merge")
    res = dict(x=x, mem=mem, proj=proj, h=h, xr=xr, xi=xi, y=y, o_ssm=o_ssm, os=os, lses=lses,
               o_attn=o_attn, kvb=kvb, hm=hm, o_mem=o_mem, merged=merged, ssm=ssm, bglu=bglu,
               bgate=bgate, g=g, gm=gm, W=W)
    return x_out, res, delivered


def _layer_bwd(dx, res, P, bias, layer, jobs):
    tag = f"l{layer}b"
    proj, ssm, W = res["proj"], res["ssm"], res["W"]
    T = dx.shape[0]
    landed = {}

    def run(stage, fn, job):
        out, landed[stage] = _carried(fn(job), job)
        if job is None:
            del landed[stage]
        return out

    dproj, dbr, do_ssm, do_attn, do_mem, dbg = run(
        "merge", lambda job: _merge_bwd(dx, res["o_ssm"], res["o_attn"], res["o_mem"], proj, res["bgate"], W["w_br_ssm"],
                                        W["w_br_attn"], W["w_br_mem"], W["w_out"], 512, f"{tag}_merge", job=job),
        jobs.get("merge"))
    gw = {}
    tk = min(T, 1024)
    gw["w_out"] = _mm_tn(res["merged"], dx, 1024, 1024, tk, f"{tag}_dw_out")
    gw["w_br_ssm"] = _mm_tn(res["o_ssm"], dbr, 768, 1024, tk, f"{tag}_dw_br_ssm", b_col=0, n=1024)
    gw["w_br_attn"] = _mm_tn(res["o_attn"], dbr, 768, 1024, tk, f"{tag}_dw_br_attn", b_col=1024, n=1024)
    gw["w_br_mem"] = _mm_tn(res["o_mem"], dbr, 512, 1024, tk, f"{tag}_dw_br_mem", b_col=2048, n=1024)

    dqm, dzm, dkv = _mem_bwd(do_mem, proj, res["kvb"], 512, f"{tag}_mem")
    M = dkv.shape[0]
    gw["w_mem_kv"] = _mm_tn(res["hm"], dkv, 1024, 1024, M, f"{tag}_dw_mem_kv")
    _, dgm = _proj_bwd(dkv.astype(BF16), W["w_mem_kv"], res["mem"], res["gm"], jnp.zeros_like(res["mem"]), M, 1024,
                       f"{tag}_mem_norm")

    do_g, corr, dproj = _attn_mix_bwd(do_attn, res["os"], res["lses"], proj, dproj, 512, f"{tag}_attn_mix")
    dbs = []
    for grp in range(3):
        dproj, db_g = _attn_bwd(proj, do_g, corr, res["lses"][grp], bias[grp], dproj, grp, f"{tag}_attn{grp}")
        dbs.append(db_g)
    dbias = jnp.stack(dbs)

    dy, dproj, gelu_b, dt_b, dbglu = _glu_bwd(do_ssm, res["y"], proj, W["w_glu"], res["bglu"], dproj, 512, f"{tag}_glu")
    gw["w_glu"] = _mm_tn(gelu_b, dt_b, 768, 768, tk, f"{tag}_dw_glu")
    dproj, dbre, dbim, dcre, dcim, dare, daim, dd = run(
        "ssm", lambda job: _ssm_bwd(dy, proj, res["xr"], res["xi"], ssm["ctre"], ssm["ctimn"], ssm["btre"], ssm["btim"],
                                    ssm["are"], ssm["aim"], ssm["d"], dproj, 256, f"{tag}_ssm", job=job),
        jobs.get("ssm"))
    _, disc_vjp = jax.vjp(_discretize, P["ssm_lambda_re"][layer], P["ssm_lambda_im"][layer], P["ssm_log_dt"][layer],
                          P["ssm_b_re"][layer], P["ssm_b_im"][layer])
    d_lre, d_lim, d_ldt, d_bre, d_bim = disc_vjp((dare.reshape(SSM_GROUPS, SSM_STATE), daim.reshape(SSM_GROUPS, SSM_STATE),
                                                  _diag_blocks(dbre, SSM_STATE, SSM_GROUP),
                                                  _diag_blocks(dbim, SSM_STATE, SSM_GROUP)))

    small = [gw[_SHARDED[i][0]] for i in _SMALL]
    for seg, piece in (("q_mem", dqm), ("z_mem", dzm)):
        dproj = lax.dynamic_update_slice(dproj, piece, (0, _OFF[seg]))
    dw_in = run("dw_in", lambda job: _mm_tn(dproj, res["h"], 2176, 1024, min(T, 1024), f"{tag}_dw_in", job=job),
                jobs["dw_in"](small) if "dw_in" in jobs else None)
    dw_in = dw_in.reshape((N_DEV,) + _SHARDED[_W_IN][1])
    dx_in, dg = run("proj", lambda job: _proj_bwd(dproj, W["w_in"], res["x"], res["g"], dx, min(T, 1024), 2176,
                                                  f"{tag}_proj", job=job, w_turned=True),
                    jobs["proj"](small, dw_in, landed) if "proj" in jobs else None)

    gp = dict(norm_g=dg[0], mem_norm_g=dgm[0], b_gate=dbg[0], ssm_lambda_re=d_lre, ssm_lambda_im=d_lim,
              ssm_log_dt=d_ldt, ssm_b_re=d_bre, ssm_b_im=d_bim,
              ssm_c_re=_diag_blocks(dcre, SSM_GROUP, SSM_STATE), ssm_c_im=_diag_blocks(dcim, SSM_GROUP, SSM_STATE),
              ssm_d=dd[0], b_glu=dbglu[0])
    return dx_in, dw_in, gp, dbias, landed


def _train_step(x, mem, target, shards, P):
    rest0 = [(i, 0) for i in _SMALL]
    thirds1 = [[(i, 1) for i in _SMALL[k::3]] for k in range(3)]
    first = [(_W_IN, 0)]
    w_in0 = _gather_via_sibling(shards[_W_IN], lambda ref: ref.at[0], lambda ref, s: ref.at[s],
                                jax.ShapeDtypeStruct((N_DEV,) + _SHARDED[_W_IN][1], BF16), "gather_w_in0")
    W0 = _landed_weights(first, [w_in0])
    buckets = _bucket_tables()
    bias = _bias_tables(P["rel_bias"], buckets, "bias_tables")
    jobs0 = {"proj": (_gather_job(shards, rest0), rest0), "ssm": (_gather_job(shards, [(_W_IN, 1)]), [(_W_IN, 1)]),
             **{f"attn{k}": (_gather_job(shards, items), items) for k, items in enumerate(thirds1)}}
    x, res0, delivered = _layer_fwd(x, mem, W0, P, bias, 0, jobs0)
    W1 = {**delivered["ssm"], **delivered["attn0"], **delivered["attn1"], **delivered["attn2"]}
    x, res1, _ = _layer_fwd(x, mem, W1, P, bias, 1, {})
    loss, dx, dgf = _loss_head(x, P["final_norm_g"].reshape(1, D_MODEL), target, 512, "loss_head")

    dx, dw_in1, gp1, dbias1, landed1 = _layer_bwd(
        dx, res1, P, bias, 1, {"proj": lambda small, dw_in, landed: _scatter_job(small, _SMALL, 1)})
    rep1 = _pack_half({n: a[None] for n, a in gp1.items()}, 0, False)
    dx, _, gp0, dbias0, landed0 = _layer_bwd(
        dx, res0, P, bias, 0,
        {"merge": _rows_job(rep1, 0), "ssm": _scatter_job([dw_in1], [_W_IN], 1),
         "dw_in": lambda small: _scatter_job(small, _SMALL, 0, parts=landed1["proj"]),
         "proj": lambda small, dw_in, landed: _scatter_job([dw_in], [_W_IN], 0, parts=landed["ssm"])})
    d_rel = _bias_grad(dbias0, dbias1, buckets, "bias_grad")
    gp0 = {n: a[None] for n, a in gp0.items()}
    gp0["rel_bias"] = jnp.sum(d_rel, axis=-1).transpose(2, 0, 1).reshape(NUM_BUCKETS, 12)
    gp0["final_norm_g"] = dgf[0]
    rep0 = _pack_half(gp0, 0, True)
    rparts = _gather_via_sibling(rep0, lambda ref: ref, lambda ref, s: ref.at[s, pl.ds(_REP_HALF_ROWS, _REP_HALF_ROWS)],
                                 jax.ShapeDtypeStruct((N_DEV, _REP_ROWS, LANES), F32), "gather_small_grads0",
                                 landing=landed0["merge"][0])
    return loss[0, 0], dx, list(landed0["proj"]) + list(landed0["dw_in"]), rparts


_WEIGHTS = ["norm_g", "mem_norm_g", "w_in", "b_gate", "ssm_lambda_re", "ssm_lambda_im", "ssm_log_dt", "ssm_b_re",
            "ssm_b_im", "ssm_c_re", "ssm_c_im", "ssm_d", "w_glu", "b_glu", "w_mem_kv", "w_br_ssm", "w_br_attn",
            "w_br_mem", "w_out", "rel_bias", "final_norm_g"]
_ADAM_ROWS = {"w_in": 136,"w_glu": 96, "w_mem_kv": 128, "w_br_ssm": 768, "w_br_attn": 768, "w_br_mem": 512,
              "w_out": 128}


def kernel(x, mem, norm_g, mem_norm_g, w_in, b_gate, ssm_lambda_re, ssm_lambda_im, ssm_log_dt, ssm_b_re, ssm_b_im, ssm_c_re, ssm_c_im, ssm_d, w_glu, b_glu, w_mem_kv, w_br_ssm, w_br_attn, w_br_mem, w_out, rel_bias, final_norm_g, loss_target, m_norm_g, m_mem_norm_g, m_w_in, m_b_gate, m_ssm_lambda_re, m_ssm_lambda_im, m_ssm_log_dt, m_ssm_b_re, m_ssm_b_im, m_ssm_c_re, m_ssm_c_im, m_ssm_d, m_w_glu, m_b_glu, m_w_mem_kv, m_w_br_ssm, m_w_br_attn, m_w_br_mem, m_w_out, m_rel_bias, m_final_norm_g, v_norm_g, v_mem_norm_g, v_w_in, v_b_gate, v_ssm_lambda_re, v_ssm_lambda_im, v_ssm_log_dt, v_ssm_b_re, v_ssm_b_im, v_ssm_c_re, v_ssm_c_im, v_ssm_d, v_w_glu, v_b_glu, v_w_mem_kv, v_w_br_ssm, v_w_br_attn, v_w_br_mem, v_w_out, v_rel_bias, v_final_norm_g):
    given = dict(locals())
    w = {n: given[n] for n in _WEIGHTS}
    m = {n: given["m_" + n] for n in _WEIGHTS}
    v = {n: given["v_" + n] for n in _WEIGHTS}

    turned = lambda n, a: a.swapaxes(1, 2) if n == "w_in" else a
    shards = [turned(n, w[n]).astype(BF16) for n, _, _ in _SHARDED]
    loss, dx, parts, rparts = _train_step(x[0], mem[0], loss_target[0], shards, w)
    loss = lax.psum(loss, ("x", "y", "c"))

    new = {}
    for (n, _, _), p in zip(_SHARDED, parts):
        new[n] = [turned(n, a) for a in _adamw(p, turned(n, w[n]), turned(n, m[n]), turned(n, v[n]), _ADAM_ROWS[n],
                                               f"adamw_{n}")]
    rp = [_unpack_replicated(a) for a in _adamw(rparts[:, None], _pack_replicated(w), _pack_replicated(m),
                                                _pack_replicated(v), _REP_ROWS // 4, "adamw_replicated")]
    for n, _ in _REPLICATED:
        new[n] = [rp[kind][n] for kind in range(4)]
    outs = [loss, dx[None]]
    for kind in range(4):
        outs.extend(new[n][kind] for n in _WEIGHTS)
    return tuple(outs)
```

```python
import functools
import math
from typing import Callable, NamedTuple

import jax
import jax.numpy as jnp
import numpy as np
from jax import lax
from jax.experimental import pallas as pl
from jax.experimental.pallas import tpu as pltpu

F32 = jnp.float32
BF16 = jnp.bfloat16

D_MODEL = 1024
DEPTH = 2
EPS = 1e-6
D_SSM = 768
SSM_GROUP = 16
SSM_GROUPS = 48
SSM_STATE = 64
N_STATE = SSM_GROUPS * SSM_STATE
SSM_BLOCKS = 6
D_ATTN = 768
ATTN_HEAD_DIM = 64
ATTN_GROUP_WIDTH = 256
ATTN_DILATIONS = (1, 4, 16)
ATTN_SPAN = 128
ATTN_BLOCK = 128
NUM_BUCKETS = 32
REL_MAX_DISTANCE = 2048
NEG_INF = -1e30
MEM_HEADS = 4
MEM_HEAD_DIM = 128
D_MEM = 512
N_GATES = 3 * D_MODEL
D_IN = 8704
N_DEV = 8
LANES = 128
ADAM_LR = 0.001
ADAM_B1 = 0.9
ADAM_B2 = 0.999
ADAM_EPS = 1e-08
ADAM_WD = 0.01
ADAM_STEP = 10

_OFF = {"u": 0, "z_ssm": 768, "q": 1536, "k": 2304, "v": 3072, "z_attn": 3840, "q_mem": 4608, "z_mem": 5120,
        "gates": 5632}
GATE_BLOCK = 512
ROW_TILE = 1024

NN = (((1,), (0,)), ((), ()))
NT = (((1,), (1,)), ((), ()))
TN = (((0,), (0,)), ((), ()))

VMEM_LIMIT = 56 * 1024 * 1024


def _dot(a, b, dims=NN):
    return lax.dot_general(a, b, dims, preferred_element_type=F32)


def _sigmoid(x):
    return 1.0 / (1.0 + jnp.exp(-x))


def _gelu_parts(x):
    k = math.sqrt(2.0 / math.pi)
    t = jnp.tanh(k * (x + 0.044715 * (x * x * x)))
    cdf = 0.5 * (1.0 + t)
    dcdf = 0.5 * (1.0 - t * t) * k * (1.0 + 3.0 * 0.044715 * (x * x))
    return x * cdf, cdf + x * dcdf


def _params(sem, vmem=VMEM_LIMIT):
    return pltpu.CompilerParams(dimension_semantics=sem, vmem_limit_bytes=vmem)


def _full(shape):
    return pl.BlockSpec(shape, lambda *_: (0,) * len(shape))


def _norm_proj(x, g, w, tm, tn, name, out_dtype=F32, job=None, w_turned=False):
    T, D = x.shape
    N = w.shape[0] if w_turned else w.shape[1]
    w_spec = pl.BlockSpec((tn, D), lambda i, j: (j, 0)) if w_turned else pl.BlockSpec((D, tn), lambda i, j: (0, j))
    dims = NT if w_turned else NN

    def body(x_ref, g_ref, w_ref, o_ref, h_ref, hs):
        @pl.when(pl.program_id(1) == 0)
        def _():
            xv = x_ref[...]
            r = lax.rsqrt(jnp.mean(xv * xv, axis=-1, keepdims=True) + EPS)
            hv = (xv * r * g_ref[...]).astype(BF16)
            hs[...] = hv
            h_ref[...] = hv

        o_ref[...] = _dot(hs[...], w_ref[...], dims).astype(out_dtype)

    return _pc(
        body, job, name=name, grid=(T // tm, N // tn),
        in_specs=[pl.BlockSpec((tm, D), lambda i, j: (i, 0)), _full((1, D)), w_spec],
        out_specs=[pl.BlockSpec((tm, tn), lambda i, j: (i, j)), pl.BlockSpec((tm, D), lambda i, j: (i, 0))],
        out_shape=[jax.ShapeDtypeStruct((T, N), out_dtype), jax.ShapeDtypeStruct((T, D), BF16)],
        scratch_shapes=[pltpu.VMEM((tm, D), BF16)], sem=("parallel", "arbitrary"), operands=(x, g, w))


def _mm_tn(a, b, tm, tn, tk, name, b_col=0, n=None, job=None):
    K, M = a.shape
    N = b.shape[1] if n is None else n
    nk = K // tk
    j0 = b_col // tn

    def body(a_ref, b_ref, o_ref, acc):
        k = pl.program_id(2)

        @pl.when(k == 0)
        def _():
            acc[...] = jnp.zeros_like(acc)

        acc[...] += _dot(a_ref[...].astype(BF16), b_ref[...].astype(BF16), TN)

        @pl.when(k == nk - 1)
        def _():
            o_ref[...] = acc[...].astype(BF16)

    out = _pc(
        body, job, name=name, grid=(M // tm, N // tn, nk),
        in_specs=[pl.BlockSpec((tk, tm), lambda i, j, k: (k, i)), pl.BlockSpec((tk, tn), lambda i, j, k: (k, j0 + j))],
        out_specs=[pl.BlockSpec((tm, tn), lambda i, j, k: (i, j))],
        out_shape=[jax.ShapeDtypeStruct((M, N), BF16)],
        scratch_shapes=[pltpu.VMEM((tm, tn), F32)], sem=("parallel", "parallel", "arbitrary"), operands=(a, b))
    return out[0] if job is None else (out[0][0], out[1])


def _proj_bwd(dp, w, x, g, dres, tm, tk, name, job=None, w_turned=False):
    T, N = dp.shape
    D = x.shape[1]
    nk = N // tk
    w_spec = pl.BlockSpec((tk, D), lambda i, k: (k, 0)) if w_turned else pl.BlockSpec((D, tk), lambda i, k: (0, k))
    dims = NN if w_turned else NT

    def body(dp_ref, w_ref, x_ref, g_ref, dres_ref, dx_ref, dg_ref, acc):
        i, k = pl.program_id(0), pl.program_id(1)

        @pl.when(k == 0)
        def _():
            acc[...] = jnp.zeros_like(acc)

        @pl.when((i == 0) & (k == 0))
        def _():
            dg_ref[...] = jnp.zeros_like(dg_ref)

        acc[...] += _dot(dp_ref[...], w_ref[...], dims)

        @pl.when(k == nk - 1)
        def _():
            xv = x_ref[...]
            dh = acc[...]
            r = lax.rsqrt(jnp.mean(xv * xv, axis=-1, keepdims=True) + EPS)
            xr = xv * r
            dg_ref[...] += jnp.sum(dh * xr, axis=0, keepdims=True)
            wv = dh * g_ref[...]
            dx_ref[...] = dres_ref[...] + r * (wv - xr * jnp.mean(wv * xr, axis=-1, keepdims=True))

    return _pc(
        body, job, name=name, grid=(T // tm, nk),
        in_specs=[pl.BlockSpec((tm, tk), lambda i, k: (i, k)), w_spec,
                  pl.BlockSpec((tm, D), lambda i, k: (i, 0)), _full((1, D)),
                  pl.BlockSpec((tm, D), lambda i, k: (i, 0))],
        out_specs=[pl.BlockSpec((tm, D), lambda i, k: (i, 0)), _full((1, D))],
        out_shape=[jax.ShapeDtypeStruct((T, D), F32), jax.ShapeDtypeStruct((1, D), F32)],
        scratch_shapes=[pltpu.VMEM((tm, D), F32)], sem=("arbitrary", "arbitrary"), operands=(dp, w, x, g, dres))


def _ssm_fwd(proj, bre, bim, cre, cimn, are, aim, d, wglu, bglu, tc, name, job=None):
    T = proj.shape[0]
    ucol, zcol = _OFF["u"] // D_SSM, _OFF["z_ssm"] // D_SSM

    def body(u_ref, z_ref, bre_ref, bim_ref, cre_ref, cim_ref, are_ref, aim_ref, d_ref, wg_ref, bg_ref,
             xr_ref, xi_ref, y_ref, o_ref, car_r, car_i):
        @pl.when(pl.program_id(0) == 0)
        def _():
            car_r[...] = jnp.zeros_like(car_r)
            car_i[...] = jnp.zeros_like(car_i)

        u = u_ref[...]
        ub = u.astype(BF16)
        for k in range(SSM_BLOCKS):
            uk = ub[:, 128 * k:128 * (k + 1)]
            xr_ref[:, 512 * k:512 * (k + 1)] = _dot(uk, bre_ref[k])
            xi_ref[:, 512 * k:512 * (k + 1)] = _dot(uk, bim_ref[k])
        ar, ai = are_ref[...], aim_ref[...]

        def step(t, c):
            pr, pi = c
            nr = ar * pr - ai * pi + xr_ref[pl.ds(t, 1), :]
            ni = ar * pi + ai * pr + xi_ref[pl.ds(t, 1), :]
            xr_ref[pl.ds(t, 1), :] = nr
            xi_ref[pl.ds(t, 1), :] = ni
            return nr, ni

        pr, pi = lax.fori_loop(0, tc, step, (car_r[...], car_i[...]))
        car_r[...] = pr
        car_i[...] = pi

        ys = []
        for k in range(SSM_BLOCKS):
            xrk = xr_ref[:, 512 * k:512 * (k + 1)].astype(BF16)
            xik = xi_ref[:, 512 * k:512 * (k + 1)].astype(BF16)
            ys.append(_dot(xrk, cre_ref[k]) + _dot(xik, cim_ref[k]))
        y = jnp.concatenate(ys, axis=1) + d_ref[...] * u
        y_ref[...] = y
        gl, _ = _gelu_parts(y)
        t = _dot(gl.astype(BF16), wg_ref[...]) + bg_ref[...]
        z = z_ref[...]
        o_ref[...] = (gl * _sigmoid(t) * (z * _sigmoid(z))).astype(BF16)

    return _pc(
        body, job, name=name, grid=(T // tc,),
        in_specs=[pl.BlockSpec((tc, D_SSM), lambda i: (i, ucol)), pl.BlockSpec((tc, D_SSM), lambda i: (i, zcol)),
                  _full((SSM_BLOCKS, 128, 512)), _full((SSM_BLOCKS, 128, 512)),
                  _full((SSM_BLOCKS, 512, 128)), _full((SSM_BLOCKS, 512, 128)),
                  _full((1, N_STATE)), _full((1, N_STATE)), _full((1, D_SSM)),
                  _full((D_SSM, D_SSM)), _full((1, D_SSM))],
        out_specs=[pl.BlockSpec((tc, N_STATE), lambda i: (i, 0)), pl.BlockSpec((tc, N_STATE), lambda i: (i, 0)),
                   pl.BlockSpec((tc, D_SSM), lambda i: (i, 0)), pl.BlockSpec((tc, D_SSM), lambda i: (i, 0))],
        out_shape=[jax.ShapeDtypeStruct((T, N_STATE), F32), jax.ShapeDtypeStruct((T, N_STATE), F32),
                   jax.ShapeDtypeStruct((T, D_SSM), F32), jax.ShapeDtypeStruct((T, D_SSM), BF16)],
        scratch_shapes=[pltpu.VMEM((1, N_STATE), F32), pltpu.VMEM((1, N_STATE), F32)], sem=("arbitrary",),
        operands=(proj, proj, bre, bim, cre, cimn, are, aim, d, wglu, bglu))


def _glu_bwd(do, y, proj, wglu, bglu, dproj, tm, name):
    T = y.shape[0]
    zcol = _OFF["z_ssm"] // D_SSM

    def body(do_ref, y_ref, z_ref, wg_ref, bg_ref, _, dy_ref, dz_ref, g_ref, dt_ref, db_ref):
        @pl.when(pl.program_id(0) == 0)
        def _():
            db_ref[...] = jnp.zeros_like(db_ref)

        dov = do_ref[...]
        gl, dgl = _gelu_parts(y_ref[...])
        glb = gl.astype(BF16)
        sg = _sigmoid(_dot(glb, wg_ref[...]) + bg_ref[...])
        z = z_ref[...]
        sz = _sigmoid(z)
        dz_ref[...] = (dov * (gl * sg) * (sz * (1.0 + z * (1.0 - sz)))).astype(BF16)
        dy2 = dov * (z * sz)
        dt = dy2 * gl * (sg * (1.0 - sg))
        dtb = dt.astype(BF16)
        dg = dy2 * sg + _dot(dtb, wg_ref[...], NT)
        dy_ref[...] = dg * dgl
        g_ref[...] = glb
        dt_ref[...] = dtb
        db_ref[...] += jnp.sum(dt, axis=0, keepdims=True)

    row = lambda i: (i, 0)
    return pl.pallas_call(
        body, name=name, grid=(T // tm,),
        in_specs=[pl.BlockSpec((tm, D_SSM), row), pl.BlockSpec((tm, D_SSM), row),
                  pl.BlockSpec((tm, D_SSM), lambda i: (i, zcol)), _full((D_SSM, D_SSM)), _full((1, D_SSM)),
                  pl.BlockSpec(memory_space=pl.ANY)],
        out_specs=[pl.BlockSpec((tm, D_SSM), row), pl.BlockSpec((tm, D_SSM), lambda i: (i, zcol)),
                   pl.BlockSpec((tm, D_SSM), row), pl.BlockSpec((tm, D_SSM), row), _full((1, D_SSM))],
        out_shape=[jax.ShapeDtypeStruct((T, D_SSM), F32), jax.ShapeDtypeStruct(dproj.shape, BF16),
                   jax.ShapeDtypeStruct((T, D_SSM), BF16), jax.ShapeDtypeStruct((T, D_SSM), BF16),
                   jax.ShapeDtypeStruct((1, D_SSM), F32)],
        input_output_aliases={5: 1},
        compiler_params=_params(("arbitrary",)),
    )(do, y, proj, wglu, bglu, dproj)


def _ssm_bwd(dy, proj, xr, xi, ctre, ctimn, btre, btim, are, aim, d, dproj, tc, name, job=None):
    T = dy.shape[0]
    nc = T // tc
    ucol = _OFF["u"] // D_SSM
    rb = tc // 8

    def body(dy_ref, u_ref, xr_ref, xi_ref, xpr_ref, xpi_ref, ctre_ref, ctim_ref, btre_ref, btim_ref,
             are_ref, aim_ref, d_ref, _,
             du_ref, dbre_ref, dbim_ref, dcre_ref, dcim_ref, dare_ref, daim_ref, dd_ref, gr, gi, car_r, car_i):
        i = pl.program_id(0)

        @pl.when(i == 0)
        def _():
            for ref in (car_r, car_i, dbre_ref, dbim_ref, dcre_ref, dcim_ref, dare_ref, daim_ref, dd_ref):
                ref[...] = jnp.zeros_like(ref)

        dyv = dy_ref[...]
        dyb = dyv.astype(BF16)
        u = u_ref[...]
        ub = u.astype(BF16)
        for k in range(SSM_BLOCKS):
            dk = dyb[:, 128 * k:128 * (k + 1)]
            gr[:, 512 * k:512 * (k + 1)] = _dot(dk, ctre_ref[k])
            gi[:, 512 * k:512 * (k + 1)] = _dot(dk, ctim_ref[k])
        ar, ai = are_ref[...], aim_ref[...]

        def step(s, c):
            pr, pi = c
            t = tc - 1 - s
            nr = gr[pl.ds(t, 1), :] + ar * pr + ai * pi
            ni = gi[pl.ds(t, 1), :] + ar * pi - ai * pr
            gr[pl.ds(t, 1), :] = nr
            gi[pl.ds(t, 1), :] = ni
            return nr, ni

        pr, pi = lax.fori_loop(0, tc, step, (car_r[...], car_i[...]))
        car_r[...] = pr
        car_i[...] = pi

        keep = jnp.where(i == nc - 1, 0.0, 1.0)
        row0 = lax.broadcasted_iota(jnp.int32, (tc, 1), 0) == 0
        dd_ref[...] += jnp.sum(dyv * u, axis=0, keepdims=True)
        for k in range(SSM_BLOCKS):
            sl = slice(512 * k, 512 * (k + 1))
            ch = slice(128 * k, 128 * (k + 1))
            xrk, xik, grk, gik = xr_ref[:, sl], xi_ref[:, sl], gr[:, sl], gi[:, sl]
            xsr = jnp.where(row0, xpr_ref[7:8, sl] * keep, pltpu.roll(xrk, 1, axis=0))
            xsi = jnp.where(row0, xpi_ref[7:8, sl] * keep, pltpu.roll(xik, 1, axis=0))
            dare_ref[:, sl] += jnp.sum(grk * xsr + gik * xsi, axis=0, keepdims=True)
            daim_ref[:, sl] += jnp.sum(gik * xsr - grk * xsi, axis=0, keepdims=True)
            grb, gib = grk.astype(BF16), gik.astype(BF16)
            du_ref[:, ch] = (_dot(grb, btre_ref[k]) + _dot(gib, btim_ref[k])
                             + d_ref[:, ch] * dyv[:, ch]).astype(BF16)
            dbre_ref[k] += _dot(grb, ub[:, ch], TN)
            dbim_ref[k] += _dot(gib, ub[:, ch], TN)
            dcre_ref[k] += _dot(dyb[:, ch], xrk.astype(BF16), TN)
            dcim_ref[k] -= _dot(dyb[:, ch], xik.astype(BF16), TN)

    rev = lambda i: (nc - 1 - i, 0)
    prev = lambda i: (jnp.maximum((nc - 1 - i) * rb - 1, 0), 0)
    return _pc(
        body, job, name=name, grid=(nc,),
        in_specs=[pl.BlockSpec((tc, D_SSM), rev), pl.BlockSpec((tc, D_SSM), lambda i: (nc - 1 - i, ucol)),
                  pl.BlockSpec((tc, N_STATE), rev), pl.BlockSpec((tc, N_STATE), rev),
                  pl.BlockSpec((8, N_STATE), prev), pl.BlockSpec((8, N_STATE), prev),
                  _full((SSM_BLOCKS, 128, 512)), _full((SSM_BLOCKS, 128, 512)),
                  _full((SSM_BLOCKS, 512, 128)), _full((SSM_BLOCKS, 512, 128)),
                  _full((1, N_STATE)), _full((1, N_STATE)), _full((1, D_SSM)), pl.BlockSpec(memory_space=pl.ANY)],
        out_specs=[pl.BlockSpec((tc, D_SSM), lambda i: (nc - 1 - i, ucol)),
                   _full((SSM_BLOCKS, 512, 128)), _full((SSM_BLOCKS, 512, 128)),
                   _full((SSM_BLOCKS, 128, 512)), _full((SSM_BLOCKS, 128, 512)),
                   _full((1, N_STATE)), _full((1, N_STATE)), _full((1, D_SSM))],
        out_shape=[jax.ShapeDtypeStruct(dproj.shape, BF16),
                   jax.ShapeDtypeStruct((SSM_BLOCKS, 512, 128), F32), jax.ShapeDtypeStruct((SSM_BLOCKS, 512, 128), F32),
                   jax.ShapeDtypeStruct((SSM_BLOCKS, 128, 512), F32), jax.ShapeDtypeStruct((SSM_BLOCKS, 128, 512), F32),
                   jax.ShapeDtypeStruct((1, N_STATE), F32), jax.ShapeDtypeStruct((1, N_STATE), F32),
                   jax.ShapeDtypeStruct((1, D_SSM), F32)],
        scratch_shapes=[pltpu.VMEM((tc, N_STATE), F32), pltpu.VMEM((tc, N_STATE), F32),
                        pltpu.VMEM((1, N_STATE), F32), pltpu.VMEM((1, N_STATE), F32)], sem=("arbitrary",),
        operands=(dy, proj, xr, xi, xr, xi, ctre, ctimn, btre, btim, are, aim, d, dproj), aliases={13: 0})


def _rel_bucket(dist):
    n = jnp.maximum(dist, 0)
    max_exact = NUM_BUCKETS // 2
    n_f = jnp.maximum(n, 1).astype(F32)
    large = max_exact + (jnp.log(n_f / max_exact) / math.log(REL_MAX_DISTANCE / max_exact)
                         * (NUM_BUCKETS - max_exact)).astype(jnp.int32)
    large = jnp.minimum(large, NUM_BUCKETS - 1)
    return jnp.where(n < max_exact, n, large)


def _bucket_tables():
    qi = jnp.arange(ATTN_BLOCK)[:, None]
    kj = jnp.arange(2 * ATTN_BLOCK)[None, :]
    delta = jnp.maximum(ATTN_BLOCK + qi - kj, 0)
    return jnp.stack([_rel_bucket(delta * r) for r in ATTN_DILATIONS]).astype(jnp.int32)


def _bias_tables(rel_bias, buckets, name):
    def body(tab_ref, bk_ref, o_ref):
        g = pl.program_id(0)
        bk = bk_ref[...]
        qi = lax.broadcasted_iota(jnp.int32, bk.shape, 0)
        kj = lax.broadcasted_iota(jnp.int32, bk.shape, 1)
        delta = ATTN_BLOCK + qi - kj
        band = (delta >= 0) & (delta <= ATTN_SPAN)
        accs = [jnp.zeros(bk.shape, F32) for _ in range(4)]
        for b in range(NUM_BUCKETS):
            hit = bk == b
            for h in range(4):
                accs[h] = jnp.where(hit, tab_ref[b, 4 * g + h], accs[h])
        for h in range(4):
            o_ref[h] = jnp.where(band, accs[h], NEG_INF)

    return pl.pallas_call(
        body, name=name, grid=(3,),
        in_specs=[pl.BlockSpec(memory_space=pltpu.SMEM),
                  pl.BlockSpec((None, ATTN_BLOCK, 2 * ATTN_BLOCK), lambda g: (g, 0, 0))],
        out_specs=pl.BlockSpec((None, 4, ATTN_BLOCK, 2 * ATTN_BLOCK), lambda g: (g, 0, 0, 0)),
        out_shape=jax.ShapeDtypeStruct((3, 4, ATTN_BLOCK, 2 * ATTN_BLOCK), F32),
        compiler_params=_params(("parallel",)),
    )(rel_bias, buckets)


def _bias_grad(db0, db1, buckets, name):
    def body(a_ref, b_ref, bk_ref, o_ref):
        bk = bk_ref[...]
        for h in range(4):
            dv = a_ref[h] + b_ref[h]
            for b in range(NUM_BUCKETS):
                o_ref[h, b:b + 1, :] = jnp.sum(jnp.where(bk == b, dv, 0.0), axis=0, keepdims=True)

    tab = pl.BlockSpec((None, 4, ATTN_BLOCK, 2 * ATTN_BLOCK), lambda g: (g, 0, 0, 0))
    return pl.pallas_call(
        body, name=name, grid=(3,),
        in_specs=[tab, tab, pl.BlockSpec((None, ATTN_BLOCK, 2 * ATTN_BLOCK), lambda g: (g, 0, 0))],
        out_specs=pl.BlockSpec((None, 4, NUM_BUCKETS, 2 * ATTN_BLOCK), lambda g: (g, 0, 0, 0)),
        out_shape=jax.ShapeDtypeStruct((3, 4, NUM_BUCKETS, 2 * ATTN_BLOCK), F32),
        compiler_params=_params(("parallel",)),
    )(db0, db1, buckets)


_ATTN_SUB = {1: 4, 4: 1, 16: 1}
_UNROLL = 4


def _unit_rows(j, s, r):
    start = j * ATTN_BLOCK * r + s
    return pl.ds(start, ATTN_BLOCK, stride=r) if r > 1 else pl.ds(start, ATTN_BLOCK)


def _for_units(r, nsub, fn, after):
    if r * nsub <= _UNROLL:
        after([fn(j, s) for j in range(nsub) for s in range(r)])
    else:
        def four(i, c):
            after([fn(0, _UNROLL * i + k) for k in range(_UNROLL)])
            return c

        lax.fori_loop(0, r // _UNROLL, four, 0)


def _attn_cols(g):
    return tuple((_OFF[n] + ATTN_GROUP_WIDTH * g) // LANES for n in ("q", "k", "v"))


def _attn_fwd(proj, bias, g, name, job=None):
    r = ATTN_DILATIONS[g]
    nsub = _ATTN_SUB[r]
    T = proj.shape[0]
    sub = ATTN_BLOCK * r
    tb = sub * nsub
    qc, kc, vc = _attn_cols(g)
    scale = ATTN_HEAD_DIM ** -0.5

    def body(q_ref, kc_ref, kp_ref, vc_ref, vp_ref, bias_ref, o_ref, lse_ref):
        lane = lax.broadcasted_iota(jnp.int32, (ATTN_BLOCK, LANES), 1)
        kj = lax.broadcasted_iota(jnp.int32, (ATTN_BLOCK, 2 * ATTN_BLOCK), 1)
        dead = (pl.program_id(0) == 0) & (kj < ATTN_BLOCK)

        def one(j, s):
            rows = _unit_rows(j, s, r)
            before = _unit_rows(max(j - 1, 0), s, r)
            k_before = kc_ref[before, :] if j else kp_ref[before, :]
            v_before = vc_ref[before, :] if j else vp_ref[before, :]
            q = q_ref[rows, :]
            kcat = jnp.concatenate([k_before, kc_ref[rows, :]], axis=0).astype(BF16)
            vcat = jnp.concatenate([v_before, vc_ref[rows, :]], axis=0).astype(BF16)
            o_acc = jnp.zeros((ATTN_BLOCK, LANES), F32)
            l_acc = jnp.zeros((ATTN_BLOCK, LANES), F32)
            for hh in range(2):
                mine = (lane >= ATTN_HEAD_DIM) if hh else (lane < ATTN_HEAD_DIM)
                qm = jnp.where(mine, q, 0.0).astype(BF16)
                sc = _dot(qm, kcat, NT) * scale + bias_ref[hh]
                if j == 0:
                    sc = jnp.where(dead, NEG_INF, sc)
                m = jnp.max(sc, axis=-1, keepdims=True)
                p = jnp.exp(sc - m)
                l = jnp.sum(p, axis=-1, keepdims=True)
                o_acc = jnp.where(mine, _dot((p / l).astype(BF16), vcat), o_acc)
                l_acc = jnp.where(mine, m + jnp.log(l), l_acc)
            o_ref[rows, :] = o_acc
            lse_ref[rows, :] = l_acc

        _for_units(r, nsub, one, lambda results: None)

    cur = lambda c: pl.BlockSpec((tb, LANES), lambda b, p: (b, c + p))
    prev = lambda c: pl.BlockSpec((sub, LANES), lambda b, p: (jnp.maximum(b * nsub - 1, 0), c + p))
    out = pl.BlockSpec((tb, LANES), lambda b, p: (b, p))
    return _pc(
        body, job, name=name, grid=(T // tb, 2),
        in_specs=[cur(qc), cur(kc), prev(kc), cur(vc), prev(vc),
                  pl.BlockSpec((2, ATTN_BLOCK, 2 * ATTN_BLOCK), lambda b, p: (p, 0, 0))],
        out_specs=[out, out],
        out_shape=[jax.ShapeDtypeStruct((T, ATTN_GROUP_WIDTH), F32), jax.ShapeDtypeStruct((T, ATTN_GROUP_WIDTH), F32)],
        scratch_shapes=[], sem=("parallel", "parallel"), operands=(proj, proj, proj, proj, proj, bias))


def _attn_bwd(proj, do, corr, lse, bias, dproj, g, name):
    r = ATTN_DILATIONS[g]
    nsub = _ATTN_SUB[r]
    T = proj.shape[0]
    sub = ATTN_BLOCK * r
    tb = sub * nsub
    nb = T // tb
    qc, kc, vc = _attn_cols(g)
    dc = ATTN_GROUP_WIDTH * g // LANES
    scale = ATTN_HEAD_DIM ** -0.5

    def body(q_ref, kc_ref, kp_ref, vc_ref, vp_ref, do_ref, corr_ref, lse_ref, bias_ref, _,
             dproj_ref, db_ref, dq_s, dkc_s, dkp_s, dvc_s, dvp_s, kacc, vacc, stage, stage_sems):
        p, b = pl.program_id(0), pl.program_id(1)

        def to_dproj(e, slot, block, col):
            rows = pl.ds(pl.multiple_of(block * tb, tb), tb)
            cols = pl.ds(pl.multiple_of((col + p) * LANES, LANES), LANES)
            return pltpu.make_async_copy(stage.at[e, slot], dproj_ref.at[rows, cols], stage_sems.at[e, slot])

        def emit(e, block, col, value):
            count = p * nb + block
            slot = count % 2

            @pl.when(count >= 2)
            def _():
                to_dproj(e, slot, 0, col).wait()

            stage[e, slot] = value.astype(BF16)
            to_dproj(e, slot, block, col).start()

        def emit_keys(block):
            emit(1, block, kc, kacc[...])
            emit(2, block, vc, vacc[...])

        @pl.when(b == 0)
        def _():
            db_ref[...] = jnp.zeros_like(db_ref)
            kacc[...] = jnp.zeros_like(kacc)
            vacc[...] = jnp.zeros_like(vacc)

        @pl.when(b == nb)
        def _():
            emit_keys(nb - 1)

        @pl.when((b == nb) & (p == 1))
        def _():
            for e, col in enumerate((qc, kc, vc)):
                for slot in range(2):
                    to_dproj(e, slot, 0, col).wait()

        @pl.when(b < nb)
        def _():
            lane = lax.broadcasted_iota(jnp.int32, (ATTN_BLOCK, LANES), 1)
            kj = lax.broadcasted_iota(jnp.int32, (ATTN_BLOCK, 2 * ATTN_BLOCK), 1)
            dead = (b == 0) & (kj < ATTN_BLOCK)

            def one(j, s):
                rows = _unit_rows(j, s, r)
                before = _unit_rows(max(j - 1, 0), s, r)
                k_before = kc_ref[before, :] if j else kp_ref[before, :]
                v_before = vc_ref[before, :] if j else vp_ref[before, :]
                q = q_ref[rows, :]
                kcat = jnp.concatenate([k_before, kc_ref[rows, :]], axis=0).astype(BF16)
                vcat = jnp.concatenate([v_before, vc_ref[rows, :]], axis=0).astype(BF16)
                dov, corrv, lsev = do_ref[rows, :], corr_ref[rows, :], lse_ref[rows, :]
                dq_acc = jnp.zeros((ATTN_BLOCK, LANES), F32)
                dk_acc = jnp.zeros((2 * ATTN_BLOCK, LANES), F32)
                dv_acc = jnp.zeros((2 * ATTN_BLOCK, LANES), F32)
                dss = []
                for hh in range(2):
                    mine = (lane >= ATTN_HEAD_DIM) if hh else (lane < ATTN_HEAD_DIM)
                    col = slice(ATTN_HEAD_DIM * hh, ATTN_HEAD_DIM * hh + 1)
                    qm = jnp.where(mine, q, 0.0).astype(BF16)
                    dom = jnp.where(mine, dov, 0.0).astype(BF16)
                    sc = _dot(qm, kcat, NT) * scale + bias_ref[hh]
                    if j == 0:
                        sc = jnp.where(dead, NEG_INF, sc)
                    p = jnp.exp(sc - lsev[:, col])
                    ds = p * (_dot(dom, vcat, NT) - corrv[:, col])
                    dss.append(ds)
                    dsb = ds.astype(BF16)
                    dq_acc = jnp.where(mine, _dot(dsb, kcat) * scale, dq_acc)
                    dk_acc += _dot(dsb, qm, TN) * scale
                    dv_acc += _dot(p.astype(BF16), dom, TN)
                dq_s[rows, :] = dq_acc
                dkp_s[rows, :] = dk_acc[:ATTN_BLOCK]
                dkc_s[rows, :] = dk_acc[ATTN_BLOCK:]
                dvp_s[rows, :] = dv_acc[:ATTN_BLOCK]
                dvc_s[rows, :] = dv_acc[ATTN_BLOCK:]
                return dss

            def add_bias_grads(results):
                for hh in range(2):
                    db_ref[hh] += functools.reduce(lambda x, y: x + y, [dss[hh] for dss in results])

            _for_units(r, nsub, one, add_bias_grads)
            emit(0, b, qc, dq_s[...])
            tail = slice((nsub - 1) * sub, nsub * sub)
            kacc[tail, :] += dkp_s[0:sub, :]
            vacc[tail, :] += dvp_s[0:sub, :]

            @pl.when(b >= 1)
            def _():
                emit_keys(b - 1)

            for acc, before_s, cur_s in ((kacc, dkp_s, dkc_s), (vacc, dvp_s, dvc_s)):
                acc[...] = cur_s[...]
                for j in range(nsub - 1):
                    acc[j * sub:(j + 1) * sub, :] += before_s[(j + 1) * sub:(j + 2) * sub, :]

    last = nb - 1
    blk = (tb, LANES)
    cur = lambda c: pl.BlockSpec(blk, lambda p, b: (jnp.minimum(b, last), c + p))
    before = lambda c: pl.BlockSpec((sub, LANES), lambda p, b: (jnp.clip(b * nsub - 1, 0, nb * nsub - 1), c + p))
    tab = pl.BlockSpec((2, ATTN_BLOCK, 2 * ATTN_BLOCK), lambda p, b: (p, 0, 0))
    hbm = pl.BlockSpec(memory_space=pl.ANY)
    return pl.pallas_call(
        body, name=name, grid=(2, nb + 1),
        in_specs=[cur(qc), cur(kc), before(kc), cur(vc), before(vc), cur(dc), cur(dc), cur(0), tab, hbm],
        out_specs=[hbm, tab],
        out_shape=[jax.ShapeDtypeStruct(dproj.shape, BF16), jax.ShapeDtypeStruct((4, ATTN_BLOCK, 2 * ATTN_BLOCK), F32)],
        input_output_aliases={9: 0},
        scratch_shapes=[pltpu.VMEM(blk, F32)] * 7 + [pltpu.VMEM((3, 2) + blk, BF16), pltpu.SemaphoreType.DMA((3, 2))],
        compiler_params=_params(("arbitrary", "arbitrary")),
    )(proj, proj, proj, proj, proj, do, corr, lse, bias, dproj)


def _mix_weights(lses):
    m = jnp.maximum(jnp.maximum(lses[0], lses[1]), lses[2])
    es = [jnp.exp(l - m) for l in lses]
    inv = 1.0 / (es[0] + es[1] + es[2])
    return jnp.concatenate([e * inv for e in es], axis=1)


def _attn_mix(os, lses, proj, tm, name):
    T = proj.shape[0]
    zcol = _OFF["z_attn"] // D_ATTN

    def body(o0, o1, o2, l0, l1, l2, z_ref, out_ref):
        z = z_ref[...]
        o = jnp.concatenate([o0[...], o1[...], o2[...]], axis=1)
        alpha = _mix_weights([l0[...], l1[...], l2[...]])
        out_ref[...] = (o * alpha * (z * _sigmoid(z))).astype(BF16)

    row = lambda i: (i, 0)
    grp = pl.BlockSpec((tm, ATTN_GROUP_WIDTH), row)
    return pl.pallas_call(
        body, name=name, grid=(T // tm,),
        in_specs=[grp] * 6 + [pl.BlockSpec((tm, D_ATTN), lambda i: (i, zcol))],
        out_specs=pl.BlockSpec((tm, D_ATTN), row),
        out_shape=jax.ShapeDtypeStruct((T, D_ATTN), BF16),
        compiler_params=_params(("parallel",)),
    )(*os, *lses, proj)


def _attn_mix_bwd(d, os, lses, proj, dproj, tm, name):
    T = proj.shape[0]
    zcol = _OFF["z_attn"] // D_ATTN

    def body(d_ref, o0, o1, o2, l0, l1, l2, z_ref, _, do_ref, corr_ref, dz_ref):
        dv, z = d_ref[...], z_ref[...]
        ov = jnp.concatenate([o0[...], o1[...], o2[...]], axis=1)
        alpha = _mix_weights([l0[...], l1[...], l2[...]])
        sz = _sigmoid(z)
        oc = ov * alpha
        dz_ref[...] = (dv * oc * (sz * (1.0 + z * (1.0 - sz)))).astype(BF16)
        doc = dv * (z * sz)
        do_ref[...] = doc * alpha
        pr = doc * oc
        p3 = pr[:, 0:256] + pr[:, 256:512] + pr[:, 512:768]
        li = lax.broadcasted_iota(jnp.int32, (256, 256), 0) // ATTN_HEAD_DIM
        lj = lax.broadcasted_iota(jnp.int32, (256, 256), 1) // ATTN_HEAD_DIM
        ones = jnp.where(li == lj, 1.0, 0.0).astype(F32)
        s = lax.dot_general(p3, ones, NN, precision=lax.Precision.HIGHEST, preferred_element_type=F32)
        corr_ref[...] = alpha * jnp.concatenate([s, s, s], axis=1)

    row = lambda i: (i, 0)
    grp = pl.BlockSpec((tm, ATTN_GROUP_WIDTH), row)
    return pl.pallas_call(
        body, name=name, grid=(T // tm,),
        in_specs=[pl.BlockSpec((tm, D_ATTN), row)] + [grp] * 6 + [pl.BlockSpec((tm, D_ATTN), lambda i: (i, zcol)),
                                                                    pl.BlockSpec(memory_space=pl.ANY)],
        out_specs=[pl.BlockSpec((tm, D_ATTN), row)] * 2 + [pl.BlockSpec((tm, D_ATTN), lambda i: (i, zcol))],
        out_shape=[jax.ShapeDtypeStruct((T, D_ATTN), F32), jax.ShapeDtypeStruct((T, D_ATTN), F32),
                   jax.ShapeDtypeStruct(dproj.shape, BF16)],
        input_output_aliases={8: 2},
        compiler_params=_params(("parallel",)),
    )(d, *os, *lses, proj, dproj)


def _mem_probs(q_ref, kv_ref, h):
    hs = slice(MEM_HEAD_DIM * h, MEM_HEAD_DIM * (h + 1))
    qh = q_ref[:, hs].astype(BF16)
    kh = kv_ref[:, hs]
    vh = kv_ref[:, D_MEM + MEM_HEAD_DIM * h:D_MEM + MEM_HEAD_DIM * (h + 1)]
    s = _dot(qh, kh, NT) * (MEM_HEAD_DIM ** -0.5)
    p = jnp.exp(s - jnp.max(s, axis=-1, keepdims=True))
    pn = p / jnp.sum(p, axis=-1, keepdims=True)
    return qh, kh, vh, pn


def _mem_fwd(proj, kv, tm, name):
    T = proj.shape[0]
    M = kv.shape[0]
    qcol, zcol = _OFF["q_mem"] // D_MEM, _OFF["z_mem"] // D_MEM

    def body(q_ref, z_ref, kv_ref, o_ref):
        outs = []
        for h in range(MEM_HEADS):
            _, _, vh, pn = _mem_probs(q_ref, kv_ref, h)
            outs.append(_dot(pn.astype(BF16), vh))
        z = z_ref[...]
        o_ref[...] = (jnp.concatenate(outs, axis=1) * (z * _sigmoid(z))).astype(BF16)

    return pl.pallas_call(
        body, name=name, grid=(T // tm,),
        in_specs=[pl.BlockSpec((tm, D_MEM), lambda i: (i, qcol)), pl.BlockSpec((tm, D_MEM), lambda i: (i, zcol)),
                  _full((M, 2 * D_MEM))],
        out_specs=pl.BlockSpec((tm, D_MEM), lambda i: (i, 0)),
        out_shape=jax.ShapeDtypeStruct((T, D_MEM), BF16),
        compiler_params=_params(("parallel",)),
    )(proj, proj, kv)


def _mem_bwd(d, proj, kv, tm, name):
    T = proj.shape[0]
    M = kv.shape[0]
    qcol, zcol = _OFF["q_mem"] // D_MEM, _OFF["z_mem"] // D_MEM

    def body(d_ref, q_ref, z_ref, kv_ref, dq_ref, dz_ref, dkv_ref):
        @pl.when(pl.program_id(0) == 0)
        def _():
            dkv_ref[...] = jnp.zeros_like(dkv_ref)

        z = z_ref[...]
        sz = _sigmoid(z)
        dv = d_ref[...]
        dov = dv * (z * sz)
        scale = MEM_HEAD_DIM ** -0.5
        outs, dqs = [], []
        for h in range(MEM_HEADS):
            hs = slice(MEM_HEAD_DIM * h, MEM_HEAD_DIM * (h + 1))
            qh, kh, vh, pn = _mem_probs(q_ref, kv_ref, h)
            pnb = pn.astype(BF16)
            oh = _dot(pnb, vh)
            outs.append(oh)
            doh = dov[:, hs]
            dohb = doh.astype(BF16)
            dp = _dot(dohb, vh, NT)
            ds = pn * (dp - jnp.sum(doh * oh, axis=-1, keepdims=True))
            dsb = ds.astype(BF16)
            dqs.append(_dot(dsb, kh) * scale)
            dkv_ref[:, hs] += _dot(dsb, qh, TN) * scale
            vs = slice(D_MEM + MEM_HEAD_DIM * h, D_MEM + MEM_HEAD_DIM * (h + 1))
            dkv_ref[:, vs] += _dot(pnb, dohb, TN)
        dq_ref[...] = jnp.concatenate(dqs, axis=1).astype(BF16)
        dz_ref[...] = (dv * jnp.concatenate(outs, axis=1) * (sz * (1.0 + z * (1.0 - sz)))).astype(BF16)

    row = lambda i: (i, 0)
    return pl.pallas_call(
        body, name=name, grid=(T // tm,),
        in_specs=[pl.BlockSpec((tm, D_MEM), row), pl.BlockSpec((tm, D_MEM), lambda i: (i, qcol)),
                  pl.BlockSpec((tm, D_MEM), lambda i: (i, zcol)), _full((M, 2 * D_MEM))],
        out_specs=[pl.BlockSpec((tm, D_MEM), row), pl.BlockSpec((tm, D_MEM), row), _full((M, 2 * D_MEM))],
        out_shape=[jax.ShapeDtypeStruct((T, D_MEM), BF16), jax.ShapeDtypeStruct((T, D_MEM), BF16),
                   jax.ShapeDtypeStruct((M, 2 * D_MEM), F32)],
        compiler_params=_params(("arbitrary",)),
    )(d, proj, proj, kv)


def _branches_and_gates(os_ref, oa_ref, om_ref, gl_refs, bg_ref, ws_ref, wa_ref, wm_ref):
    outs = (_dot(os_ref[...], ws_ref[...]), _dot(oa_ref[...], wa_ref[...]), _dot(om_ref[...], wm_ref[...]))
    gates = tuple(_sigmoid(jnp.concatenate([gl_refs[2 * k][...], gl_refs[2 * k + 1][...]], axis=1)
                           + bg_ref[:, D_MODEL * k:D_MODEL * (k + 1)]) for k in range(3))
    return outs, gates


def _merge_specs(tm):
    row = lambda i: (i, 0)
    first = _OFF["gates"] // GATE_BLOCK
    gate = [pl.BlockSpec((tm, GATE_BLOCK), (lambda i, k=k: (i, first + k))) for k in range(N_GATES // GATE_BLOCK)]
    return ([pl.BlockSpec((tm, D_SSM), row), pl.BlockSpec((tm, D_ATTN), row), pl.BlockSpec((tm, D_MEM), row)] + gate
            + [_full((1, N_GATES)), _full((D_SSM, D_MODEL)), _full((D_ATTN, D_MODEL)), _full((D_MEM, D_MODEL)),
               _full((D_MODEL, D_MODEL))])


def _merge_fwd(x, o_ssm, o_attn, o_mem, proj, bg, ws, wa, wm, wo, tm, name):
    T = x.shape[0]

    def body(os_ref, oa_ref, om_ref, g0, g1, g2, g3, g4, g5, bg_ref, ws_ref, wa_ref, wm_ref, wo_ref, x_ref,
             xo_ref, mg_ref):
        outs, gates = _branches_and_gates(os_ref, oa_ref, om_ref, (g0, g1, g2, g3, g4, g5), bg_ref, ws_ref, wa_ref,
                                          wm_ref)
        merged = (gates[0] * outs[0] + gates[1] * outs[1] + gates[2] * outs[2]).astype(BF16)
        mg_ref[...] = merged
        xo_ref[...] = x_ref[...] + _dot(merged, wo_ref[...])

    row = lambda i: (i, 0)
    return pl.pallas_call(
        body, name=name, grid=(T // tm,),
        in_specs=_merge_specs(tm) + [pl.BlockSpec((tm, D_MODEL), row)],
        out_specs=[pl.BlockSpec((tm, D_MODEL), row), pl.BlockSpec((tm, D_MODEL), row)],
        out_shape=[jax.ShapeDtypeStruct((T, D_MODEL), F32), jax.ShapeDtypeStruct((T, D_MODEL), BF16)],
        compiler_params=_params(("parallel",)),
    )(o_ssm, o_attn, o_mem, *([proj] * (N_GATES // GATE_BLOCK)), bg, ws, wa, wm, wo, x)


def _merge_bwd(dx, o_ssm, o_attn, o_mem, proj, bg, ws, wa, wm, wo, tm, name, job=None):
    T = dx.shape[0]

    n = T // tm

    def body(os_ref, oa_ref, om_ref, g0, g1, g2, g3, g4, g5, bg_ref, ws_ref, wa_ref, wm_ref, wo_ref, dx_ref,
             dproj_ref, db_ref, dos_ref, doa_ref, dom_ref, dbg_ref, dgl_buf, dgl_sems):
        i = pl.program_id(0)
        slot = i % 2

        def to_dproj(s, row0):
            return pltpu.make_async_copy(dgl_buf.at[s], dproj_ref.at[pl.ds(row0, tm), pl.ds(_OFF["gates"], N_GATES)],
                                         dgl_sems.at[s])

        @pl.when(i == 0)
        def _():
            dbg_ref[...] = jnp.zeros_like(dbg_ref)

        @pl.when(i >= 2)
        def _():
            to_dproj(slot, 0).wait()

        outs, gates = _branches_and_gates(os_ref, oa_ref, om_ref, (g0, g1, g2, g3, g4, g5), bg_ref, ws_ref, wa_ref,
                                          wm_ref)
        dm = _dot(dx_ref[...].astype(BF16), wo_ref[...], NT)
        w_refs = (ws_ref, wa_ref, wm_ref)
        do_refs = (dos_ref, doa_ref, dom_ref)
        for k in range(3):
            cols = slice(D_MODEL * k, D_MODEL * (k + 1))
            dgl = dm * outs[k] * (gates[k] * (1.0 - gates[k]))
            dgl_buf[slot, :, cols] = dgl.astype(BF16)
            dbg_ref[:, cols] += jnp.sum(dgl, axis=0, keepdims=True)
            dbk = (dm * gates[k]).astype(BF16)
            db_ref[:, cols] = dbk
            do_refs[k][...] = _dot(dbk, w_refs[k][...], NT)
        to_dproj(slot, pl.multiple_of(i * tm, tm)).start()

        @pl.when(i == n - 1)
        def _():
            for s in range(min(2, n)):
                to_dproj(s, 0).wait()

    row = lambda i: (i, 0)
    return _pc(
        body, job, name=name, grid=(n,),
        in_specs=_merge_specs(tm) + [pl.BlockSpec((tm, D_MODEL), row)],
        out_specs=[pl.BlockSpec(memory_space=pl.ANY), pl.BlockSpec((tm, N_GATES), row), pl.BlockSpec((tm, D_SSM), row),
                   pl.BlockSpec((tm, D_ATTN), row), pl.BlockSpec((tm, D_MEM), row), _full((1, N_GATES))],
        out_shape=[jax.ShapeDtypeStruct((T, D_IN), BF16), jax.ShapeDtypeStruct((T, N_GATES), BF16),
                   jax.ShapeDtypeStruct((T, D_SSM), F32), jax.ShapeDtypeStruct((T, D_ATTN), F32),
                   jax.ShapeDtypeStruct((T, D_MEM), F32), jax.ShapeDtypeStruct((1, N_GATES), F32)],
        scratch_shapes=[pltpu.VMEM((2, tm, N_GATES), BF16), pltpu.SemaphoreType.DMA((2,))], sem=("arbitrary",),
        operands=(o_ssm, o_attn, o_mem, *([proj] * (N_GATES // GATE_BLOCK)), bg, ws, wa, wm, wo, dx))


def _loss_head(x, g, target, tm, name):
    T, D = x.shape

    def body(x_ref, g_ref, t_ref, loss_ref, dx_ref, dg_ref):
        @pl.when(pl.program_id(0) == 0)
        def _():
            loss_ref[...] = jnp.zeros_like(loss_ref)
            dg_ref[...] = jnp.zeros_like(dg_ref)

        xv = x_ref[...]
        r = lax.rsqrt(jnp.mean(xv * xv, axis=-1, keepdims=True) + EPS)
        xr = xv * r
        err = xr * g_ref[...] - t_ref[...]
        loss_ref[...] += 0.5 * jnp.sum(jnp.mean(err * err, axis=-1, keepdims=True), axis=0, keepdims=True)
        dy = err * (1.0 / D)
        dg_ref[...] += jnp.sum(dy * xr, axis=0, keepdims=True)
        wv = dy * g_ref[...]
        dx_ref[...] = r * (wv - xr * jnp.mean(wv * xr, axis=-1, keepdims=True))

    row = lambda i: (i, 0)
    return pl.pallas_call(
        body, name=name, grid=(T // tm,),
        in_specs=[pl.BlockSpec((tm, D), row), _full((1, D)), pl.BlockSpec((tm, D), row)],
        out_specs=[_full((1, 128)), pl.BlockSpec((tm, D), row), _full((1, D))],
        out_shape=[jax.ShapeDtypeStruct((1, 128), F32), jax.ShapeDtypeStruct((T, D), F32),
                   jax.ShapeDtypeStruct((1, D), F32)],
        compiler_params=_params(("arbitrary",)),
    )(x, g, target)


def _adamw(parts, w, m, v, tr, name):
    L, R, C = w.shape

    def body(p_ref, w_ref, m_ref, v_ref, g_ref, d_ref, mo_ref, vo_ref):
        g = p_ref[0].astype(F32)
        for s in range(1, N_DEV):
            g = g + p_ref[s].astype(F32)
        mn = ADAM_B1 * m_ref[...] + (1.0 - ADAM_B1) * g
        vn = ADAM_B2 * v_ref[...] + (1.0 - ADAM_B2) * (g * g)
        m_hat = mn / (1.0 - ADAM_B1 ** ADAM_STEP)
        v_hat = vn / (1.0 - ADAM_B2 ** ADAM_STEP)
        g_ref[...] = g
        d_ref[...] = -ADAM_LR * (m_hat / (jnp.sqrt(v_hat) + ADAM_EPS) + ADAM_WD * w_ref[...])
        mo_ref[...] = mn
        vo_ref[...] = vn

    one = pl.BlockSpec((None, tr, C), lambda l, i: (l, i, 0))
    return pl.pallas_call(
        body, name=name, grid=(L, R // tr),
        in_specs=[pl.BlockSpec((N_DEV, None, tr, C), lambda l, i: (0, l, i, 0)), one, one, one],
        out_specs=[one] * 4,
        out_shape=[jax.ShapeDtypeStruct((L, R, C), F32)] * 4,
        compiler_params=_params(("parallel", "parallel")),
    )(parts, w, m, v)


_SHARDED = (("w_in", (1088, 1024), 1), ("w_glu", (96, 768), 0), ("w_mem_kv", (128, 1024), 0),
            ("w_br_ssm", (768, 128), 1), ("w_br_attn", (768, 128), 1), ("w_br_mem", (512, 128), 1),
            ("w_out", (128, 1024), 0))
_W_IN = 0
_SMALL = tuple(range(1, len(_SHARDED)))


class _Job(NamedTuple):
    ins: list
    out_shape: list
    aliases: dict
    pairs: Callable
    n: int


def _peers():
    x, y, c = lax.axis_index("x"), lax.axis_index("y"), lax.axis_index("c")
    me = 4 * x + 2 * y + c
    out = []
    for k in range(1, N_DEV):
        px = 1 - x if k & 4 else x
        py = 1 - y if k & 2 else y
        pc = 1 - c if k & 1 else c
        out.append(((px, py, pc), 4 * px + 2 * py + pc))
    return me, out


def _copies(pairs, send_sems, recv_sems, local_sems, arrivals):
    me, peers = _peers()
    local = [pltpu.make_async_copy(src(me), dst(me), local_sems.at[j]) for j, (src, dst) in enumerate(pairs)]
    sends, recvs = [], []
    for k, (peer, lin) in enumerate(peers):
        for j, (src, dst) in enumerate(pairs):
            for to, out in ((dst(me), sends), (dst(lin), recvs)):
                if out is sends or arrivals:
                    out.append(pltpu.make_async_remote_copy(
                        src_ref=src(lin), dst_ref=to, send_sem=send_sems.at[j, k], recv_sem=recv_sems.at[j, k],
                        device_id=peer, device_id_type=pl.DeviceIdType.MESH))
    return local, sends, recvs


def _start_copies(pairs, *sems):
    local, sends, _ = _copies(pairs, *sems, arrivals=False)
    for cp in local + sends:
        cp.start()


def _wait_copies(pairs, *sems):
    local, sends, recvs = _copies(pairs, *sems, arrivals=True)
    for cp in recvs:
        cp.wait_recv()
    for cp in sends:
        cp.wait_send()
    for cp in local:
        cp.wait()


def _job_scratch(job):
    return [pltpu.SemaphoreType.DMA((job.n, N_DEV - 1)), pltpu.SemaphoreType.DMA((job.n, N_DEV - 1)),
            pltpu.SemaphoreType.DMA((job.n,))]


def _pc(body, job, *, name, grid, in_specs, out_specs, out_shape, scratch_shapes, sem, operands, aliases=None):
    aliases = aliases or {}
    if job is None:
        return pl.pallas_call(body, name=name, grid=grid, in_specs=in_specs, out_specs=out_specs, out_shape=out_shape,
                              scratch_shapes=scratch_shapes, input_output_aliases=aliases,
                              compiler_params=_params(sem))(*operands)
    a = len(in_specs)
    b = a + len(job.ins)
    c = b + len(out_shape)
    d = c + len(job.out_shape)
    e = d + len(scratch_shapes)

    def carried(*refs):
        pairs = job.pairs(refs[a:b], refs[c:d])
        ids = [pl.program_id(k) for k in range(len(grid))]
        first = functools.reduce(jnp.logical_and, [i == 0 for i in ids])
        last = functools.reduce(jnp.logical_and, [i == n - 1 for i, n in zip(ids, grid)])

        @pl.when(first)
        def _():
            _start_copies(pairs, *refs[e:])

        body(*refs[:a], *refs[b:c], *refs[d:e])

        @pl.when(last)
        def _():
            _wait_copies(pairs, *refs[e:])

    hbm = pl.BlockSpec(memory_space=pl.ANY)
    outs = pl.pallas_call(
        carried, name=name, grid=grid,
        in_specs=list(in_specs) + [hbm] * len(job.ins), out_specs=list(out_specs) + [hbm] * len(job.out_shape),
        out_shape=list(out_shape) + list(job.out_shape),
        input_output_aliases={**aliases, **{a + i: len(out_shape) + o for i, o in job.aliases.items()}},
        scratch_shapes=list(scratch_shapes) + _job_scratch(job),
        compiler_params=_params(("arbitrary",) * len(grid)),
    )(*operands, *job.ins)
    return outs[:len(out_shape)], outs[len(out_shape):]


def _gather_via_sibling(x, take, place, out_shape, name, landing=None):
    def body(*refs):
        x_ref, o_ref = refs[0], refs[-4]
        send_sems, recv_sems, local_sem = refs[-3:]
        x, y, c = lax.axis_index("x"), lax.axis_index("y"), lax.axis_index("c")
        me, sibling = (x, y, c), (x, y, 1 - c)
        chips = [(1 - x, y), (x, 1 - y), (1 - x, 1 - y)]
        src = take(x_ref)

        def slot(px, py, pc):
            return place(o_ref, 4 * px + 2 * py + pc)

        def copy(k, block, to, first_hand):
            return pltpu.make_async_remote_copy(
                src_ref=src if first_hand else slot(*block), dst_ref=slot(*block), send_sem=send_sems.at[k],
                recv_sem=recv_sems.at[k], device_id=to, device_id_type=pl.DeviceIdType.MESH)

        mine = pltpu.make_async_copy(src, slot(*me), local_sem)
        mine.start()
        first = [copy(0, me, sibling, True)] + [copy(1 + j, me, (*chip, c), True) for j, chip in enumerate(chips)]
        for cp in first:
            cp.start()
        passed = []
        for j, chip in enumerate(chips):
            copy(1 + j, (*chip, c), me, True).wait_recv()
            passed.append(copy(4 + j, (*chip, c), sibling, False))
            passed[-1].start()
        copy(0, sibling, me, True).wait_recv()
        for j, chip in enumerate(chips):
            copy(4 + j, (*chip, 1 - c), me, False).wait_recv()
        for cp in first + passed:
            cp.wait_send()
        mine.wait()

    hbm = pl.BlockSpec(memory_space=pl.ANY)
    ins = [x] if landing is None else [x, landing]
    return pl.pallas_call(
        body, name=name, in_specs=[hbm] * len(ins), out_specs=hbm, out_shape=out_shape,
        input_output_aliases={} if landing is None else {1: 0},
        scratch_shapes=[pltpu.SemaphoreType.DMA((N_DEV - 1,)), pltpu.SemaphoreType.DMA((N_DEV - 1,)),
                        pltpu.SemaphoreType.DMA],
    )(*ins)


def _lane_window(ref, who):
    return ref.at[:, pl.ds(pl.multiple_of(who * LANES, LANES), LANES)]


def _gather_job(shards, items):
    out_shape = []
    for i, _ in items:
        _, s, axis = _SHARDED[i]
        whole = i != _W_IN and axis == 1
        out_shape.append(jax.ShapeDtypeStruct((s[0], N_DEV * s[1]) if whole else (N_DEV,) + s, BF16))

    def pairs(in_refs, out_refs):
        out = []
        for (i, l), src, dst in zip(items, in_refs, out_refs):
            if i != _W_IN and _SHARDED[i][2] == 1:
                out.append((lambda who, src=src, l=l: src.at[l], lambda who, dst=dst: _lane_window(dst, who)))
            else:
                out.append((lambda who, src=src, l=l: src.at[l], lambda who, dst=dst: dst.at[who]))
        return out

    return _Job([shards[i] for i, _ in items], out_shape, {}, pairs, len(items))


def _landed_weights(items, landed):
    out = {}
    for (i, _), a in zip(items, landed):
        n, s, axis = _SHARDED[i]
        if i == _W_IN:
            out[n] = a.reshape(D_IN, D_MODEL)
        elif axis == 0:
            out[n] = a.reshape(N_DEV * s[0], s[1])
        else:
            out[n] = a
    return out


def _scatter_job(grads, items, layer, parts=None):
    ng = len(grads)
    out_shape = [jax.ShapeDtypeStruct((N_DEV, DEPTH) + _SHARDED[i][1], BF16) for i in items]

    def pairs(in_refs, out_refs):
        out = []
        for i, src, dst in zip(items, in_refs[:ng], out_refs):
            _, s, axis = _SHARDED[i]
            if i == _W_IN:
                take = lambda who, src=src: src.at[who]
            elif axis == 0:
                take = lambda who, src=src, s=s: src.at[pl.ds(pl.multiple_of(who * s[0], 16), s[0])]
            else:
                take = lambda who, src=src: _lane_window(src, who)
            out.append((take, lambda who, dst=dst: dst.at[who, layer]))
        return out

    aliases = {} if parts is None else {ng + j: j for j in range(len(items))}
    return _Job(list(grads) + ([] if parts is None else list(parts)), out_shape, aliases, pairs, len(items))


def _rows_job(src, row0, landing=None):
    n = src.shape[0]
    pairs = lambda in_refs, out_refs: [(lambda who: in_refs[0], lambda who: out_refs[0].at[who, pl.ds(row0, n)])]
    return _Job([src] + ([] if landing is None else [landing]), [jax.ShapeDtypeStruct((N_DEV, _REP_ROWS, LANES), F32)],
                {} if landing is None else {1: 0}, pairs, 1)


_REPLICATED = (("norm_g", (2, 1024)), ("mem_norm_g", (2, 1024)), ("b_gate", (2, 3072)),
               ("ssm_lambda_re", (2, 48, 64)), ("ssm_lambda_im", (2, 48, 64)), ("ssm_log_dt", (2, 48)),
               ("ssm_b_re", (2, 48, 64, 16)), ("ssm_b_im", (2, 48, 64, 16)), ("ssm_c_re", (2, 48, 16, 64)),
               ("ssm_c_im", (2, 48, 16, 64)), ("ssm_d", (2, 768)), ("b_glu", (2, 768)), ("rel_bias", (32, 12)),
               ("final_norm_g", (1024,)))
_PER_LAYER = tuple((n, s[1:]) for n, s in _REPLICATED if s[0] == DEPTH and len(s) > 1)
_SHARED = tuple((n, s) for n, s in _REPLICATED if (n, s[1:]) not in _PER_LAYER)
_REP_HALF_ROWS = 1664
_REP_ROWS = 2 * _REP_HALF_ROWS
assert sum(int(np.prod(s)) for _, s in _PER_LAYER + _SHARED) <= _REP_HALF_ROWS * LANES


def _pack_half(tree, layer, shared):
    flat = [tree[n][layer].reshape(-1) for n, _ in _PER_LAYER]
    if shared:
        flat += [tree[n].reshape(-1) for n, _ in _SHARED]
    flat = jnp.concatenate(flat)
    return jnp.pad(flat, (0, _REP_HALF_ROWS * LANES - flat.shape[0])).reshape(_REP_HALF_ROWS, LANES)


def _pack_replicated(tree):
    return jnp.concatenate([_pack_half(tree, 1, False), _pack_half(tree, 0, True)])[None]


def _unpack_replicated(packed):
    halves = packed.reshape(2, -1)
    out, r = {}, 0
    for n, s in _PER_LAYER:
        size = int(np.prod(s))
        out[n] = jnp.stack([halves[1, r:r + size].reshape(s), halves[0, r:r + size].reshape(s)])
        r += size
    for n, s in _SHARED:
        size = int(np.prod(s))
        out[n] = halves[1, r:r + size].reshape(s)
        r += size
    return out


def _discretize(lam_re, lam_im, log_dt, b_re, b_im):
    dt = jnp.exp(log_dt)[:, None]
    mag = jnp.exp(lam_re * dt)
    abar_re, abar_im = mag * jnp.cos(lam_im * dt), mag * jnp.sin(lam_im * dt)
    den = lam_re * lam_re + lam_im * lam_im
    nr, ni = abar_re - 1.0, abar_im
    f_re = (nr * lam_re + ni * lam_im) / den
    f_im = (ni * lam_re - nr * lam_im) / den
    bbar_re = f_re[..., None] * b_re - f_im[..., None] * b_im
    bbar_im = f_re[..., None] * b_im + f_im[..., None] * b_re
    return abar_re, abar_im, bbar_re, bbar_im


def _block_diag(a):
    _, R, C = a.shape
    a = a.reshape(SSM_BLOCKS, 8, R, C)
    eye = jnp.eye(8, dtype=a.dtype)
    return (a[:, :, :, None, :] * eye[None, :, None, :, None]).reshape(SSM_BLOCKS, 8 * R, 8 * C)


def _diag_blocks(a, R, C):
    a = a.reshape(SSM_BLOCKS, 8, R, 8, C)
    eye = jnp.eye(8, dtype=a.dtype)
    return jnp.sum(a * eye[None, :, None, :, None], axis=3).reshape(SSM_GROUPS, R, C)


def _carried(result, job):
    return (result, None) if job is None else result


def _layer_fwd(x, mem, W, P, bias, layer, jobs):
    tag = f"l{layer}"
    abar_re, abar_im, bbar_re, bbar_im = _discretize(P["ssm_lambda_re"][layer], P["ssm_lambda_im"][layer],
                                                     P["ssm_log_dt"][layer], P["ssm_b_re"][layer], P["ssm_b_im"][layer])
    c_re, c_im = P["ssm_c_re"][layer], P["ssm_c_im"][layer]
    ssm = dict(
        are=abar_re.reshape(1, N_STATE), aim=abar_im.reshape(1, N_STATE),
        bre=_block_diag(bbar_re.transpose(0, 2, 1)).astype(BF16), bim=_block_diag(bbar_im.transpose(0, 2, 1)).astype(BF16),
        cre=_block_diag(c_re.transpose(0, 2, 1)).astype(BF16), cimn=_block_diag(-c_im.transpose(0, 2, 1)).astype(BF16),
        ctre=_block_diag(c_re).astype(BF16), ctimn=_block_diag(-c_im).astype(BF16),
        btre=_block_diag(bbar_re).astype(BF16), btim=_block_diag(bbar_im).astype(BF16),
        d=P["ssm_d"][layer].reshape(1, D_SSM))
    bglu = P["b_glu"][layer].reshape(1, D_SSM)
    bgate = P["b_gate"][layer].reshape(1, N_GATES)
    g = P["norm_g"][layer].reshape(1, D_MODEL)
    gm = P["mem_norm_g"][layer].reshape(1, D_MODEL)
    delivered = {}

    def carry(stage):
        return jobs[stage][0] if stage in jobs else None

    def deliver(stage, landed):
        if landed is not None:
            delivered[stage] = _landed_weights(jobs[stage][1], landed)

    T = x.shape[0]
    (proj, h), landed = _carried(_norm_proj(x, g, W["w_in"], min(T, 1024), 2176, f"{tag}_proj", job=carry("proj"),
                                            w_turned=True), carry("proj"))
    deliver("proj", landed)
    W = {**W, **delivered.get("proj", {})}
    (xr, xi, y, o_ssm), landed = _carried(
        _ssm_fwd(proj, ssm["bre"], ssm["bim"], ssm["cre"], ssm["cimn"], ssm["are"], ssm["aim"], ssm["d"], W["w_glu"],
                 bglu, 512, f"{tag}_ssm", job=carry("ssm")), carry("ssm"))
    deliver("ssm", landed)
    os, lses = [], []
    for grp in range(3):
        stage = f"attn{grp}"
        (o_g, lse_g), landed = _carried(_attn_fwd(proj, bias[grp], grp, f"{tag}_{stage}", job=carry(stage)), carry(stage))
        deliver(stage, landed)
        os.append(o_g)
        lses.append(lse_g)
    o_attn = _attn_mix(os, lses, proj, min(T, ROW_TILE), f"{tag}_attn_mix")
    kvb, hm = _norm_proj(mem, gm, W["w_mem_kv"], mem.shape[0], 1024, f"{tag}_mem_kv", out_dtype=BF16)
    o_mem = _mem_fwd(proj, kvb, min(T, ROW_TILE), f"{tag}_mem")
    x_out, merged = _merge_fwd(x, o_ssm, o_attn, o_mem, proj, bgate, W["w_br_ssm"], W["w_br_attn"], W["w_br_mem"],
                               W["w_out"], 512, f"{tag}_merge")
    res = dict(x=x, mem=mem, proj=proj, h=h, xr=xr, xi=xi, y=y, o_ssm=o_ssm, os=os, lses=lses,
               o_attn=o_attn, kvb=kvb, hm=hm, o_mem=o_mem, merged=merged, ssm=ssm, bglu=bglu,
               bgate=bgate, g=g, gm=gm, W=W)
    return x_out, res, delivered


def _layer_bwd(dx, res, P, bias, layer, jobs):
    tag = f"l{layer}b"
    proj, ssm, W = res["proj"], res["ssm"], res["W"]
    T = dx.shape[0]
    landed = {}

    def run(stage, fn, job):
        out, landed[stage] = _carried(fn(job), job)
        if job is None:
            del landed[stage]
        return out

    dproj, dbr, do_ssm, do_attn, do_mem, dbg = run(
        "merge", lambda job: _merge_bwd(dx, res["o_ssm"], res["o_attn"], res["o_mem"], proj, res["bgate"], W["w_br_ssm"],
                                        W["w_br_attn"], W["w_br_mem"], W["w_out"], 512, f"{tag}_merge", job=job),
        jobs.get("merge"))
    gw = {}
    tk = min(T, 1024)
    gw["w_out"] = _mm_tn(res["merged"], dx, 1024, 1024, tk, f"{tag}_dw_out")
    gw["w_br_ssm"] = _mm_tn(res["o_ssm"], dbr, 768, 1024, tk, f"{tag}_dw_br_ssm", b_col=0, n=1024)
    gw["w_br_attn"] = _mm_tn(res["o_attn"], dbr, 768, 1024, tk, f"{tag}_dw_br_attn", b_col=1024, n=1024)
    gw["w_br_mem"] = _mm_tn(res["o_mem"], dbr, 512, 1024, tk, f"{tag}_dw_br_mem", b_col=2048, n=1024)

    rows = min(T, ROW_TILE)
    dqm, dzm, dkv = _mem_bwd(do_mem, proj, res["kvb"], rows, f"{tag}_mem")
    M = dkv.shape[0]
    gw["w_mem_kv"] = _mm_tn(res["hm"], dkv, 1024, 1024, M, f"{tag}_dw_mem_kv")
    _, dgm = _proj_bwd(dkv.astype(BF16), W["w_mem_kv"], res["mem"], res["gm"], jnp.zeros_like(res["mem"]), M, 1024,
                       f"{tag}_mem_norm")

    do_g, corr, dproj = _attn_mix_bwd(do_attn, res["os"], res["lses"], proj, dproj, rows, f"{tag}_attn_mix")
    dbs = []
    for grp in range(3):
        dproj, db_g = _attn_bwd(proj, do_g, corr, res["lses"][grp], bias[grp], dproj, grp, f"{tag}_attn{grp}")
        dbs.append(db_g)
    dbias = jnp.stack(dbs)

    dy, dproj, gelu_b, dt_b, dbglu = _glu_bwd(do_ssm, res["y"], proj, W["w_glu"], res["bglu"], dproj, rows, f"{tag}_glu")
    gw["w_glu"] = _mm_tn(gelu_b, dt_b, 768, 768, tk, f"{tag}_dw_glu")
    dproj, dbre, dbim, dcre, dcim, dare, daim, dd = run(
        "ssm", lambda job: _ssm_bwd(dy, proj, res["xr"], res["xi"], ssm["ctre"], ssm["ctimn"], ssm["btre"], ssm["btim"],
                                    ssm["are"], ssm["aim"], ssm["d"], dproj, 256, f"{tag}_ssm", job=job),
        jobs.get("ssm"))
    _, disc_vjp = jax.vjp(_discretize, P["ssm_lambda_re"][layer], P["ssm_lambda_im"][layer], P["ssm_log_dt"][layer],
                          P["ssm_b_re"][layer], P["ssm_b_im"][layer])
    d_lre, d_lim, d_ldt, d_bre, d_bim = disc_vjp((dare.reshape(SSM_GROUPS, SSM_STATE), daim.reshape(SSM_GROUPS, SSM_STATE),
                                                  _diag_blocks(dbre, SSM_STATE, SSM_GROUP),
                                                  _diag_blocks(dbim, SSM_STATE, SSM_GROUP)))

    small = [gw[_SHARDED[i][0]] for i in _SMALL]
    for seg, piece in (("q_mem", dqm), ("z_mem", dzm)):
        dproj = lax.dynamic_update_slice(dproj, piece, (0, _OFF[seg]))
    dw_in = run("dw_in", lambda job: _mm_tn(dproj, res["h"], 2176, 1024, min(T, 1024), f"{tag}_dw_in", job=job),
                jobs["dw_in"](small) if "dw_in" in jobs else None)
    dw_in = dw_in.reshape((N_DEV,) + _SHARDED[_W_IN][1])
    dx_in, dg = run("proj", lambda job: _proj_bwd(dproj, W["w_in"], res["x"], res["g"], dx, min(T, 1024), 2176,
                                                  f"{tag}_proj", job=job, w_turned=True),
                    jobs["proj"](small, dw_in, landed) if "proj" in jobs else None)

    gp = dict(norm_g=dg[0], mem_norm_g=dgm[0], b_gate=dbg[0], ssm_lambda_re=d_lre, ssm_lambda_im=d_lim,
              ssm_log_dt=d_ldt, ssm_b_re=d_bre, ssm_b_im=d_bim,
              ssm_c_re=_diag_blocks(dcre, SSM_GROUP, SSM_STATE), ssm_c_im=_diag_blocks(dcim, SSM_GROUP, SSM_STATE),
              ssm_d=dd[0], b_glu=dbglu[0])
    return dx_in, dw_in, gp, dbias, landed


def _train_step(x, mem, target, shards, P):
    rest0 = [(i, 0) for i in _SMALL]
    thirds1 = [[(i, 1) for i in _SMALL[k::3]] for k in range(3)]
    first = [(_W_IN, 0)]
    w_in0 = _gather_via_sibling(shards[_W_IN], lambda ref: ref.at[0], lambda ref, s: ref.at[s],
                                jax.ShapeDtypeStruct((N_DEV,) + _SHARDED[_W_IN][1], BF16), "gather_w_in0")
    W0 = _landed_weights(first, [w_in0])
    buckets = _bucket_tables()
    bias = _bias_tables(P["rel_bias"], buckets, "bias_tables")
    jobs0 = {"proj": (_gather_job(shards, rest0), rest0), "ssm": (_gather_job(shards, [(_W_IN, 1)]), [(_W_IN, 1)]),
             **{f"attn{k}": (_gather_job(shards, items), items) for k, items in enumerate(thirds1)}}
    x, res0, delivered = _layer_fwd(x, mem, W0, P, bias, 0, jobs0)
    W1 = {**delivered["ssm"], **delivered["attn0"], **delivered["attn1"], **delivered["attn2"]}
    x, res1, _ = _layer_fwd(x, mem, W1, P, bias, 1, {})
    loss, dx, dgf = _loss_head(x, P["final_norm_g"].reshape(1, D_MODEL), target, min(x.shape[0], ROW_TILE),
                                "loss_head")

    dx, dw_in1, gp1, dbias1, landed1 = _layer_bwd(
        dx, res1, P, bias, 1, {"proj": lambda small, dw_in, landed: _scatter_job(small, _SMALL, 1)})
    rep1 = _pack_half({n: a[None] for n, a in gp1.items()}, 0, False)
    dx, _, gp0, dbias0, landed0 = _layer_bwd(
        dx, res0, P, bias, 0,
        {"merge": _rows_job(rep1, 0), "ssm": _scatter_job([dw_in1], [_W_IN], 1),
         "dw_in": lambda small: _scatter_job(small, _SMALL, 0, parts=landed1["proj"]),
         "proj": lambda small, dw_in, landed: _scatter_job([dw_in], [_W_IN], 0, parts=landed["ssm"])})
    d_rel = _bias_grad(dbias0, dbias1, buckets, "bias_grad")
    gp0 = {n: a[None] for n, a in gp0.items()}
    gp0["rel_bias"] = jnp.sum(d_rel, axis=-1).transpose(2, 0, 1).reshape(NUM_BUCKETS, 12)
    gp0["final_norm_g"] = dgf[0]
    rep0 = _pack_half(gp0, 0, True)
    rparts = _gather_via_sibling(rep0, lambda ref: ref, lambda ref, s: ref.at[s, pl.ds(_REP_HALF_ROWS, _REP_HALF_ROWS)],
                                 jax.ShapeDtypeStruct((N_DEV, _REP_ROWS, LANES), F32), "gather_small_grads0",
                                 landing=landed0["merge"][0])
    return loss[0, 0], dx, list(landed0["proj"]) + list(landed0["dw_in"]), rparts


_WEIGHTS = ["norm_g", "mem_norm_g", "w_in", "b_gate", "ssm_lambda_re", "ssm_lambda_im", "ssm_log_dt", "ssm_b_re",
            "ssm_b_im", "ssm_c_re", "ssm_c_im", "ssm_d", "w_glu", "b_glu", "w_mem_kv", "w_br_ssm", "w_br_attn",
            "w_br_mem", "w_out", "rel_bias", "final_norm_g"]
_ADAM_ROWS = {"w_in": 136,"w_glu": 96, "w_mem_kv": 128, "w_br_ssm": 768, "w_br_attn": 768, "w_br_mem": 512,
              "w_out": 128}


def kernel(x, mem, norm_g, mem_norm_g, w_in, b_gate, ssm_lambda_re, ssm_lambda_im, ssm_log_dt, ssm_b_re, ssm_b_im, ssm_c_re, ssm_c_im, ssm_d, w_glu, b_glu, w_mem_kv, w_br_ssm, w_br_attn, w_br_mem, w_out, rel_bias, final_norm_g, loss_target, m_norm_g, m_mem_norm_g, m_w_in, m_b_gate, m_ssm_lambda_re, m_ssm_lambda_im, m_ssm_log_dt, m_ssm_b_re, m_ssm_b_im, m_ssm_c_re, m_ssm_c_im, m_ssm_d, m_w_glu, m_b_glu, m_w_mem_kv, m_w_br_ssm, m_w_br_attn, m_w_br_mem, m_w_out, m_rel_bias, m_final_norm_g, v_norm_g, v_mem_norm_g, v_w_in, v_b_gate, v_ssm_lambda_re, v_ssm_lambda_im, v_ssm_log_dt, v_ssm_b_re, v_ssm_b_im, v_ssm_c_re, v_ssm_c_im, v_ssm_d, v_w_glu, v_b_glu, v_w_mem_kv, v_w_br_ssm, v_w_br_attn, v_w_br_mem, v_w_out, v_rel_bias, v_final_norm_g):
    given = dict(locals())
    w = {n: given[n] for n in _WEIGHTS}
    m = {n: given["m_" + n] for n in _WEIGHTS}
    v = {n: given["v_" + n] for n in _WEIGHTS}

    turned = lambda n, a: a.swapaxes(1, 2) if n == "w_in" else a
    shards = [turned(n, w[n]).astype(BF16) for n, _, _ in _SHARDED]
    loss, dx, parts, rparts = _train_step(x[0], mem[0], loss_target[0], shards, w)
    loss = lax.psum(loss, ("x", "y", "c"))

    new = {}
    for (n, _, _), p in zip(_SHARDED, parts):
        new[n] = [turned(n, a) for a in _adamw(p, turned(n, w[n]), turned(n, m[n]), turned(n, v[n]), _ADAM_ROWS[n],
                                               f"adamw_{n}")]
    rp = [_unpack_replicated(a) for a in _adamw(rparts[:, None], _pack_replicated(w), _pack_replicated(m),
                                                _pack_replicated(v), _REP_ROWS // 4, "adamw_replicated")]
    for n, _ in _REPLICATED:
        new[n] = [rp[kind][n] for kind in range(4)]
    outs = [loss, dx[None]]
    for kind in range(4):
        outs.extend(new[n][kind] for n in _WEIGHTS)
    return tuple(outs)
```

```python
import functools
import math
from typing import Callable, NamedTuple

import jax
import jax.numpy as jnp
import numpy as np
from jax import lax
from jax.experimental import pallas as pl
from jax.experimental.pallas import tpu as pltpu

F32 = jnp.float32
BF16 = jnp.bfloat16

D_MODEL = 1024
DEPTH = 2
EPS = 1e-6
D_SSM = 768
SSM_GROUP = 16
SSM_GROUPS = 48
SSM_STATE = 64
N_STATE = SSM_GROUPS * SSM_STATE
SSM_BLOCKS = 6
D_ATTN = 768
ATTN_HEAD_DIM = 64
ATTN_GROUP_WIDTH = 256
ATTN_DILATIONS = (1, 4, 16)
ATTN_SPAN = 128
ATTN_BLOCK = 128
NUM_BUCKETS = 32
REL_MAX_DISTANCE = 2048
NEG_INF = -1e30
MEM_HEADS = 4
MEM_HEAD_DIM = 128
D_MEM = 512
N_GATES = 3 * D_MODEL
D_IN = 8704
N_DEV = 8
LANES = 128
ADAM_LR = 0.001
ADAM_B1 = 0.9
ADAM_B2 = 0.999
ADAM_EPS = 1e-08
ADAM_WD = 0.01
ADAM_STEP = 10

_OFF = {"u": 0, "z_ssm": 768, "q": 1536, "k": 2304, "v": 3072, "z_attn": 3840, "q_mem": 4608, "z_mem": 5120,
        "gates": 5632}
GATE_BLOCK = 512
ROW_TILE = 1024

NN = (((1,), (0,)), ((), ()))
NT = (((1,), (1,)), ((), ()))
TN = (((0,), (0,)), ((), ()))

VMEM_LIMIT = 56 * 1024 * 1024


def _dot(a, b, dims=NN):
    return lax.dot_general(a, b, dims, preferred_element_type=F32)


def _sigmoid(x):
    return 1.0 / (1.0 + jnp.exp(-x))


def _gelu_parts(x):
    k = math.sqrt(2.0 / math.pi)
    t = jnp.tanh(k * (x + 0.044715 * (x * x * x)))
    cdf = 0.5 * (1.0 + t)
    dcdf = 0.5 * (1.0 - t * t) * k * (1.0 + 3.0 * 0.044715 * (x * x))
    return x * cdf, cdf + x * dcdf


def _params(sem, vmem=VMEM_LIMIT):
    return pltpu.CompilerParams(dimension_semantics=sem, vmem_limit_bytes=vmem)


def _full(shape):
    return pl.BlockSpec(shape, lambda *_: (0,) * len(shape))


def _norm_proj(x, g, w, tm, tn, name, out_dtype=F32, job=None, w_turned=False):
    T, D = x.shape
    N = w.shape[0] if w_turned else w.shape[1]
    w_spec = pl.BlockSpec((tn, D), lambda i, j: (j, 0)) if w_turned else pl.BlockSpec((D, tn), lambda i, j: (0, j))
    dims = NT if w_turned else NN

    def body(x_ref, g_ref, w_ref, o_ref, h_ref, hs):
        @pl.when(pl.program_id(1) == 0)
        def _():
            xv = x_ref[...]
            r = lax.rsqrt(jnp.mean(xv * xv, axis=-1, keepdims=True) + EPS)
            hv = (xv * r * g_ref[...]).astype(BF16)
            hs[...] = hv
            h_ref[...] = hv

        o_ref[...] = _dot(hs[...], w_ref[...], dims).astype(out_dtype)

    return _pc(
        body, job, name=name, grid=(T // tm, N // tn),
        in_specs=[pl.BlockSpec((tm, D), lambda i, j: (i, 0)), _full((1, D)), w_spec],
        out_specs=[pl.BlockSpec((tm, tn), lambda i, j: (i, j)), pl.BlockSpec((tm, D), lambda i, j: (i, 0))],
        out_shape=[jax.ShapeDtypeStruct((T, N), out_dtype), jax.ShapeDtypeStruct((T, D), BF16)],
        scratch_shapes=[pltpu.VMEM((tm, D), BF16)], sem=("parallel", "arbitrary"), operands=(x, g, w))


def _mm_tn(a, b, tm, tn, tk, name, b_col=0, n=None, job=None):
    K, M = a.shape
    N = b.shape[1] if n is None else n
    nk = K // tk
    j0 = b_col // tn

    def body(a_ref, b_ref, o_ref, acc):
        k = pl.program_id(2)

        @pl.when(k == 0)
        def _():
            acc[...] = jnp.zeros_like(acc)

        acc[...] += _dot(a_ref[...].astype(BF16), b_ref[...].astype(BF16), TN)

        @pl.when(k == nk - 1)
        def _():
            o_ref[...] = acc[...].astype(BF16)

    out = _pc(
        body, job, name=name, grid=(M // tm, N // tn, nk),
        in_specs=[pl.BlockSpec((tk, tm), lambda i, j, k: (k, i)), pl.BlockSpec((tk, tn), lambda i, j, k: (k, j0 + j))],
        out_specs=[pl.BlockSpec((tm, tn), lambda i, j, k: (i, j))],
        out_shape=[jax.ShapeDtypeStruct((M, N), BF16)],
        scratch_shapes=[pltpu.VMEM((tm, tn), F32)], sem=("parallel", "parallel", "arbitrary"), operands=(a, b))
    return out[0] if job is None else (out[0][0], out[1])


def _proj_bwd(dp, w, x, g, dres, tm, tk, name, job=None, w_turned=False):
    T, N = dp.shape
    D = x.shape[1]
    nk = N // tk
    w_spec = pl.BlockSpec((tk, D), lambda i, k: (k, 0)) if w_turned else pl.BlockSpec((D, tk), lambda i, k: (0, k))
    dims = NN if w_turned else NT

    def body(dp_ref, w_ref, x_ref, g_ref, dres_ref, dx_ref, dg_ref, acc):
        i, k = pl.program_id(0), pl.program_id(1)

        @pl.when(k == 0)
        def _():
            acc[...] = jnp.zeros_like(acc)

        @pl.when((i == 0) & (k == 0))
        def _():
            dg_ref[...] = jnp.zeros_like(dg_ref)

        acc[...] += _dot(dp_ref[...], w_ref[...], dims)

        @pl.when(k == nk - 1)
        def _():
            xv = x_ref[...]
            dh = acc[...]
            r = lax.rsqrt(jnp.mean(xv * xv, axis=-1, keepdims=True) + EPS)
            xr = xv * r
            dg_ref[...] += jnp.sum(dh * xr, axis=0, keepdims=True)
            wv = dh * g_ref[...]
            dx_ref[...] = dres_ref[...] + r * (wv - xr * jnp.mean(wv * xr, axis=-1, keepdims=True))

    return _pc(
        body, job, name=name, grid=(T // tm, nk),
        in_specs=[pl.BlockSpec((tm, tk), lambda i, k: (i, k)), w_spec,
                  pl.BlockSpec((tm, D), lambda i, k: (i, 0)), _full((1, D)),
                  pl.BlockSpec((tm, D), lambda i, k: (i, 0))],
        out_specs=[pl.BlockSpec((tm, D), lambda i, k: (i, 0)), _full((1, D))],
        out_shape=[jax.ShapeDtypeStruct((T, D), F32), jax.ShapeDtypeStruct((1, D), F32)],
        scratch_shapes=[pltpu.VMEM((tm, D), F32)], sem=("arbitrary", "arbitrary"), operands=(dp, w, x, g, dres))


def _ssm_fwd(proj, bre, bim, cre, cimn, are, aim, d, wglu, bglu, tc, name, job=None):
    T = proj.shape[0]
    ucol, zcol = _OFF["u"] // D_SSM, _OFF["z_ssm"] // D_SSM

    def body(u_ref, z_ref, bre_ref, bim_ref, cre_ref, cim_ref, are_ref, aim_ref, d_ref, wg_ref, bg_ref,
             xr_ref, xi_ref, y_ref, o_ref, car_r, car_i):
        @pl.when(pl.program_id(0) == 0)
        def _():
            car_r[...] = jnp.zeros_like(car_r)
            car_i[...] = jnp.zeros_like(car_i)

        u = u_ref[...]
        ub = u.astype(BF16)
        for k in range(SSM_BLOCKS):
            uk = ub[:, 128 * k:128 * (k + 1)]
            xr_ref[:, 512 * k:512 * (k + 1)] = _dot(uk, bre_ref[k])
            xi_ref[:, 512 * k:512 * (k + 1)] = _dot(uk, bim_ref[k])
        ar, ai = are_ref[...], aim_ref[...]

        def step(t, c):
            pr, pi = c
            nr = ar * pr - ai * pi + xr_ref[pl.ds(t, 1), :]
            ni = ar * pi + ai * pr + xi_ref[pl.ds(t, 1), :]
            xr_ref[pl.ds(t, 1), :] = nr
            xi_ref[pl.ds(t, 1), :] = ni
            return nr, ni

        pr, pi = lax.fori_loop(0, tc, step, (car_r[...], car_i[...]))
        car_r[...] = pr
        car_i[...] = pi

        ys = []
        for k in range(SSM_BLOCKS):
            xrk = xr_ref[:, 512 * k:512 * (k + 1)].astype(BF16)
            xik = xi_ref[:, 512 * k:512 * (k + 1)].astype(BF16)
            ys.append(_dot(xrk, cre_ref[k]) + _dot(xik, cim_ref[k]))
        y = jnp.concatenate(ys, axis=1) + d_ref[...] * u
        y_ref[...] = y
        gl, _ = _gelu_parts(y)
        t = _dot(gl.astype(BF16), wg_ref[...]) + bg_ref[...]
        z = z_ref[...]
        o_ref[...] = (gl * _sigmoid(t) * (z * _sigmoid(z))).astype(BF16)

    return _pc(
        body, job, name=name, grid=(T // tc,),
        in_specs=[pl.BlockSpec((tc, D_SSM), lambda i: (i, ucol)), pl.BlockSpec((tc, D_SSM), lambda i: (i, zcol)),
                  _full((SSM_BLOCKS, 128, 512)), _full((SSM_BLOCKS, 128, 512)),
                  _full((SSM_BLOCKS, 512, 128)), _full((SSM_BLOCKS, 512, 128)),
                  _full((1, N_STATE)), _full((1, N_STATE)), _full((1, D_SSM)),
                  _full((D_SSM, D_SSM)), _full((1, D_SSM))],
        out_specs=[pl.BlockSpec((tc, N_STATE), lambda i: (i, 0)), pl.BlockSpec((tc, N_STATE), lambda i: (i, 0)),
                   pl.BlockSpec((tc, D_SSM), lambda i: (i, 0)), pl.BlockSpec((tc, D_SSM), lambda i: (i, 0))],
        out_shape=[jax.ShapeDtypeStruct((T, N_STATE), F32), jax.ShapeDtypeStruct((T, N_STATE), F32),
                   jax.ShapeDtypeStruct((T, D_SSM), F32), jax.ShapeDtypeStruct((T, D_SSM), BF16)],
        scratch_shapes=[pltpu.VMEM((1, N_STATE), F32), pltpu.VMEM((1, N_STATE), F32)], sem=("arbitrary",),
        operands=(proj, proj, bre, bim, cre, cimn, are, aim, d, wglu, bglu))


def _glu_bwd(do, y, proj, wglu, bglu, dproj, tm, name):
    T = y.shape[0]
    zcol = _OFF["z_ssm"] // D_SSM

    def body(do_ref, y_ref, z_ref, wg_ref, bg_ref, _, dy_ref, dz_ref, g_ref, dt_ref, db_ref):
        @pl.when(pl.program_id(0) == 0)
        def _():
            db_ref[...] = jnp.zeros_like(db_ref)

        dov = do_ref[...]
        gl, dgl = _gelu_parts(y_ref[...])
        glb = gl.astype(BF16)
        sg = _sigmoid(_dot(glb, wg_ref[...]) + bg_ref[...])
        z = z_ref[...]
        sz = _sigmoid(z)
        dz_ref[...] = (dov * (gl * sg) * (sz * (1.0 + z * (1.0 - sz)))).astype(BF16)
        dy2 = dov * (z * sz)
        dt = dy2 * gl * (sg * (1.0 - sg))
        dtb = dt.astype(BF16)
        dg = dy2 * sg + _dot(dtb, wg_ref[...], NT)
        dy_ref[...] = dg * dgl
        g_ref[...] = glb
        dt_ref[...] = dtb
        db_ref[...] += jnp.sum(dt, axis=0, keepdims=True)

    row = lambda i: (i, 0)
    return pl.pallas_call(
        body, name=name, grid=(T // tm,),
        in_specs=[pl.BlockSpec((tm, D_SSM), row), pl.BlockSpec((tm, D_SSM), row),
                  pl.BlockSpec((tm, D_SSM), lambda i: (i, zcol)), _full((D_SSM, D_SSM)), _full((1, D_SSM)),
                  pl.BlockSpec(memory_space=pl.ANY)],
        out_specs=[pl.BlockSpec((tm, D_SSM), row), pl.BlockSpec((tm, D_SSM), lambda i: (i, zcol)),
                   pl.BlockSpec((tm, D_SSM), row), pl.BlockSpec((tm, D_SSM), row), _full((1, D_SSM))],
        out_shape=[jax.ShapeDtypeStruct((T, D_SSM), F32), jax.ShapeDtypeStruct(dproj.shape, BF16),
                   jax.ShapeDtypeStruct((T, D_SSM), BF16), jax.ShapeDtypeStruct((T, D_SSM), BF16),
                   jax.ShapeDtypeStruct((1, D_SSM), F32)],
        input_output_aliases={5: 1},
        compiler_params=_params(("arbitrary",)),
    )(do, y, proj, wglu, bglu, dproj)


def _ssm_bwd(dy, proj, xr, xi, bre, bim, cre, cimn, are, aim, d, dproj, tc, name, job=None):
    T = dy.shape[0]
    nc = T // tc
    ucol = _OFF["u"] // D_SSM
    rb = tc // 8

    def body(dy_ref, u_ref, xr_ref, xi_ref, xpr_ref, xpi_ref, bre_ref, bim_ref, cre_ref, cim_ref,
             are_ref, aim_ref, d_ref, _,
             du_ref, dbre_ref, dbim_ref, dcre_ref, dcim_ref, dare_ref, daim_ref, dd_ref, gr, gi, car_r, car_i):
        i = pl.program_id(0)

        @pl.when(i == 0)
        def _():
            for ref in (car_r, car_i, dbre_ref, dbim_ref, dcre_ref, dcim_ref, dare_ref, daim_ref, dd_ref):
                ref[...] = jnp.zeros_like(ref)

        dyv = dy_ref[...]
        dyb = dyv.astype(BF16)
        u = u_ref[...]
        ub = u.astype(BF16)
        for k in range(SSM_BLOCKS):
            dk = dyb[:, 128 * k:128 * (k + 1)]
            gr[:, 512 * k:512 * (k + 1)] = _dot(dk, cre_ref[k], NT)
            gi[:, 512 * k:512 * (k + 1)] = _dot(dk, cim_ref[k], NT)
        ar, ai = are_ref[...], aim_ref[...]

        def step(s, c):
            pr, pi = c
            t = tc - 1 - s
            nr = gr[pl.ds(t, 1), :] + ar * pr + ai * pi
            ni = gi[pl.ds(t, 1), :] + ar * pi - ai * pr
            gr[pl.ds(t, 1), :] = nr
            gi[pl.ds(t, 1), :] = ni
            return nr, ni

        pr, pi = lax.fori_loop(0, tc, step, (car_r[...], car_i[...]))
        car_r[...] = pr
        car_i[...] = pi

        keep = jnp.where(i == nc - 1, 0.0, 1.0)
        row0 = lax.broadcasted_iota(jnp.int32, (tc, 1), 0) == 0
        dd_ref[...] += jnp.sum(dyv * u, axis=0, keepdims=True)
        for k in range(SSM_BLOCKS):
            sl = slice(512 * k, 512 * (k + 1))
            ch = slice(128 * k, 128 * (k + 1))
            xrk, xik, grk, gik = xr_ref[:, sl], xi_ref[:, sl], gr[:, sl], gi[:, sl]
            xsr = jnp.where(row0, xpr_ref[7:8, sl] * keep, pltpu.roll(xrk, 1, axis=0))
            xsi = jnp.where(row0, xpi_ref[7:8, sl] * keep, pltpu.roll(xik, 1, axis=0))
            dare_ref[:, sl] += jnp.sum(grk * xsr + gik * xsi, axis=0, keepdims=True)
            daim_ref[:, sl] += jnp.sum(gik * xsr - grk * xsi, axis=0, keepdims=True)
            grb, gib = grk.astype(BF16), gik.astype(BF16)
            du_ref[:, ch] = (_dot(grb, bre_ref[k], NT) + _dot(gib, bim_ref[k], NT)
                             + d_ref[:, ch] * dyv[:, ch]).astype(BF16)
            dbre_ref[k] += _dot(grb, ub[:, ch], TN)
            dbim_ref[k] += _dot(gib, ub[:, ch], TN)
            dcre_ref[k] += _dot(dyb[:, ch], xrk.astype(BF16), TN)
            dcim_ref[k] -= _dot(dyb[:, ch], xik.astype(BF16), TN)

    rev = lambda i: (nc - 1 - i, 0)
    prev = lambda i: (jnp.maximum((nc - 1 - i) * rb - 1, 0), 0)
    return _pc(
        body, job, name=name, grid=(nc,),
        in_specs=[pl.BlockSpec((tc, D_SSM), rev), pl.BlockSpec((tc, D_SSM), lambda i: (nc - 1 - i, ucol)),
                  pl.BlockSpec((tc, N_STATE), rev), pl.BlockSpec((tc, N_STATE), rev),
                  pl.BlockSpec((8, N_STATE), prev), pl.BlockSpec((8, N_STATE), prev),
                  _full((SSM_BLOCKS, 128, 512)), _full((SSM_BLOCKS, 128, 512)),
                  _full((SSM_BLOCKS, 512, 128)), _full((SSM_BLOCKS, 512, 128)),
                  _full((1, N_STATE)), _full((1, N_STATE)), _full((1, D_SSM)), pl.BlockSpec(memory_space=pl.ANY)],
        out_specs=[pl.BlockSpec((tc, D_SSM), lambda i: (nc - 1 - i, ucol)),
                   _full((SSM_BLOCKS, 512, 128)), _full((SSM_BLOCKS, 512, 128)),
                   _full((SSM_BLOCKS, 128, 512)), _full((SSM_BLOCKS, 128, 512)),
                   _full((1, N_STATE)), _full((1, N_STATE)), _full((1, D_SSM))],
        out_shape=[jax.ShapeDtypeStruct(dproj.shape, BF16),
                   jax.ShapeDtypeStruct((SSM_BLOCKS, 512, 128), F32), jax.ShapeDtypeStruct((SSM_BLOCKS, 512, 128), F32),
                   jax.ShapeDtypeStruct((SSM_BLOCKS, 128, 512), F32), jax.ShapeDtypeStruct((SSM_BLOCKS, 128, 512), F32),
                   jax.ShapeDtypeStruct((1, N_STATE), F32), jax.ShapeDtypeStruct((1, N_STATE), F32),
                   jax.ShapeDtypeStruct((1, D_SSM), F32)],
        scratch_shapes=[pltpu.VMEM((tc, N_STATE), F32), pltpu.VMEM((tc, N_STATE), F32),
                        pltpu.VMEM((1, N_STATE), F32), pltpu.VMEM((1, N_STATE), F32)], sem=("arbitrary",),
        operands=(dy, proj, xr, xi, xr, xi, bre, bim, cre, cimn, are, aim, d, dproj), aliases={13: 0})


def _rel_bucket(dist):
    n = jnp.maximum(dist, 0)
    max_exact = NUM_BUCKETS // 2
    n_f = jnp.maximum(n, 1).astype(F32)
    large = max_exact + (jnp.log(n_f / max_exact) / math.log(REL_MAX_DISTANCE / max_exact)
                         * (NUM_BUCKETS - max_exact)).astype(jnp.int32)
    large = jnp.minimum(large, NUM_BUCKETS - 1)
    return jnp.where(n < max_exact, n, large)


def _bucket_tables():
    qi = jnp.arange(ATTN_BLOCK)[:, None]
    kj = jnp.arange(2 * ATTN_BLOCK)[None, :]
    delta = jnp.maximum(ATTN_BLOCK + qi - kj, 0)
    return jnp.stack([_rel_bucket(delta * r) for r in ATTN_DILATIONS]).astype(jnp.int32)


def _bias_tables(rel_bias, buckets, name):
    def body(tab_ref, bk_ref, o_ref):
        g = pl.program_id(0)
        bk = bk_ref[...]
        qi = lax.broadcasted_iota(jnp.int32, bk.shape, 0)
        kj = lax.broadcasted_iota(jnp.int32, bk.shape, 1)
        delta = ATTN_BLOCK + qi - kj
        band = (delta >= 0) & (delta <= ATTN_SPAN)
        accs = [jnp.zeros(bk.shape, F32) for _ in range(4)]
        for b in range(NUM_BUCKETS):
            hit = bk == b
            for h in range(4):
                accs[h] = jnp.where(hit, tab_ref[b, 4 * g + h], accs[h])
        for h in range(4):
            o_ref[h] = jnp.where(band, accs[h], NEG_INF)

    return pl.pallas_call(
        body, name=name, grid=(3,),
        in_specs=[pl.BlockSpec(memory_space=pltpu.SMEM),
                  pl.BlockSpec((None, ATTN_BLOCK, 2 * ATTN_BLOCK), lambda g: (g, 0, 0))],
        out_specs=pl.BlockSpec((None, 4, ATTN_BLOCK, 2 * ATTN_BLOCK), lambda g: (g, 0, 0, 0)),
        out_shape=jax.ShapeDtypeStruct((3, 4, ATTN_BLOCK, 2 * ATTN_BLOCK), F32),
        compiler_params=_params(("parallel",)),
    )(rel_bias, buckets)


def _bias_grad(db0, db1, buckets, name):
    def body(a_ref, b_ref, bk_ref, o_ref):
        bk = bk_ref[...]
        for h in range(4):
            dv = a_ref[h] + b_ref[h]
            for b in range(NUM_BUCKETS):
                o_ref[h, b:b + 1, :] = jnp.sum(jnp.where(bk == b, dv, 0.0), axis=0, keepdims=True)

    tab = pl.BlockSpec((None, 4, ATTN_BLOCK, 2 * ATTN_BLOCK), lambda g: (g, 0, 0, 0))
    return pl.pallas_call(
        body, name=name, grid=(3,),
        in_specs=[tab, tab, pl.BlockSpec((None, ATTN_BLOCK, 2 * ATTN_BLOCK), lambda g: (g, 0, 0))],
        out_specs=pl.BlockSpec((None, 4, NUM_BUCKETS, 2 * ATTN_BLOCK), lambda g: (g, 0, 0, 0)),
        out_shape=jax.ShapeDtypeStruct((3, 4, NUM_BUCKETS, 2 * ATTN_BLOCK), F32),
        compiler_params=_params(("parallel",)),
    )(db0, db1, buckets)


_ATTN_SUB = {1: 4, 4: 1, 16: 1}
_UNROLL = 4


def _unit_rows(j, s, r):
    start = j * ATTN_BLOCK * r + s
    return pl.ds(start, ATTN_BLOCK, stride=r) if r > 1 else pl.ds(start, ATTN_BLOCK)


def _for_units(r, nsub, fn, after):
    if r * nsub <= _UNROLL:
        after([fn(j, s) for j in range(nsub) for s in range(r)])
    else:
        def four(i, c):
            after([fn(0, _UNROLL * i + k) for k in range(_UNROLL)])
            return c

        lax.fori_loop(0, r // _UNROLL, four, 0)


def _attn_cols(g):
    return tuple((_OFF[n] + ATTN_GROUP_WIDTH * g) // LANES for n in ("q", "k", "v"))


def _attn_fwd(proj, bias, g, name, job=None):
    r = ATTN_DILATIONS[g]
    nsub = _ATTN_SUB[r]
    T = proj.shape[0]
    sub = ATTN_BLOCK * r
    tb = sub * nsub
    qc, kc, vc = _attn_cols(g)
    scale = ATTN_HEAD_DIM ** -0.5

    def body(q_ref, kc_ref, kp_ref, vc_ref, vp_ref, bias_ref, o_ref, lse_ref):
        lane = lax.broadcasted_iota(jnp.int32, (ATTN_BLOCK, LANES), 1)
        kj = lax.broadcasted_iota(jnp.int32, (ATTN_BLOCK, 2 * ATTN_BLOCK), 1)
        dead = (pl.program_id(0) == 0) & (kj < ATTN_BLOCK)

        def one(j, s):
            rows = _unit_rows(j, s, r)
            before = _unit_rows(max(j - 1, 0), s, r)
            k_before = kc_ref[before, :] if j else kp_ref[before, :]
            v_before = vc_ref[before, :] if j else vp_ref[before, :]
            q = q_ref[rows, :]
            kcat = jnp.concatenate([k_before, kc_ref[rows, :]], axis=0).astype(BF16)
            vcat = jnp.concatenate([v_before, vc_ref[rows, :]], axis=0).astype(BF16)
            o_acc = jnp.zeros((ATTN_BLOCK, LANES), F32)
            l_acc = jnp.zeros((ATTN_BLOCK, LANES), F32)
            for hh in range(2):
                mine = (lane >= ATTN_HEAD_DIM) if hh else (lane < ATTN_HEAD_DIM)
                qm = jnp.where(mine, q, 0.0).astype(BF16)
                sc = _dot(qm, kcat, NT) * scale + bias_ref[hh]
                if j == 0:
                    sc = jnp.where(dead, NEG_INF, sc)
                m = jnp.max(sc, axis=-1, keepdims=True)
                p = jnp.exp(sc - m)
                l = jnp.sum(p, axis=-1, keepdims=True)
                o_acc = jnp.where(mine, _dot((p / l).astype(BF16), vcat), o_acc)
                l_acc = jnp.where(mine, m + jnp.log(l), l_acc)
            o_ref[rows, :] = o_acc
            lse_ref[rows, :] = l_acc

        _for_units(r, nsub, one, lambda results: None)

    cur = lambda c: pl.BlockSpec((tb, LANES), lambda b, p: (b, c + p))
    prev = lambda c: pl.BlockSpec((sub, LANES), lambda b, p: (jnp.maximum(b * nsub - 1, 0), c + p))
    out = pl.BlockSpec((tb, LANES), lambda b, p: (b, p))
    return _pc(
        body, job, name=name, grid=(T // tb, 2),
        in_specs=[cur(qc), cur(kc), prev(kc), cur(vc), prev(vc),
                  pl.BlockSpec((2, ATTN_BLOCK, 2 * ATTN_BLOCK), lambda b, p: (p, 0, 0))],
        out_specs=[out, out],
        out_shape=[jax.ShapeDtypeStruct((T, ATTN_GROUP_WIDTH), F32), jax.ShapeDtypeStruct((T, ATTN_GROUP_WIDTH), F32)],
        scratch_shapes=[], sem=("parallel", "parallel"), operands=(proj, proj, proj, proj, proj, bias))


def _attn_bwd(proj, do, corr, lse, bias, dproj, g, name):
    r = ATTN_DILATIONS[g]
    nsub = _ATTN_SUB[r]
    T = proj.shape[0]
    sub = ATTN_BLOCK * r
    tb = sub * nsub
    nb = T // tb
    qc, kc, vc = _attn_cols(g)
    dc = ATTN_GROUP_WIDTH * g // LANES
    scale = ATTN_HEAD_DIM ** -0.5

    def body(q_ref, kc_ref, kp_ref, vc_ref, vp_ref, do_ref, corr_ref, lse_ref, bias_ref, _,
             dproj_ref, db_ref, dq_s, dkc_s, dkp_s, dvc_s, dvp_s, kacc, vacc, stage, stage_sems):
        p, b = pl.program_id(0), pl.program_id(1)

        def to_dproj(e, slot, block, col):
            rows = pl.ds(pl.multiple_of(block * tb, tb), tb)
            cols = pl.ds(pl.multiple_of((col + p) * LANES, LANES), LANES)
            return pltpu.make_async_copy(stage.at[e, slot], dproj_ref.at[rows, cols], stage_sems.at[e, slot])

        def emit(e, block, col, value):
            count = p * nb + block
            slot = count % 2

            @pl.when(count >= 2)
            def _():
                to_dproj(e, slot, 0, col).wait()

            stage[e, slot] = value.astype(BF16)
            to_dproj(e, slot, block, col).start()

        def emit_keys(block):
            emit(1, block, kc, kacc[...])
            emit(2, block, vc, vacc[...])

        @pl.when(b == 0)
        def _():
            db_ref[...] = jnp.zeros_like(db_ref)
            kacc[...] = jnp.zeros_like(kacc)
            vacc[...] = jnp.zeros_like(vacc)

        @pl.when(b == nb)
        def _():
            emit_keys(nb - 1)

        @pl.when((b == nb) & (p == 1))
        def _():
            for e, col in enumerate((qc, kc, vc)):
                for slot in range(2):
                    to_dproj(e, slot, 0, col).wait()

        @pl.when(b < nb)
        def _():
            lane = lax.broadcasted_iota(jnp.int32, (ATTN_BLOCK, LANES), 1)
            kj = lax.broadcasted_iota(jnp.int32, (ATTN_BLOCK, 2 * ATTN_BLOCK), 1)
            dead = (b == 0) & (kj < ATTN_BLOCK)

            def one(j, s):
                rows = _unit_rows(j, s, r)
                before = _unit_rows(max(j - 1, 0), s, r)
                k_before = kc_ref[before, :] if j else kp_ref[before, :]
                v_before = vc_ref[before, :] if j else vp_ref[before, :]
                q = q_ref[rows, :]
                kcat = jnp.concatenate([k_before, kc_ref[rows, :]], axis=0).astype(BF16)
                vcat = jnp.concatenate([v_before, vc_ref[rows, :]], axis=0).astype(BF16)
                dov, corrv, lsev = do_ref[rows, :], corr_ref[rows, :], lse_ref[rows, :]
                dq_acc = jnp.zeros((ATTN_BLOCK, LANES), F32)
                dk_acc = jnp.zeros((2 * ATTN_BLOCK, LANES), F32)
                dv_acc = jnp.zeros((2 * ATTN_BLOCK, LANES), F32)
                dss = []
                for hh in range(2):
                    mine = (lane >= ATTN_HEAD_DIM) if hh else (lane < ATTN_HEAD_DIM)
                    col = slice(ATTN_HEAD_DIM * hh, ATTN_HEAD_DIM * hh + 1)
                    qm = jnp.where(mine, q, 0.0).astype(BF16)
                    dom = jnp.where(mine, dov, 0.0).astype(BF16)
                    sc = _dot(qm, kcat, NT) * scale + bias_ref[hh]
                    if j == 0:
                        sc = jnp.where(dead, NEG_INF, sc)
                    p = jnp.exp(sc - lsev[:, col])
                    ds = p * (_dot(dom, vcat, NT) - corrv[:, col])
                    dss.append(ds)
                    dsb = ds.astype(BF16)
                    dq_acc = jnp.where(mine, _dot(dsb, kcat) * scale, dq_acc)
                    dk_acc += _dot(dsb, qm, TN) * scale
                    dv_acc += _dot(p.astype(BF16), dom, TN)
                dq_s[rows, :] = dq_acc
                dkp_s[rows, :] = dk_acc[:ATTN_BLOCK]
                dkc_s[rows, :] = dk_acc[ATTN_BLOCK:]
                dvp_s[rows, :] = dv_acc[:ATTN_BLOCK]
                dvc_s[rows, :] = dv_acc[ATTN_BLOCK:]
                return dss

            def add_bias_grads(results):
                for hh in range(2):
                    db_ref[hh] += functools.reduce(lambda x, y: x + y, [dss[hh] for dss in results])

            _for_units(r, nsub, one, add_bias_grads)
            emit(0, b, qc, dq_s[...])
            tail = slice((nsub - 1) * sub, nsub * sub)
            kacc[tail, :] += dkp_s[0:sub, :]
            vacc[tail, :] += dvp_s[0:sub, :]

            @pl.when(b >= 1)
            def _():
                emit_keys(b - 1)

            for acc, before_s, cur_s in ((kacc, dkp_s, dkc_s), (vacc, dvp_s, dvc_s)):
                acc[...] = cur_s[...]
                for j in range(nsub - 1):
                    acc[j * sub:(j + 1) * sub, :] += before_s[(j + 1) * sub:(j + 2) * sub, :]

    last = nb - 1
    blk = (tb, LANES)
    cur = lambda c: pl.BlockSpec(blk, lambda p, b: (jnp.minimum(b, last), c + p))
    before = lambda c: pl.BlockSpec((sub, LANES), lambda p, b: (jnp.clip(b * nsub - 1, 0, nb * nsub - 1), c + p))
    tab = pl.BlockSpec((2, ATTN_BLOCK, 2 * ATTN_BLOCK), lambda p, b: (p, 0, 0))
    hbm = pl.BlockSpec(memory_space=pl.ANY)
    return pl.pallas_call(
        body, name=name, grid=(2, nb + 1),
        in_specs=[cur(qc), cur(kc), before(kc), cur(vc), before(vc), cur(dc), cur(dc), cur(0), tab, hbm],
        out_specs=[hbm, tab],
        out_shape=[jax.ShapeDtypeStruct(dproj.shape, BF16), jax.ShapeDtypeStruct((4, ATTN_BLOCK, 2 * ATTN_BLOCK), F32)],
        input_output_aliases={9: 0},
        scratch_shapes=[pltpu.VMEM(blk, F32)] * 7 + [pltpu.VMEM((3, 2) + blk, BF16), pltpu.SemaphoreType.DMA((3, 2))],
        compiler_params=_params(("arbitrary", "arbitrary")),
    )(proj, proj, proj, proj, proj, do, corr, lse, bias, dproj)


def _mix_weights(lses):
    m = jnp.maximum(jnp.maximum(lses[0], lses[1]), lses[2])
    es = [jnp.exp(l - m) for l in lses]
    inv = 1.0 / (es[0] + es[1] + es[2])
    return jnp.concatenate([e * inv for e in es], axis=1)


def _attn_mix(os, lses, proj, tm, name):
    T = proj.shape[0]
    zcol = _OFF["z_attn"] // D_ATTN

    def body(o0, o1, o2, l0, l1, l2, z_ref, out_ref):
        z = z_ref[...]
        o = jnp.concatenate([o0[...], o1[...], o2[...]], axis=1)
        alpha = _mix_weights([l0[...], l1[...], l2[...]])
        out_ref[...] = (o * alpha * (z * _sigmoid(z))).astype(BF16)

    row = lambda i: (i, 0)
    grp = pl.BlockSpec((tm, ATTN_GROUP_WIDTH), row)
    return pl.pallas_call(
        body, name=name, grid=(T // tm,),
        in_specs=[grp] * 6 + [pl.BlockSpec((tm, D_ATTN), lambda i: (i, zcol))],
        out_specs=pl.BlockSpec((tm, D_ATTN), row),
        out_shape=jax.ShapeDtypeStruct((T, D_ATTN), BF16),
        compiler_params=_params(("parallel",)),
    )(*os, *lses, proj)


def _attn_mix_bwd(d, os, lses, proj, dproj, tm, name):
    T = proj.shape[0]
    zcol = _OFF["z_attn"] // D_ATTN

    def body(d_ref, o0, o1, o2, l0, l1, l2, z_ref, _, do_ref, corr_ref, dz_ref):
        dv, z = d_ref[...], z_ref[...]
        ov = jnp.concatenate([o0[...], o1[...], o2[...]], axis=1)
        alpha = _mix_weights([l0[...], l1[...], l2[...]])
        sz = _sigmoid(z)
        oc = ov * alpha
        dz_ref[...] = (dv * oc * (sz * (1.0 + z * (1.0 - sz)))).astype(BF16)
        doc = dv * (z * sz)
        do_ref[...] = doc * alpha
        pr = doc * oc
        p3 = pr[:, 0:256] + pr[:, 256:512] + pr[:, 512:768]
        li = lax.broadcasted_iota(jnp.int32, (256, 256), 0) // ATTN_HEAD_DIM
        lj = lax.broadcasted_iota(jnp.int32, (256, 256), 1) // ATTN_HEAD_DIM
        ones = jnp.where(li == lj, 1.0, 0.0).astype(F32)
        s = lax.dot_general(p3, ones, NN, precision=lax.Precision.HIGHEST, preferred_element_type=F32)
        corr_ref[...] = alpha * jnp.concatenate([s, s, s], axis=1)

    row = lambda i: (i, 0)
    grp = pl.BlockSpec((tm, ATTN_GROUP_WIDTH), row)
    return pl.pallas_call(
        body, name=name, grid=(T // tm,),
        in_specs=[pl.BlockSpec((tm, D_ATTN), row)] + [grp] * 6 + [pl.BlockSpec((tm, D_ATTN), lambda i: (i, zcol)),
                                                                    pl.BlockSpec(memory_space=pl.ANY)],
        out_specs=[pl.BlockSpec((tm, D_ATTN), row)] * 2 + [pl.BlockSpec((tm, D_ATTN), lambda i: (i, zcol))],
        out_shape=[jax.ShapeDtypeStruct((T, D_ATTN), F32), jax.ShapeDtypeStruct((T, D_ATTN), F32),
                   jax.ShapeDtypeStruct(dproj.shape, BF16)],
        input_output_aliases={8: 2},
        compiler_params=_params(("parallel",)),
    )(d, *os, *lses, proj, dproj)


def _mem_probs(q_ref, kv_ref, h):
    hs = slice(MEM_HEAD_DIM * h, MEM_HEAD_DIM * (h + 1))
    qh = q_ref[:, hs].astype(BF16)
    kh = kv_ref[:, hs]
    vh = kv_ref[:, D_MEM + MEM_HEAD_DIM * h:D_MEM + MEM_HEAD_DIM * (h + 1)]
    s = _dot(qh, kh, NT) * (MEM_HEAD_DIM ** -0.5)
    p = jnp.exp(s - jnp.max(s, axis=-1, keepdims=True))
    pn = p / jnp.sum(p, axis=-1, keepdims=True)
    return qh, kh, vh, pn


def _mem_fwd(proj, kv, tm, name):
    T = proj.shape[0]
    M = kv.shape[0]
    qcol, zcol = _OFF["q_mem"] // D_MEM, _OFF["z_mem"] // D_MEM

    def body(q_ref, z_ref, kv_ref, o_ref):
        outs = []
        for h in range(MEM_HEADS):
            _, _, vh, pn = _mem_probs(q_ref, kv_ref, h)
            outs.append(_dot(pn.astype(BF16), vh))
        z = z_ref[...]
        o_ref[...] = (jnp.concatenate(outs, axis=1) * (z * _sigmoid(z))).astype(BF16)

    return pl.pallas_call(
        body, name=name, grid=(T // tm,),
        in_specs=[pl.BlockSpec((tm, D_MEM), lambda i: (i, qcol)), pl.BlockSpec((tm, D_MEM), lambda i: (i, zcol)),
                  _full((M, 2 * D_MEM))],
        out_specs=pl.BlockSpec((tm, D_MEM), lambda i: (i, 0)),
        out_shape=jax.ShapeDtypeStruct((T, D_MEM), BF16),
        compiler_params=_params(("parallel",)),
    )(proj, proj, kv)


def _mem_bwd(d, proj, kv, tm, name):
    T = proj.shape[0]
    M = kv.shape[0]
    qcol, zcol = _OFF["q_mem"] // D_MEM, _OFF["z_mem"] // D_MEM

    def body(d_ref, q_ref, z_ref, kv_ref, dq_ref, dz_ref, dkv_ref):
        @pl.when(pl.program_id(0) == 0)
        def _():
            dkv_ref[...] = jnp.zeros_like(dkv_ref)

        z = z_ref[...]
        sz = _sigmoid(z)
        dv = d_ref[...]
        dov = dv * (z * sz)
        scale = MEM_HEAD_DIM ** -0.5
        outs, dqs = [], []
        for h in range(MEM_HEADS):
            hs = slice(MEM_HEAD_DIM * h, MEM_HEAD_DIM * (h + 1))
            qh, kh, vh, pn = _mem_probs(q_ref, kv_ref, h)
            pnb = pn.astype(BF16)
            oh = _dot(pnb, vh)
            outs.append(oh)
            doh = dov[:, hs]
            dohb = doh.astype(BF16)
            dp = _dot(dohb, vh, NT)
            ds = pn * (dp - jnp.sum(doh * oh, axis=-1, keepdims=True))
            dsb = ds.astype(BF16)
            dqs.append(_dot(dsb, kh) * scale)
            dkv_ref[:, hs] += _dot(dsb, qh, TN) * scale
            vs = slice(D_MEM + MEM_HEAD_DIM * h, D_MEM + MEM_HEAD_DIM * (h + 1))
            dkv_ref[:, vs] += _dot(pnb, dohb, TN)
        dq_ref[...] = jnp.concatenate(dqs, axis=1).astype(BF16)
        dz_ref[...] = (dv * jnp.concatenate(outs, axis=1) * (sz * (1.0 + z * (1.0 - sz)))).astype(BF16)

    row = lambda i: (i, 0)
    return pl.pallas_call(
        body, name=name, grid=(T // tm,),
        in_specs=[pl.BlockSpec((tm, D_MEM), row), pl.BlockSpec((tm, D_MEM), lambda i: (i, qcol)),
                  pl.BlockSpec((tm, D_MEM), lambda i: (i, zcol)), _full((M, 2 * D_MEM))],
        out_specs=[pl.BlockSpec((tm, D_MEM), row), pl.BlockSpec((tm, D_MEM), row), _full((M, 2 * D_MEM))],
        out_shape=[jax.ShapeDtypeStruct((T, D_MEM), BF16), jax.ShapeDtypeStruct((T, D_MEM), BF16),
                   jax.ShapeDtypeStruct((M, 2 * D_MEM), F32)],
        compiler_params=_params(("arbitrary",)),
    )(d, proj, proj, kv)


def _branches_and_gates(os_ref, oa_ref, om_ref, gl_refs, bg_ref, ws_ref, wa_ref, wm_ref):
    outs = (_dot(os_ref[...], ws_ref[...]), _dot(oa_ref[...], wa_ref[...]), _dot(om_ref[...], wm_ref[...]))
    gates = tuple(_sigmoid(jnp.concatenate([gl_refs[2 * k][...], gl_refs[2 * k + 1][...]], axis=1)
                           + bg_ref[:, D_MODEL * k:D_MODEL * (k + 1)]) for k in range(3))
    return outs, gates


def _merge_specs(tm):
    row = lambda i: (i, 0)
    first = _OFF["gates"] // GATE_BLOCK
    gate = [pl.BlockSpec((tm, GATE_BLOCK), (lambda i, k=k: (i, first + k))) for k in range(N_GATES // GATE_BLOCK)]
    return ([pl.BlockSpec((tm, D_SSM), row), pl.BlockSpec((tm, D_ATTN), row), pl.BlockSpec((tm, D_MEM), row)] + gate
            + [_full((1, N_GATES)), _full((D_SSM, D_MODEL)), _full((D_ATTN, D_MODEL)), _full((D_MEM, D_MODEL)),
               _full((D_MODEL, D_MODEL))])


def _merge_fwd(x, o_ssm, o_attn, o_mem, proj, bg, ws, wa, wm, wo, tm, name):
    T = x.shape[0]

    def body(os_ref, oa_ref, om_ref, g0, g1, g2, g3, g4, g5, bg_ref, ws_ref, wa_ref, wm_ref, wo_ref, x_ref,
             xo_ref, mg_ref):
        outs, gates = _branches_and_gates(os_ref, oa_ref, om_ref, (g0, g1, g2, g3, g4, g5), bg_ref, ws_ref, wa_ref,
                                          wm_ref)
        merged = (gates[0] * outs[0] + gates[1] * outs[1] + gates[2] * outs[2]).astype(BF16)
        mg_ref[...] = merged
        xo_ref[...] = x_ref[...] + _dot(merged, wo_ref[...])

    row = lambda i: (i, 0)
    return pl.pallas_call(
        body, name=name, grid=(T // tm,),
        in_specs=_merge_specs(tm) + [pl.BlockSpec((tm, D_MODEL), row)],
        out_specs=[pl.BlockSpec((tm, D_MODEL), row), pl.BlockSpec((tm, D_MODEL), row)],
        out_shape=[jax.ShapeDtypeStruct((T, D_MODEL), F32), jax.ShapeDtypeStruct((T, D_MODEL), BF16)],
        compiler_params=_params(("parallel",)),
    )(o_ssm, o_attn, o_mem, *([proj] * (N_GATES // GATE_BLOCK)), bg, ws, wa, wm, wo, x)


def _merge_bwd(dx, o_ssm, o_attn, o_mem, proj, bg, ws, wa, wm, wo, tm, name, job=None):
    T = dx.shape[0]

    n = T // tm

    def body(os_ref, oa_ref, om_ref, g0, g1, g2, g3, g4, g5, bg_ref, ws_ref, wa_ref, wm_ref, wo_ref, dx_ref,
             dproj_ref, db_ref, dos_ref, doa_ref, dom_ref, dbg_ref, dgl_buf, dgl_sems):
        i = pl.program_id(0)
        slot = i % 2

        def to_dproj(s, row0):
            return pltpu.make_async_copy(dgl_buf.at[s], dproj_ref.at[pl.ds(row0, tm), pl.ds(_OFF["gates"], N_GATES)],
                                         dgl_sems.at[s])

        @pl.when(i == 0)
        def _():
            dbg_ref[...] = jnp.zeros_like(dbg_ref)

        @pl.when(i >= 2)
        def _():
            to_dproj(slot, 0).wait()

        outs, gates = _branches_and_gates(os_ref, oa_ref, om_ref, (g0, g1, g2, g3, g4, g5), bg_ref, ws_ref, wa_ref,
                                          wm_ref)
        dm = _dot(dx_ref[...].astype(BF16), wo_ref[...], NT)
        w_refs = (ws_ref, wa_ref, wm_ref)
        do_refs = (dos_ref, doa_ref, dom_ref)
        for k in range(3):
            cols = slice(D_MODEL * k, D_MODEL * (k + 1))
            dgl = dm * outs[k] * (gates[k] * (1.0 - gates[k]))
            dgl_buf[slot, :, cols] = dgl.astype(BF16)
            dbg_ref[:, cols] += jnp.sum(dgl, axis=0, keepdims=True)
            dbk = (dm * gates[k]).astype(BF16)
            db_ref[:, cols] = dbk
            do_refs[k][...] = _dot(dbk, w_refs[k][...], NT)
        to_dproj(slot, pl.multiple_of(i * tm, tm)).start()

        @pl.when(i == n - 1)
        def _():
            for s in range(min(2, n)):
                to_dproj(s, 0).wait()

    row = lambda i: (i, 0)
    return _pc(
        body, job, name=name, grid=(n,),
        in_specs=_merge_specs(tm) + [pl.BlockSpec((tm, D_MODEL), row)],
        out_specs=[pl.BlockSpec(memory_space=pl.ANY), pl.BlockSpec((tm, N_GATES), row), pl.BlockSpec((tm, D_SSM), row),
                   pl.BlockSpec((tm, D_ATTN), row), pl.BlockSpec((tm, D_MEM), row), _full((1, N_GATES))],
        out_shape=[jax.ShapeDtypeStruct((T, D_IN), BF16), jax.ShapeDtypeStruct((T, N_GATES), BF16),
                   jax.ShapeDtypeStruct((T, D_SSM), F32), jax.ShapeDtypeStruct((T, D_ATTN), F32),
                   jax.ShapeDtypeStruct((T, D_MEM), F32), jax.ShapeDtypeStruct((1, N_GATES), F32)],
        scratch_shapes=[pltpu.VMEM((2, tm, N_GATES), BF16), pltpu.SemaphoreType.DMA((2,))], sem=("arbitrary",),
        operands=(o_ssm, o_attn, o_mem, *([proj] * (N_GATES // GATE_BLOCK)), bg, ws, wa, wm, wo, dx))


def _loss_head(x, g, target, tm, name):
    T, D = x.shape

    def body(x_ref, g_ref, t_ref, loss_ref, dx_ref, dg_ref):
        @pl.when(pl.program_id(0) == 0)
        def _():
            loss_ref[...] = jnp.zeros_like(loss_ref)
            dg_ref[...] = jnp.zeros_like(dg_ref)

        xv = x_ref[...]
        r = lax.rsqrt(jnp.mean(xv * xv, axis=-1, keepdims=True) + EPS)
        xr = xv * r
        err = xr * g_ref[...] - t_ref[...]
        loss_ref[...] += 0.5 * jnp.sum(jnp.mean(err * err, axis=-1, keepdims=True), axis=0, keepdims=True)
        dy = err * (1.0 / D)
        dg_ref[...] += jnp.sum(dy * xr, axis=0, keepdims=True)
        wv = dy * g_ref[...]
        dx_ref[...] = r * (wv - xr * jnp.mean(wv * xr, axis=-1, keepdims=True))

    row = lambda i: (i, 0)
    return pl.pallas_call(
        body, name=name, grid=(T // tm,),
        in_specs=[pl.BlockSpec((tm, D), row), _full((1, D)), pl.BlockSpec((tm, D), row)],
        out_specs=[_full((1, 128)), pl.BlockSpec((tm, D), row), _full((1, D))],
        out_shape=[jax.ShapeDtypeStruct((1, 128), F32), jax.ShapeDtypeStruct((T, D), F32),
                   jax.ShapeDtypeStruct((1, D), F32)],
        compiler_params=_params(("arbitrary",)),
    )(x, g, target)


def _adamw(parts, w, m, v, tr, name):
    L, R, C = w.shape

    def body(p_ref, w_ref, m_ref, v_ref, g_ref, d_ref, mo_ref, vo_ref):
        g = p_ref[0].astype(F32)
        for s in range(1, N_DEV):
            g = g + p_ref[s].astype(F32)
        mn = ADAM_B1 * m_ref[...] + (1.0 - ADAM_B1) * g
        vn = ADAM_B2 * v_ref[...] + (1.0 - ADAM_B2) * (g * g)
        m_hat = mn / (1.0 - ADAM_B1 ** ADAM_STEP)
        v_hat = vn / (1.0 - ADAM_B2 ** ADAM_STEP)
        g_ref[...] = g
        d_ref[...] = -ADAM_LR * (m_hat / (jnp.sqrt(v_hat) + ADAM_EPS) + ADAM_WD * w_ref[...])
        mo_ref[...] = mn
        vo_ref[...] = vn

    one = pl.BlockSpec((None, tr, C), lambda l, i: (l, i, 0))
    return pl.pallas_call(
        body, name=name, grid=(L, R // tr),
        in_specs=[pl.BlockSpec((N_DEV, None, tr, C), lambda l, i: (0, l, i, 0)), one, one, one],
        out_specs=[one] * 4,
        out_shape=[jax.ShapeDtypeStruct((L, R, C), F32)] * 4,
        compiler_params=_params(("parallel", "parallel")),
    )(parts, w, m, v)


_SHARDED = (("w_in", (1088, 1024), 1), ("w_glu", (96, 768), 0), ("w_mem_kv", (128, 1024), 0),
            ("w_br_ssm", (768, 128), 1), ("w_br_attn", (768, 128), 1), ("w_br_mem", (512, 128), 1),
            ("w_out", (128, 1024), 0))
_W_IN = 0
_SMALL = tuple(range(1, len(_SHARDED)))


class _Job(NamedTuple):
    ins: list
    out_shape: list
    aliases: dict
    pairs: Callable
    n: int


def _peers():
    x, y, c = lax.axis_index("x"), lax.axis_index("y"), lax.axis_index("c")
    me = 4 * x + 2 * y + c
    out = []
    for k in range(1, N_DEV):
        px = 1 - x if k & 4 else x
        py = 1 - y if k & 2 else y
        pc = 1 - c if k & 1 else c
        out.append(((px, py, pc), 4 * px + 2 * py + pc))
    return me, out


def _copies(pairs, send_sems, recv_sems, local_sems, arrivals):
    me, peers = _peers()
    local = [pltpu.make_async_copy(src(me), dst(me), local_sems.at[j]) for j, (src, dst) in enumerate(pairs)]
    sends, recvs = [], []
    for k, (peer, lin) in enumerate(peers):
        for j, (src, dst) in enumerate(pairs):
            for to, out in ((dst(me), sends), (dst(lin), recvs)):
                if out is sends or arrivals:
                    out.append(pltpu.make_async_remote_copy(
                        src_ref=src(lin), dst_ref=to, send_sem=send_sems.at[j, k], recv_sem=recv_sems.at[j, k],
                        device_id=peer, device_id_type=pl.DeviceIdType.MESH))
    return local, sends, recvs


def _start_copies(pairs, *sems):
    local, sends, _ = _copies(pairs, *sems, arrivals=False)
    for cp in local + sends:
        cp.start()


def _wait_copies(pairs, *sems):
    local, sends, recvs = _copies(pairs, *sems, arrivals=True)
    for cp in recvs:
        cp.wait_recv()
    for cp in sends:
        cp.wait_send()
    for cp in local:
        cp.wait()


def _job_scratch(job):
    return [pltpu.SemaphoreType.DMA((job.n, N_DEV - 1)), pltpu.SemaphoreType.DMA((job.n, N_DEV - 1)),
            pltpu.SemaphoreType.DMA((job.n,))]


def _pc(body, job, *, name, grid, in_specs, out_specs, out_shape, scratch_shapes, sem, operands, aliases=None):
    aliases = aliases or {}
    if job is None:
        return pl.pallas_call(body, name=name, grid=grid, in_specs=in_specs, out_specs=out_specs, out_shape=out_shape,
                              scratch_shapes=scratch_shapes, input_output_aliases=aliases,
                              compiler_params=_params(sem))(*operands)
    a = len(in_specs)
    b = a + len(job.ins)
    c = b + len(out_shape)
    d = c + len(job.out_shape)
    e = d + len(scratch_shapes)

    def carried(*refs):
        pairs = job.pairs(refs[a:b], refs[c:d])
        ids = [pl.program_id(k) for k in range(len(grid))]
        first = functools.reduce(jnp.logical_and, [i == 0 for i in ids])
        last = functools.reduce(jnp.logical_and, [i == n - 1 for i, n in zip(ids, grid)])

        @pl.when(first)
        def _():
            _start_copies(pairs, *refs[e:])

        body(*refs[:a], *refs[b:c], *refs[d:e])

        @pl.when(last)
        def _():
            _wait_copies(pairs, *refs[e:])

    hbm = pl.BlockSpec(memory_space=pl.ANY)
    outs = pl.pallas_call(
        carried, name=name, grid=grid,
        in_specs=list(in_specs) + [hbm] * len(job.ins), out_specs=list(out_specs) + [hbm] * len(job.out_shape),
        out_shape=list(out_shape) + list(job.out_shape),
        input_output_aliases={**aliases, **{a + i: len(out_shape) + o for i, o in job.aliases.items()}},
        scratch_shapes=list(scratch_shapes) + _job_scratch(job),
        compiler_params=_params(("arbitrary",) * len(grid)),
    )(*operands, *job.ins)
    return outs[:len(out_shape)], outs[len(out_shape):]


def _gather_via_sibling(x, take, place, out_shape, name, landing=None):
    def body(*refs):
        x_ref, o_ref = refs[0], refs[-4]
        send_sems, recv_sems, local_sem = refs[-3:]
        x, y, c = lax.axis_index("x"), lax.axis_index("y"), lax.axis_index("c")
        me, sibling = (x, y, c), (x, y, 1 - c)
        chips = [(1 - x, y), (x, 1 - y), (1 - x, 1 - y)]
        src = take(x_ref)

        def slot(px, py, pc):
            return place(o_ref, 4 * px + 2 * py + pc)

        def copy(k, block, to, first_hand):
            return pltpu.make_async_remote_copy(
                src_ref=src if first_hand else slot(*block), dst_ref=slot(*block), send_sem=send_sems.at[k],
                recv_sem=recv_sems.at[k], device_id=to, device_id_type=pl.DeviceIdType.MESH)

        mine = pltpu.make_async_copy(src, slot(*me), local_sem)
        mine.start()
        first = [copy(0, me, sibling, True)] + [copy(1 + j, me, (*chip, c), True) for j, chip in enumerate(chips)]
        for cp in first:
            cp.start()
        passed = []
        for j, chip in enumerate(chips):
            copy(1 + j, (*chip, c), me, True).wait_recv()
            passed.append(copy(4 + j, (*chip, c), sibling, False))
            passed[-1].start()
        copy(0, sibling, me, True).wait_recv()
        for j, chip in enumerate(chips):
            copy(4 + j, (*chip, 1 - c), me, False).wait_recv()
        for cp in first + passed:
            cp.wait_send()
        mine.wait()

    hbm = pl.BlockSpec(memory_space=pl.ANY)
    ins = [x] if landing is None else [x, landing]
    return pl.pallas_call(
        body, name=name, in_specs=[hbm] * len(ins), out_specs=hbm, out_shape=out_shape,
        input_output_aliases={} if landing is None else {1: 0},
        scratch_shapes=[pltpu.SemaphoreType.DMA((N_DEV - 1,)), pltpu.SemaphoreType.DMA((N_DEV - 1,)),
                        pltpu.SemaphoreType.DMA],
    )(*ins)


def _lane_window(ref, who):
    return ref.at[:, pl.ds(pl.multiple_of(who * LANES, LANES), LANES)]


def _gather_job(shards, items):
    out_shape = []
    for i, _ in items:
        _, s, axis = _SHARDED[i]
        whole = i != _W_IN and axis == 1
        out_shape.append(jax.ShapeDtypeStruct((s[0], N_DEV * s[1]) if whole else (N_DEV,) + s, BF16))

    def pairs(in_refs, out_refs):
        out = []
        for (i, l), src, dst in zip(items, in_refs, out_refs):
            if i != _W_IN and _SHARDED[i][2] == 1:
                out.append((lambda who, src=src, l=l: src.at[l], lambda who, dst=dst: _lane_window(dst, who)))
            else:
                out.append((lambda who, src=src, l=l: src.at[l], lambda who, dst=dst: dst.at[who]))
        return out

    return _Job([shards[i] for i, _ in items], out_shape, {}, pairs, len(items))


def _landed_weights(items, landed):
    out = {}
    for (i, _), a in zip(items, landed):
        n, s, axis = _SHARDED[i]
        if i == _W_IN:
            out[n] = a.reshape(D_IN, D_MODEL)
        elif axis == 0:
            out[n] = a.reshape(N_DEV * s[0], s[1])
        else:
            out[n] = a
    return out


def _scatter_job(grads, items, layer, parts=None):
    ng = len(grads)
    out_shape = [jax.ShapeDtypeStruct((N_DEV, DEPTH) + _SHARDED[i][1], BF16) for i in items]

    def pairs(in_refs, out_refs):
        out = []
        for i, src, dst in zip(items, in_refs[:ng], out_refs):
            _, s, axis = _SHARDED[i]
            if i == _W_IN:
                take = lambda who, src=src: src.at[who]
            elif axis == 0:
                take = lambda who, src=src, s=s: src.at[pl.ds(pl.multiple_of(who * s[0], 16), s[0])]
            else:
                take = lambda who, src=src: _lane_window(src, who)
            out.append((take, lambda who, dst=dst: dst.at[who, layer]))
        return out

    aliases = {} if parts is None else {ng + j: j for j in range(len(items))}
    return _Job(list(grads) + ([] if parts is None else list(parts)), out_shape, aliases, pairs, len(items))


def _rows_job(src, row0, landing=None):
    n = src.shape[0]
    pairs = lambda in_refs, out_refs: [(lambda who: in_refs[0], lambda who: out_refs[0].at[who, pl.ds(row0, n)])]
    return _Job([src] + ([] if landing is None else [landing]), [jax.ShapeDtypeStruct((N_DEV, _REP_ROWS, LANES), F32)],
                {} if landing is None else {1: 0}, pairs, 1)


_REPLICATED = (("norm_g", (2, 1024)), ("mem_norm_g", (2, 1024)), ("b_gate", (2, 3072)),
               ("ssm_lambda_re", (2, 48, 64)), ("ssm_lambda_im", (2, 48, 64)), ("ssm_log_dt", (2, 48)),
               ("ssm_b_re", (2, 48, 64, 16)), ("ssm_b_im", (2, 48, 64, 16)), ("ssm_c_re", (2, 48, 16, 64)),
               ("ssm_c_im", (2, 48, 16, 64)), ("ssm_d", (2, 768)), ("b_glu", (2, 768)), ("rel_bias", (32, 12)),
               ("final_norm_g", (1024,)))
_PER_LAYER = tuple((n, s[1:]) for n, s in _REPLICATED if s[0] == DEPTH and len(s) > 1)
_SHARED = tuple((n, s) for n, s in _REPLICATED if (n, s[1:]) not in _PER_LAYER)
_REP_HALF_ROWS = 1664
_REP_ROWS = 2 * _REP_HALF_ROWS
assert sum(int(np.prod(s)) for _, s in _PER_LAYER + _SHARED) <= _REP_HALF_ROWS * LANES


def _pack_half(tree, layer, shared):
    flat = [tree[n][layer].reshape(-1) for n, _ in _PER_LAYER]
    if shared:
        flat += [tree[n].reshape(-1) for n, _ in _SHARED]
    flat = jnp.concatenate(flat)
    return jnp.pad(flat, (0, _REP_HALF_ROWS * LANES - flat.shape[0])).reshape(_REP_HALF_ROWS, LANES)


def _pack_replicated(tree):
    return jnp.concatenate([_pack_half(tree, 1, False), _pack_half(tree, 0, True)])[None]


def _unpack_replicated(packed):
    halves = packed.reshape(2, -1)
    out, r = {}, 0
    for n, s in _PER_LAYER:
        size = int(np.prod(s))
        out[n] = jnp.stack([halves[1, r:r + size].reshape(s), halves[0, r:r + size].reshape(s)])
        r += size
    for n, s in _SHARED:
        size = int(np.prod(s))
        out[n] = halves[1, r:r + size].reshape(s)
        r += size
    return out


def _discretize(lam_re, lam_im, log_dt, b_re, b_im):
    dt = jnp.exp(log_dt)[:, None]
    mag = jnp.exp(lam_re * dt)
    abar_re, abar_im = mag * jnp.cos(lam_im * dt), mag * jnp.sin(lam_im * dt)
    den = lam_re * lam_re + lam_im * lam_im
    nr, ni = abar_re - 1.0, abar_im
    f_re = (nr * lam_re + ni * lam_im) / den
    f_im = (ni * lam_re - nr * lam_im) / den
    bbar_re = f_re[..., None] * b_re - f_im[..., None] * b_im
    bbar_im = f_re[..., None] * b_im + f_im[..., None] * b_re
    return abar_re, abar_im, bbar_re, bbar_im


def _block_diag(a):
    _, R, C = a.shape
    a = a.reshape(SSM_BLOCKS, 8, R, C)
    eye = jnp.eye(8, dtype=a.dtype)
    return (a[:, :, :, None, :] * eye[None, :, None, :, None]).reshape(SSM_BLOCKS, 8 * R, 8 * C)


def _diag_blocks(a, R, C):
    a = a.reshape(SSM_BLOCKS, 8, R, 8, C)
    eye = jnp.eye(8, dtype=a.dtype)
    return jnp.sum(a * eye[None, :, None, :, None], axis=3).reshape(SSM_GROUPS, R, C)


def _carried(result, job):
    return (result, None) if job is None else result


def _layer_fwd(x, mem, W, P, bias, layer, jobs):
    tag = f"l{layer}"
    abar_re, abar_im, bbar_re, bbar_im = _discretize(P["ssm_lambda_re"][layer], P["ssm_lambda_im"][layer],
                                                     P["ssm_log_dt"][layer], P["ssm_b_re"][layer], P["ssm_b_im"][layer])
    c_re, c_im = P["ssm_c_re"][layer], P["ssm_c_im"][layer]
    ssm = dict(
        are=abar_re.reshape(1, N_STATE), aim=abar_im.reshape(1, N_STATE),
        bre=_block_diag(bbar_re.transpose(0, 2, 1)).astype(BF16), bim=_block_diag(bbar_im.transpose(0, 2, 1)).astype(BF16),
        cre=_block_diag(c_re.transpose(0, 2, 1)).astype(BF16), cimn=_block_diag(-c_im.transpose(0, 2, 1)).astype(BF16),
        d=P["ssm_d"][layer].reshape(1, D_SSM))
    bglu = P["b_glu"][layer].reshape(1, D_SSM)
    bgate = P["b_gate"][layer].reshape(1, N_GATES)
    g = P["norm_g"][layer].reshape(1, D_MODEL)
    gm = P["mem_norm_g"][layer].reshape(1, D_MODEL)
    delivered = {}

    def carry(stage):
        return jobs[stage][0] if stage in jobs else None

    def deliver(stage, landed):
        if landed is not None:
            delivered[stage] = _landed_weights(jobs[stage][1], landed)

    T = x.shape[0]
    (proj, h), landed = _carried(_norm_proj(x, g, W["w_in"], min(T, 1024), 2176, f"{tag}_proj", job=carry("proj"),
                                            w_turned=True), carry("proj"))
    deliver("proj", landed)
    W = {**W, **delivered.get("proj", {})}
    (xr, xi, y, o_ssm), landed = _carried(
        _ssm_fwd(proj, ssm["bre"], ssm["bim"], ssm["cre"], ssm["cimn"], ssm["are"], ssm["aim"], ssm["d"], W["w_glu"],
                 bglu, 512, f"{tag}_ssm", job=carry("ssm")), carry("ssm"))
    deliver("ssm", landed)
    os, lses = [], []
    for grp in range(3):
        stage = f"attn{grp}"
        (o_g, lse_g), landed = _carried(_attn_fwd(proj, bias[grp], grp, f"{tag}_{stage}", job=carry(stage)), carry(stage))
        deliver(stage, landed)
        os.append(o_g)
        lses.append(lse_g)
    o_attn = _attn_mix(os, lses, proj, min(T, ROW_TILE), f"{tag}_attn_mix")
    kvb, hm = _norm_proj(mem, gm, W["w_mem_kv"], mem.shape[0], 1024, f"{tag}_mem_kv", out_dtype=BF16)
    o_mem = _mem_fwd(proj, kvb, min(T, ROW_TILE), f"{tag}_mem")
    x_out, merged = _merge_fwd(x, o_ssm, o_attn, o_mem, proj, bgate, W["w_br_ssm"], W["w_br_attn"], W["w_br_mem"],
                               W["w_out"], 512, f"{tag}_merge")
    res = dict(x=x, mem=mem, proj=proj, h=h, xr=xr, xi=xi, y=y, o_ssm=o_ssm, os=os, lses=lses,
               o_attn=o_attn, kvb=kvb, hm=hm, o_mem=o_mem, merged=merged, ssm=ssm, bglu=bglu,
               bgate=bgate, g=g, gm=gm, W=W)
    return x_out, res, delivered


def _layer_bwd(dx, res, P, bias, layer, jobs):
    tag = f"l{layer}b"
    proj, ssm, W = res["proj"], res["ssm"], res["W"]
    T = dx.shape[0]
    landed = {}

    def run(stage, fn, job):
        out, landed[stage] = _carried(fn(job), job)
        if job is None:
            del landed[stage]
        return out

    dproj, dbr, do_ssm, do_attn, do_mem, dbg = run(
        "merge", lambda job: _merge_bwd(dx, res["o_ssm"], res["o_attn"], res["o_mem"], proj, res["bgate"], W["w_br_ssm"],
                                        W["w_br_attn"], W["w_br_mem"], W["w_out"], 512, f"{tag}_merge", job=job),
        jobs.get("merge"))
    gw = {}
    tk = min(T, 1024)
    gw["w_out"] = _mm_tn(res["merged"], dx, 1024, 1024, tk, f"{tag}_dw_out")
    gw["w_br_ssm"] = _mm_tn(res["o_ssm"], dbr, 768, 1024, tk, f"{tag}_dw_br_ssm", b_col=0, n=1024)
    gw["w_br_attn"] = _mm_tn(res["o_attn"], dbr, 768, 1024, tk, f"{tag}_dw_br_attn", b_col=1024, n=1024)
    gw["w_br_mem"] = _mm_tn(res["o_mem"], dbr, 512, 1024, tk, f"{tag}_dw_br_mem", b_col=2048, n=1024)

    rows = min(T, ROW_TILE)
    dqm, dzm, dkv = _mem_bwd(do_mem, proj, res["kvb"], rows, f"{tag}_mem")
    M = dkv.shape[0]
    gw["w_mem_kv"] = _mm_tn(res["hm"], dkv, 1024, 1024, M, f"{tag}_dw_mem_kv")
    _, dgm = _proj_bwd(dkv.astype(BF16), W["w_mem_kv"], res["mem"], res["gm"], jnp.zeros_like(res["mem"]), M, 1024,
                       f"{tag}_mem_norm")

    do_g, corr, dproj = _attn_mix_bwd(do_attn, res["os"], res["lses"], proj, dproj, rows, f"{tag}_attn_mix")
    dbs = []
    for grp in range(3):
        dproj, db_g = _attn_bwd(proj, do_g, corr, res["lses"][grp], bias[grp], dproj, grp, f"{tag}_attn{grp}")
        dbs.append(db_g)
    dbias = jnp.stack(dbs)

    dy, dproj, gelu_b, dt_b, dbglu = _glu_bwd(do_ssm, res["y"], proj, W["w_glu"], res["bglu"], dproj, rows, f"{tag}_glu")
    gw["w_glu"] = _mm_tn(gelu_b, dt_b, 768, 768, tk, f"{tag}_dw_glu")
    dproj, dbre, dbim, dcre, dcim, dare, daim, dd = run(
        "ssm", lambda job: _ssm_bwd(dy, proj, res["xr"], res["xi"], ssm["bre"], ssm["bim"], ssm["cre"], ssm["cimn"],
                                    ssm["are"], ssm["aim"], ssm["d"], dproj, 256, f"{tag}_ssm", job=job),
        jobs.get("ssm"))
    _, disc_vjp = jax.vjp(_discretize, P["ssm_lambda_re"][layer], P["ssm_lambda_im"][layer], P["ssm_log_dt"][layer],
                          P["ssm_b_re"][layer], P["ssm_b_im"][layer])
    d_lre, d_lim, d_ldt, d_bre, d_bim = disc_vjp((dare.reshape(SSM_GROUPS, SSM_STATE), daim.reshape(SSM_GROUPS, SSM_STATE),
                                                  _diag_blocks(dbre, SSM_STATE, SSM_GROUP),
                                                  _diag_blocks(dbim, SSM_STATE, SSM_GROUP)))

    small = [gw[_SHARDED[i][0]] for i in _SMALL]
    for seg, piece in (("q_mem", dqm), ("z_mem", dzm)):
        dproj = lax.dynamic_update_slice(dproj, piece, (0, _OFF[seg]))
    dw_in = run("dw_in", lambda job: _mm_tn(dproj, res["h"], 2176, 1024, min(T, 1024), f"{tag}_dw_in", job=job),
                jobs["dw_in"](small) if "dw_in" in jobs else None)
    dw_in = dw_in.reshape((N_DEV,) + _SHARDED[_W_IN][1])
    dx_in, dg = run("proj", lambda job: _proj_bwd(dproj, W["w_in"], res["x"], res["g"], dx, min(T, 1024), 2176,
                                                  f"{tag}_proj", job=job, w_turned=True),
                    jobs["proj"](small, dw_in, landed) if "proj" in jobs else None)

    gp = dict(norm_g=dg[0], mem_norm_g=dgm[0], b_gate=dbg[0], ssm_lambda_re=d_lre, ssm_lambda_im=d_lim,
              ssm_log_dt=d_ldt, ssm_b_re=d_bre, ssm_b_im=d_bim,
              ssm_c_re=_diag_blocks(dcre, SSM_GROUP, SSM_STATE), ssm_c_im=_diag_blocks(dcim, SSM_GROUP, SSM_STATE),
              ssm_d=dd[0], b_glu=dbglu[0])
    return dx_in, dw_in, gp, dbias, landed


def _train_step(x, mem, target, shards, P):
    rest0 = [(i, 0) for i in _SMALL]
    thirds1 = [[(i, 1) for i in _SMALL[k::3]] for k in range(3)]
    first = [(_W_IN, 0)]
    w_in0 = _gather_via_sibling(shards[_W_IN], lambda ref: ref.at[0], lambda ref, s: ref.at[s],
                                jax.ShapeDtypeStruct((N_DEV,) + _SHARDED[_W_IN][1], BF16), "gather_w_in0")
    W0 = _landed_weights(first, [w_in0])
    buckets = _bucket_tables()
    bias = _bias_tables(P["rel_bias"], buckets, "bias_tables")
    jobs0 = {"proj": (_gather_job(shards, rest0), rest0), "ssm": (_gather_job(shards, [(_W_IN, 1)]), [(_W_IN, 1)]),
             **{f"attn{k}": (_gather_job(shards, items), items) for k, items in enumerate(thirds1)}}
    x, res0, delivered = _layer_fwd(x, mem, W0, P, bias, 0, jobs0)
    W1 = {**delivered["ssm"], **delivered["attn0"], **delivered["attn1"], **delivered["attn2"]}
    x, res1, _ = _layer_fwd(x, mem, W1, P, bias, 1, {})
    loss, dx, dgf = _loss_head(x, P["final_norm_g"].reshape(1, D_MODEL), target, min(x.shape[0], ROW_TILE),
                                "loss_head")

    dx, dw_in1, gp1, dbias1, landed1 = _layer_bwd(
        dx, res1, P, bias, 1, {"proj": lambda small, dw_in, landed: _scatter_job(small, _SMALL, 1)})
    rep1 = _pack_half({n: a[None] for n, a in gp1.items()}, 0, False)
    dx, _, gp0, dbias0, landed0 = _layer_bwd(
        dx, res0, P, bias, 0,
        {"merge": _rows_job(rep1, 0), "ssm": _scatter_job([dw_in1], [_W_IN], 1),
         "dw_in": lambda small: _scatter_job(small, _SMALL, 0, parts=landed1["proj"]),
         "proj": lambda small, dw_in, landed: _scatter_job([dw_in], [_W_IN], 0, parts=landed["ssm"])})
    d_rel = _bias_grad(dbias0, dbias1, buckets, "bias_grad")
    gp0 = {n: a[None] for n, a in gp0.items()}
    gp0["rel_bias"] = jnp.sum(d_rel, axis=-1).transpose(2, 0, 1).reshape(NUM_BUCKETS, 12)
    gp0["final_norm_g"] = dgf[0]
    rep0 = _pack_half(gp0, 0, True)
    rparts = _gather_via_sibling(rep0, lambda ref: ref, lambda ref, s: ref.at[s, pl.ds(_REP_HALF_ROWS, _REP_HALF_ROWS)],
                                 jax.ShapeDtypeStruct((N_DEV, _REP_ROWS, LANES), F32), "gather_small_grads0",
                                 landing=landed0["merge"][0])
    return loss[0, 0], dx, list(landed0["proj"]) + list(landed0["dw_in"]), rparts


_WEIGHTS = ["norm_g", "mem_norm_g", "w_in", "b_gate", "ssm_lambda_re", "ssm_lambda_im", "ssm_log_dt", "ssm_b_re",
            "ssm_b_im", "ssm_c_re", "ssm_c_im", "ssm_d", "w_glu", "b_glu", "w_mem_kv", "w_br_ssm", "w_br_attn",
            "w_br_mem", "w_out", "rel_bias", "final_norm_g"]
_ADAM_ROWS = {"w_in": 136,"w_glu": 96, "w_mem_kv": 128, "w_br_ssm": 768, "w_br_attn": 768, "w_br_mem": 512,
              "w_out": 128}


def kernel(x, mem, norm_g, mem_norm_g, w_in, b_gate, ssm_lambda_re, ssm_lambda_im, ssm_log_dt, ssm_b_re, ssm_b_im, ssm_c_re, ssm_c_im, ssm_d, w_glu, b_glu, w_mem_kv, w_br_ssm, w_br_attn, w_br_mem, w_out, rel_bias, final_norm_g, loss_target, m_norm_g, m_mem_norm_g, m_w_in, m_b_gate, m_ssm_lambda_re, m_ssm_lambda_im, m_ssm_log_dt, m_ssm_b_re, m_ssm_b_im, m_ssm_c_re, m_ssm_c_im, m_ssm_d, m_w_glu, m_b_glu, m_w_mem_kv, m_w_br_ssm, m_w_br_attn, m_w_br_mem, m_w_out, m_rel_bias, m_final_norm_g, v_norm_g, v_mem_norm_g, v_w_in, v_b_gate, v_ssm_lambda_re, v_ssm_lambda_im, v_ssm_log_dt, v_ssm_b_re, v_ssm_b_im, v_ssm_c_re, v_ssm_c_im, v_ssm_d, v_w_glu, v_b_glu, v_w_mem_kv, v_w_br_ssm, v_w_br_attn, v_w_br_mem, v_w_out, v_rel_bias, v_final_norm_g):
    given = dict(locals())
    w = {n: given[n] for n in _WEIGHTS}
    m = {n: given["m_" + n] for n in _WEIGHTS}
    v = {n: given["v_" + n] for n in _WEIGHTS}

    turned = lambda n, a: a.swapaxes(1, 2) if n == "w_in" else a
    shards = [turned(n, w[n]).astype(BF16) for n, _, _ in _SHARDED]
    loss, dx, parts, rparts = _train_step(x[0], mem[0], loss_target[0], shards, w)
    loss = lax.psum(loss, ("x", "y", "c"))

    new = {}
    for (n, _, _), p in zip(_SHARDED, parts):
        new[n] = [turned(n, a) for a in _adamw(p, turned(n, w[n]), turned(n, m[n]), turned(n, v[n]), _ADAM_ROWS[n],
                                               f"adamw_{n}")]
    rp = [_unpack_replicated(a) for a in _adamw(rparts[:, None], _pack_replicated(w), _pack_replicated(m),
                                                _pack_replicated(v), _REP_ROWS // 4, "adamw_replicated")]
    for n, _ in _REPLICATED:
        new[n] = [rp[kind][n] for kind in range(4)]
    outs = [loss, dx[None]]
    for kind in range(4):
        outs.extend(new[n][kind] for n in _WEIGHTS)
    return tuple(outs)
```

```python
import functools
import math
from typing import Callable, NamedTuple

import jax
import jax.numpy as jnp
import numpy as np
from jax import lax
from jax.experimental import pallas as pl
from jax.experimental.pallas import tpu as pltpu

F32 = jnp.float32
BF16 = jnp.bfloat16

D_MODEL = 1024
DEPTH = 2
EPS = 1e-6
D_SSM = 768
SSM_GROUP = 16
SSM_GROUPS = 48
SSM_STATE = 64
N_STATE = SSM_GROUPS * SSM_STATE
SSM_BLOCKS = 6
D_ATTN = 768
ATTN_HEAD_DIM = 64
ATTN_GROUP_WIDTH = 256
ATTN_DILATIONS = (1, 4, 16)
ATTN_SPAN = 128
ATTN_BLOCK = 128
NUM_BUCKETS = 32
REL_MAX_DISTANCE = 2048
NEG_INF = -1e30
MEM_HEADS = 4
MEM_HEAD_DIM = 128
D_MEM = 512
N_GATES = 3 * D_MODEL
D_IN = 8704
N_DEV = 8
LANES = 128
ADAM_LR = 0.001
ADAM_B1 = 0.9
ADAM_B2 = 0.999
ADAM_EPS = 1e-08
ADAM_WD = 0.01
ADAM_STEP = 10

_OFF = {"u": 0, "z_ssm": 768, "q": 1536, "k": 2304, "v": 3072, "z_attn": 3840, "q_mem": 4608, "z_mem": 5120,
        "gates": 5632}
GATE_BLOCK = 512
ROW_TILE = 1024

NN = (((1,), (0,)), ((), ()))
NT = (((1,), (1,)), ((), ()))
TN = (((0,), (0,)), ((), ()))

VMEM_LIMIT = 56 * 1024 * 1024


def _dot(a, b, dims=NN):
    return lax.dot_general(a, b, dims, preferred_element_type=F32)


def _sigmoid(x):
    return 1.0 / (1.0 + jnp.exp(-x))


def _gelu_parts(x):
    k = math.sqrt(2.0 / math.pi)
    t = jnp.tanh(k * (x + 0.044715 * (x * x * x)))
    cdf = 0.5 * (1.0 + t)
    dcdf = 0.5 * (1.0 - t * t) * k * (1.0 + 3.0 * 0.044715 * (x * x))
    return x * cdf, cdf + x * dcdf


def _params(sem, vmem=VMEM_LIMIT):
    return pltpu.CompilerParams(dimension_semantics=sem, vmem_limit_bytes=vmem)


def _full(shape):
    return pl.BlockSpec(shape, lambda *_: (0,) * len(shape))


def _norm_proj(x, g, w, tm, tn, name, out_dtype=F32, job=None, w_turned=False):
    T, D = x.shape
    N = w.shape[0] if w_turned else w.shape[1]
    w_spec = pl.BlockSpec((tn, D), lambda i, j: (j, 0)) if w_turned else pl.BlockSpec((D, tn), lambda i, j: (0, j))
    dims = NT if w_turned else NN

    def body(x_ref, g_ref, w_ref, o_ref, h_ref, hs):
        @pl.when(pl.program_id(1) == 0)
        def _():
            xv = x_ref[...]
            r = lax.rsqrt(jnp.mean(xv * xv, axis=-1, keepdims=True) + EPS)
            hv = (xv * r * g_ref[...]).astype(BF16)
            hs[...] = hv
            h_ref[...] = hv

        o_ref[...] = _dot(hs[...], w_ref[...], dims).astype(out_dtype)

    return _pc(
        body, job, name=name, grid=(T // tm, N // tn),
        in_specs=[pl.BlockSpec((tm, D), lambda i, j: (i, 0)), _full((1, D)), w_spec],
        out_specs=[pl.BlockSpec((tm, tn), lambda i, j: (i, j)), pl.BlockSpec((tm, D), lambda i, j: (i, 0))],
        out_shape=[jax.ShapeDtypeStruct((T, N), out_dtype), jax.ShapeDtypeStruct((T, D), BF16)],
        scratch_shapes=[pltpu.VMEM((tm, D), BF16)], sem=("parallel", "arbitrary"), operands=(x, g, w))


def _mm_tn(a, b, tm, tn, tk, name, b_col=0, n=None, job=None):
    K, M = a.shape
    N = b.shape[1] if n is None else n
    nk = K // tk
    j0 = b_col // tn

    def body(a_ref, b_ref, o_ref, acc):
        k = pl.program_id(2)

        @pl.when(k == 0)
        def _():
            acc[...] = jnp.zeros_like(acc)

        acc[...] += _dot(a_ref[...].astype(BF16), b_ref[...].astype(BF16), TN)

        @pl.when(k == nk - 1)
        def _():
            o_ref[...] = acc[...].astype(BF16)

    out = _pc(
        body, job, name=name, grid=(M // tm, N // tn, nk),
        in_specs=[pl.BlockSpec((tk, tm), lambda i, j, k: (k, i)), pl.BlockSpec((tk, tn), lambda i, j, k: (k, j0 + j))],
        out_specs=[pl.BlockSpec((tm, tn), lambda i, j, k: (i, j))],
        out_shape=[jax.ShapeDtypeStruct((M, N), BF16)],
        scratch_shapes=[pltpu.VMEM((tm, tn), F32)], sem=("parallel", "parallel", "arbitrary"), operands=(a, b))
    return out[0] if job is None else (out[0][0], out[1])


def _proj_bwd(dp, w, x, g, dres, tm, tk, name, job=None, w_turned=False):
    T, N = dp.shape
    D = x.shape[1]
    nk = N // tk
    w_spec = pl.BlockSpec((tk, D), lambda i, k: (k, 0)) if w_turned else pl.BlockSpec((D, tk), lambda i, k: (0, k))
    dims = NN if w_turned else NT

    def body(dp_ref, w_ref, x_ref, g_ref, dres_ref, dx_ref, dg_ref, acc):
        i, k = pl.program_id(0), pl.program_id(1)

        @pl.when(k == 0)
        def _():
            acc[...] = jnp.zeros_like(acc)

        @pl.when((i == 0) & (k == 0))
        def _():
            dg_ref[...] = jnp.zeros_like(dg_ref)

        acc[...] += _dot(dp_ref[...], w_ref[...], dims)

        @pl.when(k == nk - 1)
        def _():
            xv = x_ref[...]
            dh = acc[...]
            r = lax.rsqrt(jnp.mean(xv * xv, axis=-1, keepdims=True) + EPS)
            xr = xv * r
            dg_ref[...] += jnp.sum(dh * xr, axis=0, keepdims=True)
            wv = dh * g_ref[...]
            dx_ref[...] = dres_ref[...] + r * (wv - xr * jnp.mean(wv * xr, axis=-1, keepdims=True))

    return _pc(
        body, job, name=name, grid=(T // tm, nk),
        in_specs=[pl.BlockSpec((tm, tk), lambda i, k: (i, k)), w_spec,
                  pl.BlockSpec((tm, D), lambda i, k: (i, 0)), _full((1, D)),
                  pl.BlockSpec((tm, D), lambda i, k: (i, 0))],
        out_specs=[pl.BlockSpec((tm, D), lambda i, k: (i, 0)), _full((1, D))],
        out_shape=[jax.ShapeDtypeStruct((T, D), F32), jax.ShapeDtypeStruct((1, D), F32)],
        scratch_shapes=[pltpu.VMEM((tm, D), F32)], sem=("arbitrary", "arbitrary"), operands=(dp, w, x, g, dres))


def _ssm_fwd(proj, bre, bim, cre, cimn, are, aim, d, wglu, bglu, tc, name, job=None):
    T = proj.shape[0]
    ucol, zcol = _OFF["u"] // D_SSM, _OFF["z_ssm"] // D_SSM

    def body(u_ref, z_ref, bre_ref, bim_ref, cre_ref, cim_ref, are_ref, aim_ref, d_ref, wg_ref, bg_ref,
             xr_ref, xi_ref, y_ref, o_ref, car_r, car_i):
        @pl.when(pl.program_id(0) == 0)
        def _():
            car_r[...] = jnp.zeros_like(car_r)
            car_i[...] = jnp.zeros_like(car_i)

        u = u_ref[...]
        ub = u.astype(BF16)
        for k in range(SSM_BLOCKS):
            uk = ub[:, 128 * k:128 * (k + 1)]
            xr_ref[:, 512 * k:512 * (k + 1)] = _dot(uk, bre_ref[k])
            xi_ref[:, 512 * k:512 * (k + 1)] = _dot(uk, bim_ref[k])
        ar, ai = are_ref[...], aim_ref[...]

        def step(t, c):
            pr, pi = c
            nr = ar * pr - ai * pi + xr_ref[pl.ds(t, 1), :]
            ni = ar * pi + ai * pr + xi_ref[pl.ds(t, 1), :]
            xr_ref[pl.ds(t, 1), :] = nr
            xi_ref[pl.ds(t, 1), :] = ni
            return nr, ni

        pr, pi = lax.fori_loop(0, tc, step, (car_r[...], car_i[...]))
        car_r[...] = pr
        car_i[...] = pi

        ys = []
        for k in range(SSM_BLOCKS):
            xrk = xr_ref[:, 512 * k:512 * (k + 1)].astype(BF16)
            xik = xi_ref[:, 512 * k:512 * (k + 1)].astype(BF16)
            ys.append(_dot(xrk, cre_ref[k]) + _dot(xik, cim_ref[k]))
        y = jnp.concatenate(ys, axis=1) + d_ref[...] * u
        y_ref[...] = y
        gl, _ = _gelu_parts(y)
        t = _dot(gl.astype(BF16), wg_ref[...]) + bg_ref[...]
        z = z_ref[...]
        o_ref[...] = (gl * _sigmoid(t) * (z * _sigmoid(z))).astype(BF16)

    return _pc(
        body, job, name=name, grid=(T // tc,),
        in_specs=[pl.BlockSpec((tc, D_SSM), lambda i: (i, ucol)), pl.BlockSpec((tc, D_SSM), lambda i: (i, zcol)),
                  _full((SSM_BLOCKS, 128, 512)), _full((SSM_BLOCKS, 128, 512)),
                  _full((SSM_BLOCKS, 512, 128)), _full((SSM_BLOCKS, 512, 128)),
                  _full((1, N_STATE)), _full((1, N_STATE)), _full((1, D_SSM)),
                  _full((D_SSM, D_SSM)), _full((1, D_SSM))],
        out_specs=[pl.BlockSpec((tc, N_STATE), lambda i: (i, 0)), pl.BlockSpec((tc, N_STATE), lambda i: (i, 0)),
                   pl.BlockSpec((tc, D_SSM), lambda i: (i, 0)), pl.BlockSpec((tc, D_SSM), lambda i: (i, 0))],
        out_shape=[jax.ShapeDtypeStruct((T, N_STATE), F32), jax.ShapeDtypeStruct((T, N_STATE), F32),
                   jax.ShapeDtypeStruct((T, D_SSM), F32), jax.ShapeDtypeStruct((T, D_SSM), BF16)],
        scratch_shapes=[pltpu.VMEM((1, N_STATE), F32), pltpu.VMEM((1, N_STATE), F32)], sem=("arbitrary",),
        operands=(proj, proj, bre, bim, cre, cimn, are, aim, d, wglu, bglu))


def _glu_bwd(do, y, proj, wglu, bglu, dproj, tm, name):
    T = y.shape[0]
    zcol = _OFF["z_ssm"] // D_SSM

    def body(do_ref, y_ref, z_ref, wg_ref, bg_ref, _, dy_ref, dz_ref, g_ref, dt_ref, db_ref):
        @pl.when(pl.program_id(0) == 0)
        def _():
            db_ref[...] = jnp.zeros_like(db_ref)

        dov = do_ref[...]
        gl, dgl = _gelu_parts(y_ref[...])
        glb = gl.astype(BF16)
        sg = _sigmoid(_dot(glb, wg_ref[...]) + bg_ref[...])
        z = z_ref[...]
        sz = _sigmoid(z)
        dz_ref[...] = (dov * (gl * sg) * (sz * (1.0 + z * (1.0 - sz)))).astype(BF16)
        dy2 = dov * (z * sz)
        dt = dy2 * gl * (sg * (1.0 - sg))
        dtb = dt.astype(BF16)
        dg = dy2 * sg + _dot(dtb, wg_ref[...], NT)
        dy_ref[...] = dg * dgl
        g_ref[...] = glb
        dt_ref[...] = dtb
        db_ref[...] += jnp.sum(dt, axis=0, keepdims=True)

    row = lambda i: (i, 0)
    return pl.pallas_call(
        body, name=name, grid=(T // tm,),
        in_specs=[pl.BlockSpec((tm, D_SSM), row), pl.BlockSpec((tm, D_SSM), row),
                  pl.BlockSpec((tm, D_SSM), lambda i: (i, zcol)), _full((D_SSM, D_SSM)), _full((1, D_SSM)),
                  pl.BlockSpec(memory_space=pl.ANY)],
        out_specs=[pl.BlockSpec((tm, D_SSM), row), pl.BlockSpec((tm, D_SSM), lambda i: (i, zcol)),
                   pl.BlockSpec((tm, D_SSM), row), pl.BlockSpec((tm, D_SSM), row), _full((1, D_SSM))],
        out_shape=[jax.ShapeDtypeStruct((T, D_SSM), F32), jax.ShapeDtypeStruct(dproj.shape, BF16),
                   jax.ShapeDtypeStruct((T, D_SSM), BF16), jax.ShapeDtypeStruct((T, D_SSM), BF16),
                   jax.ShapeDtypeStruct((1, D_SSM), F32)],
        input_output_aliases={5: 1},
        compiler_params=_params(("arbitrary",)),
    )(do, y, proj, wglu, bglu, dproj)


def _ssm_bwd(dy, proj, xr, xi, bre, bim, cre, cimn, are, aim, d, dproj, tc, name, job=None):
    T = dy.shape[0]
    nc = T // tc
    ucol = _OFF["u"] // D_SSM
    rb = tc // 8

    def body(dy_ref, u_ref, xr_ref, xi_ref, xpr_ref, xpi_ref, bre_ref, bim_ref, cre_ref, cim_ref,
             are_ref, aim_ref, d_ref, _,
             du_ref, dbre_ref, dbim_ref, dcre_ref, dcim_ref, dare_ref, daim_ref, dd_ref, gr, gi, car_r, car_i):
        i = pl.program_id(0)

        @pl.when(i == 0)
        def _():
            for ref in (car_r, car_i, dbre_ref, dbim_ref, dcre_ref, dcim_ref, dare_ref, daim_ref, dd_ref):
                ref[...] = jnp.zeros_like(ref)

        dyv = dy_ref[...]
        dyb = dyv.astype(BF16)
        u = u_ref[...]
        ub = u.astype(BF16)
        for k in range(SSM_BLOCKS):
            dk = dyb[:, 128 * k:128 * (k + 1)]
            gr[:, 512 * k:512 * (k + 1)] = _dot(dk, cre_ref[k], NT)
            gi[:, 512 * k:512 * (k + 1)] = _dot(dk, cim_ref[k], NT)
        ar, ai = are_ref[...], aim_ref[...]

        def step(s, c):
            pr, pi = c
            t = tc - 1 - s
            nr = gr[pl.ds(t, 1), :] + ar * pr + ai * pi
            ni = gi[pl.ds(t, 1), :] + ar * pi - ai * pr
            gr[pl.ds(t, 1), :] = nr
            gi[pl.ds(t, 1), :] = ni
            return nr, ni

        pr, pi = lax.fori_loop(0, tc, step, (car_r[...], car_i[...]))
        car_r[...] = pr
        car_i[...] = pi

        keep = jnp.where(i == nc - 1, 0.0, 1.0)
        row0 = lax.broadcasted_iota(jnp.int32, (tc, 1), 0) == 0
        dd_ref[...] += jnp.sum(dyv * u, axis=0, keepdims=True)
        for k in range(SSM_BLOCKS):
            sl = slice(512 * k, 512 * (k + 1))
            ch = slice(128 * k, 128 * (k + 1))
            xrk, xik, grk, gik = xr_ref[:, sl], xi_ref[:, sl], gr[:, sl], gi[:, sl]
            xsr = jnp.where(row0, xpr_ref[7:8, sl] * keep, pltpu.roll(xrk, 1, axis=0))
            xsi = jnp.where(row0, xpi_ref[7:8, sl] * keep, pltpu.roll(xik, 1, axis=0))
            dare_ref[:, sl] += jnp.sum(grk * xsr + gik * xsi, axis=0, keepdims=True)
            daim_ref[:, sl] += jnp.sum(gik * xsr - grk * xsi, axis=0, keepdims=True)
            grb, gib = grk.astype(BF16), gik.astype(BF16)
            du_ref[:, ch] = (_dot(grb, bre_ref[k], NT) + _dot(gib, bim_ref[k], NT)
                             + d_ref[:, ch] * dyv[:, ch]).astype(BF16)
            dbre_ref[k] += _dot(grb, ub[:, ch], TN)
            dbim_ref[k] += _dot(gib, ub[:, ch], TN)
            dcre_ref[k] += _dot(dyb[:, ch], xrk.astype(BF16), TN)
            dcim_ref[k] -= _dot(dyb[:, ch], xik.astype(BF16), TN)

    rev = lambda i: (nc - 1 - i, 0)
    prev = lambda i: (jnp.maximum((nc - 1 - i) * rb - 1, 0), 0)
    return _pc(
        body, job, name=name, grid=(nc,),
        in_specs=[pl.BlockSpec((tc, D_SSM), rev), pl.BlockSpec((tc, D_SSM), lambda i: (nc - 1 - i, ucol)),
                  pl.BlockSpec((tc, N_STATE), rev), pl.BlockSpec((tc, N_STATE), rev),
                  pl.BlockSpec((8, N_STATE), prev), pl.BlockSpec((8, N_STATE), prev),
                  _full((SSM_BLOCKS, 128, 512)), _full((SSM_BLOCKS, 128, 512)),
                  _full((SSM_BLOCKS, 512, 128)), _full((SSM_BLOCKS, 512, 128)),
                  _full((1, N_STATE)), _full((1, N_STATE)), _full((1, D_SSM)), pl.BlockSpec(memory_space=pl.ANY)],
        out_specs=[pl.BlockSpec((tc, D_SSM), lambda i: (nc - 1 - i, ucol)),
                   _full((SSM_BLOCKS, 512, 128)), _full((SSM_BLOCKS, 512, 128)),
                   _full((SSM_BLOCKS, 128, 512)), _full((SSM_BLOCKS, 128, 512)),
                   _full((1, N_STATE)), _full((1, N_STATE)), _full((1, D_SSM))],
        out_shape=[jax.ShapeDtypeStruct(dproj.shape, BF16),
                   jax.ShapeDtypeStruct((SSM_BLOCKS, 512, 128), F32), jax.ShapeDtypeStruct((SSM_BLOCKS, 512, 128), F32),
                   jax.ShapeDtypeStruct((SSM_BLOCKS, 128, 512), F32), jax.ShapeDtypeStruct((SSM_BLOCKS, 128, 512), F32),
                   jax.ShapeDtypeStruct((1, N_STATE), F32), jax.ShapeDtypeStruct((1, N_STATE), F32),
                   jax.ShapeDtypeStruct((1, D_SSM), F32)],
        scratch_shapes=[pltpu.VMEM((tc, N_STATE), F32), pltpu.VMEM((tc, N_STATE), F32),
                        pltpu.VMEM((1, N_STATE), F32), pltpu.VMEM((1, N_STATE), F32)], sem=("arbitrary",),
        operands=(dy, proj, xr, xi, xr, xi, bre, bim, cre, cimn, are, aim, d, dproj), aliases={13: 0})


def _rel_bucket(dist):
    n = jnp.maximum(dist, 0)
    max_exact = NUM_BUCKETS // 2
    n_f = jnp.maximum(n, 1).astype(F32)
    large = max_exact + (jnp.log(n_f / max_exact) / math.log(REL_MAX_DISTANCE / max_exact)
                         * (NUM_BUCKETS - max_exact)).astype(jnp.int32)
    large = jnp.minimum(large, NUM_BUCKETS - 1)
    return jnp.where(n < max_exact, n, large)


def _bucket_tables():
    qi = jnp.arange(ATTN_BLOCK)[:, None]
    kj = jnp.arange(2 * ATTN_BLOCK)[None, :]
    delta = jnp.maximum(ATTN_BLOCK + qi - kj, 0)
    return jnp.stack([_rel_bucket(delta * r) for r in ATTN_DILATIONS]).astype(jnp.int32)


def _bias_tables(rel_bias, buckets, name):
    def body(tab_ref, bk_ref, o_ref):
        g = pl.program_id(0)
        bk = bk_ref[...]
        qi = lax.broadcasted_iota(jnp.int32, bk.shape, 0)
        kj = lax.broadcasted_iota(jnp.int32, bk.shape, 1)
        delta = ATTN_BLOCK + qi - kj
        band = (delta >= 0) & (delta <= ATTN_SPAN)
        accs = [jnp.zeros(bk.shape, F32) for _ in range(4)]
        for b in range(NUM_BUCKETS):
            hit = bk == b
            for h in range(4):
                accs[h] = jnp.where(hit, tab_ref[b, 4 * g + h], accs[h])
        for h in range(4):
            o_ref[h] = jnp.where(band, accs[h], NEG_INF)

    return pl.pallas_call(
        body, name=name, grid=(3,),
        in_specs=[pl.BlockSpec(memory_space=pltpu.SMEM),
                  pl.BlockSpec((None, ATTN_BLOCK, 2 * ATTN_BLOCK), lambda g: (g, 0, 0))],
        out_specs=pl.BlockSpec((None, 4, ATTN_BLOCK, 2 * ATTN_BLOCK), lambda g: (g, 0, 0, 0)),
        out_shape=jax.ShapeDtypeStruct((3, 4, ATTN_BLOCK, 2 * ATTN_BLOCK), F32),
        compiler_params=_params(("parallel",)),
    )(rel_bias, buckets)


def _bias_grad(db0, db1, buckets, name):
    def body(a_ref, b_ref, bk_ref, o_ref):
        bk = bk_ref[...]
        for h in range(4):
            dv = a_ref[h] + b_ref[h]
            for b in range(NUM_BUCKETS):
                o_ref[h, b:b + 1, :] = jnp.sum(jnp.where(bk == b, dv, 0.0), axis=0, keepdims=True)

    tab = pl.BlockSpec((None, 4, ATTN_BLOCK, 2 * ATTN_BLOCK), lambda g: (g, 0, 0, 0))
    return pl.pallas_call(
        body, name=name, grid=(3,),
        in_specs=[tab, tab, pl.BlockSpec((None, ATTN_BLOCK, 2 * ATTN_BLOCK), lambda g: (g, 0, 0))],
        out_specs=pl.BlockSpec((None, 4, NUM_BUCKETS, 2 * ATTN_BLOCK), lambda g: (g, 0, 0, 0)),
        out_shape=jax.ShapeDtypeStruct((3, 4, NUM_BUCKETS, 2 * ATTN_BLOCK), F32),
        compiler_params=_params(("parallel",)),
    )(db0, db1, buckets)


_ATTN_SUB = {1: 4, 4: 1, 16: 1}
_UNROLL = 4


def _unit_rows(j, s, r):
    start = j * ATTN_BLOCK * r + s
    return pl.ds(start, ATTN_BLOCK, stride=r) if r > 1 else pl.ds(start, ATTN_BLOCK)


def _for_units(r, nsub, fn, after):
    if r * nsub <= _UNROLL:
        after([fn(j, s) for j in range(nsub) for s in range(r)])
    else:
        def four(i, c):
            after([fn(0, _UNROLL * i + k) for k in range(_UNROLL)])
            return c

        lax.fori_loop(0, r // _UNROLL, four, 0)


def _attn_cols(g):
    return tuple((_OFF[n] + ATTN_GROUP_WIDTH * g) // LANES for n in ("q", "k", "v"))


def _attn_fwd(proj, bias, g, name, job=None):
    r = ATTN_DILATIONS[g]
    nsub = _ATTN_SUB[r]
    T = proj.shape[0]
    sub = ATTN_BLOCK * r
    tb = sub * nsub
    qc, kc, vc = _attn_cols(g)
    scale = ATTN_HEAD_DIM ** -0.5

    def body(q_ref, kc_ref, kp_ref, vc_ref, vp_ref, bias_ref, o_ref, lse_ref):
        lane = lax.broadcasted_iota(jnp.int32, (ATTN_BLOCK, LANES), 1)
        kj = lax.broadcasted_iota(jnp.int32, (ATTN_BLOCK, 2 * ATTN_BLOCK), 1)
        dead = (pl.program_id(0) == 0) & (kj < ATTN_BLOCK)

        def one(j, s):
            rows = _unit_rows(j, s, r)
            before = _unit_rows(max(j - 1, 0), s, r)
            k_before = kc_ref[before, :] if j else kp_ref[before, :]
            v_before = vc_ref[before, :] if j else vp_ref[before, :]
            q = q_ref[rows, :]
            kcat = jnp.concatenate([k_before, kc_ref[rows, :]], axis=0).astype(BF16)
            vcat = jnp.concatenate([v_before, vc_ref[rows, :]], axis=0).astype(BF16)
            o_acc = jnp.zeros((ATTN_BLOCK, LANES), F32)
            l_acc = jnp.zeros((ATTN_BLOCK, LANES), F32)
            for hh in range(2):
                mine = (lane >= ATTN_HEAD_DIM) if hh else (lane < ATTN_HEAD_DIM)
                qm = jnp.where(mine, q, 0.0).astype(BF16)
                sc = _dot(qm, kcat, NT) * scale + bias_ref[hh]
                if j == 0:
                    sc = jnp.where(dead, NEG_INF, sc)
                m = jnp.max(sc, axis=-1, keepdims=True)
                p = jnp.exp(sc - m)
                l = jnp.sum(p, axis=-1, keepdims=True)
                o_acc = jnp.where(mine, _dot((p / l).astype(BF16), vcat), o_acc)
                l_acc = jnp.where(mine, m + jnp.log(l), l_acc)
            o_ref[rows, :] = o_acc
            lse_ref[rows, :] = l_acc

        _for_units(r, nsub, one, lambda results: None)

    cur = lambda c: pl.BlockSpec((tb, LANES), lambda b, p: (b, c + p))
    prev = lambda c: pl.BlockSpec((sub, LANES), lambda b, p: (jnp.maximum(b * nsub - 1, 0), c + p))
    out = pl.BlockSpec((tb, LANES), lambda b, p: (b, p))
    return _pc(
        body, job, name=name, grid=(T // tb, 2),
        in_specs=[cur(qc), cur(kc), prev(kc), cur(vc), prev(vc),
                  pl.BlockSpec((2, ATTN_BLOCK, 2 * ATTN_BLOCK), lambda b, p: (p, 0, 0))],
        out_specs=[out, out],
        out_shape=[jax.ShapeDtypeStruct((T, ATTN_GROUP_WIDTH), F32), jax.ShapeDtypeStruct((T, ATTN_GROUP_WIDTH), F32)],
        scratch_shapes=[], sem=("parallel", "parallel"), operands=(proj, proj, proj, proj, proj, bias))


def _attn_bwd(proj, do, corr, lse, bias, dproj, g, name):
    r = ATTN_DILATIONS[g]
    nsub = _ATTN_SUB[r]
    T = proj.shape[0]
    sub = ATTN_BLOCK * r
    tb = sub * nsub
    nb = T // tb
    qc, kc, vc = _attn_cols(g)
    dc = ATTN_GROUP_WIDTH * g // LANES
    scale = ATTN_HEAD_DIM ** -0.5

    def body(q_ref, kc_ref, kp_ref, vc_ref, vp_ref, do_ref, corr_ref, lse_ref, bias_ref, _,
             dproj_ref, db_ref, dq_s, dkc_s, dkp_s, dvc_s, dvp_s, kacc, vacc, stage, stage_sems):
        p, b = pl.program_id(0), pl.program_id(1)

        def to_dproj(e, slot, block, col):
            rows = pl.ds(pl.multiple_of(block * tb, tb), tb)
            cols = pl.ds(pl.multiple_of((col + p) * LANES, LANES), LANES)
            return pltpu.make_async_copy(stage.at[e, slot], dproj_ref.at[rows, cols], stage_sems.at[e, slot])

        def emit(e, block, col, value):
            count = p * nb + block
            slot = count % 2

            @pl.when(count >= 2)
            def _():
                to_dproj(e, slot, 0, col).wait()

            stage[e, slot] = value.astype(BF16)
            to_dproj(e, slot, block, col).start()

        def emit_keys(block):
            emit(1, block, kc, kacc[...])
            emit(2, block, vc, vacc[...])

        @pl.when(b == 0)
        def _():
            db_ref[...] = jnp.zeros_like(db_ref)
            kacc[...] = jnp.zeros_like(kacc)
            vacc[...] = jnp.zeros_like(vacc)

        @pl.when(b == nb)
        def _():
            emit_keys(nb - 1)

        @pl.when((b == nb) & (p == 1))
        def _():
            for e, col in enumerate((qc, kc, vc)):
                for slot in range(2):
                    to_dproj(e, slot, 0, col).wait()

        @pl.when(b < nb)
        def _():
            lane = lax.broadcasted_iota(jnp.int32, (ATTN_BLOCK, LANES), 1)
            kj = lax.broadcasted_iota(jnp.int32, (ATTN_BLOCK, 2 * ATTN_BLOCK), 1)
            dead = (b == 0) & (kj < ATTN_BLOCK)

            def one(j, s):
                rows = _unit_rows(j, s, r)
                before = _unit_rows(max(j - 1, 0), s, r)
                k_before = kc_ref[before, :] if j else kp_ref[before, :]
                v_before = vc_ref[before, :] if j else vp_ref[before, :]
                q = q_ref[rows, :]
                kcat = jnp.concatenate([k_before, kc_ref[rows, :]], axis=0).astype(BF16)
                vcat = jnp.concatenate([v_before, vc_ref[rows, :]], axis=0).astype(BF16)
                dov, corrv, lsev = do_ref[rows, :], corr_ref[rows, :], lse_ref[rows, :]
                dq_acc = jnp.zeros((ATTN_BLOCK, LANES), F32)
                dk_acc = jnp.zeros((2 * ATTN_BLOCK, LANES), F32)
                dv_acc = jnp.zeros((2 * ATTN_BLOCK, LANES), F32)
                dss = []
                for hh in range(2):
                    mine = (lane >= ATTN_HEAD_DIM) if hh else (lane < ATTN_HEAD_DIM)
                    col = slice(ATTN_HEAD_DIM * hh, ATTN_HEAD_DIM * hh + 1)
                    qm = jnp.where(mine, q, 0.0).astype(BF16)
                    dom = jnp.where(mine, dov, 0.0).astype(BF16)
                    sc = _dot(qm, kcat, NT) * scale + bias_ref[hh]
                    if j == 0:
                        sc = jnp.where(dead, NEG_INF, sc)
                    p = jnp.exp(sc - lsev[:, col])
                    ds = p * (_dot(dom, vcat, NT) - corrv[:, col])
                    dss.append(ds)
                    dsb = ds.astype(BF16)
                    dq_acc = jnp.where(mine, _dot(dsb, kcat) * scale, dq_acc)
                    dk_acc += _dot(dsb, qm, TN) * scale
                    dv_acc += _dot(p.astype(BF16), dom, TN)
                dq_s[rows, :] = dq_acc
                dkp_s[rows, :] = dk_acc[:ATTN_BLOCK]
                dkc_s[rows, :] = dk_acc[ATTN_BLOCK:]
                dvp_s[rows, :] = dv_acc[:ATTN_BLOCK]
                dvc_s[rows, :] = dv_acc[ATTN_BLOCK:]
                return dss

            def add_bias_grads(results):
                for hh in range(2):
                    db_ref[hh] += functools.reduce(lambda x, y: x + y, [dss[hh] for dss in results])

            _for_units(r, nsub, one, add_bias_grads)
            emit(0, b, qc, dq_s[...])
            tail = slice((nsub - 1) * sub, nsub * sub)
            kacc[tail, :] += dkp_s[0:sub, :]
            vacc[tail, :] += dvp_s[0:sub, :]

            @pl.when(b >= 1)
            def _():
                emit_keys(b - 1)

            for acc, before_s, cur_s in ((kacc, dkp_s, dkc_s), (vacc, dvp_s, dvc_s)):
                acc[...] = cur_s[...]
                for j in range(nsub - 1):
                    acc[j * sub:(j + 1) * sub, :] += before_s[(j + 1) * sub:(j + 2) * sub, :]

    last = nb - 1
    blk = (tb, LANES)
    cur = lambda c: pl.BlockSpec(blk, lambda p, b: (jnp.minimum(b, last), c + p))
    before = lambda c: pl.BlockSpec((sub, LANES), lambda p, b: (jnp.clip(b * nsub - 1, 0, nb * nsub - 1), c + p))
    tab = pl.BlockSpec((2, ATTN_BLOCK, 2 * ATTN_BLOCK), lambda p, b: (p, 0, 0))
    hbm = pl.BlockSpec(memory_space=pl.ANY)
    return pl.pallas_call(
        body, name=name, grid=(2, nb + 1),
        in_specs=[cur(qc), cur(kc), before(kc), cur(vc), before(vc), cur(dc), cur(dc), cur(0), tab, hbm],
        out_specs=[hbm, tab],
        out_shape=[jax.ShapeDtypeStruct(dproj.shape, BF16), jax.ShapeDtypeStruct((4, ATTN_BLOCK, 2 * ATTN_BLOCK), F32)],
        input_output_aliases={9: 0},
        scratch_shapes=[pltpu.VMEM(blk, F32)] * 7 + [pltpu.VMEM((3, 2) + blk, BF16), pltpu.SemaphoreType.DMA((3, 2))],
        compiler_params=_params(("arbitrary", "arbitrary")),
    )(proj, proj, proj, proj, proj, do, corr, lse, bias, dproj)


def _mix_weights(lses):
    m = jnp.maximum(jnp.maximum(lses[0], lses[1]), lses[2])
    es = [jnp.exp(l - m) for l in lses]
    inv = 1.0 / (es[0] + es[1] + es[2])
    return jnp.concatenate([e * inv for e in es], axis=1)


def _attn_mix(os, lses, proj, tm, name):
    T = proj.shape[0]
    zcol = _OFF["z_attn"] // D_ATTN

    def body(o0, o1, o2, l0, l1, l2, z_ref, out_ref):
        z = z_ref[...]
        o = jnp.concatenate([o0[...], o1[...], o2[...]], axis=1)
        alpha = _mix_weights([l0[...], l1[...], l2[...]])
        out_ref[...] = (o * alpha * (z * _sigmoid(z))).astype(BF16)

    row = lambda i: (i, 0)
    grp = pl.BlockSpec((tm, ATTN_GROUP_WIDTH), row)
    return pl.pallas_call(
        body, name=name, grid=(T // tm,),
        in_specs=[grp] * 6 + [pl.BlockSpec((tm, D_ATTN), lambda i: (i, zcol))],
        out_specs=pl.BlockSpec((tm, D_ATTN), row),
        out_shape=jax.ShapeDtypeStruct((T, D_ATTN), BF16),
        compiler_params=_params(("parallel",)),
    )(*os, *lses, proj)


def _attn_mix_bwd(d, os, lses, proj, dproj, tm, name):
    T = proj.shape[0]
    zcol = _OFF["z_attn"] // D_ATTN

    def body(d_ref, o0, o1, o2, l0, l1, l2, z_ref, _, do_ref, corr_ref, dz_ref):
        dv, z = d_ref[...], z_ref[...]
        ov = jnp.concatenate([o0[...], o1[...], o2[...]], axis=1)
        alpha = _mix_weights([l0[...], l1[...], l2[...]])
        sz = _sigmoid(z)
        oc = ov * alpha
        dz_ref[...] = (dv * oc * (sz * (1.0 + z * (1.0 - sz)))).astype(BF16)
        doc = dv * (z * sz)
        do_ref[...] = doc * alpha
        pr = doc * oc
        p3 = pr[:, 0:256] + pr[:, 256:512] + pr[:, 512:768]
        li = lax.broadcasted_iota(jnp.int32, (256, 256), 0) // ATTN_HEAD_DIM
        lj = lax.broadcasted_iota(jnp.int32, (256, 256), 1) // ATTN_HEAD_DIM
        ones = jnp.where(li == lj, 1.0, 0.0).astype(F32)
        s = lax.dot_general(p3, ones, NN, precision=lax.Precision.HIGHEST, preferred_element_type=F32)
        corr_ref[...] = alpha * jnp.concatenate([s, s, s], axis=1)

    row = lambda i: (i, 0)
    grp = pl.BlockSpec((tm, ATTN_GROUP_WIDTH), row)
    return pl.pallas_call(
        body, name=name, grid=(T // tm,),
        in_specs=[pl.BlockSpec((tm, D_ATTN), row)] + [grp] * 6 + [pl.BlockSpec((tm, D_ATTN), lambda i: (i, zcol)),
                                                                    pl.BlockSpec(memory_space=pl.ANY)],
        out_specs=[pl.BlockSpec((tm, D_ATTN), row)] * 2 + [pl.BlockSpec((tm, D_ATTN), lambda i: (i, zcol))],
        out_shape=[jax.ShapeDtypeStruct((T, D_ATTN), F32), jax.ShapeDtypeStruct((T, D_ATTN), F32),
                   jax.ShapeDtypeStruct(dproj.shape, BF16)],
        input_output_aliases={8: 2},
        compiler_params=_params(("parallel",)),
    )(d, *os, *lses, proj, dproj)


def _mem_probs(q_ref, kv_ref, h):
    hs = slice(MEM_HEAD_DIM * h, MEM_HEAD_DIM * (h + 1))
    qh = q_ref[:, hs].astype(BF16)
    kh = kv_ref[:, hs]
    vh = kv_ref[:, D_MEM + MEM_HEAD_DIM * h:D_MEM + MEM_HEAD_DIM * (h + 1)]
    s = _dot(qh, kh, NT) * (MEM_HEAD_DIM ** -0.5)
    p = jnp.exp(s - jnp.max(s, axis=-1, keepdims=True))
    pn = p / jnp.sum(p, axis=-1, keepdims=True)
    return qh, kh, vh, pn


def _mem_fwd(proj, kv, tm, name):
    T = proj.shape[0]
    M = kv.shape[0]
    qcol, zcol = _OFF["q_mem"] // D_MEM, _OFF["z_mem"] // D_MEM

    def body(q_ref, z_ref, kv_ref, o_ref):
        outs = []
        for h in range(MEM_HEADS):
            _, _, vh, pn = _mem_probs(q_ref, kv_ref, h)
            outs.append(_dot(pn.astype(BF16), vh))
        z = z_ref[...]
        o_ref[...] = (jnp.concatenate(outs, axis=1) * (z * _sigmoid(z))).astype(BF16)

    return pl.pallas_call(
        body, name=name, grid=(T // tm,),
        in_specs=[pl.BlockSpec((tm, D_MEM), lambda i: (i, qcol)), pl.BlockSpec((tm, D_MEM), lambda i: (i, zcol)),
                  _full((M, 2 * D_MEM))],
        out_specs=pl.BlockSpec((tm, D_MEM), lambda i: (i, 0)),
        out_shape=jax.ShapeDtypeStruct((T, D_MEM), BF16),
        compiler_params=_params(("parallel",)),
    )(proj, proj, kv)


def _mem_bwd(d, proj, kv, dproj, tm, name):
    T = proj.shape[0]
    M = kv.shape[0]
    qcol, zcol = _OFF["q_mem"] // D_MEM, _OFF["z_mem"] // D_MEM
    assert zcol == qcol + 1

    def body(d_ref, q_ref, z_ref, kv_ref, _, out_ref, dkv_ref, dz_s):
        @pl.when((pl.program_id(0) == 0) & (pl.program_id(1) == 0))
        def _():
            dkv_ref[...] = jnp.zeros_like(dkv_ref)

        @pl.when(pl.program_id(1) == 0)
        def _():
            compute(d_ref, q_ref, z_ref, kv_ref, out_ref, dz_s, dkv_ref)

        @pl.when(pl.program_id(1) == 1)
        def _():
            out_ref[...] = dz_s[...]

    def compute(d_ref, q_ref, z_ref, kv_ref, dq_ref, dz_ref, dkv_ref):
        z = z_ref[...]
        sz = _sigmoid(z)
        dv = d_ref[...]
        dov = dv * (z * sz)
        scale = MEM_HEAD_DIM ** -0.5
        outs, dqs = [], []
        for h in range(MEM_HEADS):
            hs = slice(MEM_HEAD_DIM * h, MEM_HEAD_DIM * (h + 1))
            qh, kh, vh, pn = _mem_probs(q_ref, kv_ref, h)
            pnb = pn.astype(BF16)
            oh = _dot(pnb, vh)
            outs.append(oh)
            doh = dov[:, hs]
            dohb = doh.astype(BF16)
            dp = _dot(dohb, vh, NT)
            ds = pn * (dp - jnp.sum(doh * oh, axis=-1, keepdims=True))
            dsb = ds.astype(BF16)
            dqs.append(_dot(dsb, kh) * scale)
            dkv_ref[:, hs] += _dot(dsb, qh, TN) * scale
            vs = slice(D_MEM + MEM_HEAD_DIM * h, D_MEM + MEM_HEAD_DIM * (h + 1))
            dkv_ref[:, vs] += _dot(pnb, dohb, TN)
        dq_ref[...] = jnp.concatenate(dqs, axis=1).astype(BF16)
        dz_ref[...] = (dv * jnp.concatenate(outs, axis=1) * (sz * (1.0 + z * (1.0 - sz)))).astype(BF16)

    return pl.pallas_call(
        body, name=name, grid=(T // tm, 2),
        in_specs=[pl.BlockSpec((tm, D_MEM), lambda i, j: (i, 0)), pl.BlockSpec((tm, D_MEM), lambda i, j: (i, qcol)),
                  pl.BlockSpec((tm, D_MEM), lambda i, j: (i, zcol)), _full((M, 2 * D_MEM)),
                  pl.BlockSpec(memory_space=pl.ANY)],
        out_specs=[pl.BlockSpec((tm, D_MEM), lambda i, j: (i, qcol + j)), _full((M, 2 * D_MEM))],
        out_shape=[jax.ShapeDtypeStruct(dproj.shape, BF16), jax.ShapeDtypeStruct((M, 2 * D_MEM), F32)],
        input_output_aliases={4: 0},
        scratch_shapes=[pltpu.VMEM((tm, D_MEM), BF16)],
        compiler_params=_params(("arbitrary", "arbitrary")),
    )(d, proj, proj, kv, dproj)


def _branches_and_gates(os_ref, oa_ref, om_ref, gl_refs, bg_ref, ws_ref, wa_ref, wm_ref):
    outs = (_dot(os_ref[...], ws_ref[...]), _dot(oa_ref[...], wa_ref[...]), _dot(om_ref[...], wm_ref[...]))
    gates = tuple(_sigmoid(jnp.concatenate([gl_refs[2 * k][...], gl_refs[2 * k + 1][...]], axis=1)
                           + bg_ref[:, D_MODEL * k:D_MODEL * (k + 1)]) for k in range(3))
    return outs, gates


def _merge_specs(tm):
    row = lambda i: (i, 0)
    first = _OFF["gates"] // GATE_BLOCK
    gate = [pl.BlockSpec((tm, GATE_BLOCK), (lambda i, k=k: (i, first + k))) for k in range(N_GATES // GATE_BLOCK)]
    return ([pl.BlockSpec((tm, D_SSM), row), pl.BlockSpec((tm, D_ATTN), row), pl.BlockSpec((tm, D_MEM), row)] + gate
            + [_full((1, N_GATES)), _full((D_SSM, D_MODEL)), _full((D_ATTN, D_MODEL)), _full((D_MEM, D_MODEL)),
               _full((D_MODEL, D_MODEL))])


def _merge_fwd(x, o_ssm, o_attn, o_mem, proj, bg, ws, wa, wm, wo, tm, name):
    T = x.shape[0]

    def body(os_ref, oa_ref, om_ref, g0, g1, g2, g3, g4, g5, bg_ref, ws_ref, wa_ref, wm_ref, wo_ref, x_ref,
             xo_ref, mg_ref):
        outs, gates = _branches_and_gates(os_ref, oa_ref, om_ref, (g0, g1, g2, g3, g4, g5), bg_ref, ws_ref, wa_ref,
                                          wm_ref)
        merged = (gates[0] * outs[0] + gates[1] * outs[1] + gates[2] * outs[2]).astype(BF16)
        mg_ref[...] = merged
        xo_ref[...] = x_ref[...] + _dot(merged, wo_ref[...])

    row = lambda i: (i, 0)
    return pl.pallas_call(
        body, name=name, grid=(T // tm,),
        in_specs=_merge_specs(tm) + [pl.BlockSpec((tm, D_MODEL), row)],
        out_specs=[pl.BlockSpec((tm, D_MODEL), row), pl.BlockSpec((tm, D_MODEL), row)],
        out_shape=[jax.ShapeDtypeStruct((T, D_MODEL), F32), jax.ShapeDtypeStruct((T, D_MODEL), BF16)],
        compiler_params=_params(("parallel",)),
    )(o_ssm, o_attn, o_mem, *([proj] * (N_GATES // GATE_BLOCK)), bg, ws, wa, wm, wo, x)


def _merge_bwd(dx, o_ssm, o_attn, o_mem, proj, bg, ws, wa, wm, wo, tm, name, job=None):
    T = dx.shape[0]

    n = T // tm

    def body(os_ref, oa_ref, om_ref, g0, g1, g2, g3, g4, g5, bg_ref, ws_ref, wa_ref, wm_ref, wo_ref, dx_ref,
             dproj_ref, db_ref, dos_ref, doa_ref, dom_ref, dbg_ref, dgl_buf, dgl_sems):
        i = pl.program_id(0)
        slot = i % 2

        def to_dproj(s, row0):
            return pltpu.make_async_copy(dgl_buf.at[s], dproj_ref.at[pl.ds(row0, tm), pl.ds(_OFF["gates"], N_GATES)],
                                         dgl_sems.at[s])

        @pl.when(i == 0)
        def _():
            dbg_ref[...] = jnp.zeros_like(dbg_ref)

        @pl.when(i >= 2)
        def _():
            to_dproj(slot, 0).wait()

        outs, gates = _branches_and_gates(os_ref, oa_ref, om_ref, (g0, g1, g2, g3, g4, g5), bg_ref, ws_ref, wa_ref,
                                          wm_ref)
        dm = _dot(dx_ref[...].astype(BF16), wo_ref[...], NT)
        w_refs = (ws_ref, wa_ref, wm_ref)
        do_refs = (dos_ref, doa_ref, dom_ref)
        for k in range(3):
            cols = slice(D_MODEL * k, D_MODEL * (k + 1))
            dgl = dm * outs[k] * (gates[k] * (1.0 - gates[k]))
            dgl_buf[slot, :, cols] = dgl.astype(BF16)
            dbg_ref[:, cols] += jnp.sum(dgl, axis=0, keepdims=True)
            dbk = (dm * gates[k]).astype(BF16)
            db_ref[:, cols] = dbk
            do_refs[k][...] = _dot(dbk, w_refs[k][...], NT)
        to_dproj(slot, pl.multiple_of(i * tm, tm)).start()

        @pl.when(i == n - 1)
        def _():
            for s in range(min(2, n)):
                to_dproj(s, 0).wait()

    row = lambda i: (i, 0)
    return _pc(
        body, job, name=name, grid=(n,),
        in_specs=_merge_specs(tm) + [pl.BlockSpec((tm, D_MODEL), row)],
        out_specs=[pl.BlockSpec(memory_space=pl.ANY), pl.BlockSpec((tm, N_GATES), row), pl.BlockSpec((tm, D_SSM), row),
                   pl.BlockSpec((tm, D_ATTN), row), pl.BlockSpec((tm, D_MEM), row), _full((1, N_GATES))],
        out_shape=[jax.ShapeDtypeStruct((T, D_IN), BF16), jax.ShapeDtypeStruct((T, N_GATES), BF16),
                   jax.ShapeDtypeStruct((T, D_SSM), F32), jax.ShapeDtypeStruct((T, D_ATTN), F32),
                   jax.ShapeDtypeStruct((T, D_MEM), F32), jax.ShapeDtypeStruct((1, N_GATES), F32)],
        scratch_shapes=[pltpu.VMEM((2, tm, N_GATES), BF16), pltpu.SemaphoreType.DMA((2,))], sem=("arbitrary",),
        operands=(o_ssm, o_attn, o_mem, *([proj] * (N_GATES // GATE_BLOCK)), bg, ws, wa, wm, wo, dx))


def _loss_head(x, g, target, tm, name):
    T, D = x.shape

    def body(x_ref, g_ref, t_ref, loss_ref, dx_ref, dg_ref):
        @pl.when(pl.program_id(0) == 0)
        def _():
            loss_ref[...] = jnp.zeros_like(loss_ref)
            dg_ref[...] = jnp.zeros_like(dg_ref)

        xv = x_ref[...]
        r = lax.rsqrt(jnp.mean(xv * xv, axis=-1, keepdims=True) + EPS)
        xr = xv * r
        err = xr * g_ref[...] - t_ref[...]
        loss_ref[...] += 0.5 * jnp.sum(jnp.mean(err * err, axis=-1, keepdims=True), axis=0, keepdims=True)
        dy = err * (1.0 / D)
        dg_ref[...] += jnp.sum(dy * xr, axis=0, keepdims=True)
        wv = dy * g_ref[...]
        dx_ref[...] = r * (wv - xr * jnp.mean(wv * xr, axis=-1, keepdims=True))

    row = lambda i: (i, 0)
    return pl.pallas_call(
        body, name=name, grid=(T // tm,),
        in_specs=[pl.BlockSpec((tm, D), row), _full((1, D)), pl.BlockSpec((tm, D), row)],
        out_specs=[_full((1, 128)), pl.BlockSpec((tm, D), row), _full((1, D))],
        out_shape=[jax.ShapeDtypeStruct((1, 128), F32), jax.ShapeDtypeStruct((T, D), F32),
                   jax.ShapeDtypeStruct((1, D), F32)],
        compiler_params=_params(("arbitrary",)),
    )(x, g, target)


def _adamw(parts, w, m, v, tr, name):
    L, R, C = w.shape

    def body(p_ref, w_ref, m_ref, v_ref, g_ref, d_ref, mo_ref, vo_ref):
        g = p_ref[0].astype(F32)
        for s in range(1, N_DEV):
            g = g + p_ref[s].astype(F32)
        mn = ADAM_B1 * m_ref[...] + (1.0 - ADAM_B1) * g
        vn = ADAM_B2 * v_ref[...] + (1.0 - ADAM_B2) * (g * g)
        m_hat = mn / (1.0 - ADAM_B1 ** ADAM_STEP)
        v_hat = vn / (1.0 - ADAM_B2 ** ADAM_STEP)
        g_ref[...] = g
        d_ref[...] = -ADAM_LR * (m_hat / (jnp.sqrt(v_hat) + ADAM_EPS) + ADAM_WD * w_ref[...])
        mo_ref[...] = mn
        vo_ref[...] = vn

    one = pl.BlockSpec((None, tr, C), lambda l, i: (l, i, 0))
    return pl.pallas_call(
        body, name=name, grid=(L, R // tr),
        in_specs=[pl.BlockSpec((N_DEV, None, tr, C), lambda l, i: (0, l, i, 0)), one, one, one],
        out_specs=[one] * 4,
        out_shape=[jax.ShapeDtypeStruct((L, R, C), F32)] * 4,
        compiler_params=_params(("parallel", "parallel")),
    )(parts, w, m, v)


_SHARDED = (("w_in", (1088, 1024), 1), ("w_glu", (96, 768), 0), ("w_mem_kv", (128, 1024), 0),
            ("w_br_ssm", (768, 128), 1), ("w_br_attn", (768, 128), 1), ("w_br_mem", (512, 128), 1),
            ("w_out", (128, 1024), 0))
_W_IN = 0
_SMALL = tuple(range(1, len(_SHARDED)))


class _Job(NamedTuple):
    ins: list
    out_shape: list
    aliases: dict
    pairs: Callable
    n: int


def _peers():
    x, y, c = lax.axis_index("x"), lax.axis_index("y"), lax.axis_index("c")
    me = 4 * x + 2 * y + c
    out = []
    for k in range(1, N_DEV):
        px = 1 - x if k & 4 else x
        py = 1 - y if k & 2 else y
        pc = 1 - c if k & 1 else c
        out.append(((px, py, pc), 4 * px + 2 * py + pc))
    return me, out


def _copies(pairs, send_sems, recv_sems, local_sems, arrivals):
    me, peers = _peers()
    local = [pltpu.make_async_copy(src(me), dst(me), local_sems.at[j]) for j, (src, dst) in enumerate(pairs)]
    sends, recvs = [], []
    for k, (peer, lin) in enumerate(peers):
        for j, (src, dst) in enumerate(pairs):
            for to, out in ((dst(me), sends), (dst(lin), recvs)):
                if out is sends or arrivals:
                    out.append(pltpu.make_async_remote_copy(
                        src_ref=src(lin), dst_ref=to, send_sem=send_sems.at[j, k], recv_sem=recv_sems.at[j, k],
                        device_id=peer, device_id_type=pl.DeviceIdType.MESH))
    return local, sends, recvs


def _start_copies(pairs, *sems):
    local, sends, _ = _copies(pairs, *sems, arrivals=False)
    for cp in local + sends:
        cp.start()


def _wait_copies(pairs, *sems):
    local, sends, recvs = _copies(pairs, *sems, arrivals=True)
    for cp in recvs:
        cp.wait_recv()
    for cp in sends:
        cp.wait_send()
    for cp in local:
        cp.wait()


def _job_scratch(job):
    return [pltpu.SemaphoreType.DMA((job.n, N_DEV - 1)), pltpu.SemaphoreType.DMA((job.n, N_DEV - 1)),
            pltpu.SemaphoreType.DMA((job.n,))]


def _pc(body, job, *, name, grid, in_specs, out_specs, out_shape, scratch_shapes, sem, operands, aliases=None):
    aliases = aliases or {}
    if job is None:
        return pl.pallas_call(body, name=name, grid=grid, in_specs=in_specs, out_specs=out_specs, out_shape=out_shape,
                              scratch_shapes=scratch_shapes, input_output_aliases=aliases,
                              compiler_params=_params(sem))(*operands)
    a = len(in_specs)
    b = a + len(job.ins)
    c = b + len(out_shape)
    d = c + len(job.out_shape)
    e = d + len(scratch_shapes)

    def carried(*refs):
        pairs = job.pairs(refs[a:b], refs[c:d])
        ids = [pl.program_id(k) for k in range(len(grid))]
        first = functools.reduce(jnp.logical_and, [i == 0 for i in ids])
        last = functools.reduce(jnp.logical_and, [i == n - 1 for i, n in zip(ids, grid)])

        @pl.when(first)
        def _():
            _start_copies(pairs, *refs[e:])

        body(*refs[:a], *refs[b:c], *refs[d:e])

        @pl.when(last)
        def _():
            _wait_copies(pairs, *refs[e:])

    hbm = pl.BlockSpec(memory_space=pl.ANY)
    outs = pl.pallas_call(
        carried, name=name, grid=grid,
        in_specs=list(in_specs) + [hbm] * len(job.ins), out_specs=list(out_specs) + [hbm] * len(job.out_shape),
        out_shape=list(out_shape) + list(job.out_shape),
        input_output_aliases={**aliases, **{a + i: len(out_shape) + o for i, o in job.aliases.items()}},
        scratch_shapes=list(scratch_shapes) + _job_scratch(job),
        compiler_params=_params(("arbitrary",) * len(grid)),
    )(*operands, *job.ins)
    return outs[:len(out_shape)], outs[len(out_shape):]


def _gather_via_sibling(x, take, place, out_shape, name, landing=None):
    def body(*refs):
        x_ref, o_ref = refs[0], refs[-4]
        send_sems, recv_sems, local_sem = refs[-3:]
        x, y, c = lax.axis_index("x"), lax.axis_index("y"), lax.axis_index("c")
        me, sibling = (x, y, c), (x, y, 1 - c)
        chips = [(1 - x, y), (x, 1 - y), (1 - x, 1 - y)]
        src = take(x_ref)

        def slot(px, py, pc):
            return place(o_ref, 4 * px + 2 * py + pc)

        def copy(k, block, to, first_hand):
            return pltpu.make_async_remote_copy(
                src_ref=src if first_hand else slot(*block), dst_ref=slot(*block), send_sem=send_sems.at[k],
                recv_sem=recv_sems.at[k], device_id=to, device_id_type=pl.DeviceIdType.MESH)

        mine = pltpu.make_async_copy(src, slot(*me), local_sem)
        mine.start()
        first = [copy(0, me, sibling, True)] + [copy(1 + j, me, (*chip, c), True) for j, chip in enumerate(chips)]
        for cp in first:
            cp.start()
        passed = []
        for j, chip in enumerate(chips):
            copy(1 + j, (*chip, c), me, True).wait_recv()
            passed.append(copy(4 + j, (*chip, c), sibling, False))
            passed[-1].start()
        copy(0, sibling, me, True).wait_recv()
        for j, chip in enumerate(chips):
            copy(4 + j, (*chip, 1 - c), me, False).wait_recv()
        for cp in first + passed:
            cp.wait_send()
        mine.wait()

    hbm = pl.BlockSpec(memory_space=pl.ANY)
    ins = [x] if landing is None else [x, landing]
    return pl.pallas_call(
        body, name=name, in_specs=[hbm] * len(ins), out_specs=hbm, out_shape=out_shape,
        input_output_aliases={} if landing is None else {1: 0},
        scratch_shapes=[pltpu.SemaphoreType.DMA((N_DEV - 1,)), pltpu.SemaphoreType.DMA((N_DEV - 1,)),
                        pltpu.SemaphoreType.DMA],
    )(*ins)


def _lane_window(ref, who):
    return ref.at[:, pl.ds(pl.multiple_of(who * LANES, LANES), LANES)]


def _gather_job(shards, items):
    out_shape = []
    for i, _ in items:
        _, s, axis = _SHARDED[i]
        whole = i != _W_IN and axis == 1
        out_shape.append(jax.ShapeDtypeStruct((s[0], N_DEV * s[1]) if whole else (N_DEV,) + s, BF16))

    def pairs(in_refs, out_refs):
        out = []
        for (i, l), src, dst in zip(items, in_refs, out_refs):
            if i != _W_IN and _SHARDED[i][2] == 1:
                out.append((lambda who, src=src, l=l: src.at[l], lambda who, dst=dst: _lane_window(dst, who)))
            else:
                out.append((lambda who, src=src, l=l: src.at[l], lambda who, dst=dst: dst.at[who]))
        return out

    return _Job([shards[i] for i, _ in items], out_shape, {}, pairs, len(items))


def _landed_weights(items, landed):
    out = {}
    for (i, _), a in zip(items, landed):
        n, s, axis = _SHARDED[i]
        if i == _W_IN:
            out[n] = a.reshape(D_IN, D_MODEL)
        elif axis == 0:
            out[n] = a.reshape(N_DEV * s[0], s[1])
        else:
            out[n] = a
    return out


def _scatter_job(grads, items, layer, parts=None):
    ng = len(grads)
    out_shape = [jax.ShapeDtypeStruct((N_DEV, DEPTH) + _SHARDED[i][1], BF16) for i in items]

    def pairs(in_refs, out_refs):
        out = []
        for i, src, dst in zip(items, in_refs[:ng], out_refs):
            _, s, axis = _SHARDED[i]
            if i == _W_IN:
                take = lambda who, src=src: src.at[who]
            elif axis == 0:
                take = lambda who, src=src, s=s: src.at[pl.ds(pl.multiple_of(who * s[0], 16), s[0])]
            else:
                take = lambda who, src=src: _lane_window(src, who)
            out.append((take, lambda who, dst=dst: dst.at[who, layer]))
        return out

    aliases = {} if parts is None else {ng + j: j for j in range(len(items))}
    return _Job(list(grads) + ([] if parts is None else list(parts)), out_shape, aliases, pairs, len(items))


def _rows_job(src, row0, landing=None):
    n = src.shape[0]
    pairs = lambda in_refs, out_refs: [(lambda who: in_refs[0], lambda who: out_refs[0].at[who, pl.ds(row0, n)])]
    return _Job([src] + ([] if landing is None else [landing]), [jax.ShapeDtypeStruct((N_DEV, _REP_ROWS, LANES), F32)],
                {} if landing is None else {1: 0}, pairs, 1)


_REPLICATED = (("norm_g", (2, 1024)), ("mem_norm_g", (2, 1024)), ("b_gate", (2, 3072)),
               ("ssm_lambda_re", (2, 48, 64)), ("ssm_lambda_im", (2, 48, 64)), ("ssm_log_dt", (2, 48)),
               ("ssm_b_re", (2, 48, 64, 16)), ("ssm_b_im", (2, 48, 64, 16)), ("ssm_c_re", (2, 48, 16, 64)),
               ("ssm_c_im", (2, 48, 16, 64)), ("ssm_d", (2, 768)), ("b_glu", (2, 768)), ("rel_bias", (32, 12)),
               ("final_norm_g", (1024,)))
_PER_LAYER = tuple((n, s[1:]) for n, s in _REPLICATED if s[0] == DEPTH and len(s) > 1)
_SHARED = tuple((n, s) for n, s in _REPLICATED if (n, s[1:]) not in _PER_LAYER)
_REP_HALF_ROWS = 1664
_REP_ROWS = 2 * _REP_HALF_ROWS
assert sum(int(np.prod(s)) for _, s in _PER_LAYER + _SHARED) <= _REP_HALF_ROWS * LANES


def _pack_half(tree, layer, shared):
    flat = [tree[n][layer].reshape(-1) for n, _ in _PER_LAYER]
    if shared:
        flat += [tree[n].reshape(-1) for n, _ in _SHARED]
    flat = jnp.concatenate(flat)
    return jnp.pad(flat, (0, _REP_HALF_ROWS * LANES - flat.shape[0])).reshape(_REP_HALF_ROWS, LANES)


def _pack_replicated(tree):
    return jnp.concatenate([_pack_half(tree, 1, False), _pack_half(tree, 0, True)])[None]


def _unpack_replicated(packed):
    halves = packed.reshape(2, -1)
    out, r = {}, 0
    for n, s in _PER_LAYER:
        size = int(np.prod(s))
        out[n] = jnp.stack([halves[1, r:r + size].reshape(s), halves[0, r:r + size].reshape(s)])
        r += size
    for n, s in _SHARED:
        size = int(np.prod(s))
        out[n] = halves[1, r:r + size].reshape(s)
        r += size
    return out


def _discretize(lam_re, lam_im, log_dt, b_re, b_im):
    dt = jnp.exp(log_dt)[:, None]
    mag = jnp.exp(lam_re * dt)
    abar_re, abar_im = mag * jnp.cos(lam_im * dt), mag * jnp.sin(lam_im * dt)
    den = lam_re * lam_re + lam_im * lam_im
    nr, ni = abar_re - 1.0, abar_im
    f_re = (nr * lam_re + ni * lam_im) / den
    f_im = (ni * lam_re - nr * lam_im) / den
    bbar_re = f_re[..., None] * b_re - f_im[..., None] * b_im
    bbar_im = f_re[..., None] * b_im + f_im[..., None] * b_re
    return abar_re, abar_im, bbar_re, bbar_im


def _block_diag(a):
    _, R, C = a.shape
    a = a.reshape(SSM_BLOCKS, 8, R, C)
    eye = jnp.eye(8, dtype=a.dtype)
    return (a[:, :, :, None, :] * eye[None, :, None, :, None]).reshape(SSM_BLOCKS, 8 * R, 8 * C)


def _diag_blocks(a, R, C):
    a = a.reshape(SSM_BLOCKS, 8, R, 8, C)
    eye = jnp.eye(8, dtype=a.dtype)
    return jnp.sum(a * eye[None, :, None, :, None], axis=3).reshape(SSM_GROUPS, R, C)


def _carried(result, job):
    return (result, None) if job is None else result


def _layer_fwd(x, mem, W, P, bias, layer, jobs):
    tag = f"l{layer}"
    abar_re, abar_im, bbar_re, bbar_im = _discretize(P["ssm_lambda_re"][layer], P["ssm_lambda_im"][layer],
                                                     P["ssm_log_dt"][layer], P["ssm_b_re"][layer], P["ssm_b_im"][layer])
    c_re, c_im = P["ssm_c_re"][layer], P["ssm_c_im"][layer]
    ssm = dict(
        are=abar_re.reshape(1, N_STATE), aim=abar_im.reshape(1, N_STATE),
        bre=_block_diag(bbar_re.transpose(0, 2, 1)).astype(BF16), bim=_block_diag(bbar_im.transpose(0, 2, 1)).astype(BF16),
        cre=_block_diag(c_re.transpose(0, 2, 1)).astype(BF16), cimn=_block_diag(-c_im.transpose(0, 2, 1)).astype(BF16),
        d=P["ssm_d"][layer].reshape(1, D_SSM))
    bglu = P["b_glu"][layer].reshape(1, D_SSM)
    bgate = P["b_gate"][layer].reshape(1, N_GATES)
    g = P["norm_g"][layer].reshape(1, D_MODEL)
    gm = P["mem_norm_g"][layer].reshape(1, D_MODEL)
    delivered = {}

    def carry(stage):
        return jobs[stage][0] if stage in jobs else None

    def deliver(stage, landed):
        if landed is not None:
            delivered[stage] = _landed_weights(jobs[stage][1], landed)

    T = x.shape[0]
    (proj, h), landed = _carried(_norm_proj(x, g, W["w_in"], min(T, 1024), 2176, f"{tag}_proj", job=carry("proj"),
                                            w_turned=True), carry("proj"))
    deliver("proj", landed)
    W = {**W, **delivered.get("proj", {})}
    (xr, xi, y, o_ssm), landed = _carried(
        _ssm_fwd(proj, ssm["bre"], ssm["bim"], ssm["cre"], ssm["cimn"], ssm["are"], ssm["aim"], ssm["d"], W["w_glu"],
                 bglu, 512, f"{tag}_ssm", job=carry("ssm")), carry("ssm"))
    deliver("ssm", landed)
    os, lses = [], []
    for grp in range(3):
        stage = f"attn{grp}"
        (o_g, lse_g), landed = _carried(_attn_fwd(proj, bias[grp], grp, f"{tag}_{stage}", job=carry(stage)), carry(stage))
        deliver(stage, landed)
        os.append(o_g)
        lses.append(lse_g)
    o_attn = _attn_mix(os, lses, proj, min(T, ROW_TILE), f"{tag}_attn_mix")
    kvb, hm = _norm_proj(mem, gm, W["w_mem_kv"], mem.shape[0], 1024, f"{tag}_mem_kv", out_dtype=BF16)
    o_mem = _mem_fwd(proj, kvb, min(T, ROW_TILE), f"{tag}_mem")
    x_out, merged = _merge_fwd(x, o_ssm, o_attn, o_mem, proj, bgate, W["w_br_ssm"], W["w_br_attn"], W["w_br_mem"],
                               W["w_out"], 512, f"{tag}_merge")
    res = dict(x=x, mem=mem, proj=proj, h=h, xr=xr, xi=xi, y=y, o_ssm=o_ssm, os=os, lses=lses,
               o_attn=o_attn, kvb=kvb, hm=hm, o_mem=o_mem, merged=merged, ssm=ssm, bglu=bglu,
               bgate=bgate, g=g, gm=gm, W=W)
    return x_out, res, delivered


def _layer_bwd(dx, res, P, bias, layer, jobs):
    tag = f"l{layer}b"
    proj, ssm, W = res["proj"], res["ssm"], res["W"]
    T = dx.shape[0]
    landed = {}

    def run(stage, fn, job):
        out, landed[stage] = _carried(fn(job), job)
        if job is None:
            del landed[stage]
        return out

    dproj, dbr, do_ssm, do_attn, do_mem, dbg = run(
        "merge", lambda job: _merge_bwd(dx, res["o_ssm"], res["o_attn"], res["o_mem"], proj, res["bgate"], W["w_br_ssm"],
                                        W["w_br_attn"], W["w_br_mem"], W["w_out"], 512, f"{tag}_merge", job=job),
        jobs.get("merge"))
    gw = {}
    tk = min(T, 1024)
    gw["w_out"] = _mm_tn(res["merged"], dx, 1024, 1024, tk, f"{tag}_dw_out")
    gw["w_br_ssm"] = _mm_tn(res["o_ssm"], dbr, 768, 1024, tk, f"{tag}_dw_br_ssm", b_col=0, n=1024)
    gw["w_br_attn"] = _mm_tn(res["o_attn"], dbr, 768, 1024, tk, f"{tag}_dw_br_attn", b_col=1024, n=1024)
    gw["w_br_mem"] = _mm_tn(res["o_mem"], dbr, 512, 1024, tk, f"{tag}_dw_br_mem", b_col=2048, n=1024)

    rows = min(T, ROW_TILE)
    dproj, dkv = _mem_bwd(do_mem, proj, res["kvb"], dproj, rows, f"{tag}_mem")
    M = dkv.shape[0]
    gw["w_mem_kv"] = _mm_tn(res["hm"], dkv, 1024, 1024, M, f"{tag}_dw_mem_kv")
    _, dgm = _proj_bwd(dkv.astype(BF16), W["w_mem_kv"], res["mem"], res["gm"], jnp.zeros_like(res["mem"]), M, 1024,
                       f"{tag}_mem_norm")

    do_g, corr, dproj = _attn_mix_bwd(do_attn, res["os"], res["lses"], proj, dproj, rows, f"{tag}_attn_mix")
    dbs = []
    for grp in range(3):
        dproj, db_g = _attn_bwd(proj, do_g, corr, res["lses"][grp], bias[grp], dproj, grp, f"{tag}_attn{grp}")
        dbs.append(db_g)
    dbias = jnp.stack(dbs)

    dy, dproj, gelu_b, dt_b, dbglu = _glu_bwd(do_ssm, res["y"], proj, W["w_glu"], res["bglu"], dproj, rows, f"{tag}_glu")
    gw["w_glu"] = _mm_tn(gelu_b, dt_b, 768, 768, tk, f"{tag}_dw_glu")
    dproj, dbre, dbim, dcre, dcim, dare, daim, dd = run(
        "ssm", lambda job: _ssm_bwd(dy, proj, res["xr"], res["xi"], ssm["bre"], ssm["bim"], ssm["cre"], ssm["cimn"],
                                    ssm["are"], ssm["aim"], ssm["d"], dproj, 256, f"{tag}_ssm", job=job),
        jobs.get("ssm"))
    _, disc_vjp = jax.vjp(_discretize, P["ssm_lambda_re"][layer], P["ssm_lambda_im"][layer], P["ssm_log_dt"][layer],
                          P["ssm_b_re"][layer], P["ssm_b_im"][layer])
    d_lre, d_lim, d_ldt, d_bre, d_bim = disc_vjp((dare.reshape(SSM_GROUPS, SSM_STATE), daim.reshape(SSM_GROUPS, SSM_STATE),
                                                  _diag_blocks(dbre, SSM_STATE, SSM_GROUP),
                                                  _diag_blocks(dbim, SSM_STATE, SSM_GROUP)))

    small = [gw[_SHARDED[i][0]] for i in _SMALL]
    dw_in = run("dw_in", lambda job: _mm_tn(dproj, res["h"], 2176, 1024, min(T, 1024), f"{tag}_dw_in", job=job),
                jobs["dw_in"](small) if "dw_in" in jobs else None)
    dw_in = dw_in.reshape((N_DEV,) + _SHARDED[_W_IN][1])
    dx_in, dg = run("proj", lambda job: _proj_bwd(dproj, W["w_in"], res["x"], res["g"], dx, min(T, 1024), 2176,
                                                  f"{tag}_proj", job=job, w_turned=True),
                    jobs["proj"](small, dw_in, landed) if "proj" in jobs else None)

    gp = dict(norm_g=dg[0], mem_norm_g=dgm[0], b_gate=dbg[0], ssm_lambda_re=d_lre, ssm_lambda_im=d_lim,
              ssm_log_dt=d_ldt, ssm_b_re=d_bre, ssm_b_im=d_bim,
              ssm_c_re=_diag_blocks(dcre, SSM_GROUP, SSM_STATE), ssm_c_im=_diag_blocks(dcim, SSM_GROUP, SSM_STATE),
              ssm_d=dd[0], b_glu=dbglu[0])
    return dx_in, dw_in, gp, dbias, landed


def _train_step(x, mem, target, shards, P):
    rest0 = [(i, 0) for i in _SMALL]
    thirds1 = [[(i, 1) for i in _SMALL[k::3]] for k in range(3)]
    first = [(_W_IN, 0)]
    w_in0 = _gather_via_sibling(shards[_W_IN], lambda ref: ref.at[0], lambda ref, s: ref.at[s],
                                jax.ShapeDtypeStruct((N_DEV,) + _SHARDED[_W_IN][1], BF16), "gather_w_in0")
    W0 = _landed_weights(first, [w_in0])
    buckets = _bucket_tables()
    bias = _bias_tables(P["rel_bias"], buckets, "bias_tables")
    jobs0 = {"proj": (_gather_job(shards, rest0), rest0), "ssm": (_gather_job(shards, [(_W_IN, 1)]), [(_W_IN, 1)]),
             **{f"attn{k}": (_gather_job(shards, items), items) for k, items in enumerate(thirds1)}}
    x, res0, delivered = _layer_fwd(x, mem, W0, P, bias, 0, jobs0)
    W1 = {**delivered["ssm"], **delivered["attn0"], **delivered["attn1"], **delivered["attn2"]}
    x, res1, _ = _layer_fwd(x, mem, W1, P, bias, 1, {})
    loss, dx, dgf = _loss_head(x, P["final_norm_g"].reshape(1, D_MODEL), target, min(x.shape[0], ROW_TILE),
                                "loss_head")

    dx, dw_in1, gp1, dbias1, landed1 = _layer_bwd(
        dx, res1, P, bias, 1, {"proj": lambda small, dw_in, landed: _scatter_job(small, _SMALL, 1)})
    rep1 = _pack_half({n: a[None] for n, a in gp1.items()}, 0, False)
    dx, _, gp0, dbias0, landed0 = _layer_bwd(
        dx, res0, P, bias, 0,
        {"merge": _rows_job(rep1, 0), "ssm": _scatter_job([dw_in1], [_W_IN], 1),
         "dw_in": lambda small: _scatter_job(small, _SMALL, 0, parts=landed1["proj"]),
         "proj": lambda small, dw_in, landed: _scatter_job([dw_in], [_W_IN], 0, parts=landed["ssm"])})
    d_rel = _bias_grad(dbias0, dbias1, buckets, "bias_grad")
    gp0 = {n: a[None] for n, a in gp0.items()}
    gp0["rel_bias"] = jnp.sum(d_rel, axis=-1).transpose(2, 0, 1).reshape(NUM_BUCKETS, 12)
    gp0["final_norm_g"] = dgf[0]
    rep0 = _pack_half(gp0, 0, True)
    rparts = _gather_via_sibling(rep0, lambda ref: ref, lambda ref, s: ref.at[s, pl.ds(_REP_HALF_ROWS, _REP_HALF_ROWS)],
                                 jax.ShapeDtypeStruct((N_DEV, _REP_ROWS, LANES), F32), "gather_small_grads0",
                                 landing=landed0["merge"][0])
    return loss[0, 0], dx, list(landed0["proj"]) + list(landed0["dw_in"]), rparts


_WEIGHTS = ["norm_g", "mem_norm_g", "w_in", "b_gate", "ssm_lambda_re", "ssm_lambda_im", "ssm_log_dt", "ssm_b_re",
            "ssm_b_im", "ssm_c_re", "ssm_c_im", "ssm_d", "w_glu", "b_glu", "w_mem_kv", "w_br_ssm", "w_br_attn",
            "w_br_mem", "w_out", "rel_bias", "final_norm_g"]
_ADAM_ROWS = {"w_in": 136,"w_glu": 96, "w_mem_kv": 128, "w_br_ssm": 768, "w_br_attn": 768, "w_br_mem": 512,
              "w_out": 128}


def kernel(x, mem, norm_g, mem_norm_g, w_in, b_gate, ssm_lambda_re, ssm_lambda_im, ssm_log_dt, ssm_b_re, ssm_b_im, ssm_c_re, ssm_c_im, ssm_d, w_glu, b_glu, w_mem_kv, w_br_ssm, w_br_attn, w_br_mem, w_out, rel_bias, final_norm_g, loss_target, m_norm_g, m_mem_norm_g, m_w_in, m_b_gate, m_ssm_lambda_re, m_ssm_lambda_im, m_ssm_log_dt, m_ssm_b_re, m_ssm_b_im, m_ssm_c_re, m_ssm_c_im, m_ssm_d, m_w_glu, m_b_glu, m_w_mem_kv, m_w_br_ssm, m_w_br_attn, m_w_br_mem, m_w_out, m_rel_bias, m_final_norm_g, v_norm_g, v_mem_norm_g, v_w_in, v_b_gate, v_ssm_lambda_re, v_ssm_lambda_im, v_ssm_log_dt, v_ssm_b_re, v_ssm_b_im, v_ssm_c_re, v_ssm_c_im, v_ssm_d, v_w_glu, v_b_glu, v_w_mem_kv, v_w_br_ssm, v_w_br_attn, v_w_br_mem, v_w_out, v_rel_bias, v_final_norm_g):
    given = dict(locals())
    w = {n: given[n] for n in _WEIGHTS}
    m = {n: given["m_" + n] for n in _WEIGHTS}
    v = {n: given["v_" + n] for n in _WEIGHTS}

    turned = lambda n, a: a.swapaxes(1, 2) if n == "w_in" else a
    shards = [turned(n, w[n]).astype(BF16) for n, _, _ in _SHARDED]
    loss, dx, parts, rparts = _train_step(x[0], mem[0], loss_target[0], shards, w)
    loss = lax.psum(loss, ("x", "y", "c"))

    new = {}
    for (n, _, _), p in zip(_SHARDED, parts):
        new[n] = [turned(n, a) for a in _adamw(p, turned(n, w[n]), turned(n, m[n]), turned(n, v[n]), _ADAM_ROWS[n],
                                               f"adamw_{n}")]
    rp = [_unpack_replicated(a) for a in _adamw(rparts[:, None], _pack_replicated(w), _pack_replicated(m),
                                                _pack_replicated(v), _REP_ROWS // 4, "adamw_replicated")]
    for n, _ in _REPLICATED:
        new[n] = [rp[kind][n] for kind in range(4)]
    outs = [loss, dx[None]]
    for kind in range(4):
        outs.extend(new[n][kind] for n in _WEIGHTS)
    return tuple(outs)
```

```python
import functools
import math
from typing import Callable, NamedTuple

import jax
import jax.numpy as jnp
import numpy as np
from jax import lax
from jax.experimental import pallas as pl
from jax.experimental.pallas import tpu as pltpu

F32 = jnp.float32
BF16 = jnp.bfloat16

D_MODEL = 1024
DEPTH = 2
EPS = 1e-6
D_SSM = 768
SSM_GROUP = 16
SSM_GROUPS = 48
SSM_STATE = 64
N_STATE = SSM_GROUPS * SSM_STATE
SSM_BLOCKS = 6
D_ATTN = 768
ATTN_HEAD_DIM = 64
ATTN_GROUP_WIDTH = 256
ATTN_DILATIONS = (1, 4, 16)
ATTN_SPAN = 128
ATTN_BLOCK = 128
NUM_BUCKETS = 32
REL_MAX_DISTANCE = 2048
NEG_INF = -1e30
MEM_HEADS = 4
MEM_HEAD_DIM = 128
D_MEM = 512
N_GATES = 3 * D_MODEL
D_IN = 8704
N_DEV = 8
LANES = 128
ADAM_LR = 0.001
ADAM_B1 = 0.9
ADAM_B2 = 0.999
ADAM_EPS = 1e-08
ADAM_WD = 0.01
ADAM_STEP = 10

_OFF = {"u": 0, "z_ssm": 768, "q": 1536, "k": 2304, "v": 3072, "z_attn": 3840, "q_mem": 4608, "z_mem": 5120,
        "gates": 5632}
GATE_BLOCK = 512
ROW_TILE = 1024

NN = (((1,), (0,)), ((), ()))
NT = (((1,), (1,)), ((), ()))
TN = (((0,), (0,)), ((), ()))

VMEM_LIMIT = 56 * 1024 * 1024


def _dot(a, b, dims=NN):
    return lax.dot_general(a, b, dims, preferred_element_type=F32)


def _sigmoid(x):
    return 1.0 / (1.0 + jnp.exp(-x))


def _gelu_parts(x):
    k = math.sqrt(2.0 / math.pi)
    t = jnp.tanh(k * (x + 0.044715 * (x * x * x)))
    cdf = 0.5 * (1.0 + t)
    dcdf = 0.5 * (1.0 - t * t) * k * (1.0 + 3.0 * 0.044715 * (x * x))
    return x * cdf, cdf + x * dcdf


def _params(sem, vmem=VMEM_LIMIT):
    return pltpu.CompilerParams(dimension_semantics=sem, vmem_limit_bytes=vmem)


def _full(shape):
    return pl.BlockSpec(shape, lambda *_: (0,) * len(shape))


def _norm_proj(x, g, w, tm, tn, name, out_dtype=F32, job=None, w_turned=False):
    T, D = x.shape
    N = w.shape[0] if w_turned else w.shape[1]
    w_spec = pl.BlockSpec((tn, D), lambda i, j: (j, 0)) if w_turned else pl.BlockSpec((D, tn), lambda i, j: (0, j))
    dims = NT if w_turned else NN

    def body(x_ref, g_ref, w_ref, o_ref, h_ref, hs):
        @pl.when(pl.program_id(1) == 0)
        def _():
            xv = x_ref[...]
            r = lax.rsqrt(jnp.mean(xv * xv, axis=-1, keepdims=True) + EPS)
            hv = (xv * r * g_ref[...]).astype(BF16)
            hs[...] = hv
            h_ref[...] = hv

        o_ref[...] = _dot(hs[...], w_ref[...], dims).astype(out_dtype)

    return _pc(
        body, job, name=name, grid=(T // tm, N // tn),
        in_specs=[pl.BlockSpec((tm, D), lambda i, j: (i, 0)), _full((1, D)), w_spec],
        out_specs=[pl.BlockSpec((tm, tn), lambda i, j: (i, j)), pl.BlockSpec((tm, D), lambda i, j: (i, 0))],
        out_shape=[jax.ShapeDtypeStruct((T, N), out_dtype), jax.ShapeDtypeStruct((T, D), BF16)],
        scratch_shapes=[pltpu.VMEM((tm, D), BF16)], sem=("parallel", "arbitrary"), operands=(x, g, w))


def _mm_tn(a, b, tm, tn, tk, name, b_col=0, n=None, job=None):
    K, M = a.shape
    N = b.shape[1] if n is None else n
    nk = K // tk
    j0 = b_col // tn

    def body(a_ref, b_ref, o_ref, acc):
        k = pl.program_id(2)

        @pl.when(k == 0)
        def _():
            acc[...] = jnp.zeros_like(acc)

        acc[...] += _dot(a_ref[...].astype(BF16), b_ref[...].astype(BF16), TN)

        @pl.when(k == nk - 1)
        def _():
            o_ref[...] = acc[...].astype(BF16)

    out = _pc(
        body, job, name=name, grid=(M // tm, N // tn, nk),
        in_specs=[pl.BlockSpec((tk, tm), lambda i, j, k: (k, i)), pl.BlockSpec((tk, tn), lambda i, j, k: (k, j0 + j))],
        out_specs=[pl.BlockSpec((tm, tn), lambda i, j, k: (i, j))],
        out_shape=[jax.ShapeDtypeStruct((M, N), BF16)],
        scratch_shapes=[pltpu.VMEM((tm, tn), F32)], sem=("parallel", "parallel", "arbitrary"), operands=(a, b))
    return out[0] if job is None else (out[0][0], out[1])


def _proj_bwd(dp, w, x, g, dres, tm, tk, name, job=None, w_turned=False):
    T, N = dp.shape
    D = x.shape[1]
    nk = N // tk
    w_spec = pl.BlockSpec((tk, D), lambda i, k: (k, 0)) if w_turned else pl.BlockSpec((D, tk), lambda i, k: (0, k))
    dims = NN if w_turned else NT

    def body(dp_ref, w_ref, x_ref, g_ref, dres_ref, dx_ref, dg_ref, acc):
        i, k = pl.program_id(0), pl.program_id(1)

        @pl.when(k == 0)
        def _():
            acc[...] = jnp.zeros_like(acc)

        @pl.when((i == 0) & (k == 0))
        def _():
            dg_ref[...] = jnp.zeros_like(dg_ref)

        acc[...] += _dot(dp_ref[...], w_ref[...], dims)

        @pl.when(k == nk - 1)
        def _():
            xv = x_ref[...]
            dh = acc[...]
            r = lax.rsqrt(jnp.mean(xv * xv, axis=-1, keepdims=True) + EPS)
            xr = xv * r
            dg_ref[...] += jnp.sum(dh * xr, axis=0, keepdims=True)
            wv = dh * g_ref[...]
            dx_ref[...] = dres_ref[...] + r * (wv - xr * jnp.mean(wv * xr, axis=-1, keepdims=True))

    return _pc(
        body, job, name=name, grid=(T // tm, nk),
        in_specs=[pl.BlockSpec((tm, tk), lambda i, k: (i, k)), w_spec,
                  pl.BlockSpec((tm, D), lambda i, k: (i, 0)), _full((1, D)),
                  pl.BlockSpec((tm, D), lambda i, k: (i, 0))],
        out_specs=[pl.BlockSpec((tm, D), lambda i, k: (i, 0)), _full((1, D))],
        out_shape=[jax.ShapeDtypeStruct((T, D), F32), jax.ShapeDtypeStruct((1, D), F32)],
        scratch_shapes=[pltpu.VMEM((tm, D), F32)], sem=("arbitrary", "arbitrary"), operands=(dp, w, x, g, dres))


def _ssm_fwd(proj, bre, bim, cre, cimn, are, aim, d, wglu, bglu, tc, name, job=None):
    T = proj.shape[0]
    ucol, zcol = _OFF["u"] // D_SSM, _OFF["z_ssm"] // D_SSM

    def body(u_ref, z_ref, bre_ref, bim_ref, cre_ref, cim_ref, are_ref, aim_ref, d_ref, wg_ref, bg_ref,
             xr_ref, xi_ref, y_ref, o_ref, car_r, car_i):
        @pl.when(pl.program_id(0) == 0)
        def _():
            car_r[...] = jnp.zeros_like(car_r)
            car_i[...] = jnp.zeros_like(car_i)

        u = u_ref[...]
        ub = u.astype(BF16)
        for k in range(SSM_BLOCKS):
            uk = ub[:, 128 * k:128 * (k + 1)]
            xr_ref[:, 512 * k:512 * (k + 1)] = _dot(uk, bre_ref[k])
            xi_ref[:, 512 * k:512 * (k + 1)] = _dot(uk, bim_ref[k])
        ar, ai = are_ref[...], aim_ref[...]

        def step(t, c):
            pr, pi = c
            nr = ar * pr - ai * pi + xr_ref[pl.ds(t, 1), :]
            ni = ar * pi + ai * pr + xi_ref[pl.ds(t, 1), :]
            xr_ref[pl.ds(t, 1), :] = nr
            xi_ref[pl.ds(t, 1), :] = ni
            return nr, ni

        pr, pi = lax.fori_loop(0, tc, step, (car_r[...], car_i[...]))
        car_r[...] = pr
        car_i[...] = pi

        ys = []
        for k in range(SSM_BLOCKS):
            xrk = xr_ref[:, 512 * k:512 * (k + 1)].astype(BF16)
            xik = xi_ref[:, 512 * k:512 * (k + 1)].astype(BF16)
            ys.append(_dot(xrk, cre_ref[k]) + _dot(xik, cim_ref[k]))
        y = jnp.concatenate(ys, axis=1) + d_ref[...] * u
        y_ref[...] = y
        gl, _ = _gelu_parts(y)
        t = _dot(gl.astype(BF16), wg_ref[...]) + bg_ref[...]
        z = z_ref[...]
        o_ref[...] = (gl * _sigmoid(t) * (z * _sigmoid(z))).astype(BF16)

    return _pc(
        body, job, name=name, grid=(T // tc,),
        in_specs=[pl.BlockSpec((tc, D_SSM), lambda i: (i, ucol)), pl.BlockSpec((tc, D_SSM), lambda i: (i, zcol)),
                  _full((SSM_BLOCKS, 128, 512)), _full((SSM_BLOCKS, 128, 512)),
                  _full((SSM_BLOCKS, 512, 128)), _full((SSM_BLOCKS, 512, 128)),
                  _full((1, N_STATE)), _full((1, N_STATE)), _full((1, D_SSM)),
                  _full((D_SSM, D_SSM)), _full((1, D_SSM))],
        out_specs=[pl.BlockSpec((tc, N_STATE), lambda i: (i, 0)), pl.BlockSpec((tc, N_STATE), lambda i: (i, 0)),
                   pl.BlockSpec((tc, D_SSM), lambda i: (i, 0)), pl.BlockSpec((tc, D_SSM), lambda i: (i, 0))],
        out_shape=[jax.ShapeDtypeStruct((T, N_STATE), F32), jax.ShapeDtypeStruct((T, N_STATE), F32),
                   jax.ShapeDtypeStruct((T, D_SSM), F32), jax.ShapeDtypeStruct((T, D_SSM), BF16)],
        scratch_shapes=[pltpu.VMEM((1, N_STATE), F32), pltpu.VMEM((1, N_STATE), F32)], sem=("arbitrary",),
        operands=(proj, proj, bre, bim, cre, cimn, are, aim, d, wglu, bglu))


def _glu_bwd(do, y, proj, wglu, bglu, dproj, tm, name):
    T = y.shape[0]
    zcol = _OFF["z_ssm"] // D_SSM

    def body(do_ref, y_ref, z_ref, wg_ref, bg_ref, _, dy_ref, dz_ref, g_ref, dt_ref, db_ref):
        @pl.when(pl.program_id(0) == 0)
        def _():
            db_ref[...] = jnp.zeros_like(db_ref)

        dov = do_ref[...]
        gl, dgl = _gelu_parts(y_ref[...])
        glb = gl.astype(BF16)
        sg = _sigmoid(_dot(glb, wg_ref[...]) + bg_ref[...])
        z = z_ref[...]
        sz = _sigmoid(z)
        dz_ref[...] = (dov * (gl * sg) * (sz * (1.0 + z * (1.0 - sz)))).astype(BF16)
        dy2 = dov * (z * sz)
        dt = dy2 * gl * (sg * (1.0 - sg))
        dtb = dt.astype(BF16)
        dg = dy2 * sg + _dot(dtb, wg_ref[...], NT)
        dy_ref[...] = dg * dgl
        g_ref[...] = glb
        dt_ref[...] = dtb
        db_ref[...] += jnp.sum(dt, axis=0, keepdims=True)

    row = lambda i: (i, 0)
    return pl.pallas_call(
        body, name=name, grid=(T // tm,),
        in_specs=[pl.BlockSpec((tm, D_SSM), row), pl.BlockSpec((tm, D_SSM), row),
                  pl.BlockSpec((tm, D_SSM), lambda i: (i, zcol)), _full((D_SSM, D_SSM)), _full((1, D_SSM)),
                  pl.BlockSpec(memory_space=pl.ANY)],
        out_specs=[pl.BlockSpec((tm, D_SSM), row), pl.BlockSpec((tm, D_SSM), lambda i: (i, zcol)),
                   pl.BlockSpec((tm, D_SSM), row), pl.BlockSpec((tm, D_SSM), row), _full((1, D_SSM))],
        out_shape=[jax.ShapeDtypeStruct((T, D_SSM), F32), jax.ShapeDtypeStruct(dproj.shape, BF16),
                   jax.ShapeDtypeStruct((T, D_SSM), BF16), jax.ShapeDtypeStruct((T, D_SSM), BF16),
                   jax.ShapeDtypeStruct((1, D_SSM), F32)],
        input_output_aliases={5: 1},
        compiler_params=_params(("arbitrary",)),
    )(do, y, proj, wglu, bglu, dproj)


def _ssm_bwd(dy, proj, xr, xi, bre, bim, cre, cimn, are, aim, d, dproj, tc, name, job=None):
    T = dy.shape[0]
    nc = T // tc
    ucol = _OFF["u"] // D_SSM
    rb = tc // 8

    def body(dy_ref, u_ref, xr_ref, xi_ref, xpr_ref, xpi_ref, bre_ref, bim_ref, cre_ref, cim_ref,
             are_ref, aim_ref, d_ref, _,
             du_ref, dbre_ref, dbim_ref, dcre_ref, dcim_ref, dare_ref, daim_ref, dd_ref, gr, gi, car_r, car_i):
        i = pl.program_id(0)

        @pl.when(i == 0)
        def _():
            for ref in (car_r, car_i, dbre_ref, dbim_ref, dcre_ref, dcim_ref, dare_ref, daim_ref, dd_ref):
                ref[...] = jnp.zeros_like(ref)

        dyv = dy_ref[...]
        dyb = dyv.astype(BF16)
        u = u_ref[...]
        ub = u.astype(BF16)
        for k in range(SSM_BLOCKS):
            dk = dyb[:, 128 * k:128 * (k + 1)]
            gr[:, 512 * k:512 * (k + 1)] = _dot(dk, cre_ref[k], NT)
            gi[:, 512 * k:512 * (k + 1)] = _dot(dk, cim_ref[k], NT)
        ar, ai = are_ref[...], aim_ref[...]

        def step(s, c):
            pr, pi = c
            t = tc - 1 - s
            nr = gr[pl.ds(t, 1), :] + ar * pr + ai * pi
            ni = gi[pl.ds(t, 1), :] + ar * pi - ai * pr
            gr[pl.ds(t, 1), :] = nr
            gi[pl.ds(t, 1), :] = ni
            return nr, ni

        pr, pi = lax.fori_loop(0, tc, step, (car_r[...], car_i[...]))
        car_r[...] = pr
        car_i[...] = pi

        keep = jnp.where(i == nc - 1, 0.0, 1.0)
        row0 = lax.broadcasted_iota(jnp.int32, (tc, 1), 0) == 0
        dd_ref[...] += jnp.sum(dyv * u, axis=0, keepdims=True)
        for k in range(SSM_BLOCKS):
            sl = slice(512 * k, 512 * (k + 1))
            ch = slice(128 * k, 128 * (k + 1))
            xrk, xik, grk, gik = xr_ref[:, sl], xi_ref[:, sl], gr[:, sl], gi[:, sl]
            xsr = jnp.where(row0, xpr_ref[7:8, sl] * keep, pltpu.roll(xrk, 1, axis=0))
            xsi = jnp.where(row0, xpi_ref[7:8, sl] * keep, pltpu.roll(xik, 1, axis=0))
            dare_ref[:, sl] += jnp.sum(grk * xsr + gik * xsi, axis=0, keepdims=True)
            daim_ref[:, sl] += jnp.sum(gik * xsr - grk * xsi, axis=0, keepdims=True)
            grb, gib = grk.astype(BF16), gik.astype(BF16)
            du_ref[:, ch] = (_dot(grb, bre_ref[k], NT) + _dot(gib, bim_ref[k], NT)
                             + d_ref[:, ch] * dyv[:, ch]).astype(BF16)
            dbre_ref[k] += _dot(grb, ub[:, ch], TN)
            dbim_ref[k] += _dot(gib, ub[:, ch], TN)
            dcre_ref[k] += _dot(dyb[:, ch], xrk.astype(BF16), TN)
            dcim_ref[k] -= _dot(dyb[:, ch], xik.astype(BF16), TN)

    rev = lambda i: (nc - 1 - i, 0)
    prev = lambda i: (jnp.maximum((nc - 1 - i) * rb - 1, 0), 0)
    return _pc(
        body, job, name=name, grid=(nc,),
        in_specs=[pl.BlockSpec((tc, D_SSM), rev), pl.BlockSpec((tc, D_SSM), lambda i: (nc - 1 - i, ucol)),
                  pl.BlockSpec((tc, N_STATE), rev), pl.BlockSpec((tc, N_STATE), rev),
                  pl.BlockSpec((8, N_STATE), prev), pl.BlockSpec((8, N_STATE), prev),
                  _full((SSM_BLOCKS, 128, 512)), _full((SSM_BLOCKS, 128, 512)),
                  _full((SSM_BLOCKS, 512, 128)), _full((SSM_BLOCKS, 512, 128)),
                  _full((1, N_STATE)), _full((1, N_STATE)), _full((1, D_SSM)), pl.BlockSpec(memory_space=pl.ANY)],
        out_specs=[pl.BlockSpec((tc, D_SSM), lambda i: (nc - 1 - i, ucol)),
                   _full((SSM_BLOCKS, 512, 128)), _full((SSM_BLOCKS, 512, 128)),
                   _full((SSM_BLOCKS, 128, 512)), _full((SSM_BLOCKS, 128, 512)),
                   _full((1, N_STATE)), _full((1, N_STATE)), _full((1, D_SSM))],
        out_shape=[jax.ShapeDtypeStruct(dproj.shape, BF16),
                   jax.ShapeDtypeStruct((SSM_BLOCKS, 512, 128), F32), jax.ShapeDtypeStruct((SSM_BLOCKS, 512, 128), F32),
                   jax.ShapeDtypeStruct((SSM_BLOCKS, 128, 512), F32), jax.ShapeDtypeStruct((SSM_BLOCKS, 128, 512), F32),
                   jax.ShapeDtypeStruct((1, N_STATE), F32), jax.ShapeDtypeStruct((1, N_STATE), F32),
                   jax.ShapeDtypeStruct((1, D_SSM), F32)],
        scratch_shapes=[pltpu.VMEM((tc, N_STATE), F32), pltpu.VMEM((tc, N_STATE), F32),
                        pltpu.VMEM((1, N_STATE), F32), pltpu.VMEM((1, N_STATE), F32)], sem=("arbitrary",),
        operands=(dy, proj, xr, xi, xr, xi, bre, bim, cre, cimn, are, aim, d, dproj), aliases={13: 0})


def _rel_bucket(dist):
    n = jnp.maximum(dist, 0)
    max_exact = NUM_BUCKETS // 2
    n_f = jnp.maximum(n, 1).astype(F32)
    large = max_exact + (jnp.log(n_f / max_exact) / math.log(REL_MAX_DISTANCE / max_exact)
                         * (NUM_BUCKETS - max_exact)).astype(jnp.int32)
    large = jnp.minimum(large, NUM_BUCKETS - 1)
    return jnp.where(n < max_exact, n, large)


def _bucket_tables():
    qi = jnp.arange(ATTN_BLOCK)[:, None]
    kj = jnp.arange(2 * ATTN_BLOCK)[None, :]
    delta = jnp.maximum(ATTN_BLOCK + qi - kj, 0)
    return jnp.stack([_rel_bucket(delta * r) for r in ATTN_DILATIONS]).astype(jnp.int32)


def _bias_tables(rel_bias, buckets, name):
    def body(tab_ref, bk_ref, o_ref):
        g = pl.program_id(0)
        bk = bk_ref[...]
        qi = lax.broadcasted_iota(jnp.int32, bk.shape, 0)
        kj = lax.broadcasted_iota(jnp.int32, bk.shape, 1)
        delta = ATTN_BLOCK + qi - kj
        band = (delta >= 0) & (delta <= ATTN_SPAN)
        accs = [jnp.zeros(bk.shape, F32) for _ in range(4)]
        for b in range(NUM_BUCKETS):
            hit = bk == b
            for h in range(4):
                accs[h] = jnp.where(hit, tab_ref[b, 4 * g + h], accs[h])
        for h in range(4):
            o_ref[h] = jnp.where(band, accs[h], NEG_INF)

    return pl.pallas_call(
        body, name=name, grid=(3,),
        in_specs=[pl.BlockSpec(memory_space=pltpu.SMEM),
                  pl.BlockSpec((None, ATTN_BLOCK, 2 * ATTN_BLOCK), lambda g: (g, 0, 0))],
        out_specs=pl.BlockSpec((None, 4, ATTN_BLOCK, 2 * ATTN_BLOCK), lambda g: (g, 0, 0, 0)),
        out_shape=jax.ShapeDtypeStruct((3, 4, ATTN_BLOCK, 2 * ATTN_BLOCK), F32),
        compiler_params=_params(("parallel",)),
    )(rel_bias, buckets)


def _bias_grad(db0, db1, buckets, name):
    def body(a_ref, b_ref, bk_ref, o_ref):
        bk = bk_ref[...]
        for h in range(4):
            dv = a_ref[h] + b_ref[h]
            for b in range(NUM_BUCKETS):
                o_ref[h, b:b + 1, :] = jnp.sum(jnp.where(bk == b, dv, 0.0), axis=0, keepdims=True)

    tab = pl.BlockSpec((None, 4, ATTN_BLOCK, 2 * ATTN_BLOCK), lambda g: (g, 0, 0, 0))
    return pl.pallas_call(
        body, name=name, grid=(3,),
        in_specs=[tab, tab, pl.BlockSpec((None, ATTN_BLOCK, 2 * ATTN_BLOCK), lambda g: (g, 0, 0))],
        out_specs=pl.BlockSpec((None, 4, NUM_BUCKETS, 2 * ATTN_BLOCK), lambda g: (g, 0, 0, 0)),
        out_shape=jax.ShapeDtypeStruct((3, 4, NUM_BUCKETS, 2 * ATTN_BLOCK), F32),
        compiler_params=_params(("parallel",)),
    )(db0, db1, buckets)


_ATTN_SUB = {1: 4, 4: 1, 16: 1}
_UNROLL = 4


def _unit_rows(j, s, r):
    start = j * ATTN_BLOCK * r + s
    return pl.ds(start, ATTN_BLOCK, stride=r) if r > 1 else pl.ds(start, ATTN_BLOCK)


def _for_units(r, nsub, fn, after):
    if r * nsub <= _UNROLL:
        after([fn(j, s) for j in range(nsub) for s in range(r)])
    else:
        def four(i, c):
            after([fn(0, _UNROLL * i + k) for k in range(_UNROLL)])
            return c

        lax.fori_loop(0, r // _UNROLL, four, 0)


def _attn_cols(g):
    return tuple((_OFF[n] + ATTN_GROUP_WIDTH * g) // LANES for n in ("q", "k", "v"))


def _attn_fwd(proj, bias, g, name, job=None):
    r = ATTN_DILATIONS[g]
    nsub = _ATTN_SUB[r]
    T = proj.shape[0]
    sub = ATTN_BLOCK * r
    tb = sub * nsub
    qc, kc, vc = _attn_cols(g)
    scale = ATTN_HEAD_DIM ** -0.5

    def body(q_ref, kc_ref, kp_ref, vc_ref, vp_ref, bias_ref, o_ref, lse_ref):
        lane = lax.broadcasted_iota(jnp.int32, (ATTN_BLOCK, LANES), 1)
        kj = lax.broadcasted_iota(jnp.int32, (ATTN_BLOCK, 2 * ATTN_BLOCK), 1)
        dead = (pl.program_id(0) == 0) & (kj < ATTN_BLOCK)

        def one(j, s):
            rows = _unit_rows(j, s, r)
            before = _unit_rows(max(j - 1, 0), s, r)
            k_before = kc_ref[before, :] if j else kp_ref[before, :]
            v_before = vc_ref[before, :] if j else vp_ref[before, :]
            q = q_ref[rows, :]
            kcat = jnp.concatenate([k_before, kc_ref[rows, :]], axis=0).astype(BF16)
            vcat = jnp.concatenate([v_before, vc_ref[rows, :]], axis=0).astype(BF16)
            o_acc = jnp.zeros((ATTN_BLOCK, LANES), F32)
            l_acc = jnp.zeros((ATTN_BLOCK, LANES), F32)
            for hh in range(2):
                mine = (lane >= ATTN_HEAD_DIM) if hh else (lane < ATTN_HEAD_DIM)
                qm = jnp.where(mine, q, 0.0).astype(BF16)
                sc = _dot(qm, kcat, NT) * scale + bias_ref[hh]
                if j == 0:
                    sc = jnp.where(dead, NEG_INF, sc)
                m = jnp.max(sc, axis=-1, keepdims=True)
                p = jnp.exp(sc - m)
                l = jnp.sum(p, axis=-1, keepdims=True)
                o_acc = jnp.where(mine, _dot((p / l).astype(BF16), vcat), o_acc)
                l_acc = jnp.where(mine, m + jnp.log(l), l_acc)
            o_ref[rows, :] = o_acc
            lse_ref[rows, :] = l_acc

        _for_units(r, nsub, one, lambda results: None)

    cur = lambda c: pl.BlockSpec((tb, LANES), lambda b, p: (b, c + p))
    prev = lambda c: pl.BlockSpec((sub, LANES), lambda b, p: (jnp.maximum(b * nsub - 1, 0), c + p))
    out = pl.BlockSpec((tb, LANES), lambda b, p: (b, p))
    return _pc(
        body, job, name=name, grid=(T // tb, 2),
        in_specs=[cur(qc), cur(kc), prev(kc), cur(vc), prev(vc),
                  pl.BlockSpec((2, ATTN_BLOCK, 2 * ATTN_BLOCK), lambda b, p: (p, 0, 0))],
        out_specs=[out, out],
        out_shape=[jax.ShapeDtypeStruct((T, ATTN_GROUP_WIDTH), F32), jax.ShapeDtypeStruct((T, ATTN_GROUP_WIDTH), F32)],
        scratch_shapes=[], sem=("parallel", "parallel"), operands=(proj, proj, proj, proj, proj, bias))


def _attn_bwd(proj, do, corr, lse, bias, dproj, g, name):
    r = ATTN_DILATIONS[g]
    nsub = _ATTN_SUB[r]
    T = proj.shape[0]
    sub = ATTN_BLOCK * r
    tb = sub * nsub
    nb = T // tb
    qc, kc, vc = _attn_cols(g)
    dc = ATTN_GROUP_WIDTH * g // LANES
    scale = ATTN_HEAD_DIM ** -0.5

    def body(q_ref, kc_ref, kp_ref, vc_ref, vp_ref, do_ref, corr_ref, lse_ref, bias_ref, _,
             dproj_ref, db_ref, dq_s, dkc_s, dkp_s, dvc_s, dvp_s, kacc, vacc, stage, stage_sems):
        p, b = pl.program_id(0), pl.program_id(1)

        def to_dproj(e, slot, block, col):
            rows = pl.ds(pl.multiple_of(block * tb, tb), tb)
            cols = pl.ds(pl.multiple_of((col + p) * LANES, LANES), LANES)
            return pltpu.make_async_copy(stage.at[e, slot], dproj_ref.at[rows, cols], stage_sems.at[e, slot])

        def emit(e, block, col, value):
            count = p * nb + block
            slot = count % 2

            @pl.when(count >= 2)
            def _():
                to_dproj(e, slot, 0, col).wait()

            stage[e, slot] = value.astype(BF16)
            to_dproj(e, slot, block, col).start()

        def emit_keys(block):
            emit(1, block, kc, kacc[...])
            emit(2, block, vc, vacc[...])

        @pl.when(b == 0)
        def _():
            db_ref[...] = jnp.zeros_like(db_ref)
            kacc[...] = jnp.zeros_like(kacc)
            vacc[...] = jnp.zeros_like(vacc)

        @pl.when(b == nb)
        def _():
            emit_keys(nb - 1)

        @pl.when((b == nb) & (p == 1))
        def _():
            for e, col in enumerate((qc, kc, vc)):
                for slot in range(2):
                    to_dproj(e, slot, 0, col).wait()

        @pl.when(b < nb)
        def _():
            lane = lax.broadcasted_iota(jnp.int32, (ATTN_BLOCK, LANES), 1)
            kj = lax.broadcasted_iota(jnp.int32, (ATTN_BLOCK, 2 * ATTN_BLOCK), 1)
            dead = (b == 0) & (kj < ATTN_BLOCK)

            def one(j, s):
                rows = _unit_rows(j, s, r)
                before = _unit_rows(max(j - 1, 0), s, r)
                k_before = kc_ref[before, :] if j else kp_ref[before, :]
                v_before = vc_ref[before, :] if j else vp_ref[before, :]
                q = q_ref[rows, :]
                kcat = jnp.concatenate([k_before, kc_ref[rows, :]], axis=0).astype(BF16)
                vcat = jnp.concatenate([v_before, vc_ref[rows, :]], axis=0).astype(BF16)
                dov, corrv, lsev = do_ref[rows, :], corr_ref[rows, :], lse_ref[rows, :]
                dq_acc = jnp.zeros((ATTN_BLOCK, LANES), F32)
                dk_acc = jnp.zeros((2 * ATTN_BLOCK, LANES), F32)
                dv_acc = jnp.zeros((2 * ATTN_BLOCK, LANES), F32)
                dss = []
                for hh in range(2):
                    mine = (lane >= ATTN_HEAD_DIM) if hh else (lane < ATTN_HEAD_DIM)
                    col = slice(ATTN_HEAD_DIM * hh, ATTN_HEAD_DIM * hh + 1)
                    qm = jnp.where(mine, q, 0.0).astype(BF16)
                    dom = jnp.where(mine, dov, 0.0).astype(BF16)
                    sc = _dot(qm, kcat, NT) * scale + bias_ref[hh]
                    if j == 0:
                        sc = jnp.where(dead, NEG_INF, sc)
                    p = jnp.exp(sc - lsev[:, col])
                    ds = p * (_dot(dom, vcat, NT) - corrv[:, col])
                    dss.append(ds)
                    dsb = ds.astype(BF16)
                    dq_acc = jnp.where(mine, _dot(dsb, kcat) * scale, dq_acc)
                    dk_acc += _dot(dsb, qm, TN) * scale
                    dv_acc += _dot(p.astype(BF16), dom, TN)
                dq_s[rows, :] = dq_acc
                dkp_s[rows, :] = dk_acc[:ATTN_BLOCK]
                dkc_s[rows, :] = dk_acc[ATTN_BLOCK:]
                dvp_s[rows, :] = dv_acc[:ATTN_BLOCK]
                dvc_s[rows, :] = dv_acc[ATTN_BLOCK:]
                return dss

            def add_bias_grads(results):
                for hh in range(2):
                    db_ref[hh] += functools.reduce(lambda x, y: x + y, [dss[hh] for dss in results])

            _for_units(r, nsub, one, add_bias_grads)
            emit(0, b, qc, dq_s[...])
            tail = slice((nsub - 1) * sub, nsub * sub)
            kacc[tail, :] += dkp_s[0:sub, :]
            vacc[tail, :] += dvp_s[0:sub, :]

            @pl.when(b >= 1)
            def _():
                emit_keys(b - 1)

            for acc, before_s, cur_s in ((kacc, dkp_s, dkc_s), (vacc, dvp_s, dvc_s)):
                acc[...] = cur_s[...]
                for j in range(nsub - 1):
                    acc[j * sub:(j + 1) * sub, :] += before_s[(j + 1) * sub:(j + 2) * sub, :]

    last = nb - 1
    blk = (tb, LANES)
    cur = lambda c: pl.BlockSpec(blk, lambda p, b: (jnp.minimum(b, last), c + p))
    before = lambda c: pl.BlockSpec((sub, LANES), lambda p, b: (jnp.clip(b * nsub - 1, 0, nb * nsub - 1), c + p))
    tab = pl.BlockSpec((2, ATTN_BLOCK, 2 * ATTN_BLOCK), lambda p, b: (p, 0, 0))
    hbm = pl.BlockSpec(memory_space=pl.ANY)
    return pl.pallas_call(
        body, name=name, grid=(2, nb + 1),
        in_specs=[cur(qc), cur(kc), before(kc), cur(vc), before(vc), cur(dc), cur(dc), cur(0), tab, hbm],
        out_specs=[hbm, tab],
        out_shape=[jax.ShapeDtypeStruct(dproj.shape, BF16), jax.ShapeDtypeStruct((4, ATTN_BLOCK, 2 * ATTN_BLOCK), F32)],
        input_output_aliases={9: 0},
        scratch_shapes=[pltpu.VMEM(blk, F32)] * 7 + [pltpu.VMEM((3, 2) + blk, BF16), pltpu.SemaphoreType.DMA((3, 2))],
        compiler_params=_params(("arbitrary", "arbitrary")),
    )(proj, proj, proj, proj, proj, do, corr, lse, bias, dproj)


def _mix_weights(lses):
    m = jnp.maximum(jnp.maximum(lses[0], lses[1]), lses[2])
    es = [jnp.exp(l - m) for l in lses]
    inv = 1.0 / (es[0] + es[1] + es[2])
    return jnp.concatenate([e * inv for e in es], axis=1)


def _attn_mix(os, lses, proj, tm, name):
    T = proj.shape[0]
    zcol = _OFF["z_attn"] // D_ATTN

    def body(o0, o1, o2, l0, l1, l2, z_ref, out_ref):
        z = z_ref[...]
        o = jnp.concatenate([o0[...], o1[...], o2[...]], axis=1)
        alpha = _mix_weights([l0[...], l1[...], l2[...]])
        out_ref[...] = (o * alpha * (z * _sigmoid(z))).astype(BF16)

    row = lambda i: (i, 0)
    grp = pl.BlockSpec((tm, ATTN_GROUP_WIDTH), row)
    return pl.pallas_call(
        body, name=name, grid=(T // tm,),
        in_specs=[grp] * 6 + [pl.BlockSpec((tm, D_ATTN), lambda i: (i, zcol))],
        out_specs=pl.BlockSpec((tm, D_ATTN), row),
        out_shape=jax.ShapeDtypeStruct((T, D_ATTN), BF16),
        compiler_params=_params(("parallel",)),
    )(*os, *lses, proj)


def _attn_mix_bwd(d, os, lses, proj, dproj, tm, name):
    T = proj.shape[0]
    zcol = _OFF["z_attn"] // D_ATTN

    def body(d_ref, o0, o1, o2, l0, l1, l2, z_ref, _, do_ref, corr_ref, dz_ref):
        dv, z = d_ref[...], z_ref[...]
        ov = jnp.concatenate([o0[...], o1[...], o2[...]], axis=1)
        alpha = _mix_weights([l0[...], l1[...], l2[...]])
        sz = _sigmoid(z)
        oc = ov * alpha
        dz_ref[...] = (dv * oc * (sz * (1.0 + z * (1.0 - sz)))).astype(BF16)
        doc = dv * (z * sz)
        do_ref[...] = doc * alpha
        pr = doc * oc
        p3 = pr[:, 0:256] + pr[:, 256:512] + pr[:, 512:768]
        li = lax.broadcasted_iota(jnp.int32, (256, 256), 0) // ATTN_HEAD_DIM
        lj = lax.broadcasted_iota(jnp.int32, (256, 256), 1) // ATTN_HEAD_DIM
        ones = jnp.where(li == lj, 1.0, 0.0).astype(F32)
        s = lax.dot_general(p3, ones, NN, precision=lax.Precision.HIGHEST, preferred_element_type=F32)
        corr_ref[...] = alpha * jnp.concatenate([s, s, s], axis=1)

    row = lambda i: (i, 0)
    grp = pl.BlockSpec((tm, ATTN_GROUP_WIDTH), row)
    return pl.pallas_call(
        body, name=name, grid=(T // tm,),
        in_specs=[pl.BlockSpec((tm, D_ATTN), row)] + [grp] * 6 + [pl.BlockSpec((tm, D_ATTN), lambda i: (i, zcol)),
                                                                    pl.BlockSpec(memory_space=pl.ANY)],
        out_specs=[pl.BlockSpec((tm, D_ATTN), row)] * 2 + [pl.BlockSpec((tm, D_ATTN), lambda i: (i, zcol))],
        out_shape=[jax.ShapeDtypeStruct((T, D_ATTN), F32), jax.ShapeDtypeStruct((T, D_ATTN), F32),
                   jax.ShapeDtypeStruct(dproj.shape, BF16)],
        input_output_aliases={8: 2},
        compiler_params=_params(("parallel",)),
    )(d, *os, *lses, proj, dproj)


def _mem_probs(q_ref, kv_ref, h):
    hs = slice(MEM_HEAD_DIM * h, MEM_HEAD_DIM * (h + 1))
    qh = q_ref[:, hs].astype(BF16)
    kh = kv_ref[:, hs]
    vh = kv_ref[:, D_MEM + MEM_HEAD_DIM * h:D_MEM + MEM_HEAD_DIM * (h + 1)]
    s = _dot(qh, kh, NT) * (MEM_HEAD_DIM ** -0.5)
    p = jnp.exp(s - jnp.max(s, axis=-1, keepdims=True))
    pn = p / jnp.sum(p, axis=-1, keepdims=True)
    return qh, kh, vh, pn


def _mem_fwd(proj, kv, tm, name):
    T = proj.shape[0]
    M = kv.shape[0]
    qcol, zcol = _OFF["q_mem"] // D_MEM, _OFF["z_mem"] // D_MEM

    def body(q_ref, z_ref, kv_ref, o_ref):
        outs = []
        for h in range(MEM_HEADS):
            _, _, vh, pn = _mem_probs(q_ref, kv_ref, h)
            outs.append(_dot(pn.astype(BF16), vh))
        z = z_ref[...]
        o_ref[...] = (jnp.concatenate(outs, axis=1) * (z * _sigmoid(z))).astype(BF16)

    return pl.pallas_call(
        body, name=name, grid=(T // tm,),
        in_specs=[pl.BlockSpec((tm, D_MEM), lambda i: (i, qcol)), pl.BlockSpec((tm, D_MEM), lambda i: (i, zcol)),
                  _full((M, 2 * D_MEM))],
        out_specs=pl.BlockSpec((tm, D_MEM), lambda i: (i, 0)),
        out_shape=jax.ShapeDtypeStruct((T, D_MEM), BF16),
        compiler_params=_params(("parallel",)),
    )(proj, proj, kv)


def _mem_bwd(d, proj, kv, tm, name):
    T = proj.shape[0]
    M = kv.shape[0]
    qcol, zcol = _OFF["q_mem"] // D_MEM, _OFF["z_mem"] // D_MEM

    def body(d_ref, q_ref, z_ref, kv_ref, dq_ref, dz_ref, dkv_ref):
        @pl.when(pl.program_id(0) == 0)
        def _():
            dkv_ref[...] = jnp.zeros_like(dkv_ref)

        z = z_ref[...]
        sz = _sigmoid(z)
        dv = d_ref[...]
        dov = dv * (z * sz)
        scale = MEM_HEAD_DIM ** -0.5
        outs, dqs = [], []
        for h in range(MEM_HEADS):
            hs = slice(MEM_HEAD_DIM * h, MEM_HEAD_DIM * (h + 1))
            qh, kh, vh, pn = _mem_probs(q_ref, kv_ref, h)
            pnb = pn.astype(BF16)
            oh = _dot(pnb, vh)
            outs.append(oh)
            doh = dov[:, hs]
            dohb = doh.astype(BF16)
            dp = _dot(dohb, vh, NT)
            ds = pn * (dp - jnp.sum(doh * oh, axis=-1, keepdims=True))
            dsb = ds.astype(BF16)
            dqs.append(_dot(dsb, kh) * scale)
            dkv_ref[:, hs] += _dot(dsb, qh, TN) * scale
            vs = slice(D_MEM + MEM_HEAD_DIM * h, D_MEM + MEM_HEAD_DIM * (h + 1))
            dkv_ref[:, vs] += _dot(pnb, dohb, TN)
        dq_ref[...] = jnp.concatenate(dqs, axis=1).astype(BF16)
        dz_ref[...] = (dv * jnp.concatenate(outs, axis=1) * (sz * (1.0 + z * (1.0 - sz)))).astype(BF16)

    row = lambda i: (i, 0)
    return pl.pallas_call(
        body, name=name, grid=(T // tm,),
        in_specs=[pl.BlockSpec((tm, D_MEM), row), pl.BlockSpec((tm, D_MEM), lambda i: (i, qcol)),
                  pl.BlockSpec((tm, D_MEM), lambda i: (i, zcol)), _full((M, 2 * D_MEM))],
        out_specs=[pl.BlockSpec((tm, D_MEM), row), pl.BlockSpec((tm, D_MEM), row), _full((M, 2 * D_MEM))],
        out_shape=[jax.ShapeDtypeStruct((T, D_MEM), BF16), jax.ShapeDtypeStruct((T, D_MEM), BF16),
                   jax.ShapeDtypeStruct((M, 2 * D_MEM), F32)],
        compiler_params=_params(("arbitrary",)),
    )(d, proj, proj, kv)


def _branches_and_gates(os_ref, oa_ref, om_ref, gl_refs, bg_ref, ws_ref, wa_ref, wm_ref):
    outs = (_dot(os_ref[...], ws_ref[...]), _dot(oa_ref[...], wa_ref[...]), _dot(om_ref[...], wm_ref[...]))
    gates = tuple(_sigmoid(jnp.concatenate([gl_refs[2 * k][...], gl_refs[2 * k + 1][...]], axis=1)
                           + bg_ref[:, D_MODEL * k:D_MODEL * (k + 1)]) for k in range(3))
    return outs, gates


def _merge_specs(tm):
    row = lambda i: (i, 0)
    first = _OFF["gates"] // GATE_BLOCK
    gate = [pl.BlockSpec((tm, GATE_BLOCK), (lambda i, k=k: (i, first + k))) for k in range(N_GATES // GATE_BLOCK)]
    return ([pl.BlockSpec((tm, D_SSM), row), pl.BlockSpec((tm, D_ATTN), row), pl.BlockSpec((tm, D_MEM), row)] + gate
            + [_full((1, N_GATES)), _full((D_SSM, D_MODEL)), _full((D_ATTN, D_MODEL)), _full((D_MEM, D_MODEL)),
               _full((D_MODEL, D_MODEL))])


def _merge_fwd(x, o_ssm, o_attn, o_mem, proj, bg, ws, wa, wm, wo, tm, name):
    T = x.shape[0]

    def body(os_ref, oa_ref, om_ref, g0, g1, g2, g3, g4, g5, bg_ref, ws_ref, wa_ref, wm_ref, wo_ref, x_ref,
             xo_ref, mg_ref):
        outs, gates = _branches_and_gates(os_ref, oa_ref, om_ref, (g0, g1, g2, g3, g4, g5), bg_ref, ws_ref, wa_ref,
                                          wm_ref)
        merged = (gates[0] * outs[0] + gates[1] * outs[1] + gates[2] * outs[2]).astype(BF16)
        mg_ref[...] = merged
        xo_ref[...] = x_ref[...] + _dot(merged, wo_ref[...])

    row = lambda i: (i, 0)
    return pl.pallas_call(
        body, name=name, grid=(T // tm,),
        in_specs=_merge_specs(tm) + [pl.BlockSpec((tm, D_MODEL), row)],
        out_specs=[pl.BlockSpec((tm, D_MODEL), row), pl.BlockSpec((tm, D_MODEL), row)],
        out_shape=[jax.ShapeDtypeStruct((T, D_MODEL), F32), jax.ShapeDtypeStruct((T, D_MODEL), BF16)],
        compiler_params=_params(("parallel",)),
    )(o_ssm, o_attn, o_mem, *([proj] * (N_GATES // GATE_BLOCK)), bg, ws, wa, wm, wo, x)


def _merge_bwd(dx, o_ssm, o_attn, o_mem, proj, bg, ws, wa, wm, wo, tm, name, job=None):
    T = dx.shape[0]

    n = T // tm

    def body(os_ref, oa_ref, om_ref, g0, g1, g2, g3, g4, g5, bg_ref, ws_ref, wa_ref, wm_ref, wo_ref, dx_ref,
             dproj_ref, db_ref, dos_ref, doa_ref, dom_ref, dbg_ref, dgl_buf, dgl_sems):
        i = pl.program_id(0)
        slot = i % 2

        def to_dproj(s, row0):
            return pltpu.make_async_copy(dgl_buf.at[s], dproj_ref.at[pl.ds(row0, tm), pl.ds(_OFF["gates"], N_GATES)],
                                         dgl_sems.at[s])

        @pl.when(i == 0)
        def _():
            dbg_ref[...] = jnp.zeros_like(dbg_ref)

        @pl.when(i >= 2)
        def _():
            to_dproj(slot, 0).wait()

        outs, gates = _branches_and_gates(os_ref, oa_ref, om_ref, (g0, g1, g2, g3, g4, g5), bg_ref, ws_ref, wa_ref,
                                          wm_ref)
        dm = _dot(dx_ref[...].astype(BF16), wo_ref[...], NT)
        w_refs = (ws_ref, wa_ref, wm_ref)
        do_refs = (dos_ref, doa_ref, dom_ref)
        for k in range(3):
            cols = slice(D_MODEL * k, D_MODEL * (k + 1))
            dgl = dm * outs[k] * (gates[k] * (1.0 - gates[k]))
            dgl_buf[slot, :, cols] = dgl.astype(BF16)
            dbg_ref[:, cols] += jnp.sum(dgl, axis=0, keepdims=True)
            dbk = (dm * gates[k]).astype(BF16)
            db_ref[:, cols] = dbk
            do_refs[k][...] = _dot(dbk, w_refs[k][...], NT)
        to_dproj(slot, pl.multiple_of(i * tm, tm)).start()

        @pl.when(i == n - 1)
        def _():
            for s in range(min(2, n)):
                to_dproj(s, 0).wait()

    row = lambda i: (i, 0)
    return _pc(
        body, job, name=name, grid=(n,),
        in_specs=_merge_specs(tm) + [pl.BlockSpec((tm, D_MODEL), row)],
        out_specs=[pl.BlockSpec(memory_space=pl.ANY), pl.BlockSpec((tm, N_GATES), row), pl.BlockSpec((tm, D_SSM), row),
                   pl.BlockSpec((tm, D_ATTN), row), pl.BlockSpec((tm, D_MEM), row), _full((1, N_GATES))],
        out_shape=[jax.ShapeDtypeStruct((T, D_IN), BF16), jax.ShapeDtypeStruct((T, N_GATES), BF16),
                   jax.ShapeDtypeStruct((T, D_SSM), F32), jax.ShapeDtypeStruct((T, D_ATTN), F32),
                   jax.ShapeDtypeStruct((T, D_MEM), F32), jax.ShapeDtypeStruct((1, N_GATES), F32)],
        scratch_shapes=[pltpu.VMEM((2, tm, N_GATES), BF16), pltpu.SemaphoreType.DMA((2,))], sem=("arbitrary",),
        operands=(o_ssm, o_attn, o_mem, *([proj] * (N_GATES // GATE_BLOCK)), bg, ws, wa, wm, wo, dx))


def _loss_head(x, g, target, tm, name):
    T, D = x.shape

    def body(x_ref, g_ref, t_ref, loss_ref, dx_ref, dg_ref):
        @pl.when(pl.program_id(0) == 0)
        def _():
            loss_ref[...] = jnp.zeros_like(loss_ref)
            dg_ref[...] = jnp.zeros_like(dg_ref)

        xv = x_ref[...]
        r = lax.rsqrt(jnp.mean(xv * xv, axis=-1, keepdims=True) + EPS)
        xr = xv * r
        err = xr * g_ref[...] - t_ref[...]
        loss_ref[...] += 0.5 * jnp.sum(jnp.mean(err * err, axis=-1, keepdims=True), axis=0, keepdims=True)
        dy = err * (1.0 / D)
        dg_ref[...] += jnp.sum(dy * xr, axis=0, keepdims=True)
        wv = dy * g_ref[...]
        dx_ref[...] = r * (wv - xr * jnp.mean(wv * xr, axis=-1, keepdims=True))

    row = lambda i: (i, 0)
    return pl.pallas_call(
        body, name=name, grid=(T // tm,),
        in_specs=[pl.BlockSpec((tm, D), row), _full((1, D)), pl.BlockSpec((tm, D), row)],
        out_specs=[_full((1, 128)), pl.BlockSpec((tm, D), row), _full((1, D))],
        out_shape=[jax.ShapeDtypeStruct((1, 128), F32), jax.ShapeDtypeStruct((T, D), F32),
                   jax.ShapeDtypeStruct((1, D), F32)],
        compiler_params=_params(("arbitrary",)),
    )(x, g, target)


def _adamw(parts, w, m, v, tr, name):
    L, R, C = w.shape

    def body(p_ref, w_ref, m_ref, v_ref, g_ref, d_ref, mo_ref, vo_ref):
        g = p_ref[0].astype(F32)
        for s in range(1, N_DEV):
            g = g + p_ref[s].astype(F32)
        mn = ADAM_B1 * m_ref[...] + (1.0 - ADAM_B1) * g
        vn = ADAM_B2 * v_ref[...] + (1.0 - ADAM_B2) * (g * g)
        m_hat = mn / (1.0 - ADAM_B1 ** ADAM_STEP)
        v_hat = vn / (1.0 - ADAM_B2 ** ADAM_STEP)
        g_ref[...] = g
        d_ref[...] = -ADAM_LR * (m_hat / (jnp.sqrt(v_hat) + ADAM_EPS) + ADAM_WD * w_ref[...])
        mo_ref[...] = mn
        vo_ref[...] = vn

    one = pl.BlockSpec((None, tr, C), lambda l, i: (l, i, 0))
    return pl.pallas_call(
        body, name=name, grid=(L, R // tr),
        in_specs=[pl.BlockSpec((N_DEV, None, tr, C), lambda l, i: (0, l, i, 0)), one, one, one],
        out_specs=[one] * 4,
        out_shape=[jax.ShapeDtypeStruct((L, R, C), F32)] * 4,
        compiler_params=_params(("parallel", "parallel")),
    )(parts, w, m, v)


_SHARDED = (("w_in", (1088, 1024), 1), ("w_glu", (96, 768), 0), ("w_mem_kv", (128, 1024), 0),
            ("w_br_ssm", (768, 128), 1), ("w_br_attn", (768, 128), 1), ("w_br_mem", (512, 128), 1),
            ("w_out", (128, 1024), 0))
_W_IN = 0
_SMALL = tuple(range(1, len(_SHARDED)))


class _Job(NamedTuple):
    ins: list
    out_shape: list
    aliases: dict
    pairs: Callable
    n: int


def _peers():
    x, y, c = lax.axis_index("x"), lax.axis_index("y"), lax.axis_index("c")
    me = 4 * x + 2 * y + c
    out = []
    for k in range(1, N_DEV):
        px = 1 - x if k & 4 else x
        py = 1 - y if k & 2 else y
        pc = 1 - c if k & 1 else c
        out.append(((px, py, pc), 4 * px + 2 * py + pc))
    return me, out


def _copies(pairs, send_sems, recv_sems, local_sems, arrivals):
    me, peers = _peers()
    local = [pltpu.make_async_copy(src(me), dst(me), local_sems.at[j]) for j, (src, dst) in enumerate(pairs)]
    sends, recvs = [], []
    for k, (peer, lin) in enumerate(peers):
        for j, (src, dst) in enumerate(pairs):
            for to, out in ((dst(me), sends), (dst(lin), recvs)):
                if out is sends or arrivals:
                    out.append(pltpu.make_async_remote_copy(
                        src_ref=src(lin), dst_ref=to, send_sem=send_sems.at[j, k], recv_sem=recv_sems.at[j, k],
                        device_id=peer, device_id_type=pl.DeviceIdType.MESH))
    return local, sends, recvs


def _start_copies(pairs, *sems):
    local, sends, _ = _copies(pairs, *sems, arrivals=False)
    for cp in local + sends:
        cp.start()


def _wait_copies(pairs, *sems):
    local, sends, recvs = _copies(pairs, *sems, arrivals=True)
    for cp in recvs:
        cp.wait_recv()
    for cp in sends:
        cp.wait_send()
    for cp in local:
        cp.wait()


def _job_scratch(job):
    return [pltpu.SemaphoreType.DMA((job.n, N_DEV - 1)), pltpu.SemaphoreType.DMA((job.n, N_DEV - 1)),
            pltpu.SemaphoreType.DMA((job.n,))]


def _pc(body, job, *, name, grid, in_specs, out_specs, out_shape, scratch_shapes, sem, operands, aliases=None):
    aliases = aliases or {}
    if job is None:
        return pl.pallas_call(body, name=name, grid=grid, in_specs=in_specs, out_specs=out_specs, out_shape=out_shape,
                              scratch_shapes=scratch_shapes, input_output_aliases=aliases,
                              compiler_params=_params(sem))(*operands)
    a = len(in_specs)
    b = a + len(job.ins)
    c = b + len(out_shape)
    d = c + len(job.out_shape)
    e = d + len(scratch_shapes)

    def carried(*refs):
        pairs = job.pairs(refs[a:b], refs[c:d])
        ids = [pl.program_id(k) for k in range(len(grid))]
        first = functools.reduce(jnp.logical_and, [i == 0 for i in ids])
        last = functools.reduce(jnp.logical_and, [i == n - 1 for i, n in zip(ids, grid)])

        @pl.when(first)
        def _():
            _start_copies(pairs, *refs[e:])

        body(*refs[:a], *refs[b:c], *refs[d:e])

        @pl.when(last)
        def _():
            _wait_copies(pairs, *refs[e:])

    hbm = pl.BlockSpec(memory_space=pl.ANY)
    outs = pl.pallas_call(
        carried, name=name, grid=grid,
        in_specs=list(in_specs) + [hbm] * len(job.ins), out_specs=list(out_specs) + [hbm] * len(job.out_shape),
        out_shape=list(out_shape) + list(job.out_shape),
        input_output_aliases={**aliases, **{a + i: len(out_shape) + o for i, o in job.aliases.items()}},
        scratch_shapes=list(scratch_shapes) + _job_scratch(job),
        compiler_params=_params(("arbitrary",) * len(grid)),
    )(*operands, *job.ins)
    return outs[:len(out_shape)], outs[len(out_shape):]


def _gather_via_sibling(x, take, place, out_shape, name, landing=None):
    def body(*refs):
        x_ref, o_ref = refs[0], refs[-4]
        send_sems, recv_sems, local_sem = refs[-3:]
        x, y, c = lax.axis_index("x"), lax.axis_index("y"), lax.axis_index("c")
        me, sibling = (x, y, c), (x, y, 1 - c)
        chips = [(1 - x, y), (x, 1 - y), (1 - x, 1 - y)]
        src = take(x_ref)

        def slot(px, py, pc):
            return place(o_ref, 4 * px + 2 * py + pc)

        def copy(k, block, to, first_hand):
            return pltpu.make_async_remote_copy(
                src_ref=src if first_hand else slot(*block), dst_ref=slot(*block), send_sem=send_sems.at[k],
                recv_sem=recv_sems.at[k], device_id=to, device_id_type=pl.DeviceIdType.MESH)

        mine = pltpu.make_async_copy(src, slot(*me), local_sem)
        mine.start()
        first = [copy(0, me, sibling, True)] + [copy(1 + j, me, (*chip, c), True) for j, chip in enumerate(chips)]
        for cp in first:
            cp.start()
        passed = []
        for j, chip in enumerate(chips):
            copy(1 + j, (*chip, c), me, True).wait_recv()
            passed.append(copy(4 + j, (*chip, c), sibling, False))
            passed[-1].start()
        copy(0, sibling, me, True).wait_recv()
        for j, chip in enumerate(chips):
            copy(4 + j, (*chip, 1 - c), me, False).wait_recv()
        for cp in first + passed:
            cp.wait_send()
        mine.wait()

    hbm = pl.BlockSpec(memory_space=pl.ANY)
    ins = [x] if landing is None else [x, landing]
    return pl.pallas_call(
        body, name=name, in_specs=[hbm] * len(ins), out_specs=hbm, out_shape=out_shape,
        input_output_aliases={} if landing is None else {1: 0},
        scratch_shapes=[pltpu.SemaphoreType.DMA((N_DEV - 1,)), pltpu.SemaphoreType.DMA((N_DEV - 1,)),
                        pltpu.SemaphoreType.DMA],
    )(*ins)


def _lane_window(ref, who):
    return ref.at[:, pl.ds(pl.multiple_of(who * LANES, LANES), LANES)]


def _gather_job(shards, items):
    out_shape = []
    for i, _ in items:
        _, s, axis = _SHARDED[i]
        whole = i != _W_IN and axis == 1
        out_shape.append(jax.ShapeDtypeStruct((s[0], N_DEV * s[1]) if whole else (N_DEV,) + s, BF16))

    def pairs(in_refs, out_refs):
        out = []
        for (i, l), src, dst in zip(items, in_refs, out_refs):
            if i != _W_IN and _SHARDED[i][2] == 1:
                out.append((lambda who, src=src, l=l: src.at[l], lambda who, dst=dst: _lane_window(dst, who)))
            else:
                out.append((lambda who, src=src, l=l: src.at[l], lambda who, dst=dst: dst.at[who]))
        return out

    return _Job([shards[i] for i, _ in items], out_shape, {}, pairs, len(items))


def _landed_weights(items, landed):
    out = {}
    for (i, _), a in zip(items, landed):
        n, s, axis = _SHARDED[i]
        if i == _W_IN:
            out[n] = a.reshape(D_IN, D_MODEL)
        elif axis == 0:
            out[n] = a.reshape(N_DEV * s[0], s[1])
        else:
            out[n] = a
    return out


def _scatter_job(grads, items, layer, parts=None):
    ng = len(grads)
    out_shape = [jax.ShapeDtypeStruct((N_DEV, DEPTH) + _SHARDED[i][1], BF16) for i in items]

    def pairs(in_refs, out_refs):
        out = []
        for i, src, dst in zip(items, in_refs[:ng], out_refs):
            _, s, axis = _SHARDED[i]
            if i == _W_IN:
                take = lambda who, src=src: src.at[who]
            elif axis == 0:
                take = lambda who, src=src, s=s: src.at[pl.ds(pl.multiple_of(who * s[0], 16), s[0])]
            else:
                take = lambda who, src=src: _lane_window(src, who)
            out.append((take, lambda who, dst=dst: dst.at[who, layer]))
        return out

    aliases = {} if parts is None else {ng + j: j for j in range(len(items))}
    return _Job(list(grads) + ([] if parts is None else list(parts)), out_shape, aliases, pairs, len(items))


def _rows_job(src, row0, landing=None):
    n = src.shape[0]
    pairs = lambda in_refs, out_refs: [(lambda who: in_refs[0], lambda who: out_refs[0].at[who, pl.ds(row0, n)])]
    return _Job([src] + ([] if landing is None else [landing]), [jax.ShapeDtypeStruct((N_DEV, _REP_ROWS, LANES), F32)],
                {} if landing is None else {1: 0}, pairs, 1)


_REPLICATED = (("norm_g", (2, 1024)), ("mem_norm_g", (2, 1024)), ("b_gate", (2, 3072)),
               ("ssm_lambda_re", (2, 48, 64)), ("ssm_lambda_im", (2, 48, 64)), ("ssm_log_dt", (2, 48)),
               ("ssm_b_re", (2, 48, 64, 16)), ("ssm_b_im", (2, 48, 64, 16)), ("ssm_c_re", (2, 48, 16, 64)),
               ("ssm_c_im", (2, 48, 16, 64)), ("ssm_d", (2, 768)), ("b_glu", (2, 768)), ("rel_bias", (32, 12)),
               ("final_norm_g", (1024,)))
_PER_LAYER = tuple((n, s[1:]) for n, s in _REPLICATED if s[0] == DEPTH and len(s) > 1)
_SHARED = tuple((n, s) for n, s in _REPLICATED if (n, s[1:]) not in _PER_LAYER)
_REP_HALF_ROWS = 1664
_REP_ROWS = 2 * _REP_HALF_ROWS
assert sum(int(np.prod(s)) for _, s in _PER_LAYER + _SHARED) <= _REP_HALF_ROWS * LANES


def _pack_half(tree, layer, shared):
    flat = [tree[n][layer].reshape(-1) for n, _ in _PER_LAYER]
    if shared:
        flat += [tree[n].reshape(-1) for n, _ in _SHARED]
    flat = jnp.concatenate(flat)
    return jnp.pad(flat, (0, _REP_HALF_ROWS * LANES - flat.shape[0])).reshape(_REP_HALF_ROWS, LANES)


def _pack_replicated(tree):
    return jnp.concatenate([_pack_half(tree, 1, False), _pack_half(tree, 0, True)])[None]


def _unpack_replicated(packed):
    halves = packed.reshape(2, -1)
    out, r = {}, 0
    for n, s in _PER_LAYER:
        size = int(np.prod(s))
        out[n] = jnp.stack([halves[1, r:r + size].reshape(s), halves[0, r:r + size].reshape(s)])
        r += size
    for n, s in _SHARED:
        size = int(np.prod(s))
        out[n] = halves[1, r:r + size].reshape(s)
        r += size
    return out


def _discretize(lam_re, lam_im, log_dt, b_re, b_im):
    dt = jnp.exp(log_dt)[:, None]
    mag = jnp.exp(lam_re * dt)
    abar_re, abar_im = mag * jnp.cos(lam_im * dt), mag * jnp.sin(lam_im * dt)
    den = lam_re * lam_re + lam_im * lam_im
    nr, ni = abar_re - 1.0, abar_im
    f_re = (nr * lam_re + ni * lam_im) / den
    f_im = (ni * lam_re - nr * lam_im) / den
    bbar_re = f_re[..., None] * b_re - f_im[..., None] * b_im
    bbar_im = f_re[..., None] * b_im + f_im[..., None] * b_re
    return abar_re, abar_im, bbar_re, bbar_im


def _block_diag(a):
    _, R, C = a.shape
    a = a.reshape(SSM_BLOCKS, 8, R, C)
    eye = jnp.eye(8, dtype=a.dtype)
    return (a[:, :, :, None, :] * eye[None, :, None, :, None]).reshape(SSM_BLOCKS, 8 * R, 8 * C)


def _diag_blocks(a, R, C):
    a = a.reshape(SSM_BLOCKS, 8, R, 8, C)
    eye = jnp.eye(8, dtype=a.dtype)
    return jnp.sum(a * eye[None, :, None, :, None], axis=3).reshape(SSM_GROUPS, R, C)


def _carried(result, job):
    return (result, None) if job is None else result


def _layer_fwd(x, mem, W, P, bias, layer, jobs):
    tag = f"l{layer}"
    abar_re, abar_im, bbar_re, bbar_im = _discretize(P["ssm_lambda_re"][layer], P["ssm_lambda_im"][layer],
                                                     P["ssm_log_dt"][layer], P["ssm_b_re"][layer], P["ssm_b_im"][layer])
    c_re, c_im = P["ssm_c_re"][layer], P["ssm_c_im"][layer]
    ssm = dict(
        are=abar_re.reshape(1, N_STATE), aim=abar_im.reshape(1, N_STATE),
        bre=_block_diag(bbar_re.transpose(0, 2, 1)).astype(BF16), bim=_block_diag(bbar_im.transpose(0, 2, 1)).astype(BF16),
        cre=_block_diag(c_re.transpose(0, 2, 1)).astype(BF16), cimn=_block_diag(-c_im.transpose(0, 2, 1)).astype(BF16),
        d=P["ssm_d"][layer].reshape(1, D_SSM))
    bglu = P["b_glu"][layer].reshape(1, D_SSM)
    bgate = P["b_gate"][layer].reshape(1, N_GATES)
    g = P["norm_g"][layer].reshape(1, D_MODEL)
    gm = P["mem_norm_g"][layer].reshape(1, D_MODEL)
    delivered = {}

    def carry(stage):
        return jobs[stage][0] if stage in jobs else None

    def deliver(stage, landed):
        if landed is not None:
            delivered[stage] = _landed_weights(jobs[stage][1], landed)

    T = x.shape[0]
    (proj, h), landed = _carried(_norm_proj(x, g, W["w_in"], min(T, 1024), 2176, f"{tag}_proj", job=carry("proj"),
                                            w_turned=True), carry("proj"))
    deliver("proj", landed)
    W = {**W, **delivered.get("proj", {})}
    (xr, xi, y, o_ssm), landed = _carried(
        _ssm_fwd(proj, ssm["bre"], ssm["bim"], ssm["cre"], ssm["cimn"], ssm["are"], ssm["aim"], ssm["d"], W["w_glu"],
                 bglu, 512, f"{tag}_ssm", job=carry("ssm")), carry("ssm"))
    deliver("ssm", landed)
    os, lses = [], []
    for grp in range(3):
        stage = f"attn{grp}"
        (o_g, lse_g), landed = _carried(_attn_fwd(proj, bias[grp], grp, f"{tag}_{stage}", job=carry(stage)), carry(stage))
        deliver(stage, landed)
        os.append(o_g)
        lses.append(lse_g)
    o_attn = _attn_mix(os, lses, proj, min(T, ROW_TILE), f"{tag}_attn_mix")
    kvb, hm = _norm_proj(mem, gm, W["w_mem_kv"], mem.shape[0], 1024, f"{tag}_mem_kv", out_dtype=BF16)
    o_mem = _mem_fwd(proj, kvb, min(T, ROW_TILE), f"{tag}_mem")
    x_out, merged = _merge_fwd(x, o_ssm, o_attn, o_mem, proj, bgate, W["w_br_ssm"], W["w_br_attn"], W["w_br_mem"],
                               W["w_out"], 512, f"{tag}_merge")
    res = dict(x=x, mem=mem, proj=proj, h=h, xr=xr, xi=xi, y=y, o_ssm=o_ssm, os=os, lses=lses,
               o_attn=o_attn, kvb=kvb, hm=hm, o_mem=o_mem, merged=merged, ssm=ssm, bglu=bglu,
               bgate=bgate, g=g, gm=gm, W=W)
    return x_out, res, delivered


def _layer_bwd(dx, res, P, bias, layer, jobs):
    tag = f"l{layer}b"
    proj, ssm, W = res["proj"], res["ssm"], res["W"]
    T = dx.shape[0]
    landed = {}

    def run(stage, fn, job):
        out, landed[stage] = _carried(fn(job), job)
        if job is None:
            del landed[stage]
        return out

    dproj, dbr, do_ssm, do_attn, do_mem, dbg = run(
        "merge", lambda job: _merge_bwd(dx, res["o_ssm"], res["o_attn"], res["o_mem"], proj, res["bgate"], W["w_br_ssm"],
                                        W["w_br_attn"], W["w_br_mem"], W["w_out"], 512, f"{tag}_merge", job=job),
        jobs.get("merge"))
    gw = {}
    tk = min(T, 1024)
    gw["w_out"] = _mm_tn(res["merged"], dx, 1024, 1024, tk, f"{tag}_dw_out")
    gw["w_br_ssm"] = _mm_tn(res["o_ssm"], dbr, 768, 1024, tk, f"{tag}_dw_br_ssm", b_col=0, n=1024)
    gw["w_br_attn"] = _mm_tn(res["o_attn"], dbr, 768, 1024, tk, f"{tag}_dw_br_attn", b_col=1024, n=1024)
    gw["w_br_mem"] = _mm_tn(res["o_mem"], dbr, 512, 1024, tk, f"{tag}_dw_br_mem", b_col=2048, n=1024)

    rows = min(T, ROW_TILE)
    dqm, dzm, dkv = _mem_bwd(do_mem, proj, res["kvb"], rows, f"{tag}_mem")
    M = dkv.shape[0]
    gw["w_mem_kv"] = _mm_tn(res["hm"], dkv, 1024, 1024, M, f"{tag}_dw_mem_kv")
    _, dgm = _proj_bwd(dkv.astype(BF16), W["w_mem_kv"], res["mem"], res["gm"], jnp.zeros_like(res["mem"]), M, 1024,
                       f"{tag}_mem_norm")

    do_g, corr, dproj = _attn_mix_bwd(do_attn, res["os"], res["lses"], proj, dproj, rows, f"{tag}_attn_mix")
    dbs = []
    for grp in range(3):
        dproj, db_g = _attn_bwd(proj, do_g, corr, res["lses"][grp], bias[grp], dproj, grp, f"{tag}_attn{grp}")
        dbs.append(db_g)
    dbias = jnp.stack(dbs)

    dy, dproj, gelu_b, dt_b, dbglu = _glu_bwd(do_ssm, res["y"], proj, W["w_glu"], res["bglu"], dproj, rows, f"{tag}_glu")
    gw["w_glu"] = _mm_tn(gelu_b, dt_b, 768, 768, tk, f"{tag}_dw_glu")
    dproj, dbre, dbim, dcre, dcim, dare, daim, dd = run(
        "ssm", lambda job: _ssm_bwd(dy, proj, res["xr"], res["xi"], ssm["bre"], ssm["bim"], ssm["cre"], ssm["cimn"],
                                    ssm["are"], ssm["aim"], ssm["d"], dproj, 256, f"{tag}_ssm", job=job),
        jobs.get("ssm"))
    _, disc_vjp = jax.vjp(_discretize, P["ssm_lambda_re"][layer], P["ssm_lambda_im"][layer], P["ssm_log_dt"][layer],
                          P["ssm_b_re"][layer], P["ssm_b_im"][layer])
    d_lre, d_lim, d_ldt, d_bre, d_bim = disc_vjp((dare.reshape(SSM_GROUPS, SSM_STATE), daim.reshape(SSM_GROUPS, SSM_STATE),
                                                  _diag_blocks(dbre, SSM_STATE, SSM_GROUP),
                                                  _diag_blocks(dbim, SSM_STATE, SSM_GROUP)))

    small = [gw[_SHARDED[i][0]] for i in _SMALL]
    for seg, piece in (("q_mem", dqm), ("z_mem", dzm)):
        dproj = lax.dynamic_update_slice(dproj, piece, (0, _OFF[seg]))
    dw_in = run("dw_in", lambda job: _mm_tn(dproj, res["h"], 2176, 1024, min(T, 1024), f"{tag}_dw_in", job=job),
                jobs["dw_in"](small) if "dw_in" in jobs else None)
    dw_in = dw_in.reshape((N_DEV,) + _SHARDED[_W_IN][1])
    dx_in, dg = run("proj", lambda job: _proj_bwd(dproj, W["w_in"], res["x"], res["g"], dx, min(T, 1024), 2176,
                                                  f"{tag}_proj", job=job, w_turned=True),
                    jobs["proj"](small, dw_in, landed) if "proj" in jobs else None)

    gp = dict(norm_g=dg[0], mem_norm_g=dgm[0], b_gate=dbg[0], ssm_lambda_re=d_lre, ssm_lambda_im=d_lim,
              ssm_log_dt=d_ldt, ssm_b_re=d_bre, ssm_b_im=d_bim,
              ssm_c_re=_diag_blocks(dcre, SSM_GROUP, SSM_STATE), ssm_c_im=_diag_blocks(dcim, SSM_GROUP, SSM_STATE),
              ssm_d=dd[0], b_glu=dbglu[0])
    return dx_in, dw_in, gp, dbias, landed


def _train_step(x, mem, target, shards, P):
    rest0 = [(i, 0) for i in _SMALL]
    thirds1 = [[(i, 1) for i in _SMALL[k::3]] for k in range(3)]
    first = [(_W_IN, 0)]
    w_in0 = _gather_via_sibling(shards[_W_IN], lambda ref: ref.at[0], lambda ref, s: ref.at[s],
                                jax.ShapeDtypeStruct((N_DEV,) + _SHARDED[_W_IN][1], BF16), "gather_w_in0")
    W0 = _landed_weights(first, [w_in0])
    buckets = _bucket_tables()
    bias = _bias_tables(P["rel_bias"], buckets, "bias_tables")
    jobs0 = {"proj": (_gather_job(shards, rest0), rest0), "ssm": (_gather_job(shards, [(_W_IN, 1)]), [(_W_IN, 1)]),
             **{f"attn{k}": (_gather_job(shards, items), items) for k, items in enumerate(thirds1)}}
    x, res0, delivered = _layer_fwd(x, mem, W0, P, bias, 0, jobs0)
    W1 = {**delivered["ssm"], **delivered["attn0"], **delivered["attn1"], **delivered["attn2"]}
    x, res1, _ = _layer_fwd(x, mem, W1, P, bias, 1, {})
    loss, dx, dgf = _loss_head(x, P["final_norm_g"].reshape(1, D_MODEL), target, min(x.shape[0], ROW_TILE),
                                "loss_head")

    dx, _, gp1, dbias1, landed1 = _layer_bwd(
        dx, res1, P, bias, 1,
        {"dw_in": lambda small: _scatter_job(small, _SMALL, 1),
         "proj": lambda small, dw_in, landed: _scatter_job([dw_in], [_W_IN], 1)})
    rep1 = _pack_half({n: a[None] for n, a in gp1.items()}, 0, False)
    dx, _, gp0, dbias0, landed0 = _layer_bwd(
        dx, res0, P, bias, 0,
        {"merge": _rows_job(rep1, 0),
         "dw_in": lambda small: _scatter_job(small, _SMALL, 0, parts=landed1["dw_in"]),
         "proj": lambda small, dw_in, landed: _scatter_job([dw_in], [_W_IN], 0, parts=landed1["proj"])})
    d_rel = _bias_grad(dbias0, dbias1, buckets, "bias_grad")
    gp0 = {n: a[None] for n, a in gp0.items()}
    gp0["rel_bias"] = jnp.sum(d_rel, axis=-1).transpose(2, 0, 1).reshape(NUM_BUCKETS, 12)
    gp0["final_norm_g"] = dgf[0]
    rep0 = _pack_half(gp0, 0, True)
    rparts = _gather_via_sibling(rep0, lambda ref: ref, lambda ref, s: ref.at[s, pl.ds(_REP_HALF_ROWS, _REP_HALF_ROWS)],
                                 jax.ShapeDtypeStruct((N_DEV, _REP_ROWS, LANES), F32), "gather_small_grads0",
                                 landing=landed0["merge"][0])
    return loss[0, 0], dx, list(landed0["proj"]) + list(landed0["dw_in"]), rparts


_WEIGHTS = ["norm_g", "mem_norm_g", "w_in", "b_gate", "ssm_lambda_re", "ssm_lambda_im", "ssm_log_dt", "ssm_b_re",
            "ssm_b_im", "ssm_c_re", "ssm_c_im", "ssm_d", "w_glu", "b_glu", "w_mem_kv", "w_br_ssm", "w_br_attn",
            "w_br_mem", "w_out", "rel_bias", "final_norm_g"]
_ADAM_ROWS = {"w_in": 136,"w_glu": 96, "w_mem_kv": 128, "w_br_ssm": 768, "w_br_attn": 768, "w_br_mem": 512,
              "w_out": 128}


def kernel(x, mem, norm_g, mem_norm_g, w_in, b_gate, ssm_lambda_re, ssm_lambda_im, ssm_log_dt, ssm_b_re, ssm_b_im, ssm_c_re, ssm_c_im, ssm_d, w_glu, b_glu, w_mem_kv, w_br_ssm, w_br_attn, w_br_mem, w_out, rel_bias, final_norm_g, loss_target, m_norm_g, m_mem_norm_g, m_w_in, m_b_gate, m_ssm_lambda_re, m_ssm_lambda_im, m_ssm_log_dt, m_ssm_b_re, m_ssm_b_im, m_ssm_c_re, m_ssm_c_im, m_ssm_d, m_w_glu, m_b_glu, m_w_mem_kv, m_w_br_ssm, m_w_br_attn, m_w_br_mem, m_w_out, m_rel_bias, m_final_norm_g, v_norm_g, v_mem_norm_g, v_w_in, v_b_gate, v_ssm_lambda_re, v_ssm_lambda_im, v_ssm_log_dt, v_ssm_b_re, v_ssm_b_im, v_ssm_c_re, v_ssm_c_im, v_ssm_d, v_w_glu, v_b_glu, v_w_mem_kv, v_w_br_ssm, v_w_br_attn, v_w_br_mem, v_w_out, v_rel_bias, v_final_norm_g):
    given = dict(locals())
    w = {n: given[n] for n in _WEIGHTS}
    m = {n: given["m_" + n] for n in _WEIGHTS}
    v = {n: given["v_" + n] for n in _WEIGHTS}

    turned = lambda n, a: a.swapaxes(1, 2) if n == "w_in" else a
    shards = [turned(n, w[n]).astype(BF16) for n, _, _ in _SHARDED]
    loss, dx, parts, rparts = _train_step(x[0], mem[0], loss_target[0], shards, w)
    loss = lax.psum(loss, ("x", "y", "c"))

    new = {}
    for (n, _, _), p in zip(_SHARDED, parts):
        new[n] = [turned(n, a) for a in _adamw(p, turned(n, w[n]), turned(n, m[n]), turned(n, v[n]), _ADAM_ROWS[n],
                                               f"adamw_{n}")]
    rp = [_unpack_replicated(a) for a in _adamw(rparts[:, None], _pack_replicated(w), _pack_replicated(m),
                                                _pack_replicated(v), _REP_ROWS // 4, "adamw_replicated")]
    for n, _ in _REPLICATED:
        new[n] = [rp[kind][n] for kind in range(4)]
    outs = [loss, dx[None]]
    for kind in range(4):
        outs.extend(new[n][kind] for n in _WEIGHTS)
    return tuple(outs)
```

```python
import functools
import math
from typing import Callable, NamedTuple

import jax
import jax.numpy as jnp
import numpy as np
from jax import lax
from jax.experimental import pallas as pl
from jax.experimental.pallas import tpu as pltpu

F32 = jnp.float32
BF16 = jnp.bfloat16

D_MODEL = 1024
DEPTH = 2
EPS = 1e-6
D_SSM = 768
SSM_GROUP = 16
SSM_GROUPS = 48
SSM_STATE = 64
N_STATE = SSM_GROUPS * SSM_STATE
SSM_BLOCKS = 6
D_ATTN = 768
ATTN_HEAD_DIM = 64
ATTN_GROUP_WIDTH = 256
ATTN_DILATIONS = (1, 4, 16)
ATTN_SPAN = 128
ATTN_BLOCK = 128
NUM_BUCKETS = 32
REL_MAX_DISTANCE = 2048
NEG_INF = -1e30
MEM_HEADS = 4
MEM_HEAD_DIM = 128
D_MEM = 512
N_GATES = 3 * D_MODEL
D_IN = 8704
N_DEV = 8
LANES = 128
ADAM_LR = 0.001
ADAM_B1 = 0.9
ADAM_B2 = 0.999
ADAM_EPS = 1e-08
ADAM_WD = 0.01
ADAM_STEP = 10

_OFF = {"u": 0, "z_ssm": 768, "q": 1536, "k": 2304, "v": 3072, "z_attn": 3840, "q_mem": 4608, "z_mem": 5120,
        "gates": 5632}
GATE_BLOCK = 512
ROW_TILE = 1024

NN = (((1,), (0,)), ((), ()))
NT = (((1,), (1,)), ((), ()))
TN = (((0,), (0,)), ((), ()))

VMEM_LIMIT = 56 * 1024 * 1024


def _dot(a, b, dims=NN):
    return lax.dot_general(a, b, dims, preferred_element_type=F32)


def _sigmoid(x):
    return 1.0 / (1.0 + jnp.exp(-x))


def _gelu_parts(x):
    k = math.sqrt(2.0 / math.pi)
    t = jnp.tanh(k * (x + 0.044715 * (x * x * x)))
    cdf = 0.5 * (1.0 + t)
    dcdf = 0.5 * (1.0 - t * t) * k * (1.0 + 3.0 * 0.044715 * (x * x))
    return x * cdf, cdf + x * dcdf


def _params(sem, vmem=VMEM_LIMIT):
    return pltpu.CompilerParams(dimension_semantics=sem, vmem_limit_bytes=vmem)


def _full(shape):
    return pl.BlockSpec(shape, lambda *_: (0,) * len(shape))


def _norm_proj(x, g, w, tm, tn, name, out_dtype=F32, job=None, w_turned=False):
    T, D = x.shape
    N = w.shape[0] if w_turned else w.shape[1]
    w_spec = pl.BlockSpec((tn, D), lambda i, j: (j, 0)) if w_turned else pl.BlockSpec((D, tn), lambda i, j: (0, j))
    dims = NT if w_turned else NN

    def body(x_ref, g_ref, w_ref, o_ref, h_ref, hs):
        @pl.when(pl.program_id(1) == 0)
        def _():
            xv = x_ref[...]
            r = lax.rsqrt(jnp.mean(xv * xv, axis=-1, keepdims=True) + EPS)
            hv = (xv * r * g_ref[...]).astype(BF16)
            hs[...] = hv
            h_ref[...] = hv

        o_ref[...] = _dot(hs[...], w_ref[...], dims).astype(out_dtype)

    return _pc(
        body, job, name=name, grid=(T // tm, N // tn),
        in_specs=[pl.BlockSpec((tm, D), lambda i, j: (i, 0)), _full((1, D)), w_spec],
        out_specs=[pl.BlockSpec((tm, tn), lambda i, j: (i, j)), pl.BlockSpec((tm, D), lambda i, j: (i, 0))],
        out_shape=[jax.ShapeDtypeStruct((T, N), out_dtype), jax.ShapeDtypeStruct((T, D), BF16)],
        scratch_shapes=[pltpu.VMEM((tm, D), BF16)], sem=("parallel", "arbitrary"), operands=(x, g, w))


def _mm_tn(a, b, tm, tn, tk, name, b_col=0, n=None, job=None):
    K, M = a.shape
    N = b.shape[1] if n is None else n
    nk = K // tk
    j0 = b_col // tn

    def body(a_ref, b_ref, o_ref, acc):
        k = pl.program_id(2)

        @pl.when(k == 0)
        def _():
            acc[...] = jnp.zeros_like(acc)

        acc[...] += _dot(a_ref[...].astype(BF16), b_ref[...].astype(BF16), TN)

        @pl.when(k == nk - 1)
        def _():
            o_ref[...] = acc[...].astype(BF16)

    out = _pc(
        body, job, name=name, grid=(M // tm, N // tn, nk),
        in_specs=[pl.BlockSpec((tk, tm), lambda i, j, k: (k, i)), pl.BlockSpec((tk, tn), lambda i, j, k: (k, j0 + j))],
        out_specs=[pl.BlockSpec((tm, tn), lambda i, j, k: (i, j))],
        out_shape=[jax.ShapeDtypeStruct((M, N), BF16)],
        scratch_shapes=[pltpu.VMEM((tm, tn), F32)], sem=("parallel", "parallel", "arbitrary"), operands=(a, b))
    return out[0] if job is None else (out[0][0], out[1])


def _proj_bwd(dp, w, x, g, dres, tm, tk, name, job=None, w_turned=False):
    T, N = dp.shape
    D = x.shape[1]
    nk = N // tk
    w_spec = pl.BlockSpec((tk, D), lambda i, k: (k, 0)) if w_turned else pl.BlockSpec((D, tk), lambda i, k: (0, k))
    dims = NN if w_turned else NT

    def body(dp_ref, w_ref, x_ref, g_ref, dres_ref, dx_ref, dg_ref, acc):
        i, k = pl.program_id(0), pl.program_id(1)

        @pl.when(k == 0)
        def _():
            acc[...] = jnp.zeros_like(acc)

        @pl.when((i == 0) & (k == 0))
        def _():
            dg_ref[...] = jnp.zeros_like(dg_ref)

        acc[...] += _dot(dp_ref[...], w_ref[...], dims)

        @pl.when(k == nk - 1)
        def _():
            xv = x_ref[...]
            dh = acc[...]
            r = lax.rsqrt(jnp.mean(xv * xv, axis=-1, keepdims=True) + EPS)
            xr = xv * r
            dg_ref[...] += jnp.sum(dh * xr, axis=0, keepdims=True)
            wv = dh * g_ref[...]
            dx_ref[...] = dres_ref[...] + r * (wv - xr * jnp.mean(wv * xr, axis=-1, keepdims=True))

    return _pc(
        body, job, name=name, grid=(T // tm, nk),
        in_specs=[pl.BlockSpec((tm, tk), lambda i, k: (i, k)), w_spec,
                  pl.BlockSpec((tm, D), lambda i, k: (i, 0)), _full((1, D)),
                  pl.BlockSpec((tm, D), lambda i, k: (i, 0))],
        out_specs=[pl.BlockSpec((tm, D), lambda i, k: (i, 0)), _full((1, D))],
        out_shape=[jax.ShapeDtypeStruct((T, D), F32), jax.ShapeDtypeStruct((1, D), F32)],
        scratch_shapes=[pltpu.VMEM((tm, D), F32)], sem=("arbitrary", "arbitrary"), operands=(dp, w, x, g, dres))


def _ssm_fwd(proj, bre, bim, cre, cimn, are, aim, d, wglu, bglu, tc, name, job=None):
    T = proj.shape[0]
    ucol, zcol = _OFF["u"] // D_SSM, _OFF["z_ssm"] // D_SSM

    def body(u_ref, z_ref, bre_ref, bim_ref, cre_ref, cim_ref, are_ref, aim_ref, d_ref, wg_ref, bg_ref,
             xr_ref, xi_ref, y_ref, o_ref, car_r, car_i):
        @pl.when(pl.program_id(0) == 0)
        def _():
            car_r[...] = jnp.zeros_like(car_r)
            car_i[...] = jnp.zeros_like(car_i)

        u = u_ref[...]
        ub = u.astype(BF16)
        for k in range(SSM_BLOCKS):
            uk = ub[:, 128 * k:128 * (k + 1)]
            xr_ref[:, 512 * k:512 * (k + 1)] = _dot(uk, bre_ref[k])
            xi_ref[:, 512 * k:512 * (k + 1)] = _dot(uk, bim_ref[k])
        ar, ai = are_ref[...], aim_ref[...]

        def step(t, c):
            pr, pi = c
            nr = ar * pr - ai * pi + xr_ref[pl.ds(t, 1), :]
            ni = ar * pi + ai * pr + xi_ref[pl.ds(t, 1), :]
            xr_ref[pl.ds(t, 1), :] = nr
            xi_ref[pl.ds(t, 1), :] = ni
            return nr, ni

        pr, pi = lax.fori_loop(0, tc, step, (car_r[...], car_i[...]))
        car_r[...] = pr
        car_i[...] = pi

        ys = []
        for k in range(SSM_BLOCKS):
            xrk = xr_ref[:, 512 * k:512 * (k + 1)].astype(BF16)
            xik = xi_ref[:, 512 * k:512 * (k + 1)].astype(BF16)
            ys.append(_dot(xrk, cre_ref[k]) + _dot(xik, cim_ref[k]))
        y = jnp.concatenate(ys, axis=1) + d_ref[...] * u
        y_ref[...] = y
        gl, _ = _gelu_parts(y)
        t = _dot(gl.astype(BF16), wg_ref[...]) + bg_ref[...]
        z = z_ref[...]
        o_ref[...] = (gl * _sigmoid(t) * (z * _sigmoid(z))).astype(BF16)

    return _pc(
        body, job, name=name, grid=(T // tc,),
        in_specs=[pl.BlockSpec((tc, D_SSM), lambda i: (i, ucol)), pl.BlockSpec((tc, D_SSM), lambda i: (i, zcol)),
                  _full((SSM_BLOCKS, 128, 512)), _full((SSM_BLOCKS, 128, 512)),
                  _full((SSM_BLOCKS, 512, 128)), _full((SSM_BLOCKS, 512, 128)),
                  _full((1, N_STATE)), _full((1, N_STATE)), _full((1, D_SSM)),
                  _full((D_SSM, D_SSM)), _full((1, D_SSM))],
        out_specs=[pl.BlockSpec((tc, N_STATE), lambda i: (i, 0)), pl.BlockSpec((tc, N_STATE), lambda i: (i, 0)),
                   pl.BlockSpec((tc, D_SSM), lambda i: (i, 0)), pl.BlockSpec((tc, D_SSM), lambda i: (i, 0))],
        out_shape=[jax.ShapeDtypeStruct((T, N_STATE), F32), jax.ShapeDtypeStruct((T, N_STATE), F32),
                   jax.ShapeDtypeStruct((T, D_SSM), F32), jax.ShapeDtypeStruct((T, D_SSM), BF16)],
        scratch_shapes=[pltpu.VMEM((1, N_STATE), F32), pltpu.VMEM((1, N_STATE), F32)], sem=("arbitrary",),
        operands=(proj, proj, bre, bim, cre, cimn, are, aim, d, wglu, bglu))


def _glu_bwd(do, y, proj, wglu, bglu, dproj, tm, name):
    T = y.shape[0]
    zcol = _OFF["z_ssm"] // D_SSM

    def body(do_ref, y_ref, z_ref, wg_ref, bg_ref, _, dy_ref, dz_ref, g_ref, dt_ref, db_ref):
        @pl.when(pl.program_id(0) == 0)
        def _():
            db_ref[...] = jnp.zeros_like(db_ref)

        dov = do_ref[...]
        gl, dgl = _gelu_parts(y_ref[...])
        glb = gl.astype(BF16)
        sg = _sigmoid(_dot(glb, wg_ref[...]) + bg_ref[...])
        z = z_ref[...]
        sz = _sigmoid(z)
        dz_ref[...] = (dov * (gl * sg) * (sz * (1.0 + z * (1.0 - sz)))).astype(BF16)
        dy2 = dov * (z * sz)
        dt = dy2 * gl * (sg * (1.0 - sg))
        dtb = dt.astype(BF16)
        dg = dy2 * sg + _dot(dtb, wg_ref[...], NT)
        dy_ref[...] = dg * dgl
        g_ref[...] = glb
        dt_ref[...] = dtb
        db_ref[...] += jnp.sum(dt, axis=0, keepdims=True)

    row = lambda i: (i, 0)
    return pl.pallas_call(
        body, name=name, grid=(T // tm,),
        in_specs=[pl.BlockSpec((tm, D_SSM), row), pl.BlockSpec((tm, D_SSM), row),
                  pl.BlockSpec((tm, D_SSM), lambda i: (i, zcol)), _full((D_SSM, D_SSM)), _full((1, D_SSM)),
                  pl.BlockSpec(memory_space=pl.ANY)],
        out_specs=[pl.BlockSpec((tm, D_SSM), row), pl.BlockSpec((tm, D_SSM), lambda i: (i, zcol)),
                   pl.BlockSpec((tm, D_SSM), row), pl.BlockSpec((tm, D_SSM), row), _full((1, D_SSM))],
        out_shape=[jax.ShapeDtypeStruct((T, D_SSM), F32), jax.ShapeDtypeStruct(dproj.shape, BF16),
                   jax.ShapeDtypeStruct((T, D_SSM), BF16), jax.ShapeDtypeStruct((T, D_SSM), BF16),
                   jax.ShapeDtypeStruct((1, D_SSM), F32)],
        input_output_aliases={5: 1},
        compiler_params=_params(("arbitrary",)),
    )(do, y, proj, wglu, bglu, dproj)


def _ssm_bwd(dy, proj, xr, xi, bre, bim, cre, cimn, are, aim, d, dproj, tc, name, job=None):
    T = dy.shape[0]
    nc = T // tc
    ucol = _OFF["u"] // D_SSM
    rb = tc // 8

    def body(dy_ref, u_ref, xr_ref, xi_ref, xpr_ref, xpi_ref, bre_ref, bim_ref, cre_ref, cim_ref,
             are_ref, aim_ref, d_ref, _,
             du_ref, dbre_ref, dbim_ref, dcre_ref, dcim_ref, dare_ref, daim_ref, dd_ref, gr, gi, car_r, car_i):
        i = pl.program_id(0)

        @pl.when(i == 0)
        def _():
            for ref in (car_r, car_i, dbre_ref, dbim_ref, dcre_ref, dcim_ref, dare_ref, daim_ref, dd_ref):
                ref[...] = jnp.zeros_like(ref)

        dyv = dy_ref[...]
        dyb = dyv.astype(BF16)
        u = u_ref[...]
        ub = u.astype(BF16)
        for k in range(SSM_BLOCKS):
            dk = dyb[:, 128 * k:128 * (k + 1)]
            gr[:, 512 * k:512 * (k + 1)] = _dot(dk, cre_ref[k], NT)
            gi[:, 512 * k:512 * (k + 1)] = _dot(dk, cim_ref[k], NT)
        ar, ai = are_ref[...], aim_ref[...]

        def step(s, c):
            pr, pi = c
            t = tc - 1 - s
            nr = gr[pl.ds(t, 1), :] + ar * pr + ai * pi
            ni = gi[pl.ds(t, 1), :] + ar * pi - ai * pr
            gr[pl.ds(t, 1), :] = nr
            gi[pl.ds(t, 1), :] = ni
            return nr, ni

        pr, pi = lax.fori_loop(0, tc, step, (car_r[...], car_i[...]))
        car_r[...] = pr
        car_i[...] = pi

        keep = jnp.where(i == nc - 1, 0.0, 1.0)
        row0 = lax.broadcasted_iota(jnp.int32, (tc, 1), 0) == 0
        dd_ref[...] += jnp.sum(dyv * u, axis=0, keepdims=True)
        for k in range(SSM_BLOCKS):
            sl = slice(512 * k, 512 * (k + 1))
            ch = slice(128 * k, 128 * (k + 1))
            xrk, xik, grk, gik = xr_ref[:, sl], xi_ref[:, sl], gr[:, sl], gi[:, sl]
            xsr = jnp.where(row0, xpr_ref[7:8, sl] * keep, pltpu.roll(xrk, 1, axis=0))
            xsi = jnp.where(row0, xpi_ref[7:8, sl] * keep, pltpu.roll(xik, 1, axis=0))
            dare_ref[:, sl] += jnp.sum(grk * xsr + gik * xsi, axis=0, keepdims=True)
            daim_ref[:, sl] += jnp.sum(gik * xsr - grk * xsi, axis=0, keepdims=True)
            grb, gib = grk.astype(BF16), gik.astype(BF16)
            du_ref[:, ch] = (_dot(grb, bre_ref[k], NT) + _dot(gib, bim_ref[k], NT)
                             + d_ref[:, ch] * dyv[:, ch]).astype(BF16)
            dbre_ref[k] += _dot(grb, ub[:, ch], TN)
            dbim_ref[k] += _dot(gib, ub[:, ch], TN)
            dcre_ref[k] += _dot(dyb[:, ch], xrk.astype(BF16), TN)
            dcim_ref[k] -= _dot(dyb[:, ch], xik.astype(BF16), TN)

    rev = lambda i: (nc - 1 - i, 0)
    prev = lambda i: (jnp.maximum((nc - 1 - i) * rb - 1, 0), 0)
    return _pc(
        body, job, name=name, grid=(nc,),
        in_specs=[pl.BlockSpec((tc, D_SSM), rev), pl.BlockSpec((tc, D_SSM), lambda i: (nc - 1 - i, ucol)),
                  pl.BlockSpec((tc, N_STATE), rev), pl.BlockSpec((tc, N_STATE), rev),
                  pl.BlockSpec((8, N_STATE), prev), pl.BlockSpec((8, N_STATE), prev),
                  _full((SSM_BLOCKS, 128, 512)), _full((SSM_BLOCKS, 128, 512)),
                  _full((SSM_BLOCKS, 512, 128)), _full((SSM_BLOCKS, 512, 128)),
                  _full((1, N_STATE)), _full((1, N_STATE)), _full((1, D_SSM)), pl.BlockSpec(memory_space=pl.ANY)],
        out_specs=[pl.BlockSpec((tc, D_SSM), lambda i: (nc - 1 - i, ucol)),
                   _full((SSM_BLOCKS, 512, 128)), _full((SSM_BLOCKS, 512, 128)),
                   _full((SSM_BLOCKS, 128, 512)), _full((SSM_BLOCKS, 128, 512)),
                   _full((1, N_STATE)), _full((1, N_STATE)), _full((1, D_SSM))],
        out_shape=[jax.ShapeDtypeStruct(dproj.shape, BF16),
                   jax.ShapeDtypeStruct((SSM_BLOCKS, 512, 128), F32), jax.ShapeDtypeStruct((SSM_BLOCKS, 512, 128), F32),
                   jax.ShapeDtypeStruct((SSM_BLOCKS, 128, 512), F32), jax.ShapeDtypeStruct((SSM_BLOCKS, 128, 512), F32),
                   jax.ShapeDtypeStruct((1, N_STATE), F32), jax.ShapeDtypeStruct((1, N_STATE), F32),
                   jax.ShapeDtypeStruct((1, D_SSM), F32)],
        scratch_shapes=[pltpu.VMEM((tc, N_STATE), F32), pltpu.VMEM((tc, N_STATE), F32),
                        pltpu.VMEM((1, N_STATE), F32), pltpu.VMEM((1, N_STATE), F32)], sem=("arbitrary",),
        operands=(dy, proj, xr, xi, xr, xi, bre, bim, cre, cimn, are, aim, d, dproj), aliases={13: 0})


def _rel_bucket(dist):
    n = jnp.maximum(dist, 0)
    max_exact = NUM_BUCKETS // 2
    n_f = jnp.maximum(n, 1).astype(F32)
    large = max_exact + (jnp.log(n_f / max_exact) / math.log(REL_MAX_DISTANCE / max_exact)
                         * (NUM_BUCKETS - max_exact)).astype(jnp.int32)
    large = jnp.minimum(large, NUM_BUCKETS - 1)
    return jnp.where(n < max_exact, n, large)


def _bucket_tables():
    qi = jnp.arange(ATTN_BLOCK)[:, None]
    kj = jnp.arange(2 * ATTN_BLOCK)[None, :]
    delta = jnp.maximum(ATTN_BLOCK + qi - kj, 0)
    return jnp.stack([_rel_bucket(delta * r) for r in ATTN_DILATIONS]).astype(jnp.int32)


def _bias_tables(rel_bias, buckets, name):
    def body(tab_ref, bk_ref, o_ref):
        g = pl.program_id(0)
        bk = bk_ref[...]
        qi = lax.broadcasted_iota(jnp.int32, bk.shape, 0)
        kj = lax.broadcasted_iota(jnp.int32, bk.shape, 1)
        delta = ATTN_BLOCK + qi - kj
        band = (delta >= 0) & (delta <= ATTN_SPAN)
        accs = [jnp.zeros(bk.shape, F32) for _ in range(4)]
        for b in range(NUM_BUCKETS):
            hit = bk == b
            for h in range(4):
                accs[h] = jnp.where(hit, tab_ref[b, 4 * g + h], accs[h])
        for h in range(4):
            o_ref[h] = jnp.where(band, accs[h], NEG_INF)

    return pl.pallas_call(
        body, name=name, grid=(3,),
        in_specs=[pl.BlockSpec(memory_space=pltpu.SMEM),
                  pl.BlockSpec((None, ATTN_BLOCK, 2 * ATTN_BLOCK), lambda g: (g, 0, 0))],
        out_specs=pl.BlockSpec((None, 4, ATTN_BLOCK, 2 * ATTN_BLOCK), lambda g: (g, 0, 0, 0)),
        out_shape=jax.ShapeDtypeStruct((3, 4, ATTN_BLOCK, 2 * ATTN_BLOCK), F32),
        compiler_params=_params(("parallel",)),
    )(rel_bias, buckets)


def _bias_grad(db0, db1, buckets, name):
    def body(a_ref, b_ref, bk_ref, o_ref):
        bk = bk_ref[...]
        for h in range(4):
            dv = a_ref[h] + b_ref[h]
            for b in range(NUM_BUCKETS):
                o_ref[h, b:b + 1, :] = jnp.sum(jnp.where(bk == b, dv, 0.0), axis=0, keepdims=True)

    tab = pl.BlockSpec((None, 4, ATTN_BLOCK, 2 * ATTN_BLOCK), lambda g: (g, 0, 0, 0))
    return pl.pallas_call(
        body, name=name, grid=(3,),
        in_specs=[tab, tab, pl.BlockSpec((None, ATTN_BLOCK, 2 * ATTN_BLOCK), lambda g: (g, 0, 0))],
        out_specs=pl.BlockSpec((None, 4, NUM_BUCKETS, 2 * ATTN_BLOCK), lambda g: (g, 0, 0, 0)),
        out_shape=jax.ShapeDtypeStruct((3, 4, NUM_BUCKETS, 2 * ATTN_BLOCK), F32),
        compiler_params=_params(("parallel",)),
    )(db0, db1, buckets)


_ATTN_SUB = {1: 8, 4: 2, 16: 1}
_UNROLL = 4
_STATIC_UNITS = 8


def _unit_rows(j, s, r):
    start = j * ATTN_BLOCK * r + s
    return pl.ds(start, ATTN_BLOCK, stride=r) if r > 1 else pl.ds(start, ATTN_BLOCK)


def _for_units(r, nsub, fn, after):
    if r * nsub <= _STATIC_UNITS:
        units = [(j, s) for j in range(nsub) for s in range(r)]
        for i in range(0, len(units), _UNROLL):
            after([fn(j, s) for j, s in units[i:i + _UNROLL]])
    else:
        assert nsub == 1
        def four(i, c):
            after([fn(0, _UNROLL * i + k) for k in range(_UNROLL)])
            return c

        lax.fori_loop(0, r // _UNROLL, four, 0)


def _attn_cols(g):
    return tuple((_OFF[n] + ATTN_GROUP_WIDTH * g) // LANES for n in ("q", "k", "v"))


def _attn_fwd(proj, bias, g, name, job=None):
    r = ATTN_DILATIONS[g]
    nsub = _ATTN_SUB[r]
    T = proj.shape[0]
    sub = ATTN_BLOCK * r
    tb = sub * nsub
    qc, kc, vc = _attn_cols(g)
    scale = ATTN_HEAD_DIM ** -0.5

    def body(q_ref, kc_ref, kp_ref, vc_ref, vp_ref, bias_ref, o_ref, lse_ref):
        lane = lax.broadcasted_iota(jnp.int32, (ATTN_BLOCK, LANES), 1)
        kj = lax.broadcasted_iota(jnp.int32, (ATTN_BLOCK, 2 * ATTN_BLOCK), 1)
        dead = (pl.program_id(0) == 0) & (kj < ATTN_BLOCK)

        def one(j, s):
            rows = _unit_rows(j, s, r)
            before = _unit_rows(max(j - 1, 0), s, r)
            k_before = kc_ref[before, :] if j else kp_ref[before, :]
            v_before = vc_ref[before, :] if j else vp_ref[before, :]
            q = q_ref[rows, :]
            kcat = jnp.concatenate([k_before, kc_ref[rows, :]], axis=0).astype(BF16)
            vcat = jnp.concatenate([v_before, vc_ref[rows, :]], axis=0).astype(BF16)
            o_acc = jnp.zeros((ATTN_BLOCK, LANES), F32)
            l_acc = jnp.zeros((ATTN_BLOCK, LANES), F32)
            for hh in range(2):
                mine = (lane >= ATTN_HEAD_DIM) if hh else (lane < ATTN_HEAD_DIM)
                qm = jnp.where(mine, q, 0.0).astype(BF16)
                sc = _dot(qm, kcat, NT) * scale + bias_ref[hh]
                if j == 0:
                    sc = jnp.where(dead, NEG_INF, sc)
                m = jnp.max(sc, axis=-1, keepdims=True)
                p = jnp.exp(sc - m)
                l = jnp.sum(p, axis=-1, keepdims=True)
                o_acc = jnp.where(mine, _dot((p / l).astype(BF16), vcat), o_acc)
                l_acc = jnp.where(mine, m + jnp.log(l), l_acc)
            o_ref[rows, :] = o_acc
            lse_ref[rows, :] = l_acc

        _for_units(r, nsub, one, lambda results: None)

    cur = lambda c: pl.BlockSpec((tb, LANES), lambda b, p: (b, c + p))
    prev = lambda c: pl.BlockSpec((sub, LANES), lambda b, p: (jnp.maximum(b * nsub - 1, 0), c + p))
    out = pl.BlockSpec((tb, LANES), lambda b, p: (b, p))
    return _pc(
        body, job, name=name, grid=(T // tb, 2),
        in_specs=[cur(qc), cur(kc), prev(kc), cur(vc), prev(vc),
                  pl.BlockSpec((2, ATTN_BLOCK, 2 * ATTN_BLOCK), lambda b, p: (p, 0, 0))],
        out_specs=[out, out],
        out_shape=[jax.ShapeDtypeStruct((T, ATTN_GROUP_WIDTH), F32), jax.ShapeDtypeStruct((T, ATTN_GROUP_WIDTH), F32)],
        scratch_shapes=[], sem=("parallel", "parallel"), operands=(proj, proj, proj, proj, proj, bias))


def _attn_bwd(proj, do, corr, lse, bias, dproj, g, name):
    r = ATTN_DILATIONS[g]
    nsub = _ATTN_SUB[r]
    T = proj.shape[0]
    sub = ATTN_BLOCK * r
    tb = sub * nsub
    nb = T // tb
    qc, kc, vc = _attn_cols(g)
    dc = ATTN_GROUP_WIDTH * g // LANES
    scale = ATTN_HEAD_DIM ** -0.5

    def body(q_ref, kc_ref, kp_ref, vc_ref, vp_ref, do_ref, corr_ref, lse_ref, bias_ref, _,
             dproj_ref, db_ref, dq_s, dkc_s, dkp_s, dvc_s, dvp_s, kacc, vacc, stage, stage_sems):
        p, b = pl.program_id(0), pl.program_id(1)

        def to_dproj(e, slot, block, col):
            rows = pl.ds(pl.multiple_of(block * tb, tb), tb)
            cols = pl.ds(pl.multiple_of((col + p) * LANES, LANES), LANES)
            return pltpu.make_async_copy(stage.at[e, slot], dproj_ref.at[rows, cols], stage_sems.at[e, slot])

        def emit(e, block, col, value):
            count = p * nb + block
            slot = count % 2

            @pl.when(count >= 2)
            def _():
                to_dproj(e, slot, 0, col).wait()

            stage[e, slot] = value.astype(BF16)
            to_dproj(e, slot, block, col).start()

        def emit_keys(block):
            emit(1, block, kc, kacc[...])
            emit(2, block, vc, vacc[...])

        @pl.when(b == 0)
        def _():
            db_ref[...] = jnp.zeros_like(db_ref)
            kacc[...] = jnp.zeros_like(kacc)
            vacc[...] = jnp.zeros_like(vacc)

        @pl.when(b == nb)
        def _():
            emit_keys(nb - 1)

        @pl.when((b == nb) & (p == 1))
        def _():
            for e, col in enumerate((qc, kc, vc)):
                for slot in range(2):
                    to_dproj(e, slot, 0, col).wait()

        @pl.when(b < nb)
        def _():
            lane = lax.broadcasted_iota(jnp.int32, (ATTN_BLOCK, LANES), 1)
            kj = lax.broadcasted_iota(jnp.int32, (ATTN_BLOCK, 2 * ATTN_BLOCK), 1)
            dead = (b == 0) & (kj < ATTN_BLOCK)

            def one(j, s):
                rows = _unit_rows(j, s, r)
                before = _unit_rows(max(j - 1, 0), s, r)
                k_before = kc_ref[before, :] if j else kp_ref[before, :]
                v_before = vc_ref[before, :] if j else vp_ref[before, :]
                q = q_ref[rows, :]
                kcat = jnp.concatenate([k_before, kc_ref[rows, :]], axis=0).astype(BF16)
                vcat = jnp.concatenate([v_before, vc_ref[rows, :]], axis=0).astype(BF16)
                dov, corrv, lsev = do_ref[rows, :], corr_ref[rows, :], lse_ref[rows, :]
                dq_acc = jnp.zeros((ATTN_BLOCK, LANES), F32)
                dk_acc = jnp.zeros((2 * ATTN_BLOCK, LANES), F32)
                dv_acc = jnp.zeros((2 * ATTN_BLOCK, LANES), F32)
                dss = []
                for hh in range(2):
                    mine = (lane >= ATTN_HEAD_DIM) if hh else (lane < ATTN_HEAD_DIM)
                    col = slice(ATTN_HEAD_DIM * hh, ATTN_HEAD_DIM * hh + 1)
                    qm = jnp.where(mine, q, 0.0).astype(BF16)
                    dom = jnp.where(mine, dov, 0.0).astype(BF16)
                    sc = _dot(qm, kcat, NT) * scale + bias_ref[hh]
                    if j == 0:
                        sc = jnp.where(dead, NEG_INF, sc)
                    p = jnp.exp(sc - lsev[:, col])
                    ds = p * (_dot(dom, vcat, NT) - corrv[:, col])
                    dss.append(ds)
                    dsb = ds.astype(BF16)
                    dq_acc = jnp.where(mine, _dot(dsb, kcat) * scale, dq_acc)
                    dk_acc += _dot(dsb, qm, TN) * scale
                    dv_acc += _dot(p.astype(BF16), dom, TN)
                dq_s[rows, :] = dq_acc
                dkp_s[rows, :] = dk_acc[:ATTN_BLOCK]
                dkc_s[rows, :] = dk_acc[ATTN_BLOCK:]
                dvp_s[rows, :] = dv_acc[:ATTN_BLOCK]
                dvc_s[rows, :] = dv_acc[ATTN_BLOCK:]
                return dss

            def add_bias_grads(results):
                for hh in range(2):
                    db_ref[hh] += functools.reduce(lambda x, y: x + y, [dss[hh] for dss in results])

            _for_units(r, nsub, one, add_bias_grads)
            emit(0, b, qc, dq_s[...])
            tail = slice((nsub - 1) * sub, nsub * sub)
            kacc[tail, :] += dkp_s[0:sub, :]
            vacc[tail, :] += dvp_s[0:sub, :]

            @pl.when(b >= 1)
            def _():
                emit_keys(b - 1)

            for acc, before_s, cur_s in ((kacc, dkp_s, dkc_s), (vacc, dvp_s, dvc_s)):
                acc[...] = cur_s[...]
                for j in range(nsub - 1):
                    acc[j * sub:(j + 1) * sub, :] += before_s[(j + 1) * sub:(j + 2) * sub, :]

    last = nb - 1
    blk = (tb, LANES)
    cur = lambda c: pl.BlockSpec(blk, lambda p, b: (jnp.minimum(b, last), c + p))
    before = lambda c: pl.BlockSpec((sub, LANES), lambda p, b: (jnp.clip(b * nsub - 1, 0, nb * nsub - 1), c + p))
    tab = pl.BlockSpec((2, ATTN_BLOCK, 2 * ATTN_BLOCK), lambda p, b: (p, 0, 0))
    hbm = pl.BlockSpec(memory_space=pl.ANY)
    return pl.pallas_call(
        body, name=name, grid=(2, nb + 1),
        in_specs=[cur(qc), cur(kc), before(kc), cur(vc), before(vc), cur(dc), cur(dc), cur(0), tab, hbm],
        out_specs=[hbm, tab],
        out_shape=[jax.ShapeDtypeStruct(dproj.shape, BF16), jax.ShapeDtypeStruct((4, ATTN_BLOCK, 2 * ATTN_BLOCK), F32)],
        input_output_aliases={9: 0},
        scratch_shapes=[pltpu.VMEM(blk, F32)] * 7 + [pltpu.VMEM((3, 2) + blk, BF16), pltpu.SemaphoreType.DMA((3, 2))],
        compiler_params=_params(("arbitrary", "arbitrary")),
    )(proj, proj, proj, proj, proj, do, corr, lse, bias, dproj)


def _mix_weights(lses):
    m = jnp.maximum(jnp.maximum(lses[0], lses[1]), lses[2])
    es = [jnp.exp(l - m) for l in lses]
    inv = 1.0 / (es[0] + es[1] + es[2])
    return jnp.concatenate([e * inv for e in es], axis=1)


def _attn_mix(os, lses, proj, tm, name):
    T = proj.shape[0]
    zcol = _OFF["z_attn"] // D_ATTN

    def body(o0, o1, o2, l0, l1, l2, z_ref, out_ref):
        z = z_ref[...]
        o = jnp.concatenate([o0[...], o1[...], o2[...]], axis=1)
        alpha = _mix_weights([l0[...], l1[...], l2[...]])
        out_ref[...] = (o * alpha * (z * _sigmoid(z))).astype(BF16)

    row = lambda i: (i, 0)
    grp = pl.BlockSpec((tm, ATTN_GROUP_WIDTH), row)
    return pl.pallas_call(
        body, name=name, grid=(T // tm,),
        in_specs=[grp] * 6 + [pl.BlockSpec((tm, D_ATTN), lambda i: (i, zcol))],
        out_specs=pl.BlockSpec((tm, D_ATTN), row),
        out_shape=jax.ShapeDtypeStruct((T, D_ATTN), BF16),
        compiler_params=_params(("parallel",)),
    )(*os, *lses, proj)


def _attn_mix_bwd(d, os, lses, proj, dproj, tm, name):
    T = proj.shape[0]
    zcol = _OFF["z_attn"] // D_ATTN

    def body(d_ref, o0, o1, o2, l0, l1, l2, z_ref, _, do_ref, corr_ref, dz_ref):
        dv, z = d_ref[...], z_ref[...]
        ov = jnp.concatenate([o0[...], o1[...], o2[...]], axis=1)
        alpha = _mix_weights([l0[...], l1[...], l2[...]])
        sz = _sigmoid(z)
        oc = ov * alpha
        dz_ref[...] = (dv * oc * (sz * (1.0 + z * (1.0 - sz)))).astype(BF16)
        doc = dv * (z * sz)
        do_ref[...] = doc * alpha
        pr = doc * oc
        p3 = pr[:, 0:256] + pr[:, 256:512] + pr[:, 512:768]
        li = lax.broadcasted_iota(jnp.int32, (256, 256), 0) // ATTN_HEAD_DIM
        lj = lax.broadcasted_iota(jnp.int32, (256, 256), 1) // ATTN_HEAD_DIM
        ones = jnp.where(li == lj, 1.0, 0.0).astype(F32)
        s = lax.dot_general(p3, ones, NN, precision=lax.Precision.HIGHEST, preferred_element_type=F32)
        corr_ref[...] = alpha * jnp.concatenate([s, s, s], axis=1)

    row = lambda i: (i, 0)
    grp = pl.BlockSpec((tm, ATTN_GROUP_WIDTH), row)
    return pl.pallas_call(
        body, name=name, grid=(T // tm,),
        in_specs=[pl.BlockSpec((tm, D_ATTN), row)] + [grp] * 6 + [pl.BlockSpec((tm, D_ATTN), lambda i: (i, zcol)),
                                                                    pl.BlockSpec(memory_space=pl.ANY)],
        out_specs=[pl.BlockSpec((tm, D_ATTN), row)] * 2 + [pl.BlockSpec((tm, D_ATTN), lambda i: (i, zcol))],
        out_shape=[jax.ShapeDtypeStruct((T, D_ATTN), F32), jax.ShapeDtypeStruct((T, D_ATTN), F32),
                   jax.ShapeDtypeStruct(dproj.shape, BF16)],
        input_output_aliases={8: 2},
        compiler_params=_params(("parallel",)),
    )(d, *os, *lses, proj, dproj)


def _mem_probs(q_ref, kv_ref, h):
    hs = slice(MEM_HEAD_DIM * h, MEM_HEAD_DIM * (h + 1))
    qh = q_ref[:, hs].astype(BF16)
    kh = kv_ref[:, hs]
    vh = kv_ref[:, D_MEM + MEM_HEAD_DIM * h:D_MEM + MEM_HEAD_DIM * (h + 1)]
    s = _dot(qh, kh, NT) * (MEM_HEAD_DIM ** -0.5)
    p = jnp.exp(s - jnp.max(s, axis=-1, keepdims=True))
    pn = p / jnp.sum(p, axis=-1, keepdims=True)
    return qh, kh, vh, pn


def _mem_fwd(proj, kv, tm, name):
    T = proj.shape[0]
    M = kv.shape[0]
    qcol, zcol = _OFF["q_mem"] // D_MEM, _OFF["z_mem"] // D_MEM

    def body(q_ref, z_ref, kv_ref, o_ref):
        outs = []
        for h in range(MEM_HEADS):
            _, _, vh, pn = _mem_probs(q_ref, kv_ref, h)
            outs.append(_dot(pn.astype(BF16), vh))
        z = z_ref[...]
        o_ref[...] = (jnp.concatenate(outs, axis=1) * (z * _sigmoid(z))).astype(BF16)

    return pl.pallas_call(
        body, name=name, grid=(T // tm,),
        in_specs=[pl.BlockSpec((tm, D_MEM), lambda i: (i, qcol)), pl.BlockSpec((tm, D_MEM), lambda i: (i, zcol)),
                  _full((M, 2 * D_MEM))],
        out_specs=pl.BlockSpec((tm, D_MEM), lambda i: (i, 0)),
        out_shape=jax.ShapeDtypeStruct((T, D_MEM), BF16),
        compiler_params=_params(("parallel",)),
    )(proj, proj, kv)


def _mem_bwd(d, proj, kv, tm, name):
    T = proj.shape[0]
    M = kv.shape[0]
    qcol, zcol = _OFF["q_mem"] // D_MEM, _OFF["z_mem"] // D_MEM

    def body(d_ref, q_ref, z_ref, kv_ref, dq_ref, dz_ref, dkv_ref):
        @pl.when(pl.program_id(0) == 0)
        def _():
            dkv_ref[...] = jnp.zeros_like(dkv_ref)

        z = z_ref[...]
        sz = _sigmoid(z)
        dv = d_ref[...]
        dov = dv * (z * sz)
        scale = MEM_HEAD_DIM ** -0.5
        outs, dqs = [], []
        for h in range(MEM_HEADS):
            hs = slice(MEM_HEAD_DIM * h, MEM_HEAD_DIM * (h + 1))
            qh, kh, vh, pn = _mem_probs(q_ref, kv_ref, h)
            pnb = pn.astype(BF16)
            oh = _dot(pnb, vh)
            outs.append(oh)
            doh = dov[:, hs]
            dohb = doh.astype(BF16)
            dp = _dot(dohb, vh, NT)
            ds = pn * (dp - jnp.sum(doh * oh, axis=-1, keepdims=True))
            dsb = ds.astype(BF16)
            dqs.append(_dot(dsb, kh) * scale)
            dkv_ref[:, hs] += _dot(dsb, qh, TN) * scale
            vs = slice(D_MEM + MEM_HEAD_DIM * h, D_MEM + MEM_HEAD_DIM * (h + 1))
            dkv_ref[:, vs] += _dot(pnb, dohb, TN)
        dq_ref[...] = jnp.concatenate(dqs, axis=1).astype(BF16)
        dz_ref[...] = (dv * jnp.concatenate(outs, axis=1) * (sz * (1.0 + z * (1.0 - sz)))).astype(BF16)

    row = lambda i: (i, 0)
    return pl.pallas_call(
        body, name=name, grid=(T // tm,),
        in_specs=[pl.BlockSpec((tm, D_MEM), row), pl.BlockSpec((tm, D_MEM), lambda i: (i, qcol)),
                  pl.BlockSpec((tm, D_MEM), lambda i: (i, zcol)), _full((M, 2 * D_MEM))],
        out_specs=[pl.BlockSpec((tm, D_MEM), row), pl.BlockSpec((tm, D_MEM), row), _full((M, 2 * D_MEM))],
        out_shape=[jax.ShapeDtypeStruct((T, D_MEM), BF16), jax.ShapeDtypeStruct((T, D_MEM), BF16),
                   jax.ShapeDtypeStruct((M, 2 * D_MEM), F32)],
        compiler_params=_params(("arbitrary",)),
    )(d, proj, proj, kv)


def _branches_and_gates(os_ref, oa_ref, om_ref, gl_refs, bg_ref, ws_ref, wa_ref, wm_ref):
    outs = (_dot(os_ref[...], ws_ref[...]), _dot(oa_ref[...], wa_ref[...]), _dot(om_ref[...], wm_ref[...]))
    gates = tuple(_sigmoid(jnp.concatenate([gl_refs[2 * k][...], gl_refs[2 * k + 1][...]], axis=1)
                           + bg_ref[:, D_MODEL * k:D_MODEL * (k + 1)]) for k in range(3))
    return outs, gates


def _merge_specs(tm):
    row = lambda i: (i, 0)
    first = _OFF["gates"] // GATE_BLOCK
    gate = [pl.BlockSpec((tm, GATE_BLOCK), (lambda i, k=k: (i, first + k))) for k in range(N_GATES // GATE_BLOCK)]
    return ([pl.BlockSpec((tm, D_SSM), row), pl.BlockSpec((tm, D_ATTN), row), pl.BlockSpec((tm, D_MEM), row)] + gate
            + [_full((1, N_GATES)), _full((D_SSM, D_MODEL)), _full((D_ATTN, D_MODEL)), _full((D_MEM, D_MODEL)),
               _full((D_MODEL, D_MODEL))])


def _merge_fwd(x, o_ssm, o_attn, o_mem, proj, bg, ws, wa, wm, wo, tm, name):
    T = x.shape[0]

    def body(os_ref, oa_ref, om_ref, g0, g1, g2, g3, g4, g5, bg_ref, ws_ref, wa_ref, wm_ref, wo_ref, x_ref,
             xo_ref, mg_ref):
        outs, gates = _branches_and_gates(os_ref, oa_ref, om_ref, (g0, g1, g2, g3, g4, g5), bg_ref, ws_ref, wa_ref,
                                          wm_ref)
        merged = (gates[0] * outs[0] + gates[1] * outs[1] + gates[2] * outs[2]).astype(BF16)
        mg_ref[...] = merged
        xo_ref[...] = x_ref[...] + _dot(merged, wo_ref[...])

    row = lambda i: (i, 0)
    return pl.pallas_call(
        body, name=name, grid=(T // tm,),
        in_specs=_merge_specs(tm) + [pl.BlockSpec((tm, D_MODEL), row)],
        out_specs=[pl.BlockSpec((tm, D_MODEL), row), pl.BlockSpec((tm, D_MODEL), row)],
        out_shape=[jax.ShapeDtypeStruct((T, D_MODEL), F32), jax.ShapeDtypeStruct((T, D_MODEL), BF16)],
        compiler_params=_params(("parallel",)),
    )(o_ssm, o_attn, o_mem, *([proj] * (N_GATES // GATE_BLOCK)), bg, ws, wa, wm, wo, x)


def _merge_bwd(dx, o_ssm, o_attn, o_mem, proj, bg, ws, wa, wm, wo, tm, name, job=None):
    T = dx.shape[0]

    n = T // tm

    def body(os_ref, oa_ref, om_ref, g0, g1, g2, g3, g4, g5, bg_ref, ws_ref, wa_ref, wm_ref, wo_ref, dx_ref,
             dproj_ref, db_ref, dos_ref, doa_ref, dom_ref, dbg_ref, dgl_buf, dgl_sems):
        i = pl.program_id(0)
        slot = i % 2

        def to_dproj(s, row0):
            return pltpu.make_async_copy(dgl_buf.at[s], dproj_ref.at[pl.ds(row0, tm), pl.ds(_OFF["gates"], N_GATES)],
                                         dgl_sems.at[s])

        @pl.when(i == 0)
        def _():
            dbg_ref[...] = jnp.zeros_like(dbg_ref)

        @pl.when(i >= 2)
        def _():
            to_dproj(slot, 0).wait()

        outs, gates = _branches_and_gates(os_ref, oa_ref, om_ref, (g0, g1, g2, g3, g4, g5), bg_ref, ws_ref, wa_ref,
                                          wm_ref)
        dm = _dot(dx_ref[...].astype(BF16), wo_ref[...], NT)
        w_refs = (ws_ref, wa_ref, wm_ref)
        do_refs = (dos_ref, doa_ref, dom_ref)
        for k in range(3):
            cols = slice(D_MODEL * k, D_MODEL * (k + 1))
            dgl = dm * outs[k] * (gates[k] * (1.0 - gates[k]))
            dgl_buf[slot, :, cols] = dgl.astype(BF16)
            dbg_ref[:, cols] += jnp.sum(dgl, axis=0, keepdims=True)
            dbk = (dm * gates[k]).astype(BF16)
            db_ref[:, cols] = dbk
            do_refs[k][...] = _dot(dbk, w_refs[k][...], NT)
        to_dproj(slot, pl.multiple_of(i * tm, tm)).start()

        @pl.when(i == n - 1)
        def _():
            for s in range(min(2, n)):
                to_dproj(s, 0).wait()

    row = lambda i: (i, 0)
    return _pc(
        body, job, name=name, grid=(n,),
        in_specs=_merge_specs(tm) + [pl.BlockSpec((tm, D_MODEL), row)],
        out_specs=[pl.BlockSpec(memory_space=pl.ANY), pl.BlockSpec((tm, N_GATES), row), pl.BlockSpec((tm, D_SSM), row),
                   pl.BlockSpec((tm, D_ATTN), row), pl.BlockSpec((tm, D_MEM), row), _full((1, N_GATES))],
        out_shape=[jax.ShapeDtypeStruct((T, D_IN), BF16), jax.ShapeDtypeStruct((T, N_GATES), BF16),
                   jax.ShapeDtypeStruct((T, D_SSM), F32), jax.ShapeDtypeStruct((T, D_ATTN), F32),
                   jax.ShapeDtypeStruct((T, D_MEM), F32), jax.ShapeDtypeStruct((1, N_GATES), F32)],
        scratch_shapes=[pltpu.VMEM((2, tm, N_GATES), BF16), pltpu.SemaphoreType.DMA((2,))], sem=("arbitrary",),
        operands=(o_ssm, o_attn, o_mem, *([proj] * (N_GATES // GATE_BLOCK)), bg, ws, wa, wm, wo, dx))


def _loss_head(x, g, target, tm, name):
    T, D = x.shape

    def body(x_ref, g_ref, t_ref, loss_ref, dx_ref, dg_ref):
        @pl.when(pl.program_id(0) == 0)
        def _():
            loss_ref[...] = jnp.zeros_like(loss_ref)
            dg_ref[...] = jnp.zeros_like(dg_ref)

        xv = x_ref[...]
        r = lax.rsqrt(jnp.mean(xv * xv, axis=-1, keepdims=True) + EPS)
        xr = xv * r
        err = xr * g_ref[...] - t_ref[...]
        loss_ref[...] += 0.5 * jnp.sum(jnp.mean(err * err, axis=-1, keepdims=True), axis=0, keepdims=True)
        dy = err * (1.0 / D)
        dg_ref[...] += jnp.sum(dy * xr, axis=0, keepdims=True)
        wv = dy * g_ref[...]
        dx_ref[...] = r * (wv - xr * jnp.mean(wv * xr, axis=-1, keepdims=True))

    row = lambda i: (i, 0)
    return pl.pallas_call(
        body, name=name, grid=(T // tm,),
        in_specs=[pl.BlockSpec((tm, D), row), _full((1, D)), pl.BlockSpec((tm, D), row)],
        out_specs=[_full((1, 128)), pl.BlockSpec((tm, D), row), _full((1, D))],
        out_shape=[jax.ShapeDtypeStruct((1, 128), F32), jax.ShapeDtypeStruct((T, D), F32),
                   jax.ShapeDtypeStruct((1, D), F32)],
        compiler_params=_params(("arbitrary",)),
    )(x, g, target)


def _adamw(parts, w, m, v, tr, name):
    L, R, C = w.shape

    def body(p_ref, w_ref, m_ref, v_ref, g_ref, d_ref, mo_ref, vo_ref):
        g = p_ref[0].astype(F32)
        for s in range(1, N_DEV):
            g = g + p_ref[s].astype(F32)
        mn = ADAM_B1 * m_ref[...] + (1.0 - ADAM_B1) * g
        vn = ADAM_B2 * v_ref[...] + (1.0 - ADAM_B2) * (g * g)
        m_hat = mn / (1.0 - ADAM_B1 ** ADAM_STEP)
        v_hat = vn / (1.0 - ADAM_B2 ** ADAM_STEP)
        g_ref[...] = g
        d_ref[...] = -ADAM_LR * (m_hat / (jnp.sqrt(v_hat) + ADAM_EPS) + ADAM_WD * w_ref[...])
        mo_ref[...] = mn
        vo_ref[...] = vn

    one = pl.BlockSpec((None, tr, C), lambda l, i: (l, i, 0))
    return pl.pallas_call(
        body, name=name, grid=(L, R // tr),
        in_specs=[pl.BlockSpec((N_DEV, None, tr, C), lambda l, i: (0, l, i, 0)), one, one, one],
        out_specs=[one] * 4,
        out_shape=[jax.ShapeDtypeStruct((L, R, C), F32)] * 4,
        compiler_params=_params(("parallel", "parallel")),
    )(parts, w, m, v)


_SHARDED = (("w_in", (1088, 1024), 1), ("w_glu", (96, 768), 0), ("w_mem_kv", (128, 1024), 0),
            ("w_br_ssm", (768, 128), 1), ("w_br_attn", (768, 128), 1), ("w_br_mem", (512, 128), 1),
            ("w_out", (128, 1024), 0))
_W_IN = 0
_SMALL = tuple(range(1, len(_SHARDED)))


class _Job(NamedTuple):
    ins: list
    out_shape: list
    aliases: dict
    pairs: Callable
    n: int


def _peers():
    x, y, c = lax.axis_index("x"), lax.axis_index("y"), lax.axis_index("c")
    me = 4 * x + 2 * y + c
    out = []
    for k in range(1, N_DEV):
        px = 1 - x if k & 4 else x
        py = 1 - y if k & 2 else y
        pc = 1 - c if k & 1 else c
        out.append(((px, py, pc), 4 * px + 2 * py + pc))
    return me, out


def _copies(pairs, send_sems, recv_sems, local_sems, arrivals):
    me, peers = _peers()
    local = [pltpu.make_async_copy(src(me), dst(me), local_sems.at[j]) for j, (src, dst) in enumerate(pairs)]
    sends, recvs = [], []
    for k, (peer, lin) in enumerate(peers):
        for j, (src, dst) in enumerate(pairs):
            for to, out in ((dst(me), sends), (dst(lin), recvs)):
                if out is sends or arrivals:
                    out.append(pltpu.make_async_remote_copy(
                        src_ref=src(lin), dst_ref=to, send_sem=send_sems.at[j, k], recv_sem=recv_sems.at[j, k],
                        device_id=peer, device_id_type=pl.DeviceIdType.MESH))
    return local, sends, recvs


def _start_copies(pairs, *sems):
    local, sends, _ = _copies(pairs, *sems, arrivals=False)
    for cp in local + sends:
        cp.start()


def _wait_copies(pairs, *sems):
    local, sends, recvs = _copies(pairs, *sems, arrivals=True)
    for cp in recvs:
        cp.wait_recv()
    for cp in sends:
        cp.wait_send()
    for cp in local:
        cp.wait()


def _job_scratch(job):
    return [pltpu.SemaphoreType.DMA((job.n, N_DEV - 1)), pltpu.SemaphoreType.DMA((job.n, N_DEV - 1)),
            pltpu.SemaphoreType.DMA((job.n,))]


def _pc(body, job, *, name, grid, in_specs, out_specs, out_shape, scratch_shapes, sem, operands, aliases=None):
    aliases = aliases or {}
    if job is None:
        return pl.pallas_call(body, name=name, grid=grid, in_specs=in_specs, out_specs=out_specs, out_shape=out_shape,
                              scratch_shapes=scratch_shapes, input_output_aliases=aliases,
                              compiler_params=_params(sem))(*operands)
    a = len(in_specs)
    b = a + len(job.ins)
    c = b + len(out_shape)
    d = c + len(job.out_shape)
    e = d + len(scratch_shapes)

    def carried(*refs):
        pairs = job.pairs(refs[a:b], refs[c:d])
        ids = [pl.program_id(k) for k in range(len(grid))]
        first = functools.reduce(jnp.logical_and, [i == 0 for i in ids])
        last = functools.reduce(jnp.logical_and, [i == n - 1 for i, n in zip(ids, grid)])

        @pl.when(first)
        def _():
            _start_copies(pairs, *refs[e:])

        body(*refs[:a], *refs[b:c], *refs[d:e])

        @pl.when(last)
        def _():
            _wait_copies(pairs, *refs[e:])

    hbm = pl.BlockSpec(memory_space=pl.ANY)
    outs = pl.pallas_call(
        carried, name=name, grid=grid,
        in_specs=list(in_specs) + [hbm] * len(job.ins), out_specs=list(out_specs) + [hbm] * len(job.out_shape),
        out_shape=list(out_shape) + list(job.out_shape),
        input_output_aliases={**aliases, **{a + i: len(out_shape) + o for i, o in job.aliases.items()}},
        scratch_shapes=list(scratch_shapes) + _job_scratch(job),
        compiler_params=_params(("arbitrary",) * len(grid)),
    )(*operands, *job.ins)
    return outs[:len(out_shape)], outs[len(out_shape):]


def _gather_via_sibling(x, take, place, out_shape, name, landing=None):
    def body(*refs):
        x_ref, o_ref = refs[0], refs[-4]
        send_sems, recv_sems, local_sem = refs[-3:]
        x, y, c = lax.axis_index("x"), lax.axis_index("y"), lax.axis_index("c")
        me, sibling = (x, y, c), (x, y, 1 - c)
        chips = [(1 - x, y), (x, 1 - y), (1 - x, 1 - y)]
        src = take(x_ref)

        def slot(px, py, pc):
            return place(o_ref, 4 * px + 2 * py + pc)

        def copy(k, block, to, first_hand):
            return pltpu.make_async_remote_copy(
                src_ref=src if first_hand else slot(*block), dst_ref=slot(*block), send_sem=send_sems.at[k],
                recv_sem=recv_sems.at[k], device_id=to, device_id_type=pl.DeviceIdType.MESH)

        mine = pltpu.make_async_copy(src, slot(*me), local_sem)
        mine.start()
        first = [copy(0, me, sibling, True)] + [copy(1 + j, me, (*chip, c), True) for j, chip in enumerate(chips)]
        for cp in first:
            cp.start()
        passed = []
        for j, chip in enumerate(chips):
            copy(1 + j, (*chip, c), me, True).wait_recv()
            passed.append(copy(4 + j, (*chip, c), sibling, False))
            passed[-1].start()
        copy(0, sibling, me, True).wait_recv()
        for j, chip in enumerate(chips):
            copy(4 + j, (*chip, 1 - c), me, False).wait_recv()
        for cp in first + passed:
            cp.wait_send()
        mine.wait()

    hbm = pl.BlockSpec(memory_space=pl.ANY)
    ins = [x] if landing is None else [x, landing]
    return pl.pallas_call(
        body, name=name, in_specs=[hbm] * len(ins), out_specs=hbm, out_shape=out_shape,
        input_output_aliases={} if landing is None else {1: 0},
        scratch_shapes=[pltpu.SemaphoreType.DMA((N_DEV - 1,)), pltpu.SemaphoreType.DMA((N_DEV - 1,)),
                        pltpu.SemaphoreType.DMA],
    )(*ins)


def _lane_window(ref, who):
    return ref.at[:, pl.ds(pl.multiple_of(who * LANES, LANES), LANES)]


def _gather_job(shards, items):
    out_shape = []
    for i, _ in items:
        _, s, axis = _SHARDED[i]
        whole = i != _W_IN and axis == 1
        out_shape.append(jax.ShapeDtypeStruct((s[0], N_DEV * s[1]) if whole else (N_DEV,) + s, BF16))

    def pairs(in_refs, out_refs):
        out = []
        for (i, l), src, dst in zip(items, in_refs, out_refs):
            if i != _W_IN and _SHARDED[i][2] == 1:
                out.append((lambda who, src=src, l=l: src.at[l], lambda who, dst=dst: _lane_window(dst, who)))
            else:
                out.append((lambda who, src=src, l=l: src.at[l], lambda who, dst=dst: dst.at[who]))
        return out

    return _Job([shards[i] for i, _ in items], out_shape, {}, pairs, len(items))


def _landed_weights(items, landed):
    out = {}
    for (i, _), a in zip(items, landed):
        n, s, axis = _SHARDED[i]
        if i == _W_IN:
            out[n] = a.reshape(D_IN, D_MODEL)
        elif axis == 0:
            out[n] = a.reshape(N_DEV * s[0], s[1])
        else:
            out[n] = a
    return out


def _scatter_job(grads, items, layer, parts=None):
    ng = len(grads)
    out_shape = [jax.ShapeDtypeStruct((N_DEV, DEPTH) + _SHARDED[i][1], BF16) for i in items]

    def pairs(in_refs, out_refs):
        out = []
        for i, src, dst in zip(items, in_refs[:ng], out_refs):
            _, s, axis = _SHARDED[i]
            if i == _W_IN:
                take = lambda who, src=src: src.at[who]
            elif axis == 0:
                take = lambda who, src=src, s=s: src.at[pl.ds(pl.multiple_of(who * s[0], 16), s[0])]
            else:
                take = lambda who, src=src: _lane_window(src, who)
            out.append((take, lambda who, dst=dst: dst.at[who, layer]))
        return out

    aliases = {} if parts is None else {ng + j: j for j in range(len(items))}
    return _Job(list(grads) + ([] if parts is None else list(parts)), out_shape, aliases, pairs, len(items))


def _rows_job(src, row0, landing=None):
    n = src.shape[0]
    pairs = lambda in_refs, out_refs: [(lambda who: in_refs[0], lambda who: out_refs[0].at[who, pl.ds(row0, n)])]
    return _Job([src] + ([] if landing is None else [landing]), [jax.ShapeDtypeStruct((N_DEV, _REP_ROWS, LANES), F32)],
                {} if landing is None else {1: 0}, pairs, 1)


_REPLICATED = (("norm_g", (2, 1024)), ("mem_norm_g", (2, 1024)), ("b_gate", (2, 3072)),
               ("ssm_lambda_re", (2, 48, 64)), ("ssm_lambda_im", (2, 48, 64)), ("ssm_log_dt", (2, 48)),
               ("ssm_b_re", (2, 48, 64, 16)), ("ssm_b_im", (2, 48, 64, 16)), ("ssm_c_re", (2, 48, 16, 64)),
               ("ssm_c_im", (2, 48, 16, 64)), ("ssm_d", (2, 768)), ("b_glu", (2, 768)), ("rel_bias", (32, 12)),
               ("final_norm_g", (1024,)))
_PER_LAYER = tuple((n, s[1:]) for n, s in _REPLICATED if s[0] == DEPTH and len(s) > 1)
_SHARED = tuple((n, s) for n, s in _REPLICATED if (n, s[1:]) not in _PER_LAYER)
_REP_HALF_ROWS = 1664
_REP_ROWS = 2 * _REP_HALF_ROWS
assert sum(int(np.prod(s)) for _, s in _PER_LAYER + _SHARED) <= _REP_HALF_ROWS * LANES


def _pack_half(tree, layer, shared):
    flat = [tree[n][layer].reshape(-1) for n, _ in _PER_LAYER]
    if shared:
        flat += [tree[n].reshape(-1) for n, _ in _SHARED]
    flat = jnp.concatenate(flat)
    return jnp.pad(flat, (0, _REP_HALF_ROWS * LANES - flat.shape[0])).reshape(_REP_HALF_ROWS, LANES)


def _pack_replicated(tree):
    return jnp.concatenate([_pack_half(tree, 1, False), _pack_half(tree, 0, True)])[None]


def _unpack_replicated(packed):
    halves = packed.reshape(2, -1)
    out, r = {}, 0
    for n, s in _PER_LAYER:
        size = int(np.prod(s))
        out[n] = jnp.stack([halves[1, r:r + size].reshape(s), halves[0, r:r + size].reshape(s)])
        r += size
    for n, s in _SHARED:
        size = int(np.prod(s))
        out[n] = halves[1, r:r + size].reshape(s)
        r += size
    return out


def _discretize(lam_re, lam_im, log_dt, b_re, b_im):
    dt = jnp.exp(log_dt)[:, None]
    mag = jnp.exp(lam_re * dt)
    abar_re, abar_im = mag * jnp.cos(lam_im * dt), mag * jnp.sin(lam_im * dt)
    den = lam_re * lam_re + lam_im * lam_im
    nr, ni = abar_re - 1.0, abar_im
    f_re = (nr * lam_re + ni * lam_im) / den
    f_im = (ni * lam_re - nr * lam_im) / den
    bbar_re = f_re[..., None] * b_re - f_im[..., None] * b_im
    bbar_im = f_re[..., None] * b_im + f_im[..., None] * b_re
    return abar_re, abar_im, bbar_re, bbar_im


def _block_diag(a):
    _, R, C = a.shape
    a = a.reshape(SSM_BLOCKS, 8, R, C)
    eye = jnp.eye(8, dtype=a.dtype)
    return (a[:, :, :, None, :] * eye[None, :, None, :, None]).reshape(SSM_BLOCKS, 8 * R, 8 * C)


def _diag_blocks(a, R, C):
    a = a.reshape(SSM_BLOCKS, 8, R, 8, C)
    eye = jnp.eye(8, dtype=a.dtype)
    return jnp.sum(a * eye[None, :, None, :, None], axis=3).reshape(SSM_GROUPS, R, C)


def _carried(result, job):
    return (result, None) if job is None else result


def _layer_fwd(x, mem, W, P, bias, layer, jobs):
    tag = f"l{layer}"
    abar_re, abar_im, bbar_re, bbar_im = _discretize(P["ssm_lambda_re"][layer], P["ssm_lambda_im"][layer],
                                                     P["ssm_log_dt"][layer], P["ssm_b_re"][layer], P["ssm_b_im"][layer])
    c_re, c_im = P["ssm_c_re"][layer], P["ssm_c_im"][layer]
    ssm = dict(
        are=abar_re.reshape(1, N_STATE), aim=abar_im.reshape(1, N_STATE),
        bre=_block_diag(bbar_re.transpose(0, 2, 1)).astype(BF16), bim=_block_diag(bbar_im.transpose(0, 2, 1)).astype(BF16),
        cre=_block_diag(c_re.transpose(0, 2, 1)).astype(BF16), cimn=_block_diag(-c_im.transpose(0, 2, 1)).astype(BF16),
        d=P["ssm_d"][layer].reshape(1, D_SSM))
    bglu = P["b_glu"][layer].reshape(1, D_SSM)
    bgate = P["b_gate"][layer].reshape(1, N_GATES)
    g = P["norm_g"][layer].reshape(1, D_MODEL)
    gm = P["mem_norm_g"][layer].reshape(1, D_MODEL)
    delivered = {}

    def carry(stage):
        return jobs[stage][0] if stage in jobs else None

    def deliver(stage, landed):
        if landed is not None:
            delivered[stage] = _landed_weights(jobs[stage][1], landed)

    T = x.shape[0]
    (proj, h), landed = _carried(_norm_proj(x, g, W["w_in"], min(T, 1024), 2176, f"{tag}_proj", job=carry("proj"),
                                            w_turned=True), carry("proj"))
    deliver("proj", landed)
    W = {**W, **delivered.get("proj", {})}
    (xr, xi, y, o_ssm), landed = _carried(
        _ssm_fwd(proj, ssm["bre"], ssm["bim"], ssm["cre"], ssm["cimn"], ssm["are"], ssm["aim"], ssm["d"], W["w_glu"],
                 bglu, 512, f"{tag}_ssm", job=carry("ssm")), carry("ssm"))
    deliver("ssm", landed)
    os, lses = [], []
    for grp in range(3):
        stage = f"attn{grp}"
        (o_g, lse_g), landed = _carried(_attn_fwd(proj, bias[grp], grp, f"{tag}_{stage}", job=carry(stage)), carry(stage))
        deliver(stage, landed)
        os.append(o_g)
        lses.append(lse_g)
    o_attn = _attn_mix(os, lses, proj, min(T, ROW_TILE), f"{tag}_attn_mix")
    kvb, hm = _norm_proj(mem, gm, W["w_mem_kv"], mem.shape[0], 1024, f"{tag}_mem_kv", out_dtype=BF16)
    o_mem = _mem_fwd(proj, kvb, min(T, ROW_TILE), f"{tag}_mem")
    x_out, merged = _merge_fwd(x, o_ssm, o_attn, o_mem, proj, bgate, W["w_br_ssm"], W["w_br_attn"], W["w_br_mem"],
                               W["w_out"], 512, f"{tag}_merge")
    res = dict(x=x, mem=mem, proj=proj, h=h, xr=xr, xi=xi, y=y, o_ssm=o_ssm, os=os, lses=lses,
               o_attn=o_attn, kvb=kvb, hm=hm, o_mem=o_mem, merged=merged, ssm=ssm, bglu=bglu,
               bgate=bgate, g=g, gm=gm, W=W)
    return x_out, res, delivered


def _layer_bwd(dx, res, P, bias, layer, jobs):
    tag = f"l{layer}b"
    proj, ssm, W = res["proj"], res["ssm"], res["W"]
    T = dx.shape[0]
    landed = {}

    def run(stage, fn, job):
        out, landed[stage] = _carried(fn(job), job)
        if job is None:
            del landed[stage]
        return out

    dproj, dbr, do_ssm, do_attn, do_mem, dbg = run(
        "merge", lambda job: _merge_bwd(dx, res["o_ssm"], res["o_attn"], res["o_mem"], proj, res["bgate"], W["w_br_ssm"],
                                        W["w_br_attn"], W["w_br_mem"], W["w_out"], 512, f"{tag}_merge", job=job),
        jobs.get("merge"))
    gw = {}
    tk = min(T, 1024)
    gw["w_out"] = _mm_tn(res["merged"], dx, 1024, 1024, tk, f"{tag}_dw_out")
    gw["w_br_ssm"] = _mm_tn(res["o_ssm"], dbr, 768, 1024, tk, f"{tag}_dw_br_ssm", b_col=0, n=1024)
    gw["w_br_attn"] = _mm_tn(res["o_attn"], dbr, 768, 1024, tk, f"{tag}_dw_br_attn", b_col=1024, n=1024)
    gw["w_br_mem"] = _mm_tn(res["o_mem"], dbr, 512, 1024, tk, f"{tag}_dw_br_mem", b_col=2048, n=1024)

    rows = min(T, ROW_TILE)
    dqm, dzm, dkv = _mem_bwd(do_mem, proj, res["kvb"], rows, f"{tag}_mem")
    M = dkv.shape[0]
    gw["w_mem_kv"] = _mm_tn(res["hm"], dkv, 1024, 1024, M, f"{tag}_dw_mem_kv")
    _, dgm = _proj_bwd(dkv.astype(BF16), W["w_mem_kv"], res["mem"], res["gm"], jnp.zeros_like(res["mem"]), M, 1024,
                       f"{tag}_mem_norm")

    do_g, corr, dproj = _attn_mix_bwd(do_attn, res["os"], res["lses"], proj, dproj, rows, f"{tag}_attn_mix")
    dbs = []
    for grp in range(3):
        dproj, db_g = _attn_bwd(proj, do_g, corr, res["lses"][grp], bias[grp], dproj, grp, f"{tag}_attn{grp}")
        dbs.append(db_g)
    dbias = jnp.stack(dbs)

    dy, dproj, gelu_b, dt_b, dbglu = _glu_bwd(do_ssm, res["y"], proj, W["w_glu"], res["bglu"], dproj, rows, f"{tag}_glu")
    gw["w_glu"] = _mm_tn(gelu_b, dt_b, 768, 768, tk, f"{tag}_dw_glu")
    dproj, dbre, dbim, dcre, dcim, dare, daim, dd = run(
        "ssm", lambda job: _ssm_bwd(dy, proj, res["xr"], res["xi"], ssm["bre"], ssm["bim"], ssm["cre"], ssm["cimn"],
                                    ssm["are"], ssm["aim"], ssm["d"], dproj, 256, f"{tag}_ssm", job=job),
        jobs.get("ssm"))
    _, disc_vjp = jax.vjp(_discretize, P["ssm_lambda_re"][layer], P["ssm_lambda_im"][layer], P["ssm_log_dt"][layer],
                          P["ssm_b_re"][layer], P["ssm_b_im"][layer])
    d_lre, d_lim, d_ldt, d_bre, d_bim = disc_vjp((dare.reshape(SSM_GROUPS, SSM_STATE), daim.reshape(SSM_GROUPS, SSM_STATE),
                                                  _diag_blocks(dbre, SSM_STATE, SSM_GROUP),
                                                  _diag_blocks(dbim, SSM_STATE, SSM_GROUP)))

    small = [gw[_SHARDED[i][0]] for i in _SMALL]
    for seg, piece in (("q_mem", dqm), ("z_mem", dzm)):
        dproj = lax.dynamic_update_slice(dproj, piece, (0, _OFF[seg]))
    dw_in = run("dw_in", lambda job: _mm_tn(dproj, res["h"], 2176, 1024, min(T, 1024), f"{tag}_dw_in", job=job),
                jobs["dw_in"](small) if "dw_in" in jobs else None)
    dw_in = dw_in.reshape((N_DEV,) + _SHARDED[_W_IN][1])
    dx_in, dg = run("proj", lambda job: _proj_bwd(dproj, W["w_in"], res["x"], res["g"], dx, min(T, 1024), 2176,
                                                  f"{tag}_proj", job=job, w_turned=True),
                    jobs["proj"](small, dw_in, landed) if "proj" in jobs else None)

    gp = dict(norm_g=dg[0], mem_norm_g=dgm[0], b_gate=dbg[0], ssm_lambda_re=d_lre, ssm_lambda_im=d_lim,
              ssm_log_dt=d_ldt, ssm_b_re=d_bre, ssm_b_im=d_bim,
              ssm_c_re=_diag_blocks(dcre, SSM_GROUP, SSM_STATE), ssm_c_im=_diag_blocks(dcim, SSM_GROUP, SSM_STATE),
              ssm_d=dd[0], b_glu=dbglu[0])
    return dx_in, dw_in, gp, dbias, landed


def _train_step(x, mem, target, shards, P):
    rest0 = [(i, 0) for i in _SMALL]
    thirds1 = [[(i, 1) for i in _SMALL[k::3]] for k in range(3)]
    first = [(_W_IN, 0)]
    w_in0 = _gather_via_sibling(shards[_W_IN], lambda ref: ref.at[0], lambda ref, s: ref.at[s],
                                jax.ShapeDtypeStruct((N_DEV,) + _SHARDED[_W_IN][1], BF16), "gather_w_in0")
    W0 = _landed_weights(first, [w_in0])
    buckets = _bucket_tables()
    bias = _bias_tables(P["rel_bias"], buckets, "bias_tables")
    jobs0 = {"proj": (_gather_job(shards, rest0), rest0), "ssm": (_gather_job(shards, [(_W_IN, 1)]), [(_W_IN, 1)]),
             **{f"attn{k}": (_gather_job(shards, items), items) for k, items in enumerate(thirds1)}}
    x, res0, delivered = _layer_fwd(x, mem, W0, P, bias, 0, jobs0)
    W1 = {**delivered["ssm"], **delivered["attn0"], **delivered["attn1"], **delivered["attn2"]}
    x, res1, _ = _layer_fwd(x, mem, W1, P, bias, 1, {})
    loss, dx, dgf = _loss_head(x, P["final_norm_g"].reshape(1, D_MODEL), target, min(x.shape[0], ROW_TILE),
                                "loss_head")

    dx, _, gp1, dbias1, landed1 = _layer_bwd(
        dx, res1, P, bias, 1,
        {"dw_in": lambda small: _scatter_job(small, _SMALL, 1),
         "proj": lambda small, dw_in, landed: _scatter_job([dw_in], [_W_IN], 1)})
    rep1 = _pack_half({n: a[None] for n, a in gp1.items()}, 0, False)
    dx, _, gp0, dbias0, landed0 = _layer_bwd(
        dx, res0, P, bias, 0,
        {"merge": _rows_job(rep1, 0),
         "dw_in": lambda small: _scatter_job(small, _SMALL, 0, parts=landed1["dw_in"]),
         "proj": lambda small, dw_in, landed: _scatter_job([dw_in], [_W_IN], 0, parts=landed1["proj"])})
    d_rel = _bias_grad(dbias0, dbias1, buckets, "bias_grad")
    gp0 = {n: a[None] for n, a in gp0.items()}
    gp0["rel_bias"] = jnp.sum(d_rel, axis=-1).transpose(2, 0, 1).reshape(NUM_BUCKETS, 12)
    gp0["final_norm_g"] = dgf[0]
    rep0 = _pack_half(gp0, 0, True)
    rparts = _gather_via_sibling(rep0, lambda ref: ref, lambda ref, s: ref.at[s, pl.ds(_REP_HALF_ROWS, _REP_HALF_ROWS)],
                                 jax.ShapeDtypeStruct((N_DEV, _REP_ROWS, LANES), F32), "gather_small_grads0",
                                 landing=landed0["merge"][0])
    return loss[0, 0], dx, list(landed0["proj"]) + list(landed0["dw_in"]), rparts


_WEIGHTS = ["norm_g", "mem_norm_g", "w_in", "b_gate", "ssm_lambda_re", "ssm_lambda_im", "ssm_log_dt", "ssm_b_re",
            "ssm_b_im", "ssm_c_re", "ssm_c_im", "ssm_d", "w_glu", "b_glu", "w_mem_kv", "w_br_ssm", "w_br_attn",
            "w_br_mem", "w_out", "rel_bias", "final_norm_g"]
_ADAM_ROWS = {"w_in": 136,"w_glu": 96, "w_mem_kv": 128, "w_br_ssm": 768, "w_br_attn": 768, "w_br_mem": 512,
              "w_out": 128}


def kernel(x, mem, norm_g, mem_norm_g, w_in, b_gate, ssm_lambda_re, ssm_lambda_im, ssm_log_dt, ssm_b_re, ssm_b_im, ssm_c_re, ssm_c_im, ssm_d, w_glu, b_glu, w_mem_kv, w_br_ssm, w_br_attn, w_br_mem, w_out, rel_bias, final_norm_g, loss_target, m_norm_g, m_mem_norm_g, m_w_in, m_b_gate, m_ssm_lambda_re, m_ssm_lambda_im, m_ssm_log_dt, m_ssm_b_re, m_ssm_b_im, m_ssm_c_re, m_ssm_c_im, m_ssm_d, m_w_glu, m_b_glu, m_w_mem_kv, m_w_br_ssm, m_w_br_attn, m_w_br_mem, m_w_out, m_rel_bias, m_final_norm_g, v_norm_g, v_mem_norm_g, v_w_in, v_b_gate, v_ssm_lambda_re, v_ssm_lambda_im, v_ssm_log_dt, v_ssm_b_re, v_ssm_b_im, v_ssm_c_re, v_ssm_c_im, v_ssm_d, v_w_glu, v_b_glu, v_w_mem_kv, v_w_br_ssm, v_w_br_attn, v_w_br_mem, v_w_out, v_rel_bias, v_final_norm_g):
    given = dict(locals())
    w = {n: given[n] for n in _WEIGHTS}
    m = {n: given["m_" + n] for n in _WEIGHTS}
    v = {n: given["v_" + n] for n in _WEIGHTS}

    turned = lambda n, a: a.swapaxes(1, 2) if n == "w_in" else a
    shards = [turned(n, w[n]).astype(BF16) for n, _, _ in _SHARDED]
    loss, dx, parts, rparts = _train_step(x[0], mem[0], loss_target[0], shards, w)
    loss = lax.psum(loss, ("x", "y", "c"))

    new = {}
    for (n, _, _), p in zip(_SHARDED, parts):
        new[n] = [turned(n, a) for a in _adamw(p, turned(n, w[n]), turned(n, m[n]), turned(n, v[n]), _ADAM_ROWS[n],
                                               f"adamw_{n}")]
    rp = [_unpack_replicated(a) for a in _adamw(rparts[:, None], _pack_replicated(w), _pack_replicated(m),
                                                _pack_replicated(v), _REP_ROWS // 4, "adamw_replicated")]
    for n, _ in _REPLICATED:
        new[n] = [rp[kind][n] for kind in range(4)]
    outs = [loss, dx[None]]
    for kind in range(4):
        outs.extend(new[n][kind] for n in _WEIGHTS)
    return tuple(outs)
```

```python
import functools
import math
from typing import Callable, NamedTuple

import jax
import jax.numpy as jnp
import numpy as np
from jax import lax
from jax.experimental import pallas as pl
from jax.experimental.pallas import tpu as pltpu

F32 = jnp.float32
BF16 = jnp.bfloat16

D_MODEL = 1024
DEPTH = 2
EPS = 1e-6
D_SSM = 768
SSM_GROUP = 16
SSM_GROUPS = 48
SSM_STATE = 64
N_STATE = SSM_GROUPS * SSM_STATE
SSM_BLOCKS = 6
D_ATTN = 768
ATTN_HEAD_DIM = 64
ATTN_GROUP_WIDTH = 256
ATTN_DILATIONS = (1, 4, 16)
ATTN_SPAN = 128
ATTN_BLOCK = 128
NUM_BUCKETS = 32
REL_MAX_DISTANCE = 2048
NEG_INF = -1e30
MEM_HEADS = 4
MEM_HEAD_DIM = 128
D_MEM = 512
N_GATES = 3 * D_MODEL
D_IN = 8704
N_DEV = 8
LANES = 128
ADAM_LR = 0.001
ADAM_B1 = 0.9
ADAM_B2 = 0.999
ADAM_EPS = 1e-08
ADAM_WD = 0.01
ADAM_STEP = 10

_OFF = {"u": 0, "z_ssm": 768, "q": 1536, "k": 2304, "v": 3072, "z_attn": 3840, "q_mem": 4608, "z_mem": 5120,
        "gates": 5632}
GATE_BLOCK = 512
ROW_TILE = 1024

NN = (((1,), (0,)), ((), ()))
NT = (((1,), (1,)), ((), ()))
TN = (((0,), (0,)), ((), ()))

VMEM_LIMIT = 56 * 1024 * 1024


def _dot(a, b, dims=NN):
    return lax.dot_general(a, b, dims, preferred_element_type=F32)


def _sigmoid(x):
    return 1.0 / (1.0 + jnp.exp(-x))


def _gelu_parts(x):
    k = math.sqrt(2.0 / math.pi)
    t = jnp.tanh(k * (x + 0.044715 * (x * x * x)))
    cdf = 0.5 * (1.0 + t)
    dcdf = 0.5 * (1.0 - t * t) * k * (1.0 + 3.0 * 0.044715 * (x * x))
    return x * cdf, cdf + x * dcdf


def _params(sem, vmem=VMEM_LIMIT):
    return pltpu.CompilerParams(dimension_semantics=sem, vmem_limit_bytes=vmem)


def _full(shape):
    return pl.BlockSpec(shape, lambda *_: (0,) * len(shape))


def _norm_proj(x, g, w, tm, tn, name, out_dtype=F32, job=None, w_turned=False):
    T, D = x.shape
    N = w.shape[0] if w_turned else w.shape[1]
    w_spec = pl.BlockSpec((tn, D), lambda i, j: (j, 0)) if w_turned else pl.BlockSpec((D, tn), lambda i, j: (0, j))
    dims = NT if w_turned else NN

    def body(x_ref, g_ref, w_ref, o_ref, h_ref, hs):
        @pl.when(pl.program_id(1) == 0)
        def _():
            xv = x_ref[...]
            r = lax.rsqrt(jnp.mean(xv * xv, axis=-1, keepdims=True) + EPS)
            hv = (xv * r * g_ref[...]).astype(BF16)
            hs[...] = hv
            h_ref[...] = hv

        o_ref[...] = _dot(hs[...], w_ref[...], dims).astype(out_dtype)

    return _pc(
        body, job, name=name, grid=(T // tm, N // tn),
        in_specs=[pl.BlockSpec((tm, D), lambda i, j: (i, 0)), _full((1, D)), w_spec],
        out_specs=[pl.BlockSpec((tm, tn), lambda i, j: (i, j)), pl.BlockSpec((tm, D), lambda i, j: (i, 0))],
        out_shape=[jax.ShapeDtypeStruct((T, N), out_dtype), jax.ShapeDtypeStruct((T, D), BF16)],
        scratch_shapes=[pltpu.VMEM((tm, D), BF16)], sem=("parallel", "arbitrary"), operands=(x, g, w))


def _mm_tn(a, b, tm, tn, tk, name, b_col=0, n=None, job=None):
    K, M = a.shape
    N = b.shape[1] if n is None else n
    nk = K // tk
    j0 = b_col // tn

    def body(a_ref, b_ref, o_ref, acc):
        k = pl.program_id(2)

        @pl.when(k == 0)
        def _():
            acc[...] = jnp.zeros_like(acc)

        acc[...] += _dot(a_ref[...].astype(BF16), b_ref[...].astype(BF16), TN)

        @pl.when(k == nk - 1)
        def _():
            o_ref[...] = acc[...].astype(BF16)

    out = _pc(
        body, job, name=name, grid=(M // tm, N // tn, nk),
        in_specs=[pl.BlockSpec((tk, tm), lambda i, j, k: (k, i)), pl.BlockSpec((tk, tn), lambda i, j, k: (k, j0 + j))],
        out_specs=[pl.BlockSpec((tm, tn), lambda i, j, k: (i, j))],
        out_shape=[jax.ShapeDtypeStruct((M, N), BF16)],
        scratch_shapes=[pltpu.VMEM((tm, tn), F32)], sem=("parallel", "parallel", "arbitrary"), operands=(a, b))
    return out[0] if job is None else (out[0][0], out[1])


def _proj_bwd(dp, w, x, g, dres, tm, tk, name, job=None, w_turned=False):
    T, N = dp.shape
    D = x.shape[1]
    nk = N // tk
    w_spec = pl.BlockSpec((tk, D), lambda i, k: (k, 0)) if w_turned else pl.BlockSpec((D, tk), lambda i, k: (0, k))
    dims = NN if w_turned else NT

    def body(dp_ref, w_ref, x_ref, g_ref, dres_ref, dx_ref, dg_ref, acc):
        i, k = pl.program_id(0), pl.program_id(1)

        @pl.when(k == 0)
        def _():
            acc[...] = jnp.zeros_like(acc)

        @pl.when((i == 0) & (k == 0))
        def _():
            dg_ref[...] = jnp.zeros_like(dg_ref)

        acc[...] += _dot(dp_ref[...], w_ref[...], dims)

        @pl.when(k == nk - 1)
        def _():
            xv = x_ref[...]
            dh = acc[...]
            r = lax.rsqrt(jnp.mean(xv * xv, axis=-1, keepdims=True) + EPS)
            xr = xv * r
            dg_ref[...] += jnp.sum(dh * xr, axis=0, keepdims=True)
            wv = dh * g_ref[...]
            dx_ref[...] = dres_ref[...] + r * (wv - xr * jnp.mean(wv * xr, axis=-1, keepdims=True))

    return _pc(
        body, job, name=name, grid=(T // tm, nk),
        in_specs=[pl.BlockSpec((tm, tk), lambda i, k: (i, k)), w_spec,
                  pl.BlockSpec((tm, D), lambda i, k: (i, 0)), _full((1, D)),
                  pl.BlockSpec((tm, D), lambda i, k: (i, 0))],
        out_specs=[pl.BlockSpec((tm, D), lambda i, k: (i, 0)), _full((1, D))],
        out_shape=[jax.ShapeDtypeStruct((T, D), F32), jax.ShapeDtypeStruct((1, D), F32)],
        scratch_shapes=[pltpu.VMEM((tm, D), F32)], sem=("arbitrary", "arbitrary"), operands=(dp, w, x, g, dres))


def _ssm_fwd(proj, bre, bim, cre, cimn, are, aim, d, wglu, bglu, tc, name, job=None):
    T = proj.shape[0]
    ucol, zcol = _OFF["u"] // D_SSM, _OFF["z_ssm"] // D_SSM

    def body(u_ref, z_ref, bre_ref, bim_ref, cre_ref, cim_ref, are_ref, aim_ref, d_ref, wg_ref, bg_ref,
             xr_ref, xi_ref, y_ref, o_ref, car_r, car_i):
        @pl.when(pl.program_id(0) == 0)
        def _():
            car_r[...] = jnp.zeros_like(car_r)
            car_i[...] = jnp.zeros_like(car_i)

        u = u_ref[...]
        ub = u.astype(BF16)
        for k in range(SSM_BLOCKS):
            uk = ub[:, 128 * k:128 * (k + 1)]
            xr_ref[:, 512 * k:512 * (k + 1)] = _dot(uk, bre_ref[k])
            xi_ref[:, 512 * k:512 * (k + 1)] = _dot(uk, bim_ref[k])
        ar, ai = are_ref[...], aim_ref[...]

        def step(t, c):
            pr, pi = c
            nr = ar * pr - ai * pi + xr_ref[pl.ds(t, 1), :]
            ni = ar * pi + ai * pr + xi_ref[pl.ds(t, 1), :]
            xr_ref[pl.ds(t, 1), :] = nr
            xi_ref[pl.ds(t, 1), :] = ni
            return nr, ni

        pr, pi = lax.fori_loop(0, tc, step, (car_r[...], car_i[...]))
        car_r[...] = pr
        car_i[...] = pi

        ys = []
        for k in range(SSM_BLOCKS):
            xrk = xr_ref[:, 512 * k:512 * (k + 1)].astype(BF16)
            xik = xi_ref[:, 512 * k:512 * (k + 1)].astype(BF16)
            ys.append(_dot(xrk, cre_ref[k]) + _dot(xik, cim_ref[k]))
        y = jnp.concatenate(ys, axis=1) + d_ref[...] * u
        y_ref[...] = y
        gl, _ = _gelu_parts(y)
        t = _dot(gl.astype(BF16), wg_ref[...]) + bg_ref[...]
        z = z_ref[...]
        o_ref[...] = (gl * _sigmoid(t) * (z * _sigmoid(z))).astype(BF16)

    return _pc(
        body, job, name=name, grid=(T // tc,),
        in_specs=[pl.BlockSpec((tc, D_SSM), lambda i: (i, ucol)), pl.BlockSpec((tc, D_SSM), lambda i: (i, zcol)),
                  _full((SSM_BLOCKS, 128, 512)), _full((SSM_BLOCKS, 128, 512)),
                  _full((SSM_BLOCKS, 512, 128)), _full((SSM_BLOCKS, 512, 128)),
                  _full((1, N_STATE)), _full((1, N_STATE)), _full((1, D_SSM)),
                  _full((D_SSM, D_SSM)), _full((1, D_SSM))],
        out_specs=[pl.BlockSpec((tc, N_STATE), lambda i: (i, 0)), pl.BlockSpec((tc, N_STATE), lambda i: (i, 0)),
                   pl.BlockSpec((tc, D_SSM), lambda i: (i, 0)), pl.BlockSpec((tc, D_SSM), lambda i: (i, 0))],
        out_shape=[jax.ShapeDtypeStruct((T, N_STATE), F32), jax.ShapeDtypeStruct((T, N_STATE), F32),
                   jax.ShapeDtypeStruct((T, D_SSM), F32), jax.ShapeDtypeStruct((T, D_SSM), BF16)],
        scratch_shapes=[pltpu.VMEM((1, N_STATE), F32), pltpu.VMEM((1, N_STATE), F32)], sem=("arbitrary",),
        operands=(proj, proj, bre, bim, cre, cimn, are, aim, d, wglu, bglu))


def _glu_bwd(do, y, proj, wglu, bglu, dproj, tm, name):
    T = y.shape[0]
    zcol = _OFF["z_ssm"] // D_SSM

    def body(do_ref, y_ref, z_ref, wg_ref, bg_ref, _, dy_ref, dz_ref, g_ref, dt_ref, db_ref):
        @pl.when(pl.program_id(0) == 0)
        def _():
            db_ref[...] = jnp.zeros_like(db_ref)

        dov = do_ref[...]
        gl, dgl = _gelu_parts(y_ref[...])
        glb = gl.astype(BF16)
        sg = _sigmoid(_dot(glb, wg_ref[...]) + bg_ref[...])
        z = z_ref[...]
        sz = _sigmoid(z)
        dz_ref[...] = (dov * (gl * sg) * (sz * (1.0 + z * (1.0 - sz)))).astype(BF16)
        dy2 = dov * (z * sz)
        dt = dy2 * gl * (sg * (1.0 - sg))
        dtb = dt.astype(BF16)
        dg = dy2 * sg + _dot(dtb, wg_ref[...], NT)
        dy_ref[...] = dg * dgl
        g_ref[...] = glb
        dt_ref[...] = dtb
        db_ref[...] += jnp.sum(dt, axis=0, keepdims=True)

    row = lambda i: (i, 0)
    return pl.pallas_call(
        body, name=name, grid=(T // tm,),
        in_specs=[pl.BlockSpec((tm, D_SSM), row), pl.BlockSpec((tm, D_SSM), row),
                  pl.BlockSpec((tm, D_SSM), lambda i: (i, zcol)), _full((D_SSM, D_SSM)), _full((1, D_SSM)),
                  pl.BlockSpec(memory_space=pl.ANY)],
        out_specs=[pl.BlockSpec((tm, D_SSM), row), pl.BlockSpec((tm, D_SSM), lambda i: (i, zcol)),
                   pl.BlockSpec((tm, D_SSM), row), pl.BlockSpec((tm, D_SSM), row), _full((1, D_SSM))],
        out_shape=[jax.ShapeDtypeStruct((T, D_SSM), F32), jax.ShapeDtypeStruct(dproj.shape, BF16),
                   jax.ShapeDtypeStruct((T, D_SSM), BF16), jax.ShapeDtypeStruct((T, D_SSM), BF16),
                   jax.ShapeDtypeStruct((1, D_SSM), F32)],
        input_output_aliases={5: 1},
        compiler_params=_params(("arbitrary",)),
    )(do, y, proj, wglu, bglu, dproj)


def _ssm_bwd(dy, proj, xr, xi, bre, bim, cre, cimn, are, aim, d, dproj, tc, name, job=None):
    T = dy.shape[0]
    nc = T // tc
    ucol = _OFF["u"] // D_SSM
    rb = tc // 8

    def body(dy_ref, u_ref, xr_ref, xi_ref, xpr_ref, xpi_ref, bre_ref, bim_ref, cre_ref, cim_ref,
             are_ref, aim_ref, d_ref, _,
             du_ref, dbre_ref, dbim_ref, dcre_ref, dcim_ref, dare_ref, daim_ref, dd_ref, gr, gi, car_r, car_i):
        i = pl.program_id(0)

        @pl.when(i == 0)
        def _():
            for ref in (car_r, car_i, dbre_ref, dbim_ref, dcre_ref, dcim_ref, dare_ref, daim_ref, dd_ref):
                ref[...] = jnp.zeros_like(ref)

        dyv = dy_ref[...]
        dyb = dyv.astype(BF16)
        u = u_ref[...]
        ub = u.astype(BF16)
        for k in range(SSM_BLOCKS):
            dk = dyb[:, 128 * k:128 * (k + 1)]
            gr[:, 512 * k:512 * (k + 1)] = _dot(dk, cre_ref[k], NT)
            gi[:, 512 * k:512 * (k + 1)] = _dot(dk, cim_ref[k], NT)
        ar, ai = are_ref[...], aim_ref[...]

        def step(s, c):
            pr, pi = c
            t = tc - 1 - s
            nr = gr[pl.ds(t, 1), :] + ar * pr + ai * pi
            ni = gi[pl.ds(t, 1), :] + ar * pi - ai * pr
            gr[pl.ds(t, 1), :] = nr
            gi[pl.ds(t, 1), :] = ni
            return nr, ni

        pr, pi = lax.fori_loop(0, tc, step, (car_r[...], car_i[...]))
        car_r[...] = pr
        car_i[...] = pi

        keep = jnp.where(i == nc - 1, 0.0, 1.0)
        row0 = lax.broadcasted_iota(jnp.int32, (tc, 1), 0) == 0
        dd_ref[...] += jnp.sum(dyv * u, axis=0, keepdims=True)
        for k in range(SSM_BLOCKS):
            sl = slice(512 * k, 512 * (k + 1))
            ch = slice(128 * k, 128 * (k + 1))
            xrk, xik, grk, gik = xr_ref[:, sl], xi_ref[:, sl], gr[:, sl], gi[:, sl]
            xsr = jnp.where(row0, xpr_ref[7:8, sl] * keep, pltpu.roll(xrk, 1, axis=0))
            xsi = jnp.where(row0, xpi_ref[7:8, sl] * keep, pltpu.roll(xik, 1, axis=0))
            dare_ref[:, sl] += jnp.sum(grk * xsr + gik * xsi, axis=0, keepdims=True)
            daim_ref[:, sl] += jnp.sum(gik * xsr - grk * xsi, axis=0, keepdims=True)
            grb, gib = grk.astype(BF16), gik.astype(BF16)
            du_ref[:, ch] = (_dot(grb, bre_ref[k], NT) + _dot(gib, bim_ref[k], NT)
                             + d_ref[:, ch] * dyv[:, ch]).astype(BF16)
            dbre_ref[k] += _dot(grb, ub[:, ch], TN)
            dbim_ref[k] += _dot(gib, ub[:, ch], TN)
            dcre_ref[k] += _dot(dyb[:, ch], xrk.astype(BF16), TN)
            dcim_ref[k] -= _dot(dyb[:, ch], xik.astype(BF16), TN)

    rev = lambda i: (nc - 1 - i, 0)
    prev = lambda i: (jnp.maximum((nc - 1 - i) * rb - 1, 0), 0)
    return _pc(
        body, job, name=name, grid=(nc,),
        in_specs=[pl.BlockSpec((tc, D_SSM), rev), pl.BlockSpec((tc, D_SSM), lambda i: (nc - 1 - i, ucol)),
                  pl.BlockSpec((tc, N_STATE), rev), pl.BlockSpec((tc, N_STATE), rev),
                  pl.BlockSpec((8, N_STATE), prev), pl.BlockSpec((8, N_STATE), prev),
                  _full((SSM_BLOCKS, 128, 512)), _full((SSM_BLOCKS, 128, 512)),
                  _full((SSM_BLOCKS, 512, 128)), _full((SSM_BLOCKS, 512, 128)),
                  _full((1, N_STATE)), _full((1, N_STATE)), _full((1, D_SSM)), pl.BlockSpec(memory_space=pl.ANY)],
        out_specs=[pl.BlockSpec((tc, D_SSM), lambda i: (nc - 1 - i, ucol)),
                   _full((SSM_BLOCKS, 512, 128)), _full((SSM_BLOCKS, 512, 128)),
                   _full((SSM_BLOCKS, 128, 512)), _full((SSM_BLOCKS, 128, 512)),
                   _full((1, N_STATE)), _full((1, N_STATE)), _full((1, D_SSM))],
        out_shape=[jax.ShapeDtypeStruct(dproj.shape, BF16),
                   jax.ShapeDtypeStruct((SSM_BLOCKS, 512, 128), F32), jax.ShapeDtypeStruct((SSM_BLOCKS, 512, 128), F32),
                   jax.ShapeDtypeStruct((SSM_BLOCKS, 128, 512), F32), jax.ShapeDtypeStruct((SSM_BLOCKS, 128, 512), F32),
                   jax.ShapeDtypeStruct((1, N_STATE), F32), jax.ShapeDtypeStruct((1, N_STATE), F32),
                   jax.ShapeDtypeStruct((1, D_SSM), F32)],
        scratch_shapes=[pltpu.VMEM((tc, N_STATE), F32), pltpu.VMEM((tc, N_STATE), F32),
                        pltpu.VMEM((1, N_STATE), F32), pltpu.VMEM((1, N_STATE), F32)], sem=("arbitrary",),
        operands=(dy, proj, xr, xi, xr, xi, bre, bim, cre, cimn, are, aim, d, dproj), aliases={13: 0})


def _rel_bucket(dist):
    n = jnp.maximum(dist, 0)
    max_exact = NUM_BUCKETS // 2
    n_f = jnp.maximum(n, 1).astype(F32)
    large = max_exact + (jnp.log(n_f / max_exact) / math.log(REL_MAX_DISTANCE / max_exact)
                         * (NUM_BUCKETS - max_exact)).astype(jnp.int32)
    large = jnp.minimum(large, NUM_BUCKETS - 1)
    return jnp.where(n < max_exact, n, large)


def _bucket_tables():
    qi = jnp.arange(ATTN_BLOCK)[:, None]
    kj = jnp.arange(2 * ATTN_BLOCK)[None, :]
    delta = jnp.maximum(ATTN_BLOCK + qi - kj, 0)
    return jnp.stack([_rel_bucket(delta * r) for r in ATTN_DILATIONS]).astype(jnp.int32)


def _bias_tables(rel_bias, buckets, name):
    def body(tab_ref, bk_ref, o_ref):
        g = pl.program_id(0)
        bk = bk_ref[...]
        qi = lax.broadcasted_iota(jnp.int32, bk.shape, 0)
        kj = lax.broadcasted_iota(jnp.int32, bk.shape, 1)
        delta = ATTN_BLOCK + qi - kj
        band = (delta >= 0) & (delta <= ATTN_SPAN)
        accs = [jnp.zeros(bk.shape, F32) for _ in range(4)]
        for b in range(NUM_BUCKETS):
            hit = bk == b
            for h in range(4):
                accs[h] = jnp.where(hit, tab_ref[b, 4 * g + h], accs[h])
        for h in range(4):
            o_ref[h] = jnp.where(band, accs[h], NEG_INF)

    return pl.pallas_call(
        body, name=name, grid=(3,),
        in_specs=[pl.BlockSpec(memory_space=pltpu.SMEM),
                  pl.BlockSpec((None, ATTN_BLOCK, 2 * ATTN_BLOCK), lambda g: (g, 0, 0))],
        out_specs=pl.BlockSpec((None, 4, ATTN_BLOCK, 2 * ATTN_BLOCK), lambda g: (g, 0, 0, 0)),
        out_shape=jax.ShapeDtypeStruct((3, 4, ATTN_BLOCK, 2 * ATTN_BLOCK), F32),
        compiler_params=_params(("parallel",)),
    )(rel_bias, buckets)


def _bias_grad(db0, db1, buckets, name):
    def body(a_ref, b_ref, bk_ref, o_ref):
        bk = bk_ref[...]
        for h in range(4):
            dv = a_ref[h] + b_ref[h]
            for b in range(NUM_BUCKETS):
                o_ref[h, b:b + 1, :] = jnp.sum(jnp.where(bk == b, dv, 0.0), axis=0, keepdims=True)

    tab = pl.BlockSpec((None, 4, ATTN_BLOCK, 2 * ATTN_BLOCK), lambda g: (g, 0, 0, 0))
    return pl.pallas_call(
        body, name=name, grid=(3,),
        in_specs=[tab, tab, pl.BlockSpec((None, ATTN_BLOCK, 2 * ATTN_BLOCK), lambda g: (g, 0, 0))],
        out_specs=pl.BlockSpec((None, 4, NUM_BUCKETS, 2 * ATTN_BLOCK), lambda g: (g, 0, 0, 0)),
        out_shape=jax.ShapeDtypeStruct((3, 4, NUM_BUCKETS, 2 * ATTN_BLOCK), F32),
        compiler_params=_params(("parallel",)),
    )(db0, db1, buckets)


_ATTN_SUB = {1: 16, 4: 4, 16: 1}
_UNROLL = 4
_STATIC_UNITS = 16


def _unit_rows(j, s, r):
    start = j * ATTN_BLOCK * r + s
    return pl.ds(start, ATTN_BLOCK, stride=r) if r > 1 else pl.ds(start, ATTN_BLOCK)


def _for_units(r, nsub, fn, after):
    if r * nsub <= _STATIC_UNITS:
        units = [(j, s) for j in range(nsub) for s in range(r)]
        for i in range(0, len(units), _UNROLL):
            after([fn(j, s) for j, s in units[i:i + _UNROLL]])
    else:
        assert nsub == 1
        def four(i, c):
            after([fn(0, _UNROLL * i + k) for k in range(_UNROLL)])
            return c

        lax.fori_loop(0, r // _UNROLL, four, 0)


def _attn_cols(g):
    return tuple((_OFF[n] + ATTN_GROUP_WIDTH * g) // LANES for n in ("q", "k", "v"))


def _attn_fwd(proj, bias, g, name, job=None):
    r = ATTN_DILATIONS[g]
    nsub = _ATTN_SUB[r]
    T = proj.shape[0]
    sub = ATTN_BLOCK * r
    tb = sub * nsub
    qc, kc, vc = _attn_cols(g)
    scale = ATTN_HEAD_DIM ** -0.5

    def body(q_ref, kc_ref, kp_ref, vc_ref, vp_ref, bias_ref, o_ref, lse_ref):
        lane = lax.broadcasted_iota(jnp.int32, (ATTN_BLOCK, LANES), 1)
        kj = lax.broadcasted_iota(jnp.int32, (ATTN_BLOCK, 2 * ATTN_BLOCK), 1)
        dead = (pl.program_id(0) == 0) & (kj < ATTN_BLOCK)

        def one(j, s):
            rows = _unit_rows(j, s, r)
            before = _unit_rows(max(j - 1, 0), s, r)
            k_before = kc_ref[before, :] if j else kp_ref[before, :]
            v_before = vc_ref[before, :] if j else vp_ref[before, :]
            q = q_ref[rows, :]
            kcat = jnp.concatenate([k_before, kc_ref[rows, :]], axis=0).astype(BF16)
            vcat = jnp.concatenate([v_before, vc_ref[rows, :]], axis=0).astype(BF16)
            o_acc = jnp.zeros((ATTN_BLOCK, LANES), F32)
            l_acc = jnp.zeros((ATTN_BLOCK, LANES), F32)
            for hh in range(2):
                mine = (lane >= ATTN_HEAD_DIM) if hh else (lane < ATTN_HEAD_DIM)
                qm = jnp.where(mine, q, 0.0).astype(BF16)
                sc = _dot(qm, kcat, NT) * scale + bias_ref[hh]
                if j == 0:
                    sc = jnp.where(dead, NEG_INF, sc)
                m = jnp.max(sc, axis=-1, keepdims=True)
                p = jnp.exp(sc - m)
                l = jnp.sum(p, axis=-1, keepdims=True)
                o_acc = jnp.where(mine, _dot((p / l).astype(BF16), vcat), o_acc)
                l_acc = jnp.where(mine, m + jnp.log(l), l_acc)
            o_ref[rows, :] = o_acc
            lse_ref[rows, :] = l_acc

        _for_units(r, nsub, one, lambda results: None)

    cur = lambda c: pl.BlockSpec((tb, LANES), lambda b, p: (b, c + p))
    prev = lambda c: pl.BlockSpec((sub, LANES), lambda b, p: (jnp.maximum(b * nsub - 1, 0), c + p))
    out = pl.BlockSpec((tb, LANES), lambda b, p: (b, p))
    return _pc(
        body, job, name=name, grid=(T // tb, 2),
        in_specs=[cur(qc), cur(kc), prev(kc), cur(vc), prev(vc),
                  pl.BlockSpec((2, ATTN_BLOCK, 2 * ATTN_BLOCK), lambda b, p: (p, 0, 0))],
        out_specs=[out, out],
        out_shape=[jax.ShapeDtypeStruct((T, ATTN_GROUP_WIDTH), F32), jax.ShapeDtypeStruct((T, ATTN_GROUP_WIDTH), F32)],
        scratch_shapes=[], sem=("parallel", "parallel"), operands=(proj, proj, proj, proj, proj, bias))


def _attn_bwd(proj, do, corr, lse, bias, dproj, g, name):
    r = ATTN_DILATIONS[g]
    nsub = _ATTN_SUB[r]
    T = proj.shape[0]
    sub = ATTN_BLOCK * r
    tb = sub * nsub
    nb = T // tb
    qc, kc, vc = _attn_cols(g)
    dc = ATTN_GROUP_WIDTH * g // LANES
    scale = ATTN_HEAD_DIM ** -0.5

    def body(q_ref, kc_ref, kp_ref, vc_ref, vp_ref, do_ref, corr_ref, lse_ref, bias_ref, _,
             dproj_ref, db_ref, dq_s, dkc_s, dkp_s, dvc_s, dvp_s, kacc, vacc, stage, stage_sems):
        p, b = pl.program_id(0), pl.program_id(1)

        def to_dproj(e, slot, block, col):
            rows = pl.ds(pl.multiple_of(block * tb, tb), tb)
            cols = pl.ds(pl.multiple_of((col + p) * LANES, LANES), LANES)
            return pltpu.make_async_copy(stage.at[e, slot], dproj_ref.at[rows, cols], stage_sems.at[e, slot])

        def emit(e, block, col, value):
            count = p * nb + block
            slot = count % 2

            @pl.when(count >= 2)
            def _():
                to_dproj(e, slot, 0, col).wait()

            stage[e, slot] = value.astype(BF16)
            to_dproj(e, slot, block, col).start()

        def emit_keys(block):
            emit(1, block, kc, kacc[...])
            emit(2, block, vc, vacc[...])

        @pl.when(b == 0)
        def _():
            db_ref[...] = jnp.zeros_like(db_ref)
            kacc[...] = jnp.zeros_like(kacc)
            vacc[...] = jnp.zeros_like(vacc)

        @pl.when(b == nb)
        def _():
            emit_keys(nb - 1)

        @pl.when((b == nb) & (p == 1))
        def _():
            for e, col in enumerate((qc, kc, vc)):
                for slot in range(2):
                    to_dproj(e, slot, 0, col).wait()

        @pl.when(b < nb)
        def _():
            lane = lax.broadcasted_iota(jnp.int32, (ATTN_BLOCK, LANES), 1)
            kj = lax.broadcasted_iota(jnp.int32, (ATTN_BLOCK, 2 * ATTN_BLOCK), 1)
            dead = (b == 0) & (kj < ATTN_BLOCK)

            def one(j, s):
                rows = _unit_rows(j, s, r)
                before = _unit_rows(max(j - 1, 0), s, r)
                k_before = kc_ref[before, :] if j else kp_ref[before, :]
                v_before = vc_ref[before, :] if j else vp_ref[before, :]
                q = q_ref[rows, :]
                kcat = jnp.concatenate([k_before, kc_ref[rows, :]], axis=0).astype(BF16)
                vcat = jnp.concatenate([v_before, vc_ref[rows, :]], axis=0).astype(BF16)
                dov, corrv, lsev = do_ref[rows, :], corr_ref[rows, :], lse_ref[rows, :]
                dq_acc = jnp.zeros((ATTN_BLOCK, LANES), F32)
                dk_acc = jnp.zeros((2 * ATTN_BLOCK, LANES), F32)
                dv_acc = jnp.zeros((2 * ATTN_BLOCK, LANES), F32)
                dss = []
                for hh in range(2):
                    mine = (lane >= ATTN_HEAD_DIM) if hh else (lane < ATTN_HEAD_DIM)
                    col = slice(ATTN_HEAD_DIM * hh, ATTN_HEAD_DIM * hh + 1)
                    qm = jnp.where(mine, q, 0.0).astype(BF16)
                    dom = jnp.where(mine, dov, 0.0).astype(BF16)
                    sc = _dot(qm, kcat, NT) * scale + bias_ref[hh]
                    if j == 0:
                        sc = jnp.where(dead, NEG_INF, sc)
                    p = jnp.exp(sc - lsev[:, col])
                    ds = p * (_dot(dom, vcat, NT) - corrv[:, col])
                    dss.append(ds)
                    dsb = ds.astype(BF16)
                    dq_acc = jnp.where(mine, _dot(dsb, kcat) * scale, dq_acc)
                    dk_acc += _dot(dsb, qm, TN) * scale
                    dv_acc += _dot(p.astype(BF16), dom, TN)
                dq_s[rows, :] = dq_acc
                dkp_s[rows, :] = dk_acc[:ATTN_BLOCK]
                dkc_s[rows, :] = dk_acc[ATTN_BLOCK:]
                dvp_s[rows, :] = dv_acc[:ATTN_BLOCK]
                dvc_s[rows, :] = dv_acc[ATTN_BLOCK:]
                return dss

            def add_bias_grads(results):
                for hh in range(2):
                    db_ref[hh] += functools.reduce(lambda x, y: x + y, [dss[hh] for dss in results])

            _for_units(r, nsub, one, add_bias_grads)
            emit(0, b, qc, dq_s[...])
            tail = slice((nsub - 1) * sub, nsub * sub)
            kacc[tail, :] += dkp_s[0:sub, :]
            vacc[tail, :] += dvp_s[0:sub, :]

            @pl.when(b >= 1)
            def _():
                emit_keys(b - 1)

            for acc, before_s, cur_s in ((kacc, dkp_s, dkc_s), (vacc, dvp_s, dvc_s)):
                acc[...] = cur_s[...]
                for j in range(nsub - 1):
                    acc[j * sub:(j + 1) * sub, :] += before_s[(j + 1) * sub:(j + 2) * sub, :]

    last = nb - 1
    blk = (tb, LANES)
    cur = lambda c: pl.BlockSpec(blk, lambda p, b: (jnp.minimum(b, last), c + p))
    before = lambda c: pl.BlockSpec((sub, LANES), lambda p, b: (jnp.clip(b * nsub - 1, 0, nb * nsub - 1), c + p))
    tab = pl.BlockSpec((2, ATTN_BLOCK, 2 * ATTN_BLOCK), lambda p, b: (p, 0, 0))
    hbm = pl.BlockSpec(memory_space=pl.ANY)
    return pl.pallas_call(
        body, name=name, grid=(2, nb + 1),
        in_specs=[cur(qc), cur(kc), before(kc), cur(vc), before(vc), cur(dc), cur(dc), cur(0), tab, hbm],
        out_specs=[hbm, tab],
        out_shape=[jax.ShapeDtypeStruct(dproj.shape, BF16), jax.ShapeDtypeStruct((4, ATTN_BLOCK, 2 * ATTN_BLOCK), F32)],
        input_output_aliases={9: 0},
        scratch_shapes=[pltpu.VMEM(blk, F32)] * 7 + [pltpu.VMEM((3, 2) + blk, BF16), pltpu.SemaphoreType.DMA((3, 2))],
        compiler_params=_params(("arbitrary", "arbitrary")),
    )(proj, proj, proj, proj, proj, do, corr, lse, bias, dproj)


def _mix_weights(lses):
    m = jnp.maximum(jnp.maximum(lses[0], lses[1]), lses[2])
    es = [jnp.exp(l - m) for l in lses]
    inv = 1.0 / (es[0] + es[1] + es[2])
    return jnp.concatenate([e * inv for e in es], axis=1)


def _attn_mix(os, lses, proj, tm, name):
    T = proj.shape[0]
    zcol = _OFF["z_attn"] // D_ATTN

    def body(o0, o1, o2, l0, l1, l2, z_ref, out_ref):
        z = z_ref[...]
        o = jnp.concatenate([o0[...], o1[...], o2[...]], axis=1)
        alpha = _mix_weights([l0[...], l1[...], l2[...]])
        out_ref[...] = (o * alpha * (z * _sigmoid(z))).astype(BF16)

    row = lambda i: (i, 0)
    grp = pl.BlockSpec((tm, ATTN_GROUP_WIDTH), row)
    return pl.pallas_call(
        body, name=name, grid=(T // tm,),
        in_specs=[grp] * 6 + [pl.BlockSpec((tm, D_ATTN), lambda i: (i, zcol))],
        out_specs=pl.BlockSpec((tm, D_ATTN), row),
        out_shape=jax.ShapeDtypeStruct((T, D_ATTN), BF16),
        compiler_params=_params(("parallel",)),
    )(*os, *lses, proj)


def _attn_mix_bwd(d, os, lses, proj, dproj, tm, name):
    T = proj.shape[0]
    zcol = _OFF["z_attn"] // D_ATTN

    def body(d_ref, o0, o1, o2, l0, l1, l2, z_ref, _, do_ref, corr_ref, dz_ref):
        dv, z = d_ref[...], z_ref[...]
        ov = jnp.concatenate([o0[...], o1[...], o2[...]], axis=1)
        alpha = _mix_weights([l0[...], l1[...], l2[...]])
        sz = _sigmoid(z)
        oc = ov * alpha
        dz_ref[...] = (dv * oc * (sz * (1.0 + z * (1.0 - sz)))).astype(BF16)
        doc = dv * (z * sz)
        do_ref[...] = doc * alpha
        pr = doc * oc
        p3 = pr[:, 0:256] + pr[:, 256:512] + pr[:, 512:768]
        li = lax.broadcasted_iota(jnp.int32, (256, 256), 0) // ATTN_HEAD_DIM
        lj = lax.broadcasted_iota(jnp.int32, (256, 256), 1) // ATTN_HEAD_DIM
        ones = jnp.where(li == lj, 1.0, 0.0).astype(F32)
        s = lax.dot_general(p3, ones, NN, precision=lax.Precision.HIGHEST, preferred_element_type=F32)
        corr_ref[...] = alpha * jnp.concatenate([s, s, s], axis=1)

    row = lambda i: (i, 0)
    grp = pl.BlockSpec((tm, ATTN_GROUP_WIDTH), row)
    return pl.pallas_call(
        body, name=name, grid=(T // tm,),
        in_specs=[pl.BlockSpec((tm, D_ATTN), row)] + [grp] * 6 + [pl.BlockSpec((tm, D_ATTN), lambda i: (i, zcol)),
                                                                    pl.BlockSpec(memory_space=pl.ANY)],
        out_specs=[pl.BlockSpec((tm, D_ATTN), row)] * 2 + [pl.BlockSpec((tm, D_ATTN), lambda i: (i, zcol))],
        out_shape=[jax.ShapeDtypeStruct((T, D_ATTN), F32), jax.ShapeDtypeStruct((T, D_ATTN), F32),
                   jax.ShapeDtypeStruct(dproj.shape, BF16)],
        input_output_aliases={8: 2},
        compiler_params=_params(("parallel",)),
    )(d, *os, *lses, proj, dproj)


def _mem_probs(q_ref, kv_ref, h):
    hs = slice(MEM_HEAD_DIM * h, MEM_HEAD_DIM * (h + 1))
    qh = q_ref[:, hs].astype(BF16)
    kh = kv_ref[:, hs]
    vh = kv_ref[:, D_MEM + MEM_HEAD_DIM * h:D_MEM + MEM_HEAD_DIM * (h + 1)]
    s = _dot(qh, kh, NT) * (MEM_HEAD_DIM ** -0.5)
    p = jnp.exp(s - jnp.max(s, axis=-1, keepdims=True))
    pn = p / jnp.sum(p, axis=-1, keepdims=True)
    return qh, kh, vh, pn


def _mem_fwd(proj, kv, tm, name):
    T = proj.shape[0]
    M = kv.shape[0]
    qcol, zcol = _OFF["q_mem"] // D_MEM, _OFF["z_mem"] // D_MEM

    def body(q_ref, z_ref, kv_ref, o_ref):
        outs = []
        for h in range(MEM_HEADS):
            _, _, vh, pn = _mem_probs(q_ref, kv_ref, h)
            outs.append(_dot(pn.astype(BF16), vh))
        z = z_ref[...]
        o_ref[...] = (jnp.concatenate(outs, axis=1) * (z * _sigmoid(z))).astype(BF16)

    return pl.pallas_call(
        body, name=name, grid=(T // tm,),
        in_specs=[pl.BlockSpec((tm, D_MEM), lambda i: (i, qcol)), pl.BlockSpec((tm, D_MEM), lambda i: (i, zcol)),
                  _full((M, 2 * D_MEM))],
        out_specs=pl.BlockSpec((tm, D_MEM), lambda i: (i, 0)),
        out_shape=jax.ShapeDtypeStruct((T, D_MEM), BF16),
        compiler_params=_params(("parallel",)),
    )(proj, proj, kv)


def _mem_bwd(d, proj, kv, tm, name):
    T = proj.shape[0]
    M = kv.shape[0]
    qcol, zcol = _OFF["q_mem"] // D_MEM, _OFF["z_mem"] // D_MEM

    def body(d_ref, q_ref, z_ref, kv_ref, dq_ref, dz_ref, dkv_ref):
        @pl.when(pl.program_id(0) == 0)
        def _():
            dkv_ref[...] = jnp.zeros_like(dkv_ref)

        z = z_ref[...]
        sz = _sigmoid(z)
        dv = d_ref[...]
        dov = dv * (z * sz)
        scale = MEM_HEAD_DIM ** -0.5
        outs, dqs = [], []
        for h in range(MEM_HEADS):
            hs = slice(MEM_HEAD_DIM * h, MEM_HEAD_DIM * (h + 1))
            qh, kh, vh, pn = _mem_probs(q_ref, kv_ref, h)
            pnb = pn.astype(BF16)
            oh = _dot(pnb, vh)
            outs.append(oh)
            doh = dov[:, hs]
            dohb = doh.astype(BF16)
            dp = _dot(dohb, vh, NT)
            ds = pn * (dp - jnp.sum(doh * oh, axis=-1, keepdims=True))
            dsb = ds.astype(BF16)
            dqs.append(_dot(dsb, kh) * scale)
            dkv_ref[:, hs] += _dot(dsb, qh, TN) * scale
            vs = slice(D_MEM + MEM_HEAD_DIM * h, D_MEM + MEM_HEAD_DIM * (h + 1))
            dkv_ref[:, vs] += _dot(pnb, dohb, TN)
        dq_ref[...] = jnp.concatenate(dqs, axis=1).astype(BF16)
        dz_ref[...] = (dv * jnp.concatenate(outs, axis=1) * (sz * (1.0 + z * (1.0 - sz)))).astype(BF16)

    row = lambda i: (i, 0)
    return pl.pallas_call(
        body, name=name, grid=(T // tm,),
        in_specs=[pl.BlockSpec((tm, D_MEM), row), pl.BlockSpec((tm, D_MEM), lambda i: (i, qcol)),
                  pl.BlockSpec((tm, D_MEM), lambda i: (i, zcol)), _full((M, 2 * D_MEM))],
        out_specs=[pl.BlockSpec((tm, D_MEM), row), pl.BlockSpec((tm, D_MEM), row), _full((M, 2 * D_MEM))],
        out_shape=[jax.ShapeDtypeStruct((T, D_MEM), BF16), jax.ShapeDtypeStruct((T, D_MEM), BF16),
                   jax.ShapeDtypeStruct((M, 2 * D_MEM), F32)],
        compiler_params=_params(("arbitrary",)),
    )(d, proj, proj, kv)


def _branches_and_gates(os_ref, oa_ref, om_ref, gl_refs, bg_ref, ws_ref, wa_ref, wm_ref):
    outs = (_dot(os_ref[...], ws_ref[...]), _dot(oa_ref[...], wa_ref[...]), _dot(om_ref[...], wm_ref[...]))
    gates = tuple(_sigmoid(jnp.concatenate([gl_refs[2 * k][...], gl_refs[2 * k + 1][...]], axis=1)
                           + bg_ref[:, D_MODEL * k:D_MODEL * (k + 1)]) for k in range(3))
    return outs, gates


def _merge_specs(tm):
    row = lambda i: (i, 0)
    first = _OFF["gates"] // GATE_BLOCK
    gate = [pl.BlockSpec((tm, GATE_BLOCK), (lambda i, k=k: (i, first + k))) for k in range(N_GATES // GATE_BLOCK)]
    return ([pl.BlockSpec((tm, D_SSM), row), pl.BlockSpec((tm, D_ATTN), row), pl.BlockSpec((tm, D_MEM), row)] + gate
            + [_full((1, N_GATES)), _full((D_SSM, D_MODEL)), _full((D_ATTN, D_MODEL)), _full((D_MEM, D_MODEL)),
               _full((D_MODEL, D_MODEL))])


def _merge_fwd(x, o_ssm, o_attn, o_mem, proj, bg, ws, wa, wm, wo, tm, name):
    T = x.shape[0]

    def body(os_ref, oa_ref, om_ref, g0, g1, g2, g3, g4, g5, bg_ref, ws_ref, wa_ref, wm_ref, wo_ref, x_ref,
             xo_ref, mg_ref):
        outs, gates = _branches_and_gates(os_ref, oa_ref, om_ref, (g0, g1, g2, g3, g4, g5), bg_ref, ws_ref, wa_ref,
                                          wm_ref)
        merged = (gates[0] * outs[0] + gates[1] * outs[1] + gates[2] * outs[2]).astype(BF16)
        mg_ref[...] = merged
        xo_ref[...] = x_ref[...] + _dot(merged, wo_ref[...])

    row = lambda i: (i, 0)
    return pl.pallas_call(
        body, name=name, grid=(T // tm,),
        in_specs=_merge_specs(tm) + [pl.BlockSpec((tm, D_MODEL), row)],
        out_specs=[pl.BlockSpec((tm, D_MODEL), row), pl.BlockSpec((tm, D_MODEL), row)],
        out_shape=[jax.ShapeDtypeStruct((T, D_MODEL), F32), jax.ShapeDtypeStruct((T, D_MODEL), BF16)],
        compiler_params=_params(("parallel",)),
    )(o_ssm, o_attn, o_mem, *([proj] * (N_GATES // GATE_BLOCK)), bg, ws, wa, wm, wo, x)


def _merge_bwd(dx, o_ssm, o_attn, o_mem, proj, bg, ws, wa, wm, wo, tm, name, job=None):
    T = dx.shape[0]

    n = T // tm

    def body(os_ref, oa_ref, om_ref, g0, g1, g2, g3, g4, g5, bg_ref, ws_ref, wa_ref, wm_ref, wo_ref, dx_ref,
             dproj_ref, db_ref, dos_ref, doa_ref, dom_ref, dbg_ref, dgl_buf, dgl_sems):
        i = pl.program_id(0)
        slot = i % 2

        def to_dproj(s, row0):
            return pltpu.make_async_copy(dgl_buf.at[s], dproj_ref.at[pl.ds(row0, tm), pl.ds(_OFF["gates"], N_GATES)],
                                         dgl_sems.at[s])

        @pl.when(i == 0)
        def _():
            dbg_ref[...] = jnp.zeros_like(dbg_ref)

        @pl.when(i >= 2)
        def _():
            to_dproj(slot, 0).wait()

        outs, gates = _branches_and_gates(os_ref, oa_ref, om_ref, (g0, g1, g2, g3, g4, g5), bg_ref, ws_ref, wa_ref,
                                          wm_ref)
        dm = _dot(dx_ref[...].astype(BF16), wo_ref[...], NT)
        w_refs = (ws_ref, wa_ref, wm_ref)
        do_refs = (dos_ref, doa_ref, dom_ref)
        for k in range(3):
            cols = slice(D_MODEL * k, D_MODEL * (k + 1))
            dgl = dm * outs[k] * (gates[k] * (1.0 - gates[k]))
            dgl_buf[slot, :, cols] = dgl.astype(BF16)
            dbg_ref[:, cols] += jnp.sum(dgl, axis=0, keepdims=True)
            dbk = (dm * gates[k]).astype(BF16)
            db_ref[:, cols] = dbk
            do_refs[k][...] = _dot(dbk, w_refs[k][...], NT)
        to_dproj(slot, pl.multiple_of(i * tm, tm)).start()

        @pl.when(i == n - 1)
        def _():
            for s in range(min(2, n)):
                to_dproj(s, 0).wait()

    row = lambda i: (i, 0)
    return _pc(
        body, job, name=name, grid=(n,),
        in_specs=_merge_specs(tm) + [pl.BlockSpec((tm, D_MODEL), row)],
        out_specs=[pl.BlockSpec(memory_space=pl.ANY), pl.BlockSpec((tm, N_GATES), row), pl.BlockSpec((tm, D_SSM), row),
                   pl.BlockSpec((tm, D_ATTN), row), pl.BlockSpec((tm, D_MEM), row), _full((1, N_GATES))],
        out_shape=[jax.ShapeDtypeStruct((T, D_IN), BF16), jax.ShapeDtypeStruct((T, N_GATES), BF16),
                   jax.ShapeDtypeStruct((T, D_SSM), F32), jax.ShapeDtypeStruct((T, D_ATTN), F32),
                   jax.ShapeDtypeStruct((T, D_MEM), F32), jax.ShapeDtypeStruct((1, N_GATES), F32)],
        scratch_shapes=[pltpu.VMEM((2, tm, N_GATES), BF16), pltpu.SemaphoreType.DMA((2,))], sem=("arbitrary",),
        operands=(o_ssm, o_attn, o_mem, *([proj] * (N_GATES // GATE_BLOCK)), bg, ws, wa, wm, wo, dx))


def _loss_head(x, g, target, tm, name):
    T, D = x.shape

    def body(x_ref, g_ref, t_ref, loss_ref, dx_ref, dg_ref):
        @pl.when(pl.program_id(0) == 0)
        def _():
            loss_ref[...] = jnp.zeros_like(loss_ref)
            dg_ref[...] = jnp.zeros_like(dg_ref)

        xv = x_ref[...]
        r = lax.rsqrt(jnp.mean(xv * xv, axis=-1, keepdims=True) + EPS)
        xr = xv * r
        err = xr * g_ref[...] - t_ref[...]
        loss_ref[...] += 0.5 * jnp.sum(jnp.mean(err * err, axis=-1, keepdims=True), axis=0, keepdims=True)
        dy = err * (1.0 / D)
        dg_ref[...] += jnp.sum(dy * xr, axis=0, keepdims=True)
        wv = dy * g_ref[...]
        dx_ref[...] = r * (wv - xr * jnp.mean(wv * xr, axis=-1, keepdims=True))

    row = lambda i: (i, 0)
    return pl.pallas_call(
        body, name=name, grid=(T // tm,),
        in_specs=[pl.BlockSpec((tm, D), row), _full((1, D)), pl.BlockSpec((tm, D), row)],
        out_specs=[_full((1, 128)), pl.BlockSpec((tm, D), row), _full((1, D))],
        out_shape=[jax.ShapeDtypeStruct((1, 128), F32), jax.ShapeDtypeStruct((T, D), F32),
                   jax.ShapeDtypeStruct((1, D), F32)],
        compiler_params=_params(("arbitrary",)),
    )(x, g, target)


def _adamw(parts, w, m, v, tr, name):
    L, R, C = w.shape

    def body(p_ref, w_ref, m_ref, v_ref, g_ref, d_ref, mo_ref, vo_ref):
        g = p_ref[0].astype(F32)
        for s in range(1, N_DEV):
            g = g + p_ref[s].astype(F32)
        mn = ADAM_B1 * m_ref[...] + (1.0 - ADAM_B1) * g
        vn = ADAM_B2 * v_ref[...] + (1.0 - ADAM_B2) * (g * g)
        m_hat = mn / (1.0 - ADAM_B1 ** ADAM_STEP)
        v_hat = vn / (1.0 - ADAM_B2 ** ADAM_STEP)
        g_ref[...] = g
        d_ref[...] = -ADAM_LR * (m_hat / (jnp.sqrt(v_hat) + ADAM_EPS) + ADAM_WD * w_ref[...])
        mo_ref[...] = mn
        vo_ref[...] = vn

    one = pl.BlockSpec((None, tr, C), lambda l, i: (l, i, 0))
    return pl.pallas_call(
        body, name=name, grid=(L, R // tr),
        in_specs=[pl.BlockSpec((N_DEV, None, tr, C), lambda l, i: (0, l, i, 0)), one, one, one],
        out_specs=[one] * 4,
        out_shape=[jax.ShapeDtypeStruct((L, R, C), F32)] * 4,
        compiler_params=_params(("parallel", "parallel")),
    )(parts, w, m, v)


_SHARDED = (("w_in", (1088, 1024), 1), ("w_glu", (96, 768), 0), ("w_mem_kv", (128, 1024), 0),
            ("w_br_ssm", (768, 128), 1), ("w_br_attn", (768, 128), 1), ("w_br_mem", (512, 128), 1),
            ("w_out", (128, 1024), 0))
_W_IN = 0
_SMALL = tuple(range(1, len(_SHARDED)))


class _Job(NamedTuple):
    ins: list
    out_shape: list
    aliases: dict
    pairs: Callable
    n: int


def _peers():
    x, y, c = lax.axis_index("x"), lax.axis_index("y"), lax.axis_index("c")
    me = 4 * x + 2 * y + c
    out = []
    for k in range(1, N_DEV):
        px = 1 - x if k & 4 else x
        py = 1 - y if k & 2 else y
        pc = 1 - c if k & 1 else c
        out.append(((px, py, pc), 4 * px + 2 * py + pc))
    return me, out


def _copies(pairs, send_sems, recv_sems, local_sems, arrivals):
    me, peers = _peers()
    local = [pltpu.make_async_copy(src(me), dst(me), local_sems.at[j]) for j, (src, dst) in enumerate(pairs)]
    sends, recvs = [], []
    for k, (peer, lin) in enumerate(peers):
        for j, (src, dst) in enumerate(pairs):
            for to, out in ((dst(me), sends), (dst(lin), recvs)):
                if out is sends or arrivals:
                    out.append(pltpu.make_async_remote_copy(
                        src_ref=src(lin), dst_ref=to, send_sem=send_sems.at[j, k], recv_sem=recv_sems.at[j, k],
                        device_id=peer, device_id_type=pl.DeviceIdType.MESH))
    return local, sends, recvs


def _start_copies(pairs, *sems):
    local, sends, _ = _copies(pairs, *sems, arrivals=False)
    for cp in local + sends:
        cp.start()


def _wait_copies(pairs, *sems):
    local, sends, recvs = _copies(pairs, *sems, arrivals=True)
    for cp in recvs:
        cp.wait_recv()
    for cp in sends:
        cp.wait_send()
    for cp in local:
        cp.wait()


def _job_scratch(job):
    return [pltpu.SemaphoreType.DMA((job.n, N_DEV - 1)), pltpu.SemaphoreType.DMA((job.n, N_DEV - 1)),
            pltpu.SemaphoreType.DMA((job.n,))]


def _pc(body, job, *, name, grid, in_specs, out_specs, out_shape, scratch_shapes, sem, operands, aliases=None):
    aliases = aliases or {}
    if job is None:
        return pl.pallas_call(body, name=name, grid=grid, in_specs=in_specs, out_specs=out_specs, out_shape=out_shape,
                              scratch_shapes=scratch_shapes, input_output_aliases=aliases,
                              compiler_params=_params(sem))(*operands)
    a = len(in_specs)
    b = a + len(job.ins)
    c = b + len(out_shape)
    d = c + len(job.out_shape)
    e = d + len(scratch_shapes)

    def carried(*refs):
        pairs = job.pairs(refs[a:b], refs[c:d])
        ids = [pl.program_id(k) for k in range(len(grid))]
        first = functools.reduce(jnp.logical_and, [i == 0 for i in ids])
        last = functools.reduce(jnp.logical_and, [i == n - 1 for i, n in zip(ids, grid)])

        @pl.when(first)
        def _():
            _start_copies(pairs, *refs[e:])

        body(*refs[:a], *refs[b:c], *refs[d:e])

        @pl.when(last)
        def _():
            _wait_copies(pairs, *refs[e:])

    hbm = pl.BlockSpec(memory_space=pl.ANY)
    outs = pl.pallas_call(
        carried, name=name, grid=grid,
        in_specs=list(in_specs) + [hbm] * len(job.ins), out_specs=list(out_specs) + [hbm] * len(job.out_shape),
        out_shape=list(out_shape) + list(job.out_shape),
        input_output_aliases={**aliases, **{a + i: len(out_shape) + o for i, o in job.aliases.items()}},
        scratch_shapes=list(scratch_shapes) + _job_scratch(job),
        compiler_params=_params(("arbitrary",) * len(grid)),
    )(*operands, *job.ins)
    return outs[:len(out_shape)], outs[len(out_shape):]


def _gather_via_sibling(x, take, place, out_shape, name, landing=None):
    def body(*refs):
        x_ref, o_ref = refs[0], refs[-4]
        send_sems, recv_sems, local_sem = refs[-3:]
        x, y, c = lax.axis_index("x"), lax.axis_index("y"), lax.axis_index("c")
        me, sibling = (x, y, c), (x, y, 1 - c)
        chips = [(1 - x, y), (x, 1 - y), (1 - x, 1 - y)]
        src = take(x_ref)

        def slot(px, py, pc):
            return place(o_ref, 4 * px + 2 * py + pc)

        def copy(k, block, to, first_hand):
            return pltpu.make_async_remote_copy(
                src_ref=src if first_hand else slot(*block), dst_ref=slot(*block), send_sem=send_sems.at[k],
                recv_sem=recv_sems.at[k], device_id=to, device_id_type=pl.DeviceIdType.MESH)

        mine = pltpu.make_async_copy(src, slot(*me), local_sem)
        mine.start()
        first = [copy(0, me, sibling, True)] + [copy(1 + j, me, (*chip, c), True) for j, chip in enumerate(chips)]
        for cp in first:
            cp.start()
        passed = []
        for j, chip in enumerate(chips):
            copy(1 + j, (*chip, c), me, True).wait_recv()
            passed.append(copy(4 + j, (*chip, c), sibling, False))
            passed[-1].start()
        copy(0, sibling, me, True).wait_recv()
        for j, chip in enumerate(chips):
            copy(4 + j, (*chip, 1 - c), me, False).wait_recv()
        for cp in first + passed:
            cp.wait_send()
        mine.wait()

    hbm = pl.BlockSpec(memory_space=pl.ANY)
    ins = [x] if landing is None else [x, landing]
    return pl.pallas_call(
        body, name=name, in_specs=[hbm] * len(ins), out_specs=hbm, out_shape=out_shape,
        input_output_aliases={} if landing is None else {1: 0},
        scratch_shapes=[pltpu.SemaphoreType.DMA((N_DEV - 1,)), pltpu.SemaphoreType.DMA((N_DEV - 1,)),
                        pltpu.SemaphoreType.DMA],
    )(*ins)


def _lane_window(ref, who):
    return ref.at[:, pl.ds(pl.multiple_of(who * LANES, LANES), LANES)]


def _gather_job(shards, items):
    out_shape = []
    for i, _ in items:
        _, s, axis = _SHARDED[i]
        whole = i != _W_IN and axis == 1
        out_shape.append(jax.ShapeDtypeStruct((s[0], N_DEV * s[1]) if whole else (N_DEV,) + s, BF16))

    def pairs(in_refs, out_refs):
        out = []
        for (i, l), src, dst in zip(items, in_refs, out_refs):
            if i != _W_IN and _SHARDED[i][2] == 1:
                out.append((lambda who, src=src, l=l: src.at[l], lambda who, dst=dst: _lane_window(dst, who)))
            else:
                out.append((lambda who, src=src, l=l: src.at[l], lambda who, dst=dst: dst.at[who]))
        return out

    return _Job([shards[i] for i, _ in items], out_shape, {}, pairs, len(items))


def _landed_weights(items, landed):
    out = {}
    for (i, _), a in zip(items, landed):
        n, s, axis = _SHARDED[i]
        if i == _W_IN:
            out[n] = a.reshape(D_IN, D_MODEL)
        elif axis == 0:
            out[n] = a.reshape(N_DEV * s[0], s[1])
        else:
            out[n] = a
    return out


def _scatter_job(grads, items, layer, parts=None):
    ng = len(grads)
    out_shape = [jax.ShapeDtypeStruct((N_DEV, DEPTH) + _SHARDED[i][1], BF16) for i in items]

    def pairs(in_refs, out_refs):
        out = []
        for i, src, dst in zip(items, in_refs[:ng], out_refs):
            _, s, axis = _SHARDED[i]
            if i == _W_IN:
                take = lambda who, src=src: src.at[who]
            elif axis == 0:
                take = lambda who, src=src, s=s: src.at[pl.ds(pl.multiple_of(who * s[0], 16), s[0])]
            else:
                take = lambda who, src=src: _lane_window(src, who)
            out.append((take, lambda who, dst=dst: dst.at[who, layer]))
        return out

    aliases = {} if parts is None else {ng + j: j for j in range(len(items))}
    return _Job(list(grads) + ([] if parts is None else list(parts)), out_shape, aliases, pairs, len(items))


def _rows_job(src, row0, landing=None):
    n = src.shape[0]
    pairs = lambda in_refs, out_refs: [(lambda who: in_refs[0], lambda who: out_refs[0].at[who, pl.ds(row0, n)])]
    return _Job([src] + ([] if landing is None else [landing]), [jax.ShapeDtypeStruct((N_DEV, _REP_ROWS, LANES), F32)],
                {} if landing is None else {1: 0}, pairs, 1)


_REPLICATED = (("norm_g", (2, 1024)), ("mem_norm_g", (2, 1024)), ("b_gate", (2, 3072)),
               ("ssm_lambda_re", (2, 48, 64)), ("ssm_lambda_im", (2, 48, 64)), ("ssm_log_dt", (2, 48)),
               ("ssm_b_re", (2, 48, 64, 16)), ("ssm_b_im", (2, 48, 64, 16)), ("ssm_c_re", (2, 48, 16, 64)),
               ("ssm_c_im", (2, 48, 16, 64)), ("ssm_d", (2, 768)), ("b_glu", (2, 768)), ("rel_bias", (32, 12)),
               ("final_norm_g", (1024,)))
_PER_LAYER = tuple((n, s[1:]) for n, s in _REPLICATED if s[0] == DEPTH and len(s) > 1)
_SHARED = tuple((n, s) for n, s in _REPLICATED if (n, s[1:]) not in _PER_LAYER)
_REP_HALF_ROWS = 1664
_REP_ROWS = 2 * _REP_HALF_ROWS
assert sum(int(np.prod(s)) for _, s in _PER_LAYER + _SHARED) <= _REP_HALF_ROWS * LANES


def _pack_half(tree, layer, shared):
    flat = [tree[n][layer].reshape(-1) for n, _ in _PER_LAYER]
    if shared:
        flat += [tree[n].reshape(-1) for n, _ in _SHARED]
    flat = jnp.concatenate(flat)
    return jnp.pad(flat, (0, _REP_HALF_ROWS * LANES - flat.shape[0])).reshape(_REP_HALF_ROWS, LANES)


def _pack_replicated(tree):
    return jnp.concatenate([_pack_half(tree, 1, False), _pack_half(tree, 0, True)])[None]


def _unpack_replicated(packed):
    halves = packed.reshape(2, -1)
    out, r = {}, 0
    for n, s in _PER_LAYER:
        size = int(np.prod(s))
        out[n] = jnp.stack([halves[1, r:r + size].reshape(s), halves[0, r:r + size].reshape(s)])
        r += size
    for n, s in _SHARED:
        size = int(np.prod(s))
        out[n] = halves[1, r:r + size].reshape(s)
        r += size
    return out


def _discretize(lam_re, lam_im, log_dt, b_re, b_im):
    dt = jnp.exp(log_dt)[:, None]
    mag = jnp.exp(lam_re * dt)
    abar_re, abar_im = mag * jnp.cos(lam_im * dt), mag * jnp.sin(lam_im * dt)
    den = lam_re * lam_re + lam_im * lam_im
    nr, ni = abar_re - 1.0, abar_im
    f_re = (nr * lam_re + ni * lam_im) / den
    f_im = (ni * lam_re - nr * lam_im) / den
    bbar_re = f_re[..., None] * b_re - f_im[..., None] * b_im
    bbar_im = f_re[..., None] * b_im + f_im[..., None] * b_re
    return abar_re, abar_im, bbar_re, bbar_im


def _block_diag(a):
    _, R, C = a.shape
    a = a.reshape(SSM_BLOCKS, 8, R, C)
    eye = jnp.eye(8, dtype=a.dtype)
    return (a[:, :, :, None, :] * eye[None, :, None, :, None]).reshape(SSM_BLOCKS, 8 * R, 8 * C)


def _diag_blocks(a, R, C):
    a = a.reshape(SSM_BLOCKS, 8, R, 8, C)
    eye = jnp.eye(8, dtype=a.dtype)
    return jnp.sum(a * eye[None, :, None, :, None], axis=3).reshape(SSM_GROUPS, R, C)


def _carried(result, job):
    return (result, None) if job is None else result


def _layer_fwd(x, mem, W, P, bias, layer, jobs):
    tag = f"l{layer}"
    abar_re, abar_im, bbar_re, bbar_im = _discretize(P["ssm_lambda_re"][layer], P["ssm_lambda_im"][layer],
                                                     P["ssm_log_dt"][layer], P["ssm_b_re"][layer], P["ssm_b_im"][layer])
    c_re, c_im = P["ssm_c_re"][layer], P["ssm_c_im"][layer]
    ssm = dict(
        are=abar_re.reshape(1, N_STATE), aim=abar_im.reshape(1, N_STATE),
        bre=_block_diag(bbar_re.transpose(0, 2, 1)).astype(BF16), bim=_block_diag(bbar_im.transpose(0, 2, 1)).astype(BF16),
        cre=_block_diag(c_re.transpose(0, 2, 1)).astype(BF16), cimn=_block_diag(-c_im.transpose(0, 2, 1)).astype(BF16),
        d=P["ssm_d"][layer].reshape(1, D_SSM))
    bglu = P["b_glu"][layer].reshape(1, D_SSM)
    bgate = P["b_gate"][layer].reshape(1, N_GATES)
    g = P["norm_g"][layer].reshape(1, D_MODEL)
    gm = P["mem_norm_g"][layer].reshape(1, D_MODEL)
    delivered = {}

    def carry(stage):
        return jobs[stage][0] if stage in jobs else None

    def deliver(stage, landed):
        if landed is not None:
            delivered[stage] = _landed_weights(jobs[stage][1], landed)

    T = x.shape[0]
    (proj, h), landed = _carried(_norm_proj(x, g, W["w_in"], min(T, 1024), 2176, f"{tag}_proj", job=carry("proj"),
                                            w_turned=True), carry("proj"))
    deliver("proj", landed)
    W = {**W, **delivered.get("proj", {})}
    (xr, xi, y, o_ssm), landed = _carried(
        _ssm_fwd(proj, ssm["bre"], ssm["bim"], ssm["cre"], ssm["cimn"], ssm["are"], ssm["aim"], ssm["d"], W["w_glu"],
                 bglu, 512, f"{tag}_ssm", job=carry("ssm")), carry("ssm"))
    deliver("ssm", landed)
    os, lses = [], []
    for grp in range(3):
        stage = f"attn{grp}"
        (o_g, lse_g), landed = _carried(_attn_fwd(proj, bias[grp], grp, f"{tag}_{stage}", job=carry(stage)), carry(stage))
        deliver(stage, landed)
        os.append(o_g)
        lses.append(lse_g)
    o_attn = _attn_mix(os, lses, proj, min(T, ROW_TILE), f"{tag}_attn_mix")
    kvb, hm = _norm_proj(mem, gm, W["w_mem_kv"], mem.shape[0], 1024, f"{tag}_mem_kv", out_dtype=BF16)
    o_mem = _mem_fwd(proj, kvb, min(T, ROW_TILE), f"{tag}_mem")
    x_out, merged = _merge_fwd(x, o_ssm, o_attn, o_mem, proj, bgate, W["w_br_ssm"], W["w_br_attn"], W["w_br_mem"],
                               W["w_out"], 512, f"{tag}_merge")
    res = dict(x=x, mem=mem, proj=proj, h=h, xr=xr, xi=xi, y=y, o_ssm=o_ssm, os=os, lses=lses,
               o_attn=o_attn, kvb=kvb, hm=hm, o_mem=o_mem, merged=merged, ssm=ssm, bglu=bglu,
               bgate=bgate, g=g, gm=gm, W=W)
    return x_out, res, delivered


def _layer_bwd(dx, res, P, bias, layer, jobs):
    tag = f"l{layer}b"
    proj, ssm, W = res["proj"], res["ssm"], res["W"]
    T = dx.shape[0]
    landed = {}

    def run(stage, fn, job):
        out, landed[stage] = _carried(fn(job), job)
        if job is None:
            del landed[stage]
        return out

    dproj, dbr, do_ssm, do_attn, do_mem, dbg = run(
        "merge", lambda job: _merge_bwd(dx, res["o_ssm"], res["o_attn"], res["o_mem"], proj, res["bgate"], W["w_br_ssm"],
                                        W["w_br_attn"], W["w_br_mem"], W["w_out"], 512, f"{tag}_merge", job=job),
        jobs.get("merge"))
    gw = {}
    tk = min(T, 1024)
    gw["w_out"] = _mm_tn(res["merged"], dx, 1024, 1024, tk, f"{tag}_dw_out")
    gw["w_br_ssm"] = _mm_tn(res["o_ssm"], dbr, 768, 1024, tk, f"{tag}_dw_br_ssm", b_col=0, n=1024)
    gw["w_br_attn"] = _mm_tn(res["o_attn"], dbr, 768, 1024, tk, f"{tag}_dw_br_attn", b_col=1024, n=1024)
    gw["w_br_mem"] = _mm_tn(res["o_mem"], dbr, 512, 1024, tk, f"{tag}_dw_br_mem", b_col=2048, n=1024)

    rows = min(T, ROW_TILE)
    dqm, dzm, dkv = _mem_bwd(do_mem, proj, res["kvb"], rows, f"{tag}_mem")
    M = dkv.shape[0]
    gw["w_mem_kv"] = _mm_tn(res["hm"], dkv, 1024, 1024, M, f"{tag}_dw_mem_kv")
    _, dgm = _proj_bwd(dkv.astype(BF16), W["w_mem_kv"], res["mem"], res["gm"], jnp.zeros_like(res["mem"]), M, 1024,
                       f"{tag}_mem_norm")

    do_g, corr, dproj = _attn_mix_bwd(do_attn, res["os"], res["lses"], proj, dproj, rows, f"{tag}_attn_mix")
    dbs = []
    for grp in range(3):
        dproj, db_g = _attn_bwd(proj, do_g, corr, res["lses"][grp], bias[grp], dproj, grp, f"{tag}_attn{grp}")
        dbs.append(db_g)
    dbias = jnp.stack(dbs)

    dy, dproj, gelu_b, dt_b, dbglu = _glu_bwd(do_ssm, res["y"], proj, W["w_glu"], res["bglu"], dproj, rows, f"{tag}_glu")
    gw["w_glu"] = _mm_tn(gelu_b, dt_b, 768, 768, tk, f"{tag}_dw_glu")
    dproj, dbre, dbim, dcre, dcim, dare, daim, dd = run(
        "ssm", lambda job: _ssm_bwd(dy, proj, res["xr"], res["xi"], ssm["bre"], ssm["bim"], ssm["cre"], ssm["cimn"],
                                    ssm["are"], ssm["aim"], ssm["d"], dproj, 256, f"{tag}_ssm", job=job),
        jobs.get("ssm"))
    _, disc_vjp = jax.vjp(_discretize, P["ssm_lambda_re"][layer], P["ssm_lambda_im"][layer], P["ssm_log_dt"][layer],
                          P["ssm_b_re"][layer], P["ssm_b_im"][layer])
    d_lre, d_lim, d_ldt, d_bre, d_bim = disc_vjp((dare.reshape(SSM_GROUPS, SSM_STATE), daim.reshape(SSM_GROUPS, SSM_STATE),
                                                  _diag_blocks(dbre, SSM_STATE, SSM_GROUP),
                                                  _diag_blocks(dbim, SSM_STATE, SSM_GROUP)))

    small = [gw[_SHARDED[i][0]] for i in _SMALL]
    for seg, piece in (("q_mem", dqm), ("z_mem", dzm)):
        dproj = lax.dynamic_update_slice(dproj, piece, (0, _OFF[seg]))
    dw_in = run("dw_in", lambda job: _mm_tn(dproj, res["h"], 2176, 1024, min(T, 1024), f"{tag}_dw_in", job=job),
                jobs["dw_in"](small) if "dw_in" in jobs else None)
    dw_in = dw_in.reshape((N_DEV,) + _SHARDED[_W_IN][1])
    dx_in, dg = run("proj", lambda job: _proj_bwd(dproj, W["w_in"], res["x"], res["g"], dx, min(T, 1024), 2176,
                                                  f"{tag}_proj", job=job, w_turned=True),
                    jobs["proj"](small, dw_in, landed) if "proj" in jobs else None)

    gp = dict(norm_g=dg[0], mem_norm_g=dgm[0], b_gate=dbg[0], ssm_lambda_re=d_lre, ssm_lambda_im=d_lim,
              ssm_log_dt=d_ldt, ssm_b_re=d_bre, ssm_b_im=d_bim,
              ssm_c_re=_diag_blocks(dcre, SSM_GROUP, SSM_STATE), ssm_c_im=_diag_blocks(dcim, SSM_GROUP, SSM_STATE),
              ssm_d=dd[0], b_glu=dbglu[0])
    return dx_in, dw_in, gp, dbias, landed


def _train_step(x, mem, target, shards, P):
    rest0 = [(i, 0) for i in _SMALL]
    thirds1 = [[(i, 1) for i in _SMALL[k::3]] for k in range(3)]
    first = [(_W_IN, 0)]
    w_in0 = _gather_via_sibling(shards[_W_IN], lambda ref: ref.at[0], lambda ref, s: ref.at[s],
                                jax.ShapeDtypeStruct((N_DEV,) + _SHARDED[_W_IN][1], BF16), "gather_w_in0")
    W0 = _landed_weights(first, [w_in0])
    buckets = _bucket_tables()
    bias = _bias_tables(P["rel_bias"], buckets, "bias_tables")
    jobs0 = {"proj": (_gather_job(shards, rest0), rest0), "ssm": (_gather_job(shards, [(_W_IN, 1)]), [(_W_IN, 1)]),
             **{f"attn{k}": (_gather_job(shards, items), items) for k, items in enumerate(thirds1)}}
    x, res0, delivered = _layer_fwd(x, mem, W0, P, bias, 0, jobs0)
    W1 = {**delivered["ssm"], **delivered["attn0"], **delivered["attn1"], **delivered["attn2"]}
    x, res1, _ = _layer_fwd(x, mem, W1, P, bias, 1, {})
    loss, dx, dgf = _loss_head(x, P["final_norm_g"].reshape(1, D_MODEL), target, min(x.shape[0], ROW_TILE),
                                "loss_head")

    dx, _, gp1, dbias1, landed1 = _layer_bwd(
        dx, res1, P, bias, 1,
        {"dw_in": lambda small: _scatter_job(small, _SMALL, 1),
         "proj": lambda small, dw_in, landed: _scatter_job([dw_in], [_W_IN], 1)})
    rep1 = _pack_half({n: a[None] for n, a in gp1.items()}, 0, False)
    dx, _, gp0, dbias0, landed0 = _layer_bwd(
        dx, res0, P, bias, 0,
        {"merge": _rows_job(rep1, 0),
         "dw_in": lambda small: _scatter_job(small, _SMALL, 0, parts=landed1["dw_in"]),
         "proj": lambda small, dw_in, landed: _scatter_job([dw_in], [_W_IN], 0, parts=landed1["proj"])})
    d_rel = _bias_grad(dbias0, dbias1, buckets, "bias_grad")
    gp0 = {n: a[None] for n, a in gp0.items()}
    gp0["rel_bias"] = jnp.sum(d_rel, axis=-1).transpose(2, 0, 1).reshape(NUM_BUCKETS, 12)
    gp0["final_norm_g"] = dgf[0]
    rep0 = _pack_half(gp0, 0, True)
    rparts = _gather_via_sibling(rep0, lambda ref: ref, lambda ref, s: ref.at[s, pl.ds(_REP_HALF_ROWS, _REP_HALF_ROWS)],
                                 jax.ShapeDtypeStruct((N_DEV, _REP_ROWS, LANES), F32), "gather_small_grads0",
                                 landing=landed0["merge"][0])
    return loss[0, 0], dx, list(landed0["proj"]) + list(landed0["dw_in"]), rparts


_WEIGHTS = ["norm_g", "mem_norm_g", "w_in", "b_gate", "ssm_lambda_re", "ssm_lambda_im", "ssm_log_dt", "ssm_b_re",
            "ssm_b_im", "ssm_c_re", "ssm_c_im", "ssm_d", "w_glu", "b_glu", "w_mem_kv", "w_br_ssm", "w_br_attn",
            "w_br_mem", "w_out", "rel_bias", "final_norm_g"]
_ADAM_ROWS = {"w_in": 136,"w_glu": 96, "w_mem_kv": 128, "w_br_ssm": 768, "w_br_attn": 768, "w_br_mem": 512,
              "w_out": 128}


def kernel(x, mem, norm_g, mem_norm_g, w_in, b_gate, ssm_lambda_re, ssm_lambda_im, ssm_log_dt, ssm_b_re, ssm_b_im, ssm_c_re, ssm_c_im, ssm_d, w_glu, b_glu, w_mem_kv, w_br_ssm, w_br_attn, w_br_mem, w_out, rel_bias, final_norm_g, loss_target, m_norm_g, m_mem_norm_g, m_w_in, m_b_gate, m_ssm_lambda_re, m_ssm_lambda_im, m_ssm_log_dt, m_ssm_b_re, m_ssm_b_im, m_ssm_c_re, m_ssm_c_im, m_ssm_d, m_w_glu, m_b_glu, m_w_mem_kv, m_w_br_ssm, m_w_br_attn, m_w_br_mem, m_w_out, m_rel_bias, m_final_norm_g, v_norm_g, v_mem_norm_g, v_w_in, v_b_gate, v_ssm_lambda_re, v_ssm_lambda_im, v_ssm_log_dt, v_ssm_b_re, v_ssm_b_im, v_ssm_c_re, v_ssm_c_im, v_ssm_d, v_w_glu, v_b_glu, v_w_mem_kv, v_w_br_ssm, v_w_br_attn, v_w_br_mem, v_w_out, v_rel_bias, v_final_norm_g):
    given = dict(locals())
    w = {n: given[n] for n in _WEIGHTS}
    m = {n: given["m_" + n] for n in _WEIGHTS}
    v = {n: given["v_" + n] for n in _WEIGHTS}

    turned = lambda n, a: a.swapaxes(1, 2) if n == "w_in" else a
    shards = [turned(n, w[n]).astype(BF16) for n, _, _ in _SHARDED]
    loss, dx, parts, rparts = _train_step(x[0], mem[0], loss_target[0], shards, w)
    loss = lax.psum(loss, ("x", "y", "c"))

    new = {}
    for (n, _, _), p in zip(_SHARDED, parts):
        new[n] = [turned(n, a) for a in _adamw(p, turned(n, w[n]), turned(n, m[n]), turned(n, v[n]), _ADAM_ROWS[n],
                                               f"adamw_{n}")]
    rp = [_unpack_replicated(a) for a in _adamw(rparts[:, None], _pack_replicated(w), _pack_replicated(m),
                                                _pack_replicated(v), _REP_ROWS // 4, "adamw_replicated")]
    for n, _ in _REPLICATED:
        new[n] = [rp[kind][n] for kind in range(4)]
    outs = [loss, dx[None]]
    for kind in range(4):
        outs.extend(new[n][kind] for n in _WEIGHTS)
    return tuple(outs)
```

```python
import functools
import math
from typing import Callable, NamedTuple

import jax
import jax.numpy as jnp
import numpy as np
from jax import lax
from jax.experimental import pallas as pl
from jax.experimental.pallas import tpu as pltpu

F32 = jnp.float32
BF16 = jnp.bfloat16

D_MODEL = 1024
DEPTH = 2
EPS = 1e-6
D_SSM = 768
SSM_GROUP = 16
SSM_GROUPS = 48
SSM_STATE = 64
N_STATE = SSM_GROUPS * SSM_STATE
SSM_BLOCKS = 6
D_ATTN = 768
ATTN_HEAD_DIM = 64
ATTN_GROUP_WIDTH = 256
ATTN_DILATIONS = (1, 4, 16)
ATTN_SPAN = 128
ATTN_BLOCK = 128
NUM_BUCKETS = 32
REL_MAX_DISTANCE = 2048
NEG_INF = -1e30
MEM_HEADS = 4
MEM_HEAD_DIM = 128
D_MEM = 512
N_GATES = 3 * D_MODEL
D_IN = 8704
N_DEV = 8
LANES = 128
ADAM_LR = 0.001
ADAM_B1 = 0.9
ADAM_B2 = 0.999
ADAM_EPS = 1e-08
ADAM_WD = 0.01
ADAM_STEP = 10

_OFF = {"u": 0, "z_ssm": 768, "q": 1536, "k": 2304, "v": 3072, "z_attn": 3840, "q_mem": 4608, "z_mem": 5120,
        "gates": 5632}
GATE_BLOCK = 512
ROW_TILE = 1024

NN = (((1,), (0,)), ((), ()))
NT = (((1,), (1,)), ((), ()))
TN = (((0,), (0,)), ((), ()))

VMEM_LIMIT = 56 * 1024 * 1024


def _dot(a, b, dims=NN):
    return lax.dot_general(a, b, dims, preferred_element_type=F32)


def _sigmoid(x):
    return 1.0 / (1.0 + jnp.exp(-x))


def _gelu_parts(x):
    k = math.sqrt(2.0 / math.pi)
    t = jnp.tanh(k * (x + 0.044715 * (x * x * x)))
    cdf = 0.5 * (1.0 + t)
    dcdf = 0.5 * (1.0 - t * t) * k * (1.0 + 3.0 * 0.044715 * (x * x))
    return x * cdf, cdf + x * dcdf


def _params(sem, vmem=VMEM_LIMIT):
    return pltpu.CompilerParams(dimension_semantics=sem, vmem_limit_bytes=vmem)


def _full(shape):
    return pl.BlockSpec(shape, lambda *_: (0,) * len(shape))


def _norm_proj(x, g, w, tm, tn, name, out_dtype=F32, job=None, w_turned=False):
    T, D = x.shape
    N = w.shape[0] if w_turned else w.shape[1]
    w_spec = pl.BlockSpec((tn, D), lambda i, j: (j, 0)) if w_turned else pl.BlockSpec((D, tn), lambda i, j: (0, j))
    dims = NT if w_turned else NN

    def body(x_ref, g_ref, w_ref, o_ref, h_ref, hs):
        @pl.when(pl.program_id(1) == 0)
        def _():
            xv = x_ref[...]
            r = lax.rsqrt(jnp.mean(xv * xv, axis=-1, keepdims=True) + EPS)
            hv = (xv * r * g_ref[...]).astype(BF16)
            hs[...] = hv
            h_ref[...] = hv

        o_ref[...] = _dot(hs[...], w_ref[...], dims).astype(out_dtype)

    return _pc(
        body, job, name=name, grid=(T // tm, N // tn),
        in_specs=[pl.BlockSpec((tm, D), lambda i, j: (i, 0)), _full((1, D)), w_spec],
        out_specs=[pl.BlockSpec((tm, tn), lambda i, j: (i, j)), pl.BlockSpec((tm, D), lambda i, j: (i, 0))],
        out_shape=[jax.ShapeDtypeStruct((T, N), out_dtype), jax.ShapeDtypeStruct((T, D), BF16)],
        scratch_shapes=[pltpu.VMEM((tm, D), BF16)], sem=("parallel", "arbitrary"), operands=(x, g, w))


def _mm_tn(a, b, tm, tn, tk, name, b_col=0, n=None, job=None):
    K, M = a.shape
    N = b.shape[1] if n is None else n
    nk = K // tk
    j0 = b_col // tn

    def body(a_ref, b_ref, o_ref, acc):
        k = pl.program_id(2)

        @pl.when(k == 0)
        def _():
            acc[...] = jnp.zeros_like(acc)

        acc[...] += _dot(a_ref[...].astype(BF16), b_ref[...].astype(BF16), TN)

        @pl.when(k == nk - 1)
        def _():
            o_ref[...] = acc[...].astype(BF16)

    out = _pc(
        body, job, name=name, grid=(M // tm, N // tn, nk),
        in_specs=[pl.BlockSpec((tk, tm), lambda i, j, k: (k, i)), pl.BlockSpec((tk, tn), lambda i, j, k: (k, j0 + j))],
        out_specs=[pl.BlockSpec((tm, tn), lambda i, j, k: (i, j))],
        out_shape=[jax.ShapeDtypeStruct((M, N), BF16)],
        scratch_shapes=[pltpu.VMEM((tm, tn), F32)], sem=("parallel", "parallel", "arbitrary"), operands=(a, b))
    return out[0] if job is None else (out[0][0], out[1])


def _proj_bwd(dp, w, x, g, dres, tm, tk, name, job=None, w_turned=False):
    T, N = dp.shape
    D = x.shape[1]
    nk = N // tk
    w_spec = pl.BlockSpec((tk, D), lambda i, k: (k, 0)) if w_turned else pl.BlockSpec((D, tk), lambda i, k: (0, k))
    dims = NN if w_turned else NT

    def body(dp_ref, w_ref, x_ref, g_ref, dres_ref, dx_ref, dg_ref, acc):
        i, k = pl.program_id(0), pl.program_id(1)

        @pl.when(k == 0)
        def _():
            acc[...] = jnp.zeros_like(acc)

        @pl.when((i == 0) & (k == 0))
        def _():
            dg_ref[...] = jnp.zeros_like(dg_ref)

        acc[...] += _dot(dp_ref[...], w_ref[...], dims)

        @pl.when(k == nk - 1)
        def _():
            xv = x_ref[...]
            dh = acc[...]
            r = lax.rsqrt(jnp.mean(xv * xv, axis=-1, keepdims=True) + EPS)
            xr = xv * r
            dg_ref[...] += jnp.sum(dh * xr, axis=0, keepdims=True)
            wv = dh * g_ref[...]
            dx_ref[...] = dres_ref[...] + r * (wv - xr * jnp.mean(wv * xr, axis=-1, keepdims=True))

    return _pc(
        body, job, name=name, grid=(T // tm, nk),
        in_specs=[pl.BlockSpec((tm, tk), lambda i, k: (i, k)), w_spec,
                  pl.BlockSpec((tm, D), lambda i, k: (i, 0)), _full((1, D)),
                  pl.BlockSpec((tm, D), lambda i, k: (i, 0))],
        out_specs=[pl.BlockSpec((tm, D), lambda i, k: (i, 0)), _full((1, D))],
        out_shape=[jax.ShapeDtypeStruct((T, D), F32), jax.ShapeDtypeStruct((1, D), F32)],
        scratch_shapes=[pltpu.VMEM((tm, D), F32)], sem=("arbitrary", "arbitrary"), operands=(dp, w, x, g, dres))


def _ssm_fwd(proj, bre, bim, cre, cimn, are, aim, d, wglu, bglu, tc, name, job=None):
    T = proj.shape[0]
    ucol, zcol = _OFF["u"] // D_SSM, _OFF["z_ssm"] // D_SSM

    def body(u_ref, z_ref, bre_ref, bim_ref, cre_ref, cim_ref, are_ref, aim_ref, d_ref, wg_ref, bg_ref,
             xr_ref, xi_ref, y_ref, o_ref, car_r, car_i):
        @pl.when(pl.program_id(0) == 0)
        def _():
            car_r[...] = jnp.zeros_like(car_r)
            car_i[...] = jnp.zeros_like(car_i)

        u = u_ref[...]
        ub = u.astype(BF16)
        for k in range(SSM_BLOCKS):
            uk = ub[:, 128 * k:128 * (k + 1)]
            xr_ref[:, 512 * k:512 * (k + 1)] = _dot(uk, bre_ref[k])
            xi_ref[:, 512 * k:512 * (k + 1)] = _dot(uk, bim_ref[k])
        ar, ai = are_ref[...], aim_ref[...]

        def step(t, c):
            pr, pi = c
            nr = ar * pr - ai * pi + xr_ref[pl.ds(t, 1), :]
            ni = ar * pi + ai * pr + xi_ref[pl.ds(t, 1), :]
            xr_ref[pl.ds(t, 1), :] = nr
            xi_ref[pl.ds(t, 1), :] = ni
            return nr, ni

        pr, pi = lax.fori_loop(0, tc, step, (car_r[...], car_i[...]))
        car_r[...] = pr
        car_i[...] = pi

        ys = []
        for k in range(SSM_BLOCKS):
            xrk = xr_ref[:, 512 * k:512 * (k + 1)].astype(BF16)
            xik = xi_ref[:, 512 * k:512 * (k + 1)].astype(BF16)
            ys.append(_dot(xrk, cre_ref[k]) + _dot(xik, cim_ref[k]))
        y = jnp.concatenate(ys, axis=1) + d_ref[...] * u
        y_ref[...] = y
        gl, _ = _gelu_parts(y)
        t = _dot(gl.astype(BF16), wg_ref[...]) + bg_ref[...]
        z = z_ref[...]
        o_ref[...] = (gl * _sigmoid(t) * (z * _sigmoid(z))).astype(BF16)

    return _pc(
        body, job, name=name, grid=(T // tc,),
        in_specs=[pl.BlockSpec((tc, D_SSM), lambda i: (i, ucol)), pl.BlockSpec((tc, D_SSM), lambda i: (i, zcol)),
                  _full((SSM_BLOCKS, 128, 512)), _full((SSM_BLOCKS, 128, 512)),
                  _full((SSM_BLOCKS, 512, 128)), _full((SSM_BLOCKS, 512, 128)),
                  _full((1, N_STATE)), _full((1, N_STATE)), _full((1, D_SSM)),
                  _full((D_SSM, D_SSM)), _full((1, D_SSM))],
        out_specs=[pl.BlockSpec((tc, N_STATE), lambda i: (i, 0)), pl.BlockSpec((tc, N_STATE), lambda i: (i, 0)),
                   pl.BlockSpec((tc, D_SSM), lambda i: (i, 0)), pl.BlockSpec((tc, D_SSM), lambda i: (i, 0))],
        out_shape=[jax.ShapeDtypeStruct((T, N_STATE), F32), jax.ShapeDtypeStruct((T, N_STATE), F32),
                   jax.ShapeDtypeStruct((T, D_SSM), F32), jax.ShapeDtypeStruct((T, D_SSM), BF16)],
        scratch_shapes=[pltpu.VMEM((1, N_STATE), F32), pltpu.VMEM((1, N_STATE), F32)], sem=("arbitrary",),
        operands=(proj, proj, bre, bim, cre, cimn, are, aim, d, wglu, bglu))


def _glu_bwd(do, y, proj, wglu, bglu, dproj, tm, name):
    T = y.shape[0]
    zcol = _OFF["z_ssm"] // D_SSM

    def body(do_ref, y_ref, z_ref, wg_ref, bg_ref, _, dy_ref, dz_ref, g_ref, dt_ref, db_ref):
        @pl.when(pl.program_id(0) == 0)
        def _():
            db_ref[...] = jnp.zeros_like(db_ref)

        dov = do_ref[...]
        gl, dgl = _gelu_parts(y_ref[...])
        glb = gl.astype(BF16)
        sg = _sigmoid(_dot(glb, wg_ref[...]) + bg_ref[...])
        z = z_ref[...]
        sz = _sigmoid(z)
        dz_ref[...] = (dov * (gl * sg) * (sz * (1.0 + z * (1.0 - sz)))).astype(BF16)
        dy2 = dov * (z * sz)
        dt = dy2 * gl * (sg * (1.0 - sg))
        dtb = dt.astype(BF16)
        dg = dy2 * sg + _dot(dtb, wg_ref[...], NT)
        dy_ref[...] = dg * dgl
        g_ref[...] = glb
        dt_ref[...] = dtb
        db_ref[...] += jnp.sum(dt, axis=0, keepdims=True)

    row = lambda i: (i, 0)
    return pl.pallas_call(
        body, name=name, grid=(T // tm,),
        in_specs=[pl.BlockSpec((tm, D_SSM), row), pl.BlockSpec((tm, D_SSM), row),
                  pl.BlockSpec((tm, D_SSM), lambda i: (i, zcol)), _full((D_SSM, D_SSM)), _full((1, D_SSM)),
                  pl.BlockSpec(memory_space=pl.ANY)],
        out_specs=[pl.BlockSpec((tm, D_SSM), row), pl.BlockSpec((tm, D_SSM), lambda i: (i, zcol)),
                   pl.BlockSpec((tm, D_SSM), row), pl.BlockSpec((tm, D_SSM), row), _full((1, D_SSM))],
        out_shape=[jax.ShapeDtypeStruct((T, D_SSM), F32), jax.ShapeDtypeStruct(dproj.shape, BF16),
                   jax.ShapeDtypeStruct((T, D_SSM), BF16), jax.ShapeDtypeStruct((T, D_SSM), BF16),
                   jax.ShapeDtypeStruct((1, D_SSM), F32)],
        input_output_aliases={5: 1},
        compiler_params=_params(("arbitrary",)),
    )(do, y, proj, wglu, bglu, dproj)


def _ssm_bwd(dy, proj, xr, xi, bre, bim, cre, cimn, are, aim, d, dproj, tc, name, job=None):
    T = dy.shape[0]
    nc = T // tc
    ucol = _OFF["u"] // D_SSM
    rb = tc // 8

    def body(dy_ref, u_ref, xr_ref, xi_ref, xpr_ref, xpi_ref, bre_ref, bim_ref, cre_ref, cim_ref,
             are_ref, aim_ref, d_ref, _,
             du_ref, dbre_ref, dbim_ref, dcre_ref, dcim_ref, dare_ref, daim_ref, dd_ref, gr, gi, car_r, car_i):
        i = pl.program_id(0)

        @pl.when(i == 0)
        def _():
            for ref in (car_r, car_i, dbre_ref, dbim_ref, dcre_ref, dcim_ref, dare_ref, daim_ref, dd_ref):
                ref[...] = jnp.zeros_like(ref)

        dyv = dy_ref[...]
        dyb = dyv.astype(BF16)
        u = u_ref[...]
        ub = u.astype(BF16)
        for k in range(SSM_BLOCKS):
            dk = dyb[:, 128 * k:128 * (k + 1)]
            gr[:, 512 * k:512 * (k + 1)] = _dot(dk, cre_ref[k], NT)
            gi[:, 512 * k:512 * (k + 1)] = _dot(dk, cim_ref[k], NT)
        ar, ai = are_ref[...], aim_ref[...]

        def step(s, c):
            pr, pi = c
            t = tc - 1 - s
            nr = gr[pl.ds(t, 1), :] + ar * pr + ai * pi
            ni = gi[pl.ds(t, 1), :] + ar * pi - ai * pr
            gr[pl.ds(t, 1), :] = nr
            gi[pl.ds(t, 1), :] = ni
            return nr, ni

        pr, pi = lax.fori_loop(0, tc, step, (car_r[...], car_i[...]))
        car_r[...] = pr
        car_i[...] = pi

        keep = jnp.where(i == nc - 1, 0.0, 1.0)
        row0 = lax.broadcasted_iota(jnp.int32, (tc, 1), 0) == 0
        dd_ref[...] += jnp.sum(dyv * u, axis=0, keepdims=True)
        for k in range(SSM_BLOCKS):
            sl = slice(512 * k, 512 * (k + 1))
            ch = slice(128 * k, 128 * (k + 1))
            xrk, xik, grk, gik = xr_ref[:, sl], xi_ref[:, sl], gr[:, sl], gi[:, sl]
            xsr = jnp.where(row0, xpr_ref[7:8, sl] * keep, pltpu.roll(xrk, 1, axis=0))
            xsi = jnp.where(row0, xpi_ref[7:8, sl] * keep, pltpu.roll(xik, 1, axis=0))
            dare_ref[:, sl] += jnp.sum(grk * xsr + gik * xsi, axis=0, keepdims=True)
            daim_ref[:, sl] += jnp.sum(gik * xsr - grk * xsi, axis=0, keepdims=True)
            grb, gib = grk.astype(BF16), gik.astype(BF16)
            du_ref[:, ch] = (_dot(grb, bre_ref[k], NT) + _dot(gib, bim_ref[k], NT)
                             + d_ref[:, ch] * dyv[:, ch]).astype(BF16)
            dbre_ref[k] += _dot(grb, ub[:, ch], TN)
            dbim_ref[k] += _dot(gib, ub[:, ch], TN)
            dcre_ref[k] += _dot(dyb[:, ch], xrk.astype(BF16), TN)
            dcim_ref[k] -= _dot(dyb[:, ch], xik.astype(BF16), TN)

    rev = lambda i: (nc - 1 - i, 0)
    prev = lambda i: (jnp.maximum((nc - 1 - i) * rb - 1, 0), 0)
    return _pc(
        body, job, name=name, grid=(nc,),
        in_specs=[pl.BlockSpec((tc, D_SSM), rev), pl.BlockSpec((tc, D_SSM), lambda i: (nc - 1 - i, ucol)),
                  pl.BlockSpec((tc, N_STATE), rev), pl.BlockSpec((tc, N_STATE), rev),
                  pl.BlockSpec((8, N_STATE), prev), pl.BlockSpec((8, N_STATE), prev),
                  _full((SSM_BLOCKS, 128, 512)), _full((SSM_BLOCKS, 128, 512)),
                  _full((SSM_BLOCKS, 512, 128)), _full((SSM_BLOCKS, 512, 128)),
                  _full((1, N_STATE)), _full((1, N_STATE)), _full((1, D_SSM)), pl.BlockSpec(memory_space=pl.ANY)],
        out_specs=[pl.BlockSpec((tc, D_SSM), lambda i: (nc - 1 - i, ucol)),
                   _full((SSM_BLOCKS, 512, 128)), _full((SSM_BLOCKS, 512, 128)),
                   _full((SSM_BLOCKS, 128, 512)), _full((SSM_BLOCKS, 128, 512)),
                   _full((1, N_STATE)), _full((1, N_STATE)), _full((1, D_SSM))],
        out_shape=[jax.ShapeDtypeStruct(dproj.shape, BF16),
                   jax.ShapeDtypeStruct((SSM_BLOCKS, 512, 128), F32), jax.ShapeDtypeStruct((SSM_BLOCKS, 512, 128), F32),
                   jax.ShapeDtypeStruct((SSM_BLOCKS, 128, 512), F32), jax.ShapeDtypeStruct((SSM_BLOCKS, 128, 512), F32),
                   jax.ShapeDtypeStruct((1, N_STATE), F32), jax.ShapeDtypeStruct((1, N_STATE), F32),
                   jax.ShapeDtypeStruct((1, D_SSM), F32)],
        scratch_shapes=[pltpu.VMEM((tc, N_STATE), F32), pltpu.VMEM((tc, N_STATE), F32),
                        pltpu.VMEM((1, N_STATE), F32), pltpu.VMEM((1, N_STATE), F32)], sem=("arbitrary",),
        operands=(dy, proj, xr, xi, xr, xi, bre, bim, cre, cimn, are, aim, d, dproj), aliases={13: 0})


def _rel_bucket(dist):
    n = jnp.maximum(dist, 0)
    max_exact = NUM_BUCKETS // 2
    n_f = jnp.maximum(n, 1).astype(F32)
    large = max_exact + (jnp.log(n_f / max_exact) / math.log(REL_MAX_DISTANCE / max_exact)
                         * (NUM_BUCKETS - max_exact)).astype(jnp.int32)
    large = jnp.minimum(large, NUM_BUCKETS - 1)
    return jnp.where(n < max_exact, n, large)


def _bucket_tables():
    qi = jnp.arange(ATTN_BLOCK)[:, None]
    kj = jnp.arange(2 * ATTN_BLOCK)[None, :]
    delta = jnp.maximum(ATTN_BLOCK + qi - kj, 0)
    return jnp.stack([_rel_bucket(delta * r) for r in ATTN_DILATIONS]).astype(jnp.int32)


def _bias_tables(rel_bias, buckets, name):
    def body(tab_ref, bk_ref, o_ref):
        g = pl.program_id(0)
        bk = bk_ref[...]
        qi = lax.broadcasted_iota(jnp.int32, bk.shape, 0)
        kj = lax.broadcasted_iota(jnp.int32, bk.shape, 1)
        delta = ATTN_BLOCK + qi - kj
        band = (delta >= 0) & (delta <= ATTN_SPAN)
        accs = [jnp.zeros(bk.shape, F32) for _ in range(4)]
        for b in range(NUM_BUCKETS):
            hit = bk == b
            for h in range(4):
                accs[h] = jnp.where(hit, tab_ref[b, 4 * g + h], accs[h])
        for h in range(4):
            o_ref[h] = jnp.where(band, accs[h], NEG_INF)

    return pl.pallas_call(
        body, name=name, grid=(3,),
        in_specs=[pl.BlockSpec(memory_space=pltpu.SMEM),
                  pl.BlockSpec((None, ATTN_BLOCK, 2 * ATTN_BLOCK), lambda g: (g, 0, 0))],
        out_specs=pl.BlockSpec((None, 4, ATTN_BLOCK, 2 * ATTN_BLOCK), lambda g: (g, 0, 0, 0)),
        out_shape=jax.ShapeDtypeStruct((3, 4, ATTN_BLOCK, 2 * ATTN_BLOCK), F32),
        compiler_params=_params(("parallel",)),
    )(rel_bias, buckets)


def _bias_grad(db0, db1, buckets, name):
    def body(a_ref, b_ref, bk_ref, o_ref):
        bk = bk_ref[...]
        for h in range(4):
            dv = a_ref[h] + b_ref[h]
            for b in range(NUM_BUCKETS):
                o_ref[h, b:b + 1, :] = jnp.sum(jnp.where(bk == b, dv, 0.0), axis=0, keepdims=True)

    tab = pl.BlockSpec((None, 4, ATTN_BLOCK, 2 * ATTN_BLOCK), lambda g: (g, 0, 0, 0))
    return pl.pallas_call(
        body, name=name, grid=(3,),
        in_specs=[tab, tab, pl.BlockSpec((None, ATTN_BLOCK, 2 * ATTN_BLOCK), lambda g: (g, 0, 0))],
        out_specs=pl.BlockSpec((None, 4, NUM_BUCKETS, 2 * ATTN_BLOCK), lambda g: (g, 0, 0, 0)),
        out_shape=jax.ShapeDtypeStruct((3, 4, NUM_BUCKETS, 2 * ATTN_BLOCK), F32),
        compiler_params=_params(("parallel",)),
    )(db0, db1, buckets)


_ATTN_SUB = {1: 16, 4: 4, 16: 1}
_UNROLL = 4
_STATIC_UNITS = 16


def _unit_rows(j, s, r):
    start = j * ATTN_BLOCK * r + s
    return pl.ds(start, ATTN_BLOCK, stride=r) if r > 1 else pl.ds(start, ATTN_BLOCK)


def _for_units(r, nsub, fn, after):
    if r * nsub <= _STATIC_UNITS:
        units = [(j, s) for j in range(nsub) for s in range(r)]
        for i in range(0, len(units), _UNROLL):
            after([fn(j, s) for j, s in units[i:i + _UNROLL]])
    else:
        assert nsub == 1
        def four(i, c):
            after([fn(0, _UNROLL * i + k) for k in range(_UNROLL)])
            return c

        lax.fori_loop(0, r // _UNROLL, four, 0)


def _attn_cols(g):
    return tuple((_OFF[n] + ATTN_GROUP_WIDTH * g) // LANES for n in ("q", "k", "v"))


def _attn_fwd(proj, bias, g, name, job=None):
    r = ATTN_DILATIONS[g]
    nsub = _ATTN_SUB[r]
    T = proj.shape[0]
    sub = ATTN_BLOCK * r
    tb = sub * nsub
    qc, kc, vc = _attn_cols(g)
    scale = ATTN_HEAD_DIM ** -0.5

    def body(q_ref, kc_ref, kp_ref, vc_ref, vp_ref, bias_ref, o_ref, lse_ref):
        lane = lax.broadcasted_iota(jnp.int32, (ATTN_BLOCK, LANES), 1)
        kj = lax.broadcasted_iota(jnp.int32, (ATTN_BLOCK, 2 * ATTN_BLOCK), 1)
        dead = (pl.program_id(0) == 0) & (kj < ATTN_BLOCK)

        def one(j, s):
            rows = _unit_rows(j, s, r)
            before = _unit_rows(max(j - 1, 0), s, r)
            k_before = kc_ref[before, :] if j else kp_ref[before, :]
            v_before = vc_ref[before, :] if j else vp_ref[before, :]
            q = q_ref[rows, :]
            kcat = jnp.concatenate([k_before, kc_ref[rows, :]], axis=0).astype(BF16)
            vcat = jnp.concatenate([v_before, vc_ref[rows, :]], axis=0).astype(BF16)
            o_acc = jnp.zeros((ATTN_BLOCK, LANES), F32)
            l_acc = jnp.zeros((ATTN_BLOCK, LANES), F32)
            for hh in range(2):
                mine = (lane >= ATTN_HEAD_DIM) if hh else (lane < ATTN_HEAD_DIM)
                qm = jnp.where(mine, q, 0.0).astype(BF16)
                sc = _dot(qm, kcat, NT) * scale + bias_ref[hh]
                if j == 0:
                    sc = jnp.where(dead, NEG_INF, sc)
                m = jnp.max(sc, axis=-1, keepdims=True)
                p = jnp.exp(sc - m)
                l = jnp.sum(p, axis=-1, keepdims=True)
                o_acc = jnp.where(mine, _dot((p / l).astype(BF16), vcat), o_acc)
                l_acc = jnp.where(mine, m + jnp.log(l), l_acc)
            o_ref[rows, :] = o_acc
            lse_ref[rows, :] = l_acc

        _for_units(r, nsub, one, lambda results: None)

    cur = lambda c: pl.BlockSpec((tb, LANES), lambda b, p: (b, c + p))
    prev = lambda c: pl.BlockSpec((sub, LANES), lambda b, p: (jnp.maximum(b * nsub - 1, 0), c + p))
    out = pl.BlockSpec((tb, LANES), lambda b, p: (b, p))
    return _pc(
        body, job, name=name, grid=(T // tb, 2),
        in_specs=[cur(qc), cur(kc), prev(kc), cur(vc), prev(vc),
                  pl.BlockSpec((2, ATTN_BLOCK, 2 * ATTN_BLOCK), lambda b, p: (p, 0, 0))],
        out_specs=[out, out],
        out_shape=[jax.ShapeDtypeStruct((T, ATTN_GROUP_WIDTH), F32), jax.ShapeDtypeStruct((T, ATTN_GROUP_WIDTH), F32)],
        scratch_shapes=[], sem=("parallel", "parallel"), operands=(proj, proj, proj, proj, proj, bias))


def _attn_bwd(proj, do, corr, lse, bias, dproj, g, name):
    r = ATTN_DILATIONS[g]
    nsub = _ATTN_SUB[r]
    T = proj.shape[0]
    sub = ATTN_BLOCK * r
    tb = sub * nsub
    nb = T // tb
    qc, kc, vc = _attn_cols(g)
    dc = ATTN_GROUP_WIDTH * g // LANES
    scale = ATTN_HEAD_DIM ** -0.5

    def body(q_ref, kc_ref, kp_ref, vc_ref, vp_ref, do_ref, corr_ref, lse_ref, bias_ref, _,
             dproj_ref, db_ref, dq_s, dkc_s, dkp_s, dvc_s, dvp_s, kacc, vacc, stage, stage_sems):
        p, b = pl.program_id(0), pl.program_id(1)

        def to_dproj(e, slot, block, col):
            rows = pl.ds(pl.multiple_of(block * tb, tb), tb)
            cols = pl.ds(pl.multiple_of((col + p) * LANES, LANES), LANES)
            return pltpu.make_async_copy(stage.at[e, slot], dproj_ref.at[rows, cols], stage_sems.at[e, slot])

        def fill(e, block, col, value):
            count = p * nb + block
            slot = count % 2

            @pl.when(count >= 2)
            def _():
                to_dproj(e, slot, 0, col).wait()

            stage[e, slot] = value.astype(BF16)

        def send(e, block, col):
            to_dproj(e, (p * nb + block) % 2, block, col).start()

        def fill_keys(block):
            fill(1, block, kc, kacc[...])
            fill(2, block, vc, vacc[...])

        def send_keys(block):
            send(1, block, kc)
            send(2, block, vc)

        def emit_keys(block):
            fill_keys(block)
            send_keys(block)

        @pl.when(b == 0)
        def _():
            db_ref[...] = jnp.zeros_like(db_ref)
            kacc[...] = jnp.zeros_like(kacc)
            vacc[...] = jnp.zeros_like(vacc)

        @pl.when(b == nb)
        def _():
            emit_keys(nb - 1)

        @pl.when((b == nb) & (p == 1))
        def _():
            for e, col in enumerate((qc, kc, vc)):
                for slot in range(2):
                    to_dproj(e, slot, 0, col).wait()

        @pl.when(b < nb)
        def _():
            lane = lax.broadcasted_iota(jnp.int32, (ATTN_BLOCK, LANES), 1)
            kj = lax.broadcasted_iota(jnp.int32, (ATTN_BLOCK, 2 * ATTN_BLOCK), 1)
            dead = (b == 0) & (kj < ATTN_BLOCK)

            def one(j, s):
                rows = _unit_rows(j, s, r)
                before = _unit_rows(max(j - 1, 0), s, r)
                k_before = kc_ref[before, :] if j else kp_ref[before, :]
                v_before = vc_ref[before, :] if j else vp_ref[before, :]
                q = q_ref[rows, :]
                kcat = jnp.concatenate([k_before, kc_ref[rows, :]], axis=0).astype(BF16)
                vcat = jnp.concatenate([v_before, vc_ref[rows, :]], axis=0).astype(BF16)
                dov, corrv, lsev = do_ref[rows, :], corr_ref[rows, :], lse_ref[rows, :]
                dq_acc = jnp.zeros((ATTN_BLOCK, LANES), F32)
                dk_acc = jnp.zeros((2 * ATTN_BLOCK, LANES), F32)
                dv_acc = jnp.zeros((2 * ATTN_BLOCK, LANES), F32)
                dss = []
                for hh in range(2):
                    mine = (lane >= ATTN_HEAD_DIM) if hh else (lane < ATTN_HEAD_DIM)
                    col = slice(ATTN_HEAD_DIM * hh, ATTN_HEAD_DIM * hh + 1)
                    qm = jnp.where(mine, q, 0.0).astype(BF16)
                    dom = jnp.where(mine, dov, 0.0).astype(BF16)
                    sc = _dot(qm, kcat, NT) * scale + bias_ref[hh]
                    if j == 0:
                        sc = jnp.where(dead, NEG_INF, sc)
                    p = jnp.exp(sc - lsev[:, col])
                    ds = p * (_dot(dom, vcat, NT) - corrv[:, col])
                    dss.append(ds)
                    dsb = ds.astype(BF16)
                    dq_acc = jnp.where(mine, _dot(dsb, kcat) * scale, dq_acc)
                    dk_acc += _dot(dsb, qm, TN) * scale
                    dv_acc += _dot(p.astype(BF16), dom, TN)
                dq_s[rows, :] = dq_acc
                dkp_s[rows, :] = dk_acc[:ATTN_BLOCK]
                dkc_s[rows, :] = dk_acc[ATTN_BLOCK:]
                dvp_s[rows, :] = dv_acc[:ATTN_BLOCK]
                dvc_s[rows, :] = dv_acc[ATTN_BLOCK:]
                return dss

            def add_bias_grads(results):
                for hh in range(2):
                    db_ref[hh] += functools.reduce(lambda x, y: x + y, [dss[hh] for dss in results])

            _for_units(r, nsub, one, add_bias_grads)
            fill(0, b, qc, dq_s[...])
            tail = slice((nsub - 1) * sub, nsub * sub)
            kacc[tail, :] += dkp_s[0:sub, :]
            vacc[tail, :] += dvp_s[0:sub, :]

            @pl.when(b >= 1)
            def _():
                fill_keys(b - 1)

            for acc, before_s, cur_s in ((kacc, dkp_s, dkc_s), (vacc, dvp_s, dvc_s)):
                acc[...] = cur_s[...]
                for j in range(nsub - 1):
                    acc[j * sub:(j + 1) * sub, :] += before_s[(j + 1) * sub:(j + 2) * sub, :]
            send(0, b, qc)

            @pl.when(b >= 1)
            def _():
                send_keys(b - 1)

    last = nb - 1
    blk = (tb, LANES)
    cur = lambda c: pl.BlockSpec(blk, lambda p, b: (jnp.minimum(b, last), c + p))
    before = lambda c: pl.BlockSpec((sub, LANES), lambda p, b: (jnp.clip(b * nsub - 1, 0, nb * nsub - 1), c + p))
    tab = pl.BlockSpec((2, ATTN_BLOCK, 2 * ATTN_BLOCK), lambda p, b: (p, 0, 0))
    hbm = pl.BlockSpec(memory_space=pl.ANY)
    return pl.pallas_call(
        body, name=name, grid=(2, nb + 1),
        in_specs=[cur(qc), cur(kc), before(kc), cur(vc), before(vc), cur(dc), cur(dc), cur(0), tab, hbm],
        out_specs=[hbm, tab],
        out_shape=[jax.ShapeDtypeStruct(dproj.shape, BF16), jax.ShapeDtypeStruct((4, ATTN_BLOCK, 2 * ATTN_BLOCK), F32)],
        input_output_aliases={9: 0},
        scratch_shapes=[pltpu.VMEM(blk, F32)] * 7 + [pltpu.VMEM((3, 2) + blk, BF16), pltpu.SemaphoreType.DMA((3, 2))],
        compiler_params=_params(("arbitrary", "arbitrary")),
    )(proj, proj, proj, proj, proj, do, corr, lse, bias, dproj)


def _mix_weights(lses):
    m = jnp.maximum(jnp.maximum(lses[0], lses[1]), lses[2])
    es = [jnp.exp(l - m) for l in lses]
    inv = 1.0 / (es[0] + es[1] + es[2])
    return jnp.concatenate([e * inv for e in es], axis=1)


def _attn_mix(os, lses, proj, tm, name):
    T = proj.shape[0]
    zcol = _OFF["z_attn"] // D_ATTN

    def body(o0, o1, o2, l0, l1, l2, z_ref, out_ref):
        z = z_ref[...]
        o = jnp.concatenate([o0[...], o1[...], o2[...]], axis=1)
        alpha = _mix_weights([l0[...], l1[...], l2[...]])
        out_ref[...] = (o * alpha * (z * _sigmoid(z))).astype(BF16)

    row = lambda i: (i, 0)
    grp = pl.BlockSpec((tm, ATTN_GROUP_WIDTH), row)
    return pl.pallas_call(
        body, name=name, grid=(T // tm,),
        in_specs=[grp] * 6 + [pl.BlockSpec((tm, D_ATTN), lambda i: (i, zcol))],
        out_specs=pl.BlockSpec((tm, D_ATTN), row),
        out_shape=jax.ShapeDtypeStruct((T, D_ATTN), BF16),
        compiler_params=_params(("parallel",)),
    )(*os, *lses, proj)


def _attn_mix_bwd(d, os, lses, proj, dproj, tm, name):
    T = proj.shape[0]
    zcol = _OFF["z_attn"] // D_ATTN

    def body(d_ref, o0, o1, o2, l0, l1, l2, z_ref, _, do_ref, corr_ref, dz_ref):
        dv, z = d_ref[...], z_ref[...]
        ov = jnp.concatenate([o0[...], o1[...], o2[...]], axis=1)
        alpha = _mix_weights([l0[...], l1[...], l2[...]])
        sz = _sigmoid(z)
        oc = ov * alpha
        dz_ref[...] = (dv * oc * (sz * (1.0 + z * (1.0 - sz)))).astype(BF16)
        doc = dv * (z * sz)
        do_ref[...] = doc * alpha
        pr = doc * oc
        p3 = pr[:, 0:256] + pr[:, 256:512] + pr[:, 512:768]
        li = lax.broadcasted_iota(jnp.int32, (256, 256), 0) // ATTN_HEAD_DIM
        lj = lax.broadcasted_iota(jnp.int32, (256, 256), 1) // ATTN_HEAD_DIM
        ones = jnp.where(li == lj, 1.0, 0.0).astype(F32)
        s = lax.dot_general(p3, ones, NN, precision=lax.Precision.HIGHEST, preferred_element_type=F32)
        corr_ref[...] = alpha * jnp.concatenate([s, s, s], axis=1)

    row = lambda i: (i, 0)
    grp = pl.BlockSpec((tm, ATTN_GROUP_WIDTH), row)
    return pl.pallas_call(
        body, name=name, grid=(T // tm,),
        in_specs=[pl.BlockSpec((tm, D_ATTN), row)] + [grp] * 6 + [pl.BlockSpec((tm, D_ATTN), lambda i: (i, zcol)),
                                                                    pl.BlockSpec(memory_space=pl.ANY)],
        out_specs=[pl.BlockSpec((tm, D_ATTN), row)] * 2 + [pl.BlockSpec((tm, D_ATTN), lambda i: (i, zcol))],
        out_shape=[jax.ShapeDtypeStruct((T, D_ATTN), F32), jax.ShapeDtypeStruct((T, D_ATTN), F32),
                   jax.ShapeDtypeStruct(dproj.shape, BF16)],
        input_output_aliases={8: 2},
        compiler_params=_params(("parallel",)),
    )(d, *os, *lses, proj, dproj)


def _mem_probs(q_ref, kv_ref, h):
    hs = slice(MEM_HEAD_DIM * h, MEM_HEAD_DIM * (h + 1))
    qh = q_ref[:, hs].astype(BF16)
    kh = kv_ref[:, hs]
    vh = kv_ref[:, D_MEM + MEM_HEAD_DIM * h:D_MEM + MEM_HEAD_DIM * (h + 1)]
    s = _dot(qh, kh, NT) * (MEM_HEAD_DIM ** -0.5)
    p = jnp.exp(s - jnp.max(s, axis=-1, keepdims=True))
    pn = p / jnp.sum(p, axis=-1, keepdims=True)
    return qh, kh, vh, pn


def _mem_fwd(proj, kv, tm, name):
    T = proj.shape[0]
    M = kv.shape[0]
    qcol, zcol = _OFF["q_mem"] // D_MEM, _OFF["z_mem"] // D_MEM

    def body(q_ref, z_ref, kv_ref, o_ref):
        outs = []
        for h in range(MEM_HEADS):
            _, _, vh, pn = _mem_probs(q_ref, kv_ref, h)
            outs.append(_dot(pn.astype(BF16), vh))
        z = z_ref[...]
        o_ref[...] = (jnp.concatenate(outs, axis=1) * (z * _sigmoid(z))).astype(BF16)

    return pl.pallas_call(
        body, name=name, grid=(T // tm,),
        in_specs=[pl.BlockSpec((tm, D_MEM), lambda i: (i, qcol)), pl.BlockSpec((tm, D_MEM), lambda i: (i, zcol)),
                  _full((M, 2 * D_MEM))],
        out_specs=pl.BlockSpec((tm, D_MEM), lambda i: (i, 0)),
        out_shape=jax.ShapeDtypeStruct((T, D_MEM), BF16),
        compiler_params=_params(("parallel",)),
    )(proj, proj, kv)


def _mem_bwd(d, proj, kv, tm, name):
    T = proj.shape[0]
    M = kv.shape[0]
    qcol, zcol = _OFF["q_mem"] // D_MEM, _OFF["z_mem"] // D_MEM

    def body(d_ref, q_ref, z_ref, kv_ref, dq_ref, dz_ref, dkv_ref):
        @pl.when(pl.program_id(0) == 0)
        def _():
            dkv_ref[...] = jnp.zeros_like(dkv_ref)

        z = z_ref[...]
        sz = _sigmoid(z)
        dv = d_ref[...]
        dov = dv * (z * sz)
        scale = MEM_HEAD_DIM ** -0.5
        outs, dqs = [], []
        for h in range(MEM_HEADS):
            hs = slice(MEM_HEAD_DIM * h, MEM_HEAD_DIM * (h + 1))
            qh, kh, vh, pn = _mem_probs(q_ref, kv_ref, h)
            pnb = pn.astype(BF16)
            oh = _dot(pnb, vh)
            outs.append(oh)
            doh = dov[:, hs]
            dohb = doh.astype(BF16)
            dp = _dot(dohb, vh, NT)
            ds = pn * (dp - jnp.sum(doh * oh, axis=-1, keepdims=True))
            dsb = ds.astype(BF16)
            dqs.append(_dot(dsb, kh) * scale)
            dkv_ref[:, hs] += _dot(dsb, qh, TN) * scale
            vs = slice(D_MEM + MEM_HEAD_DIM * h, D_MEM + MEM_HEAD_DIM * (h + 1))
            dkv_ref[:, vs] += _dot(pnb, dohb, TN)
        dq_ref[...] = jnp.concatenate(dqs, axis=1).astype(BF16)
        dz_ref[...] = (dv * jnp.concatenate(outs, axis=1) * (sz * (1.0 + z * (1.0 - sz)))).astype(BF16)

    row = lambda i: (i, 0)
    return pl.pallas_call(
        body, name=name, grid=(T // tm,),
        in_specs=[pl.BlockSpec((tm, D_MEM), row), pl.BlockSpec((tm, D_MEM), lambda i: (i, qcol)),
                  pl.BlockSpec((tm, D_MEM), lambda i: (i, zcol)), _full((M, 2 * D_MEM))],
        out_specs=[pl.BlockSpec((tm, D_MEM), row), pl.BlockSpec((tm, D_MEM), row), _full((M, 2 * D_MEM))],
        out_shape=[jax.ShapeDtypeStruct((T, D_MEM), BF16), jax.ShapeDtypeStruct((T, D_MEM), BF16),
                   jax.ShapeDtypeStruct((M, 2 * D_MEM), F32)],
        compiler_params=_params(("arbitrary",)),
    )(d, proj, proj, kv)


def _branches_and_gates(os_ref, oa_ref, om_ref, gl_refs, bg_ref, ws_ref, wa_ref, wm_ref):
    outs = (_dot(os_ref[...], ws_ref[...]), _dot(oa_ref[...], wa_ref[...]), _dot(om_ref[...], wm_ref[...]))
    gates = tuple(_sigmoid(jnp.concatenate([gl_refs[2 * k][...], gl_refs[2 * k + 1][...]], axis=1)
                           + bg_ref[:, D_MODEL * k:D_MODEL * (k + 1)]) for k in range(3))
    return outs, gates


def _merge_specs(tm):
    row = lambda i: (i, 0)
    first = _OFF["gates"] // GATE_BLOCK
    gate = [pl.BlockSpec((tm, GATE_BLOCK), (lambda i, k=k: (i, first + k))) for k in range(N_GATES // GATE_BLOCK)]
    return ([pl.BlockSpec((tm, D_SSM), row), pl.BlockSpec((tm, D_ATTN), row), pl.BlockSpec((tm, D_MEM), row)] + gate
            + [_full((1, N_GATES)), _full((D_SSM, D_MODEL)), _full((D_ATTN, D_MODEL)), _full((D_MEM, D_MODEL)),
               _full((D_MODEL, D_MODEL))])


def _merge_fwd(x, o_ssm, o_attn, o_mem, proj, bg, ws, wa, wm, wo, tm, name):
    T = x.shape[0]

    def body(os_ref, oa_ref, om_ref, g0, g1, g2, g3, g4, g5, bg_ref, ws_ref, wa_ref, wm_ref, wo_ref, x_ref,
             xo_ref, mg_ref):
        outs, gates = _branches_and_gates(os_ref, oa_ref, om_ref, (g0, g1, g2, g3, g4, g5), bg_ref, ws_ref, wa_ref,
                                          wm_ref)
        merged = (gates[0] * outs[0] + gates[1] * outs[1] + gates[2] * outs[2]).astype(BF16)
        mg_ref[...] = merged
        xo_ref[...] = x_ref[...] + _dot(merged, wo_ref[...])

    row = lambda i: (i, 0)
    return pl.pallas_call(
        body, name=name, grid=(T // tm,),
        in_specs=_merge_specs(tm) + [pl.BlockSpec((tm, D_MODEL), row)],
        out_specs=[pl.BlockSpec((tm, D_MODEL), row), pl.BlockSpec((tm, D_MODEL), row)],
        out_shape=[jax.ShapeDtypeStruct((T, D_MODEL), F32), jax.ShapeDtypeStruct((T, D_MODEL), BF16)],
        compiler_params=_params(("parallel",)),
    )(o_ssm, o_attn, o_mem, *([proj] * (N_GATES // GATE_BLOCK)), bg, ws, wa, wm, wo, x)


def _merge_bwd(dx, o_ssm, o_attn, o_mem, proj, bg, ws, wa, wm, wo, tm, name, job=None):
    T = dx.shape[0]

    n = T // tm

    def body(os_ref, oa_ref, om_ref, g0, g1, g2, g3, g4, g5, bg_ref, ws_ref, wa_ref, wm_ref, wo_ref, dx_ref,
             dproj_ref, db_ref, dos_ref, doa_ref, dom_ref, dbg_ref, dgl_buf, dgl_sems):
        i = pl.program_id(0)
        slot = i % 2

        def to_dproj(s, row0):
            return pltpu.make_async_copy(dgl_buf.at[s], dproj_ref.at[pl.ds(row0, tm), pl.ds(_OFF["gates"], N_GATES)],
                                         dgl_sems.at[s])

        @pl.when(i == 0)
        def _():
            dbg_ref[...] = jnp.zeros_like(dbg_ref)

        @pl.when(i >= 2)
        def _():
            to_dproj(slot, 0).wait()

        outs, gates = _branches_and_gates(os_ref, oa_ref, om_ref, (g0, g1, g2, g3, g4, g5), bg_ref, ws_ref, wa_ref,
                                          wm_ref)
        dm = _dot(dx_ref[...].astype(BF16), wo_ref[...], NT)
        w_refs = (ws_ref, wa_ref, wm_ref)
        do_refs = (dos_ref, doa_ref, dom_ref)
        for k in range(3):
            cols = slice(D_MODEL * k, D_MODEL * (k + 1))
            dgl = dm * outs[k] * (gates[k] * (1.0 - gates[k]))
            dgl_buf[slot, :, cols] = dgl.astype(BF16)
            dbg_ref[:, cols] += jnp.sum(dgl, axis=0, keepdims=True)
            dbk = (dm * gates[k]).astype(BF16)
            db_ref[:, cols] = dbk
            do_refs[k][...] = _dot(dbk, w_refs[k][...], NT)
        to_dproj(slot, pl.multiple_of(i * tm, tm)).start()

        @pl.when(i == n - 1)
        def _():
            for s in range(min(2, n)):
                to_dproj(s, 0).wait()

    row = lambda i: (i, 0)
    return _pc(
        body, job, name=name, grid=(n,),
        in_specs=_merge_specs(tm) + [pl.BlockSpec((tm, D_MODEL), row)],
        out_specs=[pl.BlockSpec(memory_space=pl.ANY), pl.BlockSpec((tm, N_GATES), row), pl.BlockSpec((tm, D_SSM), row),
                   pl.BlockSpec((tm, D_ATTN), row), pl.BlockSpec((tm, D_MEM), row), _full((1, N_GATES))],
        out_shape=[jax.ShapeDtypeStruct((T, D_IN), BF16), jax.ShapeDtypeStruct((T, N_GATES), BF16),
                   jax.ShapeDtypeStruct((T, D_SSM), F32), jax.ShapeDtypeStruct((T, D_ATTN), F32),
                   jax.ShapeDtypeStruct((T, D_MEM), F32), jax.ShapeDtypeStruct((1, N_GATES), F32)],
        scratch_shapes=[pltpu.VMEM((2, tm, N_GATES), BF16), pltpu.SemaphoreType.DMA((2,))], sem=("arbitrary",),
        operands=(o_ssm, o_attn, o_mem, *([proj] * (N_GATES // GATE_BLOCK)), bg, ws, wa, wm, wo, dx))


def _loss_head(x, g, target, tm, name):
    T, D = x.shape

    def body(x_ref, g_ref, t_ref, loss_ref, dx_ref, dg_ref):
        @pl.when(pl.program_id(0) == 0)
        def _():
            loss_ref[...] = jnp.zeros_like(loss_ref)
            dg_ref[...] = jnp.zeros_like(dg_ref)

        xv = x_ref[...]
        r = lax.rsqrt(jnp.mean(xv * xv, axis=-1, keepdims=True) + EPS)
        xr = xv * r
        err = xr * g_ref[...] - t_ref[...]
        loss_ref[...] += 0.5 * jnp.sum(jnp.mean(err * err, axis=-1, keepdims=True), axis=0, keepdims=True)
        dy = err * (1.0 / D)
        dg_ref[...] += jnp.sum(dy * xr, axis=0, keepdims=True)
        wv = dy * g_ref[...]
        dx_ref[...] = r * (wv - xr * jnp.mean(wv * xr, axis=-1, keepdims=True))

    row = lambda i: (i, 0)
    return pl.pallas_call(
        body, name=name, grid=(T // tm,),
        in_specs=[pl.BlockSpec((tm, D), row), _full((1, D)), pl.BlockSpec((tm, D), row)],
        out_specs=[_full((1, 128)), pl.BlockSpec((tm, D), row), _full((1, D))],
        out_shape=[jax.ShapeDtypeStruct((1, 128), F32), jax.ShapeDtypeStruct((T, D), F32),
                   jax.ShapeDtypeStruct((1, D), F32)],
        compiler_params=_params(("arbitrary",)),
    )(x, g, target)


def _adamw(parts, w, m, v, tr, name):
    L, R, C = w.shape

    def body(p_ref, w_ref, m_ref, v_ref, g_ref, d_ref, mo_ref, vo_ref):
        g = p_ref[0].astype(F32)
        for s in range(1, N_DEV):
            g = g + p_ref[s].astype(F32)
        mn = ADAM_B1 * m_ref[...] + (1.0 - ADAM_B1) * g
        vn = ADAM_B2 * v_ref[...] + (1.0 - ADAM_B2) * (g * g)
        m_hat = mn / (1.0 - ADAM_B1 ** ADAM_STEP)
        v_hat = vn / (1.0 - ADAM_B2 ** ADAM_STEP)
        g_ref[...] = g
        d_ref[...] = -ADAM_LR * (m_hat / (jnp.sqrt(v_hat) + ADAM_EPS) + ADAM_WD * w_ref[...])
        mo_ref[...] = mn
        vo_ref[...] = vn

    one = pl.BlockSpec((None, tr, C), lambda l, i: (l, i, 0))
    return pl.pallas_call(
        body, name=name, grid=(L, R // tr),
        in_specs=[pl.BlockSpec((N_DEV, None, tr, C), lambda l, i: (0, l, i, 0)), one, one, one],
        out_specs=[one] * 4,
        out_shape=[jax.ShapeDtypeStruct((L, R, C), F32)] * 4,
        compiler_params=_params(("parallel", "parallel")),
    )(parts, w, m, v)


_SHARDED = (("w_in", (1088, 1024), 1), ("w_glu", (96, 768), 0), ("w_mem_kv", (128, 1024), 0),
            ("w_br_ssm", (768, 128), 1), ("w_br_attn", (768, 128), 1), ("w_br_mem", (512, 128), 1),
            ("w_out", (128, 1024), 0))
_W_IN = 0
_SMALL = tuple(range(1, len(_SHARDED)))


class _Job(NamedTuple):
    ins: list
    out_shape: list
    aliases: dict
    pairs: Callable
    n: int


def _peers():
    x, y, c = lax.axis_index("x"), lax.axis_index("y"), lax.axis_index("c")
    me = 4 * x + 2 * y + c
    out = []
    for k in range(1, N_DEV):
        px = 1 - x if k & 4 else x
        py = 1 - y if k & 2 else y
        pc = 1 - c if k & 1 else c
        out.append(((px, py, pc), 4 * px + 2 * py + pc))
    return me, out


def _copies(pairs, send_sems, recv_sems, local_sems, arrivals):
    me, peers = _peers()
    local = [pltpu.make_async_copy(src(me), dst(me), local_sems.at[j]) for j, (src, dst) in enumerate(pairs)]
    sends, recvs = [], []
    for k, (peer, lin) in enumerate(peers):
        for j, (src, dst) in enumerate(pairs):
            for to, out in ((dst(me), sends), (dst(lin), recvs)):
                if out is sends or arrivals:
                    out.append(pltpu.make_async_remote_copy(
                        src_ref=src(lin), dst_ref=to, send_sem=send_sems.at[j, k], recv_sem=recv_sems.at[j, k],
                        device_id=peer, device_id_type=pl.DeviceIdType.MESH))
    return local, sends, recvs


def _start_copies(pairs, *sems):
    local, sends, _ = _copies(pairs, *sems, arrivals=False)
    for cp in local + sends:
        cp.start()


def _wait_copies(pairs, *sems):
    local, sends, recvs = _copies(pairs, *sems, arrivals=True)
    for cp in recvs:
        cp.wait_recv()
    for cp in sends:
        cp.wait_send()
    for cp in local:
        cp.wait()


def _job_scratch(job):
    return [pltpu.SemaphoreType.DMA((job.n, N_DEV - 1)), pltpu.SemaphoreType.DMA((job.n, N_DEV - 1)),
            pltpu.SemaphoreType.DMA((job.n,))]


def _pc(body, job, *, name, grid, in_specs, out_specs, out_shape, scratch_shapes, sem, operands, aliases=None):
    aliases = aliases or {}
    if job is None:
        return pl.pallas_call(body, name=name, grid=grid, in_specs=in_specs, out_specs=out_specs, out_shape=out_shape,
                              scratch_shapes=scratch_shapes, input_output_aliases=aliases,
                              compiler_params=_params(sem))(*operands)
    a = len(in_specs)
    b = a + len(job.ins)
    c = b + len(out_shape)
    d = c + len(job.out_shape)
    e = d + len(scratch_shapes)

    def carried(*refs):
        pairs = job.pairs(refs[a:b], refs[c:d])
        ids = [pl.program_id(k) for k in range(len(grid))]
        first = functools.reduce(jnp.logical_and, [i == 0 for i in ids])
        last = functools.reduce(jnp.logical_and, [i == n - 1 for i, n in zip(ids, grid)])

        @pl.when(first)
        def _():
            _start_copies(pairs, *refs[e:])

        body(*refs[:a], *refs[b:c], *refs[d:e])

        @pl.when(last)
        def _():
            _wait_copies(pairs, *refs[e:])

    hbm = pl.BlockSpec(memory_space=pl.ANY)
    outs = pl.pallas_call(
        carried, name=name, grid=grid,
        in_specs=list(in_specs) + [hbm] * len(job.ins), out_specs=list(out_specs) + [hbm] * len(job.out_shape),
        out_shape=list(out_shape) + list(job.out_shape),
        input_output_aliases={**aliases, **{a + i: len(out_shape) + o for i, o in job.aliases.items()}},
        scratch_shapes=list(scratch_shapes) + _job_scratch(job),
        compiler_params=_params(("arbitrary",) * len(grid)),
    )(*operands, *job.ins)
    return outs[:len(out_shape)], outs[len(out_shape):]


def _gather_via_sibling(x, take, place, out_shape, name, landing=None):
    def body(*refs):
        x_ref, o_ref = refs[0], refs[-4]
        send_sems, recv_sems, local_sem = refs[-3:]
        x, y, c = lax.axis_index("x"), lax.axis_index("y"), lax.axis_index("c")
        me, sibling = (x, y, c), (x, y, 1 - c)
        chips = [(1 - x, y), (x, 1 - y), (1 - x, 1 - y)]
        src = take(x_ref)

        def slot(px, py, pc):
            return place(o_ref, 4 * px + 2 * py + pc)

        def copy(k, block, to, first_hand):
            return pltpu.make_async_remote_copy(
                src_ref=src if first_hand else slot(*block), dst_ref=slot(*block), send_sem=send_sems.at[k],
                recv_sem=recv_sems.at[k], device_id=to, device_id_type=pl.DeviceIdType.MESH)

        mine = pltpu.make_async_copy(src, slot(*me), local_sem)
        mine.start()
        first = [copy(0, me, sibling, True)] + [copy(1 + j, me, (*chip, c), True) for j, chip in enumerate(chips)]
        for cp in first:
            cp.start()
        passed = []
        for j, chip in enumerate(chips):
            copy(1 + j, (*chip, c), me, True).wait_recv()
            passed.append(copy(4 + j, (*chip, c), sibling, False))
            passed[-1].start()
        copy(0, sibling, me, True).wait_recv()
        for j, chip in enumerate(chips):
            copy(4 + j, (*chip, 1 - c), me, False).wait_recv()
        for cp in first + passed:
            cp.wait_send()
        mine.wait()

    hbm = pl.BlockSpec(memory_space=pl.ANY)
    ins = [x] if landing is None else [x, landing]
    return pl.pallas_call(
        body, name=name, in_specs=[hbm] * len(ins), out_specs=hbm, out_shape=out_shape,
        input_output_aliases={} if landing is None else {1: 0},
        scratch_shapes=[pltpu.SemaphoreType.DMA((N_DEV - 1,)), pltpu.SemaphoreType.DMA((N_DEV - 1,)),
                        pltpu.SemaphoreType.DMA],
    )(*ins)


def _lane_window(ref, who):
    return ref.at[:, pl.ds(pl.multiple_of(who * LANES, LANES), LANES)]


def _gather_job(shards, items):
    out_shape = []
    for i, _ in items:
        _, s, axis = _SHARDED[i]
        whole = i != _W_IN and axis == 1
        out_shape.append(jax.ShapeDtypeStruct((s[0], N_DEV * s[1]) if whole else (N_DEV,) + s, BF16))

    def pairs(in_refs, out_refs):
        out = []
        for (i, l), src, dst in zip(items, in_refs, out_refs):
            if i != _W_IN and _SHARDED[i][2] == 1:
                out.append((lambda who, src=src, l=l: src.at[l], lambda who, dst=dst: _lane_window(dst, who)))
            else:
                out.append((lambda who, src=src, l=l: src.at[l], lambda who, dst=dst: dst.at[who]))
        return out

    return _Job([shards[i] for i, _ in items], out_shape, {}, pairs, len(items))


def _landed_weights(items, landed):
    out = {}
    for (i, _), a in zip(items, landed):
        n, s, axis = _SHARDED[i]
        if i == _W_IN:
            out[n] = a.reshape(D_IN, D_MODEL)
        elif axis == 0:
            out[n] = a.reshape(N_DEV * s[0], s[1])
        else:
            out[n] = a
    return out


def _scatter_job(grads, items, layer, parts=None):
    ng = len(grads)
    out_shape = [jax.ShapeDtypeStruct((N_DEV, DEPTH) + _SHARDED[i][1], BF16) for i in items]

    def pairs(in_refs, out_refs):
        out = []
        for i, src, dst in zip(items, in_refs[:ng], out_refs):
            _, s, axis = _SHARDED[i]
            if i == _W_IN:
                take = lambda who, src=src: src.at[who]
            elif axis == 0:
                take = lambda who, src=src, s=s: src.at[pl.ds(pl.multiple_of(who * s[0], 16), s[0])]
            else:
                take = lambda who, src=src: _lane_window(src, who)
            out.append((take, lambda who, dst=dst: dst.at[who, layer]))
        return out

    aliases = {} if parts is None else {ng + j: j for j in range(len(items))}
    return _Job(list(grads) + ([] if parts is None else list(parts)), out_shape, aliases, pairs, len(items))


def _rows_job(src, row0, landing=None):
    n = src.shape[0]
    pairs = lambda in_refs, out_refs: [(lambda who: in_refs[0], lambda who: out_refs[0].at[who, pl.ds(row0, n)])]
    return _Job([src] + ([] if landing is None else [landing]), [jax.ShapeDtypeStruct((N_DEV, _REP_ROWS, LANES), F32)],
                {} if landing is None else {1: 0}, pairs, 1)


_REPLICATED = (("norm_g", (2, 1024)), ("mem_norm_g", (2, 1024)), ("b_gate", (2, 3072)),
               ("ssm_lambda_re", (2, 48, 64)), ("ssm_lambda_im", (2, 48, 64)), ("ssm_log_dt", (2, 48)),
               ("ssm_b_re", (2, 48, 64, 16)), ("ssm_b_im", (2, 48, 64, 16)), ("ssm_c_re", (2, 48, 16, 64)),
               ("ssm_c_im", (2, 48, 16, 64)), ("ssm_d", (2, 768)), ("b_glu", (2, 768)), ("rel_bias", (32, 12)),
               ("final_norm_g", (1024,)))
_PER_LAYER = tuple((n, s[1:]) for n, s in _REPLICATED if s[0] == DEPTH and len(s) > 1)
_SHARED = tuple((n, s) for n, s in _REPLICATED if (n, s[1:]) not in _PER_LAYER)
_REP_HALF_ROWS = 1664
_REP_ROWS = 2 * _REP_HALF_ROWS
assert sum(int(np.prod(s)) for _, s in _PER_LAYER + _SHARED) <= _REP_HALF_ROWS * LANES


def _pack_half(tree, layer, shared):
    flat = [tree[n][layer].reshape(-1) for n, _ in _PER_LAYER]
    if shared:
        flat += [tree[n].reshape(-1) for n, _ in _SHARED]
    flat = jnp.concatenate(flat)
    return jnp.pad(flat, (0, _REP_HALF_ROWS * LANES - flat.shape[0])).reshape(_REP_HALF_ROWS, LANES)


def _pack_replicated(tree):
    return jnp.concatenate([_pack_half(tree, 1, False), _pack_half(tree, 0, True)])[None]


def _unpack_replicated(packed):
    halves = packed.reshape(2, -1)
    out, r = {}, 0
    for n, s in _PER_LAYER:
        size = int(np.prod(s))
        out[n] = jnp.stack([halves[1, r:r + size].reshape(s), halves[0, r:r + size].reshape(s)])
        r += size
    for n, s in _SHARED:
        size = int(np.prod(s))
        out[n] = halves[1, r:r + size].reshape(s)
        r += size
    return out


def _discretize(lam_re, lam_im, log_dt, b_re, b_im):
    dt = jnp.exp(log_dt)[:, None]
    mag = jnp.exp(lam_re * dt)
    abar_re, abar_im = mag * jnp.cos(lam_im * dt), mag * jnp.sin(lam_im * dt)
    den = lam_re * lam_re + lam_im * lam_im
    nr, ni = abar_re - 1.0, abar_im
    f_re = (nr * lam_re + ni * lam_im) / den
    f_im = (ni * lam_re - nr * lam_im) / den
    bbar_re = f_re[..., None] * b_re - f_im[..., None] * b_im
    bbar_im = f_re[..., None] * b_im + f_im[..., None] * b_re
    return abar_re, abar_im, bbar_re, bbar_im


def _block_diag(a):
    _, R, C = a.shape
    a = a.reshape(SSM_BLOCKS, 8, R, C)
    eye = jnp.eye(8, dtype=a.dtype)
    return (a[:, :, :, None, :] * eye[None, :, None, :, None]).reshape(SSM_BLOCKS, 8 * R, 8 * C)


def _diag_blocks(a, R, C):
    a = a.reshape(SSM_BLOCKS, 8, R, 8, C)
    eye = jnp.eye(8, dtype=a.dtype)
    return jnp.sum(a * eye[None, :, None, :, None], axis=3).reshape(SSM_GROUPS, R, C)


def _carried(result, job):
    return (result, None) if job is None else result


def _layer_fwd(x, mem, W, P, bias, layer, jobs):
    tag = f"l{layer}"
    abar_re, abar_im, bbar_re, bbar_im = _discretize(P["ssm_lambda_re"][layer], P["ssm_lambda_im"][layer],
                                                     P["ssm_log_dt"][layer], P["ssm_b_re"][layer], P["ssm_b_im"][layer])
    c_re, c_im = P["ssm_c_re"][layer], P["ssm_c_im"][layer]
    ssm = dict(
        are=abar_re.reshape(1, N_STATE), aim=abar_im.reshape(1, N_STATE),
        bre=_block_diag(bbar_re.transpose(0, 2, 1)).astype(BF16), bim=_block_diag(bbar_im.transpose(0, 2, 1)).astype(BF16),
        cre=_block_diag(c_re.transpose(0, 2, 1)).astype(BF16), cimn=_block_diag(-c_im.transpose(0, 2, 1)).astype(BF16),
        d=P["ssm_d"][layer].reshape(1, D_SSM))
    bglu = P["b_glu"][layer].reshape(1, D_SSM)
    bgate = P["b_gate"][layer].reshape(1, N_GATES)
    g = P["norm_g"][layer].reshape(1, D_MODEL)
    gm = P["mem_norm_g"][layer].reshape(1, D_MODEL)
    delivered = {}

    def carry(stage):
        return jobs[stage][0] if stage in jobs else None

    def deliver(stage, landed):
        if landed is not None:
            delivered[stage] = _landed_weights(jobs[stage][1], landed)

    T = x.shape[0]
    (proj, h), landed = _carried(_norm_proj(x, g, W["w_in"], min(T, 1024), 2176, f"{tag}_proj", job=carry("proj"),
                                            w_turned=True), carry("proj"))
    deliver("proj", landed)
    W = {**W, **delivered.get("proj", {})}
    (xr, xi, y, o_ssm), landed = _carried(
        _ssm_fwd(proj, ssm["bre"], ssm["bim"], ssm["cre"], ssm["cimn"], ssm["are"], ssm["aim"], ssm["d"], W["w_glu"],
                 bglu, 512, f"{tag}_ssm", job=carry("ssm")), carry("ssm"))
    deliver("ssm", landed)
    os, lses = [], []
    for grp in range(3):
        stage = f"attn{grp}"
        (o_g, lse_g), landed = _carried(_attn_fwd(proj, bias[grp], grp, f"{tag}_{stage}", job=carry(stage)), carry(stage))
        deliver(stage, landed)
        os.append(o_g)
        lses.append(lse_g)
    o_attn = _attn_mix(os, lses, proj, min(T, ROW_TILE), f"{tag}_attn_mix")
    kvb, hm = _norm_proj(mem, gm, W["w_mem_kv"], mem.shape[0], 1024, f"{tag}_mem_kv", out_dtype=BF16)
    o_mem = _mem_fwd(proj, kvb, min(T, ROW_TILE), f"{tag}_mem")
    x_out, merged = _merge_fwd(x, o_ssm, o_attn, o_mem, proj, bgate, W["w_br_ssm"], W["w_br_attn"], W["w_br_mem"],
                               W["w_out"], 512, f"{tag}_merge")
    res = dict(x=x, mem=mem, proj=proj, h=h, xr=xr, xi=xi, y=y, o_ssm=o_ssm, os=os, lses=lses,
               o_attn=o_attn, kvb=kvb, hm=hm, o_mem=o_mem, merged=merged, ssm=ssm, bglu=bglu,
               bgate=bgate, g=g, gm=gm, W=W)
    return x_out, res, delivered


def _layer_bwd(dx, res, P, bias, layer, jobs):
    tag = f"l{layer}b"
    proj, ssm, W = res["proj"], res["ssm"], res["W"]
    T = dx.shape[0]
    landed = {}

    def run(stage, fn, job):
        out, landed[stage] = _carried(fn(job), job)
        if job is None:
            del landed[stage]
        return out

    dproj, dbr, do_ssm, do_attn, do_mem, dbg = run(
        "merge", lambda job: _merge_bwd(dx, res["o_ssm"], res["o_attn"], res["o_mem"], proj, res["bgate"], W["w_br_ssm"],
                                        W["w_br_attn"], W["w_br_mem"], W["w_out"], 512, f"{tag}_merge", job=job),
        jobs.get("merge"))
    gw = {}
    tk = min(T, 1024)
    gw["w_out"] = _mm_tn(res["merged"], dx, 1024, 1024, tk, f"{tag}_dw_out")
    gw["w_br_ssm"] = _mm_tn(res["o_ssm"], dbr, 768, 1024, tk, f"{tag}_dw_br_ssm", b_col=0, n=1024)
    gw["w_br_attn"] = _mm_tn(res["o_attn"], dbr, 768, 1024, tk, f"{tag}_dw_br_attn", b_col=1024, n=1024)
    gw["w_br_mem"] = _mm_tn(res["o_mem"], dbr, 512, 1024, tk, f"{tag}_dw_br_mem", b_col=2048, n=1024)

    rows = min(T, ROW_TILE)
    dqm, dzm, dkv = _mem_bwd(do_mem, proj, res["kvb"], rows, f"{tag}_mem")
    M = dkv.shape[0]
    gw["w_mem_kv"] = _mm_tn(res["hm"], dkv, 1024, 1024, M, f"{tag}_dw_mem_kv")
    _, dgm = _proj_bwd(dkv.astype(BF16), W["w_mem_kv"], res["mem"], res["gm"], jnp.zeros_like(res["mem"]), M, 1024,
                       f"{tag}_mem_norm")

    do_g, corr, dproj = _attn_mix_bwd(do_attn, res["os"], res["lses"], proj, dproj, rows, f"{tag}_attn_mix")
    dbs = []
    for grp in range(3):
        dproj, db_g = _attn_bwd(proj, do_g, corr, res["lses"][grp], bias[grp], dproj, grp, f"{tag}_attn{grp}")
        dbs.append(db_g)
    dbias = jnp.stack(dbs)

    dy, dproj, gelu_b, dt_b, dbglu = _glu_bwd(do_ssm, res["y"], proj, W["w_glu"], res["bglu"], dproj, rows, f"{tag}_glu")
    gw["w_glu"] = _mm_tn(gelu_b, dt_b, 768, 768, tk, f"{tag}_dw_glu")
    dproj, dbre, dbim, dcre, dcim, dare, daim, dd = run(
        "ssm", lambda job: _ssm_bwd(dy, proj, res["xr"], res["xi"], ssm["bre"], ssm["bim"], ssm["cre"], ssm["cimn"],
                                    ssm["are"], ssm["aim"], ssm["d"], dproj, 256, f"{tag}_ssm", job=job),
        jobs.get("ssm"))
    _, disc_vjp = jax.vjp(_discretize, P["ssm_lambda_re"][layer], P["ssm_lambda_im"][layer], P["ssm_log_dt"][layer],
                          P["ssm_b_re"][layer], P["ssm_b_im"][layer])
    d_lre, d_lim, d_ldt, d_bre, d_bim = disc_vjp((dare.reshape(SSM_GROUPS, SSM_STATE), daim.reshape(SSM_GROUPS, SSM_STATE),
                                                  _diag_blocks(dbre, SSM_STATE, SSM_GROUP),
                                                  _diag_blocks(dbim, SSM_STATE, SSM_GROUP)))

    small = [gw[_SHARDED[i][0]] for i in _SMALL]
    for seg, piece in (("q_mem", dqm), ("z_mem", dzm)):
        dproj = lax.dynamic_update_slice(dproj, piece, (0, _OFF[seg]))
    dw_in = run("dw_in", lambda job: _mm_tn(dproj, res["h"], 2176, 1024, min(T, 1024), f"{tag}_dw_in", job=job),
                jobs["dw_in"](small) if "dw_in" in jobs else None)
    dw_in = dw_in.reshape((N_DEV,) + _SHARDED[_W_IN][1])
    dx_in, dg = run("proj", lambda job: _proj_bwd(dproj, W["w_in"], res["x"], res["g"], dx, min(T, 1024), 2176,
                                                  f"{tag}_proj", job=job, w_turned=True),
                    jobs["proj"](small, dw_in, landed) if "proj" in jobs else None)

    gp = dict(norm_g=dg[0], mem_norm_g=dgm[0], b_gate=dbg[0], ssm_lambda_re=d_lre, ssm_lambda_im=d_lim,
              ssm_log_dt=d_ldt, ssm_b_re=d_bre, ssm_b_im=d_bim,
              ssm_c_re=_diag_blocks(dcre, SSM_GROUP, SSM_STATE), ssm_c_im=_diag_blocks(dcim, SSM_GROUP, SSM_STATE),
              ssm_d=dd[0], b_glu=dbglu[0])
    return dx_in, dw_in, gp, dbias, landed


def _train_step(x, mem, target, shards, P):
    rest0 = [(i, 0) for i in _SMALL]
    thirds1 = [[(i, 1) for i in _SMALL[k::3]] for k in range(3)]
    first = [(_W_IN, 0)]
    w_in0 = _gather_via_sibling(shards[_W_IN], lambda ref: ref.at[0], lambda ref, s: ref.at[s],
                                jax.ShapeDtypeStruct((N_DEV,) + _SHARDED[_W_IN][1], BF16), "gather_w_in0")
    W0 = _landed_weights(first, [w_in0])
    buckets = _bucket_tables()
    bias = _bias_tables(P["rel_bias"], buckets, "bias_tables")
    jobs0 = {"proj": (_gather_job(shards, rest0), rest0), "ssm": (_gather_job(shards, [(_W_IN, 1)]), [(_W_IN, 1)]),
             **{f"attn{k}": (_gather_job(shards, items), items) for k, items in enumerate(thirds1)}}
    x, res0, delivered = _layer_fwd(x, mem, W0, P, bias, 0, jobs0)
    W1 = {**delivered["ssm"], **delivered["attn0"], **delivered["attn1"], **delivered["attn2"]}
    x, res1, _ = _layer_fwd(x, mem, W1, P, bias, 1, {})
    loss, dx, dgf = _loss_head(x, P["final_norm_g"].reshape(1, D_MODEL), target, min(x.shape[0], ROW_TILE),
                                "loss_head")

    dx, _, gp1, dbias1, landed1 = _layer_bwd(
        dx, res1, P, bias, 1,
        {"dw_in": lambda small: _scatter_job(small, _SMALL, 1),
         "proj": lambda small, dw_in, landed: _scatter_job([dw_in], [_W_IN], 1)})
    rep1 = _pack_half({n: a[None] for n, a in gp1.items()}, 0, False)
    dx, _, gp0, dbias0, landed0 = _layer_bwd(
        dx, res0, P, bias, 0,
        {"merge": _rows_job(rep1, 0),
         "dw_in": lambda small: _scatter_job(small, _SMALL, 0, parts=landed1["dw_in"]),
         "proj": lambda small, dw_in, landed: _scatter_job([dw_in], [_W_IN], 0, parts=landed1["proj"])})
    d_rel = _bias_grad(dbias0, dbias1, buckets, "bias_grad")
    gp0 = {n: a[None] for n, a in gp0.items()}
    gp0["rel_bias"] = jnp.sum(d_rel, axis=-1).transpose(2, 0, 1).reshape(NUM_BUCKETS, 12)
    gp0["final_norm_g"] = dgf[0]
    rep0 = _pack_half(gp0, 0, True)
    rparts = _gather_via_sibling(rep0, lambda ref: ref, lambda ref, s: ref.at[s, pl.ds(_REP_HALF_ROWS, _REP_HALF_ROWS)],
                                 jax.ShapeDtypeStruct((N_DEV, _REP_ROWS, LANES), F32), "gather_small_grads0",
                                 landing=landed0["merge"][0])
    return loss[0, 0], dx, list(landed0["proj"]) + list(landed0["dw_in"]), rparts


_WEIGHTS = ["norm_g", "mem_norm_g", "w_in", "b_gate", "ssm_lambda_re", "ssm_lambda_im", "ssm_log_dt", "ssm_b_re",
            "ssm_b_im", "ssm_c_re", "ssm_c_im", "ssm_d", "w_glu", "b_glu", "w_mem_kv", "w_br_ssm", "w_br_attn",
            "w_br_mem", "w_out", "rel_bias", "final_norm_g"]
_ADAM_ROWS = {"w_in": 136,"w_glu": 96, "w_mem_kv": 128, "w_br_ssm": 768, "w_br_attn": 768, "w_br_mem": 512,
              "w_out": 128}


def kernel(x, mem, norm_g, mem_norm_g, w_in, b_gate, ssm_lambda_re, ssm_lambda_im, ssm_log_dt, ssm_b_re, ssm_b_im, ssm_c_re, ssm_c_im, ssm_d, w_glu, b_glu, w_mem_kv, w_br_ssm, w_br_attn, w_br_mem, w_out, rel_bias, final_norm_g, loss_target, m_norm_g, m_mem_norm_g, m_w_in, m_b_gate, m_ssm_lambda_re, m_ssm_lambda_im, m_ssm_log_dt, m_ssm_b_re, m_ssm_b_im, m_ssm_c_re, m_ssm_c_im, m_ssm_d, m_w_glu, m_b_glu, m_w_mem_kv, m_w_br_ssm, m_w_br_attn, m_w_br_mem, m_w_out, m_rel_bias, m_final_norm_g, v_norm_g, v_mem_norm_g, v_w_in, v_b_gate, v_ssm_lambda_re, v_ssm_lambda_im, v_ssm_log_dt, v_ssm_b_re, v_ssm_b_im, v_ssm_c_re, v_ssm_c_im, v_ssm_d, v_w_glu, v_b_glu, v_w_mem_kv, v_w_br_ssm, v_w_br_attn, v_w_br_mem, v_w_out, v_rel_bias, v_final_norm_g):
    given = dict(locals())
    w = {n: given[n] for n in _WEIGHTS}
    m = {n: given["m_" + n] for n in _WEIGHTS}
    v = {n: given["v_" + n] for n in _WEIGHTS}

    turned = lambda n, a: a.swapaxes(1, 2) if n == "w_in" else a
    shards = [turned(n, w[n]).astype(BF16) for n, _, _ in _SHARDED]
    loss, dx, parts, rparts = _train_step(x[0], mem[0], loss_target[0], shards, w)
    loss = lax.psum(loss, ("x", "y", "c"))

    new = {}
    for (n, _, _), p in zip(_SHARDED, parts):
        new[n] = [turned(n, a) for a in _adamw(p, turned(n, w[n]), turned(n, m[n]), turned(n, v[n]), _ADAM_ROWS[n],
                                               f"adamw_{n}")]
    rp = [_unpack_replicated(a) for a in _adamw(rparts[:, None], _pack_replicated(w), _pack_replicated(m),
                                                _pack_replicated(v), _REP_ROWS // 4, "adamw_replicated")]
    for n, _ in _REPLICATED:
        new[n] = [rp[kind][n] for kind in range(4)]
    outs = [loss, dx[None]]
    for kind in range(4):
        outs.extend(new[n][kind] for n in _WEIGHTS)
    return tuple(outs)
```
